```python
import math
import jax, jax.numpy as jnp
from jax import lax
import numpy as np

D_MODEL = 1024
BATCH = 4
SEQ = 8192
DEPTH = 4

N_EVEN = (DEPTH + 1) // 2
N_ODD = DEPTH // 2
MIX_WIDTH = D_MODEL
A_WIDTH = MIX_WIDTH // 2
A_HEAD_DIM = 128
A_HEADS = A_WIDTH // A_HEAD_DIM
B_WIDTH = MIX_WIDTH - A_WIDTH
S5_GROUP = 16
S5_GROUPS = B_WIDTH // S5_GROUP
S5_STATE = 64
C_WIDTH = MIX_WIDTH // 2
C_HEAD_DIM = 64
C_HEADS = C_WIDTH // (2 * C_HEAD_DIM)
C_V_DIM = 2 * C_HEAD_DIM
D_WIDTH = MIX_WIDTH - C_WIDTH
D_HEAD_DIM = 128
D_HEADS = D_WIDTH // D_HEAD_DIM
CONV_WIDTH = 5
N_EXPERTS = 16
EXPERT_FF = 2 * D_MODEL
EC_CAPACITY_FACTOR = 2
REL_BUCKETS = 32
REL_MAX_DIST = 128
CHUNK = 64
Q_BLOCK = 128
EPS = 1e-6
EVEN_IN = 5 * A_WIDTH + B_WIDTH
ODD_IN = 3 * C_WIDTH + 3 * D_WIDTH + 4 * D_HEADS + D_WIDTH

kernel_name = 'hybrid_hgrn2_s5_diffattn_gdn_ecmoe_encoder'


def rmsnorm(x, g):
    xf = x.astype(jnp.float32)
    y = xf * lax.rsqrt(jnp.mean(xf * xf, axis=-1, keepdims=True) + EPS)
    return (y * g.astype(jnp.float32)).astype(x.dtype)


def l2norm(x):
    return x * lax.rsqrt(jnp.sum(x * x, axis=-1, keepdims=True) + EPS)


def _heads(t, n_heads):
    b, l, w = t.shape
    return jnp.transpose(t.reshape(b, l, n_heads, w // n_heads), (0, 2, 1, 3))


def _flip(t):
    return jnp.flip(t, axis=2)


def _to_chunks(t):
    b, h, l = t.shape[:3]
    t = t.reshape((b, h, l // CHUNK, CHUNK) + t.shape[3:])
    return jnp.moveaxis(t, 2, 0)


def _from_chunks(t):
    t = jnp.moveaxis(t, 0, 2)
    b, h, nc, c = t.shape[:4]
    return t.reshape((b, h, nc * c) + t.shape[4:])


def hgrn2_scan(q, k, v, log_f):
    tri = jnp.tril(jnp.ones((CHUNK, CHUNK), bool))[:, :, None]
    qc, kc, vc, lc = (_to_chunks(t) for t in (q, k, v, log_f))
    bc = jnp.cumsum(lc, axis=-2)

    def step(S, inp):
        q_, k_, v_, b_ = inp
        diff = b_[..., :, None, :] - b_[..., None, :, :]
        decay = jnp.where(tri, jnp.exp(jnp.where(tri, diff, 0.0)), 0.0)
        attn = jnp.einsum('bhtd,bhsd,bhtsd->bhts', q_, k_, decay)
        b_last = b_[..., -1, :]
        o = attn @ v_ + jnp.einsum('bhtd,bhdv->bhtv', q_ * jnp.exp(b_), S)
        S = jnp.exp(b_last)[..., None] * S + jnp.einsum(
            'bhsd,bhsv->bhdv', k_ * jnp.exp(b_last[..., None, :] - b_), v_)
        return S, o

    S0 = jnp.zeros(q.shape[:2] + (q.shape[-1], v.shape[-1]), jnp.float32)
    _, o = lax.scan(step, S0, (qc, kc, vc, bc))
    return _from_chunks(o)


def hgrn2_mixer(q_raw, f_fwd, f_bwd, i_raw, g_raw, lb, out_gain):
    f32 = jnp.float32
    b, l = q_raw.shape[:2]
    q = _heads(jax.nn.silu(q_raw.astype(f32)), A_HEADS)
    v = _heads(i_raw.astype(f32), A_HEADS)

    def gates(z, lbd):
        z = z.astype(f32)
        log_f = jnp.logaddexp(jnp.log(lbd), jnp.log1p(-lbd) + jax.nn.log_sigmoid(z))
        k = (1.0 - lbd) * jax.nn.sigmoid(-z)
        return _heads(log_f, A_HEADS), _heads(k, A_HEADS)

    lf_f, k_f = gates(f_fwd, lb[0])
    lf_b, k_b = gates(f_bwd, lb[1])
    o = hgrn2_scan(q, k_f, v, lf_f) + _flip(hgrn2_scan(_flip(q), _flip(k_b), _flip(v), _flip(lf_b)))
    o = rmsnorm(jnp.transpose(o, (0, 2, 1, 3)), out_gain).reshape(b, l, A_WIDTH)
    return o * jax.nn.silu(g_raw.astype(f32))


def s5_direction(u, lam_re, lam_im, log_step, b_re, b_im, c_re, c_im, reverse):
    step = jnp.exp(log_step)[:, None]
    mag = jnp.exp(lam_re * step)
    abar_re = mag * jnp.cos(lam_im * step)
    abar_im = mag * jnp.sin(lam_im * step)
    den = lam_re * lam_re + lam_im * lam_im
    fr = ((abar_re - 1.0) * lam_re + abar_im * lam_im) / den
    fi = (abar_im * lam_re - (abar_re - 1.0) * lam_im) / den
    bb_re = fr[..., None] * b_re - fi[..., None] * b_im
    bb_im = fr[..., None] * b_im + fi[..., None] * b_re
    bu_re = jnp.einsum('blgp,gnp->lbgn', u, bb_re)
    bu_im = jnp.einsum('blgp,gnp->lbgn', u, bb_im)
    l = u.shape[1]
    a_re = jnp.broadcast_to(abar_re, (l,) + abar_re.shape)
    a_im = jnp.broadcast_to(abar_im, (l,) + abar_im.shape)

    def combine(e1, e2):
        a1r, a1i, b1r, b1i = e1
        a2r, a2i, b2r, b2i = e2
        ar = a2r * a1r - a2i * a1i
        ai = a2r * a1i + a2i * a1r
        a2r_, a2i_ = a2r[:, None], a2i[:, None]
        br = a2r_ * b1r - a2i_ * b1i + b2r
        bi = a2r_ * b1i + a2i_ * b1r + b2i
        return ar, ai, br, bi

    _, _, xr, xi = lax.associative_scan(combine, (a_re, a_im, bu_re, bu_im), reverse=reverse, axis=0)
    return jnp.einsum('lbgn,gpn->blgp', xr, c_re) - jnp.einsum('lbgn,gpn->blgp', xi, c_im)


def s5_mixer(u_raw, lam_re, lam_im, log_step, b_re, b_im, c_re, c_im, d_skip, glu_w, glu_b):
    f32 = jnp.float32
    b, l = u_raw.shape[:2]
    u = u_raw.astype(f32).reshape(b, l, S5_GROUPS, S5_GROUP)
    y = d_skip.astype(f32).reshape(S5_GROUPS, S5_GROUP) * u
    for direction in range(2):
        y = y + s5_direction(u, lam_re[direction].astype(f32), lam_im[direction].astype(f32),
                             log_step[direction].astype(f32), b_re[direction].astype(f32),
                             b_im[direction].astype(f32), c_re[direction].astype(f32),
                             c_im[direction].astype(f32), reverse=(direction == 1))
    y = jax.nn.gelu(y.reshape(b, l, B_WIDTH))
    return y * jax.nn.sigmoid(y @ glu_w.astype(f32) + glu_b.astype(f32))


def t5_bucket(rel):
    half = REL_BUCKETS // 2
    max_exact = half // 2
    base = jnp.where(rel > 0, half, 0)
    n = jnp.abs(rel)
    nf = jnp.maximum(n, 1).astype(jnp.float32)
    large = max_exact + (jnp.log(nf / max_exact) / math.log(REL_MAX_DIST / max_exact)
                         * (half - max_exact)).astype(jnp.int32)
    large = jnp.minimum(large, half - 1)
    return base + jnp.where(n < max_exact, n, large)


def diff_attention(q_raw, k_raw, v_raw, q_gain, k_gain, lam, out_gain, rel_bias, layer_idx):
    f32 = jnp.float32
    b, l = q_raw.shape[:2]
    q = rmsnorm(q_raw.astype(f32).reshape(b, l, C_HEADS, 2, C_HEAD_DIM), q_gain) * (C_HEAD_DIM ** -0.5)
    k = rmsnorm(k_raw.astype(f32).reshape(b, l, C_HEADS, 2, C_HEAD_DIM), k_gain)
    v = v_raw.astype(f32).reshape(b, l, C_HEADS, C_V_DIM)
    lam_init = 0.8 - 0.6 * math.exp(-0.3 * layer_idx)
    lam_f = lam.astype(f32)
    lam_full = jnp.exp(jnp.sum(lam_f[0] * lam_f[1])) - jnp.exp(jnp.sum(lam_f[2] * lam_f[3])) + lam_init
    n_blocks = l // Q_BLOCK
    qb = jnp.moveaxis(q.reshape(b, n_blocks, Q_BLOCK, C_HEADS, 2, C_HEAD_DIM), 1, 0)
    k_pos = jnp.arange(l)
    table = rel_bias.astype(f32)

    def block(args):
        q_blk, blk = args
        q_pos = blk * Q_BLOCK + jnp.arange(Q_BLOCK)
        bias = jnp.transpose(table[t5_bucket(k_pos[None, :] - q_pos[:, None])], (2, 0, 1))
        s = jnp.einsum('bqhcd,bkhcd->bchqk', q_blk, k) + bias[None, None]
        p = jax.nn.softmax(s, axis=-1)
        w = p[:, 0] - lam_full * p[:, 1]
        return jnp.einsum('bhqk,bkhv->bqhv', w, v)

    o = lax.map(block, (qb, jnp.arange(n_blocks)))
    o = jnp.moveaxis(o, 0, 1).reshape(b, l, C_HEADS, C_V_DIM)
    o = rmsnorm(o, out_gain) * (1.0 - lam_init)
    return o.reshape(b, l, C_WIDTH)


def short_conv(x, w):
    ch = x.shape[-1]
    pad = CONV_WIDTH // 2
    return lax.conv_general_dilated(x, w[:, None, :].astype(x.dtype), window_strides=(1,),
                                    padding=[(pad, pad)], dimension_numbers=('NWC', 'WIO', 'NWC'),
                                    feature_group_count=ch)


def gdn_scan(q, k, v, g, beta):
    b, h, l, dk = q.shape
    dv = v.shape[-1]
    nc = l // CHUNK

    def chunks(t):
        return t.reshape((b, h, nc, CHUNK) + t.shape[3:])

    q, k, v, g, beta = (chunks(t) for t in (q, k, v, g, beta))
    gam = jnp.cumsum(g, axis=-1)
    incl = jnp.tril(jnp.ones((CHUNK, CHUNK), bool))
    strict = jnp.tril(jnp.ones((CHUNK, CHUNK), bool), -1)
    diff = gam[..., :, None] - gam[..., None, :]
    decay = jnp.where(incl, jnp.exp(jnp.where(incl, diff, 0.0)), 0.0)
    k_beta = k * beta[..., None]
    v_beta = v * beta[..., None]
    a = jnp.where(strict, jnp.einsum('bhntd,bhnsd->bhnts', k_beta, k) * decay, 0.0)
    m = a + jnp.eye(CHUNK, dtype=a.dtype)
    value = lax.linalg.triangular_solve(m, v_beta, left_side=True, lower=True, unit_diagonal=True)
    k_cum = lax.linalg.triangular_solve(m, k_beta * jnp.exp(gam)[..., None], left_side=True,
                                        lower=True, unit_diagonal=True)
    attn = jnp.einsum('bhntd,bhnsd->bhnts', q, k) * decay
    q_dec = q * jnp.exp(gam)[..., None]
    g_last = gam[..., -1]
    k_dec = k * jnp.exp(g_last[..., None] - gam)[..., None]
    xs = tuple(jnp.moveaxis(t, 2, 0) for t in (value, k_cum, attn, q_dec, k_dec, g_last))

    def step(S, inp):
        val, kc, at, qd, kd, gl = inp
        v_new = val - kc @ S
        o = qd @ S + at @ v_new
        S = jnp.exp(gl)[..., None, None] * S + jnp.einsum('bhsd,bhsv->bhdv', kd, v_new)
        return S, o

    S0 = jnp.zeros((b, h, dk, dv), jnp.float32)
    _, o = lax.scan(step, S0, xs)
    return jnp.moveaxis(o, 0, 2).reshape(b, h, l, dv)


def gated_deltanet(qkv_raw, a_raw, b_raw, g_raw, conv_w, a_log, dt_bias, out_gain):
    f32 = jnp.float32
    bsz, l = qkv_raw.shape[:2]
    qkv = jax.nn.silu(short_conv(qkv_raw, conv_w).astype(f32))
    q, k, v = jnp.split(qkv, 3, axis=-1)
    q = l2norm(_heads(q, D_HEADS)) * (D_HEAD_DIM ** -0.5)
    k = l2norm(_heads(k, D_HEADS))
    v = _heads(v, D_HEADS)
    a = a_raw.astype(f32).reshape(bsz, l, 2, D_HEADS)
    bb = b_raw.astype(f32).reshape(bsz, l, 2, D_HEADS)
    a_log = a_log.astype(f32)
    dt_bias = dt_bias.astype(f32)
    o = jnp.zeros((bsz, D_HEADS, l, D_HEAD_DIM), f32)
    for direction in range(2):
        g = -jnp.exp(a_log[direction]) * jax.nn.softplus(a[:, :, direction] + dt_bias[direction])
        beta = jax.nn.sigmoid(bb[:, :, direction])
        g = jnp.transpose(g, (0, 2, 1))
        beta = jnp.transpose(beta, (0, 2, 1))
        if direction == 0:
            o = o + gdn_scan(q, k, v, g, beta)
        else:
            o = o + _flip(gdn_scan(_flip(q), _flip(k), _flip(v), _flip(g), _flip(beta)))
    o = rmsnorm(jnp.transpose(o, (0, 2, 1, 3)), out_gain).reshape(bsz, l, D_WIDTH)
    return o * jax.nn.silu(g_raw.astype(f32))


def ec_moe(h, w_router, w_gate, w_up, w_down):
    b, l, d = h.shape
    cap = EC_CAPACITY_FACTOR * l // N_EXPERTS
    logits = jnp.einsum('bld,de->ble', h, w_router).astype(jnp.float32)
    aff = jax.nn.softmax(logits, axis=-1)
    gate, idx = lax.top_k(jnp.swapaxes(aff, 1, 2), cap)
    xs = jax.vmap(lambda hb, ib: hb[ib])(h, idx)
    hid = jax.nn.silu(jnp.einsum('becd,edf->becf', xs, w_gate)) * jnp.einsum('becd,edf->becf', xs, w_up)
    out = jnp.einsum('becf,efd->becd', hid, w_down) * gate[..., None].astype(h.dtype)
    return jax.vmap(lambda ob, ib: jnp.zeros((l, d), ob.dtype).at[ib.reshape(-1)].add(ob.reshape(-1, d)))(out, idx)


def setup_inputs(seed: int = 0) -> dict:
    key = jax.random.key(seed)
    ks = jax.random.split(key, 32)
    nrm = jax.random.normal
    f32 = jnp.float32
    x = nrm(ks[0], (BATCH, SEQ, D_MODEL), f32)
    mix_norm = 1.0 + 0.02 * nrm(ks[1], (DEPTH, D_MODEL), f32)
    ffn_norm = 1.0 + 0.02 * nrm(ks[2], (DEPTH, D_MODEL), f32)
    ev_w_in = nrm(ks[3], (N_EVEN, D_MODEL, EVEN_IN), f32) * D_MODEL ** -0.5
    ev_w_out = nrm(ks[4], (N_EVEN, MIX_WIDTH, D_MODEL), f32) * MIX_WIDTH ** -0.5
    a_lb_logits = 0.1 * nrm(ks[5], (N_EVEN, 2, A_WIDTH), f32)
    a_out_norm = 1.0 + 0.02 * nrm(ks[6], (N_EVEN, A_HEAD_DIM), f32)
    s5_shape = (N_EVEN, 2, S5_GROUPS, S5_STATE)
    s5_lambda_re = -0.5 + 0.01 * nrm(ks[7], s5_shape, f32)
    s5_lambda_im = math.pi * jnp.arange(S5_STATE, dtype=f32) + 0.01 * nrm(ks[8], s5_shape, f32)
    s5_log_step = jax.random.uniform(ks[9], (N_EVEN, 2, S5_GROUPS), f32, math.log(1e-3), math.log(1e-1))
    s5_b_re = nrm(ks[10], s5_shape + (S5_GROUP,), f32) * (2 * S5_GROUP) ** -0.5
    s5_b_im = nrm(ks[11], s5_shape + (S5_GROUP,), f32) * (2 * S5_GROUP) ** -0.5
    c_shape = (N_EVEN, 2, S5_GROUPS, S5_GROUP, S5_STATE)
    s5_c_re = nrm(ks[12], c_shape, f32) * S5_STATE ** -0.5
    s5_c_im = nrm(ks[13], c_shape, f32) * S5_STATE ** -0.5
    s5_d = 0.5 * nrm(ks[14], (N_EVEN, B_WIDTH), f32)
    s5_glu_w = nrm(ks[15], (N_EVEN, B_WIDTH, B_WIDTH), f32) * B_WIDTH ** -0.5
    s5_glu_b = 0.01 * nrm(ks[16], (N_EVEN, B_WIDTH), f32)
    od_w_in = nrm(ks[17], (N_ODD, D_MODEL, ODD_IN), f32) * D_MODEL ** -0.5
    od_w_out = nrm(ks[18], (N_ODD, MIX_WIDTH, D_MODEL), f32) * MIX_WIDTH ** -0.5
    c_q_norm = 1.0 + 0.02 * nrm(ks[19], (N_ODD, C_HEAD_DIM), f32)
    c_k_norm = 1.0 + 0.02 * nrm(ks[20], (N_ODD, C_HEAD_DIM), f32)
    c_lambda = 0.1 * nrm(ks[21], (N_ODD, 4, C_HEAD_DIM), f32)
    c_out_norm = 1.0 + 0.02 * nrm(ks[22], (N_ODD, C_V_DIM), f32)
    rel_bias = 0.5 * nrm(ks[23], (REL_BUCKETS, C_HEADS), f32)
    d_conv_w = nrm(ks[24], (N_ODD, CONV_WIDTH, 3 * D_WIDTH), f32) * CONV_WIDTH ** -0.5
    d_a_log = jnp.log(jax.random.uniform(ks[25], (N_ODD, 2, D_HEADS), f32, 1.0, 16.0))
    dt = jnp.exp(jax.random.uniform(ks[26], (N_ODD, 2, D_HEADS), f32, math.log(1e-3), math.log(1e-1)))
    d_dt_bias = dt + jnp.log(-jnp.expm1(-dt))
    d_out_norm = 1.0 + 0.02 * nrm(ks[27], (N_ODD, D_HEAD_DIM), f32)
    moe_router = nrm(ks[28], (DEPTH, D_MODEL, N_EXPERTS), f32) * D_MODEL ** -0.5
    moe_w_gate = nrm(ks[29], (DEPTH, N_EXPERTS, D_MODEL, EXPERT_FF), f32) * D_MODEL ** -0.5
    moe_w_up = nrm(ks[30], (DEPTH, N_EXPERTS, D_MODEL, EXPERT_FF), f32) * D_MODEL ** -0.5
    moe_w_down = nrm(ks[31], (DEPTH, N_EXPERTS, EXPERT_FF, D_MODEL), f32) * EXPERT_FF ** -0.5
    return {'x': x, 'mix_norm': mix_norm, 'ffn_norm': ffn_norm, 'ev_w_in': ev_w_in, 'ev_w_out': ev_w_out,
            'a_lb_logits': a_lb_logits, 'a_out_norm': a_out_norm, 's5_lambda_re': s5_lambda_re,
            's5_lambda_im': s5_lambda_im, 's5_log_step': s5_log_step, 's5_b_re': s5_b_re, 's5_b_im': s5_b_im,
            's5_c_re': s5_c_re, 's5_c_im': s5_c_im, 's5_d': s5_d, 's5_glu_w': s5_glu_w, 's5_glu_b': s5_glu_b,
            'od_w_in': od_w_in, 'od_w_out': od_w_out, 'c_q_norm': c_q_norm, 'c_k_norm': c_k_norm,
            'c_lambda': c_lambda, 'c_out_norm': c_out_norm, 'rel_bias': rel_bias, 'd_conv_w': d_conv_w,
            'd_a_log': d_a_log, 'd_dt_bias': d_dt_bias, 'd_out_norm': d_out_norm, 'moe_router': moe_router,
            'moe_w_gate': moe_w_gate, 'moe_w_up': moe_w_up, 'moe_w_down': moe_w_down}


def reference(x, mix_norm, ffn_norm, ev_w_in, ev_w_out, a_lb_logits, a_out_norm, s5_lambda_re,
              s5_lambda_im, s5_log_step, s5_b_re, s5_b_im, s5_c_re, s5_c_im, s5_d, s5_glu_w, s5_glu_b,
              od_w_in, od_w_out, c_q_norm, c_k_norm, c_lambda, c_out_norm, rel_bias, d_conv_w,
              d_a_log, d_dt_bias, d_out_norm, moe_router, moe_w_gate, moe_w_up, moe_w_down):
    p = jax.nn.softmax(a_lb_logits.astype(jnp.float32), axis=0)
    cum = jnp.cumsum(p, axis=0)
    lower_bounds = cum - cum[0:1]
    for layer in range(DEPTH):
        h = rmsnorm(x, mix_norm[layer])
        j = layer // 2
        if layer % 2 == 0:
            proj = h @ ev_w_in[j]
            q_a, f_fw, f_bw, i_a, g_a, u_b = jnp.split(
                proj, [A_WIDTH, 2 * A_WIDTH, 3 * A_WIDTH, 4 * A_WIDTH, 5 * A_WIDTH], axis=-1)
            o_a = hgrn2_mixer(q_a, f_fw, f_bw, i_a, g_a, lower_bounds[j], a_out_norm[j])
            o_b = s5_mixer(u_b, s5_lambda_re[j], s5_lambda_im[j], s5_log_step[j], s5_b_re[j], s5_b_im[j],
                           s5_c_re[j], s5_c_im[j], s5_d[j], s5_glu_w[j], s5_glu_b[j])
            mixed = jnp.concatenate([o_a, o_b], axis=-1).astype(x.dtype)
            x = x + mixed @ ev_w_out[j]
        else:
            o1 = 3 * C_WIDTH
            o2 = o1 + 3 * D_WIDTH
            o3 = o2 + 2 * D_HEADS
            o4 = o3 + 2 * D_HEADS
            q_c, k_c, v_c, qkv_d, a_d, b_d, g_d = jnp.split(
                h @ od_w_in[j], [C_WIDTH, 2 * C_WIDTH, o1, o2, o3, o4], axis=-1)
            o_c = diff_attention(q_c, k_c, v_c, c_q_norm[j], c_k_norm[j], c_lambda[j], c_out_norm[j],
                                 rel_bias, layer)
            o_d = gated_deltanet(qkv_d, a_d, b_d, g_d, d_conv_w[j], d_a_log[j], d_dt_bias[j], d_out_norm[j])
            mixed = jnp.concatenate([o_c, o_d], axis=-1).astype(x.dtype)
            x = x + mixed @ od_w_out[j]
        x = x + ec_moe(rmsnorm(x, ffn_norm[layer]), moe_router[layer], moe_w_gate[layer],
                       moe_w_up[layer], moe_w_down[layer])
    return x
```

```python
import functools
import math

import jax
import jax.numpy as jnp
from jax import lax
from jax.experimental import pallas as pl
from jax.experimental.pallas import tpu as pltpu

D_MODEL = 1024
DEPTH = 4
MIX_WIDTH = D_MODEL
A_WIDTH = MIX_WIDTH // 2
A_HEAD_DIM = 128
A_HEADS = A_WIDTH // A_HEAD_DIM
B_WIDTH = MIX_WIDTH - A_WIDTH
S5_GROUP = 16
S5_GROUPS = B_WIDTH // S5_GROUP
S5_STATE = 64
C_WIDTH = MIX_WIDTH // 2
C_HEAD_DIM = 64
C_HEADS = C_WIDTH // (2 * C_HEAD_DIM)
C_V_DIM = 2 * C_HEAD_DIM
D_WIDTH = MIX_WIDTH - C_WIDTH
D_HEAD_DIM = 128
D_HEADS = D_WIDTH // D_HEAD_DIM
CONV_WIDTH = 5
N_EXPERTS = 16
EXPERT_FF = 2 * D_MODEL
EC_CAPACITY_FACTOR = 2
REL_BUCKETS = 32
REL_MAX_DIST = 128
CHUNK = 64
Q_BLOCK = 128
EPS = 1e-6

VMEM_LIMIT_BYTES = 48 * 1024 * 1024


def _norm_matmul_kernel(x_ref, g_ref, w_ref, o_ref):
    x = x_ref[...]
    y = x * lax.rsqrt(jnp.mean(x * x, axis=-1, keepdims=True) + EPS) * g_ref[...]
    o_ref[...] = jnp.dot(y.astype(jnp.bfloat16), w_ref[...], preferred_element_type=jnp.float32)


def norm_matmul(x, gain, w, *, tm=512, tn=512):
    n, k = x.shape
    m = w.shape[1]
    tn = min(tn, m)
    assert n % tm == 0 and m % tn == 0
    return pl.pallas_call(
        _norm_matmul_kernel,
        grid=(n // tm, m // tn),
        in_specs=[pl.BlockSpec((tm, k), lambda i, j: (i, 0)),
                  pl.BlockSpec((1, k), lambda i, j: (0, 0)),
                  pl.BlockSpec((k, tn), lambda i, j: (0, j))],
        out_specs=pl.BlockSpec((tm, tn), lambda i, j: (i, j)),
        out_shape=jax.ShapeDtypeStruct((n, m), jnp.float32),
        compiler_params=pltpu.CompilerParams(dimension_semantics=("parallel", "parallel"),
                                             vmem_limit_bytes=VMEM_LIMIT_BYTES),
        name="norm_matmul",
    )(x, gain.reshape(1, k).astype(jnp.float32), w.astype(jnp.bfloat16))


def _matmul_res_kernel(a_ref, w_ref, r_ref, o_ref):
    o_ref[...] = r_ref[...] + jnp.dot(a_ref[...].astype(jnp.bfloat16), w_ref[...],
                                      preferred_element_type=jnp.float32)


def matmul_residual(a, w, res, *, tm=512):
    n, k = a.shape
    m = w.shape[1]
    return pl.pallas_call(
        _matmul_res_kernel,
        grid=(n // tm,),
        in_specs=[pl.BlockSpec((tm, k), lambda i: (i, 0)),
                  pl.BlockSpec((k, m), lambda i: (0, 0)),
                  pl.BlockSpec((tm, m), lambda i: (i, 0))],
        out_specs=pl.BlockSpec((tm, m), lambda i: (i, 0)),
        out_shape=jax.ShapeDtypeStruct((n, m), jnp.float32),
        compiler_params=pltpu.CompilerParams(dimension_semantics=("parallel",),
                                             vmem_limit_bytes=VMEM_LIMIT_BYTES),
        name="matmul_residual",
    )(a, w.astype(jnp.bfloat16), res)


def rmsnorm(x, g):
    xf = x.astype(jnp.float32)
    y = xf * lax.rsqrt(jnp.mean(xf * xf, axis=-1, keepdims=True) + EPS)
    return (y * g.astype(jnp.float32)).astype(x.dtype)


def l2norm(x):
    return x * lax.rsqrt(jnp.sum(x * x, axis=-1, keepdims=True) + EPS)


def _heads(t, n_heads):
    b, l, w = t.shape
    return jnp.transpose(t.reshape(b, l, n_heads, w // n_heads), (0, 2, 1, 3))


def _flip(t):
    return jnp.flip(t, axis=2)


def _to_chunks(t):
    b, h, l = t.shape[:3]
    t = t.reshape((b, h, l // CHUNK, CHUNK) + t.shape[3:])
    return jnp.moveaxis(t, 2, 0)


def _from_chunks(t):
    t = jnp.moveaxis(t, 0, 2)
    b, h, nc, c = t.shape[:4]
    return t.reshape((b, h, nc * c) + t.shape[4:])


def hgrn2_scan(q, k, v, log_f):
    tri = jnp.tril(jnp.ones((CHUNK, CHUNK), bool))[:, :, None]
    qc, kc, vc, lc = (_to_chunks(t) for t in (q, k, v, log_f))
    bc = jnp.cumsum(lc, axis=-2)

    def step(S, inp):
        q_, k_, v_, b_ = inp
        diff = b_[..., :, None, :] - b_[..., None, :, :]
        decay = jnp.where(tri, jnp.exp(jnp.where(tri, diff, 0.0)), 0.0)
        attn = jnp.einsum('bhtd,bhsd,bhtsd->bhts', q_, k_, decay)
        b_last = b_[..., -1, :]
        o = attn @ v_ + jnp.einsum('bhtd,bhdv->bhtv', q_ * jnp.exp(b_), S)
        S = jnp.exp(b_last)[..., None] * S + jnp.einsum(
            'bhsd,bhsv->bhdv', k_ * jnp.exp(b_last[..., None, :] - b_), v_)
        return S, o

    S0 = jnp.zeros(q.shape[:2] + (q.shape[-1], v.shape[-1]), jnp.float32)
    _, o = lax.scan(step, S0, (qc, kc, vc, bc))
    return _from_chunks(o)


def hgrn2_mixer(q_raw, f_fwd, f_bwd, i_raw, g_raw, lb, out_gain):
    f32 = jnp.float32
    b, l = q_raw.shape[:2]
    q = _heads(jax.nn.silu(q_raw.astype(f32)), A_HEADS)
    v = _heads(i_raw.astype(f32), A_HEADS)

    def gates(z, lbd):
        z = z.astype(f32)
        log_f = jnp.logaddexp(jnp.log(lbd), jnp.log1p(-lbd) + jax.nn.log_sigmoid(z))
        k = (1.0 - lbd) * jax.nn.sigmoid(-z)
        return _heads(log_f, A_HEADS), _heads(k, A_HEADS)

    lf_f, k_f = gates(f_fwd, lb[0])
    lf_b, k_b = gates(f_bwd, lb[1])
    o = hgrn2_scan(q, k_f, v, lf_f) + _flip(hgrn2_scan(_flip(q), _flip(k_b), _flip(v), _flip(lf_b)))
    o = rmsnorm(jnp.transpose(o, (0, 2, 1, 3)), out_gain).reshape(b, l, A_WIDTH)
    return o * jax.nn.silu(g_raw.astype(f32))


def s5_direction(u, lam_re, lam_im, log_step, b_re, b_im, c_re, c_im, reverse):
    step = jnp.exp(log_step)[:, None]
    mag = jnp.exp(lam_re * step)
    abar_re = mag * jnp.cos(lam_im * step)
    abar_im = mag * jnp.sin(lam_im * step)
    den = lam_re * lam_re + lam_im * lam_im
    fr = ((abar_re - 1.0) * lam_re + abar_im * lam_im) / den
    fi = (abar_im * lam_re - (abar_re - 1.0) * lam_im) / den
    bb_re = fr[..., None] * b_re - fi[..., None] * b_im
    bb_im = fr[..., None] * b_im + fi[..., None] * b_re
    bu_re = jnp.einsum('blgp,gnp->lbgn', u, bb_re)
    bu_im = jnp.einsum('blgp,gnp->lbgn', u, bb_im)
    l = u.shape[1]
    a_re = jnp.broadcast_to(abar_re, (l,) + abar_re.shape)
    a_im = jnp.broadcast_to(abar_im, (l,) + abar_im.shape)

    def combine(e1, e2):
        a1r, a1i, b1r, b1i = e1
        a2r, a2i, b2r, b2i = e2
        ar = a2r * a1r - a2i * a1i
        ai = a2r * a1i + a2i * a1r
        a2r_, a2i_ = a2r[:, None], a2i[:, None]
        br = a2r_ * b1r - a2i_ * b1i + b2r
        bi = a2r_ * b1i + a2i_ * b1r + b2i
        return ar, ai, br, bi

    _, _, xr, xi = lax.associative_scan(combine, (a_re, a_im, bu_re, bu_im), reverse=reverse, axis=0)
    return jnp.einsum('lbgn,gpn->blgp', xr, c_re) - jnp.einsum('lbgn,gpn->blgp', xi, c_im)


def s5_mixer(u_raw, lam_re, lam_im, log_step, b_re, b_im, c_re, c_im, d_skip, glu_w, glu_b):
    f32 = jnp.float32
    b, l = u_raw.shape[:2]
    u = u_raw.astype(f32).reshape(b, l, S5_GROUPS, S5_GROUP)
    y = d_skip.astype(f32).reshape(S5_GROUPS, S5_GROUP) * u
    for direction in range(2):
        y = y + s5_direction(u, lam_re[direction], lam_im[direction], log_step[direction],
                             b_re[direction], b_im[direction], c_re[direction], c_im[direction],
                             reverse=(direction == 1))
    y = jax.nn.gelu(y.reshape(b, l, B_WIDTH))
    return y * jax.nn.sigmoid(y @ glu_w.astype(f32) + glu_b.astype(f32))


def t5_bucket(rel):
    half = REL_BUCKETS // 2
    max_exact = half // 2
    base = jnp.where(rel > 0, half, 0)
    n = jnp.abs(rel)
    nf = jnp.maximum(n, 1).astype(jnp.float32)
    large = max_exact + (jnp.log(nf / max_exact) / math.log(REL_MAX_DIST / max_exact)
                         * (half - max_exact)).astype(jnp.int32)
    large = jnp.minimum(large, half - 1)
    return base + jnp.where(n < max_exact, n, large)


def diff_attention(q_raw, k_raw, v_raw, q_gain, k_gain, lam, out_gain, rel_bias, layer_idx):
    f32 = jnp.float32
    b, l = q_raw.shape[:2]
    q = rmsnorm(q_raw.astype(f32).reshape(b, l, C_HEADS, 2, C_HEAD_DIM), q_gain) * (C_HEAD_DIM ** -0.5)
    k = rmsnorm(k_raw.astype(f32).reshape(b, l, C_HEADS, 2, C_HEAD_DIM), k_gain)
    v = v_raw.astype(f32).reshape(b, l, C_HEADS, C_V_DIM)
    lam_init = 0.8 - 0.6 * math.exp(-0.3 * layer_idx)
    lam_f = lam.astype(f32)
    lam_full = jnp.exp(jnp.sum(lam_f[0] * lam_f[1])) - jnp.exp(jnp.sum(lam_f[2] * lam_f[3])) + lam_init
    n_blocks = l // Q_BLOCK
    qb = jnp.moveaxis(q.reshape(b, n_blocks, Q_BLOCK, C_HEADS, 2, C_HEAD_DIM), 1, 0)
    k_pos = jnp.arange(l)
    table = rel_bias.astype(f32)

    def block(args):
        q_blk, blk = args
        q_pos = blk * Q_BLOCK + jnp.arange(Q_BLOCK)
        bias = jnp.transpose(table[t5_bucket(k_pos[None, :] - q_pos[:, None])], (2, 0, 1))
        s = jnp.einsum('bqhcd,bkhcd->bchqk', q_blk, k) + bias[None, None]
        p = jax.nn.softmax(s, axis=-1)
        w = p[:, 0] - lam_full * p[:, 1]
        return jnp.einsum('bhqk,bkhv->bqhv', w, v)

    o = lax.map(block, (qb, jnp.arange(n_blocks)))
    o = jnp.moveaxis(o, 0, 1).reshape(b, l, C_HEADS, C_V_DIM)
    o = rmsnorm(o, out_gain) * (1.0 - lam_init)
    return o.reshape(b, l, C_WIDTH)


def short_conv(x, w):
    ch = x.shape[-1]
    pad = CONV_WIDTH // 2
    return lax.conv_general_dilated(x, w[:, None, :].astype(x.dtype), window_strides=(1,),
                                    padding=[(pad, pad)], dimension_numbers=('NWC', 'WIO', 'NWC'),
                                    feature_group_count=ch)


def gdn_scan(q, k, v, g, beta):
    b, h, l, dk = q.shape
    dv = v.shape[-1]
    nc = l // CHUNK

    def chunks(t):
        return t.reshape((b, h, nc, CHUNK) + t.shape[3:])

    q, k, v, g, beta = (chunks(t) for t in (q, k, v, g, beta))
    gam = jnp.cumsum(g, axis=-1)
    incl = jnp.tril(jnp.ones((CHUNK, CHUNK), bool))
    strict = jnp.tril(jnp.ones((CHUNK, CHUNK), bool), -1)
    diff = gam[..., :, None] - gam[..., None, :]
    decay = jnp.where(incl, jnp.exp(jnp.where(incl, diff, 0.0)), 0.0)
    k_beta = k * beta[..., None]
    v_beta = v * beta[..., None]
    a = jnp.where(strict, jnp.einsum('bhntd,bhnsd->bhnts', k_beta, k) * decay, 0.0)
    m = a + jnp.eye(CHUNK, dtype=a.dtype)
    value = lax.linalg.triangular_solve(m, v_beta, left_side=True, lower=True, unit_diagonal=True)
    k_cum = lax.linalg.triangular_solve(m, k_beta * jnp.exp(gam)[..., None], left_side=True,
                                        lower=True, unit_diagonal=True)
    attn = jnp.einsum('bhntd,bhnsd->bhnts', q, k) * decay
    q_dec = q * jnp.exp(gam)[..., None]
    g_last = gam[..., -1]
    k_dec = k * jnp.exp(g_last[..., None] - gam)[..., None]
    xs = tuple(jnp.moveaxis(t, 2, 0) for t in (value, k_cum, attn, q_dec, k_dec, g_last))

    def step(S, inp):
        val, kc, at, qd, kd, gl = inp
        v_new = val - kc @ S
        o = qd @ S + at @ v_new
        S = jnp.exp(gl)[..., None, None] * S + jnp.einsum('bhsd,bhsv->bhdv', kd, v_new)
        return S, o

    S0 = jnp.zeros((b, h, dk, dv), jnp.float32)
    _, o = lax.scan(step, S0, xs)
    return jnp.moveaxis(o, 0, 2).reshape(b, h, l, dv)


def gated_deltanet(qkv_raw, a_raw, b_raw, g_raw, conv_w, a_log, dt_bias, out_gain):
    f32 = jnp.float32
    bsz, l = qkv_raw.shape[:2]
    qkv = jax.nn.silu(short_conv(qkv_raw, conv_w).astype(f32))
    q, k, v = jnp.split(qkv, 3, axis=-1)
    q = l2norm(_heads(q, D_HEADS)) * (D_HEAD_DIM ** -0.5)
    k = l2norm(_heads(k, D_HEADS))
    v = _heads(v, D_HEADS)
    a = a_raw.astype(f32).reshape(bsz, l, 2, D_HEADS)
    bb = b_raw.astype(f32).reshape(bsz, l, 2, D_HEADS)
    a_log = a_log.astype(f32)
    dt_bias = dt_bias.astype(f32)
    o = jnp.zeros((bsz, D_HEADS, l, D_HEAD_DIM), f32)
    for direction in range(2):
        g = -jnp.exp(a_log[direction]) * jax.nn.softplus(a[:, :, direction] + dt_bias[direction])
        beta = jax.nn.sigmoid(bb[:, :, direction])
        g = jnp.transpose(g, (0, 2, 1))
        beta = jnp.transpose(beta, (0, 2, 1))
        if direction == 0:
            o = o + gdn_scan(q, k, v, g, beta)
        else:
            o = o + _flip(gdn_scan(_flip(q), _flip(k), _flip(v), _flip(g), _flip(beta)))
    o = rmsnorm(jnp.transpose(o, (0, 2, 1, 3)), out_gain).reshape(bsz, l, D_WIDTH)
    return o * jax.nn.silu(g_raw.astype(f32))


def ec_moe(h, w_router, w_gate, w_up, w_down):
    b, l, d = h.shape
    cap = EC_CAPACITY_FACTOR * l // N_EXPERTS
    logits = jnp.einsum('bld,de->ble', h, w_router).astype(jnp.float32)
    aff = jax.nn.softmax(logits, axis=-1)
    gate, idx = lax.top_k(jnp.swapaxes(aff, 1, 2), cap)
    xs = jax.vmap(lambda hb, ib: hb[ib])(h, idx)
    hid = jax.nn.silu(jnp.einsum('becd,edf->becf', xs, w_gate)) * jnp.einsum('becd,edf->becf', xs, w_up)
    out = jnp.einsum('becf,efd->becd', hid, w_down) * gate[..., None].astype(h.dtype)
    return jax.vmap(lambda ob, ib: jnp.zeros((l, d), ob.dtype).at[ib.reshape(-1)].add(ob.reshape(-1, d)))(out, idx)


def kernel(x, mix_norm, ffn_norm, ev_w_in, ev_w_out, a_lb_logits, a_out_norm, s5_lambda_re, s5_lambda_im, s5_log_step, s5_b_re, s5_b_im, s5_c_re, s5_c_im, s5_d, s5_glu_w, s5_glu_b, od_w_in, od_w_out, c_q_norm, c_k_norm, c_lambda, c_out_norm, rel_bias, d_conv_w, d_a_log, d_dt_bias, d_out_norm, moe_router, moe_w_gate, moe_w_up, moe_w_down):
    bsz, l, d = x.shape
    n = bsz * l
    p = jax.nn.softmax(a_lb_logits.astype(jnp.float32), axis=0)
    cum = jnp.cumsum(p, axis=0)
    lower_bounds = cum - cum[0:1]
    for layer in range(DEPTH):
        j = layer // 2
        if layer % 2 == 0:
            proj = norm_matmul(x.reshape(n, d), mix_norm[layer], ev_w_in[j]).reshape(bsz, l, -1)
            q_a, f_fw, f_bw, i_a, g_a, u_b = jnp.split(
                proj, [A_WIDTH, 2 * A_WIDTH, 3 * A_WIDTH, 4 * A_WIDTH, 5 * A_WIDTH], axis=-1)
            o_a = hgrn2_mixer(q_a, f_fw, f_bw, i_a, g_a, lower_bounds[j], a_out_norm[j])
            o_b = s5_mixer(u_b, s5_lambda_re[j], s5_lambda_im[j], s5_log_step[j], s5_b_re[j], s5_b_im[j],
                           s5_c_re[j], s5_c_im[j], s5_d[j], s5_glu_w[j], s5_glu_b[j])
            mixed = jnp.concatenate([o_a, o_b], axis=-1)
            x = matmul_residual(mixed.reshape(n, -1), ev_w_out[j], x.reshape(n, d)).reshape(bsz, l, d)
        else:
            o1 = 3 * C_WIDTH
            o2 = o1 + 3 * D_WIDTH
            o3 = o2 + 2 * D_HEADS
            o4 = o3 + 2 * D_HEADS
            w_in = od_w_in[j]
            pad = (-w_in.shape[1]) % 1280
            w_in = jnp.pad(w_in, ((0, 0), (0, pad)))
            proj = norm_matmul(x.reshape(n, d), mix_norm[layer], w_in, tn=1280)
            proj = proj.reshape(bsz, l, -1)
            q_c, k_c, v_c, qkv_d, a_d, b_d, g_d, _ = jnp.split(
                proj, [C_WIDTH, 2 * C_WIDTH, o1, o2, o3, o4, o4 + D_WIDTH], axis=-1)
            o_c = diff_attention(q_c, k_c, v_c, c_q_norm[j], c_k_norm[j], c_lambda[j], c_out_norm[j],
                                 rel_bias, layer)
            o_d = gated_deltanet(qkv_d, a_d, b_d, g_d, d_conv_w[j], d_a_log[j], d_dt_bias[j], d_out_norm[j])
            mixed = jnp.concatenate([o_c, o_d], axis=-1)
            x = matmul_residual(mixed.reshape(n, -1), od_w_out[j], x.reshape(n, d)).reshape(bsz, l, d)
        x = x + ec_moe(rmsnorm(x, ffn_norm[layer]), moe_router[layer], moe_w_gate[layer],
                       moe_w_up[layer], moe_w_down[layer])
    return x
```

```python
import functools
import math

import jax
import jax.numpy as jnp
from jax import lax
from jax.experimental import pallas as pl
from jax.experimental.pallas import tpu as pltpu

D_MODEL = 1024
DEPTH = 4
MIX_WIDTH = D_MODEL
A_WIDTH = MIX_WIDTH // 2
A_HEAD_DIM = 128
A_HEADS = A_WIDTH // A_HEAD_DIM
B_WIDTH = MIX_WIDTH - A_WIDTH
S5_GROUP = 16
S5_GROUPS = B_WIDTH // S5_GROUP
S5_STATE = 64
C_WIDTH = MIX_WIDTH // 2
C_HEAD_DIM = 64
C_HEADS = C_WIDTH // (2 * C_HEAD_DIM)
C_V_DIM = 2 * C_HEAD_DIM
D_WIDTH = MIX_WIDTH - C_WIDTH
D_HEAD_DIM = 128
D_HEADS = D_WIDTH // D_HEAD_DIM
CONV_WIDTH = 5
N_EXPERTS = 16
EXPERT_FF = 2 * D_MODEL
EC_CAPACITY_FACTOR = 2
REL_BUCKETS = 32
REL_MAX_DIST = 128
CHUNK = 64
Q_BLOCK = 128
EPS = 1e-6

VMEM_LIMIT_BYTES = 48 * 1024 * 1024


def _norm_matmul_kernel(x_ref, g_ref, w_ref, o_ref):
    x = x_ref[...]
    y = x * lax.rsqrt(jnp.mean(x * x, axis=-1, keepdims=True) + EPS) * g_ref[...]
    o_ref[...] = jnp.dot(y.astype(jnp.bfloat16), w_ref[...], preferred_element_type=jnp.float32)


def norm_matmul(x, gain, w, *, tm=512, tn=512):
    n, k = x.shape
    m = w.shape[1]
    tn = min(tn, m)
    assert n % tm == 0 and m % tn == 0
    return pl.pallas_call(
        _norm_matmul_kernel,
        grid=(n // tm, m // tn),
        in_specs=[pl.BlockSpec((tm, k), lambda i, j: (i, 0)),
                  pl.BlockSpec((1, k), lambda i, j: (0, 0)),
                  pl.BlockSpec((k, tn), lambda i, j: (0, j))],
        out_specs=pl.BlockSpec((tm, tn), lambda i, j: (i, j)),
        out_shape=jax.ShapeDtypeStruct((n, m), jnp.float32),
        compiler_params=pltpu.CompilerParams(dimension_semantics=("parallel", "parallel"),
                                             vmem_limit_bytes=VMEM_LIMIT_BYTES),
        name="norm_matmul",
    )(x, gain.reshape(1, k).astype(jnp.float32), w.astype(jnp.bfloat16))


def _matmul_res_kernel(a_ref, w_ref, r_ref, o_ref):
    o_ref[...] = r_ref[...] + jnp.dot(a_ref[...].astype(jnp.bfloat16), w_ref[...],
                                      preferred_element_type=jnp.float32)


def matmul_residual(a, w, res, *, tm=512):
    n, k = a.shape
    m = w.shape[1]
    return pl.pallas_call(
        _matmul_res_kernel,
        grid=(n // tm,),
        in_specs=[pl.BlockSpec((tm, k), lambda i: (i, 0)),
                  pl.BlockSpec((k, m), lambda i: (0, 0)),
                  pl.BlockSpec((tm, m), lambda i: (i, 0))],
        out_specs=pl.BlockSpec((tm, m), lambda i: (i, 0)),
        out_shape=jax.ShapeDtypeStruct((n, m), jnp.float32),
        compiler_params=pltpu.CompilerParams(dimension_semantics=("parallel",),
                                             vmem_limit_bytes=VMEM_LIMIT_BYTES),
        name="matmul_residual",
    )(a, w.astype(jnp.bfloat16), res)


def rmsnorm(x, g):
    xf = x.astype(jnp.float32)
    y = xf * lax.rsqrt(jnp.mean(xf * xf, axis=-1, keepdims=True) + EPS)
    return (y * g.astype(jnp.float32)).astype(x.dtype)


def l2norm(x):
    return x * lax.rsqrt(jnp.sum(x * x, axis=-1, keepdims=True) + EPS)


def _heads(t, n_heads):
    b, l, w = t.shape
    return jnp.transpose(t.reshape(b, l, n_heads, w // n_heads), (0, 2, 1, 3))


def _flip(t):
    return jnp.flip(t, axis=2)


def _to_chunks(t):
    b, h, l = t.shape[:3]
    t = t.reshape((b, h, l // CHUNK, CHUNK) + t.shape[3:])
    return jnp.moveaxis(t, 2, 0)


def _from_chunks(t):
    t = jnp.moveaxis(t, 0, 2)
    b, h, nc, c = t.shape[:4]
    return t.reshape((b, h, nc * c) + t.shape[4:])


def hgrn2_scan(q, k, v, log_f):
    tri = jnp.tril(jnp.ones((CHUNK, CHUNK), bool))[:, :, None]
    qc, kc, vc, lc = (_to_chunks(t) for t in (q, k, v, log_f))
    bc = jnp.cumsum(lc, axis=-2)

    def step(S, inp):
        q_, k_, v_, b_ = inp
        diff = b_[..., :, None, :] - b_[..., None, :, :]
        decay = jnp.where(tri, jnp.exp(jnp.where(tri, diff, 0.0)), 0.0)
        attn = jnp.einsum('bhtd,bhsd,bhtsd->bhts', q_, k_, decay)
        b_last = b_[..., -1, :]
        o = attn @ v_ + jnp.einsum('bhtd,bhdv->bhtv', q_ * jnp.exp(b_), S)
        S = jnp.exp(b_last)[..., None] * S + jnp.einsum(
            'bhsd,bhsv->bhdv', k_ * jnp.exp(b_last[..., None, :] - b_), v_)
        return S, o

    S0 = jnp.zeros(q.shape[:2] + (q.shape[-1], v.shape[-1]), jnp.float32)
    _, o = lax.scan(step, S0, (qc, kc, vc, bc))
    return _from_chunks(o)


def hgrn2_mixer(q_raw, f_fwd, f_bwd, i_raw, g_raw, lb, out_gain):
    f32 = jnp.float32
    b, l = q_raw.shape[:2]
    q = _heads(jax.nn.silu(q_raw.astype(f32)), A_HEADS)
    v = _heads(i_raw.astype(f32), A_HEADS)

    def gates(z, lbd):
        z = z.astype(f32)
        log_f = jnp.logaddexp(jnp.log(lbd), jnp.log1p(-lbd) + jax.nn.log_sigmoid(z))
        k = (1.0 - lbd) * jax.nn.sigmoid(-z)
        return _heads(log_f, A_HEADS), _heads(k, A_HEADS)

    lf_f, k_f = gates(f_fwd, lb[0])
    lf_b, k_b = gates(f_bwd, lb[1])
    o = hgrn2_scan(q, k_f, v, lf_f) + _flip(hgrn2_scan(_flip(q), _flip(k_b), _flip(v), _flip(lf_b)))
    o = rmsnorm(jnp.transpose(o, (0, 2, 1, 3)), out_gain).reshape(b, l, A_WIDTH)
    return o * jax.nn.silu(g_raw.astype(f32))


S5_NS = S5_GROUPS * S5_STATE
S5_TT = 64
SUBLANES = 8


def _s5_scan_kernel(u_ref, win_ref, ar_ref, ai_ref, wout_ref, y_ref, bu_sc, xs_sc, st_sc, *, bsz, tt, reverse):
    @pl.when(pl.program_id(0) == 0)
    def _():
        st_sc[...] = jnp.zeros_like(st_sc)

    bu_sc[...] = jnp.dot(u_ref[...].astype(jnp.bfloat16), win_ref[...], preferred_element_type=jnp.float32)
    ar = jnp.broadcast_to(ar_ref[...], (bsz, S5_NS))
    ai = jnp.broadcast_to(ai_ref[...], (bsz, S5_NS))
    per = SUBLANES // bsz
    ngroups = tt // per

    def body(s, carry):
        xr, xi = carry
        p = (ngroups - 1 - s) if reverse else s
        base = pl.multiple_of(p * SUBLANES, SUBLANES)
        blk = bu_sc[pl.ds(base, SUBLANES), :]
        outs_r = [None] * per
        outs_i = [None] * per
        for ph in (range(per - 1, -1, -1) if reverse else range(per)):
            br = blk[ph * bsz:(ph + 1) * bsz, :S5_NS]
            bi = blk[ph * bsz:(ph + 1) * bsz, S5_NS:]
            xr, xi = ar * xr - ai * xi + br, ar * xi + ai * xr + bi
            outs_r[ph] = xr
            outs_i[ph] = xi
        xs_sc[pl.ds(base, SUBLANES), :S5_NS] = jnp.concatenate(outs_r, axis=0)
        xs_sc[pl.ds(base, SUBLANES), S5_NS:] = jnp.concatenate(outs_i, axis=0)
        return xr, xi

    xr, xi = lax.fori_loop(0, ngroups, body, (st_sc[0], st_sc[1]))
    st_sc[0] = xr
    st_sc[1] = xi
    y_ref[...] = jnp.dot(xs_sc[...].astype(jnp.bfloat16), wout_ref[...], preferred_element_type=jnp.float32)


def s5_scan(u_tb, win, ar, ai, wout, *, bsz, reverse):
    n = u_tb.shape[0]
    rows = S5_TT * bsz
    nt = n // rows
    assert n % rows == 0 and SUBLANES % bsz == 0
    idx = (lambda i: (nt - 1 - i, 0)) if reverse else (lambda i: (i, 0))
    const = lambda i: (0, 0)
    return pl.pallas_call(
        functools.partial(_s5_scan_kernel, bsz=bsz, tt=S5_TT, reverse=reverse),
        grid=(nt,),
        in_specs=[pl.BlockSpec((rows, B_WIDTH), idx),
                  pl.BlockSpec((B_WIDTH, 2 * S5_NS), const),
                  pl.BlockSpec((1, S5_NS), const),
                  pl.BlockSpec((1, S5_NS), const),
                  pl.BlockSpec((2 * S5_NS, B_WIDTH), const)],
        out_specs=pl.BlockSpec((rows, B_WIDTH), idx),
        out_shape=jax.ShapeDtypeStruct((n, B_WIDTH), jnp.float32),
        scratch_shapes=[pltpu.VMEM((rows, 2 * S5_NS), jnp.float32),
                        pltpu.VMEM((rows, 2 * S5_NS), jnp.float32),
                        pltpu.VMEM((2, bsz, S5_NS), jnp.float32)],
        compiler_params=pltpu.CompilerParams(dimension_semantics=("arbitrary",),
                                             vmem_limit_bytes=VMEM_LIMIT_BYTES),
        name="s5_scan_bwd" if reverse else "s5_scan_fwd",
    )(u_tb, win, ar, ai, wout)


def _s5_final_kernel(u_ref, yf_ref, yb_ref, d_ref, w_ref, b_ref, o_ref):
    y = d_ref[...] * u_ref[...] + yf_ref[...] + yb_ref[...]
    y = jax.nn.gelu(y)
    z = jnp.dot(y.astype(jnp.bfloat16), w_ref[...], preferred_element_type=jnp.float32) + b_ref[...]
    o_ref[...] = y * jax.nn.sigmoid(z)


def s5_finalize(u, yf, yb, d_skip, glu_w, glu_b, *, tm=512):
    n, w = u.shape
    row = pl.BlockSpec((tm, w), lambda i: (i, 0))
    vec = pl.BlockSpec((1, w), lambda i: (0, 0))
    return pl.pallas_call(
        _s5_final_kernel,
        grid=(n // tm,),
        in_specs=[row, row, row, vec, pl.BlockSpec((w, w), lambda i: (0, 0)), vec],
        out_specs=row,
        out_shape=jax.ShapeDtypeStruct((n, w), jnp.float32),
        compiler_params=pltpu.CompilerParams(dimension_semantics=("parallel",),
                                             vmem_limit_bytes=VMEM_LIMIT_BYTES),
        name="s5_finalize",
    )(u, yf, yb, d_skip.reshape(1, w).astype(jnp.float32), glu_w.astype(jnp.bfloat16),
      glu_b.reshape(1, w).astype(jnp.float32))


def s5_direction_params(lam_re, lam_im, log_step, b_re, b_im, c_re, c_im):
    step = jnp.exp(log_step)[:, None]
    mag = jnp.exp(lam_re * step)
    abar_re = mag * jnp.cos(lam_im * step)
    abar_im = mag * jnp.sin(lam_im * step)
    den = lam_re * lam_re + lam_im * lam_im
    fr = ((abar_re - 1.0) * lam_re + abar_im * lam_im) / den
    fi = (abar_im * lam_re - (abar_re - 1.0) * lam_im) / den
    bb_re = fr[..., None] * b_re - fi[..., None] * b_im
    bb_im = fr[..., None] * b_im + fi[..., None] * b_re
    eye = jnp.eye(S5_GROUPS, dtype=jnp.float32)
    win = jnp.concatenate([jnp.einsum('gnp,gh->gphn', bb, eye).reshape(B_WIDTH, S5_NS) for bb in (bb_re, bb_im)],
                          axis=1)
    wout = jnp.concatenate([jnp.einsum('gpn,gh->hngp', c, eye).reshape(S5_NS, B_WIDTH) for c in (c_re, -c_im)],
                           axis=0)
    return (win.astype(jnp.bfloat16), abar_re.reshape(1, S5_NS), abar_im.reshape(1, S5_NS),
            wout.astype(jnp.bfloat16))


def s5_mixer_tb(u_tb, bsz, lam_re, lam_im, log_step, b_re, b_im, c_re, c_im, d_skip, glu_w, glu_b):
    f32 = jnp.float32
    ys = []
    for direction in range(2):
        prm = s5_direction_params(lam_re[direction].astype(f32), lam_im[direction].astype(f32),
                                  log_step[direction].astype(f32), b_re[direction].astype(f32),
                                  b_im[direction].astype(f32), c_re[direction].astype(f32),
                                  c_im[direction].astype(f32))
        ys.append(s5_scan(u_tb, *prm, bsz=bsz, reverse=(direction == 1)))
    return s5_finalize(u_tb, ys[0], ys[1], d_skip, glu_w, glu_b)


def t5_bucket(rel):
    half = REL_BUCKETS // 2
    max_exact = half // 2
    base = jnp.where(rel > 0, half, 0)
    n = jnp.abs(rel)
    nf = jnp.maximum(n, 1).astype(jnp.float32)
    large = max_exact + (jnp.log(nf / max_exact) / math.log(REL_MAX_DIST / max_exact)
                         * (half - max_exact)).astype(jnp.int32)
    large = jnp.minimum(large, half - 1)
    return base + jnp.where(n < max_exact, n, large)


ATT_T = 512


def rel_bias_tiles(rel_bias, t):
    assert t >= REL_MAX_DIST
    table = rel_bias.astype(jnp.float32)
    tiles = []
    for d in (-1, 0, 1):
        c = table[t5_bucket(d * t + jnp.arange(-(t - 1), t))]
        w = jnp.concatenate([c, c[:1]], axis=0)
        m = jnp.tile(w, (t, 1))[:t * (2 * t - 1)].reshape(t, 2 * t - 1, -1)
        tiles.append(m[:, t - 1:2 * t - 1])
    far_neg = jnp.broadcast_to(table[t5_bucket(jnp.array(-2 * t))], tiles[0].shape)
    far_pos = jnp.broadcast_to(table[t5_bucket(jnp.array(2 * t))], tiles[0].shape)
    out = jnp.stack([far_neg] + tiles + [far_pos], axis=0)
    return jnp.transpose(out, (3, 0, 1, 2))


def _attn_prep_kernel(q_ref, k_ref, v_ref, qg_ref, kg_ref, q2_ref, kt_ref, vb_ref):
    lane = lax.broadcasted_iota(jnp.int32, q_ref.shape, 1)
    lo = lane < C_HEAD_DIM

    def halfnorm(x, g):
        sq = x * x
        s_lo = jnp.sum(jnp.where(lo, sq, 0.0), axis=-1, keepdims=True)
        s_hi = jnp.sum(jnp.where(lo, 0.0, sq), axis=-1, keepdims=True)
        ms = jnp.where(lo, s_lo, s_hi) * (1.0 / C_HEAD_DIM)
        return x * lax.rsqrt(ms + EPS) * g

    qn = halfnorm(q_ref[...], qg_ref[...]) * (C_HEAD_DIM ** -0.5)
    kn = halfnorm(k_ref[...], kg_ref[...])
    q2_ref[0] = jnp.where(lo, qn, 0.0).astype(jnp.bfloat16)
    q2_ref[1] = jnp.where(lo, 0.0, qn).astype(jnp.bfloat16)
    kt_ref[...] = kn.T.astype(jnp.bfloat16)
    vb_ref[...] = v_ref[...].astype(jnp.bfloat16)


def attn_prep(proj3, q_gain, k_gain, *, tl=512):
    bsz, l, _ = proj3.shape
    hw = 2 * C_HEAD_DIM
    gq = jnp.tile(q_gain.astype(jnp.float32), 2).reshape(1, hw)
    gk = jnp.tile(k_gain.astype(jnp.float32), 2).reshape(1, hw)
    vec = pl.BlockSpec((1, hw), lambda b, h, i: (0, 0))
    return pl.pallas_call(
        _attn_prep_kernel,
        grid=(bsz, C_HEADS, l // tl),
        in_specs=[pl.BlockSpec((None, tl, hw), lambda b, h, i: (b, i, h)),
                  pl.BlockSpec((None, tl, hw), lambda b, h, i: (b, i, C_HEADS + h)),
                  pl.BlockSpec((None, tl, hw), lambda b, h, i: (b, i, 2 * C_HEADS + h)),
                  vec, vec],
        out_specs=[pl.BlockSpec((None, None, 2, tl, hw), lambda b, h, i: (b, h, 0, i, 0)),
                   pl.BlockSpec((None, None, hw, tl), lambda b, h, i: (b, h, 0, i)),
                   pl.BlockSpec((None, None, tl, hw), lambda b, h, i: (b, h, i, 0))],
        out_shape=[jax.ShapeDtypeStruct((bsz, C_HEADS, 2, l, hw), jnp.bfloat16),
                   jax.ShapeDtypeStruct((bsz, C_HEADS, hw, l), jnp.bfloat16),
                   jax.ShapeDtypeStruct((bsz, C_HEADS, l, hw), jnp.bfloat16)],
        compiler_params=pltpu.CompilerParams(dimension_semantics=("parallel", "parallel", "parallel"),
                                             vmem_limit_bytes=VMEM_LIMIT_BYTES),
        name="attn_prep",
    )(proj3, proj3, proj3, gq, gk)


def _attn_kernel(lam_ref, q2_ref, kt_ref, v_ref, bias_ref, g_ref, o_ref, m_sc, l_sc, acc_sc, *, t, nk, out_scale):
    qi = pl.program_id(2)
    q2 = q2_ref[...].reshape(2 * t, 2 * C_HEAD_DIM)
    m_sc[...] = jnp.full(m_sc.shape, -jnp.inf, jnp.float32)
    l_sc[...] = jnp.zeros_like(l_sc)
    acc_sc[...] = jnp.zeros_like(acc_sc)

    def body(ki, carry):
        off = pl.multiple_of(ki * t, t)
        s = jnp.dot(q2, kt_ref[:, pl.ds(off, t)], preferred_element_type=jnp.float32)
        bias = bias_ref[jnp.clip(ki - qi, -2, 2) + 2]
        s = (s.reshape(2, t, t) + bias[None]).reshape(2 * t, t)
        m_prev = m_sc[...]
        m_new = jnp.maximum(m_prev, jnp.max(s, axis=-1, keepdims=True))
        alpha = jnp.exp(m_prev - m_new)
        p = jnp.exp(s - m_new)
        l_sc[...] = alpha * l_sc[...] + jnp.sum(p, axis=-1, keepdims=True)
        acc_sc[...] = alpha * acc_sc[...] + jnp.dot(p.astype(jnp.bfloat16), v_ref[pl.ds(off, t), :],
                                                    preferred_element_type=jnp.float32)
        m_sc[...] = m_new
        return carry

    lax.fori_loop(0, nk, body, 0)
    a = acc_sc[...] / l_sc[...]
    o = a[:t] - lam_ref[0] * a[t:]
    y = o * lax.rsqrt(jnp.mean(o * o, axis=-1, keepdims=True) + EPS)
    o_ref[...] = y * g_ref[...] * out_scale


def diff_attention(proj3, q_gain, k_gain, lam, out_gain, bias5, layer_idx):
    f32 = jnp.float32
    bsz, l, _ = proj3.shape
    t = ATT_T
    hw = 2 * C_HEAD_DIM
    lam_init = 0.8 - 0.6 * math.exp(-0.3 * layer_idx)
    lam_f = lam.astype(f32)
    lam_full = jnp.exp(jnp.sum(lam_f[0] * lam_f[1])) - jnp.exp(jnp.sum(lam_f[2] * lam_f[3])) + lam_init
    q2, kt, vb = attn_prep(proj3, q_gain, k_gain)
    return pl.pallas_call(
        functools.partial(_attn_kernel, t=t, nk=l // t, out_scale=1.0 - lam_init),
        grid=(bsz, C_HEADS, l // t),
        in_specs=[pl.BlockSpec(memory_space=pltpu.SMEM),
                  pl.BlockSpec((None, None, 2, t, hw), lambda b, h, i: (b, h, 0, i, 0)),
                  pl.BlockSpec((None, None, hw, l), lambda b, h, i: (b, h, 0, 0)),
                  pl.BlockSpec((None, None, l, hw), lambda b, h, i: (b, h, 0, 0)),
                  pl.BlockSpec((None, 5, t, t), lambda b, h, i: (h, 0, 0, 0)),
                  pl.BlockSpec((1, hw), lambda b, h, i: (0, 0))],
        out_specs=pl.BlockSpec((None, t, hw), lambda b, h, i: (b, i, h)),
        out_shape=jax.ShapeDtypeStruct((bsz, l, C_WIDTH), f32),
        scratch_shapes=[pltpu.VMEM((2 * t, 1), f32), pltpu.VMEM((2 * t, 1), f32), pltpu.VMEM((2 * t, hw), f32)],
        compiler_params=pltpu.CompilerParams(dimension_semantics=("parallel", "parallel", "arbitrary"),
                                             vmem_limit_bytes=VMEM_LIMIT_BYTES),
        name="diff_attention",
    )(lam_full.reshape(1), q2, kt, vb, bias5, out_gain.reshape(1, hw).astype(f32))


def short_conv(x, w):
    ch = x.shape[-1]
    pad = CONV_WIDTH // 2
    return lax.conv_general_dilated(x, w[:, None, :].astype(x.dtype), window_strides=(1,),
                                    padding=[(pad, pad)], dimension_numbers=('NWC', 'WIO', 'NWC'),
                                    feature_group_count=ch)


def gdn_scan(q, k, v, g, beta):
    b, h, l, dk = q.shape
    dv = v.shape[-1]
    nc = l // CHUNK

    def chunks(t):
        return t.reshape((b, h, nc, CHUNK) + t.shape[3:])

    q, k, v, g, beta = (chunks(t) for t in (q, k, v, g, beta))
    gam = jnp.cumsum(g, axis=-1)
    incl = jnp.tril(jnp.ones((CHUNK, CHUNK), bool))
    strict = jnp.tril(jnp.ones((CHUNK, CHUNK), bool), -1)
    diff = gam[..., :, None] - gam[..., None, :]
    decay = jnp.where(incl, jnp.exp(jnp.where(incl, diff, 0.0)), 0.0)
    k_beta = k * beta[..., None]
    v_beta = v * beta[..., None]
    a = jnp.where(strict, jnp.einsum('bhntd,bhnsd->bhnts', k_beta, k) * decay, 0.0)
    m = a + jnp.eye(CHUNK, dtype=a.dtype)
    value = lax.linalg.triangular_solve(m, v_beta, left_side=True, lower=True, unit_diagonal=True)
    k_cum = lax.linalg.triangular_solve(m, k_beta * jnp.exp(gam)[..., None], left_side=True,
                                        lower=True, unit_diagonal=True)
    attn = jnp.einsum('bhntd,bhnsd->bhnts', q, k) * decay
    q_dec = q * jnp.exp(gam)[..., None]
    g_last = gam[..., -1]
    k_dec = k * jnp.exp(g_last[..., None] - gam)[..., None]
    xs = tuple(jnp.moveaxis(t, 2, 0) for t in (value, k_cum, attn, q_dec, k_dec, g_last))

    def step(S, inp):
        val, kc, at, qd, kd, gl = inp
        v_new = val - kc @ S
        o = qd @ S + at @ v_new
        S = jnp.exp(gl)[..., None, None] * S + jnp.einsum('bhsd,bhsv->bhdv', kd, v_new)
        return S, o

    S0 = jnp.zeros((b, h, dk, dv), jnp.float32)
    _, o = lax.scan(step, S0, xs)
    return jnp.moveaxis(o, 0, 2).reshape(b, h, l, dv)


def gated_deltanet(qkv_raw, a_raw, b_raw, g_raw, conv_w, a_log, dt_bias, out_gain):
    f32 = jnp.float32
    bsz, l = qkv_raw.shape[:2]
    qkv = jax.nn.silu(short_conv(qkv_raw, conv_w).astype(f32))
    q, k, v = jnp.split(qkv, 3, axis=-1)
    q = l2norm(_heads(q, D_HEADS)) * (D_HEAD_DIM ** -0.5)
    k = l2norm(_heads(k, D_HEADS))
    v = _heads(v, D_HEADS)
    a = a_raw.astype(f32).reshape(bsz, l, 2, D_HEADS)
    bb = b_raw.astype(f32).reshape(bsz, l, 2, D_HEADS)
    a_log = a_log.astype(f32)
    dt_bias = dt_bias.astype(f32)
    o = jnp.zeros((bsz, D_HEADS, l, D_HEAD_DIM), f32)
    for direction in range(2):
        g = -jnp.exp(a_log[direction]) * jax.nn.softplus(a[:, :, direction] + dt_bias[direction])
        beta = jax.nn.sigmoid(bb[:, :, direction])
        g = jnp.transpose(g, (0, 2, 1))
        beta = jnp.transpose(beta, (0, 2, 1))
        if direction == 0:
            o = o + gdn_scan(q, k, v, g, beta)
        else:
            o = o + _flip(gdn_scan(_flip(q), _flip(k), _flip(v), _flip(g), _flip(beta)))
    o = rmsnorm(jnp.transpose(o, (0, 2, 1, 3)), out_gain).reshape(bsz, l, D_WIDTH)
    return o * jax.nn.silu(g_raw.astype(f32))


def ec_moe(h, w_router, w_gate, w_up, w_down):
    b, l, d = h.shape
    cap = EC_CAPACITY_FACTOR * l // N_EXPERTS
    logits = jnp.einsum('bld,de->ble', h, w_router).astype(jnp.float32)
    aff = jax.nn.softmax(logits, axis=-1)
    gate, idx = lax.top_k(jnp.swapaxes(aff, 1, 2), cap)
    xs = jax.vmap(lambda hb, ib: hb[ib])(h, idx)
    hid = jax.nn.silu(jnp.einsum('becd,edf->becf', xs, w_gate)) * jnp.einsum('becd,edf->becf', xs, w_up)
    out = jnp.einsum('becf,efd->becd', hid, w_down) * gate[..., None].astype(h.dtype)
    return jax.vmap(lambda ob, ib: jnp.zeros((l, d), ob.dtype).at[ib.reshape(-1)].add(ob.reshape(-1, d)))(out, idx)


def kernel(x, mix_norm, ffn_norm, ev_w_in, ev_w_out, a_lb_logits, a_out_norm, s5_lambda_re, s5_lambda_im, s5_log_step, s5_b_re, s5_b_im, s5_c_re, s5_c_im, s5_d, s5_glu_w, s5_glu_b, od_w_in, od_w_out, c_q_norm, c_k_norm, c_lambda, c_out_norm, rel_bias, d_conv_w, d_a_log, d_dt_bias, d_out_norm, moe_router, moe_w_gate, moe_w_up, moe_w_down):
    bsz, l, d = x.shape
    n = bsz * l
    p = jax.nn.softmax(a_lb_logits.astype(jnp.float32), axis=0)
    cum = jnp.cumsum(p, axis=0)
    lower_bounds = cum - cum[0:1]
    bias5 = rel_bias_tiles(rel_bias, ATT_T)
    for layer in range(DEPTH):
        j = layer // 2
        if layer % 2 == 0:
            proj = norm_matmul(x.reshape(n, d), mix_norm[layer], ev_w_in[j]).reshape(bsz, l, -1)
            q_a, f_fw, f_bw, i_a, g_a, u_b = jnp.split(
                proj, [A_WIDTH, 2 * A_WIDTH, 3 * A_WIDTH, 4 * A_WIDTH, 5 * A_WIDTH], axis=-1)
            o_a = hgrn2_mixer(q_a, f_fw, f_bw, i_a, g_a, lower_bounds[j], a_out_norm[j])
            u_tb = jnp.transpose(u_b, (1, 0, 2)).reshape(l * bsz, B_WIDTH)
            o_b = s5_mixer_tb(u_tb, bsz, s5_lambda_re[j], s5_lambda_im[j], s5_log_step[j], s5_b_re[j], s5_b_im[j],
                              s5_c_re[j], s5_c_im[j], s5_d[j], s5_glu_w[j], s5_glu_b[j])
            o_b = jnp.transpose(o_b.reshape(l, bsz, B_WIDTH), (1, 0, 2))
            mixed = jnp.concatenate([o_a, o_b], axis=-1)
            x = matmul_residual(mixed.reshape(n, -1), ev_w_out[j], x.reshape(n, d)).reshape(bsz, l, d)
        else:
            o1 = 3 * C_WIDTH
            o2 = o1 + 3 * D_WIDTH
            o3 = o2 + 2 * D_HEADS
            o4 = o3 + 2 * D_HEADS
            w_in = od_w_in[j]
            pad = (-w_in.shape[1]) % 1280
            w_in = jnp.pad(w_in, ((0, 0), (0, pad)))
            proj = norm_matmul(x.reshape(n, d), mix_norm[layer], w_in, tn=1280)
            proj = proj.reshape(bsz, l, -1)
            q_c, k_c, v_c, qkv_d, a_d, b_d, g_d, _ = jnp.split(
                proj, [C_WIDTH, 2 * C_WIDTH, o1, o2, o3, o4, o4 + D_WIDTH], axis=-1)
            o_c = diff_attention(proj, c_q_norm[j], c_k_norm[j], c_lambda[j], c_out_norm[j], bias5, layer)
            o_d = gated_deltanet(qkv_d, a_d, b_d, g_d, d_conv_w[j], d_a_log[j], d_dt_bias[j], d_out_norm[j])
            mixed = jnp.concatenate([o_c, o_d], axis=-1)
            x = matmul_residual(mixed.reshape(n, -1), od_w_out[j], x.reshape(n, d)).reshape(bsz, l, d)
        x = x + ec_moe(rmsnorm(x, ffn_norm[layer]), moe_router[layer], moe_w_gate[layer],
                       moe_w_up[layer], moe_w_down[layer])
    return x
```

```python
import functools
import math

import jax
import jax.numpy as jnp
from jax import lax
from jax.experimental import pallas as pl
from jax.experimental.pallas import tpu as pltpu

D_MODEL = 1024
DEPTH = 4
MIX_WIDTH = D_MODEL
A_WIDTH = MIX_WIDTH // 2
A_HEAD_DIM = 128
A_HEADS = A_WIDTH // A_HEAD_DIM
B_WIDTH = MIX_WIDTH - A_WIDTH
S5_GROUP = 16
S5_GROUPS = B_WIDTH // S5_GROUP
S5_STATE = 64
C_WIDTH = MIX_WIDTH // 2
C_HEAD_DIM = 64
C_HEADS = C_WIDTH // (2 * C_HEAD_DIM)
C_V_DIM = 2 * C_HEAD_DIM
D_WIDTH = MIX_WIDTH - C_WIDTH
D_HEAD_DIM = 128
D_HEADS = D_WIDTH // D_HEAD_DIM
CONV_WIDTH = 5
N_EXPERTS = 16
EXPERT_FF = 2 * D_MODEL
EC_CAPACITY_FACTOR = 2
REL_BUCKETS = 32
REL_MAX_DIST = 128
CHUNK = 64
Q_BLOCK = 128
EPS = 1e-6

VMEM_LIMIT_BYTES = 48 * 1024 * 1024


def _norm_matmul_kernel(x_ref, g_ref, w_ref, o_ref):
    x = x_ref[...]
    y = x * lax.rsqrt(jnp.mean(x * x, axis=-1, keepdims=True) + EPS) * g_ref[...]
    o_ref[...] = jnp.dot(y.astype(jnp.bfloat16), w_ref[...], preferred_element_type=jnp.float32)


def norm_matmul(x, gain, w, *, tm=512, tn=512):
    n, k = x.shape
    m = w.shape[1]
    tn = min(tn, m)
    assert n % tm == 0 and m % tn == 0
    return pl.pallas_call(
        _norm_matmul_kernel,
        grid=(n // tm, m // tn),
        in_specs=[pl.BlockSpec((tm, k), lambda i, j: (i, 0)),
                  pl.BlockSpec((1, k), lambda i, j: (0, 0)),
                  pl.BlockSpec((k, tn), lambda i, j: (0, j))],
        out_specs=pl.BlockSpec((tm, tn), lambda i, j: (i, j)),
        out_shape=jax.ShapeDtypeStruct((n, m), jnp.float32),
        compiler_params=pltpu.CompilerParams(dimension_semantics=("parallel", "parallel"),
                                             vmem_limit_bytes=VMEM_LIMIT_BYTES),
        name="norm_matmul",
    )(x, gain.reshape(1, k).astype(jnp.float32), w.astype(jnp.bfloat16))


def _matmul_res_kernel(a_ref, w_ref, r_ref, o_ref):
    o_ref[...] = r_ref[...] + jnp.dot(a_ref[...].astype(jnp.bfloat16), w_ref[...],
                                      preferred_element_type=jnp.float32)


def matmul_residual(a, w, res, *, tm=512):
    n, k = a.shape
    m = w.shape[1]
    return pl.pallas_call(
        _matmul_res_kernel,
        grid=(n // tm,),
        in_specs=[pl.BlockSpec((tm, k), lambda i: (i, 0)),
                  pl.BlockSpec((k, m), lambda i: (0, 0)),
                  pl.BlockSpec((tm, m), lambda i: (i, 0))],
        out_specs=pl.BlockSpec((tm, m), lambda i: (i, 0)),
        out_shape=jax.ShapeDtypeStruct((n, m), jnp.float32),
        compiler_params=pltpu.CompilerParams(dimension_semantics=("parallel",),
                                             vmem_limit_bytes=VMEM_LIMIT_BYTES),
        name="matmul_residual",
    )(a, w.astype(jnp.bfloat16), res)


def rmsnorm(x, g):
    xf = x.astype(jnp.float32)
    y = xf * lax.rsqrt(jnp.mean(xf * xf, axis=-1, keepdims=True) + EPS)
    return (y * g.astype(jnp.float32)).astype(x.dtype)


def l2norm(x):
    return x * lax.rsqrt(jnp.sum(x * x, axis=-1, keepdims=True) + EPS)


def _heads(t, n_heads):
    b, l, w = t.shape
    return jnp.transpose(t.reshape(b, l, n_heads, w // n_heads), (0, 2, 1, 3))


def _flip(t):
    return jnp.flip(t, axis=2)


def _to_chunks(t):
    b, h, l = t.shape[:3]
    t = t.reshape((b, h, l // CHUNK, CHUNK) + t.shape[3:])
    return jnp.moveaxis(t, 2, 0)


def _from_chunks(t):
    t = jnp.moveaxis(t, 0, 2)
    b, h, nc, c = t.shape[:4]
    return t.reshape((b, h, nc * c) + t.shape[4:])


HG_LEVELS = tuple(CHUNK >> (i + 1) for i in range(CHUNK.bit_length() - 1))
HG_TOT_ROWS = 8
HG_TT = 512


def hgrn2_constants():
    import numpy as np
    c = CHUNK
    r = np.arange(c)[:, None]
    u = np.arange(c)[None, :]
    stacks, masks = [], []
    for direction in range(2):
        fwd = direction == 0
        lvl_masks = []
        for m in HG_LEVELS:
            blk = r // (2 * m)
            later = (r % (2 * m)) >= m
            lvl_masks.append((blk == blk.T) & (later & ~later.T if fwd else ~later & later.T))
        stacks.append(np.concatenate([(u <= r) if fwd else (u >= r), np.ones((HG_TOT_ROWS, c), bool)], axis=0))
        masks.append(np.stack(lvl_masks))
    return (jnp.asarray(np.stack(stacks), jnp.bfloat16), jnp.asarray(np.stack(masks), jnp.float32))


def _hgrn2_kernel(q_ref, f_ref, v_ref, loglb_ref, log1mlb_ref, onemlb_ref, ast_ref, mask_ref, o_ref,
                  st_sc, qd_sc, dec_sc, upd_sc, sin_sc, *, nc):
    direction = pl.program_id(2)

    @pl.when(pl.program_id(3) == 0)
    def _():
        st_sc[...] = jnp.zeros_like(st_sc)

    bf16 = jnp.bfloat16
    f32 = jnp.float32
    c = CHUNK
    hd = A_HEAD_DIM
    dirf = direction.astype(f32)
    loglb = loglb_ref[...]
    log1mlb = log1mlb_ref[...]
    onemlb = onemlb_ref[...]
    ast = ast_ref[...]
    contract_last = (((1,), (1,)), ((), ()))
    contract_first = (((0,), (0,)), ((), ()))

    for n in range(nc):
        rows = slice(n * c, (n + 1) * c)
        z = f_ref[rows, :]
        v = v_ref[rows, :]
        qr = q_ref[rows, :]
        q = qr * jax.nn.sigmoid(qr)
        e = jnp.exp(-jnp.abs(z))
        lsig = jnp.minimum(z, 0.0) - jnp.log1p(e)
        cc = log1mlb + lsig
        lf = jnp.maximum(loglb, cc) + jnp.log1p(jnp.exp(-jnp.abs(loglb - cc)))
        k = onemlb * jnp.where(z >= 0, e, 1.0) / (1.0 + e)
        hi = lf.astype(bf16)
        lo = (lf - hi.astype(f32)).astype(bf16)
        d = jnp.dot(ast, hi, preferred_element_type=f32) + jnp.dot(ast, lo, preferred_element_type=f32)
        cum = d[0:c]
        tot = d[c:c + HG_TOT_ROWS]
        rem = tot[0:1] - cum
        ref = cum - dirf * lf
        attn = jnp.zeros((c, c), f32)
        for li, m in enumerate(HG_LEVELS):
            nb = c // (2 * m)
            split = jnp.broadcast_to(ref.reshape(nb, 2 * m, hd)[:, m - 1:m, :], (nb, 2 * m, hd)).reshape(c, hd)
            x = jnp.exp(-jnp.abs(cum - split))
            s = lax.dot_general((q * x).astype(bf16), (k * x).astype(bf16), contract_last,
                                preferred_element_type=f32)
            attn = attn + mask_ref[li] * s
        vb = v.astype(bf16)
        o_ref[rows, :] = (jnp.dot(attn.astype(bf16), vb, preferred_element_type=f32)
                          + jnp.sum(q * k, axis=-1, keepdims=True) * v)
        qd_sc[n] = (q * jnp.exp(cum)).astype(bf16)
        dec_sc[n] = jnp.exp(tot)
        upd_sc[n] = lax.dot_general(vb, (k * jnp.exp(rem)).astype(bf16), contract_first, preferred_element_type=f32)

    def body(ci, st):
        ce = ci + direction * (nc - 1 - 2 * ci)
        sin_sc[ce] = st.astype(bf16)
        return st * dec_sc[ce][0:1] + upd_sc[ce]

    st_sc[...] = lax.fori_loop(0, nc, body, st_sc[...])

    for n in range(nc):
        rows = slice(n * c, (n + 1) * c)
        o_ref[rows, :] += lax.dot_general(qd_sc[n], sin_sc[n], contract_last, preferred_element_type=f32)


def hgrn2_scan(proj3, lb):
    bsz, l, _ = proj3.shape
    hd = A_HEAD_DIM
    tt = min(HG_TT, l)
    nt = l // tt
    assert l % tt == 0 and tt % CHUNK == 0
    ast, masks = hgrn2_constants()
    lb = lb.astype(jnp.float32)
    vecs = [jnp.log(lb).reshape(2, 1, A_WIDTH), jnp.log1p(-lb).reshape(2, 1, A_WIDTH), (1.0 - lb).reshape(2, 1, A_WIDTH)]
    tidx = lambda d, i: i + d * (nt - 1 - 2 * i)
    vec = pl.BlockSpec((None, 1, hd), lambda b, h, d, i: (d, 0, h))
    return pl.pallas_call(
        functools.partial(_hgrn2_kernel, nc=tt // CHUNK),
        grid=(bsz, A_HEADS, 2, nt),
        in_specs=[pl.BlockSpec((None, tt, hd), lambda b, h, d, i: (b, tidx(d, i), h)),
                  pl.BlockSpec((None, tt, hd), lambda b, h, d, i: (b, tidx(d, i), (1 + d) * A_HEADS + h)),
                  pl.BlockSpec((None, tt, hd), lambda b, h, d, i: (b, tidx(d, i), 3 * A_HEADS + h)),
                  vec, vec, vec,
                  pl.BlockSpec((None,) + ast.shape[1:], lambda b, h, d, i: (d, 0, 0)),
                  pl.BlockSpec((None,) + masks.shape[1:], lambda b, h, d, i: (d, 0, 0, 0))],
        out_specs=pl.BlockSpec((None, None, tt, hd), lambda b, h, d, i: (d, b, tidx(d, i), h)),
        out_shape=jax.ShapeDtypeStruct((2, bsz, l, A_WIDTH), jnp.float32),
        scratch_shapes=[pltpu.VMEM((hd, hd), jnp.float32),
                        pltpu.VMEM((tt // CHUNK, CHUNK, hd), jnp.bfloat16),
                        pltpu.VMEM((tt // CHUNK, HG_TOT_ROWS, hd), jnp.float32),
                        pltpu.VMEM((tt // CHUNK, hd, hd), jnp.float32),
                        pltpu.VMEM((tt // CHUNK, hd, hd), jnp.bfloat16)],
        compiler_params=pltpu.CompilerParams(dimension_semantics=("parallel", "parallel", "parallel", "arbitrary"),
                                             vmem_limit_bytes=VMEM_LIMIT_BYTES),
        name="hgrn2_scan",
    )(proj3, proj3, proj3, *vecs, ast, masks)


def _hgrn2_final_kernel(of_ref, ob_ref, g_ref, gain_ref, o_ref):
    o = of_ref[...] + ob_ref[...]
    y = o * lax.rsqrt(jnp.mean(o * o, axis=-1, keepdims=True) + EPS) * gain_ref[...]
    g = g_ref[...]
    o_ref[...] = y * (g * jax.nn.sigmoid(g))


def hgrn2_mixer(proj3, lb, out_gain, *, tm=1024):
    bsz, l, _ = proj3.shape
    hd = A_HEAD_DIM
    o2 = hgrn2_scan(proj3, lb)
    tm = min(tm, l)
    return pl.pallas_call(
        _hgrn2_final_kernel,
        grid=(bsz, l // tm, A_HEADS),
        in_specs=[pl.BlockSpec((None, None, tm, hd), lambda b, i, h: (0, b, i, h)),
                  pl.BlockSpec((None, None, tm, hd), lambda b, i, h: (1, b, i, h)),
                  pl.BlockSpec((None, tm, hd), lambda b, i, h: (b, i, 4 * A_HEADS + h)),
                  pl.BlockSpec((1, hd), lambda b, i, h: (0, 0))],
        out_specs=pl.BlockSpec((None, tm, hd), lambda b, i, h: (b, i, h)),
        out_shape=jax.ShapeDtypeStruct((bsz, l, A_WIDTH), jnp.float32),
        compiler_params=pltpu.CompilerParams(dimension_semantics=("parallel", "parallel", "parallel"),
                                             vmem_limit_bytes=VMEM_LIMIT_BYTES),
        name="hgrn2_finalize",
    )(o2, o2, proj3, out_gain.reshape(1, hd).astype(jnp.float32))


S5_NS = S5_GROUPS * S5_STATE
S5_TT = 64
SUBLANES = 8


def _s5_scan_kernel(u_ref, win_ref, ar_ref, ai_ref, wout_ref, y_ref, bu_sc, xs_sc, st_sc, *, bsz, tt, reverse):
    @pl.when(pl.program_id(0) == 0)
    def _():
        st_sc[...] = jnp.zeros_like(st_sc)

    bu_sc[...] = jnp.dot(u_ref[...].astype(jnp.bfloat16), win_ref[...], preferred_element_type=jnp.float32)
    ar = jnp.broadcast_to(ar_ref[...], (bsz, S5_NS))
    ai = jnp.broadcast_to(ai_ref[...], (bsz, S5_NS))
    per = SUBLANES // bsz
    ngroups = tt // per

    def body(s, carry):
        xr, xi = carry
        p = (ngroups - 1 - s) if reverse else s
        base = pl.multiple_of(p * SUBLANES, SUBLANES)
        blk = bu_sc[pl.ds(base, SUBLANES), :]
        outs_r = [None] * per
        outs_i = [None] * per
        for ph in (range(per - 1, -1, -1) if reverse else range(per)):
            br = blk[ph * bsz:(ph + 1) * bsz, :S5_NS]
            bi = blk[ph * bsz:(ph + 1) * bsz, S5_NS:]
            xr, xi = ar * xr - ai * xi + br, ar * xi + ai * xr + bi
            outs_r[ph] = xr
            outs_i[ph] = xi
        xs_sc[pl.ds(base, SUBLANES), :S5_NS] = jnp.concatenate(outs_r, axis=0)
        xs_sc[pl.ds(base, SUBLANES), S5_NS:] = jnp.concatenate(outs_i, axis=0)
        return xr, xi

    xr, xi = lax.fori_loop(0, ngroups, body, (st_sc[0], st_sc[1]))
    st_sc[0] = xr
    st_sc[1] = xi
    y_ref[...] = jnp.dot(xs_sc[...].astype(jnp.bfloat16), wout_ref[...], preferred_element_type=jnp.float32)


def s5_scan(u_tb, win, ar, ai, wout, *, bsz, reverse):
    n = u_tb.shape[0]
    rows = S5_TT * bsz
    nt = n // rows
    assert n % rows == 0 and SUBLANES % bsz == 0
    idx = (lambda i: (nt - 1 - i, 0)) if reverse else (lambda i: (i, 0))
    const = lambda i: (0, 0)
    return pl.pallas_call(
        functools.partial(_s5_scan_kernel, bsz=bsz, tt=S5_TT, reverse=reverse),
        grid=(nt,),
        in_specs=[pl.BlockSpec((rows, B_WIDTH), idx),
                  pl.BlockSpec((B_WIDTH, 2 * S5_NS), const),
                  pl.BlockSpec((1, S5_NS), const),
                  pl.BlockSpec((1, S5_NS), const),
                  pl.BlockSpec((2 * S5_NS, B_WIDTH), const)],
        out_specs=pl.BlockSpec((rows, B_WIDTH), idx),
        out_shape=jax.ShapeDtypeStruct((n, B_WIDTH), jnp.float32),
        scratch_shapes=[pltpu.VMEM((rows, 2 * S5_NS), jnp.float32),
                        pltpu.VMEM((rows, 2 * S5_NS), jnp.float32),
                        pltpu.VMEM((2, bsz, S5_NS), jnp.float32)],
        compiler_params=pltpu.CompilerParams(dimension_semantics=("arbitrary",),
                                             vmem_limit_bytes=VMEM_LIMIT_BYTES),
        name="s5_scan_bwd" if reverse else "s5_scan_fwd",
    )(u_tb, win, ar, ai, wout)


def _s5_final_kernel(u_ref, yf_ref, yb_ref, d_ref, w_ref, b_ref, o_ref):
    y = d_ref[...] * u_ref[...] + yf_ref[...] + yb_ref[...]
    y = jax.nn.gelu(y)
    z = jnp.dot(y.astype(jnp.bfloat16), w_ref[...], preferred_element_type=jnp.float32) + b_ref[...]
    o_ref[...] = y * jax.nn.sigmoid(z)


def s5_finalize(u, yf, yb, d_skip, glu_w, glu_b, *, tm=512):
    n, w = u.shape
    row = pl.BlockSpec((tm, w), lambda i: (i, 0))
    vec = pl.BlockSpec((1, w), lambda i: (0, 0))
    return pl.pallas_call(
        _s5_final_kernel,
        grid=(n // tm,),
        in_specs=[row, row, row, vec, pl.BlockSpec((w, w), lambda i: (0, 0)), vec],
        out_specs=row,
        out_shape=jax.ShapeDtypeStruct((n, w), jnp.float32),
        compiler_params=pltpu.CompilerParams(dimension_semantics=("parallel",),
                                             vmem_limit_bytes=VMEM_LIMIT_BYTES),
        name="s5_finalize",
    )(u, yf, yb, d_skip.reshape(1, w).astype(jnp.float32), glu_w.astype(jnp.bfloat16),
      glu_b.reshape(1, w).astype(jnp.float32))


def s5_direction_params(lam_re, lam_im, log_step, b_re, b_im, c_re, c_im):
    step = jnp.exp(log_step)[:, None]
    mag = jnp.exp(lam_re * step)
    abar_re = mag * jnp.cos(lam_im * step)
    abar_im = mag * jnp.sin(lam_im * step)
    den = lam_re * lam_re + lam_im * lam_im
    fr = ((abar_re - 1.0) * lam_re + abar_im * lam_im) / den
    fi = (abar_im * lam_re - (abar_re - 1.0) * lam_im) / den
    bb_re = fr[..., None] * b_re - fi[..., None] * b_im
    bb_im = fr[..., None] * b_im + fi[..., None] * b_re
    eye = jnp.eye(S5_GROUPS, dtype=jnp.float32)
    win = jnp.concatenate([jnp.einsum('gnp,gh->gphn', bb, eye).reshape(B_WIDTH, S5_NS) for bb in (bb_re, bb_im)],
                          axis=1)
    wout = jnp.concatenate([jnp.einsum('gpn,gh->hngp', c, eye).reshape(S5_NS, B_WIDTH) for c in (c_re, -c_im)],
                           axis=0)
    return (win.astype(jnp.bfloat16), abar_re.reshape(1, S5_NS), abar_im.reshape(1, S5_NS),
            wout.astype(jnp.bfloat16))


def s5_mixer_tb(u_tb, bsz, lam_re, lam_im, log_step, b_re, b_im, c_re, c_im, d_skip, glu_w, glu_b):
    f32 = jnp.float32
    ys = []
    for direction in range(2):
        prm = s5_direction_params(lam_re[direction].astype(f32), lam_im[direction].astype(f32),
                                  log_step[direction].astype(f32), b_re[direction].astype(f32),
                                  b_im[direction].astype(f32), c_re[direction].astype(f32),
                                  c_im[direction].astype(f32))
        ys.append(s5_scan(u_tb, *prm, bsz=bsz, reverse=(direction == 1)))
    return s5_finalize(u_tb, ys[0], ys[1], d_skip, glu_w, glu_b)


def t5_bucket(rel):
    half = REL_BUCKETS // 2
    max_exact = half // 2
    base = jnp.where(rel > 0, half, 0)
    n = jnp.abs(rel)
    nf = jnp.maximum(n, 1).astype(jnp.float32)
    large = max_exact + (jnp.log(nf / max_exact) / math.log(REL_MAX_DIST / max_exact)
                         * (half - max_exact)).astype(jnp.int32)
    large = jnp.minimum(large, half - 1)
    return base + jnp.where(n < max_exact, n, large)


ATT_T = 512
LOG2E = math.log2(math.e)


def rel_bias_tiles(rel_bias, t):
    assert t >= REL_MAX_DIST
    table = rel_bias.astype(jnp.float32) * LOG2E
    tiles = []
    for d in (-1, 0, 1):
        c = table[t5_bucket(d * t + jnp.arange(-(t - 1), t))]
        w = jnp.concatenate([c, c[:1]], axis=0)
        m = jnp.tile(w, (t, 1))[:t * (2 * t - 1)].reshape(t, 2 * t - 1, -1)
        tiles.append(m[:, t - 1:2 * t - 1])
    far_neg = jnp.broadcast_to(table[t5_bucket(jnp.array(-2 * t))], tiles[0].shape)
    far_pos = jnp.broadcast_to(table[t5_bucket(jnp.array(2 * t))], tiles[0].shape)
    out = jnp.stack([far_neg] + tiles + [far_pos], axis=0)
    return jnp.transpose(out, (3, 0, 1, 2))


def _attn_prep_kernel(q_ref, k_ref, v_ref, qg_ref, kg_ref, q2_ref, kt_ref, vb_ref):
    lane = lax.broadcasted_iota(jnp.int32, q_ref.shape, 1)
    lo = lane < C_HEAD_DIM

    def halfnorm(x, g):
        sq = x * x
        s_lo = jnp.sum(jnp.where(lo, sq, 0.0), axis=-1, keepdims=True)
        s_hi = jnp.sum(jnp.where(lo, 0.0, sq), axis=-1, keepdims=True)
        ms = jnp.where(lo, s_lo, s_hi) * (1.0 / C_HEAD_DIM)
        return x * lax.rsqrt(ms + EPS) * g

    qn = halfnorm(q_ref[...], qg_ref[...]) * (C_HEAD_DIM ** -0.5 * LOG2E)
    kn = halfnorm(k_ref[...], kg_ref[...])
    q2_ref[0] = jnp.where(lo, qn, 0.0).astype(jnp.bfloat16)
    q2_ref[1] = jnp.where(lo, 0.0, qn).astype(jnp.bfloat16)
    kt_ref[...] = kn.T.astype(jnp.bfloat16)
    vb_ref[...] = v_ref[...].astype(jnp.bfloat16)


def attn_prep(proj3, q_gain, k_gain, *, tl=512):
    bsz, l, _ = proj3.shape
    hw = 2 * C_HEAD_DIM
    gq = jnp.tile(q_gain.astype(jnp.float32), 2).reshape(1, hw)
    gk = jnp.tile(k_gain.astype(jnp.float32), 2).reshape(1, hw)
    vec = pl.BlockSpec((1, hw), lambda b, h, i: (0, 0))
    return pl.pallas_call(
        _attn_prep_kernel,
        grid=(bsz, C_HEADS, l // tl),
        in_specs=[pl.BlockSpec((None, tl, hw), lambda b, h, i: (b, i, h)),
                  pl.BlockSpec((None, tl, hw), lambda b, h, i: (b, i, C_HEADS + h)),
                  pl.BlockSpec((None, tl, hw), lambda b, h, i: (b, i, 2 * C_HEADS + h)),
                  vec, vec],
        out_specs=[pl.BlockSpec((None, None, 2, tl, hw), lambda b, h, i: (b, h, 0, i, 0)),
                   pl.BlockSpec((None, None, hw, tl), lambda b, h, i: (b, h, 0, i)),
                   pl.BlockSpec((None, None, tl, hw), lambda b, h, i: (b, h, i, 0))],
        out_shape=[jax.ShapeDtypeStruct((bsz, C_HEADS, 2, l, hw), jnp.bfloat16),
                   jax.ShapeDtypeStruct((bsz, C_HEADS, hw, l), jnp.bfloat16),
                   jax.ShapeDtypeStruct((bsz, C_HEADS, l, hw), jnp.bfloat16)],
        compiler_params=pltpu.CompilerParams(dimension_semantics=("parallel", "parallel", "parallel"),
                                             vmem_limit_bytes=VMEM_LIMIT_BYTES),
        name="attn_prep",
    )(proj3, proj3, proj3, gq, gk)


ATT_ROWS = 64


def _attn_kernel(lam_ref, q2_ref, kt_ref, v_ref, bias_ref, g_ref, o_ref, m_sc, l_sc, acc_sc, s_sc, p_sc, a_sc,
                 *, t, nk, out_scale):
    qi = pl.program_id(2)
    q2 = q2_ref[...].reshape(2 * t, 2 * C_HEAD_DIM)
    m_sc[...] = jnp.full(m_sc.shape, -jnp.inf, jnp.float32)
    l_sc[...] = jnp.zeros_like(l_sc)
    acc_sc[...] = jnp.zeros_like(acc_sc)
    r = ATT_ROWS
    hw = 2 * C_HEAD_DIM

    def body(ki, carry):
        off = pl.multiple_of(ki * t, t)
        bidx = jnp.clip(ki - qi, -2, 2) + 2
        s_sc[...] = jnp.dot(q2, kt_ref[:, pl.ds(off, t)], preferred_element_type=jnp.float32)
        for g in range(2 * t // r):
            rows = slice(g * r, (g + 1) * r)
            brow = (g * r) % t
            s = s_sc[rows, :] + bias_ref[bidx, brow:brow + r, :]
            m_prev = m_sc[rows, :]
            m_new = jnp.maximum(m_prev, jnp.max(s, axis=-1, keepdims=True))
            alpha = jnp.exp2(m_prev - m_new)
            ps = [jnp.exp2(s[:, j * hw:(j + 1) * hw] - m_new) for j in range(t // hw)]
            l_sc[rows, :] = alpha * l_sc[rows, :] + jnp.sum(sum(ps), axis=-1, keepdims=True)
            m_sc[rows, :] = m_new
            a_sc[rows, :] = alpha
            for j in range(t // hw):
                p_sc[rows, j * hw:(j + 1) * hw] = ps[j].astype(jnp.bfloat16)
        acc_sc[...] = a_sc[...] * acc_sc[...] + jnp.dot(p_sc[...], v_ref[pl.ds(off, t), :],
                                                        preferred_element_type=jnp.float32)
        return carry

    lax.fori_loop(0, nk, body, 0)
    a = acc_sc[...] / l_sc[...]
    o = a[:t] - lam_ref[0] * a[t:]
    y = o * lax.rsqrt(jnp.mean(o * o, axis=-1, keepdims=True) + EPS)
    o_ref[...] = y * g_ref[...] * out_scale


def diff_attention(proj3, q_gain, k_gain, lam, out_gain, bias5, layer_idx):
    f32 = jnp.float32
    bsz, l, _ = proj3.shape
    t = ATT_T
    hw = 2 * C_HEAD_DIM
    lam_init = 0.8 - 0.6 * math.exp(-0.3 * layer_idx)
    lam_f = lam.astype(f32)
    lam_full = jnp.exp(jnp.sum(lam_f[0] * lam_f[1])) - jnp.exp(jnp.sum(lam_f[2] * lam_f[3])) + lam_init
    q2, kt, vb = attn_prep(proj3, q_gain, k_gain)
    return pl.pallas_call(
        functools.partial(_attn_kernel, t=t, nk=l // t, out_scale=1.0 - lam_init),
        grid=(bsz, C_HEADS, l // t),
        in_specs=[pl.BlockSpec(memory_space=pltpu.SMEM),
                  pl.BlockSpec((None, None, 2, t, hw), lambda b, h, i: (b, h, 0, i, 0)),
                  pl.BlockSpec((None, None, hw, l), lambda b, h, i: (b, h, 0, 0)),
                  pl.BlockSpec((None, None, l, hw), lambda b, h, i: (b, h, 0, 0)),
                  pl.BlockSpec((None, 5, t, t), lambda b, h, i: (h, 0, 0, 0)),
                  pl.BlockSpec((1, hw), lambda b, h, i: (0, 0))],
        out_specs=pl.BlockSpec((None, t, hw), lambda b, h, i: (b, i, h)),
        out_shape=jax.ShapeDtypeStruct((bsz, l, C_WIDTH), f32),
        scratch_shapes=[pltpu.VMEM((2 * t, hw), f32), pltpu.VMEM((2 * t, hw), f32), pltpu.VMEM((2 * t, hw), f32),
                        pltpu.VMEM((2 * t, t), f32), pltpu.VMEM((2 * t, t), jnp.bfloat16), pltpu.VMEM((2 * t, hw), f32)],
        compiler_params=pltpu.CompilerParams(dimension_semantics=("parallel", "parallel", "arbitrary"),
                                             vmem_limit_bytes=VMEM_LIMIT_BYTES),
        name="diff_attention",
    )(lam_full.reshape(1), q2, kt, vb, bias5, out_gain.reshape(1, hw).astype(f32))


def short_conv(x, w):
    ch = x.shape[-1]
    pad = CONV_WIDTH // 2
    return lax.conv_general_dilated(x, w[:, None, :].astype(x.dtype), window_strides=(1,),
                                    padding=[(pad, pad)], dimension_numbers=('NWC', 'WIO', 'NWC'),
                                    feature_group_count=ch)


def gdn_scan(q, k, v, g, beta):
    b, h, l, dk = q.shape
    dv = v.shape[-1]
    nc = l // CHUNK

    def chunks(t):
        return t.reshape((b, h, nc, CHUNK) + t.shape[3:])

    q, k, v, g, beta = (chunks(t) for t in (q, k, v, g, beta))
    gam = jnp.cumsum(g, axis=-1)
    incl = jnp.tril(jnp.ones((CHUNK, CHUNK), bool))
    strict = jnp.tril(jnp.ones((CHUNK, CHUNK), bool), -1)
    diff = gam[..., :, None] - gam[..., None, :]
    decay = jnp.where(incl, jnp.exp(jnp.where(incl, diff, 0.0)), 0.0)
    k_beta = k * beta[..., None]
    v_beta = v * beta[..., None]
    a = jnp.where(strict, jnp.einsum('bhntd,bhnsd->bhnts', k_beta, k) * decay, 0.0)
    m = a + jnp.eye(CHUNK, dtype=a.dtype)
    value = lax.linalg.triangular_solve(m, v_beta, left_side=True, lower=True, unit_diagonal=True)
    k_cum = lax.linalg.triangular_solve(m, k_beta * jnp.exp(gam)[..., None], left_side=True,
                                        lower=True, unit_diagonal=True)
    attn = jnp.einsum('bhntd,bhnsd->bhnts', q, k) * decay
    q_dec = q * jnp.exp(gam)[..., None]
    g_last = gam[..., -1]
    k_dec = k * jnp.exp(g_last[..., None] - gam)[..., None]
    xs = tuple(jnp.moveaxis(t, 2, 0) for t in (value, k_cum, attn, q_dec, k_dec, g_last))

    def step(S, inp):
        val, kc, at, qd, kd, gl = inp
        v_new = val - kc @ S
        o = qd @ S + at @ v_new
        S = jnp.exp(gl)[..., None, None] * S + jnp.einsum('bhsd,bhsv->bhdv', kd, v_new)
        return S, o

    S0 = jnp.zeros((b, h, dk, dv), jnp.float32)
    _, o = lax.scan(step, S0, xs)
    return jnp.moveaxis(o, 0, 2).reshape(b, h, l, dv)


def gated_deltanet(qkv_raw, a_raw, b_raw, g_raw, conv_w, a_log, dt_bias, out_gain):
    f32 = jnp.float32
    bsz, l = qkv_raw.shape[:2]
    qkv = jax.nn.silu(short_conv(qkv_raw, conv_w).astype(f32))
    q, k, v = jnp.split(qkv, 3, axis=-1)
    q = l2norm(_heads(q, D_HEADS)) * (D_HEAD_DIM ** -0.5)
    k = l2norm(_heads(k, D_HEADS))
    v = _heads(v, D_HEADS)
    a = a_raw.astype(f32).reshape(bsz, l, 2, D_HEADS)
    bb = b_raw.astype(f32).reshape(bsz, l, 2, D_HEADS)
    a_log = a_log.astype(f32)
    dt_bias = dt_bias.astype(f32)
    o = jnp.zeros((bsz, D_HEADS, l, D_HEAD_DIM), f32)
    for direction in range(2):
        g = -jnp.exp(a_log[direction]) * jax.nn.softplus(a[:, :, direction] + dt_bias[direction])
        beta = jax.nn.sigmoid(bb[:, :, direction])
        g = jnp.transpose(g, (0, 2, 1))
        beta = jnp.transpose(beta, (0, 2, 1))
        if direction == 0:
            o = o + gdn_scan(q, k, v, g, beta)
        else:
            o = o + _flip(gdn_scan(_flip(q), _flip(k), _flip(v), _flip(g), _flip(beta)))
    o = rmsnorm(jnp.transpose(o, (0, 2, 1, 3)), out_gain).reshape(bsz, l, D_WIDTH)
    return o * jax.nn.silu(g_raw.astype(f32))


def ec_moe(h, w_router, w_gate, w_up, w_down):
    b, l, d = h.shape
    cap = EC_CAPACITY_FACTOR * l // N_EXPERTS
    logits = jnp.einsum('bld,de->ble', h, w_router).astype(jnp.float32)
    aff = jax.nn.softmax(logits, axis=-1)
    gate, idx = lax.top_k(jnp.swapaxes(aff, 1, 2), cap)
    xs = jax.vmap(lambda hb, ib: hb[ib])(h, idx)
    hid = jax.nn.silu(jnp.einsum('becd,edf->becf', xs, w_gate)) * jnp.einsum('becd,edf->becf', xs, w_up)
    out = jnp.einsum('becf,efd->becd', hid, w_down) * gate[..., None].astype(h.dtype)
    return jax.vmap(lambda ob, ib: jnp.zeros((l, d), ob.dtype).at[ib.reshape(-1)].add(ob.reshape(-1, d)))(out, idx)


def kernel(x, mix_norm, ffn_norm, ev_w_in, ev_w_out, a_lb_logits, a_out_norm, s5_lambda_re, s5_lambda_im, s5_log_step, s5_b_re, s5_b_im, s5_c_re, s5_c_im, s5_d, s5_glu_w, s5_glu_b, od_w_in, od_w_out, c_q_norm, c_k_norm, c_lambda, c_out_norm, rel_bias, d_conv_w, d_a_log, d_dt_bias, d_out_norm, moe_router, moe_w_gate, moe_w_up, moe_w_down):
    bsz, l, d = x.shape
    n = bsz * l
    p = jax.nn.softmax(a_lb_logits.astype(jnp.float32), axis=0)
    cum = jnp.cumsum(p, axis=0)
    lower_bounds = cum - cum[0:1]
    bias5 = rel_bias_tiles(rel_bias, ATT_T)
    for layer in range(DEPTH):
        j = layer // 2
        if layer % 2 == 0:
            proj = norm_matmul(x.reshape(n, d), mix_norm[layer], ev_w_in[j]).reshape(bsz, l, -1)
            u_b = proj[..., 5 * A_WIDTH:]
            o_a = hgrn2_mixer(proj, lower_bounds[j], a_out_norm[j])
            u_tb = jnp.transpose(u_b, (1, 0, 2)).reshape(l * bsz, B_WIDTH)
            o_b = s5_mixer_tb(u_tb, bsz, s5_lambda_re[j], s5_lambda_im[j], s5_log_step[j], s5_b_re[j], s5_b_im[j],
                              s5_c_re[j], s5_c_im[j], s5_d[j], s5_glu_w[j], s5_glu_b[j])
            o_b = jnp.transpose(o_b.reshape(l, bsz, B_WIDTH), (1, 0, 2))
            mixed = jnp.concatenate([o_a, o_b], axis=-1)
            x = matmul_residual(mixed.reshape(n, -1), ev_w_out[j], x.reshape(n, d)).reshape(bsz, l, d)
        else:
            o1 = 3 * C_WIDTH
            o2 = o1 + 3 * D_WIDTH
            o3 = o2 + 2 * D_HEADS
            o4 = o3 + 2 * D_HEADS
            w_in = od_w_in[j]
            pad = (-w_in.shape[1]) % 1280
            w_in = jnp.pad(w_in, ((0, 0), (0, pad)))
            proj = norm_matmul(x.reshape(n, d), mix_norm[layer], w_in, tn=1280)
            proj = proj.reshape(bsz, l, -1)
            q_c, k_c, v_c, qkv_d, a_d, b_d, g_d, _ = jnp.split(
                proj, [C_WIDTH, 2 * C_WIDTH, o1, o2, o3, o4, o4 + D_WIDTH], axis=-1)
            o_c = diff_attention(proj, c_q_norm[j], c_k_norm[j], c_lambda[j], c_out_norm[j], bias5, layer)
            o_d = gated_deltanet(qkv_d, a_d, b_d, g_d, d_conv_w[j], d_a_log[j], d_dt_bias[j], d_out_norm[j])
            mixed = jnp.concatenate([o_c, o_d], axis=-1)
            x = matmul_residual(mixed.reshape(n, -1), od_w_out[j], x.reshape(n, d)).reshape(bsz, l, d)
        x = x + ec_moe(rmsnorm(x, ffn_norm[layer]), moe_router[layer], moe_w_gate[layer],
                       moe_w_up[layer], moe_w_down[layer])
    return x
```

```python
import functools
import math

import jax
import jax.numpy as jnp
from jax import lax
from jax.experimental import pallas as pl
from jax.experimental.pallas import tpu as pltpu

D_MODEL = 1024
DEPTH = 4
MIX_WIDTH = D_MODEL
A_WIDTH = MIX_WIDTH // 2
A_HEAD_DIM = 128
A_HEADS = A_WIDTH // A_HEAD_DIM
B_WIDTH = MIX_WIDTH - A_WIDTH
S5_GROUP = 16
S5_GROUPS = B_WIDTH // S5_GROUP
S5_STATE = 64
C_WIDTH = MIX_WIDTH // 2
C_HEAD_DIM = 64
C_HEADS = C_WIDTH // (2 * C_HEAD_DIM)
C_V_DIM = 2 * C_HEAD_DIM
D_WIDTH = MIX_WIDTH - C_WIDTH
D_HEAD_DIM = 128
D_HEADS = D_WIDTH // D_HEAD_DIM
CONV_WIDTH = 5
N_EXPERTS = 16
EXPERT_FF = 2 * D_MODEL
EC_CAPACITY_FACTOR = 2
REL_BUCKETS = 32
REL_MAX_DIST = 128
CHUNK = 64
Q_BLOCK = 128
EPS = 1e-6

VMEM_LIMIT_BYTES = 48 * 1024 * 1024


def _norm_matmul_kernel(x_ref, g_ref, w_ref, o_ref):
    x = x_ref[...]
    y = x * lax.rsqrt(jnp.mean(x * x, axis=-1, keepdims=True) + EPS) * g_ref[...]
    o_ref[...] = jnp.dot(y.astype(jnp.bfloat16), w_ref[...], preferred_element_type=jnp.float32)


def norm_matmul(x, gain, w, *, tm=512, tn=512):
    n, k = x.shape
    m = w.shape[1]
    tn = min(tn, m)
    assert n % tm == 0 and m % tn == 0
    return pl.pallas_call(
        _norm_matmul_kernel,
        grid=(n // tm, m // tn),
        in_specs=[pl.BlockSpec((tm, k), lambda i, j: (i, 0)),
                  pl.BlockSpec((1, k), lambda i, j: (0, 0)),
                  pl.BlockSpec((k, tn), lambda i, j: (0, j))],
        out_specs=pl.BlockSpec((tm, tn), lambda i, j: (i, j)),
        out_shape=jax.ShapeDtypeStruct((n, m), jnp.float32),
        compiler_params=pltpu.CompilerParams(dimension_semantics=("parallel", "parallel"),
                                             vmem_limit_bytes=VMEM_LIMIT_BYTES),
        name="norm_matmul",
    )(x, gain.reshape(1, k).astype(jnp.float32), w.astype(jnp.bfloat16))


def _matmul_res_kernel(a_ref, w_ref, r_ref, o_ref):
    o_ref[...] = r_ref[...] + jnp.dot(a_ref[...].astype(jnp.bfloat16), w_ref[...],
                                      preferred_element_type=jnp.float32)


def matmul_residual(a, w, res, *, tm=512):
    n, k = a.shape
    m = w.shape[1]
    return pl.pallas_call(
        _matmul_res_kernel,
        grid=(n // tm,),
        in_specs=[pl.BlockSpec((tm, k), lambda i: (i, 0)),
                  pl.BlockSpec((k, m), lambda i: (0, 0)),
                  pl.BlockSpec((tm, m), lambda i: (i, 0))],
        out_specs=pl.BlockSpec((tm, m), lambda i: (i, 0)),
        out_shape=jax.ShapeDtypeStruct((n, m), jnp.float32),
        compiler_params=pltpu.CompilerParams(dimension_semantics=("parallel",),
                                             vmem_limit_bytes=VMEM_LIMIT_BYTES),
        name="matmul_residual",
    )(a, w.astype(jnp.bfloat16), res)


def rmsnorm(x, g):
    xf = x.astype(jnp.float32)
    y = xf * lax.rsqrt(jnp.mean(xf * xf, axis=-1, keepdims=True) + EPS)
    return (y * g.astype(jnp.float32)).astype(x.dtype)


def l2norm(x):
    return x * lax.rsqrt(jnp.sum(x * x, axis=-1, keepdims=True) + EPS)


def _heads(t, n_heads):
    b, l, w = t.shape
    return jnp.transpose(t.reshape(b, l, n_heads, w // n_heads), (0, 2, 1, 3))


def _flip(t):
    return jnp.flip(t, axis=2)


def _to_chunks(t):
    b, h, l = t.shape[:3]
    t = t.reshape((b, h, l // CHUNK, CHUNK) + t.shape[3:])
    return jnp.moveaxis(t, 2, 0)


def _from_chunks(t):
    t = jnp.moveaxis(t, 0, 2)
    b, h, nc, c = t.shape[:4]
    return t.reshape((b, h, nc * c) + t.shape[4:])


HG_LEVELS = tuple(CHUNK >> (i + 1) for i in range(CHUNK.bit_length() - 1))
HG_TOT_ROWS = 8
HG_TT = 512


def hgrn2_constants():
    import numpy as np
    c = CHUNK
    r = np.arange(c)[:, None]
    u = np.arange(c)[None, :]
    stacks, masks = [], []
    for direction in range(2):
        fwd = direction == 0
        lvl_masks = []
        for m in HG_LEVELS:
            blk = r // (2 * m)
            later = (r % (2 * m)) >= m
            lvl_masks.append((blk == blk.T) & (later & ~later.T if fwd else ~later & later.T))
        stacks.append(np.concatenate([(u <= r) if fwd else (u >= r), np.ones((HG_TOT_ROWS, c), bool)], axis=0))
        masks.append(np.stack(lvl_masks))
    return (jnp.asarray(np.stack(stacks), jnp.bfloat16), jnp.asarray(np.stack(masks), jnp.float32))


def _hgrn2_kernel(q_ref, f_ref, v_ref, loglb_ref, log1mlb_ref, onemlb_ref, ast_ref, mask_ref, o_ref,
                  st_sc, qd_sc, dec_sc, upd_sc, sin_sc, *, nc):
    direction = pl.program_id(2)

    @pl.when(pl.program_id(3) == 0)
    def _():
        st_sc[...] = jnp.zeros_like(st_sc)

    bf16 = jnp.bfloat16
    f32 = jnp.float32
    c = CHUNK
    hd = A_HEAD_DIM
    dirf = direction.astype(f32)
    loglb = loglb_ref[...]
    log1mlb = log1mlb_ref[...]
    onemlb = onemlb_ref[...]
    ast = ast_ref[...]
    contract_last = (((1,), (1,)), ((), ()))
    contract_first = (((0,), (0,)), ((), ()))

    ns = range(nc)
    rows = [slice(n * c, (n + 1) * c) for n in ns]
    z = [f_ref[r, :] for r in rows]
    v = [v_ref[r, :] for r in rows]
    qr = [q_ref[r, :] for r in rows]
    q = [x * jax.nn.sigmoid(x) for x in qr]
    e = [jnp.exp(-jnp.abs(x)) for x in z]
    cc = [log1mlb + jnp.minimum(z[n], 0.0) - jnp.log1p(e[n]) for n in ns]
    lf = [jnp.maximum(loglb, x) + jnp.log1p(jnp.exp(-jnp.abs(loglb - x))) for x in cc]
    k = [onemlb * jnp.where(z[n] >= 0, e[n], 1.0) / (1.0 + e[n]) for n in ns]
    hi = [x.astype(bf16) for x in lf]
    lo = [(lf[n] - hi[n].astype(f32)).astype(bf16) for n in ns]
    d = [jnp.dot(ast, hi[n], preferred_element_type=f32) + jnp.dot(ast, lo[n], preferred_element_type=f32)
         for n in ns]
    cum = [x[0:c] for x in d]
    tot = [x[c:c + HG_TOT_ROWS] for x in d]
    ref = [cum[n] - dirf * lf[n] for n in ns]
    attn = [jnp.zeros((c, c), f32) for _ in ns]
    for li, m in enumerate(HG_LEVELS):
        nb = c // (2 * m)
        split = [jnp.broadcast_to(x.reshape(nb, 2 * m, hd)[:, m - 1:m, :], (nb, 2 * m, hd)).reshape(c, hd)
                 for x in ref]
        x = [jnp.exp(-jnp.abs(cum[n] - split[n])) for n in ns]
        s = [lax.dot_general((q[n] * x[n]).astype(bf16), (k[n] * x[n]).astype(bf16), contract_last,
                             preferred_element_type=f32) for n in ns]
        attn = [attn[n] + mask_ref[li] * s[n] for n in ns]
    vb = [x.astype(bf16) for x in v]
    intra = [jnp.dot(attn[n].astype(bf16), vb[n], preferred_element_type=f32) for n in ns]
    upd = [lax.dot_general(vb[n], (k[n] * jnp.exp(tot[n][0:1] - cum[n])).astype(bf16), contract_first,
                           preferred_element_type=f32) for n in ns]
    for n in ns:
        o_ref[rows[n], :] = intra[n] + jnp.sum(q[n] * k[n], axis=-1, keepdims=True) * v[n]
        qd_sc[n] = (q[n] * jnp.exp(cum[n])).astype(bf16)
        dec_sc[n] = jnp.exp(tot[n])
        upd_sc[n] = upd[n]

    def body(ci, st):
        ce = ci + direction * (nc - 1 - 2 * ci)
        sin_sc[ce] = st.astype(bf16)
        return st * dec_sc[ce][0:1] + upd_sc[ce]

    st_sc[...] = lax.fori_loop(0, nc, body, st_sc[...])

    for n in range(nc):
        rows = slice(n * c, (n + 1) * c)
        o_ref[rows, :] += lax.dot_general(qd_sc[n], sin_sc[n], contract_last, preferred_element_type=f32)


def hgrn2_scan(proj3, lb):
    bsz, l, _ = proj3.shape
    hd = A_HEAD_DIM
    tt = min(HG_TT, l)
    nt = l // tt
    assert l % tt == 0 and tt % CHUNK == 0
    ast, masks = hgrn2_constants()
    lb = lb.astype(jnp.float32)
    vecs = [jnp.log(lb).reshape(2, 1, A_WIDTH), jnp.log1p(-lb).reshape(2, 1, A_WIDTH), (1.0 - lb).reshape(2, 1, A_WIDTH)]
    tidx = lambda d, i: i + d * (nt - 1 - 2 * i)
    vec = pl.BlockSpec((None, 1, hd), lambda b, h, d, i: (d, 0, h))
    return pl.pallas_call(
        functools.partial(_hgrn2_kernel, nc=tt // CHUNK),
        grid=(bsz, A_HEADS, 2, nt),
        in_specs=[pl.BlockSpec((None, tt, hd), lambda b, h, d, i: (b, tidx(d, i), h)),
                  pl.BlockSpec((None, tt, hd), lambda b, h, d, i: (b, tidx(d, i), (1 + d) * A_HEADS + h)),
                  pl.BlockSpec((None, tt, hd), lambda b, h, d, i: (b, tidx(d, i), 3 * A_HEADS + h)),
                  vec, vec, vec,
                  pl.BlockSpec((None,) + ast.shape[1:], lambda b, h, d, i: (d, 0, 0)),
                  pl.BlockSpec((None,) + masks.shape[1:], lambda b, h, d, i: (d, 0, 0, 0))],
        out_specs=pl.BlockSpec((None, None, tt, hd), lambda b, h, d, i: (d, b, tidx(d, i), h)),
        out_shape=jax.ShapeDtypeStruct((2, bsz, l, A_WIDTH), jnp.float32),
        scratch_shapes=[pltpu.VMEM((hd, hd), jnp.float32),
                        pltpu.VMEM((tt // CHUNK, CHUNK, hd), jnp.bfloat16),
                        pltpu.VMEM((tt // CHUNK, HG_TOT_ROWS, hd), jnp.float32),
                        pltpu.VMEM((tt // CHUNK, hd, hd), jnp.float32),
                        pltpu.VMEM((tt // CHUNK, hd, hd), jnp.bfloat16)],
        compiler_params=pltpu.CompilerParams(dimension_semantics=("parallel", "parallel", "parallel", "arbitrary"),
                                             vmem_limit_bytes=VMEM_LIMIT_BYTES),
        name="hgrn2_scan",
    )(proj3, proj3, proj3, *vecs, ast, masks)


def _hgrn2_final_kernel(of_ref, ob_ref, g_ref, gain_ref, o_ref):
    o = of_ref[...] + ob_ref[...]
    y = o * lax.rsqrt(jnp.mean(o * o, axis=-1, keepdims=True) + EPS) * gain_ref[...]
    g = g_ref[...]
    o_ref[...] = y * (g * jax.nn.sigmoid(g))


def bidir_finalize(o2, proj3, gate_block, out_gain, *, name, tm=1024):
    _, bsz, l, w = o2.shape
    hd = 128
    tm = min(tm, l)
    return pl.pallas_call(
        _hgrn2_final_kernel,
        grid=(bsz, l // tm, w // hd),
        in_specs=[pl.BlockSpec((None, None, tm, hd), lambda b, i, h: (0, b, i, h)),
                  pl.BlockSpec((None, None, tm, hd), lambda b, i, h: (1, b, i, h)),
                  pl.BlockSpec((None, tm, hd), lambda b, i, h: (b, i, gate_block + h)),
                  pl.BlockSpec((1, hd), lambda b, i, h: (0, 0))],
        out_specs=pl.BlockSpec((None, tm, hd), lambda b, i, h: (b, i, h)),
        out_shape=jax.ShapeDtypeStruct((bsz, l, w), jnp.float32),
        compiler_params=pltpu.CompilerParams(dimension_semantics=("parallel", "parallel", "parallel"),
                                             vmem_limit_bytes=VMEM_LIMIT_BYTES),
        name=name,
    )(o2, o2, proj3, out_gain.reshape(1, hd).astype(jnp.float32))


def hgrn2_mixer(proj3, lb, out_gain):
    return bidir_finalize(hgrn2_scan(proj3, lb), proj3, 4 * A_HEADS, out_gain, name="hgrn2_finalize")


S5_NS = S5_GROUPS * S5_STATE
S5_TT = 64
SUBLANES = 8


def _s5_scan_kernel(u_ref, win_ref, ar_ref, ai_ref, wout_ref, y_ref, bu_sc, xs_sc, st_sc, *, bsz, tt, reverse):
    @pl.when(pl.program_id(0) == 0)
    def _():
        st_sc[...] = jnp.zeros_like(st_sc)

    bu_sc[...] = jnp.dot(u_ref[...].astype(jnp.bfloat16), win_ref[...], preferred_element_type=jnp.float32)
    ar = jnp.broadcast_to(ar_ref[...], (bsz, S5_NS))
    ai = jnp.broadcast_to(ai_ref[...], (bsz, S5_NS))
    per = SUBLANES // bsz
    ngroups = tt // per

    def body(s, carry):
        xr, xi = carry
        p = (ngroups - 1 - s) if reverse else s
        base = pl.multiple_of(p * SUBLANES, SUBLANES)
        blk = bu_sc[pl.ds(base, SUBLANES), :]
        outs_r = [None] * per
        outs_i = [None] * per
        for ph in (range(per - 1, -1, -1) if reverse else range(per)):
            br = blk[ph * bsz:(ph + 1) * bsz, :S5_NS]
            bi = blk[ph * bsz:(ph + 1) * bsz, S5_NS:]
            xr, xi = ar * xr - ai * xi + br, ar * xi + ai * xr + bi
            outs_r[ph] = xr
            outs_i[ph] = xi
        xs_sc[pl.ds(base, SUBLANES), :S5_NS] = jnp.concatenate(outs_r, axis=0)
        xs_sc[pl.ds(base, SUBLANES), S5_NS:] = jnp.concatenate(outs_i, axis=0)
        return xr, xi

    xr, xi = lax.fori_loop(0, ngroups, body, (st_sc[0], st_sc[1]))
    st_sc[0] = xr
    st_sc[1] = xi
    y_ref[...] = jnp.dot(xs_sc[...].astype(jnp.bfloat16), wout_ref[...], preferred_element_type=jnp.float32)


def s5_scan(u_tb, win, ar, ai, wout, *, bsz, reverse):
    n = u_tb.shape[0]
    rows = S5_TT * bsz
    nt = n // rows
    assert n % rows == 0 and SUBLANES % bsz == 0
    idx = (lambda i: (nt - 1 - i, 0)) if reverse else (lambda i: (i, 0))
    const = lambda i: (0, 0)
    return pl.pallas_call(
        functools.partial(_s5_scan_kernel, bsz=bsz, tt=S5_TT, reverse=reverse),
        grid=(nt,),
        in_specs=[pl.BlockSpec((rows, B_WIDTH), idx),
                  pl.BlockSpec((B_WIDTH, 2 * S5_NS), const),
                  pl.BlockSpec((1, S5_NS), const),
                  pl.BlockSpec((1, S5_NS), const),
                  pl.BlockSpec((2 * S5_NS, B_WIDTH), const)],
        out_specs=pl.BlockSpec((rows, B_WIDTH), idx),
        out_shape=jax.ShapeDtypeStruct((n, B_WIDTH), jnp.float32),
        scratch_shapes=[pltpu.VMEM((rows, 2 * S5_NS), jnp.float32),
                        pltpu.VMEM((rows, 2 * S5_NS), jnp.float32),
                        pltpu.VMEM((2, bsz, S5_NS), jnp.float32)],
        compiler_params=pltpu.CompilerParams(dimension_semantics=("arbitrary",),
                                             vmem_limit_bytes=VMEM_LIMIT_BYTES),
        name="s5_scan_bwd" if reverse else "s5_scan_fwd",
    )(u_tb, win, ar, ai, wout)


def _s5_final_kernel(u_ref, yf_ref, yb_ref, d_ref, w_ref, b_ref, o_ref):
    y = d_ref[...] * u_ref[...] + yf_ref[...] + yb_ref[...]
    y = jax.nn.gelu(y)
    z = jnp.dot(y.astype(jnp.bfloat16), w_ref[...], preferred_element_type=jnp.float32) + b_ref[...]
    o_ref[...] = y * jax.nn.sigmoid(z)


def s5_finalize(u, yf, yb, d_skip, glu_w, glu_b, *, tm=512):
    n, w = u.shape
    row = pl.BlockSpec((tm, w), lambda i: (i, 0))
    vec = pl.BlockSpec((1, w), lambda i: (0, 0))
    return pl.pallas_call(
        _s5_final_kernel,
        grid=(n // tm,),
        in_specs=[row, row, row, vec, pl.BlockSpec((w, w), lambda i: (0, 0)), vec],
        out_specs=row,
        out_shape=jax.ShapeDtypeStruct((n, w), jnp.float32),
        compiler_params=pltpu.CompilerParams(dimension_semantics=("parallel",),
                                             vmem_limit_bytes=VMEM_LIMIT_BYTES),
        name="s5_finalize",
    )(u, yf, yb, d_skip.reshape(1, w).astype(jnp.float32), glu_w.astype(jnp.bfloat16),
      glu_b.reshape(1, w).astype(jnp.float32))


def s5_direction_params(lam_re, lam_im, log_step, b_re, b_im, c_re, c_im):
    step = jnp.exp(log_step)[:, None]
    mag = jnp.exp(lam_re * step)
    abar_re = mag * jnp.cos(lam_im * step)
    abar_im = mag * jnp.sin(lam_im * step)
    den = lam_re * lam_re + lam_im * lam_im
    fr = ((abar_re - 1.0) * lam_re + abar_im * lam_im) / den
    fi = (abar_im * lam_re - (abar_re - 1.0) * lam_im) / den
    bb_re = fr[..., None] * b_re - fi[..., None] * b_im
    bb_im = fr[..., None] * b_im + fi[..., None] * b_re
    eye = jnp.eye(S5_GROUPS, dtype=jnp.float32)
    win = jnp.concatenate([jnp.einsum('gnp,gh->gphn', bb, eye).reshape(B_WIDTH, S5_NS) for bb in (bb_re, bb_im)],
                          axis=1)
    wout = jnp.concatenate([jnp.einsum('gpn,gh->hngp', c, eye).reshape(S5_NS, B_WIDTH) for c in (c_re, -c_im)],
                           axis=0)
    return (win.astype(jnp.bfloat16), abar_re.reshape(1, S5_NS), abar_im.reshape(1, S5_NS),
            wout.astype(jnp.bfloat16))


def s5_mixer_tb(u_tb, bsz, lam_re, lam_im, log_step, b_re, b_im, c_re, c_im, d_skip, glu_w, glu_b):
    f32 = jnp.float32
    ys = []
    for direction in range(2):
        prm = s5_direction_params(lam_re[direction].astype(f32), lam_im[direction].astype(f32),
                                  log_step[direction].astype(f32), b_re[direction].astype(f32),
                                  b_im[direction].astype(f32), c_re[direction].astype(f32),
                                  c_im[direction].astype(f32))
        ys.append(s5_scan(u_tb, *prm, bsz=bsz, reverse=(direction == 1)))
    return s5_finalize(u_tb, ys[0], ys[1], d_skip, glu_w, glu_b)


def t5_bucket(rel):
    half = REL_BUCKETS // 2
    max_exact = half // 2
    base = jnp.where(rel > 0, half, 0)
    n = jnp.abs(rel)
    nf = jnp.maximum(n, 1).astype(jnp.float32)
    large = max_exact + (jnp.log(nf / max_exact) / math.log(REL_MAX_DIST / max_exact)
                         * (half - max_exact)).astype(jnp.int32)
    large = jnp.minimum(large, half - 1)
    return base + jnp.where(n < max_exact, n, large)


ATT_T = 512
LOG2E = math.log2(math.e)


def rel_bias_tiles(rel_bias, t):
    assert t >= REL_MAX_DIST
    table = rel_bias.astype(jnp.float32) * LOG2E
    tiles = []
    for d in (-1, 0, 1):
        c = table[t5_bucket(d * t + jnp.arange(-(t - 1), t))]
        w = jnp.concatenate([c, c[:1]], axis=0)
        m = jnp.tile(w, (t, 1))[:t * (2 * t - 1)].reshape(t, 2 * t - 1, -1)
        tiles.append(m[:, t - 1:2 * t - 1])
    far_neg = jnp.broadcast_to(table[t5_bucket(jnp.array(-2 * t))], tiles[0].shape)
    far_pos = jnp.broadcast_to(table[t5_bucket(jnp.array(2 * t))], tiles[0].shape)
    out = jnp.stack([far_neg] + tiles + [far_pos], axis=0)
    return jnp.transpose(out, (3, 0, 1, 2))


def _attn_prep_kernel(q_ref, k_ref, v_ref, qg_ref, kg_ref, q2_ref, kt_ref, vb_ref):
    lane = lax.broadcasted_iota(jnp.int32, q_ref.shape, 1)
    lo = lane < C_HEAD_DIM

    def halfnorm(x, g):
        sq = x * x
        s_lo = jnp.sum(jnp.where(lo, sq, 0.0), axis=-1, keepdims=True)
        s_hi = jnp.sum(jnp.where(lo, 0.0, sq), axis=-1, keepdims=True)
        ms = jnp.where(lo, s_lo, s_hi) * (1.0 / C_HEAD_DIM)
        return x * lax.rsqrt(ms + EPS) * g

    qn = halfnorm(q_ref[...], qg_ref[...]) * (C_HEAD_DIM ** -0.5 * LOG2E)
    kn = halfnorm(k_ref[...], kg_ref[...])
    q2_ref[0] = jnp.where(lo, qn, 0.0).astype(jnp.bfloat16)
    q2_ref[1] = jnp.where(lo, 0.0, qn).astype(jnp.bfloat16)
    kt_ref[...] = kn.T.astype(jnp.bfloat16)
    vb_ref[...] = v_ref[...].astype(jnp.bfloat16)


def attn_prep(proj3, q_gain, k_gain, *, tl=512):
    bsz, l, _ = proj3.shape
    hw = 2 * C_HEAD_DIM
    gq = jnp.tile(q_gain.astype(jnp.float32), 2).reshape(1, hw)
    gk = jnp.tile(k_gain.astype(jnp.float32), 2).reshape(1, hw)
    vec = pl.BlockSpec((1, hw), lambda b, h, i: (0, 0))
    return pl.pallas_call(
        _attn_prep_kernel,
        grid=(bsz, C_HEADS, l // tl),
        in_specs=[pl.BlockSpec((None, tl, hw), lambda b, h, i: (b, i, h)),
                  pl.BlockSpec((None, tl, hw), lambda b, h, i: (b, i, C_HEADS + h)),
                  pl.BlockSpec((None, tl, hw), lambda b, h, i: (b, i, 2 * C_HEADS + h)),
                  vec, vec],
        out_specs=[pl.BlockSpec((None, None, 2, tl, hw), lambda b, h, i: (b, h, 0, i, 0)),
                   pl.BlockSpec((None, None, hw, tl), lambda b, h, i: (b, h, 0, i)),
                   pl.BlockSpec((None, None, tl, hw), lambda b, h, i: (b, h, i, 0))],
        out_shape=[jax.ShapeDtypeStruct((bsz, C_HEADS, 2, l, hw), jnp.bfloat16),
                   jax.ShapeDtypeStruct((bsz, C_HEADS, hw, l), jnp.bfloat16),
                   jax.ShapeDtypeStruct((bsz, C_HEADS, l, hw), jnp.bfloat16)],
        compiler_params=pltpu.CompilerParams(dimension_semantics=("parallel", "parallel", "parallel"),
                                             vmem_limit_bytes=VMEM_LIMIT_BYTES),
        name="attn_prep",
    )(proj3, proj3, proj3, gq, gk)


ATT_ROWS = 64


def _attn_kernel(lam_ref, q2_ref, kt_ref, v_ref, bias_ref, g_ref, o_ref, m_sc, l_sc, acc_sc, s_sc, p_sc, a_sc,
                 *, t, nk, out_scale):
    qi = pl.program_id(2)
    q2 = q2_ref[...].reshape(2 * t, 2 * C_HEAD_DIM)
    m_sc[...] = jnp.full(m_sc.shape, -jnp.inf, jnp.float32)
    l_sc[...] = jnp.zeros_like(l_sc)
    acc_sc[...] = jnp.zeros_like(acc_sc)
    r = ATT_ROWS
    hw = 2 * C_HEAD_DIM

    def body(ki, carry):
        off = pl.multiple_of(ki * t, t)
        bidx = jnp.clip(ki - qi, -2, 2) + 2
        s_sc[...] = jnp.dot(q2, kt_ref[:, pl.ds(off, t)], preferred_element_type=jnp.float32)
        for g in range(2 * t // r):
            rows = slice(g * r, (g + 1) * r)
            brow = (g * r) % t
            s = s_sc[rows, :] + bias_ref[bidx, brow:brow + r, :]
            m_prev = m_sc[rows, :]
            m_new = jnp.maximum(m_prev, jnp.max(s, axis=-1, keepdims=True))
            alpha = jnp.exp2(m_prev - m_new)
            ps = [jnp.exp2(s[:, j * hw:(j + 1) * hw] - m_new) for j in range(t // hw)]
            l_sc[rows, :] = alpha * l_sc[rows, :] + jnp.sum(sum(ps), axis=-1, keepdims=True)
            m_sc[rows, :] = m_new
            a_sc[rows, :] = alpha
            for j in range(t // hw):
                p_sc[rows, j * hw:(j + 1) * hw] = ps[j].astype(jnp.bfloat16)
        acc_sc[...] = a_sc[...] * acc_sc[...] + jnp.dot(p_sc[...], v_ref[pl.ds(off, t), :],
                                                        preferred_element_type=jnp.float32)
        return carry

    lax.fori_loop(0, nk, body, 0)
    a = acc_sc[...] / l_sc[...]
    o = a[:t] - lam_ref[0] * a[t:]
    y = o * lax.rsqrt(jnp.mean(o * o, axis=-1, keepdims=True) + EPS)
    o_ref[...] = y * g_ref[...] * out_scale


def diff_attention(proj3, q_gain, k_gain, lam, out_gain, bias5, layer_idx):
    f32 = jnp.float32
    bsz, l, _ = proj3.shape
    t = ATT_T
    hw = 2 * C_HEAD_DIM
    lam_init = 0.8 - 0.6 * math.exp(-0.3 * layer_idx)
    lam_f = lam.astype(f32)
    lam_full = jnp.exp(jnp.sum(lam_f[0] * lam_f[1])) - jnp.exp(jnp.sum(lam_f[2] * lam_f[3])) + lam_init
    q2, kt, vb = attn_prep(proj3, q_gain, k_gain)
    return pl.pallas_call(
        functools.partial(_attn_kernel, t=t, nk=l // t, out_scale=1.0 - lam_init),
        grid=(bsz, C_HEADS, l // t),
        in_specs=[pl.BlockSpec(memory_space=pltpu.SMEM),
                  pl.BlockSpec((None, None, 2, t, hw), lambda b, h, i: (b, h, 0, i, 0)),
                  pl.BlockSpec((None, None, hw, l), lambda b, h, i: (b, h, 0, 0)),
                  pl.BlockSpec((None, None, l, hw), lambda b, h, i: (b, h, 0, 0)),
                  pl.BlockSpec((None, 5, t, t), lambda b, h, i: (h, 0, 0, 0)),
                  pl.BlockSpec((1, hw), lambda b, h, i: (0, 0))],
        out_specs=pl.BlockSpec((None, t, hw), lambda b, h, i: (b, i, h)),
        out_shape=jax.ShapeDtypeStruct((bsz, l, C_WIDTH), f32),
        scratch_shapes=[pltpu.VMEM((2 * t, hw), f32), pltpu.VMEM((2 * t, hw), f32), pltpu.VMEM((2 * t, hw), f32),
                        pltpu.VMEM((2 * t, t), f32), pltpu.VMEM((2 * t, t), jnp.bfloat16), pltpu.VMEM((2 * t, hw), f32)],
        compiler_params=pltpu.CompilerParams(dimension_semantics=("parallel", "parallel", "arbitrary"),
                                             vmem_limit_bytes=VMEM_LIMIT_BYTES),
        name="diff_attention",
    )(lam_full.reshape(1), q2, kt, vb, bias5, out_gain.reshape(1, hw).astype(f32))


GDN_TT = 512
LANES = 128
OD_QKV_BLOCK = 3 * C_WIDTH // LANES
OD_GATE_BLOCK = OD_QKV_BLOCK + 3 * D_WIDTH // LANES
OD_AB_BLOCK = OD_GATE_BLOCK + D_WIDTH // LANES
OD_COLS = 3840


def _gdn_prep_kernel(prev_ref, cur_ref, next_ref, w_ref, o_ref, *, tl, nl):
    i = pl.program_id(1)
    part = pl.program_id(2)
    prev = jnp.where(i > 0, prev_ref[...], 0.0)
    nxt = jnp.where(i < nl - 1, next_ref[...], 0.0)
    ext = jnp.concatenate([prev, cur_ref[...], nxt], axis=0)
    halo = prev.shape[0]
    acc = None
    for j in range(CONV_WIDTH):
        start = halo - CONV_WIDTH // 2 + j
        term = w_ref[j:j + 1, :] * ext[start:start + tl, :]
        acc = term if acc is None else acc + term
    y = acc * jax.nn.sigmoid(acc)
    scale = jnp.where(part == 0, D_HEAD_DIM ** -0.5, 1.0)
    heads = []
    for h in range(D_HEADS):
        yh = y[:, h * LANES:(h + 1) * LANES]
        heads.append(yh * (lax.rsqrt(jnp.sum(yh * yh, axis=-1, keepdims=True) + EPS) * scale))
    o_ref[...] = jnp.where(part < 2, jnp.concatenate(heads, axis=1), y)


def gdn_prep(proj3, conv_w, *, tl=512):
    bsz, l, _ = proj3.shape
    halo = SUBLANES
    nl = l // tl
    blk0 = OD_QKV_BLOCK * LANES // D_WIDTH
    return pl.pallas_call(
        functools.partial(_gdn_prep_kernel, tl=tl, nl=nl),
        grid=(bsz, nl, 3),
        in_specs=[pl.BlockSpec((None, halo, D_WIDTH), lambda b, i, p: (b, jnp.maximum(i * (tl // halo) - 1, 0), blk0 + p)),
                  pl.BlockSpec((None, tl, D_WIDTH), lambda b, i, p: (b, i, blk0 + p)),
                  pl.BlockSpec((None, halo, D_WIDTH),
                               lambda b, i, p: (b, jnp.minimum((i + 1) * (tl // halo), l // halo - 1), blk0 + p)),
                  pl.BlockSpec((CONV_WIDTH, D_WIDTH), lambda b, i, p: (0, p))],
        out_specs=pl.BlockSpec((None, None, tl, D_WIDTH), lambda b, i, p: (p, b, i, 0)),
        out_shape=jax.ShapeDtypeStruct((3, bsz, l, D_WIDTH), jnp.float32),
        compiler_params=pltpu.CompilerParams(dimension_semantics=("parallel", "parallel", "parallel"),
                                             vmem_limit_bytes=VMEM_LIMIT_BYTES),
        name="gdn_prep",
    )(proj3, proj3, proj3, conv_w.astype(jnp.float32))


def _gdn_gates_kernel(x_ref, nega_ref, dtb_ref, o_ref):
    x = x_ref[...]
    z = x + dtb_ref[...]
    g = nega_ref[...] * (jnp.maximum(z, 0.0) + jnp.log1p(jnp.exp(-jnp.abs(z))))
    lane = lax.broadcasted_iota(jnp.int32, x.shape, 1)
    y = jnp.where(lane < 2 * D_HEADS, g, jax.nn.sigmoid(x))
    o_ref[...] = y.T[0:4 * D_HEADS, :]


def gdn_gates(proj3, a_log, dt_bias, *, tl=512):
    bsz, l, _ = proj3.shape
    pad = LANES - 2 * D_HEADS
    nega = jnp.pad(-jnp.exp(a_log.astype(jnp.float32)).reshape(1, -1), ((0, 0), (0, pad)))
    dtb = jnp.pad(dt_bias.astype(jnp.float32).reshape(1, -1), ((0, 0), (0, pad)))
    vec = pl.BlockSpec((1, LANES), lambda b, i: (0, 0))
    return pl.pallas_call(
        _gdn_gates_kernel,
        grid=(bsz, l // tl),
        in_specs=[pl.BlockSpec((None, tl, LANES), lambda b, i: (b, i, OD_AB_BLOCK)), vec, vec],
        out_specs=pl.BlockSpec((None, 4 * D_HEADS, tl), lambda b, i: (b, 0, i)),
        out_shape=jax.ShapeDtypeStruct((bsz, 4 * D_HEADS, l), jnp.float32),
        compiler_params=pltpu.CompilerParams(dimension_semantics=("parallel", "parallel"),
                                             vmem_limit_bytes=VMEM_LIMIT_BYTES),
        name="gdn_gates",
    )(proj3, nega, dtb)


def gdn_constants():
    import numpy as np
    c = CHUNK
    r = np.arange(c)[:, None]
    u = np.arange(c)[None, :]
    cum, incl, strict = [], [], []
    for direction in range(2):
        fwd = direction == 0
        cum.append(np.concatenate([(r <= u) if fwd else (r >= u), np.ones((c, c), bool)], axis=1))
        incl.append((u <= r) if fwd else (u >= r))
        strict.append((u < r) if fwd else (u > r))
    same = lambda b: (r // b) == (u // b)
    merges = [same(2 * b) & ~same(b) for b in (8, 16, 32)]
    f32 = jnp.float32
    return (jnp.asarray(np.stack(cum), jnp.bfloat16), jnp.asarray(np.stack(incl), f32),
            jnp.asarray(np.stack(strict), f32), jnp.asarray(same(8), f32), jnp.asarray(np.stack(merges), f32))


def _gdn_kernel(q_ref, k_ref, v_ref, g_ref, b_ref, cum_ref, incl_ref, strict_ref, d8_ref, mrg_ref, o_ref,
                s_sc, qd_sc, dec_sc, w_sc, u_sc, sin_sc, *, nc):
    direction = pl.program_id(2)

    @pl.when(pl.program_id(3) == 0)
    def _():
        s_sc[...] = jnp.zeros_like(s_sc)

    bf16 = jnp.bfloat16
    f32 = jnp.float32
    c = CHUNK
    hd = D_HEAD_DIM
    contract_last = (((1,), (1,)), ((), ()))
    contract_first = (((0,), (0,)), ((), ()))
    cumm = cum_ref[...]
    incl = incl_ref[...]
    strict = strict_ref[...]
    d8 = d8_ref[...]
    eye = (lax.broadcasted_iota(jnp.int32, (c, c), 0) == lax.broadcasted_iota(jnp.int32, (c, c), 1)).astype(f32)

    def mm(a, b):
        return jnp.dot(a.astype(bf16), b.astype(bf16), preferred_element_type=f32)

    def rep(x):
        return jnp.concatenate([x] * (hd // c), axis=1)

    ns = range(nc)
    rows = [slice(n * c, (n + 1) * c) for n in ns]
    q = [q_ref[r, :] for r in rows]
    k = [k_ref[r, :] for r in rows]
    v = [v_ref[r, :] for r in rows]
    kb = [x.astype(bf16) for x in k]
    kk = [lax.dot_general(x, x, contract_last, preferred_element_type=f32) for x in kb]
    qk = [lax.dot_general(q[n].astype(bf16), kb[n], contract_last, preferred_element_type=f32) for n in ns]
    grow = [jnp.broadcast_to(g_ref[:, r], (c, c)) for r in rows]
    ghi = [x.astype(bf16) for x in grow]
    glo = [(grow[n] - ghi[n].astype(f32)).astype(bf16) for n in ns]
    gm = [jnp.dot(ghi[n], cumm, preferred_element_type=f32) + jnp.dot(glo[n], cumm, preferred_element_type=f32)
          for n in ns]
    gam_row = [x[:, :c] for x in gm]
    tot = [x[:, c:] for x in gm]
    gam_col = [x.T for x in gam_row]
    beta_col = [jnp.broadcast_to(b_ref[:, r], (c, c)).T for r in rows]
    decay = [incl * jnp.exp(jnp.minimum(gam_col[n] - gam_row[n], 0.0)) for n in ns]
    a = [strict * beta_col[n] * kk[n] * decay[n] for n in ns]
    a0 = [x * d8 for x in a]
    n2 = [mm(x, x) for x in a0]
    n4 = [mm(x, x) for x in n2]
    t = [mm(eye - a0[n], eye + n2[n]) for n in ns]
    t = [mm(t[n], eye + n4[n]) for n in ns]
    for j in range(mrg_ref.shape[0]):
        p = [mm(a[n] * mrg_ref[j], t[n]) for n in ns]
        t = [t[n] - mm(t[n], p[n]) for n in ns]
    beta128 = [rep(x) for x in beta_col]
    egam128 = [rep(jnp.exp(x)) for x in gam_col]
    solb = [mm(t[n], jnp.concatenate([k[n] * beta128[n] * egam128[n], v[n] * beta128[n]], axis=1)).astype(bf16)
            for n in ns]
    av = [jnp.dot((qk[n] * decay[n]).astype(bf16), solb[n], preferred_element_type=f32) for n in ns]
    k_dec = [(k[n] * rep(jnp.exp(tot[n] - gam_col[n]))).astype(bf16) for n in ns]
    wu = [lax.dot_general(k_dec[n], solb[n], contract_first, preferred_element_type=f32) for n in ns]
    for n in ns:
        qd_sc[n] = (q[n] * egam128[n] - av[n][:, :hd]).astype(bf16)
        o_ref[rows[n], :] = av[n][:, hd:]
        w_sc[n] = wu[n][:, :hd].astype(bf16)
        u_sc[n] = wu[n][:, hd:]
        dec_sc[n] = rep(jnp.exp(tot[n][0:SUBLANES, :]))

    def body(ci, s):
        ce = ci + direction * (nc - 1 - 2 * ci)
        sb = s.astype(bf16)
        sin_sc[ce] = sb
        return s * dec_sc[ce][0:1] - jnp.dot(w_sc[ce], sb, preferred_element_type=f32) + u_sc[ce]

    s_sc[...] = lax.fori_loop(0, nc, body, s_sc[...])

    for n in range(nc):
        rows = slice(n * c, (n + 1) * c)
        o_ref[rows, :] += jnp.dot(qd_sc[n], sin_sc[n], preferred_element_type=f32)


def gdn_scan(qkv, gb):
    _, bsz, l, _ = qkv.shape
    hd = D_HEAD_DIM
    tt = min(GDN_TT, l)
    nt = l // tt
    nc = tt // CHUNK
    assert l % tt == 0 and tt % CHUNK == 0
    consts = gdn_constants()
    gb4 = gb.reshape(bsz, 4 * D_HEADS, 1, l)
    tidx = lambda d, i: i + d * (nt - 1 - 2 * i)
    qkv_spec = lambda p: pl.BlockSpec((None, None, tt, hd), lambda b, h, d, i: (p, b, tidx(d, i), h))
    row_spec = lambda off: pl.BlockSpec((None, None, 1, tt),
                                        lambda b, h, d, i: (b, off + d * D_HEADS + h, 0, tidx(d, i)))
    per_dir = lambda a: pl.BlockSpec((None,) + a.shape[1:], lambda b, h, d, i: (d,) + (0,) * (a.ndim - 1))
    whole = lambda a: pl.BlockSpec(a.shape, lambda b, h, d, i: (0,) * a.ndim)
    return pl.pallas_call(
        functools.partial(_gdn_kernel, nc=nc),
        grid=(bsz, D_HEADS, 2, nt),
        in_specs=[qkv_spec(0), qkv_spec(1), qkv_spec(2), row_spec(0), row_spec(2 * D_HEADS),
                  per_dir(consts[0]), per_dir(consts[1]), per_dir(consts[2]), whole(consts[3]), whole(consts[4])],
        out_specs=pl.BlockSpec((None, None, tt, hd), lambda b, h, d, i: (d, b, tidx(d, i), h)),
        out_shape=jax.ShapeDtypeStruct((2, bsz, l, D_WIDTH), jnp.float32),
        scratch_shapes=[pltpu.VMEM((hd, hd), jnp.float32),
                        pltpu.VMEM((nc, CHUNK, hd), jnp.bfloat16),
                        pltpu.VMEM((nc, SUBLANES, hd), jnp.float32),
                        pltpu.VMEM((nc, hd, hd), jnp.bfloat16),
                        pltpu.VMEM((nc, hd, hd), jnp.float32),
                        pltpu.VMEM((nc, hd, hd), jnp.bfloat16)],
        compiler_params=pltpu.CompilerParams(dimension_semantics=("parallel", "parallel", "parallel", "arbitrary"),
                                             vmem_limit_bytes=VMEM_LIMIT_BYTES),
        name="gdn_scan",
    )(qkv, qkv, qkv, gb4, gb4, *consts)


def gated_deltanet(proj3, conv_w, a_log, dt_bias, out_gain):
    o2 = gdn_scan(gdn_prep(proj3, conv_w), gdn_gates(proj3, a_log, dt_bias))
    return bidir_finalize(o2, proj3, OD_GATE_BLOCK, out_gain, name="gdn_finalize")


def ec_moe(h, w_router, w_gate, w_up, w_down):
    b, l, d = h.shape
    cap = EC_CAPACITY_FACTOR * l // N_EXPERTS
    logits = jnp.einsum('bld,de->ble', h, w_router).astype(jnp.float32)
    aff = jax.nn.softmax(logits, axis=-1)
    gate, idx = lax.top_k(jnp.swapaxes(aff, 1, 2), cap)
    xs = jax.vmap(lambda hb, ib: hb[ib])(h, idx)
    hid = jax.nn.silu(jnp.einsum('becd,edf->becf', xs, w_gate)) * jnp.einsum('becd,edf->becf', xs, w_up)
    out = jnp.einsum('becf,efd->becd', hid, w_down) * gate[..., None].astype(h.dtype)
    return jax.vmap(lambda ob, ib: jnp.zeros((l, d), ob.dtype).at[ib.reshape(-1)].add(ob.reshape(-1, d)))(out, idx)


def kernel(x, mix_norm, ffn_norm, ev_w_in, ev_w_out, a_lb_logits, a_out_norm, s5_lambda_re, s5_lambda_im, s5_log_step, s5_b_re, s5_b_im, s5_c_re, s5_c_im, s5_d, s5_glu_w, s5_glu_b, od_w_in, od_w_out, c_q_norm, c_k_norm, c_lambda, c_out_norm, rel_bias, d_conv_w, d_a_log, d_dt_bias, d_out_norm, moe_router, moe_w_gate, moe_w_up, moe_w_down):
    bsz, l, d = x.shape
    n = bsz * l
    p = jax.nn.softmax(a_lb_logits.astype(jnp.float32), axis=0)
    cum = jnp.cumsum(p, axis=0)
    lower_bounds = cum - cum[0:1]
    bias5 = rel_bias_tiles(rel_bias, ATT_T)
    for layer in range(DEPTH):
        j = layer // 2
        if layer % 2 == 0:
            proj = norm_matmul(x.reshape(n, d), mix_norm[layer], ev_w_in[j]).reshape(bsz, l, -1)
            u_b = proj[..., 5 * A_WIDTH:]
            o_a = hgrn2_mixer(proj, lower_bounds[j], a_out_norm[j])
            u_tb = jnp.transpose(u_b, (1, 0, 2)).reshape(l * bsz, B_WIDTH)
            o_b = s5_mixer_tb(u_tb, bsz, s5_lambda_re[j], s5_lambda_im[j], s5_log_step[j], s5_b_re[j], s5_b_im[j],
                              s5_c_re[j], s5_c_im[j], s5_d[j], s5_glu_w[j], s5_glu_b[j])
            o_b = jnp.transpose(o_b.reshape(l, bsz, B_WIDTH), (1, 0, 2))
            mixed = jnp.concatenate([o_a, o_b], axis=-1)
            x = matmul_residual(mixed.reshape(n, -1), ev_w_out[j], x.reshape(n, d)).reshape(bsz, l, d)
        else:
            o2 = 3 * C_WIDTH + 3 * D_WIDTH
            o4 = o2 + 4 * D_HEADS
            w = od_w_in[j]
            w_in = jnp.concatenate([w[:, :o2], w[:, o4:], w[:, o2:o4],
                                    jnp.zeros((d, OD_COLS - w.shape[1]), w.dtype)], axis=1)
            proj = norm_matmul(x.reshape(n, d), mix_norm[layer], w_in, tn=1280).reshape(bsz, l, OD_COLS)
            o_c = diff_attention(proj, c_q_norm[j], c_k_norm[j], c_lambda[j], c_out_norm[j], bias5, layer)
            o_d = gated_deltanet(proj, d_conv_w[j], d_a_log[j], d_dt_bias[j], d_out_norm[j])
            mixed = jnp.concatenate([o_c, o_d], axis=-1)
            x = matmul_residual(mixed.reshape(n, -1), od_w_out[j], x.reshape(n, d)).reshape(bsz, l, d)
        x = x + ec_moe(rmsnorm(x, ffn_norm[layer]), moe_router[layer], moe_w_gate[layer],
                       moe_w_up[layer], moe_w_down[layer])
    return x
```

```python
import functools
import math

import jax
import jax.numpy as jnp
from jax import lax
from jax.experimental import pallas as pl
from jax.experimental.pallas import tpu as pltpu

D_MODEL = 1024
DEPTH = 4
MIX_WIDTH = D_MODEL
A_WIDTH = MIX_WIDTH // 2
A_HEAD_DIM = 128
A_HEADS = A_WIDTH // A_HEAD_DIM
B_WIDTH = MIX_WIDTH - A_WIDTH
S5_GROUP = 16
S5_GROUPS = B_WIDTH // S5_GROUP
S5_STATE = 64
C_WIDTH = MIX_WIDTH // 2
C_HEAD_DIM = 64
C_HEADS = C_WIDTH // (2 * C_HEAD_DIM)
C_V_DIM = 2 * C_HEAD_DIM
D_WIDTH = MIX_WIDTH - C_WIDTH
D_HEAD_DIM = 128
D_HEADS = D_WIDTH // D_HEAD_DIM
CONV_WIDTH = 5
N_EXPERTS = 16
EXPERT_FF = 2 * D_MODEL
EC_CAPACITY_FACTOR = 2
REL_BUCKETS = 32
REL_MAX_DIST = 128
CHUNK = 64
Q_BLOCK = 128
EPS = 1e-6

VMEM_LIMIT_BYTES = 48 * 1024 * 1024


def _norm_matmul_kernel(x_ref, g_ref, w_ref, o_ref):
    x = x_ref[...]
    y = x * lax.rsqrt(jnp.mean(x * x, axis=-1, keepdims=True) + EPS) * g_ref[...]
    o_ref[...] = jnp.dot(y.astype(jnp.bfloat16), w_ref[...], preferred_element_type=jnp.float32)


def norm_matmul(x, gain, w, *, tm=512, tn=512):
    n, k = x.shape
    m = w.shape[1]
    tn = min(tn, m)
    assert n % tm == 0 and m % tn == 0
    return pl.pallas_call(
        _norm_matmul_kernel,
        grid=(n // tm, m // tn),
        in_specs=[pl.BlockSpec((tm, k), lambda i, j: (i, 0)),
                  pl.BlockSpec((1, k), lambda i, j: (0, 0)),
                  pl.BlockSpec((k, tn), lambda i, j: (0, j))],
        out_specs=pl.BlockSpec((tm, tn), lambda i, j: (i, j)),
        out_shape=jax.ShapeDtypeStruct((n, m), jnp.float32),
        compiler_params=pltpu.CompilerParams(dimension_semantics=("parallel", "parallel"),
                                             vmem_limit_bytes=VMEM_LIMIT_BYTES),
        name="norm_matmul",
    )(x, gain.reshape(1, k).astype(jnp.float32), w.astype(jnp.bfloat16))


def _matmul_res_kernel(a_ref, w_ref, r_ref, o_ref):
    o_ref[...] = r_ref[...] + jnp.dot(a_ref[...].astype(jnp.bfloat16), w_ref[...],
                                      preferred_element_type=jnp.float32)


def matmul_residual(a, w, res, *, tm=512):
    n, k = a.shape
    m = w.shape[1]
    return pl.pallas_call(
        _matmul_res_kernel,
        grid=(n // tm,),
        in_specs=[pl.BlockSpec((tm, k), lambda i: (i, 0)),
                  pl.BlockSpec((k, m), lambda i: (0, 0)),
                  pl.BlockSpec((tm, m), lambda i: (i, 0))],
        out_specs=pl.BlockSpec((tm, m), lambda i: (i, 0)),
        out_shape=jax.ShapeDtypeStruct((n, m), jnp.float32),
        compiler_params=pltpu.CompilerParams(dimension_semantics=("parallel",),
                                             vmem_limit_bytes=VMEM_LIMIT_BYTES),
        name="matmul_residual",
    )(a, w.astype(jnp.bfloat16), res)


def rmsnorm(x, g):
    xf = x.astype(jnp.float32)
    y = xf * lax.rsqrt(jnp.mean(xf * xf, axis=-1, keepdims=True) + EPS)
    return (y * g.astype(jnp.float32)).astype(x.dtype)


def l2norm(x):
    return x * lax.rsqrt(jnp.sum(x * x, axis=-1, keepdims=True) + EPS)


def _heads(t, n_heads):
    b, l, w = t.shape
    return jnp.transpose(t.reshape(b, l, n_heads, w // n_heads), (0, 2, 1, 3))


def _flip(t):
    return jnp.flip(t, axis=2)


def _to_chunks(t):
    b, h, l = t.shape[:3]
    t = t.reshape((b, h, l // CHUNK, CHUNK) + t.shape[3:])
    return jnp.moveaxis(t, 2, 0)


def _from_chunks(t):
    t = jnp.moveaxis(t, 0, 2)
    b, h, nc, c = t.shape[:4]
    return t.reshape((b, h, nc * c) + t.shape[4:])


HG_LEVELS = tuple(CHUNK >> (i + 1) for i in range(CHUNK.bit_length() - 1))
HG_TOT_ROWS = 8
HG_TT = 512


def hgrn2_constants():
    import numpy as np
    c = CHUNK
    r = np.arange(c)[:, None]
    u = np.arange(c)[None, :]
    stacks, masks = [], []
    for direction in range(2):
        fwd = direction == 0
        lvl_masks = []
        for m in HG_LEVELS:
            blk = r // (2 * m)
            later = (r % (2 * m)) >= m
            lvl_masks.append((blk == blk.T) & (later & ~later.T if fwd else ~later & later.T))
        stacks.append(np.concatenate([(u <= r) if fwd else (u >= r), np.ones((HG_TOT_ROWS, c), bool)], axis=0))
        masks.append(np.stack(lvl_masks))
    return (jnp.asarray(np.stack(stacks), jnp.bfloat16), jnp.asarray(np.stack(masks), jnp.float32))


def _hgrn2_kernel(q_ref, f_ref, v_ref, loglb_ref, log1mlb_ref, onemlb_ref, ast_ref, mask_ref, o_ref,
                  st_sc, qd_sc, dec_sc, upd_sc, sin_sc, *, nc):
    direction = pl.program_id(2)

    @pl.when(pl.program_id(3) == 0)
    def _():
        st_sc[...] = jnp.zeros_like(st_sc)

    bf16 = jnp.bfloat16
    f32 = jnp.float32
    c = CHUNK
    hd = A_HEAD_DIM
    dirf = direction.astype(f32)
    loglb = loglb_ref[...]
    log1mlb = log1mlb_ref[...]
    onemlb = onemlb_ref[...]
    ast = ast_ref[...]
    contract_last = (((1,), (1,)), ((), ()))
    contract_first = (((0,), (0,)), ((), ()))

    ns = range(nc)
    rows = [slice(n * c, (n + 1) * c) for n in ns]
    z = [f_ref[r, :] for r in rows]
    v = [v_ref[r, :] for r in rows]
    qr = [q_ref[r, :] for r in rows]
    q = [x * jax.nn.sigmoid(x) for x in qr]
    e = [jnp.exp(-jnp.abs(x)) for x in z]
    cc = [log1mlb + jnp.minimum(z[n], 0.0) - jnp.log1p(e[n]) for n in ns]
    lf = [jnp.maximum(loglb, x) + jnp.log1p(jnp.exp(-jnp.abs(loglb - x))) for x in cc]
    k = [onemlb * jnp.where(z[n] >= 0, e[n], 1.0) / (1.0 + e[n]) for n in ns]
    hi = [x.astype(bf16) for x in lf]
    lo = [(lf[n] - hi[n].astype(f32)).astype(bf16) for n in ns]
    d = [jnp.dot(ast, hi[n], preferred_element_type=f32) + jnp.dot(ast, lo[n], preferred_element_type=f32)
         for n in ns]
    cum = [x[0:c] for x in d]
    tot = [x[c:c + HG_TOT_ROWS] for x in d]
    ref = [cum[n] - dirf * lf[n] for n in ns]
    attn = [jnp.zeros((c, c), f32) for _ in ns]
    for li, m in enumerate(HG_LEVELS):
        nb = c // (2 * m)
        split = [jnp.broadcast_to(x.reshape(nb, 2 * m, hd)[:, m - 1:m, :], (nb, 2 * m, hd)).reshape(c, hd)
                 for x in ref]
        x = [jnp.exp(-jnp.abs(cum[n] - split[n])) for n in ns]
        s = [lax.dot_general((q[n] * x[n]).astype(bf16), (k[n] * x[n]).astype(bf16), contract_last,
                             preferred_element_type=f32) for n in ns]
        attn = [attn[n] + mask_ref[li] * s[n] for n in ns]
    vb = [x.astype(bf16) for x in v]
    intra = [jnp.dot(attn[n].astype(bf16), vb[n], preferred_element_type=f32) for n in ns]
    upd = [lax.dot_general(vb[n], (k[n] * jnp.exp(tot[n][0:1] - cum[n])).astype(bf16), contract_first,
                           preferred_element_type=f32) for n in ns]
    for n in ns:
        o_ref[rows[n], :] = intra[n] + jnp.sum(q[n] * k[n], axis=-1, keepdims=True) * v[n]
        qd_sc[n] = (q[n] * jnp.exp(cum[n])).astype(bf16)
        dec_sc[n] = jnp.exp(tot[n])
        upd_sc[n] = upd[n]

    def body(ci, st):
        ce = ci + direction * (nc - 1 - 2 * ci)
        sin_sc[ce] = st.astype(bf16)
        return st * dec_sc[ce][0:1] + upd_sc[ce]

    st_sc[...] = lax.fori_loop(0, nc, body, st_sc[...])

    for n in range(nc):
        rows = slice(n * c, (n + 1) * c)
        o_ref[rows, :] += lax.dot_general(qd_sc[n], sin_sc[n], contract_last, preferred_element_type=f32)


def hgrn2_scan(proj3, lb):
    bsz, l, _ = proj3.shape
    hd = A_HEAD_DIM
    tt = min(HG_TT, l)
    nt = l // tt
    assert l % tt == 0 and tt % CHUNK == 0
    ast, masks = hgrn2_constants()
    lb = lb.astype(jnp.float32)
    vecs = [jnp.log(lb).reshape(2, 1, A_WIDTH), jnp.log1p(-lb).reshape(2, 1, A_WIDTH), (1.0 - lb).reshape(2, 1, A_WIDTH)]
    tidx = lambda d, i: i + d * (nt - 1 - 2 * i)
    vec = pl.BlockSpec((None, 1, hd), lambda b, h, d, i: (d, 0, h))
    return pl.pallas_call(
        functools.partial(_hgrn2_kernel, nc=tt // CHUNK),
        grid=(bsz, A_HEADS, 2, nt),
        in_specs=[pl.BlockSpec((None, tt, hd), lambda b, h, d, i: (b, tidx(d, i), h)),
                  pl.BlockSpec((None, tt, hd), lambda b, h, d, i: (b, tidx(d, i), (1 + d) * A_HEADS + h)),
                  pl.BlockSpec((None, tt, hd), lambda b, h, d, i: (b, tidx(d, i), 3 * A_HEADS + h)),
                  vec, vec, vec,
                  pl.BlockSpec((None,) + ast.shape[1:], lambda b, h, d, i: (d, 0, 0)),
                  pl.BlockSpec((None,) + masks.shape[1:], lambda b, h, d, i: (d, 0, 0, 0))],
        out_specs=pl.BlockSpec((None, None, tt, hd), lambda b, h, d, i: (d, b, tidx(d, i), h)),
        out_shape=jax.ShapeDtypeStruct((2, bsz, l, A_WIDTH), jnp.float32),
        scratch_shapes=[pltpu.VMEM((hd, hd), jnp.float32),
                        pltpu.VMEM((tt // CHUNK, CHUNK, hd), jnp.bfloat16),
                        pltpu.VMEM((tt // CHUNK, HG_TOT_ROWS, hd), jnp.float32),
                        pltpu.VMEM((tt // CHUNK, hd, hd), jnp.float32),
                        pltpu.VMEM((tt // CHUNK, hd, hd), jnp.bfloat16)],
        compiler_params=pltpu.CompilerParams(dimension_semantics=("parallel", "parallel", "parallel", "arbitrary"),
                                             vmem_limit_bytes=VMEM_LIMIT_BYTES),
        name="hgrn2_scan",
    )(proj3, proj3, proj3, *vecs, ast, masks)


def _hgrn2_final_kernel(of_ref, ob_ref, g_ref, gain_ref, o_ref):
    o = of_ref[...] + ob_ref[...]
    y = o * lax.rsqrt(jnp.mean(o * o, axis=-1, keepdims=True) + EPS) * gain_ref[...]
    g = g_ref[...]
    o_ref[...] = y * (g * jax.nn.sigmoid(g))


def bidir_finalize(o2, proj3, gate_block, out_gain, *, name, tm=1024):
    _, bsz, l, w = o2.shape
    hd = 128
    tm = min(tm, l)
    return pl.pallas_call(
        _hgrn2_final_kernel,
        grid=(bsz, l // tm, w // hd),
        in_specs=[pl.BlockSpec((None, None, tm, hd), lambda b, i, h: (0, b, i, h)),
                  pl.BlockSpec((None, None, tm, hd), lambda b, i, h: (1, b, i, h)),
                  pl.BlockSpec((None, tm, hd), lambda b, i, h: (b, i, gate_block + h)),
                  pl.BlockSpec((1, hd), lambda b, i, h: (0, 0))],
        out_specs=pl.BlockSpec((None, tm, hd), lambda b, i, h: (b, i, h)),
        out_shape=jax.ShapeDtypeStruct((bsz, l, w), jnp.float32),
        compiler_params=pltpu.CompilerParams(dimension_semantics=("parallel", "parallel", "parallel"),
                                             vmem_limit_bytes=VMEM_LIMIT_BYTES),
        name=name,
    )(o2, o2, proj3, out_gain.reshape(1, hd).astype(jnp.float32))


def hgrn2_mixer(proj3, lb, out_gain):
    return bidir_finalize(hgrn2_scan(proj3, lb), proj3, 4 * A_HEADS, out_gain, name="hgrn2_finalize")


S5_NS = S5_GROUPS * S5_STATE
S5_TT = 64
SUBLANES = 8


def _s5_scan_kernel(u_ref, win_ref, ar_ref, ai_ref, wout_ref, y_ref, bu_sc, xs_sc, st_sc, *, bsz, tt, reverse):
    @pl.when(pl.program_id(0) == 0)
    def _():
        st_sc[...] = jnp.zeros_like(st_sc)

    bu_sc[...] = jnp.dot(u_ref[...].astype(jnp.bfloat16), win_ref[...], preferred_element_type=jnp.float32)
    ar = jnp.broadcast_to(ar_ref[...], (bsz, S5_NS))
    ai = jnp.broadcast_to(ai_ref[...], (bsz, S5_NS))
    per = SUBLANES // bsz
    ngroups = tt // per

    def body(s, carry):
        xr, xi = carry
        p = (ngroups - 1 - s) if reverse else s
        base = pl.multiple_of(p * SUBLANES, SUBLANES)
        blk = bu_sc[pl.ds(base, SUBLANES), :]
        outs_r = [None] * per
        outs_i = [None] * per
        for ph in (range(per - 1, -1, -1) if reverse else range(per)):
            br = blk[ph * bsz:(ph + 1) * bsz, :S5_NS]
            bi = blk[ph * bsz:(ph + 1) * bsz, S5_NS:]
            xr, xi = ar * xr - ai * xi + br, ar * xi + ai * xr + bi
            outs_r[ph] = xr
            outs_i[ph] = xi
        xs_sc[pl.ds(base, SUBLANES), :S5_NS] = jnp.concatenate(outs_r, axis=0)
        xs_sc[pl.ds(base, SUBLANES), S5_NS:] = jnp.concatenate(outs_i, axis=0)
        return xr, xi

    xr, xi = lax.fori_loop(0, ngroups, body, (st_sc[0], st_sc[1]))
    st_sc[0] = xr
    st_sc[1] = xi
    y_ref[...] = jnp.dot(xs_sc[...].astype(jnp.bfloat16), wout_ref[...], preferred_element_type=jnp.float32)


def s5_scan(u_tb, win, ar, ai, wout, *, bsz, reverse):
    n = u_tb.shape[0]
    rows = S5_TT * bsz
    nt = n // rows
    assert n % rows == 0 and SUBLANES % bsz == 0
    idx = (lambda i: (nt - 1 - i, 0)) if reverse else (lambda i: (i, 0))
    const = lambda i: (0, 0)
    return pl.pallas_call(
        functools.partial(_s5_scan_kernel, bsz=bsz, tt=S5_TT, reverse=reverse),
        grid=(nt,),
        in_specs=[pl.BlockSpec((rows, B_WIDTH), idx),
                  pl.BlockSpec((B_WIDTH, 2 * S5_NS), const),
                  pl.BlockSpec((1, S5_NS), const),
                  pl.BlockSpec((1, S5_NS), const),
                  pl.BlockSpec((2 * S5_NS, B_WIDTH), const)],
        out_specs=pl.BlockSpec((rows, B_WIDTH), idx),
        out_shape=jax.ShapeDtypeStruct((n, B_WIDTH), jnp.float32),
        scratch_shapes=[pltpu.VMEM((rows, 2 * S5_NS), jnp.float32),
                        pltpu.VMEM((rows, 2 * S5_NS), jnp.float32),
                        pltpu.VMEM((2, bsz, S5_NS), jnp.float32)],
        compiler_params=pltpu.CompilerParams(dimension_semantics=("arbitrary",),
                                             vmem_limit_bytes=VMEM_LIMIT_BYTES),
        name="s5_scan_bwd" if reverse else "s5_scan_fwd",
    )(u_tb, win, ar, ai, wout)


def _s5_final_kernel(u_ref, yf_ref, yb_ref, d_ref, w_ref, b_ref, o_ref):
    y = d_ref[...] * u_ref[...] + yf_ref[...] + yb_ref[...]
    y = jax.nn.gelu(y)
    z = jnp.dot(y.astype(jnp.bfloat16), w_ref[...], preferred_element_type=jnp.float32) + b_ref[...]
    o_ref[...] = y * jax.nn.sigmoid(z)


def s5_finalize(u, yf, yb, d_skip, glu_w, glu_b, *, tm=512):
    n, w = u.shape
    row = pl.BlockSpec((tm, w), lambda i: (i, 0))
    vec = pl.BlockSpec((1, w), lambda i: (0, 0))
    return pl.pallas_call(
        _s5_final_kernel,
        grid=(n // tm,),
        in_specs=[row, row, row, vec, pl.BlockSpec((w, w), lambda i: (0, 0)), vec],
        out_specs=row,
        out_shape=jax.ShapeDtypeStruct((n, w), jnp.float32),
        compiler_params=pltpu.CompilerParams(dimension_semantics=("parallel",),
                                             vmem_limit_bytes=VMEM_LIMIT_BYTES),
        name="s5_finalize",
    )(u, yf, yb, d_skip.reshape(1, w).astype(jnp.float32), glu_w.astype(jnp.bfloat16),
      glu_b.reshape(1, w).astype(jnp.float32))


def s5_direction_params(lam_re, lam_im, log_step, b_re, b_im, c_re, c_im):
    step = jnp.exp(log_step)[:, None]
    mag = jnp.exp(lam_re * step)
    abar_re = mag * jnp.cos(lam_im * step)
    abar_im = mag * jnp.sin(lam_im * step)
    den = lam_re * lam_re + lam_im * lam_im
    fr = ((abar_re - 1.0) * lam_re + abar_im * lam_im) / den
    fi = (abar_im * lam_re - (abar_re - 1.0) * lam_im) / den
    bb_re = fr[..., None] * b_re - fi[..., None] * b_im
    bb_im = fr[..., None] * b_im + fi[..., None] * b_re
    eye = jnp.eye(S5_GROUPS, dtype=jnp.float32)
    win = jnp.concatenate([jnp.einsum('gnp,gh->gphn', bb, eye).reshape(B_WIDTH, S5_NS) for bb in (bb_re, bb_im)],
                          axis=1)
    wout = jnp.concatenate([jnp.einsum('gpn,gh->hngp', c, eye).reshape(S5_NS, B_WIDTH) for c in (c_re, -c_im)],
                           axis=0)
    return (win.astype(jnp.bfloat16), abar_re.reshape(1, S5_NS), abar_im.reshape(1, S5_NS),
            wout.astype(jnp.bfloat16))


def s5_mixer_tb(u_tb, bsz, lam_re, lam_im, log_step, b_re, b_im, c_re, c_im, d_skip, glu_w, glu_b):
    f32 = jnp.float32
    ys = []
    for direction in range(2):
        prm = s5_direction_params(lam_re[direction].astype(f32), lam_im[direction].astype(f32),
                                  log_step[direction].astype(f32), b_re[direction].astype(f32),
                                  b_im[direction].astype(f32), c_re[direction].astype(f32),
                                  c_im[direction].astype(f32))
        ys.append(s5_scan(u_tb, *prm, bsz=bsz, reverse=(direction == 1)))
    return s5_finalize(u_tb, ys[0], ys[1], d_skip, glu_w, glu_b)


def t5_bucket(rel):
    half = REL_BUCKETS // 2
    max_exact = half // 2
    base = jnp.where(rel > 0, half, 0)
    n = jnp.abs(rel)
    nf = jnp.maximum(n, 1).astype(jnp.float32)
    large = max_exact + (jnp.log(nf / max_exact) / math.log(REL_MAX_DIST / max_exact)
                         * (half - max_exact)).astype(jnp.int32)
    large = jnp.minimum(large, half - 1)
    return base + jnp.where(n < max_exact, n, large)


ATT_T = 512
LOG2E = math.log2(math.e)


def rel_bias_tiles(rel_bias, t):
    assert t >= REL_MAX_DIST
    table = rel_bias.astype(jnp.float32) * LOG2E
    tiles = []
    for d in (-1, 0, 1):
        c = table[t5_bucket(d * t + jnp.arange(-(t - 1), t))]
        w = jnp.concatenate([c, c[:1]], axis=0)
        m = jnp.tile(w, (t, 1))[:t * (2 * t - 1)].reshape(t, 2 * t - 1, -1)
        tiles.append(m[:, t - 1:2 * t - 1])
    far_neg = jnp.broadcast_to(table[t5_bucket(jnp.array(-2 * t))], tiles[0].shape)
    far_pos = jnp.broadcast_to(table[t5_bucket(jnp.array(2 * t))], tiles[0].shape)
    out = jnp.stack([far_neg] + tiles + [far_pos], axis=0)
    return jnp.transpose(out, (3, 0, 1, 2))


def _attn_prep_kernel(q_ref, k_ref, v_ref, qg_ref, kg_ref, q2_ref, kt_ref, vb_ref):
    lane = lax.broadcasted_iota(jnp.int32, q_ref.shape, 1)
    lo = lane < C_HEAD_DIM

    def halfnorm(x, g):
        sq = x * x
        s_lo = jnp.sum(jnp.where(lo, sq, 0.0), axis=-1, keepdims=True)
        s_hi = jnp.sum(jnp.where(lo, 0.0, sq), axis=-1, keepdims=True)
        ms = jnp.where(lo, s_lo, s_hi) * (1.0 / C_HEAD_DIM)
        return x * lax.rsqrt(ms + EPS) * g

    qn = halfnorm(q_ref[...], qg_ref[...]) * (C_HEAD_DIM ** -0.5 * LOG2E)
    kn = halfnorm(k_ref[...], kg_ref[...])
    q2_ref[0] = jnp.where(lo, qn, 0.0).astype(jnp.bfloat16)
    q2_ref[1] = jnp.where(lo, 0.0, qn).astype(jnp.bfloat16)
    kt_ref[...] = kn.T.astype(jnp.bfloat16)
    vb_ref[...] = v_ref[...].astype(jnp.bfloat16)


def attn_prep(proj3, q_gain, k_gain, *, tl=512):
    bsz, l, _ = proj3.shape
    hw = 2 * C_HEAD_DIM
    gq = jnp.tile(q_gain.astype(jnp.float32), 2).reshape(1, hw)
    gk = jnp.tile(k_gain.astype(jnp.float32), 2).reshape(1, hw)
    vec = pl.BlockSpec((1, hw), lambda b, h, i: (0, 0))
    return pl.pallas_call(
        _attn_prep_kernel,
        grid=(bsz, C_HEADS, l // tl),
        in_specs=[pl.BlockSpec((None, tl, hw), lambda b, h, i: (b, i, h)),
                  pl.BlockSpec((None, tl, hw), lambda b, h, i: (b, i, C_HEADS + h)),
                  pl.BlockSpec((None, tl, hw), lambda b, h, i: (b, i, 2 * C_HEADS + h)),
                  vec, vec],
        out_specs=[pl.BlockSpec((None, None, 2, tl, hw), lambda b, h, i: (b, h, 0, i, 0)),
                   pl.BlockSpec((None, None, hw, tl), lambda b, h, i: (b, h, 0, i)),
                   pl.BlockSpec((None, None, tl, hw), lambda b, h, i: (b, h, i, 0))],
        out_shape=[jax.ShapeDtypeStruct((bsz, C_HEADS, 2, l, hw), jnp.bfloat16),
                   jax.ShapeDtypeStruct((bsz, C_HEADS, hw, l), jnp.bfloat16),
                   jax.ShapeDtypeStruct((bsz, C_HEADS, l, hw), jnp.bfloat16)],
        compiler_params=pltpu.CompilerParams(dimension_semantics=("parallel", "parallel", "parallel"),
                                             vmem_limit_bytes=VMEM_LIMIT_BYTES),
        name="attn_prep",
    )(proj3, proj3, proj3, gq, gk)


ATT_ROWS = 64


def _attn_kernel(lam_ref, q2_ref, kt_ref, v_ref, bias_ref, g_ref, o_ref, m_sc, l_sc, acc_sc, s_sc, p_sc, a_sc,
                 *, t, nk, out_scale):
    qi = pl.program_id(2)
    q2 = q2_ref[...].reshape(2 * t, 2 * C_HEAD_DIM)
    m_sc[...] = jnp.full(m_sc.shape, -jnp.inf, jnp.float32)
    l_sc[...] = jnp.zeros_like(l_sc)
    acc_sc[...] = jnp.zeros_like(acc_sc)
    r = ATT_ROWS
    hw = 2 * C_HEAD_DIM

    def body(ki, carry):
        off = pl.multiple_of(ki * t, t)
        bidx = jnp.clip(ki - qi, -2, 2) + 2
        s_sc[...] = jnp.dot(q2, kt_ref[:, pl.ds(off, t)], preferred_element_type=jnp.float32)
        for g in range(2 * t // r):
            rows = slice(g * r, (g + 1) * r)
            brow = (g * r) % t
            s = s_sc[rows, :] + bias_ref[bidx, brow:brow + r, :]
            m_prev = m_sc[rows, :]
            m_new = jnp.maximum(m_prev, jnp.max(s, axis=-1, keepdims=True))
            alpha = jnp.exp2(m_prev - m_new)
            ps = [jnp.exp2(s[:, j * hw:(j + 1) * hw] - m_new) for j in range(t // hw)]
            l_sc[rows, :] = alpha * l_sc[rows, :] + jnp.sum(sum(ps), axis=-1, keepdims=True)
            m_sc[rows, :] = m_new
            a_sc[rows, :] = alpha
            for j in range(t // hw):
                p_sc[rows, j * hw:(j + 1) * hw] = ps[j].astype(jnp.bfloat16)
        acc_sc[...] = a_sc[...] * acc_sc[...] + jnp.dot(p_sc[...], v_ref[pl.ds(off, t), :],
                                                        preferred_element_type=jnp.float32)
        return carry

    lax.fori_loop(0, nk, body, 0)
    a = acc_sc[...] / l_sc[...]
    o = a[:t] - lam_ref[0] * a[t:]
    y = o * lax.rsqrt(jnp.mean(o * o, axis=-1, keepdims=True) + EPS)
    o_ref[...] = y * g_ref[...] * out_scale


def diff_attention(proj3, q_gain, k_gain, lam, out_gain, bias5, layer_idx):
    f32 = jnp.float32
    bsz, l, _ = proj3.shape
    t = ATT_T
    hw = 2 * C_HEAD_DIM
    lam_init = 0.8 - 0.6 * math.exp(-0.3 * layer_idx)
    lam_f = lam.astype(f32)
    lam_full = jnp.exp(jnp.sum(lam_f[0] * lam_f[1])) - jnp.exp(jnp.sum(lam_f[2] * lam_f[3])) + lam_init
    q2, kt, vb = attn_prep(proj3, q_gain, k_gain)
    return pl.pallas_call(
        functools.partial(_attn_kernel, t=t, nk=l // t, out_scale=1.0 - lam_init),
        grid=(bsz, C_HEADS, l // t),
        in_specs=[pl.BlockSpec(memory_space=pltpu.SMEM),
                  pl.BlockSpec((None, None, 2, t, hw), lambda b, h, i: (b, h, 0, i, 0)),
                  pl.BlockSpec((None, None, hw, l), lambda b, h, i: (b, h, 0, 0)),
                  pl.BlockSpec((None, None, l, hw), lambda b, h, i: (b, h, 0, 0)),
                  pl.BlockSpec((None, 5, t, t), lambda b, h, i: (h, 0, 0, 0)),
                  pl.BlockSpec((1, hw), lambda b, h, i: (0, 0))],
        out_specs=pl.BlockSpec((None, t, hw), lambda b, h, i: (b, i, h)),
        out_shape=jax.ShapeDtypeStruct((bsz, l, C_WIDTH), f32),
        scratch_shapes=[pltpu.VMEM((2 * t, hw), f32), pltpu.VMEM((2 * t, hw), f32), pltpu.VMEM((2 * t, hw), f32),
                        pltpu.VMEM((2 * t, t), f32), pltpu.VMEM((2 * t, t), jnp.bfloat16), pltpu.VMEM((2 * t, hw), f32)],
        compiler_params=pltpu.CompilerParams(dimension_semantics=("parallel", "parallel", "arbitrary"),
                                             vmem_limit_bytes=VMEM_LIMIT_BYTES),
        name="diff_attention",
    )(lam_full.reshape(1), q2, kt, vb, bias5, out_gain.reshape(1, hw).astype(f32))


GDN_TT = 512
LANES = 128
OD_QKV_BLOCK = 3 * C_WIDTH // LANES
OD_GATE_BLOCK = OD_QKV_BLOCK + 3 * D_WIDTH // LANES
OD_AB_BLOCK = OD_GATE_BLOCK + D_WIDTH // LANES
OD_COLS = 3840


def _gdn_prep_kernel(prev_ref, cur_ref, next_ref, w_ref, o_ref, *, tl, nl):
    i = pl.program_id(1)
    part = pl.program_id(2)
    prev = jnp.where(i > 0, prev_ref[...], 0.0)
    nxt = jnp.where(i < nl - 1, next_ref[...], 0.0)
    ext = jnp.concatenate([prev, cur_ref[...], nxt], axis=0)
    halo = prev.shape[0]
    acc = None
    for j in range(CONV_WIDTH):
        start = halo - CONV_WIDTH // 2 + j
        term = w_ref[j:j + 1, :] * ext[start:start + tl, :]
        acc = term if acc is None else acc + term
    y = acc * jax.nn.sigmoid(acc)
    scale = jnp.where(part == 0, D_HEAD_DIM ** -0.5, 1.0)
    heads = []
    for h in range(D_HEADS):
        yh = y[:, h * LANES:(h + 1) * LANES]
        heads.append(yh * (lax.rsqrt(jnp.sum(yh * yh, axis=-1, keepdims=True) + EPS) * scale))
    o_ref[...] = jnp.where(part < 2, jnp.concatenate(heads, axis=1), y)


def gdn_prep(proj3, conv_w, *, tl=512):
    bsz, l, _ = proj3.shape
    halo = SUBLANES
    nl = l // tl
    blk0 = OD_QKV_BLOCK * LANES // D_WIDTH
    return pl.pallas_call(
        functools.partial(_gdn_prep_kernel, tl=tl, nl=nl),
        grid=(bsz, nl, 3),
        in_specs=[pl.BlockSpec((None, halo, D_WIDTH), lambda b, i, p: (b, jnp.maximum(i * (tl // halo) - 1, 0), blk0 + p)),
                  pl.BlockSpec((None, tl, D_WIDTH), lambda b, i, p: (b, i, blk0 + p)),
                  pl.BlockSpec((None, halo, D_WIDTH),
                               lambda b, i, p: (b, jnp.minimum((i + 1) * (tl // halo), l // halo - 1), blk0 + p)),
                  pl.BlockSpec((CONV_WIDTH, D_WIDTH), lambda b, i, p: (0, p))],
        out_specs=pl.BlockSpec((None, None, tl, D_WIDTH), lambda b, i, p: (p, b, i, 0)),
        out_shape=jax.ShapeDtypeStruct((3, bsz, l, D_WIDTH), jnp.float32),
        compiler_params=pltpu.CompilerParams(dimension_semantics=("parallel", "parallel", "parallel"),
                                             vmem_limit_bytes=VMEM_LIMIT_BYTES),
        name="gdn_prep",
    )(proj3, proj3, proj3, conv_w.astype(jnp.float32))


def _gdn_gates_kernel(x_ref, nega_ref, dtb_ref, o_ref):
    x = x_ref[...]
    z = x + dtb_ref[...]
    g = nega_ref[...] * (jnp.maximum(z, 0.0) + jnp.log1p(jnp.exp(-jnp.abs(z))))
    lane = lax.broadcasted_iota(jnp.int32, x.shape, 1)
    y = jnp.where(lane < 2 * D_HEADS, g, jax.nn.sigmoid(x))
    o_ref[...] = y.T[0:4 * D_HEADS, :]


def gdn_gates(proj3, a_log, dt_bias, *, tl=512):
    bsz, l, _ = proj3.shape
    pad = LANES - 2 * D_HEADS
    nega = jnp.pad(-jnp.exp(a_log.astype(jnp.float32)).reshape(1, -1), ((0, 0), (0, pad)))
    dtb = jnp.pad(dt_bias.astype(jnp.float32).reshape(1, -1), ((0, 0), (0, pad)))
    vec = pl.BlockSpec((1, LANES), lambda b, i: (0, 0))
    return pl.pallas_call(
        _gdn_gates_kernel,
        grid=(bsz, l // tl),
        in_specs=[pl.BlockSpec((None, tl, LANES), lambda b, i: (b, i, OD_AB_BLOCK)), vec, vec],
        out_specs=pl.BlockSpec((None, 4 * D_HEADS, tl), lambda b, i: (b, 0, i)),
        out_shape=jax.ShapeDtypeStruct((bsz, 4 * D_HEADS, l), jnp.float32),
        compiler_params=pltpu.CompilerParams(dimension_semantics=("parallel", "parallel"),
                                             vmem_limit_bytes=VMEM_LIMIT_BYTES),
        name="gdn_gates",
    )(proj3, nega, dtb)


def gdn_constants():
    import numpy as np
    c = CHUNK
    r = np.arange(c)[:, None]
    u = np.arange(c)[None, :]
    cum, incl, strict = [], [], []
    for direction in range(2):
        fwd = direction == 0
        cum.append(np.concatenate([(r <= u) if fwd else (r >= u), np.ones((c, c), bool)], axis=1))
        incl.append((u <= r) if fwd else (u >= r))
        strict.append((u < r) if fwd else (u > r))
    same = lambda b: (r // b) == (u // b)
    merges = [same(2 * b) & ~same(b) for b in (8, 16, 32)]
    f32 = jnp.float32
    return (jnp.asarray(np.stack(cum), jnp.bfloat16), jnp.asarray(np.stack(incl), f32),
            jnp.asarray(np.stack(strict), f32), jnp.asarray(same(8), f32), jnp.asarray(np.stack(merges), f32))


def _gdn_kernel(q_ref, k_ref, v_ref, g_ref, b_ref, cum_ref, incl_ref, strict_ref, d8_ref, mrg_ref, o_ref,
                s_sc, qd_sc, dec_sc, w_sc, u_sc, sin_sc, *, nc):
    direction = pl.program_id(2)

    @pl.when(pl.program_id(3) == 0)
    def _():
        s_sc[...] = jnp.zeros_like(s_sc)

    bf16 = jnp.bfloat16
    f32 = jnp.float32
    c = CHUNK
    hd = D_HEAD_DIM
    contract_last = (((1,), (1,)), ((), ()))
    contract_first = (((0,), (0,)), ((), ()))
    cumm = cum_ref[...]
    incl = incl_ref[...]
    strict = strict_ref[...]
    d8 = d8_ref[...]
    eye = (lax.broadcasted_iota(jnp.int32, (c, c), 0) == lax.broadcasted_iota(jnp.int32, (c, c), 1)).astype(f32)

    def mm(a, b):
        return jnp.dot(a.astype(bf16), b.astype(bf16), preferred_element_type=f32)

    def rep(x):
        return jnp.concatenate([x] * (hd // c), axis=1)

    ns = range(nc)
    rows = [slice(n * c, (n + 1) * c) for n in ns]
    q = [q_ref[r, :] for r in rows]
    k = [k_ref[r, :] for r in rows]
    v = [v_ref[r, :] for r in rows]
    kb = [x.astype(bf16) for x in k]
    kk = [lax.dot_general(x, x, contract_last, preferred_element_type=f32) for x in kb]
    qk = [lax.dot_general(q[n].astype(bf16), kb[n], contract_last, preferred_element_type=f32) for n in ns]
    grow = [jnp.broadcast_to(g_ref[:, r], (c, c)) for r in rows]
    ghi = [x.astype(bf16) for x in grow]
    glo = [(grow[n] - ghi[n].astype(f32)).astype(bf16) for n in ns]
    gm = [jnp.dot(ghi[n], cumm, preferred_element_type=f32) + jnp.dot(glo[n], cumm, preferred_element_type=f32)
          for n in ns]
    gam_row = [x[:, :c] for x in gm]
    tot = [x[:, c:] for x in gm]
    gam_col = [x.T for x in gam_row]
    beta_col = [jnp.broadcast_to(b_ref[:, r], (c, c)).T for r in rows]
    decay = [incl * jnp.exp(jnp.minimum(gam_col[n] - gam_row[n], 0.0)) for n in ns]
    a = [strict * beta_col[n] * kk[n] * decay[n] for n in ns]
    a0 = [x * d8 for x in a]
    n2 = [mm(x, x) for x in a0]
    n4 = [mm(x, x) for x in n2]
    t = [mm(eye - a0[n], eye + n2[n]) for n in ns]
    t = [mm(t[n], eye + n4[n]) for n in ns]
    for j in range(mrg_ref.shape[0]):
        p = [mm(a[n] * mrg_ref[j], t[n]) for n in ns]
        t = [t[n] - mm(t[n], p[n]) for n in ns]
    beta128 = [rep(x) for x in beta_col]
    egam128 = [rep(jnp.exp(x)) for x in gam_col]
    solb = [mm(t[n], jnp.concatenate([k[n] * beta128[n] * egam128[n], v[n] * beta128[n]], axis=1)).astype(bf16)
            for n in ns]
    av = [jnp.dot((qk[n] * decay[n]).astype(bf16), solb[n], preferred_element_type=f32) for n in ns]
    k_dec = [(k[n] * rep(jnp.exp(tot[n] - gam_col[n]))).astype(bf16) for n in ns]
    wu = [lax.dot_general(k_dec[n], solb[n], contract_first, preferred_element_type=f32) for n in ns]
    for n in ns:
        qd_sc[n] = (q[n] * egam128[n] - av[n][:, :hd]).astype(bf16)
        o_ref[rows[n], :] = av[n][:, hd:]
        w_sc[n] = wu[n][:, :hd].astype(bf16)
        u_sc[n] = wu[n][:, hd:]
        dec_sc[n] = rep(jnp.exp(tot[n][0:SUBLANES, :]))

    def body(ci, s):
        ce = ci + direction * (nc - 1 - 2 * ci)
        sb = s.astype(bf16)
        sin_sc[ce] = sb
        return s * dec_sc[ce][0:1] - jnp.dot(w_sc[ce], sb, preferred_element_type=f32) + u_sc[ce]

    s_sc[...] = lax.fori_loop(0, nc, body, s_sc[...])

    for n in range(nc):
        rows = slice(n * c, (n + 1) * c)
        o_ref[rows, :] += jnp.dot(qd_sc[n], sin_sc[n], preferred_element_type=f32)


def gdn_scan(qkv, gb):
    _, bsz, l, _ = qkv.shape
    hd = D_HEAD_DIM
    tt = min(GDN_TT, l)
    nt = l // tt
    nc = tt // CHUNK
    assert l % tt == 0 and tt % CHUNK == 0
    consts = gdn_constants()
    gb4 = gb.reshape(bsz, 4 * D_HEADS, 1, l)
    tidx = lambda d, i: i + d * (nt - 1 - 2 * i)
    qkv_spec = lambda p: pl.BlockSpec((None, None, tt, hd), lambda b, h, d, i: (p, b, tidx(d, i), h))
    row_spec = lambda off: pl.BlockSpec((None, None, 1, tt),
                                        lambda b, h, d, i: (b, off + d * D_HEADS + h, 0, tidx(d, i)))
    per_dir = lambda a: pl.BlockSpec((None,) + a.shape[1:], lambda b, h, d, i: (d,) + (0,) * (a.ndim - 1))
    whole = lambda a: pl.BlockSpec(a.shape, lambda b, h, d, i: (0,) * a.ndim)
    return pl.pallas_call(
        functools.partial(_gdn_kernel, nc=nc),
        grid=(bsz, D_HEADS, 2, nt),
        in_specs=[qkv_spec(0), qkv_spec(1), qkv_spec(2), row_spec(0), row_spec(2 * D_HEADS),
                  per_dir(consts[0]), per_dir(consts[1]), per_dir(consts[2]), whole(consts[3]), whole(consts[4])],
        out_specs=pl.BlockSpec((None, None, tt, hd), lambda b, h, d, i: (d, b, tidx(d, i), h)),
        out_shape=jax.ShapeDtypeStruct((2, bsz, l, D_WIDTH), jnp.float32),
        scratch_shapes=[pltpu.VMEM((hd, hd), jnp.float32),
                        pltpu.VMEM((nc, CHUNK, hd), jnp.bfloat16),
                        pltpu.VMEM((nc, SUBLANES, hd), jnp.float32),
                        pltpu.VMEM((nc, hd, hd), jnp.bfloat16),
                        pltpu.VMEM((nc, hd, hd), jnp.float32),
                        pltpu.VMEM((nc, hd, hd), jnp.bfloat16)],
        compiler_params=pltpu.CompilerParams(dimension_semantics=("parallel", "parallel", "parallel", "arbitrary"),
                                             vmem_limit_bytes=VMEM_LIMIT_BYTES),
        name="gdn_scan",
    )(qkv, qkv, qkv, gb4, gb4, *consts)


def gated_deltanet(proj3, conv_w, a_log, dt_bias, out_gain):
    o2 = gdn_scan(gdn_prep(proj3, conv_w), gdn_gates(proj3, a_log, dt_bias))
    return bidir_finalize(o2, proj3, OD_GATE_BLOCK, out_gain, name="gdn_finalize")


MOE_TT = 512
MOE_SUB = 128
MOE_ROWS = 256
MOE_SLAB = 256
MOE_VMEM_LIMIT_BYTES = 56 * 1024 * 1024


def _router_kernel(x_ref, g_ref, wr_ref, h_ref, aff_ref):
    x = x_ref[...]
    h = (x * lax.rsqrt(jnp.mean(x * x, axis=-1, keepdims=True) + EPS) * g_ref[...]).astype(jnp.bfloat16)
    h_ref[...] = h
    logits = jnp.dot(h, wr_ref[...], preferred_element_type=jnp.float32)
    lane = lax.broadcasted_iota(jnp.int32, logits.shape, 1)
    logits = jnp.where(lane < N_EXPERTS, logits, -jnp.inf)
    p = jnp.exp(logits - jnp.max(logits, axis=-1, keepdims=True))
    aff = p / jnp.sum(p, axis=-1, keepdims=True)
    aff_ref[...] = aff.T[0:N_EXPERTS, :]


def moe_route(x, gain, w_router, *, tm=MOE_TT):
    bsz, l, d = x.shape
    wr = jnp.pad(w_router.astype(jnp.bfloat16), ((0, 0), (0, LANES - N_EXPERTS)))
    return pl.pallas_call(
        _router_kernel,
        grid=(bsz, l // tm),
        in_specs=[pl.BlockSpec((None, tm, d), lambda b, i: (b, i, 0)),
                  pl.BlockSpec((1, d), lambda b, i: (0, 0)),
                  pl.BlockSpec((d, LANES), lambda b, i: (0, 0))],
        out_specs=[pl.BlockSpec((None, tm, d), lambda b, i: (b, i, 0)),
                   pl.BlockSpec((None, N_EXPERTS, tm), lambda b, i: (b, 0, i))],
        out_shape=[jax.ShapeDtypeStruct((bsz, l, d), jnp.bfloat16),
                   jax.ShapeDtypeStruct((bsz, N_EXPERTS, l), jnp.float32)],
        compiler_params=pltpu.CompilerParams(dimension_semantics=("parallel", "parallel"),
                                             vmem_limit_bytes=VMEM_LIMIT_BYTES),
        name="moe_router",
    )(x, gain.reshape(1, d).astype(jnp.float32), wr)


def _select_kernel(aff_ref, pre_ref, smap_ref, gate_ref, cnt_ref, *, cap, tt):
    f32 = jnp.float32
    bf16 = jnp.bfloat16
    aff = aff_ref[...]
    e, l = aff.shape
    nl = l // LANES
    tiles = [slice(j * LANES, (j + 1) * LANES) for j in range(nl)]
    bits = pltpu.bitcast(aff, jnp.int32)
    bt = [bits[:, s] for s in tiles]

    def lane_total(x):
        return jnp.broadcast_to(jnp.sum(x, axis=-1, keepdims=True), (e, LANES))

    def search(i, thr):
        cand = thr | jnp.left_shift(jnp.int32(1), 30 - i)
        acc = jnp.zeros((e, LANES), jnp.int32)
        for x in bt:
            acc = acc + (x >= cand).astype(jnp.int32)
        return jnp.where(lane_total(acc) >= cap, cand, thr)

    thr = lax.fori_loop(0, 31, search, jnp.zeros((e, LANES), jnp.int32))
    gt = [x > thr for x in bt]
    eq = [x == thr for x in bt]
    acc = jnp.zeros((e, LANES), jnp.int32)
    for x in gt:
        acc = acc + x.astype(jnp.int32)
    need = (cap - lane_total(acc)).astype(f32)

    pre = pre_ref[...]

    def prefix(flags):
        outs = [jnp.dot(jnp.where(x, 1.0, 0.0).astype(bf16), pre, preferred_element_type=f32) for x in flags]
        carry = jnp.zeros((e, LANES), f32)
        res = []
        for o in outs:
            res.append(o[:, :LANES] + carry)
            carry = carry + o[:, LANES:]
        return res, [o[:, LANES:] for o in outs]

    rank_eq, _ = prefix(eq)
    sel = [jnp.logical_or(gt[j], jnp.logical_and(eq[j], rank_eq[j] < need)) for j in range(nl)]
    pos, totals = prefix(sel)
    lane = lax.broadcasted_iota(jnp.int32, (e, LANES), 1)
    cnt = jnp.zeros((e, LANES), f32)
    per = tt // LANES
    for j in range(nl):
        smap_ref[:, tiles[j]] = jnp.where(sel[j], pos[j], -1.0)
        gate_ref[:, tiles[j]] = jnp.where(sel[j], aff[:, tiles[j]], 0.0)
        cnt = cnt + jnp.where(lane == j // per, totals[j], 0.0)
    cnt_ref[...] = cnt


def moe_select(aff, cap, *, tt=MOE_TT):
    import numpy as np
    bsz, e, l = aff.shape
    assert l // tt <= LANES
    i = np.arange(LANES)
    pre = np.concatenate([i[:, None] < i[None, :], np.ones((LANES, LANES), bool)], axis=1)
    row = pl.BlockSpec((None, e, l), lambda b: (b, 0, 0))
    return pl.pallas_call(
        functools.partial(_select_kernel, cap=cap, tt=tt),
        grid=(bsz,),
        in_specs=[row, pl.BlockSpec((LANES, 2 * LANES), lambda b: (0, 0))],
        out_specs=[row, row, pl.BlockSpec((None, e, LANES), lambda b: (b, 0, 0))],
        out_shape=[jax.ShapeDtypeStruct((bsz, e, l), jnp.float32), jax.ShapeDtypeStruct((bsz, e, l), jnp.float32),
                   jax.ShapeDtypeStruct((bsz, e, LANES), jnp.float32)],
        compiler_params=pltpu.CompilerParams(dimension_semantics=("parallel",),
                                             vmem_limit_bytes=VMEM_LIMIT_BYTES),
        name="moe_select",
    )(aff, jnp.asarray(pre, jnp.bfloat16))


def _slot_one_hot(pos, base, n):
    slot = (base + lax.broadcasted_iota(jnp.int32, (MOE_SUB, n), 0)).astype(jnp.float32)
    return jnp.where(pos == slot, 1.0, 0.0).astype(jnp.bfloat16)


def _expert_kernel(cs_ref, h_ref, smap_ref, gate_ref, wg_ref, wu_ref, wd_ref, o_ref, xs_sc, gs_sc, *, nj, cap):
    e = pl.program_id(0)
    b = pl.program_id(1)
    j = pl.program_id(2)
    f32 = jnp.float32
    bf16 = jnp.bfloat16

    @pl.when(j == 0)
    def _():
        xs_sc[...] = jnp.zeros_like(xs_sc)
        gs_sc[...] = jnp.zeros_like(gs_sc)

    base = (b * N_EXPERTS + e) * (nj + 1) + j
    c0 = cs_ref[base]
    c1 = cs_ref[base + 1]
    pos = smap_ref[...]
    gate = gate_ref[...]
    hb = h_ref[...]
    tt = hb.shape[0]

    def gather(st, carry):
        r0 = pl.multiple_of(st * MOE_SUB, MOE_SUB)
        oh = _slot_one_hot(pos, r0, tt)
        xs_sc[pl.ds(r0, MOE_SUB), :] += jnp.dot(oh, hb, preferred_element_type=f32)
        g = jnp.sum(oh.astype(f32) * gate, axis=-1, keepdims=True)
        gs_sc[pl.ds(r0, MOE_SUB), :] += jnp.broadcast_to(g, (MOE_SUB, LANES))
        return carry

    lax.fori_loop(c0 // MOE_SUB, (c1 + MOE_SUB - 1) // MOE_SUB, gather, 0)

    @pl.when(j == nj - 1)
    def _():
        rows_per = min(MOE_ROWS, cap)
        for r in range(cap // rows_per):
            rows = slice(r * rows_per, (r + 1) * rows_per)
            xb = xs_sc[rows, :].astype(bf16)
            g = jnp.dot(xb, wg_ref[...], preferred_element_type=f32)
            u = jnp.dot(xb, wu_ref[...], preferred_element_type=f32)
            hid = (g * jax.nn.sigmoid(g) * u).astype(bf16)
            out = jnp.dot(hid, wd_ref[...], preferred_element_type=f32)
            scale = jnp.concatenate([gs_sc[rows, :]] * (out.shape[1] // LANES), axis=1)
            o_ref[rows, :] = (out * scale).astype(bf16)


def moe_experts(hb, smap, gate, cs, w_gate, w_up, w_down, cap, *, tt=MOE_TT):
    bsz, l, d = hb.shape
    e, _, ff = w_gate.shape
    nj = l // tt
    smap4 = smap.reshape(bsz, e, 1, l)
    gate4 = gate.reshape(bsz, e, 1, l)
    tok = pl.BlockSpec((None, None, 1, tt), lambda ei, b, j, cs_ref: (b, ei, 0, j))
    grid_spec = pltpu.PrefetchScalarGridSpec(
        num_scalar_prefetch=1,
        grid=(e, bsz, nj),
        in_specs=[pl.BlockSpec((None, tt, d), lambda ei, b, j, cs_ref: (b, j, 0)), tok, tok,
                  pl.BlockSpec((None, d, ff), lambda ei, b, j, cs_ref: (ei, 0, 0)),
                  pl.BlockSpec((None, d, ff), lambda ei, b, j, cs_ref: (ei, 0, 0)),
                  pl.BlockSpec((None, ff, d), lambda ei, b, j, cs_ref: (ei, 0, 0))],
        out_specs=pl.BlockSpec((None, None, cap, d), lambda ei, b, j, cs_ref: (b, ei, 0, 0)),
        scratch_shapes=[pltpu.VMEM((cap, d), jnp.float32), pltpu.VMEM((cap, LANES), jnp.float32)])
    return pl.pallas_call(
        functools.partial(_expert_kernel, nj=nj, cap=cap),
        grid_spec=grid_spec,
        out_shape=jax.ShapeDtypeStruct((bsz, e, cap, d), jnp.bfloat16),
        compiler_params=pltpu.CompilerParams(dimension_semantics=("parallel", "arbitrary", "arbitrary"),
                                             vmem_limit_bytes=MOE_VMEM_LIMIT_BYTES),
        name="moe_experts",
    )(cs, hb, smap4, gate4, w_gate.astype(jnp.bfloat16), w_up.astype(jnp.bfloat16), w_down.astype(jnp.bfloat16))


def _combine_kernel(cs_ref, x_ref, smap_ref, ow_ref, y_ref, *, nj, tt):
    b = pl.program_id(0)
    e = pl.program_id(2)

    @pl.when(e == 0)
    def _():
        y_ref[...] = x_ref[...]

    contract_first = (((0,), (0,)), ((), ()))
    for j in range(nj):
        base = (b * N_EXPERTS + e) * (nj + 1) + j
        cols = slice(j * tt, (j + 1) * tt)
        pos = smap_ref[:, cols]

        def scatter(st, carry):
            r0 = pl.multiple_of(st * MOE_SUB, MOE_SUB)
            oh = _slot_one_hot(pos, r0, tt)
            y_ref[cols, :] += lax.dot_general(oh, ow_ref[pl.ds(r0, MOE_SUB), :], contract_first,
                                              preferred_element_type=jnp.float32)
            return carry

        lax.fori_loop(cs_ref[base] // MOE_SUB, (cs_ref[base + 1] + MOE_SUB - 1) // MOE_SUB, scatter, 0)


def moe_combine(x, smap, outw, cs, *, tt=MOE_TT):
    bsz, l, d = x.shape
    e, cap = outw.shape[1:3]
    nj = l // tt
    smap4 = smap.reshape(bsz, e, 1, l)
    grid_spec = pltpu.PrefetchScalarGridSpec(
        num_scalar_prefetch=1,
        grid=(bsz, d // MOE_SLAB, e),
        in_specs=[pl.BlockSpec((None, l, MOE_SLAB), lambda b, s, ei, cs_ref: (b, 0, s)),
                  pl.BlockSpec((None, None, 1, l), lambda b, s, ei, cs_ref: (b, ei, 0, 0)),
                  pl.BlockSpec((None, None, cap, MOE_SLAB), lambda b, s, ei, cs_ref: (b, ei, 0, s))],
        out_specs=pl.BlockSpec((None, l, MOE_SLAB), lambda b, s, ei, cs_ref: (b, 0, s)))
    return pl.pallas_call(
        functools.partial(_combine_kernel, nj=nj, tt=tt),
        grid_spec=grid_spec,
        out_shape=jax.ShapeDtypeStruct((bsz, l, d), jnp.float32),
        compiler_params=pltpu.CompilerParams(dimension_semantics=("parallel", "parallel", "arbitrary"),
                                             vmem_limit_bytes=MOE_VMEM_LIMIT_BYTES),
        name="moe_combine",
    )(cs, x, smap4, outw)


def ec_moe_layer(x, gain, w_router, w_gate, w_up, w_down):
    bsz, l, d = x.shape
    cap = EC_CAPACITY_FACTOR * l // N_EXPERTS
    tt = min(MOE_TT, l)
    nj = l // tt
    hb, aff = moe_route(x, gain, w_router, tm=tt)
    smap, gate, cnt = moe_select(aff, cap, tt=tt)
    cs = jnp.concatenate([jnp.zeros((bsz, N_EXPERTS, 1), jnp.float32), jnp.cumsum(cnt[..., :nj], axis=-1)], axis=-1)
    cs = cs.astype(jnp.int32).reshape(-1)
    outw = moe_experts(hb, smap, gate, cs, w_gate, w_up, w_down, cap, tt=tt)
    return moe_combine(x, smap, outw, cs, tt=tt)


def kernel(x, mix_norm, ffn_norm, ev_w_in, ev_w_out, a_lb_logits, a_out_norm, s5_lambda_re, s5_lambda_im, s5_log_step, s5_b_re, s5_b_im, s5_c_re, s5_c_im, s5_d, s5_glu_w, s5_glu_b, od_w_in, od_w_out, c_q_norm, c_k_norm, c_lambda, c_out_norm, rel_bias, d_conv_w, d_a_log, d_dt_bias, d_out_norm, moe_router, moe_w_gate, moe_w_up, moe_w_down):
    bsz, l, d = x.shape
    n = bsz * l
    p = jax.nn.softmax(a_lb_logits.astype(jnp.float32), axis=0)
    cum = jnp.cumsum(p, axis=0)
    lower_bounds = cum - cum[0:1]
    bias5 = rel_bias_tiles(rel_bias, ATT_T)
    for layer in range(DEPTH):
        j = layer // 2
        if layer % 2 == 0:
            proj = norm_matmul(x.reshape(n, d), mix_norm[layer], ev_w_in[j]).reshape(bsz, l, -1)
            u_b = proj[..., 5 * A_WIDTH:]
            o_a = hgrn2_mixer(proj, lower_bounds[j], a_out_norm[j])
            u_tb = jnp.transpose(u_b, (1, 0, 2)).reshape(l * bsz, B_WIDTH)
            o_b = s5_mixer_tb(u_tb, bsz, s5_lambda_re[j], s5_lambda_im[j], s5_log_step[j], s5_b_re[j], s5_b_im[j],
                              s5_c_re[j], s5_c_im[j], s5_d[j], s5_glu_w[j], s5_glu_b[j])
            o_b = jnp.transpose(o_b.reshape(l, bsz, B_WIDTH), (1, 0, 2))
            mixed = jnp.concatenate([o_a, o_b], axis=-1)
            x = matmul_residual(mixed.reshape(n, -1), ev_w_out[j], x.reshape(n, d)).reshape(bsz, l, d)
        else:
            o2 = 3 * C_WIDTH + 3 * D_WIDTH
            o4 = o2 + 4 * D_HEADS
            w = od_w_in[j]
            w_in = jnp.concatenate([w[:, :o2], w[:, o4:], w[:, o2:o4],
                                    jnp.zeros((d, OD_COLS - w.shape[1]), w.dtype)], axis=1)
            proj = norm_matmul(x.reshape(n, d), mix_norm[layer], w_in, tn=1280).reshape(bsz, l, OD_COLS)
            o_c = diff_attention(proj, c_q_norm[j], c_k_norm[j], c_lambda[j], c_out_norm[j], bias5, layer)
            o_d = gated_deltanet(proj, d_conv_w[j], d_a_log[j], d_dt_bias[j], d_out_norm[j])
            mixed = jnp.concatenate([o_c, o_d], axis=-1)
            x = matmul_residual(mixed.reshape(n, -1), od_w_out[j], x.reshape(n, d)).reshape(bsz, l, d)
        x = ec_moe_layer(x, ffn_norm[layer], moe_router[layer], moe_w_gate[layer], moe_w_up[layer],
                         moe_w_down[layer])
    return x
```

```python
import functools
import math

import jax
import jax.numpy as jnp
from jax import lax
from jax.experimental import pallas as pl
from jax.experimental.pallas import tpu as pltpu

D_MODEL = 1024
DEPTH = 4
MIX_WIDTH = D_MODEL
A_WIDTH = MIX_WIDTH // 2
A_HEAD_DIM = 128
A_HEADS = A_WIDTH // A_HEAD_DIM
B_WIDTH = MIX_WIDTH - A_WIDTH
S5_GROUP = 16
S5_GROUPS = B_WIDTH // S5_GROUP
S5_STATE = 64
C_WIDTH = MIX_WIDTH // 2
C_HEAD_DIM = 64
C_HEADS = C_WIDTH // (2 * C_HEAD_DIM)
C_V_DIM = 2 * C_HEAD_DIM
D_WIDTH = MIX_WIDTH - C_WIDTH
D_HEAD_DIM = 128
D_HEADS = D_WIDTH // D_HEAD_DIM
CONV_WIDTH = 5
N_EXPERTS = 16
EXPERT_FF = 2 * D_MODEL
EC_CAPACITY_FACTOR = 2
REL_BUCKETS = 32
REL_MAX_DIST = 128
CHUNK = 64
Q_BLOCK = 128
EPS = 1e-6

VMEM_LIMIT_BYTES = 48 * 1024 * 1024


def _norm_matmul_kernel(x_ref, g_ref, w_ref, o_ref):
    x = x_ref[...]
    y = x * lax.rsqrt(jnp.mean(x * x, axis=-1, keepdims=True) + EPS) * g_ref[...]
    o_ref[...] = jnp.dot(y.astype(jnp.bfloat16), w_ref[...], preferred_element_type=jnp.float32)


def norm_matmul(x, gain, w, *, tm=512, tn=512):
    n, k = x.shape
    m = w.shape[1]
    tn = min(tn, m)
    assert n % tm == 0 and m % tn == 0
    return pl.pallas_call(
        _norm_matmul_kernel,
        grid=(n // tm, m // tn),
        in_specs=[pl.BlockSpec((tm, k), lambda i, j: (i, 0)),
                  pl.BlockSpec((1, k), lambda i, j: (0, 0)),
                  pl.BlockSpec((k, tn), lambda i, j: (0, j))],
        out_specs=pl.BlockSpec((tm, tn), lambda i, j: (i, j)),
        out_shape=jax.ShapeDtypeStruct((n, m), jnp.float32),
        compiler_params=pltpu.CompilerParams(dimension_semantics=("parallel", "parallel"),
                                             vmem_limit_bytes=VMEM_LIMIT_BYTES),
        name="norm_matmul",
    )(x, gain.reshape(1, k).astype(jnp.float32), w.astype(jnp.bfloat16))


def _matmul_res_kernel(a_ref, w_ref, r_ref, o_ref):
    o_ref[...] = r_ref[...] + jnp.dot(a_ref[...].astype(jnp.bfloat16), w_ref[...],
                                      preferred_element_type=jnp.float32)


def matmul_residual(a, w, res, *, tm=512):
    n, k = a.shape
    m = w.shape[1]
    return pl.pallas_call(
        _matmul_res_kernel,
        grid=(n // tm,),
        in_specs=[pl.BlockSpec((tm, k), lambda i: (i, 0)),
                  pl.BlockSpec((k, m), lambda i: (0, 0)),
                  pl.BlockSpec((tm, m), lambda i: (i, 0))],
        out_specs=pl.BlockSpec((tm, m), lambda i: (i, 0)),
        out_shape=jax.ShapeDtypeStruct((n, m), jnp.float32),
        compiler_params=pltpu.CompilerParams(dimension_semantics=("parallel",),
                                             vmem_limit_bytes=VMEM_LIMIT_BYTES),
        name="matmul_residual",
    )(a, w.astype(jnp.bfloat16), res)


def rmsnorm(x, g):
    xf = x.astype(jnp.float32)
    y = xf * lax.rsqrt(jnp.mean(xf * xf, axis=-1, keepdims=True) + EPS)
    return (y * g.astype(jnp.float32)).astype(x.dtype)


def l2norm(x):
    return x * lax.rsqrt(jnp.sum(x * x, axis=-1, keepdims=True) + EPS)


def _heads(t, n_heads):
    b, l, w = t.shape
    return jnp.transpose(t.reshape(b, l, n_heads, w // n_heads), (0, 2, 1, 3))


def _flip(t):
    return jnp.flip(t, axis=2)


def _to_chunks(t):
    b, h, l = t.shape[:3]
    t = t.reshape((b, h, l // CHUNK, CHUNK) + t.shape[3:])
    return jnp.moveaxis(t, 2, 0)


def _from_chunks(t):
    t = jnp.moveaxis(t, 0, 2)
    b, h, nc, c = t.shape[:4]
    return t.reshape((b, h, nc * c) + t.shape[4:])


HG_LEVELS = tuple(CHUNK >> (i + 1) for i in range(CHUNK.bit_length() - 1))
HG_TOT_ROWS = 8
HG_TT = 512


def hgrn2_constants():
    import numpy as np
    c = CHUNK
    r = np.arange(c)[:, None]
    u = np.arange(c)[None, :]
    stacks, masks = [], []
    for direction in range(2):
        fwd = direction == 0
        lvl_masks = []
        for m in HG_LEVELS:
            blk = r // (2 * m)
            later = (r % (2 * m)) >= m
            lvl_masks.append((blk == blk.T) & (later & ~later.T if fwd else ~later & later.T))
        stacks.append(np.concatenate([(u <= r) if fwd else (u >= r), np.ones((HG_TOT_ROWS, c), bool)], axis=0))
        masks.append(np.stack(lvl_masks))
    return (jnp.asarray(np.stack(stacks), jnp.bfloat16), jnp.asarray(np.stack(masks), jnp.float32))


def _hgrn2_kernel(q_ref, f_ref, v_ref, loglb_ref, log1mlb_ref, onemlb_ref, ast_ref, mask_ref, o_ref,
                  st_sc, qd_sc, dec_sc, upd_sc, sin_sc, *, nc):
    direction = pl.program_id(2)

    @pl.when(pl.program_id(3) == 0)
    def _():
        st_sc[...] = jnp.zeros_like(st_sc)

    bf16 = jnp.bfloat16
    f32 = jnp.float32
    c = CHUNK
    hd = A_HEAD_DIM
    dirf = direction.astype(f32)
    loglb = loglb_ref[...]
    log1mlb = log1mlb_ref[...]
    onemlb = onemlb_ref[...]
    ast = ast_ref[...]
    contract_last = (((1,), (1,)), ((), ()))
    contract_first = (((0,), (0,)), ((), ()))

    ns = range(nc)
    rows = [slice(n * c, (n + 1) * c) for n in ns]
    z = [f_ref[r, :] for r in rows]
    v = [v_ref[r, :] for r in rows]
    qr = [q_ref[r, :] for r in rows]
    q = [x * jax.nn.sigmoid(x) for x in qr]
    e = [jnp.exp(-jnp.abs(x)) for x in z]
    cc = [log1mlb + jnp.minimum(z[n], 0.0) - jnp.log1p(e[n]) for n in ns]
    lf = [jnp.maximum(loglb, x) + jnp.log1p(jnp.exp(-jnp.abs(loglb - x))) for x in cc]
    k = [onemlb * jnp.where(z[n] >= 0, e[n], 1.0) / (1.0 + e[n]) for n in ns]
    hi = [x.astype(bf16) for x in lf]
    lo = [(lf[n] - hi[n].astype(f32)).astype(bf16) for n in ns]
    d = [jnp.dot(ast, hi[n], preferred_element_type=f32) + jnp.dot(ast, lo[n], preferred_element_type=f32)
         for n in ns]
    cum = [x[0:c] for x in d]
    tot = [x[c:c + HG_TOT_ROWS] for x in d]
    ref = [cum[n] - dirf * lf[n] for n in ns]
    attn = [jnp.zeros((c, c), f32) for _ in ns]
    for li, m in enumerate(HG_LEVELS):
        nb = c // (2 * m)
        split = [jnp.broadcast_to(x.reshape(nb, 2 * m, hd)[:, m - 1:m, :], (nb, 2 * m, hd)).reshape(c, hd)
                 for x in ref]
        x = [jnp.exp(-jnp.abs(cum[n] - split[n])) for n in ns]
        s = [lax.dot_general((q[n] * x[n]).astype(bf16), (k[n] * x[n]).astype(bf16), contract_last,
                             preferred_element_type=f32) for n in ns]
        attn = [attn[n] + mask_ref[li] * s[n] for n in ns]
    vb = [x.astype(bf16) for x in v]
    intra = [jnp.dot(attn[n].astype(bf16), vb[n], preferred_element_type=f32) for n in ns]
    upd = [lax.dot_general(vb[n], (k[n] * jnp.exp(tot[n][0:1] - cum[n])).astype(bf16), contract_first,
                           preferred_element_type=f32) for n in ns]
    for n in ns:
        o_ref[rows[n], :] = intra[n] + jnp.sum(q[n] * k[n], axis=-1, keepdims=True) * v[n]
        qd_sc[n] = (q[n] * jnp.exp(cum[n])).astype(bf16)
        dec_sc[n] = jnp.exp(tot[n])
        upd_sc[n] = upd[n]

    def body(ci, st):
        ce = ci + direction * (nc - 1 - 2 * ci)
        sin_sc[ce] = st.astype(bf16)
        return st * dec_sc[ce][0:1] + upd_sc[ce]

    st_sc[...] = lax.fori_loop(0, nc, body, st_sc[...])

    for n in range(nc):
        rows = slice(n * c, (n + 1) * c)
        o_ref[rows, :] += lax.dot_general(qd_sc[n], sin_sc[n], contract_last, preferred_element_type=f32)


def hgrn2_scan(proj3, lb):
    bsz, l, _ = proj3.shape
    hd = A_HEAD_DIM
    tt = min(HG_TT, l)
    nt = l // tt
    assert l % tt == 0 and tt % CHUNK == 0
    ast, masks = hgrn2_constants()
    lb = lb.astype(jnp.float32)
    vecs = [jnp.log(lb).reshape(2, 1, A_WIDTH), jnp.log1p(-lb).reshape(2, 1, A_WIDTH), (1.0 - lb).reshape(2, 1, A_WIDTH)]
    tidx = lambda d, i: i + d * (nt - 1 - 2 * i)
    vec = pl.BlockSpec((None, 1, hd), lambda b, h, d, i: (d, 0, h))
    return pl.pallas_call(
        functools.partial(_hgrn2_kernel, nc=tt // CHUNK),
        grid=(bsz, A_HEADS, 2, nt),
        in_specs=[pl.BlockSpec((None, tt, hd), lambda b, h, d, i: (b, tidx(d, i), h)),
                  pl.BlockSpec((None, tt, hd), lambda b, h, d, i: (b, tidx(d, i), (1 + d) * A_HEADS + h)),
                  pl.BlockSpec((None, tt, hd), lambda b, h, d, i: (b, tidx(d, i), 3 * A_HEADS + h)),
                  vec, vec, vec,
                  pl.BlockSpec((None,) + ast.shape[1:], lambda b, h, d, i: (d, 0, 0)),
                  pl.BlockSpec((None,) + masks.shape[1:], lambda b, h, d, i: (d, 0, 0, 0))],
        out_specs=pl.BlockSpec((None, None, tt, hd), lambda b, h, d, i: (d, b, tidx(d, i), h)),
        out_shape=jax.ShapeDtypeStruct((2, bsz, l, A_WIDTH), jnp.float32),
        scratch_shapes=[pltpu.VMEM((hd, hd), jnp.float32),
                        pltpu.VMEM((tt // CHUNK, CHUNK, hd), jnp.bfloat16),
                        pltpu.VMEM((tt // CHUNK, HG_TOT_ROWS, hd), jnp.float32),
                        pltpu.VMEM((tt // CHUNK, hd, hd), jnp.float32),
                        pltpu.VMEM((tt // CHUNK, hd, hd), jnp.bfloat16)],
        compiler_params=pltpu.CompilerParams(dimension_semantics=("parallel", "parallel", "parallel", "arbitrary"),
                                             vmem_limit_bytes=VMEM_LIMIT_BYTES),
        name="hgrn2_scan",
    )(proj3, proj3, proj3, *vecs, ast, masks)


def _hgrn2_final_kernel(of_ref, ob_ref, g_ref, gain_ref, o_ref):
    o = of_ref[...] + ob_ref[...]
    y = o * lax.rsqrt(jnp.mean(o * o, axis=-1, keepdims=True) + EPS) * gain_ref[...]
    g = g_ref[...]
    o_ref[...] = y * (g * jax.nn.sigmoid(g))


def bidir_finalize(o2, proj3, gate_block, out_gain, *, name, tm=1024):
    _, bsz, l, w = o2.shape
    hd = 128
    tm = min(tm, l)
    return pl.pallas_call(
        _hgrn2_final_kernel,
        grid=(bsz, l // tm, w // hd),
        in_specs=[pl.BlockSpec((None, None, tm, hd), lambda b, i, h: (0, b, i, h)),
                  pl.BlockSpec((None, None, tm, hd), lambda b, i, h: (1, b, i, h)),
                  pl.BlockSpec((None, tm, hd), lambda b, i, h: (b, i, gate_block + h)),
                  pl.BlockSpec((1, hd), lambda b, i, h: (0, 0))],
        out_specs=pl.BlockSpec((None, tm, hd), lambda b, i, h: (b, i, h)),
        out_shape=jax.ShapeDtypeStruct((bsz, l, w), jnp.float32),
        compiler_params=pltpu.CompilerParams(dimension_semantics=("parallel", "parallel", "parallel"),
                                             vmem_limit_bytes=VMEM_LIMIT_BYTES),
        name=name,
    )(o2, o2, proj3, out_gain.reshape(1, hd).astype(jnp.float32))


def hgrn2_mixer(proj3, lb, out_gain):
    return bidir_finalize(hgrn2_scan(proj3, lb), proj3, 4 * A_HEADS, out_gain, name="hgrn2_finalize")


S5_NS = S5_GROUPS * S5_STATE
S5_TT = 64
SUBLANES = 8


def _s5_scan_kernel(u_ref, win_ref, ar_ref, ai_ref, wout_ref, y_ref, bu_sc, xs_sc, st_sc, *, bsz, tt, reverse):
    @pl.when(pl.program_id(0) == 0)
    def _():
        st_sc[...] = jnp.zeros_like(st_sc)

    bu_sc[...] = jnp.dot(u_ref[...].astype(jnp.bfloat16), win_ref[...], preferred_element_type=jnp.float32)
    ar = jnp.broadcast_to(ar_ref[...], (bsz, S5_NS))
    ai = jnp.broadcast_to(ai_ref[...], (bsz, S5_NS))
    per = SUBLANES // bsz
    ngroups = tt // per

    def body(s, carry):
        xr, xi = carry
        p = (ngroups - 1 - s) if reverse else s
        base = pl.multiple_of(p * SUBLANES, SUBLANES)
        blk = bu_sc[pl.ds(base, SUBLANES), :]
        outs_r = [None] * per
        outs_i = [None] * per
        for ph in (range(per - 1, -1, -1) if reverse else range(per)):
            br = blk[ph * bsz:(ph + 1) * bsz, :S5_NS]
            bi = blk[ph * bsz:(ph + 1) * bsz, S5_NS:]
            xr, xi = ar * xr - ai * xi + br, ar * xi + ai * xr + bi
            outs_r[ph] = xr
            outs_i[ph] = xi
        xs_sc[pl.ds(base, SUBLANES), :S5_NS] = jnp.concatenate(outs_r, axis=0)
        xs_sc[pl.ds(base, SUBLANES), S5_NS:] = jnp.concatenate(outs_i, axis=0)
        return xr, xi

    xr, xi = lax.fori_loop(0, ngroups, body, (st_sc[0], st_sc[1]))
    st_sc[0] = xr
    st_sc[1] = xi
    y_ref[...] = jnp.dot(xs_sc[...].astype(jnp.bfloat16), wout_ref[...], preferred_element_type=jnp.float32)


def s5_scan(u_tb, win, ar, ai, wout, *, bsz, reverse):
    n = u_tb.shape[0]
    rows = S5_TT * bsz
    nt = n // rows
    assert n % rows == 0 and SUBLANES % bsz == 0
    idx = (lambda i: (nt - 1 - i, 0)) if reverse else (lambda i: (i, 0))
    const = lambda i: (0, 0)
    return pl.pallas_call(
        functools.partial(_s5_scan_kernel, bsz=bsz, tt=S5_TT, reverse=reverse),
        grid=(nt,),
        in_specs=[pl.BlockSpec((rows, B_WIDTH), idx),
                  pl.BlockSpec((B_WIDTH, 2 * S5_NS), const),
                  pl.BlockSpec((1, S5_NS), const),
                  pl.BlockSpec((1, S5_NS), const),
                  pl.BlockSpec((2 * S5_NS, B_WIDTH), const)],
        out_specs=pl.BlockSpec((rows, B_WIDTH), idx),
        out_shape=jax.ShapeDtypeStruct((n, B_WIDTH), jnp.float32),
        scratch_shapes=[pltpu.VMEM((rows, 2 * S5_NS), jnp.float32),
                        pltpu.VMEM((rows, 2 * S5_NS), jnp.float32),
                        pltpu.VMEM((2, bsz, S5_NS), jnp.float32)],
        compiler_params=pltpu.CompilerParams(dimension_semantics=("arbitrary",),
                                             vmem_limit_bytes=VMEM_LIMIT_BYTES),
        name="s5_scan_bwd" if reverse else "s5_scan_fwd",
    )(u_tb, win, ar, ai, wout)


def _s5_final_kernel(u_ref, yf_ref, yb_ref, d_ref, w_ref, b_ref, o_ref):
    y = d_ref[...] * u_ref[...] + yf_ref[...] + yb_ref[...]
    y = jax.nn.gelu(y)
    z = jnp.dot(y.astype(jnp.bfloat16), w_ref[...], preferred_element_type=jnp.float32) + b_ref[...]
    o_ref[...] = y * jax.nn.sigmoid(z)


def s5_finalize(u, yf, yb, d_skip, glu_w, glu_b, *, tm=512):
    n, w = u.shape
    row = pl.BlockSpec((tm, w), lambda i: (i, 0))
    vec = pl.BlockSpec((1, w), lambda i: (0, 0))
    return pl.pallas_call(
        _s5_final_kernel,
        grid=(n // tm,),
        in_specs=[row, row, row, vec, pl.BlockSpec((w, w), lambda i: (0, 0)), vec],
        out_specs=row,
        out_shape=jax.ShapeDtypeStruct((n, w), jnp.float32),
        compiler_params=pltpu.CompilerParams(dimension_semantics=("parallel",),
                                             vmem_limit_bytes=VMEM_LIMIT_BYTES),
        name="s5_finalize",
    )(u, yf, yb, d_skip.reshape(1, w).astype(jnp.float32), glu_w.astype(jnp.bfloat16),
      glu_b.reshape(1, w).astype(jnp.float32))


def s5_direction_params(lam_re, lam_im, log_step, b_re, b_im, c_re, c_im):
    step = jnp.exp(log_step)[:, None]
    mag = jnp.exp(lam_re * step)
    abar_re = mag * jnp.cos(lam_im * step)
    abar_im = mag * jnp.sin(lam_im * step)
    den = lam_re * lam_re + lam_im * lam_im
    fr = ((abar_re - 1.0) * lam_re + abar_im * lam_im) / den
    fi = (abar_im * lam_re - (abar_re - 1.0) * lam_im) / den
    bb_re = fr[..., None] * b_re - fi[..., None] * b_im
    bb_im = fr[..., None] * b_im + fi[..., None] * b_re
    eye = jnp.eye(S5_GROUPS, dtype=jnp.float32)
    win = jnp.concatenate([jnp.einsum('gnp,gh->gphn', bb, eye).reshape(B_WIDTH, S5_NS) for bb in (bb_re, bb_im)],
                          axis=1)
    wout = jnp.concatenate([jnp.einsum('gpn,gh->hngp', c, eye).reshape(S5_NS, B_WIDTH) for c in (c_re, -c_im)],
                           axis=0)
    return (win.astype(jnp.bfloat16), abar_re.reshape(1, S5_NS), abar_im.reshape(1, S5_NS),
            wout.astype(jnp.bfloat16))


def s5_mixer_tb(u_tb, bsz, lam_re, lam_im, log_step, b_re, b_im, c_re, c_im, d_skip, glu_w, glu_b):
    f32 = jnp.float32
    ys = []
    for direction in range(2):
        prm = s5_direction_params(lam_re[direction].astype(f32), lam_im[direction].astype(f32),
                                  log_step[direction].astype(f32), b_re[direction].astype(f32),
                                  b_im[direction].astype(f32), c_re[direction].astype(f32),
                                  c_im[direction].astype(f32))
        ys.append(s5_scan(u_tb, *prm, bsz=bsz, reverse=(direction == 1)))
    return s5_finalize(u_tb, ys[0], ys[1], d_skip, glu_w, glu_b)


def t5_bucket(rel):
    half = REL_BUCKETS // 2
    max_exact = half // 2
    base = jnp.where(rel > 0, half, 0)
    n = jnp.abs(rel)
    nf = jnp.maximum(n, 1).astype(jnp.float32)
    large = max_exact + (jnp.log(nf / max_exact) / math.log(REL_MAX_DIST / max_exact)
                         * (half - max_exact)).astype(jnp.int32)
    large = jnp.minimum(large, half - 1)
    return base + jnp.where(n < max_exact, n, large)


ATT_T = 512
LOG2E = math.log2(math.e)


def rel_bias_tiles(rel_bias, t):
    assert t >= REL_MAX_DIST
    table = rel_bias.astype(jnp.float32) * LOG2E
    tiles = []
    for d in (-1, 0, 1):
        c = table[t5_bucket(d * t + jnp.arange(-(t - 1), t))]
        w = jnp.concatenate([c, c[:1]], axis=0)
        m = jnp.tile(w, (t, 1))[:t * (2 * t - 1)].reshape(t, 2 * t - 1, -1)
        tiles.append(m[:, t - 1:2 * t - 1])
    far_neg = jnp.broadcast_to(table[t5_bucket(jnp.array(-2 * t))], tiles[0].shape)
    far_pos = jnp.broadcast_to(table[t5_bucket(jnp.array(2 * t))], tiles[0].shape)
    out = jnp.stack([far_neg] + tiles + [far_pos], axis=0)
    return jnp.transpose(out, (3, 0, 1, 2))


def _attn_prep_kernel(q_ref, k_ref, v_ref, qg_ref, kg_ref, q2_ref, kt_ref, vb_ref):
    lane = lax.broadcasted_iota(jnp.int32, q_ref.shape, 1)
    lo = lane < C_HEAD_DIM

    def halfnorm(x, g):
        sq = x * x
        s_lo = jnp.sum(jnp.where(lo, sq, 0.0), axis=-1, keepdims=True)
        s_hi = jnp.sum(jnp.where(lo, 0.0, sq), axis=-1, keepdims=True)
        ms = jnp.where(lo, s_lo, s_hi) * (1.0 / C_HEAD_DIM)
        return x * lax.rsqrt(ms + EPS) * g

    qn = halfnorm(q_ref[...], qg_ref[...]) * (C_HEAD_DIM ** -0.5 * LOG2E)
    kn = halfnorm(k_ref[...], kg_ref[...])
    q2_ref[0] = jnp.where(lo, qn, 0.0).astype(jnp.bfloat16)
    q2_ref[1] = jnp.where(lo, 0.0, qn).astype(jnp.bfloat16)
    kt_ref[...] = kn.T.astype(jnp.bfloat16)
    vb_ref[...] = v_ref[...].astype(jnp.bfloat16)


def attn_prep(proj3, q_gain, k_gain, *, tl=512):
    bsz, l, _ = proj3.shape
    hw = 2 * C_HEAD_DIM
    gq = jnp.tile(q_gain.astype(jnp.float32), 2).reshape(1, hw)
    gk = jnp.tile(k_gain.astype(jnp.float32), 2).reshape(1, hw)
    vec = pl.BlockSpec((1, hw), lambda b, h, i: (0, 0))
    return pl.pallas_call(
        _attn_prep_kernel,
        grid=(bsz, C_HEADS, l // tl),
        in_specs=[pl.BlockSpec((None, tl, hw), lambda b, h, i: (b, i, h)),
                  pl.BlockSpec((None, tl, hw), lambda b, h, i: (b, i, C_HEADS + h)),
                  pl.BlockSpec((None, tl, hw), lambda b, h, i: (b, i, 2 * C_HEADS + h)),
                  vec, vec],
        out_specs=[pl.BlockSpec((None, None, 2, tl, hw), lambda b, h, i: (b, h, 0, i, 0)),
                   pl.BlockSpec((None, None, hw, tl), lambda b, h, i: (b, h, 0, i)),
                   pl.BlockSpec((None, None, tl, hw), lambda b, h, i: (b, h, i, 0))],
        out_shape=[jax.ShapeDtypeStruct((bsz, C_HEADS, 2, l, hw), jnp.bfloat16),
                   jax.ShapeDtypeStruct((bsz, C_HEADS, hw, l), jnp.bfloat16),
                   jax.ShapeDtypeStruct((bsz, C_HEADS, l, hw), jnp.bfloat16)],
        compiler_params=pltpu.CompilerParams(dimension_semantics=("parallel", "parallel", "parallel"),
                                             vmem_limit_bytes=VMEM_LIMIT_BYTES),
        name="attn_prep",
    )(proj3, proj3, proj3, gq, gk)


ATT_ROWS = 64


def _attn_kernel(lam_ref, q2_ref, kt_ref, v_ref, bias_ref, g_ref, o_ref, m_sc, l_sc, acc_sc, s_sc, p_sc, a_sc,
                 *, t, nk, out_scale):
    qi = pl.program_id(2)
    q2 = q2_ref[...].reshape(2 * t, 2 * C_HEAD_DIM)
    m_sc[...] = jnp.full(m_sc.shape, -jnp.inf, jnp.float32)
    l_sc[...] = jnp.zeros_like(l_sc)
    acc_sc[...] = jnp.zeros_like(acc_sc)
    r = ATT_ROWS
    hw = 2 * C_HEAD_DIM

    def body(ki, carry):
        off = pl.multiple_of(ki * t, t)
        bidx = jnp.clip(ki - qi, -2, 2) + 2
        s_sc[...] = jnp.dot(q2, kt_ref[:, pl.ds(off, t)], preferred_element_type=jnp.float32)
        for g in range(2 * t // r):
            rows = slice(g * r, (g + 1) * r)
            brow = (g * r) % t
            s = s_sc[rows, :] + bias_ref[bidx, brow:brow + r, :]
            m_prev = m_sc[rows, :]
            m_new = jnp.maximum(m_prev, jnp.max(s, axis=-1, keepdims=True))
            alpha = jnp.exp2(m_prev - m_new)
            ps = [jnp.exp2(s[:, j * hw:(j + 1) * hw] - m_new) for j in range(t // hw)]
            l_sc[rows, :] = alpha * l_sc[rows, :] + jnp.sum(sum(ps), axis=-1, keepdims=True)
            m_sc[rows, :] = m_new
            a_sc[rows, :] = alpha
            for j in range(t // hw):
                p_sc[rows, j * hw:(j + 1) * hw] = ps[j].astype(jnp.bfloat16)
        acc_sc[...] = a_sc[...] * acc_sc[...] + jnp.dot(p_sc[...], v_ref[pl.ds(off, t), :],
                                                        preferred_element_type=jnp.float32)
        return carry

    lax.fori_loop(0, nk, body, 0)
    a = acc_sc[...] / l_sc[...]
    o = a[:t] - lam_ref[0] * a[t:]
    y = o * lax.rsqrt(jnp.mean(o * o, axis=-1, keepdims=True) + EPS)
    o_ref[...] = y * g_ref[...] * out_scale


def diff_attention(proj3, q_gain, k_gain, lam, out_gain, bias5, layer_idx):
    f32 = jnp.float32
    bsz, l, _ = proj3.shape
    t = ATT_T
    hw = 2 * C_HEAD_DIM
    lam_init = 0.8 - 0.6 * math.exp(-0.3 * layer_idx)
    lam_f = lam.astype(f32)
    lam_full = jnp.exp(jnp.sum(lam_f[0] * lam_f[1])) - jnp.exp(jnp.sum(lam_f[2] * lam_f[3])) + lam_init
    q2, kt, vb = attn_prep(proj3, q_gain, k_gain)
    return pl.pallas_call(
        functools.partial(_attn_kernel, t=t, nk=l // t, out_scale=1.0 - lam_init),
        grid=(bsz, C_HEADS, l // t),
        in_specs=[pl.BlockSpec(memory_space=pltpu.SMEM),
                  pl.BlockSpec((None, None, 2, t, hw), lambda b, h, i: (b, h, 0, i, 0)),
                  pl.BlockSpec((None, None, hw, l), lambda b, h, i: (b, h, 0, 0)),
                  pl.BlockSpec((None, None, l, hw), lambda b, h, i: (b, h, 0, 0)),
                  pl.BlockSpec((None, 5, t, t), lambda b, h, i: (h, 0, 0, 0)),
                  pl.BlockSpec((1, hw), lambda b, h, i: (0, 0))],
        out_specs=pl.BlockSpec((None, t, hw), lambda b, h, i: (b, i, h)),
        out_shape=jax.ShapeDtypeStruct((bsz, l, C_WIDTH), f32),
        scratch_shapes=[pltpu.VMEM((2 * t, hw), f32), pltpu.VMEM((2 * t, hw), f32), pltpu.VMEM((2 * t, hw), f32),
                        pltpu.VMEM((2 * t, t), f32), pltpu.VMEM((2 * t, t), jnp.bfloat16), pltpu.VMEM((2 * t, hw), f32)],
        compiler_params=pltpu.CompilerParams(dimension_semantics=("parallel", "parallel", "arbitrary"),
                                             vmem_limit_bytes=VMEM_LIMIT_BYTES),
        name="diff_attention",
    )(lam_full.reshape(1), q2, kt, vb, bias5, out_gain.reshape(1, hw).astype(f32))


GDN_TT = 512
LANES = 128
OD_QKV_BLOCK = 3 * C_WIDTH // LANES
OD_GATE_BLOCK = OD_QKV_BLOCK + 3 * D_WIDTH // LANES
OD_AB_BLOCK = OD_GATE_BLOCK + D_WIDTH // LANES
OD_COLS = 3840


def _gdn_prep_kernel(prev_ref, cur_ref, next_ref, w_ref, o_ref, *, tl, nl):
    i = pl.program_id(1)
    part = pl.program_id(2)
    prev = jnp.where(i > 0, prev_ref[...], 0.0)
    nxt = jnp.where(i < nl - 1, next_ref[...], 0.0)
    ext = jnp.concatenate([prev, cur_ref[...], nxt], axis=0)
    halo = prev.shape[0]
    acc = None
    for j in range(CONV_WIDTH):
        start = halo - CONV_WIDTH // 2 + j
        term = w_ref[j:j + 1, :] * ext[start:start + tl, :]
        acc = term if acc is None else acc + term
    y = acc * jax.nn.sigmoid(acc)
    scale = jnp.where(part == 0, D_HEAD_DIM ** -0.5, 1.0)
    heads = []
    for h in range(D_HEADS):
        yh = y[:, h * LANES:(h + 1) * LANES]
        heads.append(yh * (lax.rsqrt(jnp.sum(yh * yh, axis=-1, keepdims=True) + EPS) * scale))
    o_ref[...] = jnp.where(part < 2, jnp.concatenate(heads, axis=1), y)


def gdn_prep(proj3, conv_w, *, tl=512):
    bsz, l, _ = proj3.shape
    halo = SUBLANES
    nl = l // tl
    blk0 = OD_QKV_BLOCK * LANES // D_WIDTH
    return pl.pallas_call(
        functools.partial(_gdn_prep_kernel, tl=tl, nl=nl),
        grid=(bsz, nl, 3),
        in_specs=[pl.BlockSpec((None, halo, D_WIDTH), lambda b, i, p: (b, jnp.maximum(i * (tl // halo) - 1, 0), blk0 + p)),
                  pl.BlockSpec((None, tl, D_WIDTH), lambda b, i, p: (b, i, blk0 + p)),
                  pl.BlockSpec((None, halo, D_WIDTH),
                               lambda b, i, p: (b, jnp.minimum((i + 1) * (tl // halo), l // halo - 1), blk0 + p)),
                  pl.BlockSpec((CONV_WIDTH, D_WIDTH), lambda b, i, p: (0, p))],
        out_specs=pl.BlockSpec((None, None, tl, D_WIDTH), lambda b, i, p: (p, b, i, 0)),
        out_shape=jax.ShapeDtypeStruct((3, bsz, l, D_WIDTH), jnp.float32),
        compiler_params=pltpu.CompilerParams(dimension_semantics=("parallel", "parallel", "parallel"),
                                             vmem_limit_bytes=VMEM_LIMIT_BYTES),
        name="gdn_prep",
    )(proj3, proj3, proj3, conv_w.astype(jnp.float32))


def _gdn_gates_kernel(x_ref, nega_ref, dtb_ref, o_ref):
    x = x_ref[...]
    z = x + dtb_ref[...]
    g = nega_ref[...] * (jnp.maximum(z, 0.0) + jnp.log1p(jnp.exp(-jnp.abs(z))))
    lane = lax.broadcasted_iota(jnp.int32, x.shape, 1)
    y = jnp.where(lane < 2 * D_HEADS, g, jax.nn.sigmoid(x))
    o_ref[...] = y.T[0:4 * D_HEADS, :]


def gdn_gates(proj3, a_log, dt_bias, *, tl=512):
    bsz, l, _ = proj3.shape
    pad = LANES - 2 * D_HEADS
    nega = jnp.pad(-jnp.exp(a_log.astype(jnp.float32)).reshape(1, -1), ((0, 0), (0, pad)))
    dtb = jnp.pad(dt_bias.astype(jnp.float32).reshape(1, -1), ((0, 0), (0, pad)))
    vec = pl.BlockSpec((1, LANES), lambda b, i: (0, 0))
    return pl.pallas_call(
        _gdn_gates_kernel,
        grid=(bsz, l // tl),
        in_specs=[pl.BlockSpec((None, tl, LANES), lambda b, i: (b, i, OD_AB_BLOCK)), vec, vec],
        out_specs=pl.BlockSpec((None, 4 * D_HEADS, tl), lambda b, i: (b, 0, i)),
        out_shape=jax.ShapeDtypeStruct((bsz, 4 * D_HEADS, l), jnp.float32),
        compiler_params=pltpu.CompilerParams(dimension_semantics=("parallel", "parallel"),
                                             vmem_limit_bytes=VMEM_LIMIT_BYTES),
        name="gdn_gates",
    )(proj3, nega, dtb)


def gdn_constants():
    import numpy as np
    c = CHUNK
    r = np.arange(c)[:, None]
    u = np.arange(c)[None, :]
    cum, incl, strict = [], [], []
    for direction in range(2):
        fwd = direction == 0
        cum.append(np.concatenate([(r <= u) if fwd else (r >= u), np.ones((c, c), bool)], axis=1))
        incl.append((u <= r) if fwd else (u >= r))
        strict.append((u < r) if fwd else (u > r))
    same = lambda b: (r // b) == (u // b)
    merges = [same(2 * b) & ~same(b) for b in (8, 16, 32)]
    f32 = jnp.float32
    return (jnp.asarray(np.stack(cum), jnp.bfloat16), jnp.asarray(np.stack(incl), f32),
            jnp.asarray(np.stack(strict), f32), jnp.asarray(same(8), f32), jnp.asarray(np.stack(merges), f32))


def _gdn_kernel(q_ref, k_ref, v_ref, g_ref, b_ref, cum_ref, incl_ref, strict_ref, d8_ref, mrg_ref, o_ref,
                s_sc, qd_sc, dec_sc, w_sc, u_sc, sin_sc, *, nc):
    direction = pl.program_id(2)

    @pl.when(pl.program_id(3) == 0)
    def _():
        s_sc[...] = jnp.zeros_like(s_sc)

    bf16 = jnp.bfloat16
    f32 = jnp.float32
    c = CHUNK
    hd = D_HEAD_DIM
    contract_last = (((1,), (1,)), ((), ()))
    contract_first = (((0,), (0,)), ((), ()))
    cumm = cum_ref[...]
    incl = incl_ref[...]
    strict = strict_ref[...]
    d8 = d8_ref[...]
    eye = (lax.broadcasted_iota(jnp.int32, (c, c), 0) == lax.broadcasted_iota(jnp.int32, (c, c), 1)).astype(f32)

    def mm(a, b):
        return jnp.dot(a.astype(bf16), b.astype(bf16), preferred_element_type=f32)

    def rep(x):
        return jnp.concatenate([x] * (hd // c), axis=1)

    ns = range(nc)
    rows = [slice(n * c, (n + 1) * c) for n in ns]
    q = [q_ref[r, :] for r in rows]
    k = [k_ref[r, :] for r in rows]
    v = [v_ref[r, :] for r in rows]
    kb = [x.astype(bf16) for x in k]
    kk = [lax.dot_general(x, x, contract_last, preferred_element_type=f32) for x in kb]
    qk = [lax.dot_general(q[n].astype(bf16), kb[n], contract_last, preferred_element_type=f32) for n in ns]
    grow = [jnp.broadcast_to(g_ref[:, r], (c, c)) for r in rows]
    ghi = [x.astype(bf16) for x in grow]
    glo = [(grow[n] - ghi[n].astype(f32)).astype(bf16) for n in ns]
    gm = [jnp.dot(ghi[n], cumm, preferred_element_type=f32) + jnp.dot(glo[n], cumm, preferred_element_type=f32)
          for n in ns]
    gam_row = [x[:, :c] for x in gm]
    tot = [x[:, c:] for x in gm]
    gam_col = [x.T for x in gam_row]
    beta_col = [jnp.broadcast_to(b_ref[:, r], (c, c)).T for r in rows]
    decay = [incl * jnp.exp(jnp.minimum(gam_col[n] - gam_row[n], 0.0)) for n in ns]
    a = [strict * beta_col[n] * kk[n] * decay[n] for n in ns]
    a0 = [x * d8 for x in a]
    n2 = [mm(x, x) for x in a0]
    n4 = [mm(x, x) for x in n2]
    t = [mm(eye - a0[n], eye + n2[n]) for n in ns]
    t = [mm(t[n], eye + n4[n]) for n in ns]
    for j in range(mrg_ref.shape[0]):
        p = [mm(a[n] * mrg_ref[j], t[n]) for n in ns]
        t = [t[n] - mm(t[n], p[n]) for n in ns]
    beta128 = [rep(x) for x in beta_col]
    egam128 = [rep(jnp.exp(x)) for x in gam_col]
    solb = [mm(t[n], jnp.concatenate([k[n] * beta128[n] * egam128[n], v[n] * beta128[n]], axis=1)).astype(bf16)
            for n in ns]
    av = [jnp.dot((qk[n] * decay[n]).astype(bf16), solb[n], preferred_element_type=f32) for n in ns]
    k_dec = [(k[n] * rep(jnp.exp(tot[n] - gam_col[n]))).astype(bf16) for n in ns]
    wu = [lax.dot_general(k_dec[n], solb[n], contract_first, preferred_element_type=f32) for n in ns]
    for n in ns:
        qd_sc[n] = (q[n] * egam128[n] - av[n][:, :hd]).astype(bf16)
        o_ref[rows[n], :] = av[n][:, hd:]
        w_sc[n] = wu[n][:, :hd].astype(bf16)
        u_sc[n] = wu[n][:, hd:]
        dec_sc[n] = rep(jnp.exp(tot[n][0:SUBLANES, :]))

    def body(ci, s):
        ce = ci + direction * (nc - 1 - 2 * ci)
        sb = s.astype(bf16)
        sin_sc[ce] = sb
        return s * dec_sc[ce][0:1] - jnp.dot(w_sc[ce], sb, preferred_element_type=f32) + u_sc[ce]

    s_sc[...] = lax.fori_loop(0, nc, body, s_sc[...])

    for n in range(nc):
        rows = slice(n * c, (n + 1) * c)
        o_ref[rows, :] += jnp.dot(qd_sc[n], sin_sc[n], preferred_element_type=f32)


def gdn_scan(qkv, gb):
    _, bsz, l, _ = qkv.shape
    hd = D_HEAD_DIM
    tt = min(GDN_TT, l)
    nt = l // tt
    nc = tt // CHUNK
    assert l % tt == 0 and tt % CHUNK == 0
    consts = gdn_constants()
    gb4 = gb.reshape(bsz, 4 * D_HEADS, 1, l)
    tidx = lambda d, i: i + d * (nt - 1 - 2 * i)
    qkv_spec = lambda p: pl.BlockSpec((None, None, tt, hd), lambda b, h, d, i: (p, b, tidx(d, i), h))
    row_spec = lambda off: pl.BlockSpec((None, None, 1, tt),
                                        lambda b, h, d, i: (b, off + d * D_HEADS + h, 0, tidx(d, i)))
    per_dir = lambda a: pl.BlockSpec((None,) + a.shape[1:], lambda b, h, d, i: (d,) + (0,) * (a.ndim - 1))
    whole = lambda a: pl.BlockSpec(a.shape, lambda b, h, d, i: (0,) * a.ndim)
    return pl.pallas_call(
        functools.partial(_gdn_kernel, nc=nc),
        grid=(bsz, D_HEADS, 2, nt),
        in_specs=[qkv_spec(0), qkv_spec(1), qkv_spec(2), row_spec(0), row_spec(2 * D_HEADS),
                  per_dir(consts[0]), per_dir(consts[1]), per_dir(consts[2]), whole(consts[3]), whole(consts[4])],
        out_specs=pl.BlockSpec((None, None, tt, hd), lambda b, h, d, i: (d, b, tidx(d, i), h)),
        out_shape=jax.ShapeDtypeStruct((2, bsz, l, D_WIDTH), jnp.float32),
        scratch_shapes=[pltpu.VMEM((hd, hd), jnp.float32),
                        pltpu.VMEM((nc, CHUNK, hd), jnp.bfloat16),
                        pltpu.VMEM((nc, SUBLANES, hd), jnp.float32),
                        pltpu.VMEM((nc, hd, hd), jnp.bfloat16),
                        pltpu.VMEM((nc, hd, hd), jnp.float32),
                        pltpu.VMEM((nc, hd, hd), jnp.bfloat16)],
        compiler_params=pltpu.CompilerParams(dimension_semantics=("parallel", "parallel", "parallel", "arbitrary"),
                                             vmem_limit_bytes=VMEM_LIMIT_BYTES),
        name="gdn_scan",
    )(qkv, qkv, qkv, gb4, gb4, *consts)


def gated_deltanet(proj3, conv_w, a_log, dt_bias, out_gain):
    o2 = gdn_scan(gdn_prep(proj3, conv_w), gdn_gates(proj3, a_log, dt_bias))
    return bidir_finalize(o2, proj3, OD_GATE_BLOCK, out_gain, name="gdn_finalize")


MOE_TT = 512
MOE_SUB = 128
MOE_ROWS = 256
MOE_SLAB = 256
MOE_VMEM_LIMIT_BYTES = 56 * 1024 * 1024


def _router_kernel(x_ref, g_ref, wr_ref, h_ref, aff_ref):
    x = x_ref[...]
    h = (x * lax.rsqrt(jnp.mean(x * x, axis=-1, keepdims=True) + EPS) * g_ref[...]).astype(jnp.bfloat16)
    h_ref[...] = h
    logits = jnp.dot(h, wr_ref[...], preferred_element_type=jnp.float32)
    lane = lax.broadcasted_iota(jnp.int32, logits.shape, 1)
    logits = jnp.where(lane < N_EXPERTS, logits, -jnp.inf)
    p = jnp.exp(logits - jnp.max(logits, axis=-1, keepdims=True))
    aff = p / jnp.sum(p, axis=-1, keepdims=True)
    aff_ref[...] = aff.T[0:N_EXPERTS, :]


def moe_route(x, gain, w_router, *, tm=MOE_TT):
    bsz, l, d = x.shape
    wr = jnp.pad(w_router.astype(jnp.bfloat16), ((0, 0), (0, LANES - N_EXPERTS)))
    return pl.pallas_call(
        _router_kernel,
        grid=(bsz, l // tm),
        in_specs=[pl.BlockSpec((None, tm, d), lambda b, i: (b, i, 0)),
                  pl.BlockSpec((1, d), lambda b, i: (0, 0)),
                  pl.BlockSpec((d, LANES), lambda b, i: (0, 0))],
        out_specs=[pl.BlockSpec((None, tm, d), lambda b, i: (b, i, 0)),
                   pl.BlockSpec((None, N_EXPERTS, tm), lambda b, i: (b, 0, i))],
        out_shape=[jax.ShapeDtypeStruct((bsz, l, d), jnp.bfloat16),
                   jax.ShapeDtypeStruct((bsz, N_EXPERTS, l), jnp.float32)],
        compiler_params=pltpu.CompilerParams(dimension_semantics=("parallel", "parallel"),
                                             vmem_limit_bytes=VMEM_LIMIT_BYTES),
        name="moe_router",
    )(x, gain.reshape(1, d).astype(jnp.float32), wr)


def _select_kernel(aff_ref, pre_ref, smap_ref, gate_ref, cnt_ref, *, cap, tt):
    f32 = jnp.float32
    bf16 = jnp.bfloat16
    aff = aff_ref[...]
    e, l = aff.shape
    nl = l // LANES
    tiles = [slice(j * LANES, (j + 1) * LANES) for j in range(nl)]
    bits = pltpu.bitcast(aff, jnp.int32)
    bt = [bits[:, s] for s in tiles]

    def lane_total(x):
        return jnp.broadcast_to(jnp.sum(x, axis=-1, keepdims=True), (e, LANES))

    def search(i, thr):
        cand = thr | jnp.left_shift(jnp.int32(1), 30 - i)
        acc = jnp.zeros((e, LANES), jnp.int32)
        for x in bt:
            acc = acc + (x >= cand).astype(jnp.int32)
        return jnp.where(lane_total(acc) >= cap, cand, thr)

    thr = lax.fori_loop(0, 31, search, jnp.zeros((e, LANES), jnp.int32))
    gt = [x > thr for x in bt]
    eq = [x == thr for x in bt]
    acc = jnp.zeros((e, LANES), jnp.int32)
    for x in gt:
        acc = acc + x.astype(jnp.int32)
    need = (cap - lane_total(acc)).astype(f32)

    pre = pre_ref[...]

    def prefix(flags):
        outs = [jnp.dot(jnp.where(x, 1.0, 0.0).astype(bf16), pre, preferred_element_type=f32) for x in flags]
        carry = jnp.zeros((e, LANES), f32)
        res = []
        for o in outs:
            res.append(o[:, :LANES] + carry)
            carry = carry + o[:, LANES:]
        return res, [o[:, LANES:] for o in outs]

    rank_eq, _ = prefix(eq)
    sel = [jnp.logical_or(gt[j], jnp.logical_and(eq[j], rank_eq[j] < need)) for j in range(nl)]
    pos, totals = prefix(sel)
    lane = lax.broadcasted_iota(jnp.int32, (e, LANES), 1)
    cnt = jnp.zeros((e, LANES), f32)
    per = tt // LANES
    for j in range(nl):
        smap_ref[:, tiles[j]] = jnp.where(sel[j], pos[j], -1.0)
        gate_ref[:, tiles[j]] = jnp.where(sel[j], aff[:, tiles[j]], 0.0)
        cnt = cnt + jnp.where(lane == j // per, totals[j], 0.0)
    cnt_ref[...] = cnt


def moe_select(aff, cap, *, tt=MOE_TT):
    import numpy as np
    bsz, e, l = aff.shape
    assert l // tt <= LANES
    i = np.arange(LANES)
    pre = np.concatenate([i[:, None] < i[None, :], np.ones((LANES, LANES), bool)], axis=1)
    row = pl.BlockSpec((None, e, l), lambda b: (b, 0, 0))
    return pl.pallas_call(
        functools.partial(_select_kernel, cap=cap, tt=tt),
        grid=(bsz,),
        in_specs=[row, pl.BlockSpec((LANES, 2 * LANES), lambda b: (0, 0))],
        out_specs=[row, row, pl.BlockSpec((None, e, LANES), lambda b: (b, 0, 0))],
        out_shape=[jax.ShapeDtypeStruct((bsz, e, l), jnp.float32), jax.ShapeDtypeStruct((bsz, e, l), jnp.float32),
                   jax.ShapeDtypeStruct((bsz, e, LANES), jnp.float32)],
        compiler_params=pltpu.CompilerParams(dimension_semantics=("parallel",),
                                             vmem_limit_bytes=VMEM_LIMIT_BYTES),
        name="moe_select",
    )(aff, jnp.asarray(pre, jnp.bfloat16))


def _slot_one_hot(pos, base, rows, n):
    slot = (base + lax.broadcasted_iota(jnp.int32, (rows, n), 0)).astype(jnp.float32)
    return jnp.where(pos == slot, 1.0, 0.0).astype(jnp.bfloat16)


def _expert_kernel(cs_ref, h_ref, smap_ref, gate_ref, wg32_ref, wu32_ref, wd32_ref, o_ref,
                   xs_sc, gs_sc, wg_ref, wu_ref, wd_ref, *, nj, cap):
    e = pl.program_id(0)
    b = pl.program_id(1)
    j = pl.program_id(2)
    f32 = jnp.float32
    bf16 = jnp.bfloat16

    @pl.when(jnp.logical_and(b == 0, j == 0))
    def _():
        wg_ref[...] = wg32_ref[...].astype(bf16)
        wu_ref[...] = wu32_ref[...].astype(bf16)
        wd_ref[...] = wd32_ref[...].astype(bf16)

    @pl.when(j == 0)
    def _():
        xs_sc[...] = jnp.zeros_like(xs_sc)
        gs_sc[...] = jnp.zeros_like(gs_sc)

    base = (b * N_EXPERTS + e) * (nj + 1) + j
    c0 = cs_ref[base]
    c1 = cs_ref[base + 1]
    pos = smap_ref[...]
    gate = gate_ref[...]
    hb = h_ref[...]
    tt = hb.shape[0]

    def gather(st, carry):
        r0 = pl.multiple_of(st * MOE_SUB, MOE_SUB)
        oh = _slot_one_hot(pos, r0, MOE_SUB, tt)
        xs_sc[pl.ds(r0, MOE_SUB), :] += jnp.dot(oh, hb, preferred_element_type=f32)
        g = jnp.sum(oh.astype(f32) * gate, axis=-1, keepdims=True)
        gs_sc[pl.ds(r0, MOE_SUB), :] += jnp.broadcast_to(g, (MOE_SUB, LANES))
        return carry

    lax.fori_loop(c0 // MOE_SUB, (c1 + MOE_SUB - 1) // MOE_SUB, gather, 0)

    @pl.when(j == nj - 1)
    def _():
        rows_per = min(MOE_ROWS, cap)
        for r in range(cap // rows_per):
            rows = slice(r * rows_per, (r + 1) * rows_per)
            xb = xs_sc[rows, :].astype(bf16)
            g = jnp.dot(xb, wg_ref[...], preferred_element_type=f32)
            u = jnp.dot(xb, wu_ref[...], preferred_element_type=f32)
            hid = (g * jax.nn.sigmoid(g) * u).astype(bf16)
            out = jnp.dot(hid, wd_ref[...], preferred_element_type=f32)
            scale = jnp.concatenate([gs_sc[rows, :]] * (out.shape[1] // LANES), axis=1)
            o_ref[rows, :] = (out * scale).astype(bf16)


def moe_experts(hb, smap, gate, cs, w_gate, w_up, w_down, cap, *, tt=MOE_TT):
    bsz, l, d = hb.shape
    e, _, ff = w_gate.shape
    nj = l // tt
    smap4 = smap.reshape(bsz, e, 1, l)
    gate4 = gate.reshape(bsz, e, 1, l)
    tok = pl.BlockSpec((None, None, 1, tt), lambda ei, b, j, cs_ref: (b, ei, 0, j))
    once = pl.Buffered(1)
    grid_spec = pltpu.PrefetchScalarGridSpec(
        num_scalar_prefetch=1,
        grid=(e, bsz, nj),
        in_specs=[pl.BlockSpec((None, tt, d), lambda ei, b, j, cs_ref: (b, j, 0)), tok, tok,
                  pl.BlockSpec((None, d, ff), lambda ei, b, j, cs_ref: (ei, 0, 0), pipeline_mode=once),
                  pl.BlockSpec((None, d, ff), lambda ei, b, j, cs_ref: (ei, 0, 0), pipeline_mode=once),
                  pl.BlockSpec((None, ff, d), lambda ei, b, j, cs_ref: (ei, 0, 0), pipeline_mode=once)],
        out_specs=pl.BlockSpec((None, None, cap, d), lambda ei, b, j, cs_ref: (b, ei, 0, 0)),
        scratch_shapes=[pltpu.VMEM((cap, d), jnp.float32), pltpu.VMEM((cap, LANES), jnp.float32),
                        pltpu.VMEM((d, ff), jnp.bfloat16), pltpu.VMEM((d, ff), jnp.bfloat16),
                        pltpu.VMEM((ff, d), jnp.bfloat16)])
    return pl.pallas_call(
        functools.partial(_expert_kernel, nj=nj, cap=cap),
        grid_spec=grid_spec,
        out_shape=jax.ShapeDtypeStruct((bsz, e, cap, d), jnp.bfloat16),
        compiler_params=pltpu.CompilerParams(dimension_semantics=("parallel", "arbitrary", "arbitrary"),
                                             vmem_limit_bytes=MOE_VMEM_LIMIT_BYTES),
        name="moe_experts",
    )(cs, hb, smap4, gate4, w_gate, w_up, w_down)


def _combine_kernel(cs_ref, x_ref, smap_ref, ow_ref, y_ref, *, nj, tt):
    b = pl.program_id(0)
    e = pl.program_id(2)

    @pl.when(e == 0)
    def _():
        y_ref[...] = x_ref[...]

    contract_first = (((0,), (0,)), ((), ()))
    cap = ow_ref.shape[0]
    win = min(2 * MOE_SUB, cap)
    base = (b * N_EXPERTS + e) * (nj + 1)
    cols = [slice(j * tt, (j + 1) * tt) for j in range(nj)]
    pos = [smap_ref[:, c] for c in cols]
    r0 = [pl.multiple_of(jnp.minimum(cs_ref[base + j] // MOE_SUB * MOE_SUB, cap - win), MOE_SUB) for j in range(nj)]
    oh = [_slot_one_hot(pos[j], r0[j], win, tt) for j in range(nj)]
    add = [lax.dot_general(oh[j], ow_ref[pl.ds(r0[j], win), :], contract_first, preferred_element_type=jnp.float32)
           for j in range(nj)]
    for j in range(nj):
        y_ref[cols[j], :] += add[j]

    for j in range(nj):
        def scatter(st, carry):
            s0 = pl.multiple_of(st * MOE_SUB, MOE_SUB)
            y_ref[cols[j], :] += lax.dot_general(_slot_one_hot(pos[j], s0, MOE_SUB, tt),
                                                 ow_ref[pl.ds(s0, MOE_SUB), :], contract_first,
                                                 preferred_element_type=jnp.float32)
            return carry

        lax.fori_loop((r0[j] + win) // MOE_SUB, (cs_ref[base + j + 1] + MOE_SUB - 1) // MOE_SUB, scatter, 0)


def moe_combine(x, smap, outw, cs, *, tt=MOE_TT):
    bsz, l, d = x.shape
    e, cap = outw.shape[1:3]
    nj = l // tt
    smap4 = smap.reshape(bsz, e, 1, l)
    grid_spec = pltpu.PrefetchScalarGridSpec(
        num_scalar_prefetch=1,
        grid=(bsz, d // MOE_SLAB, e),
        in_specs=[pl.BlockSpec((None, l, MOE_SLAB), lambda b, s, ei, cs_ref: (b, 0, s)),
                  pl.BlockSpec((None, None, 1, l), lambda b, s, ei, cs_ref: (b, ei, 0, 0)),
                  pl.BlockSpec((None, None, cap, MOE_SLAB), lambda b, s, ei, cs_ref: (b, ei, 0, s))],
        out_specs=pl.BlockSpec((None, l, MOE_SLAB), lambda b, s, ei, cs_ref: (b, 0, s)))
    return pl.pallas_call(
        functools.partial(_combine_kernel, nj=nj, tt=tt),
        grid_spec=grid_spec,
        out_shape=jax.ShapeDtypeStruct((bsz, l, d), jnp.float32),
        compiler_params=pltpu.CompilerParams(dimension_semantics=("parallel", "parallel", "arbitrary"),
                                             vmem_limit_bytes=MOE_VMEM_LIMIT_BYTES),
        name="moe_combine",
    )(cs, x, smap4, outw)


def ec_moe_layer(x, gain, w_router, w_gate, w_up, w_down):
    bsz, l, d = x.shape
    cap = EC_CAPACITY_FACTOR * l // N_EXPERTS
    tt = min(MOE_TT, l)
    nj = l // tt
    hb, aff = moe_route(x, gain, w_router, tm=tt)
    smap, gate, cnt = moe_select(aff, cap, tt=tt)
    cs = jnp.concatenate([jnp.zeros((bsz, N_EXPERTS, 1), jnp.float32), jnp.cumsum(cnt[..., :nj], axis=-1)], axis=-1)
    cs = cs.astype(jnp.int32).reshape(-1)
    outw = moe_experts(hb, smap, gate, cs, w_gate, w_up, w_down, cap, tt=tt)
    return moe_combine(x, smap, outw, cs, tt=tt)


def kernel(x, mix_norm, ffn_norm, ev_w_in, ev_w_out, a_lb_logits, a_out_norm, s5_lambda_re, s5_lambda_im, s5_log_step, s5_b_re, s5_b_im, s5_c_re, s5_c_im, s5_d, s5_glu_w, s5_glu_b, od_w_in, od_w_out, c_q_norm, c_k_norm, c_lambda, c_out_norm, rel_bias, d_conv_w, d_a_log, d_dt_bias, d_out_norm, moe_router, moe_w_gate, moe_w_up, moe_w_down):
    bsz, l, d = x.shape
    n = bsz * l
    p = jax.nn.softmax(a_lb_logits.astype(jnp.float32), axis=0)
    cum = jnp.cumsum(p, axis=0)
    lower_bounds = cum - cum[0:1]
    bias5 = rel_bias_tiles(rel_bias, ATT_T)
    for layer in range(DEPTH):
        j = layer // 2
        if layer % 2 == 0:
            proj = norm_matmul(x.reshape(n, d), mix_norm[layer], ev_w_in[j]).reshape(bsz, l, -1)
            u_b = proj[..., 5 * A_WIDTH:]
            o_a = hgrn2_mixer(proj, lower_bounds[j], a_out_norm[j])
            u_tb = jnp.transpose(u_b, (1, 0, 2)).reshape(l * bsz, B_WIDTH)
            o_b = s5_mixer_tb(u_tb, bsz, s5_lambda_re[j], s5_lambda_im[j], s5_log_step[j], s5_b_re[j], s5_b_im[j],
                              s5_c_re[j], s5_c_im[j], s5_d[j], s5_glu_w[j], s5_glu_b[j])
            o_b = jnp.transpose(o_b.reshape(l, bsz, B_WIDTH), (1, 0, 2))
            mixed = jnp.concatenate([o_a, o_b], axis=-1)
            x = matmul_residual(mixed.reshape(n, -1), ev_w_out[j], x.reshape(n, d)).reshape(bsz, l, d)
        else:
            o2 = 3 * C_WIDTH + 3 * D_WIDTH
            o4 = o2 + 4 * D_HEADS
            w = od_w_in[j]
            w_in = jnp.concatenate([w[:, :o2], w[:, o4:], w[:, o2:o4],
                                    jnp.zeros((d, OD_COLS - w.shape[1]), w.dtype)], axis=1)
            proj = norm_matmul(x.reshape(n, d), mix_norm[layer], w_in, tn=1280).reshape(bsz, l, OD_COLS)
            o_c = diff_attention(proj, c_q_norm[j], c_k_norm[j], c_lambda[j], c_out_norm[j], bias5, layer)
            o_d = gated_deltanet(proj, d_conv_w[j], d_a_log[j], d_dt_bias[j], d_out_norm[j])
            mixed = jnp.concatenate([o_c, o_d], axis=-1)
            x = matmul_residual(mixed.reshape(n, -1), od_w_out[j], x.reshape(n, d)).reshape(bsz, l, d)
        x = ec_moe_layer(x, ffn_norm[layer], moe_router[layer], moe_w_gate[layer], moe_w_up[layer],
                         moe_w_down[layer])
    return x
```

```python
import functools
import math

import jax
import jax.numpy as jnp
from jax import lax
from jax.experimental import pallas as pl
from jax.experimental.pallas import tpu as pltpu

D_MODEL = 1024
DEPTH = 4
MIX_WIDTH = D_MODEL
A_WIDTH = MIX_WIDTH // 2
A_HEAD_DIM = 128
A_HEADS = A_WIDTH // A_HEAD_DIM
B_WIDTH = MIX_WIDTH - A_WIDTH
S5_GROUP = 16
S5_GROUPS = B_WIDTH // S5_GROUP
S5_STATE = 64
C_WIDTH = MIX_WIDTH // 2
C_HEAD_DIM = 64
C_HEADS = C_WIDTH // (2 * C_HEAD_DIM)
C_V_DIM = 2 * C_HEAD_DIM
D_WIDTH = MIX_WIDTH - C_WIDTH
D_HEAD_DIM = 128
D_HEADS = D_WIDTH // D_HEAD_DIM
CONV_WIDTH = 5
N_EXPERTS = 16
EXPERT_FF = 2 * D_MODEL
EC_CAPACITY_FACTOR = 2
REL_BUCKETS = 32
REL_MAX_DIST = 128
CHUNK = 64
Q_BLOCK = 128
EPS = 1e-6

VMEM_LIMIT_BYTES = 48 * 1024 * 1024


PROJ_TM = 512
PROJ_COLS = 512


def _norm_matmul_kernel(x_ref, g_ref, w_ref, o_ref, *tail_ref, main):
    x = x_ref[...]
    y = (x * lax.rsqrt(jnp.mean(x * x, axis=-1, keepdims=True) + EPS) * g_ref[...]).astype(jnp.bfloat16)
    for c0 in range(0, main, PROJ_COLS):
        c1 = min(c0 + PROJ_COLS, main)
        o_ref[:, c0:c1] = jnp.dot(y, w_ref[:, c0:c1], preferred_element_type=jnp.float32)
    if tail_ref:
        tail_ref[0][...] = jnp.dot(y, w_ref[:, main:], preferred_element_type=jnp.float32)


def norm_matmul(x, gain, w, *, tail=0, tm=PROJ_TM):
    bsz, l, k = x.shape
    m = w.shape[1]
    main = m - tail
    tm = min(tm, l)
    out_shape = [jax.ShapeDtypeStruct((bsz, l, main), jnp.float32)]
    out_specs = [pl.BlockSpec((None, tm, main), lambda b, i: (b, i, 0))]
    if tail:
        out_shape.append(jax.ShapeDtypeStruct((l, bsz * tail), jnp.float32))
        out_specs.append(pl.BlockSpec((tm, tail), lambda b, i: (i, b)))
    outs = pl.pallas_call(
        functools.partial(_norm_matmul_kernel, main=main),
        grid=(bsz, l // tm),
        in_specs=[pl.BlockSpec((None, tm, k), lambda b, i: (b, i, 0)),
                  pl.BlockSpec((1, k), lambda b, i: (0, 0)),
                  pl.BlockSpec((k, m), lambda b, i: (0, 0), pipeline_mode=pl.Buffered(1))],
        out_specs=out_specs,
        out_shape=out_shape,
        compiler_params=pltpu.CompilerParams(dimension_semantics=("parallel", "parallel"),
                                             vmem_limit_bytes=VMEM_LIMIT_BYTES),
        name="norm_matmul",
    )(x, gain.reshape(1, k).astype(jnp.float32), w.astype(jnp.bfloat16))
    return outs if tail else outs[0]


def _matmul_res_kernel(a1_ref, a2_ref, w1_ref, w2_ref, r_ref, o_ref):
    bf16 = jnp.bfloat16
    o_ref[...] = (r_ref[...] + jnp.dot(a1_ref[...].astype(bf16), w1_ref[...], preferred_element_type=jnp.float32)
                  + jnp.dot(a2_ref[...].astype(bf16), w2_ref[...], preferred_element_type=jnp.float32))


def matmul_residual(a1, a2, w, res, *, a2_time_major=False, tm=PROJ_TM):
    bsz, l, k1 = a1.shape
    m = w.shape[1]
    k2 = w.shape[0] - k1
    tm = min(tm, l)
    wb = w.astype(jnp.bfloat16)
    a2_spec = (pl.BlockSpec((tm, k2), lambda b, i: (i, b)) if a2_time_major
               else pl.BlockSpec((None, tm, k2), lambda b, i: (b, i, 0)))
    row = pl.BlockSpec((None, tm, m), lambda b, i: (b, i, 0))
    return pl.pallas_call(
        _matmul_res_kernel,
        grid=(bsz, l // tm),
        in_specs=[pl.BlockSpec((None, tm, k1), lambda b, i: (b, i, 0)), a2_spec,
                  pl.BlockSpec((k1, m), lambda b, i: (0, 0)),
                  pl.BlockSpec((k2, m), lambda b, i: (0, 0)), row],
        out_specs=row,
        out_shape=jax.ShapeDtypeStruct((bsz, l, m), jnp.float32),
        compiler_params=pltpu.CompilerParams(dimension_semantics=("parallel", "parallel"),
                                             vmem_limit_bytes=VMEM_LIMIT_BYTES),
        name="matmul_residual",
    )(a1, a2, wb[:k1], wb[k1:], res)


HG_LEVELS = tuple(CHUNK >> (i + 1) for i in range(CHUNK.bit_length() - 1))
HG_TOT_ROWS = 8
HG_TT = 512


def hgrn2_constants():
    import numpy as np
    c = CHUNK
    r = np.arange(c)[:, None]
    u = np.arange(c)[None, :]
    stacks, masks = [], []
    for direction in range(2):
        fwd = direction == 0
        lvl_masks = []
        for m in HG_LEVELS:
            blk = r // (2 * m)
            later = (r % (2 * m)) >= m
            lvl_masks.append((blk == blk.T) & (later & ~later.T if fwd else ~later & later.T))
        stacks.append(np.concatenate([(u <= r) if fwd else (u >= r), np.ones((HG_TOT_ROWS, c), bool)], axis=0))
        masks.append(np.stack(lvl_masks))
    return (jnp.asarray(np.stack(stacks), jnp.bfloat16), jnp.asarray(np.stack(masks), jnp.float32))


def _hgrn2_kernel(q_ref, f_ref, v_ref, loglb_ref, log1mlb_ref, onemlb_ref, ast_ref, mask_ref, o_ref,
                  st_sc, qd_sc, dec_sc, upd_sc, sin_sc, *, nc):
    direction = pl.program_id(2)

    @pl.when(pl.program_id(3) == 0)
    def _():
        st_sc[...] = jnp.zeros_like(st_sc)

    bf16 = jnp.bfloat16
    f32 = jnp.float32
    c = CHUNK
    hd = A_HEAD_DIM
    dirf = direction.astype(f32)
    loglb = loglb_ref[...]
    log1mlb = log1mlb_ref[...]
    onemlb = onemlb_ref[...]
    ast = ast_ref[...]
    contract_last = (((1,), (1,)), ((), ()))
    contract_first = (((0,), (0,)), ((), ()))

    ns = range(nc)
    rows = [slice(n * c, (n + 1) * c) for n in ns]
    z = [f_ref[r, :] for r in rows]
    v = [v_ref[r, :] for r in rows]
    qr = [q_ref[r, :] for r in rows]
    q = [x * jax.nn.sigmoid(x) for x in qr]
    e = [jnp.exp(-jnp.abs(x)) for x in z]
    cc = [log1mlb + jnp.minimum(z[n], 0.0) - jnp.log1p(e[n]) for n in ns]
    lf = [jnp.maximum(loglb, x) + jnp.log1p(jnp.exp(-jnp.abs(loglb - x))) for x in cc]
    k = [onemlb * jnp.where(z[n] >= 0, e[n], 1.0) / (1.0 + e[n]) for n in ns]
    hi = [x.astype(bf16) for x in lf]
    lo = [(lf[n] - hi[n].astype(f32)).astype(bf16) for n in ns]
    d = [jnp.dot(ast, hi[n], preferred_element_type=f32) + jnp.dot(ast, lo[n], preferred_element_type=f32)
         for n in ns]
    cum = [x[0:c] for x in d]
    tot = [x[c:c + HG_TOT_ROWS] for x in d]
    ref = [cum[n] - dirf * lf[n] for n in ns]
    attn = [jnp.zeros((c, c), f32) for _ in ns]
    for li, m in enumerate(HG_LEVELS):
        nb = c // (2 * m)
        split = [jnp.broadcast_to(x.reshape(nb, 2 * m, hd)[:, m - 1:m, :], (nb, 2 * m, hd)).reshape(c, hd)
                 for x in ref]
        x = [jnp.exp(-jnp.abs(cum[n] - split[n])) for n in ns]
        s = [lax.dot_general((q[n] * x[n]).astype(bf16), (k[n] * x[n]).astype(bf16), contract_last,
                             preferred_element_type=f32) for n in ns]
        attn = [attn[n] + mask_ref[li] * s[n] for n in ns]
    vb = [x.astype(bf16) for x in v]
    intra = [jnp.dot(attn[n].astype(bf16), vb[n], preferred_element_type=f32) for n in ns]
    upd = [lax.dot_general(vb[n], (k[n] * jnp.exp(tot[n][0:1] - cum[n])).astype(bf16), contract_first,
                           preferred_element_type=f32) for n in ns]
    for n in ns:
        o_ref[rows[n], :] = intra[n] + jnp.sum(q[n] * k[n], axis=-1, keepdims=True) * v[n]
        qd_sc[n] = (q[n] * jnp.exp(cum[n])).astype(bf16)
        dec_sc[n] = jnp.exp(tot[n])
        upd_sc[n] = upd[n]

    def body(ci, st):
        ce = ci + direction * (nc - 1 - 2 * ci)
        sin_sc[ce] = st.astype(bf16)
        return st * dec_sc[ce][0:1] + upd_sc[ce]

    st_sc[...] = lax.fori_loop(0, nc, body, st_sc[...])

    for n in range(nc):
        rows = slice(n * c, (n + 1) * c)
        o_ref[rows, :] += lax.dot_general(qd_sc[n], sin_sc[n], contract_last, preferred_element_type=f32)


def hgrn2_scan(proj3, lb):
    bsz, l, _ = proj3.shape
    hd = A_HEAD_DIM
    tt = min(HG_TT, l)
    nt = l // tt
    assert l % tt == 0 and tt % CHUNK == 0
    ast, masks = hgrn2_constants()
    lb = lb.astype(jnp.float32)
    vecs = [jnp.log(lb).reshape(2, 1, A_WIDTH), jnp.log1p(-lb).reshape(2, 1, A_WIDTH), (1.0 - lb).reshape(2, 1, A_WIDTH)]
    tidx = lambda d, i: i + d * (nt - 1 - 2 * i)
    vec = pl.BlockSpec((None, 1, hd), lambda b, h, d, i: (d, 0, h))
    return pl.pallas_call(
        functools.partial(_hgrn2_kernel, nc=tt // CHUNK),
        grid=(bsz, A_HEADS, 2, nt),
        in_specs=[pl.BlockSpec((None, tt, hd), lambda b, h, d, i: (b, tidx(d, i), h)),
                  pl.BlockSpec((None, tt, hd), lambda b, h, d, i: (b, tidx(d, i), (1 + d) * A_HEADS + h)),
                  pl.BlockSpec((None, tt, hd), lambda b, h, d, i: (b, tidx(d, i), 3 * A_HEADS + h)),
                  vec, vec, vec,
                  pl.BlockSpec((None,) + ast.shape[1:], lambda b, h, d, i: (d, 0, 0)),
                  pl.BlockSpec((None,) + masks.shape[1:], lambda b, h, d, i: (d, 0, 0, 0))],
        out_specs=pl.BlockSpec((None, None, tt, hd), lambda b, h, d, i: (d, b, tidx(d, i), h)),
        out_shape=jax.ShapeDtypeStruct((2, bsz, l, A_WIDTH), jnp.float32),
        scratch_shapes=[pltpu.VMEM((hd, hd), jnp.float32),
                        pltpu.VMEM((tt // CHUNK, CHUNK, hd), jnp.bfloat16),
                        pltpu.VMEM((tt // CHUNK, HG_TOT_ROWS, hd), jnp.float32),
                        pltpu.VMEM((tt // CHUNK, hd, hd), jnp.float32),
                        pltpu.VMEM((tt // CHUNK, hd, hd), jnp.bfloat16)],
        compiler_params=pltpu.CompilerParams(dimension_semantics=("parallel", "parallel", "parallel", "arbitrary"),
                                             vmem_limit_bytes=VMEM_LIMIT_BYTES),
        name="hgrn2_scan",
    )(proj3, proj3, proj3, *vecs, ast, masks)


def _hgrn2_final_kernel(of_ref, ob_ref, g_ref, gain_ref, o_ref):
    o = of_ref[...] + ob_ref[...]
    y = o * lax.rsqrt(jnp.mean(o * o, axis=-1, keepdims=True) + EPS) * gain_ref[...]
    g = g_ref[...]
    o_ref[...] = y * (g * jax.nn.sigmoid(g))


def bidir_finalize(o2, proj3, gate_block, out_gain, *, name, tm=1024):
    _, bsz, l, w = o2.shape
    hd = 128
    tm = min(tm, l)
    return pl.pallas_call(
        _hgrn2_final_kernel,
        grid=(bsz, l // tm, w // hd),
        in_specs=[pl.BlockSpec((None, None, tm, hd), lambda b, i, h: (0, b, i, h)),
                  pl.BlockSpec((None, None, tm, hd), lambda b, i, h: (1, b, i, h)),
                  pl.BlockSpec((None, tm, hd), lambda b, i, h: (b, i, gate_block + h)),
                  pl.BlockSpec((1, hd), lambda b, i, h: (0, 0))],
        out_specs=pl.BlockSpec((None, tm, hd), lambda b, i, h: (b, i, h)),
        out_shape=jax.ShapeDtypeStruct((bsz, l, w), jnp.float32),
        compiler_params=pltpu.CompilerParams(dimension_semantics=("parallel", "parallel", "parallel"),
                                             vmem_limit_bytes=VMEM_LIMIT_BYTES),
        name=name,
    )(o2, o2, proj3, out_gain.reshape(1, hd).astype(jnp.float32))


def hgrn2_mixer(proj3, lb, out_gain):
    return bidir_finalize(hgrn2_scan(proj3, lb), proj3, 4 * A_HEADS, out_gain, name="hgrn2_finalize")


S5_NS = S5_GROUPS * S5_STATE
S5_TT = 64
SUBLANES = 8


def _s5_scan_kernel(u_ref, win_ref, ar_ref, ai_ref, wout_ref, y_ref, bu_sc, xs_sc, st_sc, *, bsz, tt, reverse):
    @pl.when(pl.program_id(0) == 0)
    def _():
        st_sc[...] = jnp.zeros_like(st_sc)

    bu_sc[...] = jnp.dot(u_ref[...].astype(jnp.bfloat16), win_ref[...], preferred_element_type=jnp.float32)
    ar = jnp.broadcast_to(ar_ref[...], (bsz, S5_NS))
    ai = jnp.broadcast_to(ai_ref[...], (bsz, S5_NS))
    per = SUBLANES // bsz
    ngroups = tt // per

    def body(s, carry):
        xr, xi = carry
        p = (ngroups - 1 - s) if reverse else s
        base = pl.multiple_of(p * SUBLANES, SUBLANES)
        blk = bu_sc[pl.ds(base, SUBLANES), :]
        outs_r = [None] * per
        outs_i = [None] * per
        for ph in (range(per - 1, -1, -1) if reverse else range(per)):
            br = blk[ph * bsz:(ph + 1) * bsz, :S5_NS]
            bi = blk[ph * bsz:(ph + 1) * bsz, S5_NS:]
            xr, xi = ar * xr - ai * xi + br, ar * xi + ai * xr + bi
            outs_r[ph] = xr
            outs_i[ph] = xi
        xs_sc[pl.ds(base, SUBLANES), :S5_NS] = jnp.concatenate(outs_r, axis=0)
        xs_sc[pl.ds(base, SUBLANES), S5_NS:] = jnp.concatenate(outs_i, axis=0)
        return xr, xi

    xr, xi = lax.fori_loop(0, ngroups, body, (st_sc[0], st_sc[1]))
    st_sc[0] = xr
    st_sc[1] = xi
    y_ref[...] = jnp.dot(xs_sc[...].astype(jnp.bfloat16), wout_ref[...], preferred_element_type=jnp.float32)


def s5_scan(u_tb, win, ar, ai, wout, *, bsz, reverse):
    n = u_tb.shape[0]
    rows = S5_TT * bsz
    nt = n // rows
    assert n % rows == 0 and SUBLANES % bsz == 0
    idx = (lambda i: (nt - 1 - i, 0)) if reverse else (lambda i: (i, 0))
    const = lambda i: (0, 0)
    return pl.pallas_call(
        functools.partial(_s5_scan_kernel, bsz=bsz, tt=S5_TT, reverse=reverse),
        grid=(nt,),
        in_specs=[pl.BlockSpec((rows, B_WIDTH), idx),
                  pl.BlockSpec((B_WIDTH, 2 * S5_NS), const),
                  pl.BlockSpec((1, S5_NS), const),
                  pl.BlockSpec((1, S5_NS), const),
                  pl.BlockSpec((2 * S5_NS, B_WIDTH), const)],
        out_specs=pl.BlockSpec((rows, B_WIDTH), idx),
        out_shape=jax.ShapeDtypeStruct((n, B_WIDTH), jnp.float32),
        scratch_shapes=[pltpu.VMEM((rows, 2 * S5_NS), jnp.float32),
                        pltpu.VMEM((rows, 2 * S5_NS), jnp.float32),
                        pltpu.VMEM((2, bsz, S5_NS), jnp.float32)],
        compiler_params=pltpu.CompilerParams(dimension_semantics=("arbitrary",),
                                             vmem_limit_bytes=VMEM_LIMIT_BYTES),
        name="s5_scan_bwd" if reverse else "s5_scan_fwd",
    )(u_tb, win, ar, ai, wout)


def _s5_final_kernel(u_ref, yf_ref, yb_ref, d_ref, w_ref, b_ref, o_ref):
    y = d_ref[...] * u_ref[...] + yf_ref[...] + yb_ref[...]
    y = jax.nn.gelu(y)
    z = jnp.dot(y.astype(jnp.bfloat16), w_ref[...], preferred_element_type=jnp.float32) + b_ref[...]
    o_ref[...] = y * jax.nn.sigmoid(z)


def s5_finalize(u, yf, yb, d_skip, glu_w, glu_b, *, tm=512):
    n, w = u.shape
    row = pl.BlockSpec((tm, w), lambda i: (i, 0))
    vec = pl.BlockSpec((1, w), lambda i: (0, 0))
    return pl.pallas_call(
        _s5_final_kernel,
        grid=(n // tm,),
        in_specs=[row, row, row, vec, pl.BlockSpec((w, w), lambda i: (0, 0)), vec],
        out_specs=row,
        out_shape=jax.ShapeDtypeStruct((n, w), jnp.float32),
        compiler_params=pltpu.CompilerParams(dimension_semantics=("parallel",),
                                             vmem_limit_bytes=VMEM_LIMIT_BYTES),
        name="s5_finalize",
    )(u, yf, yb, d_skip.reshape(1, w).astype(jnp.float32), glu_w.astype(jnp.bfloat16),
      glu_b.reshape(1, w).astype(jnp.float32))


def s5_direction_params(lam_re, lam_im, log_step, b_re, b_im, c_re, c_im):
    step = jnp.exp(log_step)[:, None]
    mag = jnp.exp(lam_re * step)
    abar_re = mag * jnp.cos(lam_im * step)
    abar_im = mag * jnp.sin(lam_im * step)
    den = lam_re * lam_re + lam_im * lam_im
    fr = ((abar_re - 1.0) * lam_re + abar_im * lam_im) / den
    fi = (abar_im * lam_re - (abar_re - 1.0) * lam_im) / den
    bb_re = fr[..., None] * b_re - fi[..., None] * b_im
    bb_im = fr[..., None] * b_im + fi[..., None] * b_re
    eye = jnp.eye(S5_GROUPS, dtype=jnp.float32)
    win = jnp.concatenate([jnp.einsum('gnp,gh->gphn', bb, eye).reshape(B_WIDTH, S5_NS) for bb in (bb_re, bb_im)],
                          axis=1)
    wout = jnp.concatenate([jnp.einsum('gpn,gh->hngp', c, eye).reshape(S5_NS, B_WIDTH) for c in (c_re, -c_im)],
                           axis=0)
    return (win.astype(jnp.bfloat16), abar_re.reshape(1, S5_NS), abar_im.reshape(1, S5_NS),
            wout.astype(jnp.bfloat16))


def s5_mixer_tb(u_tb, bsz, lam_re, lam_im, log_step, b_re, b_im, c_re, c_im, d_skip, glu_w, glu_b):
    f32 = jnp.float32
    ys = []
    for direction in range(2):
        prm = s5_direction_params(lam_re[direction].astype(f32), lam_im[direction].astype(f32),
                                  log_step[direction].astype(f32), b_re[direction].astype(f32),
                                  b_im[direction].astype(f32), c_re[direction].astype(f32),
                                  c_im[direction].astype(f32))
        ys.append(s5_scan(u_tb, *prm, bsz=bsz, reverse=(direction == 1)))
    return s5_finalize(u_tb, ys[0], ys[1], d_skip, glu_w, glu_b)


def t5_bucket(rel):
    half = REL_BUCKETS // 2
    max_exact = half // 2
    base = jnp.where(rel > 0, half, 0)
    n = jnp.abs(rel)
    nf = jnp.maximum(n, 1).astype(jnp.float32)
    large = max_exact + (jnp.log(nf / max_exact) / math.log(REL_MAX_DIST / max_exact)
                         * (half - max_exact)).astype(jnp.int32)
    large = jnp.minimum(large, half - 1)
    return base + jnp.where(n < max_exact, n, large)


ATT_T = 512
LOG2E = math.log2(math.e)


def rel_bias_tiles(rel_bias, t):
    assert t >= REL_MAX_DIST
    table = rel_bias.astype(jnp.float32) * LOG2E
    tiles = []
    for d in (-1, 0, 1):
        c = table[t5_bucket(d * t + jnp.arange(-(t - 1), t))]
        w = jnp.concatenate([c, c[:1]], axis=0)
        m = jnp.tile(w, (t, 1))[:t * (2 * t - 1)].reshape(t, 2 * t - 1, -1)
        tiles.append(m[:, t - 1:2 * t - 1])
    far_neg = jnp.broadcast_to(table[t5_bucket(jnp.array(-2 * t))], tiles[0].shape)
    far_pos = jnp.broadcast_to(table[t5_bucket(jnp.array(2 * t))], tiles[0].shape)
    out = jnp.stack([far_neg] + tiles + [far_pos], axis=0)
    return jnp.transpose(out, (3, 0, 1, 2))


def _attn_prep_kernel(q_ref, k_ref, v_ref, qg_ref, kg_ref, q2_ref, kt_ref, vb_ref):
    lane = lax.broadcasted_iota(jnp.int32, q_ref.shape, 1)
    lo = lane < C_HEAD_DIM

    def halfnorm(x, g):
        sq = x * x
        s_lo = jnp.sum(jnp.where(lo, sq, 0.0), axis=-1, keepdims=True)
        s_hi = jnp.sum(jnp.where(lo, 0.0, sq), axis=-1, keepdims=True)
        ms = jnp.where(lo, s_lo, s_hi) * (1.0 / C_HEAD_DIM)
        return x * lax.rsqrt(ms + EPS) * g

    qn = halfnorm(q_ref[...], qg_ref[...]) * (C_HEAD_DIM ** -0.5 * LOG2E)
    kn = halfnorm(k_ref[...], kg_ref[...])
    q2_ref[0] = jnp.where(lo, qn, 0.0).astype(jnp.bfloat16)
    q2_ref[1] = jnp.where(lo, 0.0, qn).astype(jnp.bfloat16)
    kt_ref[...] = kn.T.astype(jnp.bfloat16)
    vb_ref[...] = v_ref[...].astype(jnp.bfloat16)


def attn_prep(proj3, q_gain, k_gain, *, tl=512):
    bsz, l, _ = proj3.shape
    hw = 2 * C_HEAD_DIM
    gq = jnp.tile(q_gain.astype(jnp.float32), 2).reshape(1, hw)
    gk = jnp.tile(k_gain.astype(jnp.float32), 2).reshape(1, hw)
    vec = pl.BlockSpec((1, hw), lambda b, h, i: (0, 0))
    return pl.pallas_call(
        _attn_prep_kernel,
        grid=(bsz, C_HEADS, l // tl),
        in_specs=[pl.BlockSpec((None, tl, hw), lambda b, h, i: (b, i, h)),
                  pl.BlockSpec((None, tl, hw), lambda b, h, i: (b, i, C_HEADS + h)),
                  pl.BlockSpec((None, tl, hw), lambda b, h, i: (b, i, 2 * C_HEADS + h)),
                  vec, vec],
        out_specs=[pl.BlockSpec((None, None, 2, tl, hw), lambda b, h, i: (b, h, 0, i, 0)),
                   pl.BlockSpec((None, None, hw, tl), lambda b, h, i: (b, h, 0, i)),
                   pl.BlockSpec((None, None, tl, hw), lambda b, h, i: (b, h, i, 0))],
        out_shape=[jax.ShapeDtypeStruct((bsz, C_HEADS, 2, l, hw), jnp.bfloat16),
                   jax.ShapeDtypeStruct((bsz, C_HEADS, hw, l), jnp.bfloat16),
                   jax.ShapeDtypeStruct((bsz, C_HEADS, l, hw), jnp.bfloat16)],
        compiler_params=pltpu.CompilerParams(dimension_semantics=("parallel", "parallel", "parallel"),
                                             vmem_limit_bytes=VMEM_LIMIT_BYTES),
        name="attn_prep",
    )(proj3, proj3, proj3, gq, gk)


ATT_ROWS = 64


def _attn_kernel(lam_ref, q2_ref, kt_ref, v_ref, bias_ref, g_ref, o_ref, m_sc, l_sc, acc_sc, s_sc, p_sc, a_sc,
                 *, t, nk, out_scale):
    qi = pl.program_id(2)
    q2 = q2_ref[...].reshape(2 * t, 2 * C_HEAD_DIM)
    m_sc[...] = jnp.full(m_sc.shape, -jnp.inf, jnp.float32)
    l_sc[...] = jnp.zeros_like(l_sc)
    acc_sc[...] = jnp.zeros_like(acc_sc)
    r = ATT_ROWS
    hw = 2 * C_HEAD_DIM

    def scores(ki, buf):
        off = pl.multiple_of(ki * t, t)
        s_sc[buf] = jnp.dot(q2, kt_ref[:, pl.ds(off, t)], preferred_element_type=jnp.float32)

    def absorb(ki, buf):
        off = pl.multiple_of(ki * t, t)
        bidx = jnp.clip(ki - qi, -2, 2) + 2
        for g in range(2 * t // r):
            rows = slice(g * r, (g + 1) * r)
            brow = (g * r) % t
            s = s_sc[buf, rows, :] + bias_ref[bidx, brow:brow + r, :]
            m_prev = m_sc[rows, :]
            m_new = jnp.maximum(m_prev, jnp.max(s, axis=-1, keepdims=True))
            alpha = jnp.exp2(m_prev - m_new)
            ps = [jnp.exp2(s[:, j * hw:(j + 1) * hw] - m_new) for j in range(t // hw)]
            l_sc[rows, :] = alpha * l_sc[rows, :] + jnp.sum(sum(ps), axis=-1, keepdims=True)
            m_sc[rows, :] = m_new
            a_sc[rows, :] = alpha
            for j in range(t // hw):
                p_sc[rows, j * hw:(j + 1) * hw] = ps[j].astype(jnp.bfloat16)
        acc_sc[...] = a_sc[...] * acc_sc[...] + jnp.dot(p_sc[...], v_ref[pl.ds(off, t), :],
                                                        preferred_element_type=jnp.float32)

    def body(ki, carry):
        scores(ki, 0)
        absorb(ki, 0)
        return carry

    lax.fori_loop(0, nk, body, 0)
    a = acc_sc[...] / l_sc[...]
    o = a[:t] - lam_ref[0] * a[t:]
    y = o * lax.rsqrt(jnp.mean(o * o, axis=-1, keepdims=True) + EPS)
    o_ref[...] = y * g_ref[...] * out_scale


def diff_attention(proj3, q_gain, k_gain, lam, out_gain, bias5, layer_idx):
    f32 = jnp.float32
    bsz, l, _ = proj3.shape
    t = ATT_T
    hw = 2 * C_HEAD_DIM
    lam_init = 0.8 - 0.6 * math.exp(-0.3 * layer_idx)
    lam_f = lam.astype(f32)
    lam_full = jnp.exp(jnp.sum(lam_f[0] * lam_f[1])) - jnp.exp(jnp.sum(lam_f[2] * lam_f[3])) + lam_init
    q2, kt, vb = attn_prep(proj3, q_gain, k_gain)
    return pl.pallas_call(
        functools.partial(_attn_kernel, t=t, nk=l // t, out_scale=1.0 - lam_init),
        grid=(bsz, C_HEADS, l // t),
        in_specs=[pl.BlockSpec(memory_space=pltpu.SMEM),
                  pl.BlockSpec((None, None, 2, t, hw), lambda b, h, i: (b, h, 0, i, 0)),
                  pl.BlockSpec((None, None, hw, l), lambda b, h, i: (b, h, 0, 0)),
                  pl.BlockSpec((None, None, l, hw), lambda b, h, i: (b, h, 0, 0)),
                  pl.BlockSpec((None, 5, t, t), lambda b, h, i: (h, 0, 0, 0)),
                  pl.BlockSpec((1, hw), lambda b, h, i: (0, 0))],
        out_specs=pl.BlockSpec((None, t, hw), lambda b, h, i: (b, i, h)),
        out_shape=jax.ShapeDtypeStruct((bsz, l, C_WIDTH), f32),
        scratch_shapes=[pltpu.VMEM((2 * t, hw), f32), pltpu.VMEM((2 * t, hw), f32), pltpu.VMEM((2 * t, hw), f32),
                        pltpu.VMEM((1, 2 * t, t), f32), pltpu.VMEM((2 * t, t), jnp.bfloat16), pltpu.VMEM((2 * t, hw), f32)],
        compiler_params=pltpu.CompilerParams(dimension_semantics=("parallel", "parallel", "arbitrary"),
                                             vmem_limit_bytes=VMEM_LIMIT_BYTES),
        name="diff_attention",
    )(lam_full.reshape(1), q2, kt, vb, bias5, out_gain.reshape(1, hw).astype(f32))


GDN_TT = 512
LANES = 128
OD_QKV_BLOCK = 3 * C_WIDTH // LANES
OD_GATE_BLOCK = OD_QKV_BLOCK + 3 * D_WIDTH // LANES
OD_AB_BLOCK = OD_GATE_BLOCK + D_WIDTH // LANES
OD_COLS = 3840


def _gdn_prep_kernel(prev_ref, cur_ref, next_ref, w_ref, o_ref, *, tl, nl):
    i = pl.program_id(1)
    part = pl.program_id(2)
    prev = jnp.where(i > 0, prev_ref[...], 0.0)
    nxt = jnp.where(i < nl - 1, next_ref[...], 0.0)
    ext = jnp.concatenate([prev, cur_ref[...], nxt], axis=0)
    halo = prev.shape[0]
    acc = None
    for j in range(CONV_WIDTH):
        start = halo - CONV_WIDTH // 2 + j
        term = w_ref[j:j + 1, :] * ext[start:start + tl, :]
        acc = term if acc is None else acc + term
    y = acc * jax.nn.sigmoid(acc)
    scale = jnp.where(part == 0, D_HEAD_DIM ** -0.5, 1.0)
    heads = []
    for h in range(D_HEADS):
        yh = y[:, h * LANES:(h + 1) * LANES]
        heads.append(yh * (lax.rsqrt(jnp.sum(yh * yh, axis=-1, keepdims=True) + EPS) * scale))
    o_ref[...] = jnp.where(part < 2, jnp.concatenate(heads, axis=1), y)


def gdn_prep(proj3, conv_w, *, tl=512):
    bsz, l, _ = proj3.shape
    halo = SUBLANES
    nl = l // tl
    blk0 = OD_QKV_BLOCK * LANES // D_WIDTH
    return pl.pallas_call(
        functools.partial(_gdn_prep_kernel, tl=tl, nl=nl),
        grid=(bsz, nl, 3),
        in_specs=[pl.BlockSpec((None, halo, D_WIDTH), lambda b, i, p: (b, jnp.maximum(i * (tl // halo) - 1, 0), blk0 + p)),
                  pl.BlockSpec((None, tl, D_WIDTH), lambda b, i, p: (b, i, blk0 + p)),
                  pl.BlockSpec((None, halo, D_WIDTH),
                               lambda b, i, p: (b, jnp.minimum((i + 1) * (tl // halo), l // halo - 1), blk0 + p)),
                  pl.BlockSpec((CONV_WIDTH, D_WIDTH), lambda b, i, p: (0, p))],
        out_specs=pl.BlockSpec((None, None, tl, D_WIDTH), lambda b, i, p: (p, b, i, 0)),
        out_shape=jax.ShapeDtypeStruct((3, bsz, l, D_WIDTH), jnp.float32),
        compiler_params=pltpu.CompilerParams(dimension_semantics=("parallel", "parallel", "parallel"),
                                             vmem_limit_bytes=VMEM_LIMIT_BYTES),
        name="gdn_prep",
    )(proj3, proj3, proj3, conv_w.astype(jnp.float32))


def _gdn_gates_kernel(x_ref, nega_ref, dtb_ref, o_ref):
    x = x_ref[...]
    z = x + dtb_ref[...]
    g = nega_ref[...] * (jnp.maximum(z, 0.0) + jnp.log1p(jnp.exp(-jnp.abs(z))))
    lane = lax.broadcasted_iota(jnp.int32, x.shape, 1)
    y = jnp.where(lane < 2 * D_HEADS, g, jax.nn.sigmoid(x))
    o_ref[...] = y.T[0:4 * D_HEADS, :]


def gdn_gates(proj3, a_log, dt_bias, *, tl=512):
    bsz, l, _ = proj3.shape
    pad = LANES - 2 * D_HEADS
    nega = jnp.pad(-jnp.exp(a_log.astype(jnp.float32)).reshape(1, -1), ((0, 0), (0, pad)))
    dtb = jnp.pad(dt_bias.astype(jnp.float32).reshape(1, -1), ((0, 0), (0, pad)))
    vec = pl.BlockSpec((1, LANES), lambda b, i: (0, 0))
    return pl.pallas_call(
        _gdn_gates_kernel,
        grid=(bsz, l // tl),
        in_specs=[pl.BlockSpec((None, tl, LANES), lambda b, i: (b, i, OD_AB_BLOCK)), vec, vec],
        out_specs=pl.BlockSpec((None, 4 * D_HEADS, tl), lambda b, i: (b, 0, i)),
        out_shape=jax.ShapeDtypeStruct((bsz, 4 * D_HEADS, l), jnp.float32),
        compiler_params=pltpu.CompilerParams(dimension_semantics=("parallel", "parallel"),
                                             vmem_limit_bytes=VMEM_LIMIT_BYTES),
        name="gdn_gates",
    )(proj3, nega, dtb)


def gdn_constants():
    import numpy as np
    c = CHUNK
    r = np.arange(c)[:, None]
    u = np.arange(c)[None, :]
    cum, incl, strict = [], [], []
    for direction in range(2):
        fwd = direction == 0
        cum.append(np.concatenate([(r <= u) if fwd else (r >= u), np.ones((c, c), bool)], axis=1))
        incl.append((u <= r) if fwd else (u >= r))
        strict.append((u < r) if fwd else (u > r))
    same = lambda b: (r // b) == (u // b)
    merges = [same(2 * b) & ~same(b) for b in (8, 16, 32)]
    f32 = jnp.float32
    return (jnp.asarray(np.stack(cum), jnp.bfloat16), jnp.asarray(np.stack(incl), f32),
            jnp.asarray(np.stack(strict), f32), jnp.asarray(same(8), f32), jnp.asarray(np.stack(merges), f32))


def _gdn_kernel(q_ref, k_ref, v_ref, g_ref, b_ref, cum_ref, incl_ref, strict_ref, d8_ref, mrg_ref, o_ref,
                s_sc, qd_sc, dec_sc, w_sc, u_sc, sin_sc, *, nc):
    direction = pl.program_id(2)

    @pl.when(pl.program_id(3) == 0)
    def _():
        s_sc[...] = jnp.zeros_like(s_sc)

    bf16 = jnp.bfloat16
    f32 = jnp.float32
    c = CHUNK
    hd = D_HEAD_DIM
    contract_last = (((1,), (1,)), ((), ()))
    contract_first = (((0,), (0,)), ((), ()))
    cumm = cum_ref[...]
    incl = incl_ref[...]
    strict = strict_ref[...]
    d8 = d8_ref[...]
    eye = (lax.broadcasted_iota(jnp.int32, (c, c), 0) == lax.broadcasted_iota(jnp.int32, (c, c), 1)).astype(f32)

    def mm(a, b):
        return jnp.dot(a.astype(bf16), b.astype(bf16), preferred_element_type=f32)

    def rep(x):
        return jnp.concatenate([x] * (hd // c), axis=1)

    ns = range(nc)
    rows = [slice(n * c, (n + 1) * c) for n in ns]
    q = [q_ref[r, :] for r in rows]
    k = [k_ref[r, :] for r in rows]
    v = [v_ref[r, :] for r in rows]
    kb = [x.astype(bf16) for x in k]
    kk = [lax.dot_general(x, x, contract_last, preferred_element_type=f32) for x in kb]
    qk = [lax.dot_general(q[n].astype(bf16), kb[n], contract_last, preferred_element_type=f32) for n in ns]
    grow = [jnp.broadcast_to(g_ref[:, r], (c, c)) for r in rows]
    ghi = [x.astype(bf16) for x in grow]
    glo = [(grow[n] - ghi[n].astype(f32)).astype(bf16) for n in ns]
    gm = [jnp.dot(ghi[n], cumm, preferred_element_type=f32) + jnp.dot(glo[n], cumm, preferred_element_type=f32)
          for n in ns]
    gam_row = [x[:, :c] for x in gm]
    tot = [x[:, c:] for x in gm]
    gam_col = [x.T for x in gam_row]
    beta_col = [jnp.broadcast_to(b_ref[:, r], (c, c)).T for r in rows]
    decay = [incl * jnp.exp(jnp.minimum(gam_col[n] - gam_row[n], 0.0)) for n in ns]
    a = [strict * beta_col[n] * kk[n] * decay[n] for n in ns]
    a0 = [x * d8 for x in a]
    n2 = [mm(x, x) for x in a0]
    n4 = [mm(x, x) for x in n2]
    t = [mm(eye - a0[n], eye + n2[n]) for n in ns]
    t = [mm(t[n], eye + n4[n]) for n in ns]
    for j in range(mrg_ref.shape[0]):
        p = [mm(a[n] * mrg_ref[j], t[n]) for n in ns]
        t = [t[n] - mm(t[n], p[n]) for n in ns]
    beta128 = [rep(x) for x in beta_col]
    egam128 = [rep(jnp.exp(x)) for x in gam_col]
    solb = [mm(t[n], jnp.concatenate([k[n] * beta128[n] * egam128[n], v[n] * beta128[n]], axis=1)).astype(bf16)
            for n in ns]
    av = [jnp.dot((qk[n] * decay[n]).astype(bf16), solb[n], preferred_element_type=f32) for n in ns]
    k_dec = [(k[n] * rep(jnp.exp(tot[n] - gam_col[n]))).astype(bf16) for n in ns]
    wu = [lax.dot_general(k_dec[n], solb[n], contract_first, preferred_element_type=f32) for n in ns]
    for n in ns:
        qd_sc[n] = (q[n] * egam128[n] - av[n][:, :hd]).astype(bf16)
        o_ref[rows[n], :] = av[n][:, hd:]
        w_sc[n] = wu[n][:, :hd].astype(bf16)
        u_sc[n] = wu[n][:, hd:]
        dec_sc[n] = rep(jnp.exp(tot[n][0:SUBLANES, :]))

    def body(ci, s):
        ce = ci + direction * (nc - 1 - 2 * ci)
        sb = s.astype(bf16)
        sin_sc[ce] = sb
        return s * dec_sc[ce][0:1] - jnp.dot(w_sc[ce], sb, preferred_element_type=f32) + u_sc[ce]

    s_sc[...] = lax.fori_loop(0, nc, body, s_sc[...])

    for n in range(nc):
        rows = slice(n * c, (n + 1) * c)
        o_ref[rows, :] += jnp.dot(qd_sc[n], sin_sc[n], preferred_element_type=f32)


def gdn_scan(qkv, gb):
    _, bsz, l, _ = qkv.shape
    hd = D_HEAD_DIM
    tt = min(GDN_TT, l)
    nt = l // tt
    nc = tt // CHUNK
    assert l % tt == 0 and tt % CHUNK == 0
    consts = gdn_constants()
    gb4 = gb.reshape(bsz, 4 * D_HEADS, 1, l)
    tidx = lambda d, i: i + d * (nt - 1 - 2 * i)
    qkv_spec = lambda p: pl.BlockSpec((None, None, tt, hd), lambda b, h, d, i: (p, b, tidx(d, i), h))
    row_spec = lambda off: pl.BlockSpec((None, None, 1, tt),
                                        lambda b, h, d, i: (b, off + d * D_HEADS + h, 0, tidx(d, i)))
    per_dir = lambda a: pl.BlockSpec((None,) + a.shape[1:], lambda b, h, d, i: (d,) + (0,) * (a.ndim - 1))
    whole = lambda a: pl.BlockSpec(a.shape, lambda b, h, d, i: (0,) * a.ndim)
    return pl.pallas_call(
        functools.partial(_gdn_kernel, nc=nc),
        grid=(bsz, D_HEADS, 2, nt),
        in_specs=[qkv_spec(0), qkv_spec(1), qkv_spec(2), row_spec(0), row_spec(2 * D_HEADS),
                  per_dir(consts[0]), per_dir(consts[1]), per_dir(consts[2]), whole(consts[3]), whole(consts[4])],
        out_specs=pl.BlockSpec((None, None, tt, hd), lambda b, h, d, i: (d, b, tidx(d, i), h)),
        out_shape=jax.ShapeDtypeStruct((2, bsz, l, D_WIDTH), jnp.float32),
        scratch_shapes=[pltpu.VMEM((hd, hd), jnp.float32),
                        pltpu.VMEM((nc, CHUNK, hd), jnp.bfloat16),
                        pltpu.VMEM((nc, SUBLANES, hd), jnp.float32),
                        pltpu.VMEM((nc, hd, hd), jnp.bfloat16),
                        pltpu.VMEM((nc, hd, hd), jnp.float32),
                        pltpu.VMEM((nc, hd, hd), jnp.bfloat16)],
        compiler_params=pltpu.CompilerParams(dimension_semantics=("parallel", "parallel", "parallel", "arbitrary"),
                                             vmem_limit_bytes=VMEM_LIMIT_BYTES),
        name="gdn_scan",
    )(qkv, qkv, qkv, gb4, gb4, *consts)


def gated_deltanet(proj3, conv_w, a_log, dt_bias, out_gain):
    o2 = gdn_scan(gdn_prep(proj3, conv_w), gdn_gates(proj3, a_log, dt_bias))
    return bidir_finalize(o2, proj3, OD_GATE_BLOCK, out_gain, name="gdn_finalize")


MOE_TT = 512
MOE_SUB = 128
MOE_ROWS = 256
MOE_SLAB = 256
MOE_VMEM_LIMIT_BYTES = 56 * 1024 * 1024


def _router_kernel(x_ref, g_ref, wr_ref, h_ref, aff_ref):
    x = x_ref[...]
    h = (x * lax.rsqrt(jnp.mean(x * x, axis=-1, keepdims=True) + EPS) * g_ref[...]).astype(jnp.bfloat16)
    h_ref[...] = h
    logits = jnp.dot(h, wr_ref[...], preferred_element_type=jnp.float32)
    lane = lax.broadcasted_iota(jnp.int32, logits.shape, 1)
    logits = jnp.where(lane < N_EXPERTS, logits, -jnp.inf)
    p = jnp.exp(logits - jnp.max(logits, axis=-1, keepdims=True))
    aff = p / jnp.sum(p, axis=-1, keepdims=True)
    aff_ref[...] = aff.T[0:N_EXPERTS, :]


def moe_route(x, gain, w_router, *, tm=MOE_TT):
    bsz, l, d = x.shape
    wr = jnp.pad(w_router.astype(jnp.bfloat16), ((0, 0), (0, LANES - N_EXPERTS)))
    return pl.pallas_call(
        _router_kernel,
        grid=(bsz, l // tm),
        in_specs=[pl.BlockSpec((None, tm, d), lambda b, i: (b, i, 0)),
                  pl.BlockSpec((1, d), lambda b, i: (0, 0)),
                  pl.BlockSpec((d, LANES), lambda b, i: (0, 0))],
        out_specs=[pl.BlockSpec((None, tm, d), lambda b, i: (b, i, 0)),
                   pl.BlockSpec((None, N_EXPERTS, tm), lambda b, i: (b, 0, i))],
        out_shape=[jax.ShapeDtypeStruct((bsz, l, d), jnp.bfloat16),
                   jax.ShapeDtypeStruct((bsz, N_EXPERTS, l), jnp.float32)],
        compiler_params=pltpu.CompilerParams(dimension_semantics=("parallel", "parallel"),
                                             vmem_limit_bytes=VMEM_LIMIT_BYTES),
        name="moe_router",
    )(x, gain.reshape(1, d).astype(jnp.float32), wr)


def _select_kernel(aff_ref, pre_ref, smap_ref, gate_ref, cnt_ref, *, cap, tt):
    f32 = jnp.float32
    bf16 = jnp.bfloat16
    aff = aff_ref[...]
    e, l = aff.shape
    nl = l // LANES
    tiles = [slice(j * LANES, (j + 1) * LANES) for j in range(nl)]
    bits = pltpu.bitcast(aff, jnp.int32)
    bt = [bits[:, s] for s in tiles]

    def lane_total(x):
        return jnp.broadcast_to(jnp.sum(x, axis=-1, keepdims=True), (e, LANES))

    def search(i, thr):
        cand = thr | jnp.left_shift(jnp.int32(1), 30 - i)
        acc = jnp.zeros((e, LANES), jnp.int32)
        for x in bt:
            acc = acc + (x >= cand).astype(jnp.int32)
        return jnp.where(lane_total(acc) >= cap, cand, thr)

    thr = lax.fori_loop(0, 31, search, jnp.zeros((e, LANES), jnp.int32))
    gt = [x > thr for x in bt]
    eq = [x == thr for x in bt]
    acc = jnp.zeros((e, LANES), jnp.int32)
    for x in gt:
        acc = acc + x.astype(jnp.int32)
    need = (cap - lane_total(acc)).astype(f32)

    pre = pre_ref[...]

    def prefix(flags):
        outs = [jnp.dot(jnp.where(x, 1.0, 0.0).astype(bf16), pre, preferred_element_type=f32) for x in flags]
        carry = jnp.zeros((e, LANES), f32)
        res = []
        for o in outs:
            res.append(o[:, :LANES] + carry)
            carry = carry + o[:, LANES:]
        return res, [o[:, LANES:] for o in outs]

    rank_eq, _ = prefix(eq)
    sel = [jnp.logical_or(gt[j], jnp.logical_and(eq[j], rank_eq[j] < need)) for j in range(nl)]
    pos, totals = prefix(sel)
    lane = lax.broadcasted_iota(jnp.int32, (e, LANES), 1)
    cnt = jnp.zeros((e, LANES), f32)
    per = tt // LANES
    for j in range(nl):
        smap_ref[:, tiles[j]] = jnp.where(sel[j], pos[j], -1.0)
        gate_ref[:, tiles[j]] = jnp.where(sel[j], aff[:, tiles[j]], 0.0)
        cnt = cnt + jnp.where(lane == j // per, totals[j], 0.0)
    cnt_ref[...] = cnt


def moe_select(aff, cap, *, tt=MOE_TT):
    import numpy as np
    bsz, e, l = aff.shape
    assert l // tt <= LANES
    i = np.arange(LANES)
    pre = np.concatenate([i[:, None] < i[None, :], np.ones((LANES, LANES), bool)], axis=1)
    row = pl.BlockSpec((None, e, l), lambda b: (b, 0, 0))
    return pl.pallas_call(
        functools.partial(_select_kernel, cap=cap, tt=tt),
        grid=(bsz,),
        in_specs=[row, pl.BlockSpec((LANES, 2 * LANES), lambda b: (0, 0))],
        out_specs=[row, row, pl.BlockSpec((None, e, LANES), lambda b: (b, 0, 0))],
        out_shape=[jax.ShapeDtypeStruct((bsz, e, l), jnp.float32), jax.ShapeDtypeStruct((bsz, e, l), jnp.float32),
                   jax.ShapeDtypeStruct((bsz, e, LANES), jnp.float32)],
        compiler_params=pltpu.CompilerParams(dimension_semantics=("parallel",),
                                             vmem_limit_bytes=VMEM_LIMIT_BYTES),
        name="moe_select",
    )(aff, jnp.asarray(pre, jnp.bfloat16))


def _slot_one_hot(pos, base, rows, n):
    slot = (base + lax.broadcasted_iota(jnp.int32, (rows, n), 0)).astype(jnp.float32)
    return jnp.where(pos == slot, 1.0, 0.0).astype(jnp.bfloat16)


def _expert_kernel(cs_ref, h_ref, smap_ref, gate_ref, wg32_ref, wu32_ref, wd32_ref, o_ref,
                   xs_sc, gs_sc, wg_ref, wu_ref, wd_ref, *, nj, cap):
    e = pl.program_id(0)
    b = pl.program_id(1)
    j = pl.program_id(2)
    f32 = jnp.float32
    bf16 = jnp.bfloat16

    @pl.when(jnp.logical_and(b == 0, j == 0))
    def _():
        wg_ref[...] = wg32_ref[...].astype(bf16)
        wu_ref[...] = wu32_ref[...].astype(bf16)
        wd_ref[...] = wd32_ref[...].astype(bf16)

    @pl.when(j == 0)
    def _():
        xs_sc[...] = jnp.zeros_like(xs_sc)
        gs_sc[...] = jnp.zeros_like(gs_sc)

    base = (b * N_EXPERTS + e) * (nj + 1) + j
    c0 = cs_ref[base]
    c1 = cs_ref[base + 1]
    pos = smap_ref[...]
    gate = gate_ref[...]
    hb = h_ref[...]
    tt = hb.shape[0]

    def gather(st, carry):
        r0 = pl.multiple_of(st * MOE_SUB, MOE_SUB)
        oh = _slot_one_hot(pos, r0, MOE_SUB, tt)
        xs_sc[pl.ds(r0, MOE_SUB), :] += jnp.dot(oh, hb, preferred_element_type=f32)
        g = jnp.sum(oh.astype(f32) * gate, axis=-1, keepdims=True)
        gs_sc[pl.ds(r0, MOE_SUB), :] += jnp.broadcast_to(g, (MOE_SUB, LANES))
        return carry

    lax.fori_loop(c0 // MOE_SUB, (c1 + MOE_SUB - 1) // MOE_SUB, gather, 0)

    @pl.when(j == nj - 1)
    def _():
        rows_per = min(MOE_ROWS, cap)
        for r in range(cap // rows_per):
            rows = slice(r * rows_per, (r + 1) * rows_per)
            xb = xs_sc[rows, :].astype(bf16)
            g = jnp.dot(xb, wg_ref[...], preferred_element_type=f32)
            u = jnp.dot(xb, wu_ref[...], preferred_element_type=f32)
            hid = (g * jax.nn.sigmoid(g) * u).astype(bf16)
            out = jnp.dot(hid, wd_ref[...], preferred_element_type=f32)
            scale = jnp.concatenate([gs_sc[rows, :]] * (out.shape[1] // LANES), axis=1)
            o_ref[rows, :] = (out * scale).astype(bf16)


def moe_experts(hb, smap, gate, cs, w_gate, w_up, w_down, cap, *, tt=MOE_TT):
    bsz, l, d = hb.shape
    e, _, ff = w_gate.shape
    nj = l // tt
    smap4 = smap.reshape(bsz, e, 1, l)
    gate4 = gate.reshape(bsz, e, 1, l)
    tok = pl.BlockSpec((None, None, 1, tt), lambda ei, b, j, cs_ref: (b, ei, 0, j))
    once = pl.Buffered(1)
    grid_spec = pltpu.PrefetchScalarGridSpec(
        num_scalar_prefetch=1,
        grid=(e, bsz, nj),
        in_specs=[pl.BlockSpec((None, tt, d), lambda ei, b, j, cs_ref: (b, j, 0)), tok, tok,
                  pl.BlockSpec((None, d, ff), lambda ei, b, j, cs_ref: (ei, 0, 0), pipeline_mode=once),
                  pl.BlockSpec((None, d, ff), lambda ei, b, j, cs_ref: (ei, 0, 0), pipeline_mode=once),
                  pl.BlockSpec((None, ff, d), lambda ei, b, j, cs_ref: (ei, 0, 0), pipeline_mode=once)],
        out_specs=pl.BlockSpec((None, None, cap, d), lambda ei, b, j, cs_ref: (b, ei, 0, 0)),
        scratch_shapes=[pltpu.VMEM((cap, d), jnp.float32), pltpu.VMEM((cap, LANES), jnp.float32),
                        pltpu.VMEM((d, ff), jnp.bfloat16), pltpu.VMEM((d, ff), jnp.bfloat16),
                        pltpu.VMEM((ff, d), jnp.bfloat16)])
    return pl.pallas_call(
        functools.partial(_expert_kernel, nj=nj, cap=cap),
        grid_spec=grid_spec,
        out_shape=jax.ShapeDtypeStruct((bsz, e, cap, d), jnp.bfloat16),
        compiler_params=pltpu.CompilerParams(dimension_semantics=("parallel", "arbitrary", "arbitrary"),
                                             vmem_limit_bytes=MOE_VMEM_LIMIT_BYTES),
        name="moe_experts",
    )(cs, hb, smap4, gate4, w_gate, w_up, w_down)


def _combine_kernel(cs_ref, x_ref, smap_ref, ow_ref, y_ref, *, nj, tt):
    b = pl.program_id(0)
    e = pl.program_id(2)

    @pl.when(e == 0)
    def _():
        y_ref[...] = x_ref[...]

    contract_first = (((0,), (0,)), ((), ()))
    cap = ow_ref.shape[0]
    win = min(2 * MOE_SUB, cap)
    base = (b * N_EXPERTS + e) * (nj + 1)
    cols = [slice(j * tt, (j + 1) * tt) for j in range(nj)]
    pos = [smap_ref[:, c] for c in cols]
    r0 = [pl.multiple_of(jnp.minimum(cs_ref[base + j] // MOE_SUB * MOE_SUB, cap - win), MOE_SUB) for j in range(nj)]
    oh = [_slot_one_hot(pos[j], r0[j], win, tt) for j in range(nj)]
    add = [lax.dot_general(oh[j], ow_ref[pl.ds(r0[j], win), :], contract_first, preferred_element_type=jnp.float32)
           for j in range(nj)]
    for j in range(nj):
        y_ref[cols[j], :] += add[j]

    for j in range(nj):
        def scatter(st, carry):
            s0 = pl.multiple_of(st * MOE_SUB, MOE_SUB)
            y_ref[cols[j], :] += lax.dot_general(_slot_one_hot(pos[j], s0, MOE_SUB, tt),
                                                 ow_ref[pl.ds(s0, MOE_SUB), :], contract_first,
                                                 preferred_element_type=jnp.float32)
            return carry

        lax.fori_loop((r0[j] + win) // MOE_SUB, (cs_ref[base + j + 1] + MOE_SUB - 1) // MOE_SUB, scatter, 0)


def moe_combine(x, smap, outw, cs, *, tt=MOE_TT):
    bsz, l, d = x.shape
    e, cap = outw.shape[1:3]
    nj = l // tt
    smap4 = smap.reshape(bsz, e, 1, l)
    grid_spec = pltpu.PrefetchScalarGridSpec(
        num_scalar_prefetch=1,
        grid=(bsz, d // MOE_SLAB, e),
        in_specs=[pl.BlockSpec((None, l, MOE_SLAB), lambda b, s, ei, cs_ref: (b, 0, s)),
                  pl.BlockSpec((None, None, 1, l), lambda b, s, ei, cs_ref: (b, ei, 0, 0)),
                  pl.BlockSpec((None, None, cap, MOE_SLAB), lambda b, s, ei, cs_ref: (b, ei, 0, s))],
        out_specs=pl.BlockSpec((None, l, MOE_SLAB), lambda b, s, ei, cs_ref: (b, 0, s)))
    return pl.pallas_call(
        functools.partial(_combine_kernel, nj=nj, tt=tt),
        grid_spec=grid_spec,
        out_shape=jax.ShapeDtypeStruct((bsz, l, d), jnp.float32),
        compiler_params=pltpu.CompilerParams(dimension_semantics=("parallel", "parallel", "arbitrary"),
                                             vmem_limit_bytes=MOE_VMEM_LIMIT_BYTES),
        name="moe_combine",
    )(cs, x, smap4, outw)


def ec_moe_layer(x, gain, w_router, w_gate, w_up, w_down):
    bsz, l, d = x.shape
    cap = EC_CAPACITY_FACTOR * l // N_EXPERTS
    tt = min(MOE_TT, l)
    nj = l // tt
    hb, aff = moe_route(x, gain, w_router, tm=tt)
    smap, gate, cnt = moe_select(aff, cap, tt=tt)
    cs = jnp.concatenate([jnp.zeros((bsz, N_EXPERTS, 1), jnp.float32), jnp.cumsum(cnt[..., :nj], axis=-1)], axis=-1)
    cs = cs.astype(jnp.int32).reshape(-1)
    outw = moe_experts(hb, smap, gate, cs, w_gate, w_up, w_down, cap, tt=tt)
    return moe_combine(x, smap, outw, cs, tt=tt)


def kernel(x, mix_norm, ffn_norm, ev_w_in, ev_w_out, a_lb_logits, a_out_norm, s5_lambda_re, s5_lambda_im, s5_log_step, s5_b_re, s5_b_im, s5_c_re, s5_c_im, s5_d, s5_glu_w, s5_glu_b, od_w_in, od_w_out, c_q_norm, c_k_norm, c_lambda, c_out_norm, rel_bias, d_conv_w, d_a_log, d_dt_bias, d_out_norm, moe_router, moe_w_gate, moe_w_up, moe_w_down):
    bsz, l, d = x.shape
    p = jax.nn.softmax(a_lb_logits.astype(jnp.float32), axis=0)
    cum = jnp.cumsum(p, axis=0)
    lower_bounds = cum - cum[0:1]
    bias5 = rel_bias_tiles(rel_bias, ATT_T)
    for layer in range(DEPTH):
        j = layer // 2
        if layer % 2 == 0:
            proj, u_tb = norm_matmul(x, mix_norm[layer], ev_w_in[j], tail=B_WIDTH)
            o_a = hgrn2_mixer(proj, lower_bounds[j], a_out_norm[j])
            o_b = s5_mixer_tb(u_tb.reshape(l * bsz, B_WIDTH), bsz, s5_lambda_re[j], s5_lambda_im[j], s5_log_step[j],
                              s5_b_re[j], s5_b_im[j], s5_c_re[j], s5_c_im[j], s5_d[j], s5_glu_w[j], s5_glu_b[j])
            x = matmul_residual(o_a, o_b.reshape(l, bsz * B_WIDTH), ev_w_out[j], x, a2_time_major=True)
        else:
            o2 = 3 * C_WIDTH + 3 * D_WIDTH
            o4 = o2 + 4 * D_HEADS
            w = od_w_in[j]
            w_in = jnp.concatenate([w[:, :o2], w[:, o4:], w[:, o2:o4],
                                    jnp.zeros((d, OD_COLS - w.shape[1]), w.dtype)], axis=1)
            proj = norm_matmul(x, mix_norm[layer], w_in)
            o_c = diff_attention(proj, c_q_norm[j], c_k_norm[j], c_lambda[j], c_out_norm[j], bias5, layer)
            o_d = gated_deltanet(proj, d_conv_w[j], d_a_log[j], d_dt_bias[j], d_out_norm[j])
            x = matmul_residual(o_c, o_d, od_w_out[j], x)
        x = ec_moe_layer(x, ffn_norm[layer], moe_router[layer], moe_w_gate[layer], moe_w_up[layer],
                         moe_w_down[layer])
    return x
```

```python
import functools
import math

import jax
import jax.numpy as jnp
from jax import lax
from jax.experimental import pallas as pl
from jax.experimental.pallas import tpu as pltpu

D_MODEL = 1024
DEPTH = 4
MIX_WIDTH = D_MODEL
A_WIDTH = MIX_WIDTH // 2
A_HEAD_DIM = 128
A_HEADS = A_WIDTH // A_HEAD_DIM
B_WIDTH = MIX_WIDTH - A_WIDTH
S5_GROUP = 16
S5_GROUPS = B_WIDTH // S5_GROUP
S5_STATE = 64
C_WIDTH = MIX_WIDTH // 2
C_HEAD_DIM = 64
C_HEADS = C_WIDTH // (2 * C_HEAD_DIM)
C_V_DIM = 2 * C_HEAD_DIM
D_WIDTH = MIX_WIDTH - C_WIDTH
D_HEAD_DIM = 128
D_HEADS = D_WIDTH // D_HEAD_DIM
CONV_WIDTH = 5
N_EXPERTS = 16
EXPERT_FF = 2 * D_MODEL
EC_CAPACITY_FACTOR = 2
REL_BUCKETS = 32
REL_MAX_DIST = 128
CHUNK = 64
Q_BLOCK = 128
EPS = 1e-6

VMEM_LIMIT_BYTES = 48 * 1024 * 1024


PROJ_TM = 512
PROJ_COLS = 512


def _norm_matmul_kernel(x_ref, g_ref, w_ref, o_ref, *tail_ref, main):
    x = x_ref[...]
    y = (x * lax.rsqrt(jnp.mean(x * x, axis=-1, keepdims=True) + EPS) * g_ref[...]).astype(jnp.bfloat16)
    for c0 in range(0, main, PROJ_COLS):
        c1 = min(c0 + PROJ_COLS, main)
        o_ref[:, c0:c1] = jnp.dot(y, w_ref[:, c0:c1], preferred_element_type=jnp.float32)
    if tail_ref:
        tail_ref[0][...] = jnp.dot(y, w_ref[:, main:], preferred_element_type=jnp.float32)


def norm_matmul(x, gain, w, *, tail=0, tm=PROJ_TM):
    bsz, l, k = x.shape
    m = w.shape[1]
    main = m - tail
    tm = min(tm, l)
    out_shape = [jax.ShapeDtypeStruct((bsz, l, main), jnp.float32)]
    out_specs = [pl.BlockSpec((None, tm, main), lambda b, i: (b, i, 0))]
    if tail:
        out_shape.append(jax.ShapeDtypeStruct((l, bsz * tail), jnp.float32))
        out_specs.append(pl.BlockSpec((tm, tail), lambda b, i: (i, b)))
    outs = pl.pallas_call(
        functools.partial(_norm_matmul_kernel, main=main),
        grid=(bsz, l // tm),
        in_specs=[pl.BlockSpec((None, tm, k), lambda b, i: (b, i, 0)),
                  pl.BlockSpec((1, k), lambda b, i: (0, 0)),
                  pl.BlockSpec((k, m), lambda b, i: (0, 0), pipeline_mode=pl.Buffered(1))],
        out_specs=out_specs,
        out_shape=out_shape,
        compiler_params=pltpu.CompilerParams(dimension_semantics=("parallel", "parallel"),
                                             vmem_limit_bytes=VMEM_LIMIT_BYTES),
        name="norm_matmul",
    )(x, gain.reshape(1, k).astype(jnp.float32), w.astype(jnp.bfloat16))
    return outs if tail else outs[0]


def _matmul_res_kernel(a1_ref, a2_ref, w1_ref, w2_ref, r_ref, o_ref):
    bf16 = jnp.bfloat16
    o_ref[...] = (r_ref[...] + jnp.dot(a1_ref[...].astype(bf16), w1_ref[...], preferred_element_type=jnp.float32)
                  + jnp.dot(a2_ref[...].astype(bf16), w2_ref[...], preferred_element_type=jnp.float32))


def matmul_residual(a1, a2, w, res, *, a2_time_major=False, tm=PROJ_TM):
    bsz, l, k1 = a1.shape
    m = w.shape[1]
    k2 = w.shape[0] - k1
    tm = min(tm, l)
    wb = w.astype(jnp.bfloat16)
    a2_spec = (pl.BlockSpec((tm, k2), lambda b, i: (i, b)) if a2_time_major
               else pl.BlockSpec((None, tm, k2), lambda b, i: (b, i, 0)))
    row = pl.BlockSpec((None, tm, m), lambda b, i: (b, i, 0))
    return pl.pallas_call(
        _matmul_res_kernel,
        grid=(bsz, l // tm),
        in_specs=[pl.BlockSpec((None, tm, k1), lambda b, i: (b, i, 0)), a2_spec,
                  pl.BlockSpec((k1, m), lambda b, i: (0, 0)),
                  pl.BlockSpec((k2, m), lambda b, i: (0, 0)), row],
        out_specs=row,
        out_shape=jax.ShapeDtypeStruct((bsz, l, m), jnp.float32),
        compiler_params=pltpu.CompilerParams(dimension_semantics=("parallel", "parallel"),
                                             vmem_limit_bytes=VMEM_LIMIT_BYTES),
        name="matmul_residual",
    )(a1, a2, wb[:k1], wb[k1:], res)


HG_LEVELS = tuple(CHUNK >> (i + 1) for i in range(CHUNK.bit_length() - 1))
HG_TOT_ROWS = 8
HG_TT = 512


def hgrn2_constants():
    import numpy as np
    c = CHUNK
    r = np.arange(c)[:, None]
    u = np.arange(c)[None, :]
    stacks, masks = [], []
    for direction in range(2):
        fwd = direction == 0
        lvl_masks = []
        for m in HG_LEVELS:
            blk = r // (2 * m)
            later = (r % (2 * m)) >= m
            lvl_masks.append((blk == blk.T) & (later & ~later.T if fwd else ~later & later.T))
        stacks.append(np.concatenate([(u <= r) if fwd else (u >= r), np.ones((HG_TOT_ROWS, c), bool)], axis=0))
        masks.append(np.stack(lvl_masks))
    return (jnp.asarray(np.stack(stacks), jnp.bfloat16), jnp.asarray(np.stack(masks), jnp.float32))


def _hgrn2_kernel(q_ref, f_ref, v_ref, loglb_ref, log1mlb_ref, onemlb_ref, ast_ref, mask_ref, o_ref,
                  st_sc, qd_sc, dec_sc, upd_sc, sin_sc, *, nc):
    direction = pl.program_id(2)

    @pl.when(pl.program_id(3) == 0)
    def _():
        st_sc[...] = jnp.zeros_like(st_sc)

    bf16 = jnp.bfloat16
    f32 = jnp.float32
    c = CHUNK
    hd = A_HEAD_DIM
    dirf = direction.astype(f32)
    loglb = loglb_ref[...]
    log1mlb = log1mlb_ref[...]
    onemlb = onemlb_ref[...]
    ast = ast_ref[...]
    contract_last = (((1,), (1,)), ((), ()))
    contract_first = (((0,), (0,)), ((), ()))

    ns = range(nc)
    rows = [slice(n * c, (n + 1) * c) for n in ns]
    z = [f_ref[r, :] for r in rows]
    v = [v_ref[r, :] for r in rows]
    qr = [q_ref[r, :] for r in rows]
    q = [x * jax.nn.sigmoid(x) for x in qr]
    e = [jnp.exp(-jnp.abs(x)) for x in z]
    cc = [log1mlb + jnp.minimum(z[n], 0.0) - jnp.log1p(e[n]) for n in ns]
    lf = [jnp.maximum(loglb, x) + jnp.log1p(jnp.exp(-jnp.abs(loglb - x))) for x in cc]
    k = [onemlb * jnp.where(z[n] >= 0, e[n], 1.0) / (1.0 + e[n]) for n in ns]
    hi = [x.astype(bf16) for x in lf]
    lo = [(lf[n] - hi[n].astype(f32)).astype(bf16) for n in ns]
    d = [jnp.dot(ast, hi[n], preferred_element_type=f32) + jnp.dot(ast, lo[n], preferred_element_type=f32)
         for n in ns]
    cum = [x[0:c] for x in d]
    tot = [x[c:c + HG_TOT_ROWS] for x in d]
    ref = [cum[n] - dirf * lf[n] for n in ns]
    attn = [jnp.zeros((c, c), f32) for _ in ns]
    for li, m in enumerate(HG_LEVELS):
        nb = c // (2 * m)
        split = [jnp.broadcast_to(x.reshape(nb, 2 * m, hd)[:, m - 1:m, :], (nb, 2 * m, hd)).reshape(c, hd)
                 for x in ref]
        x = [jnp.exp(-jnp.abs(cum[n] - split[n])) for n in ns]
        s = [lax.dot_general((q[n] * x[n]).astype(bf16), (k[n] * x[n]).astype(bf16), contract_last,
                             preferred_element_type=f32) for n in ns]
        attn = [attn[n] + mask_ref[li] * s[n] for n in ns]
    vb = [x.astype(bf16) for x in v]
    intra = [jnp.dot(attn[n].astype(bf16), vb[n], preferred_element_type=f32) for n in ns]
    upd = [lax.dot_general(vb[n], (k[n] * jnp.exp(tot[n][0:1] - cum[n])).astype(bf16), contract_first,
                           preferred_element_type=f32) for n in ns]
    for n in ns:
        o_ref[rows[n], :] = intra[n] + jnp.sum(q[n] * k[n], axis=-1, keepdims=True) * v[n]
        qd_sc[n] = (q[n] * jnp.exp(cum[n])).astype(bf16)
        dec_sc[n] = jnp.exp(tot[n])
        upd_sc[n] = upd[n]

    def body(ci, st):
        ce = ci + direction * (nc - 1 - 2 * ci)
        sin_sc[ce] = st.astype(bf16)
        return st * dec_sc[ce][0:1] + upd_sc[ce]

    st_sc[...] = lax.fori_loop(0, nc, body, st_sc[...])

    for n in range(nc):
        rows = slice(n * c, (n + 1) * c)
        o_ref[rows, :] += lax.dot_general(qd_sc[n], sin_sc[n], contract_last, preferred_element_type=f32)


def hgrn2_scan(proj3, lb):
    bsz, l, _ = proj3.shape
    hd = A_HEAD_DIM
    tt = min(HG_TT, l)
    nt = l // tt
    assert l % tt == 0 and tt % CHUNK == 0
    ast, masks = hgrn2_constants()
    lb = lb.astype(jnp.float32)
    vecs = [jnp.log(lb).reshape(2, 1, A_WIDTH), jnp.log1p(-lb).reshape(2, 1, A_WIDTH), (1.0 - lb).reshape(2, 1, A_WIDTH)]
    tidx = lambda d, i: i + d * (nt - 1 - 2 * i)
    vec = pl.BlockSpec((None, 1, hd), lambda b, h, d, i: (d, 0, h))
    return pl.pallas_call(
        functools.partial(_hgrn2_kernel, nc=tt // CHUNK),
        grid=(bsz, A_HEADS, 2, nt),
        in_specs=[pl.BlockSpec((None, tt, hd), lambda b, h, d, i: (b, tidx(d, i), h)),
                  pl.BlockSpec((None, tt, hd), lambda b, h, d, i: (b, tidx(d, i), (1 + d) * A_HEADS + h)),
                  pl.BlockSpec((None, tt, hd), lambda b, h, d, i: (b, tidx(d, i), 3 * A_HEADS + h)),
                  vec, vec, vec,
                  pl.BlockSpec((None,) + ast.shape[1:], lambda b, h, d, i: (d, 0, 0)),
                  pl.BlockSpec((None,) + masks.shape[1:], lambda b, h, d, i: (d, 0, 0, 0))],
        out_specs=pl.BlockSpec((None, None, tt, hd), lambda b, h, d, i: (d, b, tidx(d, i), h)),
        out_shape=jax.ShapeDtypeStruct((2, bsz, l, A_WIDTH), jnp.float32),
        scratch_shapes=[pltpu.VMEM((hd, hd), jnp.float32),
                        pltpu.VMEM((tt // CHUNK, CHUNK, hd), jnp.bfloat16),
                        pltpu.VMEM((tt // CHUNK, HG_TOT_ROWS, hd), jnp.float32),
                        pltpu.VMEM((tt // CHUNK, hd, hd), jnp.float32),
                        pltpu.VMEM((tt // CHUNK, hd, hd), jnp.bfloat16)],
        compiler_params=pltpu.CompilerParams(dimension_semantics=("parallel", "parallel", "parallel", "arbitrary"),
                                             vmem_limit_bytes=VMEM_LIMIT_BYTES),
        name="hgrn2_scan",
    )(proj3, proj3, proj3, *vecs, ast, masks)


def _hgrn2_final_kernel(of_ref, ob_ref, g_ref, gain_ref, o_ref):
    o = of_ref[...] + ob_ref[...]
    y = o * lax.rsqrt(jnp.mean(o * o, axis=-1, keepdims=True) + EPS) * gain_ref[...]
    g = g_ref[...]
    o_ref[...] = y * (g * jax.nn.sigmoid(g))


def bidir_finalize(o2, proj3, gate_block, out_gain, *, name, tm=1024):
    _, bsz, l, w = o2.shape
    hd = 128
    tm = min(tm, l)
    return pl.pallas_call(
        _hgrn2_final_kernel,
        grid=(bsz, l // tm, w // hd),
        in_specs=[pl.BlockSpec((None, None, tm, hd), lambda b, i, h: (0, b, i, h)),
                  pl.BlockSpec((None, None, tm, hd), lambda b, i, h: (1, b, i, h)),
                  pl.BlockSpec((None, tm, hd), lambda b, i, h: (b, i, gate_block + h)),
                  pl.BlockSpec((1, hd), lambda b, i, h: (0, 0))],
        out_specs=pl.BlockSpec((None, tm, hd), lambda b, i, h: (b, i, h)),
        out_shape=jax.ShapeDtypeStruct((bsz, l, w), jnp.float32),
        compiler_params=pltpu.CompilerParams(dimension_semantics=("parallel", "parallel", "parallel"),
                                             vmem_limit_bytes=VMEM_LIMIT_BYTES),
        name=name,
    )(o2, o2, proj3, out_gain.reshape(1, hd).astype(jnp.float32))


def hgrn2_mixer(proj3, lb, out_gain):
    return bidir_finalize(hgrn2_scan(proj3, lb), proj3, 4 * A_HEADS, out_gain, name="hgrn2_finalize")


S5_NS = S5_GROUPS * S5_STATE
S5_TT = 64
SUBLANES = 8


def _s5_scan_kernel(u_ref, win_ref, ar_ref, ai_ref, wout_ref, y_ref, bu_sc, xs_sc, st_sc, *, bsz, tt, reverse):
    @pl.when(pl.program_id(0) == 0)
    def _():
        st_sc[...] = jnp.zeros_like(st_sc)

    ub = u_ref[...].astype(jnp.bfloat16)
    halves = 2
    uw = B_WIDTH // halves
    sw = S5_NS // halves
    for hf in range(halves):
        for part in range(2):
            sc = slice(part * S5_NS + hf * sw, part * S5_NS + (hf + 1) * sw)
            bu_sc[:, sc] = jnp.dot(ub[:, hf * uw:(hf + 1) * uw], win_ref[hf * uw:(hf + 1) * uw, sc],
                                   preferred_element_type=jnp.float32)
    ar = jnp.broadcast_to(ar_ref[...], (bsz, S5_NS))
    ai = jnp.broadcast_to(ai_ref[...], (bsz, S5_NS))
    per = SUBLANES // bsz
    ngroups = tt // per

    def body(s, carry):
        xr, xi = carry
        p = (ngroups - 1 - s) if reverse else s
        base = pl.multiple_of(p * SUBLANES, SUBLANES)
        blk = bu_sc[pl.ds(base, SUBLANES), :]
        outs_r = [None] * per
        outs_i = [None] * per
        for ph in (range(per - 1, -1, -1) if reverse else range(per)):
            br = blk[ph * bsz:(ph + 1) * bsz, :S5_NS]
            bi = blk[ph * bsz:(ph + 1) * bsz, S5_NS:]
            xr, xi = ar * xr - ai * xi + br, ar * xi + ai * xr + bi
            outs_r[ph] = xr
            outs_i[ph] = xi
        xs_sc[pl.ds(base, SUBLANES), :S5_NS] = jnp.concatenate(outs_r, axis=0)
        xs_sc[pl.ds(base, SUBLANES), S5_NS:] = jnp.concatenate(outs_i, axis=0)
        return xr, xi

    xr, xi = lax.fori_loop(0, ngroups, body, (st_sc[0], st_sc[1]))
    st_sc[0] = xr
    st_sc[1] = xi
    for hf in range(halves):
        yc = slice(hf * uw, (hf + 1) * uw)
        acc = None
        for part in range(2):
            sc = slice(part * S5_NS + hf * sw, part * S5_NS + (hf + 1) * sw)
            term = jnp.dot(xs_sc[:, sc].astype(jnp.bfloat16), wout_ref[sc, yc], preferred_element_type=jnp.float32)
            acc = term if acc is None else acc + term
        y_ref[:, yc] = acc


def s5_scan(u_tb, win, ar, ai, wout, *, bsz, reverse):
    n = u_tb.shape[0]
    rows = S5_TT * bsz
    nt = n // rows
    assert n % rows == 0 and SUBLANES % bsz == 0
    idx = (lambda i: (nt - 1 - i, 0)) if reverse else (lambda i: (i, 0))
    const = lambda i: (0, 0)
    return pl.pallas_call(
        functools.partial(_s5_scan_kernel, bsz=bsz, tt=S5_TT, reverse=reverse),
        grid=(nt,),
        in_specs=[pl.BlockSpec((rows, B_WIDTH), idx),
                  pl.BlockSpec((B_WIDTH, 2 * S5_NS), const),
                  pl.BlockSpec((1, S5_NS), const),
                  pl.BlockSpec((1, S5_NS), const),
                  pl.BlockSpec((2 * S5_NS, B_WIDTH), const)],
        out_specs=pl.BlockSpec((rows, B_WIDTH), idx),
        out_shape=jax.ShapeDtypeStruct((n, B_WIDTH), jnp.float32),
        scratch_shapes=[pltpu.VMEM((rows, 2 * S5_NS), jnp.float32),
                        pltpu.VMEM((rows, 2 * S5_NS), jnp.float32),
                        pltpu.VMEM((2, bsz, S5_NS), jnp.float32)],
        compiler_params=pltpu.CompilerParams(dimension_semantics=("arbitrary",),
                                             vmem_limit_bytes=VMEM_LIMIT_BYTES),
        name="s5_scan_bwd" if reverse else "s5_scan_fwd",
    )(u_tb, win, ar, ai, wout)


def _s5_final_kernel(u_ref, yf_ref, yb_ref, d_ref, w_ref, b_ref, o_ref):
    y = d_ref[...] * u_ref[...] + yf_ref[...] + yb_ref[...]
    y = jax.nn.gelu(y)
    z = jnp.dot(y.astype(jnp.bfloat16), w_ref[...], preferred_element_type=jnp.float32) + b_ref[...]
    o_ref[...] = y * jax.nn.sigmoid(z)


def s5_finalize(u, yf, yb, d_skip, glu_w, glu_b, *, tm=512):
    n, w = u.shape
    row = pl.BlockSpec((tm, w), lambda i: (i, 0))
    vec = pl.BlockSpec((1, w), lambda i: (0, 0))
    return pl.pallas_call(
        _s5_final_kernel,
        grid=(n // tm,),
        in_specs=[row, row, row, vec, pl.BlockSpec((w, w), lambda i: (0, 0)), vec],
        out_specs=row,
        out_shape=jax.ShapeDtypeStruct((n, w), jnp.float32),
        compiler_params=pltpu.CompilerParams(dimension_semantics=("parallel",),
                                             vmem_limit_bytes=VMEM_LIMIT_BYTES),
        name="s5_finalize",
    )(u, yf, yb, d_skip.reshape(1, w).astype(jnp.float32), glu_w.astype(jnp.bfloat16),
      glu_b.reshape(1, w).astype(jnp.float32))


def s5_direction_params(lam_re, lam_im, log_step, b_re, b_im, c_re, c_im):
    step = jnp.exp(log_step)[:, None]
    mag = jnp.exp(lam_re * step)
    abar_re = mag * jnp.cos(lam_im * step)
    abar_im = mag * jnp.sin(lam_im * step)
    den = lam_re * lam_re + lam_im * lam_im
    fr = ((abar_re - 1.0) * lam_re + abar_im * lam_im) / den
    fi = (abar_im * lam_re - (abar_re - 1.0) * lam_im) / den
    bb_re = fr[..., None] * b_re - fi[..., None] * b_im
    bb_im = fr[..., None] * b_im + fi[..., None] * b_re
    eye = jnp.eye(S5_GROUPS, dtype=jnp.float32)
    win = jnp.concatenate([jnp.einsum('gnp,gh->gphn', bb, eye).reshape(B_WIDTH, S5_NS) for bb in (bb_re, bb_im)],
                          axis=1)
    wout = jnp.concatenate([jnp.einsum('gpn,gh->hngp', c, eye).reshape(S5_NS, B_WIDTH) for c in (c_re, -c_im)],
                           axis=0)
    return (win.astype(jnp.bfloat16), abar_re.reshape(1, S5_NS), abar_im.reshape(1, S5_NS),
            wout.astype(jnp.bfloat16))


def s5_mixer_tb(u_tb, bsz, lam_re, lam_im, log_step, b_re, b_im, c_re, c_im, d_skip, glu_w, glu_b):
    f32 = jnp.float32
    ys = []
    for direction in range(2):
        prm = s5_direction_params(lam_re[direction].astype(f32), lam_im[direction].astype(f32),
                                  log_step[direction].astype(f32), b_re[direction].astype(f32),
                                  b_im[direction].astype(f32), c_re[direction].astype(f32),
                                  c_im[direction].astype(f32))
        ys.append(s5_scan(u_tb, *prm, bsz=bsz, reverse=(direction == 1)))
    return s5_finalize(u_tb, ys[0], ys[1], d_skip, glu_w, glu_b)


def t5_bucket(rel):
    half = REL_BUCKETS // 2
    max_exact = half // 2
    base = jnp.where(rel > 0, half, 0)
    n = jnp.abs(rel)
    nf = jnp.maximum(n, 1).astype(jnp.float32)
    large = max_exact + (jnp.log(nf / max_exact) / math.log(REL_MAX_DIST / max_exact)
                         * (half - max_exact)).astype(jnp.int32)
    large = jnp.minimum(large, half - 1)
    return base + jnp.where(n < max_exact, n, large)


ATT_T = 512
LOG2E = math.log2(math.e)


def rel_bias_tiles(rel_bias, t):
    assert t >= REL_MAX_DIST
    table = rel_bias.astype(jnp.float32) * LOG2E
    tiles = []
    for d in (-1, 0, 1):
        c = table[t5_bucket(d * t + jnp.arange(-(t - 1), t))]
        w = jnp.concatenate([c, c[:1]], axis=0)
        m = jnp.tile(w, (t, 1))[:t * (2 * t - 1)].reshape(t, 2 * t - 1, -1)
        tiles.append(m[:, t - 1:2 * t - 1])
    far_neg = jnp.broadcast_to(table[t5_bucket(jnp.array(-2 * t))], tiles[0].shape)
    far_pos = jnp.broadcast_to(table[t5_bucket(jnp.array(2 * t))], tiles[0].shape)
    out = jnp.stack([far_neg] + tiles + [far_pos], axis=0)
    return jnp.transpose(out, (3, 0, 1, 2))


def _attn_prep_kernel(q_ref, k_ref, v_ref, qg_ref, kg_ref, q2_ref, kt_ref, vb_ref):
    lane = lax.broadcasted_iota(jnp.int32, q_ref.shape, 1)
    lo = lane < C_HEAD_DIM

    def halfnorm(x, g):
        sq = x * x
        s_lo = jnp.sum(jnp.where(lo, sq, 0.0), axis=-1, keepdims=True)
        s_hi = jnp.sum(jnp.where(lo, 0.0, sq), axis=-1, keepdims=True)
        ms = jnp.where(lo, s_lo, s_hi) * (1.0 / C_HEAD_DIM)
        return x * lax.rsqrt(ms + EPS) * g

    qn = halfnorm(q_ref[...], qg_ref[...]) * (C_HEAD_DIM ** -0.5 * LOG2E)
    kn = halfnorm(k_ref[...], kg_ref[...])
    q2_ref[0] = jnp.where(lo, qn, 0.0).astype(jnp.bfloat16)
    q2_ref[1] = jnp.where(lo, 0.0, qn).astype(jnp.bfloat16)
    kt_ref[...] = kn.T.astype(jnp.bfloat16)
    vb_ref[...] = v_ref[...].astype(jnp.bfloat16)


def attn_prep(proj3, q_gain, k_gain, *, tl=512):
    bsz, l, _ = proj3.shape
    hw = 2 * C_HEAD_DIM
    gq = jnp.tile(q_gain.astype(jnp.float32), 2).reshape(1, hw)
    gk = jnp.tile(k_gain.astype(jnp.float32), 2).reshape(1, hw)
    vec = pl.BlockSpec((1, hw), lambda b, h, i: (0, 0))
    return pl.pallas_call(
        _attn_prep_kernel,
        grid=(bsz, C_HEADS, l // tl),
        in_specs=[pl.BlockSpec((None, tl, hw), lambda b, h, i: (b, i, h)),
                  pl.BlockSpec((None, tl, hw), lambda b, h, i: (b, i, C_HEADS + h)),
                  pl.BlockSpec((None, tl, hw), lambda b, h, i: (b, i, 2 * C_HEADS + h)),
                  vec, vec],
        out_specs=[pl.BlockSpec((None, None, 2, tl, hw), lambda b, h, i: (b, h, 0, i, 0)),
                   pl.BlockSpec((None, None, hw, tl), lambda b, h, i: (b, h, 0, i)),
                   pl.BlockSpec((None, None, tl, hw), lambda b, h, i: (b, h, i, 0))],
        out_shape=[jax.ShapeDtypeStruct((bsz, C_HEADS, 2, l, hw), jnp.bfloat16),
                   jax.ShapeDtypeStruct((bsz, C_HEADS, hw, l), jnp.bfloat16),
                   jax.ShapeDtypeStruct((bsz, C_HEADS, l, hw), jnp.bfloat16)],
        compiler_params=pltpu.CompilerParams(dimension_semantics=("parallel", "parallel", "parallel"),
                                             vmem_limit_bytes=VMEM_LIMIT_BYTES),
        name="attn_prep",
    )(proj3, proj3, proj3, gq, gk)


ATT_ROWS = 64


def _attn_kernel(lam_ref, q2_ref, kt_ref, v_ref, bias_ref, g_ref, o_ref, m_sc, l_sc, acc_sc, s_sc, p_sc, a_sc,
                 *, t, nk, out_scale):
    qi = pl.program_id(2)
    q2 = q2_ref[...].reshape(2 * t, 2 * C_HEAD_DIM)
    m_sc[...] = jnp.full(m_sc.shape, -jnp.inf, jnp.float32)
    l_sc[...] = jnp.zeros_like(l_sc)
    acc_sc[...] = jnp.zeros_like(acc_sc)
    r = ATT_ROWS
    hw = 2 * C_HEAD_DIM

    def scores(ki, buf):
        off = pl.multiple_of(ki * t, t)
        s_sc[buf] = jnp.dot(q2, kt_ref[:, pl.ds(off, t)], preferred_element_type=jnp.float32)

    def absorb(ki, buf):
        off = pl.multiple_of(ki * t, t)
        bidx = jnp.clip(ki - qi, -2, 2) + 2
        for g in range(2 * t // r):
            rows = slice(g * r, (g + 1) * r)
            brow = (g * r) % t
            s = s_sc[buf, rows, :] + bias_ref[bidx, brow:brow + r, :]
            m_prev = m_sc[rows, :]
            m_new = jnp.maximum(m_prev, jnp.max(s, axis=-1, keepdims=True))
            alpha = jnp.exp2(m_prev - m_new)
            ps = [jnp.exp2(s[:, j * hw:(j + 1) * hw] - m_new) for j in range(t // hw)]
            l_sc[rows, :] = alpha * l_sc[rows, :] + jnp.sum(sum(ps), axis=-1, keepdims=True)
            m_sc[rows, :] = m_new
            a_sc[rows, :] = alpha
            for j in range(t // hw):
                p_sc[rows, j * hw:(j + 1) * hw] = ps[j].astype(jnp.bfloat16)
        acc_sc[...] = a_sc[...] * acc_sc[...] + jnp.dot(p_sc[...], v_ref[pl.ds(off, t), :],
                                                        preferred_element_type=jnp.float32)

    def body(ki, carry):
        scores(ki, 0)
        absorb(ki, 0)
        return carry

    lax.fori_loop(0, nk, body, 0)
    a = acc_sc[...] / l_sc[...]
    o = a[:t] - lam_ref[0] * a[t:]
    y = o * lax.rsqrt(jnp.mean(o * o, axis=-1, keepdims=True) + EPS)
    o_ref[...] = y * g_ref[...] * out_scale


def diff_attention(proj3, q_gain, k_gain, lam, out_gain, bias5, layer_idx):
    f32 = jnp.float32
    bsz, l, _ = proj3.shape
    t = ATT_T
    hw = 2 * C_HEAD_DIM
    lam_init = 0.8 - 0.6 * math.exp(-0.3 * layer_idx)
    lam_f = lam.astype(f32)
    lam_full = jnp.exp(jnp.sum(lam_f[0] * lam_f[1])) - jnp.exp(jnp.sum(lam_f[2] * lam_f[3])) + lam_init
    q2, kt, vb = attn_prep(proj3, q_gain, k_gain)
    return pl.pallas_call(
        functools.partial(_attn_kernel, t=t, nk=l // t, out_scale=1.0 - lam_init),
        grid=(bsz, C_HEADS, l // t),
        in_specs=[pl.BlockSpec(memory_space=pltpu.SMEM),
                  pl.BlockSpec((None, None, 2, t, hw), lambda b, h, i: (b, h, 0, i, 0)),
                  pl.BlockSpec((None, None, hw, l), lambda b, h, i: (b, h, 0, 0)),
                  pl.BlockSpec((None, None, l, hw), lambda b, h, i: (b, h, 0, 0)),
                  pl.BlockSpec((None, 5, t, t), lambda b, h, i: (h, 0, 0, 0)),
                  pl.BlockSpec((1, hw), lambda b, h, i: (0, 0))],
        out_specs=pl.BlockSpec((None, t, hw), lambda b, h, i: (b, i, h)),
        out_shape=jax.ShapeDtypeStruct((bsz, l, C_WIDTH), f32),
        scratch_shapes=[pltpu.VMEM((2 * t, hw), f32), pltpu.VMEM((2 * t, hw), f32), pltpu.VMEM((2 * t, hw), f32),
                        pltpu.VMEM((1, 2 * t, t), f32), pltpu.VMEM((2 * t, t), jnp.bfloat16), pltpu.VMEM((2 * t, hw), f32)],
        compiler_params=pltpu.CompilerParams(dimension_semantics=("parallel", "parallel", "arbitrary"),
                                             vmem_limit_bytes=VMEM_LIMIT_BYTES),
        name="diff_attention",
    )(lam_full.reshape(1), q2, kt, vb, bias5, out_gain.reshape(1, hw).astype(f32))


GDN_TT = 512
GDN_HEADS_PER_STEP = 4
LANES = 128
OD_QKV_BLOCK = 3 * C_WIDTH // LANES
OD_GATE_BLOCK = OD_QKV_BLOCK + 3 * D_WIDTH // LANES
OD_AB_BLOCK = OD_GATE_BLOCK + D_WIDTH // LANES
OD_COLS = 3840


def _gdn_prep_kernel(prev_ref, cur_ref, next_ref, w_ref, o_ref, *, tl, nl):
    i = pl.program_id(1)
    part = pl.program_id(2)
    prev = jnp.where(i > 0, prev_ref[...], 0.0)
    nxt = jnp.where(i < nl - 1, next_ref[...], 0.0)
    ext = jnp.concatenate([prev, cur_ref[...], nxt], axis=0)
    halo = prev.shape[0]
    acc = None
    for j in range(CONV_WIDTH):
        start = halo - CONV_WIDTH // 2 + j
        term = w_ref[j:j + 1, :] * ext[start:start + tl, :]
        acc = term if acc is None else acc + term
    y = acc * jax.nn.sigmoid(acc)
    scale = jnp.where(part == 0, D_HEAD_DIM ** -0.5, 1.0)
    heads = []
    for h in range(D_HEADS):
        yh = y[:, h * LANES:(h + 1) * LANES]
        heads.append(yh * (lax.rsqrt(jnp.sum(yh * yh, axis=-1, keepdims=True) + EPS) * scale))
    o_ref[...] = jnp.where(part < 2, jnp.concatenate(heads, axis=1), y)


def gdn_prep(proj3, conv_w, *, tl=512):
    bsz, l, _ = proj3.shape
    halo = SUBLANES
    nl = l // tl
    blk0 = OD_QKV_BLOCK * LANES // D_WIDTH
    return pl.pallas_call(
        functools.partial(_gdn_prep_kernel, tl=tl, nl=nl),
        grid=(bsz, nl, 3),
        in_specs=[pl.BlockSpec((None, halo, D_WIDTH), lambda b, i, p: (b, jnp.maximum(i * (tl // halo) - 1, 0), blk0 + p)),
                  pl.BlockSpec((None, tl, D_WIDTH), lambda b, i, p: (b, i, blk0 + p)),
                  pl.BlockSpec((None, halo, D_WIDTH),
                               lambda b, i, p: (b, jnp.minimum((i + 1) * (tl // halo), l // halo - 1), blk0 + p)),
                  pl.BlockSpec((CONV_WIDTH, D_WIDTH), lambda b, i, p: (0, p))],
        out_specs=pl.BlockSpec((None, None, tl, D_WIDTH), lambda b, i, p: (p, b, i, 0)),
        out_shape=jax.ShapeDtypeStruct((3, bsz, l, D_WIDTH), jnp.float32),
        compiler_params=pltpu.CompilerParams(dimension_semantics=("parallel", "parallel", "parallel"),
                                             vmem_limit_bytes=VMEM_LIMIT_BYTES),
        name="gdn_prep",
    )(proj3, proj3, proj3, conv_w.astype(jnp.float32))


def _gdn_gates_kernel(x_ref, nega_ref, dtb_ref, o_ref):
    x = x_ref[...]
    z = x + dtb_ref[...]
    g = nega_ref[...] * (jnp.maximum(z, 0.0) + jnp.log1p(jnp.exp(-jnp.abs(z))))
    lane = lax.broadcasted_iota(jnp.int32, x.shape, 1)
    y = jnp.where(lane < 2 * D_HEADS, g, jax.nn.sigmoid(x))
    o_ref[...] = y.T[0:4 * D_HEADS, :]


def gdn_gates(proj3, a_log, dt_bias, *, tl=512):
    bsz, l, _ = proj3.shape
    pad = LANES - 2 * D_HEADS
    nega = jnp.pad(-jnp.exp(a_log.astype(jnp.float32)).reshape(1, -1), ((0, 0), (0, pad)))
    dtb = jnp.pad(dt_bias.astype(jnp.float32).reshape(1, -1), ((0, 0), (0, pad)))
    vec = pl.BlockSpec((1, LANES), lambda b, i: (0, 0))
    return pl.pallas_call(
        _gdn_gates_kernel,
        grid=(bsz, l // tl),
        in_specs=[pl.BlockSpec((None, tl, LANES), lambda b, i: (b, i, OD_AB_BLOCK)), vec, vec],
        out_specs=pl.BlockSpec((None, 4 * D_HEADS, tl), lambda b, i: (b, 0, i)),
        out_shape=jax.ShapeDtypeStruct((bsz, 4 * D_HEADS, l), jnp.float32),
        compiler_params=pltpu.CompilerParams(dimension_semantics=("parallel", "parallel"),
                                             vmem_limit_bytes=VMEM_LIMIT_BYTES),
        name="gdn_gates",
    )(proj3, nega, dtb)


def gdn_constants():
    import numpy as np
    c = CHUNK
    r = np.arange(c)[:, None]
    u = np.arange(c)[None, :]
    cum, incl, strict = [], [], []
    for direction in range(2):
        fwd = direction == 0
        cum.append(np.concatenate([(r <= u) if fwd else (r >= u), np.ones((c, c), bool)], axis=1))
        incl.append((u <= r) if fwd else (u >= r))
        strict.append((u < r) if fwd else (u > r))
    same = lambda b: (r // b) == (u // b)
    merges = [same(2 * b) & ~same(b) for b in (8, 16, 32)]
    f32 = jnp.float32
    return (jnp.asarray(np.stack(cum), jnp.bfloat16), jnp.asarray(np.stack(incl), f32),
            jnp.asarray(np.stack(strict), f32), jnp.asarray(same(8), f32), jnp.asarray(np.stack(merges), f32))


def _gdn_kernel(q_ref, k_ref, v_ref, g_ref, b_ref, cum_ref, incl_ref, strict_ref, d8_ref, mrg_ref, o_ref,
                s_sc, qd_sc, dec_sc, w_sc, u_sc, sin_sc, *, nc):
    direction = pl.program_id(2)

    @pl.when(pl.program_id(3) == 0)
    def _():
        s_sc[...] = jnp.zeros_like(s_sc)

    bf16 = jnp.bfloat16
    f32 = jnp.float32
    c = CHUNK
    hd = D_HEAD_DIM
    contract_last = (((1,), (1,)), ((), ()))
    contract_first = (((0,), (0,)), ((), ()))
    cumm = cum_ref[...]
    incl = incl_ref[...]
    strict = strict_ref[...]
    d8 = d8_ref[...]
    eye = (lax.broadcasted_iota(jnp.int32, (c, c), 0) == lax.broadcasted_iota(jnp.int32, (c, c), 1)).astype(f32)

    def mm(a, b):
        return jnp.dot(a.astype(bf16), b.astype(bf16), preferred_element_type=f32)

    def rep(x):
        return jnp.concatenate([x] * (hd // c), axis=1)

    nh = g_ref.shape[0]
    ns = range(nh * nc)
    head = [m // nc for m in ns]
    rows = [slice((m % nc) * c, (m % nc + 1) * c) for m in ns]
    cols = [slice(h * hd, (h + 1) * hd) for h in head]
    q = [q_ref[rows[m], cols[m]] for m in ns]
    k = [k_ref[rows[m], cols[m]] for m in ns]
    v = [v_ref[rows[m], cols[m]] for m in ns]
    kb = [x.astype(bf16) for x in k]
    kk = [lax.dot_general(x, x, contract_last, preferred_element_type=f32) for x in kb]
    qk = [lax.dot_general(q[n].astype(bf16), kb[n], contract_last, preferred_element_type=f32) for n in ns]
    grow = [jnp.broadcast_to(g_ref[head[m], :, rows[m]], (c, c)) for m in ns]
    ghi = [x.astype(bf16) for x in grow]
    glo = [(grow[n] - ghi[n].astype(f32)).astype(bf16) for n in ns]
    gm = [jnp.dot(ghi[n], cumm, preferred_element_type=f32) + jnp.dot(glo[n], cumm, preferred_element_type=f32)
          for n in ns]
    gam_row = [x[:, :c] for x in gm]
    tot = [x[:, c:] for x in gm]
    gam_col = [x.T for x in gam_row]
    beta_col = [jnp.broadcast_to(b_ref[head[m], :, rows[m]], (c, c)).T for m in ns]
    decay = [incl * jnp.exp(jnp.minimum(gam_col[n] - gam_row[n], 0.0)) for n in ns]
    a = [strict * beta_col[n] * kk[n] * decay[n] for n in ns]
    a0 = [x * d8 for x in a]
    n2 = [mm(x, x) for x in a0]
    n4 = [mm(x, x) for x in n2]
    t = [mm(eye - a0[n], eye + n2[n]) for n in ns]
    t = [mm(t[n], eye + n4[n]) for n in ns]
    for j in range(mrg_ref.shape[0]):
        p = [mm(a[n] * mrg_ref[j], t[n]) for n in ns]
        t = [t[n] - mm(t[n], p[n]) for n in ns]
    beta128 = [rep(x) for x in beta_col]
    egam128 = [rep(jnp.exp(x)) for x in gam_col]
    solb = [mm(t[n], jnp.concatenate([k[n] * beta128[n] * egam128[n], v[n] * beta128[n]], axis=1)).astype(bf16)
            for n in ns]
    av = [jnp.dot((qk[n] * decay[n]).astype(bf16), solb[n], preferred_element_type=f32) for n in ns]
    k_dec = [(k[n] * rep(jnp.exp(tot[n] - gam_col[n]))).astype(bf16) for n in ns]
    wu = [lax.dot_general(k_dec[n], solb[n], contract_first, preferred_element_type=f32) for n in ns]
    for n in ns:
        qd_sc[n] = (q[n] * egam128[n] - av[n][:, :hd]).astype(bf16)
        o_ref[rows[n], cols[n]] = av[n][:, hd:]
        w_sc[n] = wu[n][:, :hd].astype(bf16)
        u_sc[n] = wu[n][:, hd:]
        dec_sc[n] = rep(jnp.exp(tot[n][0:SUBLANES, :]))

    def body(ci, states):
        ce = ci + direction * (nc - 1 - 2 * ci)
        new = []
        for h in range(nh):
            m = h * nc + ce
            sb = states[h].astype(bf16)
            sin_sc[m] = sb
            new.append(states[h] * dec_sc[m][0:1] - jnp.dot(w_sc[m], sb, preferred_element_type=f32) + u_sc[m])
        return tuple(new)

    states = lax.fori_loop(0, nc, body, tuple(s_sc[h] for h in range(nh)))
    for h in range(nh):
        s_sc[h] = states[h]

    for m in ns:
        o_ref[rows[m], cols[m]] += jnp.dot(qd_sc[m], sin_sc[m], preferred_element_type=f32)


def gdn_scan(qkv, gb):
    _, bsz, l, _ = qkv.shape
    hd = D_HEAD_DIM
    tt = min(GDN_TT, l)
    nt = l // tt
    nc = tt // CHUNK
    assert l % tt == 0 and tt % CHUNK == 0
    consts = gdn_constants()
    gb4 = gb.reshape(bsz, 4 * D_HEADS, 1, l)
    nh = GDN_HEADS_PER_STEP
    assert D_HEADS % nh == 0
    tidx = lambda d, i: i + d * (nt - 1 - 2 * i)
    qkv_spec = lambda p: pl.BlockSpec((None, None, tt, nh * hd), lambda b, h, d, i: (p, b, tidx(d, i), h))
    row_spec = lambda off: pl.BlockSpec((None, nh, 1, tt),
                                        lambda b, h, d, i: (b, (off + d * D_HEADS) // nh + h, 0, tidx(d, i)))
    per_dir = lambda a: pl.BlockSpec((None,) + a.shape[1:], lambda b, h, d, i: (d,) + (0,) * (a.ndim - 1))
    whole = lambda a: pl.BlockSpec(a.shape, lambda b, h, d, i: (0,) * a.ndim)
    return pl.pallas_call(
        functools.partial(_gdn_kernel, nc=nc),
        grid=(bsz, D_HEADS // nh, 2, nt),
        in_specs=[qkv_spec(0), qkv_spec(1), qkv_spec(2), row_spec(0), row_spec(2 * D_HEADS),
                  per_dir(consts[0]), per_dir(consts[1]), per_dir(consts[2]), whole(consts[3]), whole(consts[4])],
        out_specs=pl.BlockSpec((None, None, tt, nh * hd), lambda b, h, d, i: (d, b, tidx(d, i), h)),
        out_shape=jax.ShapeDtypeStruct((2, bsz, l, D_WIDTH), jnp.float32),
        scratch_shapes=[pltpu.VMEM((nh, hd, hd), jnp.float32),
                        pltpu.VMEM((nh * nc, CHUNK, hd), jnp.bfloat16),
                        pltpu.VMEM((nh * nc, SUBLANES, hd), jnp.float32),
                        pltpu.VMEM((nh * nc, hd, hd), jnp.bfloat16),
                        pltpu.VMEM((nh * nc, hd, hd), jnp.float32),
                        pltpu.VMEM((nh * nc, hd, hd), jnp.bfloat16)],
        compiler_params=pltpu.CompilerParams(dimension_semantics=("parallel", "parallel", "parallel", "arbitrary"),
                                             vmem_limit_bytes=VMEM_LIMIT_BYTES),
        name="gdn_scan",
    )(qkv, qkv, qkv, gb4, gb4, *consts)


def gated_deltanet(proj3, conv_w, a_log, dt_bias, out_gain):
    o2 = gdn_scan(gdn_prep(proj3, conv_w), gdn_gates(proj3, a_log, dt_bias))
    return bidir_finalize(o2, proj3, OD_GATE_BLOCK, out_gain, name="gdn_finalize")


MOE_TT = 512
MOE_SUB = 128
MOE_ROWS = 256
MOE_SLAB = 256
MOE_VMEM_LIMIT_BYTES = 56 * 1024 * 1024


def _router_kernel(x_ref, g_ref, wr_ref, h_ref, aff_ref):
    x = x_ref[...]
    h = (x * lax.rsqrt(jnp.mean(x * x, axis=-1, keepdims=True) + EPS) * g_ref[...]).astype(jnp.bfloat16)
    h_ref[...] = h
    logits = jnp.dot(h, wr_ref[...], preferred_element_type=jnp.float32)
    lane = lax.broadcasted_iota(jnp.int32, logits.shape, 1)
    logits = jnp.where(lane < N_EXPERTS, logits, -jnp.inf)
    p = jnp.exp(logits - jnp.max(logits, axis=-1, keepdims=True))
    aff = p / jnp.sum(p, axis=-1, keepdims=True)
    aff_ref[...] = aff.T[0:N_EXPERTS, :]


def moe_route(x, gain, w_router, *, tm=MOE_TT):
    bsz, l, d = x.shape
    wr = jnp.pad(w_router.astype(jnp.bfloat16), ((0, 0), (0, LANES - N_EXPERTS)))
    return pl.pallas_call(
        _router_kernel,
        grid=(bsz, l // tm),
        in_specs=[pl.BlockSpec((None, tm, d), lambda b, i: (b, i, 0)),
                  pl.BlockSpec((1, d), lambda b, i: (0, 0)),
                  pl.BlockSpec((d, LANES), lambda b, i: (0, 0))],
        out_specs=[pl.BlockSpec((None, tm, d), lambda b, i: (b, i, 0)),
                   pl.BlockSpec((None, N_EXPERTS, tm), lambda b, i: (b, 0, i))],
        out_shape=[jax.ShapeDtypeStruct((bsz, l, d), jnp.bfloat16),
                   jax.ShapeDtypeStruct((bsz, N_EXPERTS, l), jnp.float32)],
        compiler_params=pltpu.CompilerParams(dimension_semantics=("parallel", "parallel"),
                                             vmem_limit_bytes=VMEM_LIMIT_BYTES),
        name="moe_router",
    )(x, gain.reshape(1, d).astype(jnp.float32), wr)


def _select_kernel(aff_ref, pre_ref, smap_ref, gate_ref, cnt_ref, *, cap, tt):
    f32 = jnp.float32
    bf16 = jnp.bfloat16
    aff = aff_ref[...]
    e, l = aff.shape
    nl = l // LANES
    tiles = [slice(j * LANES, (j + 1) * LANES) for j in range(nl)]
    bits = pltpu.bitcast(aff, jnp.int32)
    bt = [bits[:, s] for s in tiles]

    def lane_total(x):
        return jnp.broadcast_to(jnp.sum(x, axis=-1, keepdims=True), (e, LANES))

    def search(i, thr):
        cand = thr | jnp.left_shift(jnp.int32(1), 30 - i)
        acc = jnp.zeros((e, LANES), jnp.int32)
        for x in bt:
            acc = acc + (x >= cand).astype(jnp.int32)
        return jnp.where(lane_total(acc) >= cap, cand, thr)

    thr = lax.fori_loop(0, 31, search, jnp.zeros((e, LANES), jnp.int32))
    gt = [x > thr for x in bt]
    eq = [x == thr for x in bt]
    acc = jnp.zeros((e, LANES), jnp.int32)
    for x in gt:
        acc = acc + x.astype(jnp.int32)
    need = (cap - lane_total(acc)).astype(f32)

    pre = pre_ref[...]

    def prefix(flags):
        outs = [jnp.dot(jnp.where(x, 1.0, 0.0).astype(bf16), pre, preferred_element_type=f32) for x in flags]
        carry = jnp.zeros((e, LANES), f32)
        res = []
        for o in outs:
            res.append(o[:, :LANES] + carry)
            carry = carry + o[:, LANES:]
        return res, [o[:, LANES:] for o in outs]

    rank_eq, _ = prefix(eq)
    sel = [jnp.logical_or(gt[j], jnp.logical_and(eq[j], rank_eq[j] < need)) for j in range(nl)]
    pos, totals = prefix(sel)
    lane = lax.broadcasted_iota(jnp.int32, (e, LANES), 1)
    cnt = jnp.zeros((e, LANES), f32)
    per = tt // LANES
    for j in range(nl):
        smap_ref[:, tiles[j]] = jnp.where(sel[j], pos[j], -1.0)
        gate_ref[:, tiles[j]] = jnp.where(sel[j], aff[:, tiles[j]], 0.0)
        cnt = cnt + jnp.where(lane == j // per, totals[j], 0.0)
    cnt_ref[...] = cnt


def moe_select(aff, cap, *, tt=MOE_TT):
    import numpy as np
    bsz, e, l = aff.shape
    assert l // tt <= LANES
    i = np.arange(LANES)
    pre = np.concatenate([i[:, None] < i[None, :], np.ones((LANES, LANES), bool)], axis=1)
    row = pl.BlockSpec((None, e, l), lambda b: (b, 0, 0))
    return pl.pallas_call(
        functools.partial(_select_kernel, cap=cap, tt=tt),
        grid=(bsz,),
        in_specs=[row, pl.BlockSpec((LANES, 2 * LANES), lambda b: (0, 0))],
        out_specs=[row, row, pl.BlockSpec((None, e, LANES), lambda b: (b, 0, 0))],
        out_shape=[jax.ShapeDtypeStruct((bsz, e, l), jnp.float32), jax.ShapeDtypeStruct((bsz, e, l), jnp.float32),
                   jax.ShapeDtypeStruct((bsz, e, LANES), jnp.float32)],
        compiler_params=pltpu.CompilerParams(dimension_semantics=("parallel",),
                                             vmem_limit_bytes=VMEM_LIMIT_BYTES),
        name="moe_select",
    )(aff, jnp.asarray(pre, jnp.bfloat16))


def _slot_one_hot(pos, base, rows, n):
    slot = (base + lax.broadcasted_iota(jnp.int32, (rows, n), 0)).astype(jnp.float32)
    return jnp.where(pos == slot, 1.0, 0.0).astype(jnp.bfloat16)


def _expert_kernel(cs_ref, h_ref, smap_ref, gate_ref, wg32_ref, wu32_ref, wd32_ref, o_ref,
                   xs_sc, gs_sc, wg_ref, wu_ref, wd_ref, *, nj, cap):
    e = pl.program_id(0)
    b = pl.program_id(1)
    j = pl.program_id(2)
    f32 = jnp.float32
    bf16 = jnp.bfloat16

    @pl.when(jnp.logical_and(b == 0, j == 0))
    def _():
        wg_ref[...] = wg32_ref[...].astype(bf16)
        wu_ref[...] = wu32_ref[...].astype(bf16)
        wd_ref[...] = wd32_ref[...].astype(bf16)

    @pl.when(j == 0)
    def _():
        xs_sc[...] = jnp.zeros_like(xs_sc)
        gs_sc[...] = jnp.zeros_like(gs_sc)

    base = (b * N_EXPERTS + e) * (nj + 1) + j
    c0 = cs_ref[base]
    c1 = cs_ref[base + 1]
    pos = smap_ref[...]
    gate = gate_ref[...]
    hb = h_ref[...]
    tt = hb.shape[0]

    def gather(st, carry):
        r0 = pl.multiple_of(st * MOE_SUB, MOE_SUB)
        oh = _slot_one_hot(pos, r0, MOE_SUB, tt)
        xs_sc[pl.ds(r0, MOE_SUB), :] += jnp.dot(oh, hb, preferred_element_type=f32)
        g = jnp.sum(oh.astype(f32) * gate, axis=-1, keepdims=True)
        gs_sc[pl.ds(r0, MOE_SUB), :] += jnp.broadcast_to(g, (MOE_SUB, LANES))
        return carry

    lax.fori_loop(c0 // MOE_SUB, (c1 + MOE_SUB - 1) // MOE_SUB, gather, 0)

    @pl.when(j == nj - 1)
    def _():
        rows_per = min(MOE_ROWS, cap)
        for r in range(cap // rows_per):
            rows = slice(r * rows_per, (r + 1) * rows_per)
            xb = xs_sc[rows, :].astype(bf16)
            g = jnp.dot(xb, wg_ref[...], preferred_element_type=f32)
            u = jnp.dot(xb, wu_ref[...], preferred_element_type=f32)
            hid = (g * jax.nn.sigmoid(g) * u).astype(bf16)
            out = jnp.dot(hid, wd_ref[...], preferred_element_type=f32)
            scale = jnp.concatenate([gs_sc[rows, :]] * (out.shape[1] // LANES), axis=1)
            o_ref[rows, :] = (out * scale).astype(bf16)


def moe_experts(hb, smap, gate, cs, w_gate, w_up, w_down, layer, cap, *, tt=MOE_TT):
    bsz, l, d = hb.shape
    _, e, _, ff = w_gate.shape
    nj = l // tt
    smap4 = smap.reshape(bsz, e, 1, l)
    gate4 = gate.reshape(bsz, e, 1, l)
    tok = pl.BlockSpec((None, None, 1, tt), lambda ei, b, j, cs_ref: (b, ei, 0, j))
    once = pl.Buffered(1)
    grid_spec = pltpu.PrefetchScalarGridSpec(
        num_scalar_prefetch=1,
        grid=(e, bsz, nj),
        in_specs=[pl.BlockSpec((None, tt, d), lambda ei, b, j, cs_ref: (b, j, 0)), tok, tok,
                  pl.BlockSpec((None, None, d, ff), lambda ei, b, j, cs_ref: (layer, ei, 0, 0), pipeline_mode=once),
                  pl.BlockSpec((None, None, d, ff), lambda ei, b, j, cs_ref: (layer, ei, 0, 0), pipeline_mode=once),
                  pl.BlockSpec((None, None, ff, d), lambda ei, b, j, cs_ref: (layer, ei, 0, 0), pipeline_mode=once)],
        out_specs=pl.BlockSpec((None, None, cap, d), lambda ei, b, j, cs_ref: (b, ei, 0, 0)),
        scratch_shapes=[pltpu.VMEM((cap, d), jnp.float32), pltpu.VMEM((cap, LANES), jnp.float32),
                        pltpu.VMEM((d, ff), jnp.bfloat16), pltpu.VMEM((d, ff), jnp.bfloat16),
                        pltpu.VMEM((ff, d), jnp.bfloat16)])
    return pl.pallas_call(
        functools.partial(_expert_kernel, nj=nj, cap=cap),
        grid_spec=grid_spec,
        out_shape=jax.ShapeDtypeStruct((bsz, e, cap, d), jnp.bfloat16),
        compiler_params=pltpu.CompilerParams(dimension_semantics=("parallel", "arbitrary", "arbitrary"),
                                             vmem_limit_bytes=MOE_VMEM_LIMIT_BYTES),
        name="moe_experts",
    )(cs, hb, smap4, gate4, w_gate, w_up, w_down)


def _combine_kernel(cs_ref, x_ref, smap_ref, ow_ref, y_ref, *, nj, tt):
    b = pl.program_id(0)
    e = pl.program_id(2)

    @pl.when(e == 0)
    def _():
        y_ref[...] = x_ref[...]

    contract_first = (((0,), (0,)), ((), ()))
    cap = ow_ref.shape[0]
    win = min(2 * MOE_SUB, cap)
    base = (b * N_EXPERTS + e) * (nj + 1)
    cols = [slice(j * tt, (j + 1) * tt) for j in range(nj)]
    pos = [smap_ref[:, c] for c in cols]
    r0 = [pl.multiple_of(jnp.minimum(cs_ref[base + j] // MOE_SUB * MOE_SUB, cap - win), MOE_SUB) for j in range(nj)]
    oh = [_slot_one_hot(pos[j], r0[j], win, tt) for j in range(nj)]
    add = [lax.dot_general(oh[j], ow_ref[pl.ds(r0[j], win), :], contract_first, preferred_element_type=jnp.float32)
           for j in range(nj)]
    for j in range(nj):
        y_ref[cols[j], :] += add[j]

    for j in range(nj):
        def scatter(st, carry):
            s0 = pl.multiple_of(st * MOE_SUB, MOE_SUB)
            y_ref[cols[j], :] += lax.dot_general(_slot_one_hot(pos[j], s0, MOE_SUB, tt),
                                                 ow_ref[pl.ds(s0, MOE_SUB), :], contract_first,
                                                 preferred_element_type=jnp.float32)
            return carry

        lax.fori_loop((r0[j] + win) // MOE_SUB, (cs_ref[base + j + 1] + MOE_SUB - 1) // MOE_SUB, scatter, 0)


def moe_combine(x, smap, outw, cs, *, tt=MOE_TT):
    bsz, l, d = x.shape
    e, cap = outw.shape[1:3]
    nj = l // tt
    smap4 = smap.reshape(bsz, e, 1, l)
    grid_spec = pltpu.PrefetchScalarGridSpec(
        num_scalar_prefetch=1,
        grid=(bsz, d // MOE_SLAB, e),
        in_specs=[pl.BlockSpec((None, l, MOE_SLAB), lambda b, s, ei, cs_ref: (b, 0, s)),
                  pl.BlockSpec((None, None, 1, l), lambda b, s, ei, cs_ref: (b, ei, 0, 0)),
                  pl.BlockSpec((None, None, cap, MOE_SLAB), lambda b, s, ei, cs_ref: (b, ei, 0, s))],
        out_specs=pl.BlockSpec((None, l, MOE_SLAB), lambda b, s, ei, cs_ref: (b, 0, s)))
    return pl.pallas_call(
        functools.partial(_combine_kernel, nj=nj, tt=tt),
        grid_spec=grid_spec,
        out_shape=jax.ShapeDtypeStruct((bsz, l, d), jnp.float32),
        compiler_params=pltpu.CompilerParams(dimension_semantics=("parallel", "parallel", "arbitrary"),
                                             vmem_limit_bytes=MOE_VMEM_LIMIT_BYTES),
        name="moe_combine",
    )(cs, x, smap4, outw)


def ec_moe_layer(x, gain, w_router, w_gate, w_up, w_down, layer):
    bsz, l, d = x.shape
    cap = EC_CAPACITY_FACTOR * l // N_EXPERTS
    tt = min(MOE_TT, l)
    nj = l // tt
    hb, aff = moe_route(x, gain, w_router, tm=tt)
    smap, gate, cnt = moe_select(aff, cap, tt=tt)
    cs = jnp.concatenate([jnp.zeros((bsz, N_EXPERTS, 1), jnp.float32), jnp.cumsum(cnt[..., :nj], axis=-1)], axis=-1)
    cs = cs.astype(jnp.int32).reshape(-1)
    outw = moe_experts(hb, smap, gate, cs, w_gate, w_up, w_down, layer, cap, tt=tt)
    return moe_combine(x, smap, outw, cs, tt=tt)


def kernel(x, mix_norm, ffn_norm, ev_w_in, ev_w_out, a_lb_logits, a_out_norm, s5_lambda_re, s5_lambda_im, s5_log_step, s5_b_re, s5_b_im, s5_c_re, s5_c_im, s5_d, s5_glu_w, s5_glu_b, od_w_in, od_w_out, c_q_norm, c_k_norm, c_lambda, c_out_norm, rel_bias, d_conv_w, d_a_log, d_dt_bias, d_out_norm, moe_router, moe_w_gate, moe_w_up, moe_w_down):
    bsz, l, d = x.shape
    p = jax.nn.softmax(a_lb_logits.astype(jnp.float32), axis=0)
    cum = jnp.cumsum(p, axis=0)
    lower_bounds = cum - cum[0:1]
    bias5 = rel_bias_tiles(rel_bias, ATT_T)
    for layer in range(DEPTH):
        j = layer // 2
        if layer % 2 == 0:
            proj, u_tb = norm_matmul(x, mix_norm[layer], ev_w_in[j], tail=B_WIDTH)
            o_a = hgrn2_mixer(proj, lower_bounds[j], a_out_norm[j])
            o_b = s5_mixer_tb(u_tb.reshape(l * bsz, B_WIDTH), bsz, s5_lambda_re[j], s5_lambda_im[j], s5_log_step[j],
                              s5_b_re[j], s5_b_im[j], s5_c_re[j], s5_c_im[j], s5_d[j], s5_glu_w[j], s5_glu_b[j])
            x = matmul_residual(o_a, o_b.reshape(l, bsz * B_WIDTH), ev_w_out[j], x, a2_time_major=True)
        else:
            o2 = 3 * C_WIDTH + 3 * D_WIDTH
            o4 = o2 + 4 * D_HEADS
            w = od_w_in[j]
            w_in = jnp.concatenate([w[:, :o2], w[:, o4:], w[:, o2:o4],
                                    jnp.zeros((d, OD_COLS - w.shape[1]), w.dtype)], axis=1)
            proj = norm_matmul(x, mix_norm[layer], w_in)
            o_c = diff_attention(proj, c_q_norm[j], c_k_norm[j], c_lambda[j], c_out_norm[j], bias5, layer)
            o_d = gated_deltanet(proj, d_conv_w[j], d_a_log[j], d_dt_bias[j], d_out_norm[j])
            x = matmul_residual(o_c, o_d, od_w_out[j], x)
        x = ec_moe_layer(x, ffn_norm[layer], moe_router[layer], moe_w_gate, moe_w_up, moe_w_down, layer)
    return x
```

```python
import functools
import math

import jax
import jax.numpy as jnp
from jax import lax
from jax.experimental import pallas as pl
from jax.experimental.pallas import tpu as pltpu

D_MODEL = 1024
DEPTH = 4
MIX_WIDTH = D_MODEL
A_WIDTH = MIX_WIDTH // 2
A_HEAD_DIM = 128
A_HEADS = A_WIDTH // A_HEAD_DIM
B_WIDTH = MIX_WIDTH - A_WIDTH
S5_GROUP = 16
S5_GROUPS = B_WIDTH // S5_GROUP
S5_STATE = 64
C_WIDTH = MIX_WIDTH // 2
C_HEAD_DIM = 64
C_HEADS = C_WIDTH // (2 * C_HEAD_DIM)
C_V_DIM = 2 * C_HEAD_DIM
D_WIDTH = MIX_WIDTH - C_WIDTH
D_HEAD_DIM = 128
D_HEADS = D_WIDTH // D_HEAD_DIM
CONV_WIDTH = 5
N_EXPERTS = 16
EXPERT_FF = 2 * D_MODEL
EC_CAPACITY_FACTOR = 2
REL_BUCKETS = 32
REL_MAX_DIST = 128
CHUNK = 64
Q_BLOCK = 128
EPS = 1e-6

VMEM_LIMIT_BYTES = 48 * 1024 * 1024


PROJ_TM = 512
PROJ_COLS = 512


def _norm_matmul_kernel(x_ref, g_ref, w_ref, o_ref, *tail_ref, main):
    x = x_ref[...]
    y = (x * lax.rsqrt(jnp.mean(x * x, axis=-1, keepdims=True) + EPS) * g_ref[...]).astype(jnp.bfloat16)
    for c0 in range(0, main, PROJ_COLS):
        c1 = min(c0 + PROJ_COLS, main)
        o_ref[:, c0:c1] = jnp.dot(y, w_ref[:, c0:c1], preferred_element_type=jnp.float32)
    if tail_ref:
        tail_ref[0][...] = jnp.dot(y, w_ref[:, main:], preferred_element_type=jnp.float32)


def norm_matmul(x, gain, w, *, tail=0, tm=PROJ_TM):
    bsz, l, k = x.shape
    m = w.shape[1]
    main = m - tail
    tm = min(tm, l)
    out_shape = [jax.ShapeDtypeStruct((bsz, l, main), jnp.float32)]
    out_specs = [pl.BlockSpec((None, tm, main), lambda b, i: (b, i, 0))]
    if tail:
        out_shape.append(jax.ShapeDtypeStruct((l, bsz * tail), jnp.float32))
        out_specs.append(pl.BlockSpec((tm, tail), lambda b, i: (i, b)))
    outs = pl.pallas_call(
        functools.partial(_norm_matmul_kernel, main=main),
        grid=(bsz, l // tm),
        in_specs=[pl.BlockSpec((None, tm, k), lambda b, i: (b, i, 0)),
                  pl.BlockSpec((1, k), lambda b, i: (0, 0)),
                  pl.BlockSpec((k, m), lambda b, i: (0, 0), pipeline_mode=pl.Buffered(1))],
        out_specs=out_specs,
        out_shape=out_shape,
        compiler_params=pltpu.CompilerParams(dimension_semantics=("parallel", "parallel"),
                                             vmem_limit_bytes=VMEM_LIMIT_BYTES),
        name="norm_matmul",
    )(x, gain.reshape(1, k).astype(jnp.float32), w.astype(jnp.bfloat16))
    return outs if tail else outs[0]


def _matmul_res_kernel(a1_ref, a2_ref, w1_ref, w2_ref, r_ref, o_ref):
    bf16 = jnp.bfloat16
    o_ref[...] = (r_ref[...] + jnp.dot(a1_ref[...].astype(bf16), w1_ref[...], preferred_element_type=jnp.float32)
                  + jnp.dot(a2_ref[...].astype(bf16), w2_ref[...], preferred_element_type=jnp.float32))


def matmul_residual(a1, a2, w, res, *, a2_time_major=False, tm=PROJ_TM):
    bsz, l, k1 = a1.shape
    m = w.shape[1]
    k2 = w.shape[0] - k1
    tm = min(tm, l)
    wb = w.astype(jnp.bfloat16)
    a2_spec = (pl.BlockSpec((tm, k2), lambda b, i: (i, b)) if a2_time_major
               else pl.BlockSpec((None, tm, k2), lambda b, i: (b, i, 0)))
    row = pl.BlockSpec((None, tm, m), lambda b, i: (b, i, 0))
    return pl.pallas_call(
        _matmul_res_kernel,
        grid=(bsz, l // tm),
        in_specs=[pl.BlockSpec((None, tm, k1), lambda b, i: (b, i, 0)), a2_spec,
                  pl.BlockSpec((k1, m), lambda b, i: (0, 0)),
                  pl.BlockSpec((k2, m), lambda b, i: (0, 0)), row],
        out_specs=row,
        out_shape=jax.ShapeDtypeStruct((bsz, l, m), jnp.float32),
        compiler_params=pltpu.CompilerParams(dimension_semantics=("parallel", "parallel"),
                                             vmem_limit_bytes=VMEM_LIMIT_BYTES),
        name="matmul_residual",
    )(a1, a2, wb[:k1], wb[k1:], res)


HG_LEVELS = tuple(CHUNK >> (i + 1) for i in range(CHUNK.bit_length() - 1))
HG_TOT_ROWS = 8
HG_TT = 512


def hgrn2_constants():
    import numpy as np
    c = CHUNK
    r = np.arange(c)[:, None]
    u = np.arange(c)[None, :]
    stacks, masks = [], []
    for direction in range(2):
        fwd = direction == 0
        lvl_masks = []
        for m in HG_LEVELS:
            blk = r // (2 * m)
            later = (r % (2 * m)) >= m
            lvl_masks.append((blk == blk.T) & (later & ~later.T if fwd else ~later & later.T))
        stacks.append(np.concatenate([(u <= r) if fwd else (u >= r), np.ones((HG_TOT_ROWS, c), bool)], axis=0))
        masks.append(np.stack(lvl_masks))
    return (jnp.asarray(np.stack(stacks), jnp.bfloat16), jnp.asarray(np.stack(masks), jnp.float32))


def _hgrn2_kernel(q_ref, f_ref, v_ref, loglb_ref, log1mlb_ref, onemlb_ref, ast_ref, mask_ref, o_ref,
                  st_sc, qd_sc, dec_sc, upd_sc, sin_sc, *, nc):
    direction = pl.program_id(2)

    @pl.when(pl.program_id(3) == 0)
    def _():
        st_sc[...] = jnp.zeros_like(st_sc)

    bf16 = jnp.bfloat16
    f32 = jnp.float32
    c = CHUNK
    hd = A_HEAD_DIM
    dirf = direction.astype(f32)
    loglb = loglb_ref[...]
    log1mlb = log1mlb_ref[...]
    onemlb = onemlb_ref[...]
    ast = ast_ref[...]
    contract_last = (((1,), (1,)), ((), ()))
    contract_first = (((0,), (0,)), ((), ()))

    ns = range(nc)
    rows = [slice(n * c, (n + 1) * c) for n in ns]
    z = [f_ref[r, :] for r in rows]
    v = [v_ref[r, :] for r in rows]
    qr = [q_ref[r, :] for r in rows]
    q = [x * jax.nn.sigmoid(x) for x in qr]
    e = [jnp.exp(-jnp.abs(x)) for x in z]
    cc = [log1mlb + jnp.minimum(z[n], 0.0) - jnp.log1p(e[n]) for n in ns]
    lf = [jnp.maximum(loglb, x) + jnp.log1p(jnp.exp(-jnp.abs(loglb - x))) for x in cc]
    k = [onemlb * jnp.where(z[n] >= 0, e[n], 1.0) / (1.0 + e[n]) for n in ns]
    hi = [x.astype(bf16) for x in lf]
    lo = [(lf[n] - hi[n].astype(f32)).astype(bf16) for n in ns]
    d = [jnp.dot(ast, hi[n], preferred_element_type=f32) + jnp.dot(ast, lo[n], preferred_element_type=f32)
         for n in ns]
    cum = [x[0:c] for x in d]
    tot = [x[c:c + HG_TOT_ROWS] for x in d]
    ref = [cum[n] - dirf * lf[n] for n in ns]
    attn = [jnp.zeros((c, c), f32) for _ in ns]
    for li, m in enumerate(HG_LEVELS):
        nb = c // (2 * m)
        split = [jnp.broadcast_to(x.reshape(nb, 2 * m, hd)[:, m - 1:m, :], (nb, 2 * m, hd)).reshape(c, hd)
                 for x in ref]
        x = [jnp.exp(-jnp.abs(cum[n] - split[n])) for n in ns]
        s = [lax.dot_general((q[n] * x[n]).astype(bf16), (k[n] * x[n]).astype(bf16), contract_last,
                             preferred_element_type=f32) for n in ns]
        attn = [attn[n] + mask_ref[li] * s[n] for n in ns]
    vb = [x.astype(bf16) for x in v]
    intra = [jnp.dot(attn[n].astype(bf16), vb[n], preferred_element_type=f32) for n in ns]
    upd = [lax.dot_general(vb[n], (k[n] * jnp.exp(tot[n][0:1] - cum[n])).astype(bf16), contract_first,
                           preferred_element_type=f32) for n in ns]
    for n in ns:
        o_ref[rows[n], :] = intra[n] + jnp.sum(q[n] * k[n], axis=-1, keepdims=True) * v[n]
        qd_sc[n] = (q[n] * jnp.exp(cum[n])).astype(bf16)
        dec_sc[n] = jnp.exp(tot[n])
        upd_sc[n] = upd[n]

    def body(ci, st):
        ce = ci + direction * (nc - 1 - 2 * ci)
        sin_sc[ce] = st.astype(bf16)
        return st * dec_sc[ce][0:1] + upd_sc[ce]

    st_sc[...] = lax.fori_loop(0, nc, body, st_sc[...])

    for n in range(nc):
        rows = slice(n * c, (n + 1) * c)
        o_ref[rows, :] += lax.dot_general(qd_sc[n], sin_sc[n], contract_last, preferred_element_type=f32)


def hgrn2_scan(proj3, lb):
    bsz, l, _ = proj3.shape
    hd = A_HEAD_DIM
    tt = min(HG_TT, l)
    nt = l // tt
    assert l % tt == 0 and tt % CHUNK == 0
    ast, masks = hgrn2_constants()
    lb = lb.astype(jnp.float32)
    vecs = [jnp.log(lb).reshape(2, 1, A_WIDTH), jnp.log1p(-lb).reshape(2, 1, A_WIDTH), (1.0 - lb).reshape(2, 1, A_WIDTH)]
    tidx = lambda d, i: i + d * (nt - 1 - 2 * i)
    vec = pl.BlockSpec((None, 1, hd), lambda b, h, d, i: (d, 0, h))
    return pl.pallas_call(
        functools.partial(_hgrn2_kernel, nc=tt // CHUNK),
        grid=(bsz, A_HEADS, 2, nt),
        in_specs=[pl.BlockSpec((None, tt, hd), lambda b, h, d, i: (b, tidx(d, i), h)),
                  pl.BlockSpec((None, tt, hd), lambda b, h, d, i: (b, tidx(d, i), (1 + d) * A_HEADS + h)),
                  pl.BlockSpec((None, tt, hd), lambda b, h, d, i: (b, tidx(d, i), 3 * A_HEADS + h)),
                  vec, vec, vec,
                  pl.BlockSpec((None,) + ast.shape[1:], lambda b, h, d, i: (d, 0, 0)),
                  pl.BlockSpec((None,) + masks.shape[1:], lambda b, h, d, i: (d, 0, 0, 0))],
        out_specs=pl.BlockSpec((None, None, tt, hd), lambda b, h, d, i: (d, b, tidx(d, i), h)),
        out_shape=jax.ShapeDtypeStruct((2, bsz, l, A_WIDTH), jnp.float32),
        scratch_shapes=[pltpu.VMEM((hd, hd), jnp.float32),
                        pltpu.VMEM((tt // CHUNK, CHUNK, hd), jnp.bfloat16),
                        pltpu.VMEM((tt // CHUNK, HG_TOT_ROWS, hd), jnp.float32),
                        pltpu.VMEM((tt // CHUNK, hd, hd), jnp.float32),
                        pltpu.VMEM((tt // CHUNK, hd, hd), jnp.bfloat16)],
        compiler_params=pltpu.CompilerParams(dimension_semantics=("parallel", "parallel", "parallel", "arbitrary"),
                                             vmem_limit_bytes=VMEM_LIMIT_BYTES),
        name="hgrn2_scan",
    )(proj3, proj3, proj3, *vecs, ast, masks)


def _hgrn2_final_kernel(of_ref, ob_ref, g_ref, gain_ref, o_ref):
    o = of_ref[...] + ob_ref[...]
    y = o * lax.rsqrt(jnp.mean(o * o, axis=-1, keepdims=True) + EPS) * gain_ref[...]
    g = g_ref[...]
    o_ref[...] = y * (g * jax.nn.sigmoid(g))


def bidir_finalize(o2, proj3, gate_block, out_gain, *, name, tm=1024):
    _, bsz, l, w = o2.shape
    hd = 128
    tm = min(tm, l)
    return pl.pallas_call(
        _hgrn2_final_kernel,
        grid=(bsz, l // tm, w // hd),
        in_specs=[pl.BlockSpec((None, None, tm, hd), lambda b, i, h: (0, b, i, h)),
                  pl.BlockSpec((None, None, tm, hd), lambda b, i, h: (1, b, i, h)),
                  pl.BlockSpec((None, tm, hd), lambda b, i, h: (b, i, gate_block + h)),
                  pl.BlockSpec((1, hd), lambda b, i, h: (0, 0))],
        out_specs=pl.BlockSpec((None, tm, hd), lambda b, i, h: (b, i, h)),
        out_shape=jax.ShapeDtypeStruct((bsz, l, w), jnp.float32),
        compiler_params=pltpu.CompilerParams(dimension_semantics=("parallel", "parallel", "parallel"),
                                             vmem_limit_bytes=VMEM_LIMIT_BYTES),
        name=name,
    )(o2, o2, proj3, out_gain.reshape(1, hd).astype(jnp.float32))


def hgrn2_mixer(proj3, lb, out_gain):
    return bidir_finalize(hgrn2_scan(proj3, lb), proj3, 4 * A_HEADS, out_gain, name="hgrn2_finalize")


S5_NS = S5_GROUPS * S5_STATE
S5_TT = 64
SUBLANES = 8


def _s5_scan_kernel(u_ref, win_ref, ar_ref, ai_ref, wout_ref, y_ref, bu_sc, xs_sc, st_sc, *, bsz, tt, reverse):
    @pl.when(pl.program_id(0) == 0)
    def _():
        st_sc[...] = jnp.zeros_like(st_sc)

    ub = u_ref[...].astype(jnp.bfloat16)
    halves = 2
    uw = B_WIDTH // halves
    sw = S5_NS // halves
    for hf in range(halves):
        for part in range(2):
            sc = slice(part * S5_NS + hf * sw, part * S5_NS + (hf + 1) * sw)
            bu_sc[:, sc] = jnp.dot(ub[:, hf * uw:(hf + 1) * uw], win_ref[hf * uw:(hf + 1) * uw, sc],
                                   preferred_element_type=jnp.float32)
    ar = jnp.broadcast_to(ar_ref[...], (bsz, S5_NS))
    ai = jnp.broadcast_to(ai_ref[...], (bsz, S5_NS))
    per = SUBLANES // bsz
    ngroups = tt // per

    def body(s, carry):
        xr, xi = carry
        p = (ngroups - 1 - s) if reverse else s
        base = pl.multiple_of(p * SUBLANES, SUBLANES)
        blk = bu_sc[pl.ds(base, SUBLANES), :]
        outs_r = [None] * per
        outs_i = [None] * per
        for ph in (range(per - 1, -1, -1) if reverse else range(per)):
            br = blk[ph * bsz:(ph + 1) * bsz, :S5_NS]
            bi = blk[ph * bsz:(ph + 1) * bsz, S5_NS:]
            xr, xi = ar * xr - ai * xi + br, ar * xi + ai * xr + bi
            outs_r[ph] = xr
            outs_i[ph] = xi
        xs_sc[pl.ds(base, SUBLANES), :S5_NS] = jnp.concatenate(outs_r, axis=0)
        xs_sc[pl.ds(base, SUBLANES), S5_NS:] = jnp.concatenate(outs_i, axis=0)
        return xr, xi

    xr, xi = lax.fori_loop(0, ngroups, body, (st_sc[0], st_sc[1]))
    st_sc[0] = xr
    st_sc[1] = xi
    for hf in range(halves):
        yc = slice(hf * uw, (hf + 1) * uw)
        acc = None
        for part in range(2):
            sc = slice(part * S5_NS + hf * sw, part * S5_NS + (hf + 1) * sw)
            term = jnp.dot(xs_sc[:, sc].astype(jnp.bfloat16), wout_ref[sc, yc], preferred_element_type=jnp.float32)
            acc = term if acc is None else acc + term
        y_ref[:, yc] = acc


def s5_scan(u_tb, win, ar, ai, wout, *, bsz, reverse):
    n = u_tb.shape[0]
    rows = S5_TT * bsz
    nt = n // rows
    assert n % rows == 0 and SUBLANES % bsz == 0
    idx = (lambda i: (nt - 1 - i, 0)) if reverse else (lambda i: (i, 0))
    const = lambda i: (0, 0)
    return pl.pallas_call(
        functools.partial(_s5_scan_kernel, bsz=bsz, tt=S5_TT, reverse=reverse),
        grid=(nt,),
        in_specs=[pl.BlockSpec((rows, B_WIDTH), idx),
                  pl.BlockSpec((B_WIDTH, 2 * S5_NS), const),
                  pl.BlockSpec((1, S5_NS), const),
                  pl.BlockSpec((1, S5_NS), const),
                  pl.BlockSpec((2 * S5_NS, B_WIDTH), const)],
        out_specs=pl.BlockSpec((rows, B_WIDTH), idx),
        out_shape=jax.ShapeDtypeStruct((n, B_WIDTH), jnp.float32),
        scratch_shapes=[pltpu.VMEM((rows, 2 * S5_NS), jnp.float32),
                        pltpu.VMEM((rows, 2 * S5_NS), jnp.float32),
                        pltpu.VMEM((2, bsz, S5_NS), jnp.float32)],
        compiler_params=pltpu.CompilerParams(dimension_semantics=("arbitrary",),
                                             vmem_limit_bytes=VMEM_LIMIT_BYTES),
        name="s5_scan_bwd" if reverse else "s5_scan_fwd",
    )(u_tb, win, ar, ai, wout)


def _s5_final_kernel(u_ref, yf_ref, yb_ref, d_ref, w_ref, b_ref, o_ref):
    y = d_ref[...] * u_ref[...] + yf_ref[...] + yb_ref[...]
    y = jax.nn.gelu(y)
    z = jnp.dot(y.astype(jnp.bfloat16), w_ref[...], preferred_element_type=jnp.float32) + b_ref[...]
    o_ref[...] = y * jax.nn.sigmoid(z)


def s5_finalize(u, yf, yb, d_skip, glu_w, glu_b, *, tm=512):
    n, w = u.shape
    row = pl.BlockSpec((tm, w), lambda i: (i, 0))
    vec = pl.BlockSpec((1, w), lambda i: (0, 0))
    return pl.pallas_call(
        _s5_final_kernel,
        grid=(n // tm,),
        in_specs=[row, row, row, vec, pl.BlockSpec((w, w), lambda i: (0, 0)), vec],
        out_specs=row,
        out_shape=jax.ShapeDtypeStruct((n, w), jnp.float32),
        compiler_params=pltpu.CompilerParams(dimension_semantics=("parallel",),
                                             vmem_limit_bytes=VMEM_LIMIT_BYTES),
        name="s5_finalize",
    )(u, yf, yb, d_skip.reshape(1, w).astype(jnp.float32), glu_w.astype(jnp.bfloat16),
      glu_b.reshape(1, w).astype(jnp.float32))


def s5_direction_params(lam_re, lam_im, log_step, b_re, b_im, c_re, c_im):
    step = jnp.exp(log_step)[:, None]
    mag = jnp.exp(lam_re * step)
    abar_re = mag * jnp.cos(lam_im * step)
    abar_im = mag * jnp.sin(lam_im * step)
    den = lam_re * lam_re + lam_im * lam_im
    fr = ((abar_re - 1.0) * lam_re + abar_im * lam_im) / den
    fi = (abar_im * lam_re - (abar_re - 1.0) * lam_im) / den
    bb_re = fr[..., None] * b_re - fi[..., None] * b_im
    bb_im = fr[..., None] * b_im + fi[..., None] * b_re
    eye = jnp.eye(S5_GROUPS, dtype=jnp.float32)
    win = jnp.concatenate([jnp.einsum('gnp,gh->gphn', bb, eye).reshape(B_WIDTH, S5_NS) for bb in (bb_re, bb_im)],
                          axis=1)
    wout = jnp.concatenate([jnp.einsum('gpn,gh->hngp', c, eye).reshape(S5_NS, B_WIDTH) for c in (c_re, -c_im)],
                           axis=0)
    return (win.astype(jnp.bfloat16), abar_re.reshape(1, S5_NS), abar_im.reshape(1, S5_NS),
            wout.astype(jnp.bfloat16))


def s5_mixer_tb(u_tb, bsz, lam_re, lam_im, log_step, b_re, b_im, c_re, c_im, d_skip, glu_w, glu_b):
    f32 = jnp.float32
    ys = []
    for direction in range(2):
        prm = s5_direction_params(lam_re[direction].astype(f32), lam_im[direction].astype(f32),
                                  log_step[direction].astype(f32), b_re[direction].astype(f32),
                                  b_im[direction].astype(f32), c_re[direction].astype(f32),
                                  c_im[direction].astype(f32))
        ys.append(s5_scan(u_tb, *prm, bsz=bsz, reverse=(direction == 1)))
    return s5_finalize(u_tb, ys[0], ys[1], d_skip, glu_w, glu_b)


def t5_bucket(rel):
    half = REL_BUCKETS // 2
    max_exact = half // 2
    base = jnp.where(rel > 0, half, 0)
    n = jnp.abs(rel)
    nf = jnp.maximum(n, 1).astype(jnp.float32)
    large = max_exact + (jnp.log(nf / max_exact) / math.log(REL_MAX_DIST / max_exact)
                         * (half - max_exact)).astype(jnp.int32)
    large = jnp.minimum(large, half - 1)
    return base + jnp.where(n < max_exact, n, large)


ATT_T = 512
LOG2E = math.log2(math.e)


def rel_bias_tiles(rel_bias, t):
    assert t >= REL_MAX_DIST
    table = rel_bias.astype(jnp.float32) * LOG2E
    tiles = []
    for d in (-1, 0, 1):
        c = table[t5_bucket(d * t + jnp.arange(-(t - 1), t))]
        w = jnp.concatenate([c, c[:1]], axis=0)
        m = jnp.tile(w, (t, 1))[:t * (2 * t - 1)].reshape(t, 2 * t - 1, -1)
        tiles.append(m[:, t - 1:2 * t - 1])
    far_neg = jnp.broadcast_to(table[t5_bucket(jnp.array(-2 * t))], tiles[0].shape)
    far_pos = jnp.broadcast_to(table[t5_bucket(jnp.array(2 * t))], tiles[0].shape)
    out = jnp.stack([far_neg] + tiles + [far_pos], axis=0)
    return jnp.transpose(out, (3, 0, 1, 2))


def _attn_prep_kernel(q_ref, k_ref, v_ref, qg_ref, kg_ref, q2_ref, kt_ref, vb_ref, st_ref):
    lane = lax.broadcasted_iota(jnp.int32, q_ref.shape, 1)
    lo = lane < C_HEAD_DIM
    f32 = jnp.float32

    def max_sq_norms(xb):
        sq = xb.astype(f32) * xb.astype(f32)
        n_lo = jnp.sum(jnp.where(lo, sq, 0.0), axis=-1, keepdims=True)
        n_hi = jnp.sum(jnp.where(lo, 0.0, sq), axis=-1, keepdims=True)
        return jnp.max(n_lo, axis=0, keepdims=True), jnp.max(n_hi, axis=0, keepdims=True)

    def halfnorm(x, g):
        sq = x * x
        s_lo = jnp.sum(jnp.where(lo, sq, 0.0), axis=-1, keepdims=True)
        s_hi = jnp.sum(jnp.where(lo, 0.0, sq), axis=-1, keepdims=True)
        ms = jnp.where(lo, s_lo, s_hi) * (1.0 / C_HEAD_DIM)
        return x * lax.rsqrt(ms + EPS) * g

    qn = halfnorm(q_ref[...], qg_ref[...]) * (C_HEAD_DIM ** -0.5 * LOG2E)
    kn = halfnorm(k_ref[...], kg_ref[...])
    qb = qn.astype(jnp.bfloat16)
    kb = kn.astype(jnp.bfloat16)
    q2_ref[0] = jnp.where(lo, qb, 0.0).astype(jnp.bfloat16)
    q2_ref[1] = jnp.where(lo, 0.0, qb).astype(jnp.bfloat16)
    kt_ref[...] = kn.T.astype(jnp.bfloat16)
    vb_ref[...] = v_ref[...].astype(jnp.bfloat16)
    q_lo, q_hi = max_sq_norms(qb)
    k_lo, k_hi = max_sq_norms(kb)
    sub = lax.broadcasted_iota(jnp.int32, st_ref.shape, 0)
    st_ref[...] = jnp.where(sub == 0, q_lo, jnp.where(sub == 1, q_hi, jnp.where(sub == 2, k_lo,
                            jnp.where(sub == 3, k_hi, 0.0))))


def attn_prep(proj3, q_gain, k_gain, *, tl=512):
    bsz, l, _ = proj3.shape
    hw = 2 * C_HEAD_DIM
    gq = jnp.tile(q_gain.astype(jnp.float32), 2).reshape(1, hw)
    gk = jnp.tile(k_gain.astype(jnp.float32), 2).reshape(1, hw)
    vec = pl.BlockSpec((1, hw), lambda b, h, i: (0, 0))
    return pl.pallas_call(
        _attn_prep_kernel,
        grid=(bsz, C_HEADS, l // tl),
        in_specs=[pl.BlockSpec((None, tl, hw), lambda b, h, i: (b, i, h)),
                  pl.BlockSpec((None, tl, hw), lambda b, h, i: (b, i, C_HEADS + h)),
                  pl.BlockSpec((None, tl, hw), lambda b, h, i: (b, i, 2 * C_HEADS + h)),
                  vec, vec],
        out_specs=[pl.BlockSpec((None, None, 2, tl, hw), lambda b, h, i: (b, h, 0, i, 0)),
                   pl.BlockSpec((None, None, hw, tl), lambda b, h, i: (b, h, 0, i)),
                   pl.BlockSpec((None, None, tl, hw), lambda b, h, i: (b, h, i, 0)),
                   pl.BlockSpec((None, None, None, SUBLANES, hw), lambda b, h, i: (b, h, i, 0, 0))],
        out_shape=[jax.ShapeDtypeStruct((bsz, C_HEADS, 2, l, hw), jnp.bfloat16),
                   jax.ShapeDtypeStruct((bsz, C_HEADS, hw, l), jnp.bfloat16),
                   jax.ShapeDtypeStruct((bsz, C_HEADS, l, hw), jnp.bfloat16),
                   jax.ShapeDtypeStruct((bsz, C_HEADS, l // tl, SUBLANES, hw), jnp.float32)],
        compiler_params=pltpu.CompilerParams(dimension_semantics=("parallel", "parallel", "parallel"),
                                             vmem_limit_bytes=VMEM_LIMIT_BYTES),
        name="attn_prep",
    )(proj3, proj3, proj3, gq, gk)


ATT_ROWS = 64
ATT_SAFE_GAP = 100.0


def _attn_kernel(lam_ref, kmax_ref, bmax_ref, q2_ref, kt_ref, v_ref, bias_ref, g_ref, o_ref,
                 m_sc, l_sc, acc_sc, s_sc, p_sc, a_sc, *, t, nk, out_scale, bounded):
    f32 = jnp.float32
    b = pl.program_id(0)
    h = pl.program_id(1)
    qi = pl.program_id(2)
    q2 = q2_ref[...].reshape(2 * t, 2 * C_HEAD_DIM)
    r = ATT_ROWS
    hw = 2 * C_HEAD_DIM
    if bounded:
        q2f = q2.astype(f32)
        nq = jnp.sqrt(jnp.sum(q2f * q2f, axis=-1, keepdims=True))
        row = lax.broadcasted_iota(jnp.int32, nq.shape, 0)
        kc = jnp.where(row < t, kmax_ref[(b * C_HEADS + h) * 2], kmax_ref[(b * C_HEADS + h) * 2 + 1])
        m_sc[...] = jnp.broadcast_to(nq * kc + bmax_ref[h], m_sc.shape)
    else:
        m_sc[...] = jnp.full(m_sc.shape, -jnp.inf, f32)
    l_sc[...] = jnp.zeros_like(l_sc)
    acc_sc[...] = jnp.zeros_like(acc_sc)

    def body(ki, carry):
        off = pl.multiple_of(ki * t, t)
        bidx = jnp.clip(ki - qi, -2, 2) + 2
        s_sc[...] = jnp.dot(q2, kt_ref[:, pl.ds(off, t)], preferred_element_type=f32)
        for g in range(2 * t // r):
            rows = slice(g * r, (g + 1) * r)
            brow = (g * r) % t
            s = s_sc[rows, :] + bias_ref[bidx, brow:brow + r, :]
            m = m_sc[rows, :]
            if not bounded:
                m_prev = m
                m = jnp.maximum(m_prev, jnp.max(s, axis=-1, keepdims=True))
                alpha = jnp.exp2(m_prev - m)
                m_sc[rows, :] = m
                a_sc[rows, :] = alpha
            ps = [jnp.exp2(s[:, j * hw:(j + 1) * hw] - m) for j in range(t // hw)]
            psum = jnp.sum(sum(ps), axis=-1, keepdims=True)
            l_sc[rows, :] = (l_sc[rows, :] if bounded else alpha * l_sc[rows, :]) + psum
            for j in range(t // hw):
                p_sc[rows, j * hw:(j + 1) * hw] = ps[j].astype(jnp.bfloat16)
        pv = jnp.dot(p_sc[...], v_ref[pl.ds(off, t), :], preferred_element_type=f32)
        acc_sc[...] = (acc_sc[...] if bounded else a_sc[...] * acc_sc[...]) + pv
        return carry

    lax.fori_loop(0, nk, body, 0)
    a = acc_sc[...] / l_sc[...]
    o = a[:t] - lam_ref[0] * a[t:]
    y = o * lax.rsqrt(jnp.mean(o * o, axis=-1, keepdims=True) + EPS)
    o_ref[...] = y * g_ref[...] * out_scale


def diff_attention(proj3, q_gain, k_gain, lam, out_gain, bias5, layer_idx):
    f32 = jnp.float32
    bsz, l, _ = proj3.shape
    t = ATT_T
    hw = 2 * C_HEAD_DIM
    lam_init = 0.8 - 0.6 * math.exp(-0.3 * layer_idx)
    lam_f = lam.astype(f32)
    lam_full = jnp.exp(jnp.sum(lam_f[0] * lam_f[1])) - jnp.exp(jnp.sum(lam_f[2] * lam_f[3])) + lam_init
    q2, kt, vb, stats = attn_prep(proj3, q_gain, k_gain)
    norms = jnp.sqrt(jnp.max(stats[..., 0:4, 0], axis=2)) * (1.0 + 1e-3)
    qmax, kmax = norms[..., 0:2], norms[..., 2:4]
    bmax = jnp.max(bias5, axis=(1, 2, 3))
    bmin = jnp.min(bias5, axis=(1, 2, 3))
    gap = 2.0 * qmax * kmax + (bmax - bmin)[None, :, None]
    smem = pl.BlockSpec(memory_space=pltpu.SMEM)

    def run(bounded):
        return pl.pallas_call(
            functools.partial(_attn_kernel, t=t, nk=l // t, out_scale=1.0 - lam_init, bounded=bounded),
            grid=(bsz, C_HEADS, l // t),
            in_specs=[smem, smem, smem,
                      pl.BlockSpec((None, None, 2, t, hw), lambda b, h, i: (b, h, 0, i, 0)),
                      pl.BlockSpec((None, None, hw, l), lambda b, h, i: (b, h, 0, 0)),
                      pl.BlockSpec((None, None, l, hw), lambda b, h, i: (b, h, 0, 0)),
                      pl.BlockSpec((None, 5, t, t), lambda b, h, i: (h, 0, 0, 0)),
                      pl.BlockSpec((1, hw), lambda b, h, i: (0, 0))],
            out_specs=pl.BlockSpec((None, t, hw), lambda b, h, i: (b, i, h)),
            out_shape=jax.ShapeDtypeStruct((bsz, l, C_WIDTH), f32),
            scratch_shapes=[pltpu.VMEM((2 * t, hw), f32), pltpu.VMEM((2 * t, hw), f32), pltpu.VMEM((2 * t, hw), f32),
                            pltpu.VMEM((2 * t, t), f32), pltpu.VMEM((2 * t, t), jnp.bfloat16),
                            pltpu.VMEM((2 * t, hw), f32)],
            compiler_params=pltpu.CompilerParams(dimension_semantics=("parallel", "parallel", "arbitrary"),
                                                 vmem_limit_bytes=VMEM_LIMIT_BYTES),
            name="diff_attention_bounded" if bounded else "diff_attention_online",
        )(lam_full.reshape(1), kmax.reshape(-1), bmax, q2, kt, vb, bias5, out_gain.reshape(1, hw).astype(f32))

    return lax.cond(jnp.all(gap < ATT_SAFE_GAP), lambda: run(True), lambda: run(False))


GDN_TT = 512
GDN_HEADS_PER_STEP = 4
LANES = 128
OD_QKV_BLOCK = 3 * C_WIDTH // LANES
OD_GATE_BLOCK = OD_QKV_BLOCK + 3 * D_WIDTH // LANES
OD_AB_BLOCK = OD_GATE_BLOCK + D_WIDTH // LANES
OD_COLS = 3840


def _gdn_prep_kernel(prev_ref, cur_ref, next_ref, w_ref, o_ref, *, tl, nl):
    i = pl.program_id(1)
    part = pl.program_id(2)
    prev = jnp.where(i > 0, prev_ref[...], 0.0)
    nxt = jnp.where(i < nl - 1, next_ref[...], 0.0)
    ext = jnp.concatenate([prev, cur_ref[...], nxt], axis=0)
    halo = prev.shape[0]
    acc = None
    for j in range(CONV_WIDTH):
        start = halo - CONV_WIDTH // 2 + j
        term = w_ref[j:j + 1, :] * ext[start:start + tl, :]
        acc = term if acc is None else acc + term
    y = acc * jax.nn.sigmoid(acc)
    scale = jnp.where(part == 0, D_HEAD_DIM ** -0.5, 1.0)
    heads = []
    for h in range(D_HEADS):
        yh = y[:, h * LANES:(h + 1) * LANES]
        heads.append(yh * (lax.rsqrt(jnp.sum(yh * yh, axis=-1, keepdims=True) + EPS) * scale))
    o_ref[...] = jnp.where(part < 2, jnp.concatenate(heads, axis=1), y)


def gdn_prep(proj3, conv_w, *, tl=512):
    bsz, l, _ = proj3.shape
    halo = SUBLANES
    nl = l // tl
    blk0 = OD_QKV_BLOCK * LANES // D_WIDTH
    return pl.pallas_call(
        functools.partial(_gdn_prep_kernel, tl=tl, nl=nl),
        grid=(bsz, nl, 3),
        in_specs=[pl.BlockSpec((None, halo, D_WIDTH), lambda b, i, p: (b, jnp.maximum(i * (tl // halo) - 1, 0), blk0 + p)),
                  pl.BlockSpec((None, tl, D_WIDTH), lambda b, i, p: (b, i, blk0 + p)),
                  pl.BlockSpec((None, halo, D_WIDTH),
                               lambda b, i, p: (b, jnp.minimum((i + 1) * (tl // halo), l // halo - 1), blk0 + p)),
                  pl.BlockSpec((CONV_WIDTH, D_WIDTH), lambda b, i, p: (0, p))],
        out_specs=pl.BlockSpec((None, None, tl, D_WIDTH), lambda b, i, p: (p, b, i, 0)),
        out_shape=jax.ShapeDtypeStruct((3, bsz, l, D_WIDTH), jnp.float32),
        compiler_params=pltpu.CompilerParams(dimension_semantics=("parallel", "parallel", "parallel"),
                                             vmem_limit_bytes=VMEM_LIMIT_BYTES),
        name="gdn_prep",
    )(proj3, proj3, proj3, conv_w.astype(jnp.float32))


def _gdn_gates_kernel(x_ref, nega_ref, dtb_ref, o_ref):
    x = x_ref[...]
    z = x + dtb_ref[...]
    g = nega_ref[...] * (jnp.maximum(z, 0.0) + jnp.log1p(jnp.exp(-jnp.abs(z))))
    lane = lax.broadcasted_iota(jnp.int32, x.shape, 1)
    y = jnp.where(lane < 2 * D_HEADS, g, jax.nn.sigmoid(x))
    o_ref[...] = y.T[0:4 * D_HEADS, :]


def gdn_gates(proj3, a_log, dt_bias, *, tl=512):
    bsz, l, _ = proj3.shape
    pad = LANES - 2 * D_HEADS
    nega = jnp.pad(-jnp.exp(a_log.astype(jnp.float32)).reshape(1, -1), ((0, 0), (0, pad)))
    dtb = jnp.pad(dt_bias.astype(jnp.float32).reshape(1, -1), ((0, 0), (0, pad)))
    vec = pl.BlockSpec((1, LANES), lambda b, i: (0, 0))
    return pl.pallas_call(
        _gdn_gates_kernel,
        grid=(bsz, l // tl),
        in_specs=[pl.BlockSpec((None, tl, LANES), lambda b, i: (b, i, OD_AB_BLOCK)), vec, vec],
        out_specs=pl.BlockSpec((None, 4 * D_HEADS, tl), lambda b, i: (b, 0, i)),
        out_shape=jax.ShapeDtypeStruct((bsz, 4 * D_HEADS, l), jnp.float32),
        compiler_params=pltpu.CompilerParams(dimension_semantics=("parallel", "parallel"),
                                             vmem_limit_bytes=VMEM_LIMIT_BYTES),
        name="gdn_gates",
    )(proj3, nega, dtb)


def gdn_constants():
    import numpy as np
    c = CHUNK
    r = np.arange(c)[:, None]
    u = np.arange(c)[None, :]
    cum, incl, strict = [], [], []
    for direction in range(2):
        fwd = direction == 0
        cum.append(np.concatenate([(r <= u) if fwd else (r >= u), np.ones((c, c), bool)], axis=1))
        incl.append((u <= r) if fwd else (u >= r))
        strict.append((u < r) if fwd else (u > r))
    same = lambda b: (r // b) == (u // b)
    merges = [same(2 * b) & ~same(b) for b in (8, 16, 32)]
    f32 = jnp.float32
    return (jnp.asarray(np.stack(cum), jnp.bfloat16), jnp.asarray(np.stack(incl), f32),
            jnp.asarray(np.stack(strict), f32), jnp.asarray(same(8), f32), jnp.asarray(np.stack(merges), f32))


def _gdn_kernel(q_ref, k_ref, v_ref, g_ref, b_ref, cum_ref, incl_ref, strict_ref, d8_ref, mrg_ref, o_ref,
                s_sc, qd_sc, dec_sc, w_sc, u_sc, sin_sc, *, nc):
    direction = pl.program_id(2)

    @pl.when(pl.program_id(3) == 0)
    def _():
        s_sc[...] = jnp.zeros_like(s_sc)

    bf16 = jnp.bfloat16
    f32 = jnp.float32
    c = CHUNK
    hd = D_HEAD_DIM
    contract_last = (((1,), (1,)), ((), ()))
    contract_first = (((0,), (0,)), ((), ()))
    cumm = cum_ref[...]
    incl = incl_ref[...]
    strict = strict_ref[...]
    d8 = d8_ref[...]
    eye = (lax.broadcasted_iota(jnp.int32, (c, c), 0) == lax.broadcasted_iota(jnp.int32, (c, c), 1)).astype(f32)

    def mm(a, b):
        return jnp.dot(a.astype(bf16), b.astype(bf16), preferred_element_type=f32)

    def rep(x):
        return jnp.concatenate([x] * (hd // c), axis=1)

    nh = g_ref.shape[0]
    ns = range(nh * nc)
    head = [m // nc for m in ns]
    rows = [slice((m % nc) * c, (m % nc + 1) * c) for m in ns]
    cols = [slice(h * hd, (h + 1) * hd) for h in head]
    q = [q_ref[rows[m], cols[m]] for m in ns]
    k = [k_ref[rows[m], cols[m]] for m in ns]
    v = [v_ref[rows[m], cols[m]] for m in ns]
    kb = [x.astype(bf16) for x in k]
    kk = [lax.dot_general(x, x, contract_last, preferred_element_type=f32) for x in kb]
    qk = [lax.dot_general(q[n].astype(bf16), kb[n], contract_last, preferred_element_type=f32) for n in ns]
    grow = [jnp.broadcast_to(g_ref[head[m], :, rows[m]], (c, c)) for m in ns]
    ghi = [x.astype(bf16) for x in grow]
    glo = [(grow[n] - ghi[n].astype(f32)).astype(bf16) for n in ns]
    gm = [jnp.dot(ghi[n], cumm, preferred_element_type=f32) + jnp.dot(glo[n], cumm, preferred_element_type=f32)
          for n in ns]
    gam_row = [x[:, :c] for x in gm]
    tot = [x[:, c:] for x in gm]
    gam_col = [x.T for x in gam_row]
    beta_col = [jnp.broadcast_to(b_ref[head[m], :, rows[m]], (c, c)).T for m in ns]
    decay = [incl * jnp.exp(jnp.minimum(gam_col[n] - gam_row[n], 0.0)) for n in ns]
    a = [strict * beta_col[n] * kk[n] * decay[n] for n in ns]
    a0 = [x * d8 for x in a]
    n2 = [mm(x, x) for x in a0]
    n4 = [mm(x, x) for x in n2]
    t = [mm(eye - a0[n], eye + n2[n]) for n in ns]
    t = [mm(t[n], eye + n4[n]) for n in ns]
    for j in range(mrg_ref.shape[0]):
        p = [mm(a[n] * mrg_ref[j], t[n]) for n in ns]
        t = [t[n] - mm(t[n], p[n]) for n in ns]
    beta128 = [rep(x) for x in beta_col]
    egam128 = [rep(jnp.exp(x)) for x in gam_col]
    solb = [mm(t[n], jnp.concatenate([k[n] * beta128[n] * egam128[n], v[n] * beta128[n]], axis=1)).astype(bf16)
            for n in ns]
    av = [jnp.dot((qk[n] * decay[n]).astype(bf16), solb[n], preferred_element_type=f32) for n in ns]
    k_dec = [(k[n] * rep(jnp.exp(tot[n] - gam_col[n]))).astype(bf16) for n in ns]
    wu = [lax.dot_general(k_dec[n], solb[n], contract_first, preferred_element_type=f32) for n in ns]
    for n in ns:
        qd_sc[n] = (q[n] * egam128[n] - av[n][:, :hd]).astype(bf16)
        o_ref[rows[n], cols[n]] = av[n][:, hd:]
        w_sc[n] = wu[n][:, :hd].astype(bf16)
        u_sc[n] = wu[n][:, hd:]
        dec_sc[n] = rep(jnp.exp(tot[n][0:SUBLANES, :]))

    def body(ci, states):
        ce = ci + direction * (nc - 1 - 2 * ci)
        new = []
        for h in range(nh):
            m = h * nc + ce
            sb = states[h].astype(bf16)
            sin_sc[m] = sb
            new.append(states[h] * dec_sc[m][0:1] - jnp.dot(w_sc[m], sb, preferred_element_type=f32) + u_sc[m])
        return tuple(new)

    states = lax.fori_loop(0, nc, body, tuple(s_sc[h] for h in range(nh)))
    for h in range(nh):
        s_sc[h] = states[h]

    for m in ns:
        o_ref[rows[m], cols[m]] += jnp.dot(qd_sc[m], sin_sc[m], preferred_element_type=f32)


def gdn_scan(qkv, gb):
    _, bsz, l, _ = qkv.shape
    hd = D_HEAD_DIM
    tt = min(GDN_TT, l)
    nt = l // tt
    nc = tt // CHUNK
    assert l % tt == 0 and tt % CHUNK == 0
    consts = gdn_constants()
    gb4 = gb.reshape(bsz, 4 * D_HEADS, 1, l)
    nh = GDN_HEADS_PER_STEP
    assert D_HEADS % nh == 0
    tidx = lambda d, i: i + d * (nt - 1 - 2 * i)
    qkv_spec = lambda p: pl.BlockSpec((None, None, tt, nh * hd), lambda b, h, d, i: (p, b, tidx(d, i), h))
    row_spec = lambda off: pl.BlockSpec((None, nh, 1, tt),
                                        lambda b, h, d, i: (b, (off + d * D_HEADS) // nh + h, 0, tidx(d, i)))
    per_dir = lambda a: pl.BlockSpec((None,) + a.shape[1:], lambda b, h, d, i: (d,) + (0,) * (a.ndim - 1))
    whole = lambda a: pl.BlockSpec(a.shape, lambda b, h, d, i: (0,) * a.ndim)
    return pl.pallas_call(
        functools.partial(_gdn_kernel, nc=nc),
        grid=(bsz, D_HEADS // nh, 2, nt),
        in_specs=[qkv_spec(0), qkv_spec(1), qkv_spec(2), row_spec(0), row_spec(2 * D_HEADS),
                  per_dir(consts[0]), per_dir(consts[1]), per_dir(consts[2]), whole(consts[3]), whole(consts[4])],
        out_specs=pl.BlockSpec((None, None, tt, nh * hd), lambda b, h, d, i: (d, b, tidx(d, i), h)),
        out_shape=jax.ShapeDtypeStruct((2, bsz, l, D_WIDTH), jnp.float32),
        scratch_shapes=[pltpu.VMEM((nh, hd, hd), jnp.float32),
                        pltpu.VMEM((nh * nc, CHUNK, hd), jnp.bfloat16),
                        pltpu.VMEM((nh * nc, SUBLANES, hd), jnp.float32),
                        pltpu.VMEM((nh * nc, hd, hd), jnp.bfloat16),
                        pltpu.VMEM((nh * nc, hd, hd), jnp.float32),
                        pltpu.VMEM((nh * nc, hd, hd), jnp.bfloat16)],
        compiler_params=pltpu.CompilerParams(dimension_semantics=("parallel", "parallel", "parallel", "arbitrary"),
                                             vmem_limit_bytes=VMEM_LIMIT_BYTES),
        name="gdn_scan",
    )(qkv, qkv, qkv, gb4, gb4, *consts)


def gated_deltanet(proj3, conv_w, a_log, dt_bias, out_gain):
    o2 = gdn_scan(gdn_prep(proj3, conv_w), gdn_gates(proj3, a_log, dt_bias))
    return bidir_finalize(o2, proj3, OD_GATE_BLOCK, out_gain, name="gdn_finalize")


MOE_TT = 512
MOE_SUB = 128
MOE_ALIGN = 64
MOE_ROWS = 256
MOE_SLAB = 256
MOE_VMEM_LIMIT_BYTES = 56 * 1024 * 1024


def _router_kernel(x_ref, g_ref, wr_ref, h_ref, aff_ref):
    x = x_ref[...]
    h = (x * lax.rsqrt(jnp.mean(x * x, axis=-1, keepdims=True) + EPS) * g_ref[...]).astype(jnp.bfloat16)
    h_ref[...] = h
    logits = jnp.dot(h, wr_ref[...], preferred_element_type=jnp.float32)
    lane = lax.broadcasted_iota(jnp.int32, logits.shape, 1)
    logits = jnp.where(lane < N_EXPERTS, logits, -jnp.inf)
    p = jnp.exp(logits - jnp.max(logits, axis=-1, keepdims=True))
    aff = p / jnp.sum(p, axis=-1, keepdims=True)
    aff_ref[...] = aff.T[0:N_EXPERTS, :]


def moe_route(x, gain, w_router, *, tm=MOE_TT):
    bsz, l, d = x.shape
    wr = jnp.pad(w_router.astype(jnp.bfloat16), ((0, 0), (0, LANES - N_EXPERTS)))
    return pl.pallas_call(
        _router_kernel,
        grid=(bsz, l // tm),
        in_specs=[pl.BlockSpec((None, tm, d), lambda b, i: (b, i, 0)),
                  pl.BlockSpec((1, d), lambda b, i: (0, 0)),
                  pl.BlockSpec((d, LANES), lambda b, i: (0, 0))],
        out_specs=[pl.BlockSpec((None, tm, d), lambda b, i: (b, i, 0)),
                   pl.BlockSpec((None, N_EXPERTS, tm), lambda b, i: (b, 0, i))],
        out_shape=[jax.ShapeDtypeStruct((bsz, l, d), jnp.bfloat16),
                   jax.ShapeDtypeStruct((bsz, N_EXPERTS, l), jnp.float32)],
        compiler_params=pltpu.CompilerParams(dimension_semantics=("parallel", "parallel"),
                                             vmem_limit_bytes=VMEM_LIMIT_BYTES),
        name="moe_router",
    )(x, gain.reshape(1, d).astype(jnp.float32), wr)


def _select_kernel(aff_ref, pre_ref, smap_ref, gate_ref, cnt_ref, *, cap, tt):
    f32 = jnp.float32
    bf16 = jnp.bfloat16
    aff = aff_ref[...]
    e, l = aff.shape
    nl = l // LANES
    tiles = [slice(j * LANES, (j + 1) * LANES) for j in range(nl)]
    bits = pltpu.bitcast(aff, jnp.int32)
    bt = [bits[:, s] for s in tiles]

    def lane_total(x):
        return jnp.broadcast_to(jnp.sum(x, axis=-1, keepdims=True), (e, LANES))

    def search(i, thr):
        cand = thr | jnp.left_shift(jnp.int32(1), 30 - i)
        acc = jnp.zeros((e, LANES), jnp.int32)
        for x in bt:
            acc = acc + (x >= cand).astype(jnp.int32)
        return jnp.where(lane_total(acc) >= cap, cand, thr)

    thr = lax.fori_loop(0, 31, search, jnp.zeros((e, LANES), jnp.int32))
    gt = [x > thr for x in bt]
    eq = [x == thr for x in bt]
    acc = jnp.zeros((e, LANES), jnp.int32)
    for x in gt:
        acc = acc + x.astype(jnp.int32)
    need = (cap - lane_total(acc)).astype(f32)

    pre = pre_ref[...]

    def prefix(flags):
        outs = [jnp.dot(jnp.where(x, 1.0, 0.0).astype(bf16), pre, preferred_element_type=f32) for x in flags]
        carry = jnp.zeros((e, LANES), f32)
        res = []
        for o in outs:
            res.append(o[:, :LANES] + carry)
            carry = carry + o[:, LANES:]
        return res, [o[:, LANES:] for o in outs]

    rank_eq, _ = prefix(eq)
    sel = [jnp.logical_or(gt[j], jnp.logical_and(eq[j], rank_eq[j] < need)) for j in range(nl)]
    pos, totals = prefix(sel)
    lane = lax.broadcasted_iota(jnp.int32, (e, LANES), 1)
    cnt = jnp.zeros((e, LANES), f32)
    per = tt // LANES
    for j in range(nl):
        smap_ref[:, tiles[j]] = jnp.where(sel[j], pos[j], -1.0)
        gate_ref[:, tiles[j]] = jnp.where(sel[j], aff[:, tiles[j]], 0.0)
        cnt = cnt + jnp.where(lane == j // per, totals[j], 0.0)
    cnt_ref[...] = cnt


def moe_select(aff, cap, *, tt=MOE_TT):
    import numpy as np
    bsz, e, l = aff.shape
    assert l // tt <= LANES
    i = np.arange(LANES)
    pre = np.concatenate([i[:, None] < i[None, :], np.ones((LANES, LANES), bool)], axis=1)
    row = pl.BlockSpec((None, e, l), lambda b: (b, 0, 0))
    return pl.pallas_call(
        functools.partial(_select_kernel, cap=cap, tt=tt),
        grid=(bsz,),
        in_specs=[row, pl.BlockSpec((LANES, 2 * LANES), lambda b: (0, 0))],
        out_specs=[row, row, pl.BlockSpec((None, e, LANES), lambda b: (b, 0, 0))],
        out_shape=[jax.ShapeDtypeStruct((bsz, e, l), jnp.float32), jax.ShapeDtypeStruct((bsz, e, l), jnp.float32),
                   jax.ShapeDtypeStruct((bsz, e, LANES), jnp.float32)],
        compiler_params=pltpu.CompilerParams(dimension_semantics=("parallel",),
                                             vmem_limit_bytes=VMEM_LIMIT_BYTES),
        name="moe_select",
    )(aff, jnp.asarray(pre, jnp.bfloat16))


def _slot_one_hot(pos, base, rows, n):
    slot = (base + lax.broadcasted_iota(jnp.int32, (rows, n), 0)).astype(jnp.float32)
    return jnp.where(pos == slot, 1.0, 0.0).astype(jnp.bfloat16)


def _expert_kernel(cs_ref, h_ref, smap_ref, gate_ref, wg32_ref, wu32_ref, wd32_ref, o_ref,
                   xs_sc, gs_sc, wg_ref, wu_ref, wd_ref, *, nj, cap):
    e = pl.program_id(0)
    b = pl.program_id(1)
    j = pl.program_id(2)
    f32 = jnp.float32
    bf16 = jnp.bfloat16

    @pl.when(jnp.logical_and(b == 0, j == 0))
    def _():
        wg_ref[...] = wg32_ref[...].astype(bf16)
        wu_ref[...] = wu32_ref[...].astype(bf16)
        wd_ref[...] = wd32_ref[...].astype(bf16)

    @pl.when(j == 0)
    def _():
        xs_sc[...] = jnp.zeros_like(xs_sc)
        gs_sc[...] = jnp.zeros_like(gs_sc)

    base = (b * N_EXPERTS + e) * (nj + 1) + j
    c0 = cs_ref[base]
    c1 = cs_ref[base + 1]
    pos = smap_ref[...]
    gate = gate_ref[...]
    hb = h_ref[...]
    tt = hb.shape[0]

    def gather(r0, rows):
        oh = _slot_one_hot(pos, r0, rows, tt)
        xs_sc[pl.ds(r0, rows), :] += jnp.dot(oh, hb, preferred_element_type=f32)
        g = jnp.sum(oh.astype(f32) * gate, axis=-1, keepdims=True)
        gs_sc[pl.ds(r0, rows), :] += jnp.broadcast_to(g, (rows, LANES))

    win = min(MOE_SUB, cap)
    w0 = pl.multiple_of(jnp.minimum(c0 // MOE_ALIGN * MOE_ALIGN, cap - win), MOE_ALIGN)
    gather(w0, win)

    def rest(st, carry):
        gather(pl.multiple_of(st * MOE_ALIGN, MOE_ALIGN), MOE_ALIGN)
        return carry

    lax.fori_loop((w0 + win) // MOE_ALIGN, (c1 + MOE_ALIGN - 1) // MOE_ALIGN, rest, 0)

    @pl.when(j == nj - 1)
    def _():
        rows_per = min(MOE_ROWS, cap)
        for r in range(cap // rows_per):
            rows = slice(r * rows_per, (r + 1) * rows_per)
            xb = xs_sc[rows, :].astype(bf16)
            g = jnp.dot(xb, wg_ref[...], preferred_element_type=f32)
            u = jnp.dot(xb, wu_ref[...], preferred_element_type=f32)
            hid = (g * jax.nn.sigmoid(g) * u).astype(bf16)
            out = jnp.dot(hid, wd_ref[...], preferred_element_type=f32)
            scale = jnp.concatenate([gs_sc[rows, :]] * (out.shape[1] // LANES), axis=1)
            o_ref[rows, :] = (out * scale).astype(bf16)


def moe_experts(hb, smap, gate, cs, w_gate, w_up, w_down, layer, cap, *, tt=MOE_TT):
    bsz, l, d = hb.shape
    _, e, _, ff = w_gate.shape
    nj = l // tt
    smap4 = smap.reshape(bsz, e, 1, l)
    gate4 = gate.reshape(bsz, e, 1, l)
    tok = pl.BlockSpec((None, None, 1, tt), lambda ei, b, j, cs_ref: (b, ei, 0, j))
    once = pl.Buffered(1)
    grid_spec = pltpu.PrefetchScalarGridSpec(
        num_scalar_prefetch=1,
        grid=(e, bsz, nj),
        in_specs=[pl.BlockSpec((None, tt, d), lambda ei, b, j, cs_ref: (b, j, 0)), tok, tok,
                  pl.BlockSpec((None, None, d, ff), lambda ei, b, j, cs_ref: (layer, ei, 0, 0), pipeline_mode=once),
                  pl.BlockSpec((None, None, d, ff), lambda ei, b, j, cs_ref: (layer, ei, 0, 0), pipeline_mode=once),
                  pl.BlockSpec((None, None, ff, d), lambda ei, b, j, cs_ref: (layer, ei, 0, 0), pipeline_mode=once)],
        out_specs=pl.BlockSpec((None, None, cap, d), lambda ei, b, j, cs_ref: (b, ei, 0, 0)),
        scratch_shapes=[pltpu.VMEM((cap, d), jnp.float32), pltpu.VMEM((cap, LANES), jnp.float32),
                        pltpu.VMEM((d, ff), jnp.bfloat16), pltpu.VMEM((d, ff), jnp.bfloat16),
                        pltpu.VMEM((ff, d), jnp.bfloat16)])
    return pl.pallas_call(
        functools.partial(_expert_kernel, nj=nj, cap=cap),
        grid_spec=grid_spec,
        out_shape=jax.ShapeDtypeStruct((bsz, e, cap, d), jnp.bfloat16),
        compiler_params=pltpu.CompilerParams(dimension_semantics=("parallel", "arbitrary", "arbitrary"),
                                             vmem_limit_bytes=MOE_VMEM_LIMIT_BYTES),
        name="moe_experts",
    )(cs, hb, smap4, gate4, w_gate, w_up, w_down)


def _combine_kernel(cs_ref, x_ref, smap_ref, ow_ref, y_ref, *, nj, tt):
    b = pl.program_id(0)
    e = pl.program_id(2)

    @pl.when(e == 0)
    def _():
        y_ref[...] = x_ref[...]

    contract_first = (((0,), (0,)), ((), ()))
    cap = ow_ref.shape[0]
    win = min(2 * MOE_SUB, cap)
    base = (b * N_EXPERTS + e) * (nj + 1)
    cols = [slice(j * tt, (j + 1) * tt) for j in range(nj)]
    pos = [smap_ref[:, c] for c in cols]
    r0 = [pl.multiple_of(jnp.minimum(cs_ref[base + j] // MOE_SUB * MOE_SUB, cap - win), MOE_SUB) for j in range(nj)]
    oh = [_slot_one_hot(pos[j], r0[j], win, tt) for j in range(nj)]
    add = [lax.dot_general(oh[j], ow_ref[pl.ds(r0[j], win), :], contract_first, preferred_element_type=jnp.float32)
           for j in range(nj)]
    for j in range(nj):
        y_ref[cols[j], :] += add[j]

    for j in range(nj):
        def scatter(st, carry):
            s0 = pl.multiple_of(st * MOE_SUB, MOE_SUB)
            y_ref[cols[j], :] += lax.dot_general(_slot_one_hot(pos[j], s0, MOE_SUB, tt),
                                                 ow_ref[pl.ds(s0, MOE_SUB), :], contract_first,
                                                 preferred_element_type=jnp.float32)
            return carry

        lax.fori_loop((r0[j] + win) // MOE_SUB, (cs_ref[base + j + 1] + MOE_SUB - 1) // MOE_SUB, scatter, 0)


def moe_combine(x, smap, outw, cs, *, tt=MOE_TT):
    bsz, l, d = x.shape
    e, cap = outw.shape[1:3]
    nj = l // tt
    smap4 = smap.reshape(bsz, e, 1, l)
    grid_spec = pltpu.PrefetchScalarGridSpec(
        num_scalar_prefetch=1,
        grid=(bsz, d // MOE_SLAB, e),
        in_specs=[pl.BlockSpec((None, l, MOE_SLAB), lambda b, s, ei, cs_ref: (b, 0, s)),
                  pl.BlockSpec((None, None, 1, l), lambda b, s, ei, cs_ref: (b, ei, 0, 0)),
                  pl.BlockSpec((None, None, cap, MOE_SLAB), lambda b, s, ei, cs_ref: (b, ei, 0, s))],
        out_specs=pl.BlockSpec((None, l, MOE_SLAB), lambda b, s, ei, cs_ref: (b, 0, s)))
    return pl.pallas_call(
        functools.partial(_combine_kernel, nj=nj, tt=tt),
        grid_spec=grid_spec,
        out_shape=jax.ShapeDtypeStruct((bsz, l, d), jnp.float32),
        compiler_params=pltpu.CompilerParams(dimension_semantics=("parallel", "parallel", "arbitrary"),
                                             vmem_limit_bytes=MOE_VMEM_LIMIT_BYTES),
        name="moe_combine",
    )(cs, x, smap4, outw)


def ec_moe_layer(x, gain, w_router, w_gate, w_up, w_down, layer):
    bsz, l, d = x.shape
    cap = EC_CAPACITY_FACTOR * l // N_EXPERTS
    tt = min(MOE_TT, l)
    nj = l // tt
    hb, aff = moe_route(x, gain, w_router, tm=tt)
    smap, gate, cnt = moe_select(aff, cap, tt=tt)
    cs = jnp.concatenate([jnp.zeros((bsz, N_EXPERTS, 1), jnp.float32), jnp.cumsum(cnt[..., :nj], axis=-1)], axis=-1)
    cs = cs.astype(jnp.int32).reshape(-1)
    outw = moe_experts(hb, smap, gate, cs, w_gate, w_up, w_down, layer, cap, tt=tt)
    return moe_combine(x, smap, outw, cs, tt=tt)


def kernel(x, mix_norm, ffn_norm, ev_w_in, ev_w_out, a_lb_logits, a_out_norm, s5_lambda_re, s5_lambda_im, s5_log_step, s5_b_re, s5_b_im, s5_c_re, s5_c_im, s5_d, s5_glu_w, s5_glu_b, od_w_in, od_w_out, c_q_norm, c_k_norm, c_lambda, c_out_norm, rel_bias, d_conv_w, d_a_log, d_dt_bias, d_out_norm, moe_router, moe_w_gate, moe_w_up, moe_w_down):
    bsz, l, d = x.shape
    p = jax.nn.softmax(a_lb_logits.astype(jnp.float32), axis=0)
    cum = jnp.cumsum(p, axis=0)
    lower_bounds = cum - cum[0:1]
    bias5 = rel_bias_tiles(rel_bias, ATT_T)
    for layer in range(DEPTH):
        j = layer // 2
        if layer % 2 == 0:
            proj, u_tb = norm_matmul(x, mix_norm[layer], ev_w_in[j], tail=B_WIDTH)
            o_a = hgrn2_mixer(proj, lower_bounds[j], a_out_norm[j])
            o_b = s5_mixer_tb(u_tb.reshape(l * bsz, B_WIDTH), bsz, s5_lambda_re[j], s5_lambda_im[j], s5_log_step[j],
                              s5_b_re[j], s5_b_im[j], s5_c_re[j], s5_c_im[j], s5_d[j], s5_glu_w[j], s5_glu_b[j])
            x = matmul_residual(o_a, o_b.reshape(l, bsz * B_WIDTH), ev_w_out[j], x, a2_time_major=True)
        else:
            o2 = 3 * C_WIDTH + 3 * D_WIDTH
            o4 = o2 + 4 * D_HEADS
            w = od_w_in[j]
            w_in = jnp.concatenate([w[:, :o2], w[:, o4:], w[:, o2:o4],
                                    jnp.zeros((d, OD_COLS - w.shape[1]), w.dtype)], axis=1)
            proj = norm_matmul(x, mix_norm[layer], w_in)
            o_c = diff_attention(proj, c_q_norm[j], c_k_norm[j], c_lambda[j], c_out_norm[j], bias5, layer)
            o_d = gated_deltanet(proj, d_conv_w[j], d_a_log[j], d_dt_bias[j], d_out_norm[j])
            x = matmul_residual(o_c, o_d, od_w_out[j], x)
        x = ec_moe_layer(x, ffn_norm[layer], moe_router[layer], moe_w_gate, moe_w_up, moe_w_down, layer)
    return x
```

```python
import functools
import math

import jax
import jax.numpy as jnp
from jax import lax
from jax.experimental import pallas as pl
from jax.experimental.pallas import tpu as pltpu

D_MODEL = 1024
DEPTH = 4
MIX_WIDTH = D_MODEL
A_WIDTH = MIX_WIDTH // 2
A_HEAD_DIM = 128
A_HEADS = A_WIDTH // A_HEAD_DIM
B_WIDTH = MIX_WIDTH - A_WIDTH
S5_GROUP = 16
S5_GROUPS = B_WIDTH // S5_GROUP
S5_STATE = 64
C_WIDTH = MIX_WIDTH // 2
C_HEAD_DIM = 64
C_HEADS = C_WIDTH // (2 * C_HEAD_DIM)
C_V_DIM = 2 * C_HEAD_DIM
D_WIDTH = MIX_WIDTH - C_WIDTH
D_HEAD_DIM = 128
D_HEADS = D_WIDTH // D_HEAD_DIM
CONV_WIDTH = 5
N_EXPERTS = 16
EXPERT_FF = 2 * D_MODEL
EC_CAPACITY_FACTOR = 2
REL_BUCKETS = 32
REL_MAX_DIST = 128
CHUNK = 64
Q_BLOCK = 128
EPS = 1e-6

VMEM_LIMIT_BYTES = 48 * 1024 * 1024


PROJ_TM = 512
PROJ_COLS = 512


def _norm_matmul_kernel(x_ref, g_ref, w_ref, o_ref, *tail_ref, main):
    x = x_ref[...]
    y = (x * lax.rsqrt(jnp.mean(x * x, axis=-1, keepdims=True) + EPS) * g_ref[...]).astype(jnp.bfloat16)
    for c0 in range(0, main, PROJ_COLS):
        c1 = min(c0 + PROJ_COLS, main)
        o_ref[:, c0:c1] = jnp.dot(y, w_ref[:, c0:c1], preferred_element_type=jnp.float32)
    if tail_ref:
        tail_ref[0][...] = jnp.dot(y, w_ref[:, main:], preferred_element_type=jnp.float32)


def norm_matmul(x, gain, w, *, tail=0, tm=PROJ_TM):
    bsz, l, k = x.shape
    m = w.shape[1]
    main = m - tail
    tm = min(tm, l)
    out_shape = [jax.ShapeDtypeStruct((bsz, l, main), jnp.float32)]
    out_specs = [pl.BlockSpec((None, tm, main), lambda b, i: (b, i, 0))]
    if tail:
        out_shape.append(jax.ShapeDtypeStruct((l, bsz * tail), jnp.float32))
        out_specs.append(pl.BlockSpec((tm, tail), lambda b, i: (i, b)))
    outs = pl.pallas_call(
        functools.partial(_norm_matmul_kernel, main=main),
        grid=(bsz, l // tm),
        in_specs=[pl.BlockSpec((None, tm, k), lambda b, i: (b, i, 0)),
                  pl.BlockSpec((1, k), lambda b, i: (0, 0)),
                  pl.BlockSpec((k, m), lambda b, i: (0, 0), pipeline_mode=pl.Buffered(1))],
        out_specs=out_specs,
        out_shape=out_shape,
        compiler_params=pltpu.CompilerParams(dimension_semantics=("parallel", "parallel"),
                                             vmem_limit_bytes=VMEM_LIMIT_BYTES),
        name="norm_matmul",
    )(x, gain.reshape(1, k).astype(jnp.float32), w.astype(jnp.bfloat16))
    return outs if tail else outs[0]


def _mixer_out_kernel(of_ref, ob_ref, g_ref, gain_ref, other_ref, wb_ref, wo_ref, r_ref, o_ref):
    bf16 = jnp.bfloat16
    o = of_ref[...] + ob_ref[...]
    g = g_ref[...]
    gate = g * jax.nn.sigmoid(g)
    hd = gain_ref.shape[1]
    heads = []
    for h in range(o.shape[1] // hd):
        oh = o[:, h * hd:(h + 1) * hd]
        heads.append(oh * lax.rsqrt(jnp.mean(oh * oh, axis=-1, keepdims=True) + EPS) * gain_ref[...])
    y = (jnp.concatenate(heads, axis=1) * gate).astype(bf16)
    o_ref[...] = (r_ref[...] + jnp.dot(y, wb_ref[...], preferred_element_type=jnp.float32)
                  + jnp.dot(other_ref[...].astype(bf16), wo_ref[...], preferred_element_type=jnp.float32))


def mixer_out_proj(o2, proj3, gate_block, out_gain, other, w, res, *, bidir_first, other_time_major=False,
                   tm=PROJ_TM):
    _, bsz, l, k = o2.shape
    m = w.shape[1]
    tm = min(tm, l)
    wb = w.astype(jnp.bfloat16)
    w_bidir, w_other = (wb[:k], wb[k:]) if bidir_first else (wb[k:], wb[:k])
    other_spec = (pl.BlockSpec((tm, k), lambda b, i: (i, b)) if other_time_major
                  else pl.BlockSpec((None, tm, k), lambda b, i: (b, i, 0)))
    gb = gate_block * LANES // k
    row = pl.BlockSpec((None, tm, m), lambda b, i: (b, i, 0))
    wspec = pl.BlockSpec((k, m), lambda b, i: (0, 0))
    return pl.pallas_call(
        _mixer_out_kernel,
        grid=(bsz, l // tm),
        in_specs=[pl.BlockSpec((None, None, tm, k), lambda b, i: (0, b, i, 0)),
                  pl.BlockSpec((None, None, tm, k), lambda b, i: (1, b, i, 0)),
                  pl.BlockSpec((None, tm, k), lambda b, i: (b, i, gb)),
                  pl.BlockSpec((1, LANES), lambda b, i: (0, 0)),
                  other_spec, wspec, wspec, row],
        out_specs=row,
        out_shape=jax.ShapeDtypeStruct((bsz, l, m), jnp.float32),
        compiler_params=pltpu.CompilerParams(dimension_semantics=("parallel", "parallel"),
                                             vmem_limit_bytes=VMEM_LIMIT_BYTES),
        name="mixer_out_proj",
    )(o2, o2, proj3, out_gain.reshape(1, LANES).astype(jnp.float32), other, w_bidir, w_other, res)


HG_LEVELS = tuple(CHUNK >> (i + 1) for i in range(CHUNK.bit_length() - 1))
HG_TOT_ROWS = 8
HG_TT = 512


def hgrn2_constants():
    import numpy as np
    c = CHUNK
    r = np.arange(c)[:, None]
    u = np.arange(c)[None, :]
    stacks, masks = [], []
    for direction in range(2):
        fwd = direction == 0
        lvl_masks = []
        for m in HG_LEVELS:
            blk = r // (2 * m)
            later = (r % (2 * m)) >= m
            lvl_masks.append((blk == blk.T) & (later & ~later.T if fwd else ~later & later.T))
        stacks.append(np.concatenate([(u <= r) if fwd else (u >= r), np.ones((HG_TOT_ROWS, c), bool)], axis=0))
        masks.append(np.stack(lvl_masks))
    return (jnp.asarray(np.stack(stacks), jnp.bfloat16), jnp.asarray(np.stack(masks), jnp.float32))


def _hgrn2_kernel(q_ref, f_ref, v_ref, loglb_ref, log1mlb_ref, onemlb_ref, ast_ref, mask_ref, o_ref,
                  st_sc, qd_sc, dec_sc, upd_sc, sin_sc, *, nc):
    direction = pl.program_id(2)

    @pl.when(pl.program_id(3) == 0)
    def _():
        st_sc[...] = jnp.zeros_like(st_sc)

    bf16 = jnp.bfloat16
    f32 = jnp.float32
    c = CHUNK
    hd = A_HEAD_DIM
    dirf = direction.astype(f32)
    loglb = loglb_ref[...]
    log1mlb = log1mlb_ref[...]
    onemlb = onemlb_ref[...]
    ast = ast_ref[...]
    contract_last = (((1,), (1,)), ((), ()))
    contract_first = (((0,), (0,)), ((), ()))

    ns = range(nc)
    rows = [slice(n * c, (n + 1) * c) for n in ns]
    z = [f_ref[r, :] for r in rows]
    v = [v_ref[r, :] for r in rows]
    qr = [q_ref[r, :] for r in rows]
    q = [x * jax.nn.sigmoid(x) for x in qr]
    e = [jnp.exp(-jnp.abs(x)) for x in z]
    cc = [log1mlb + jnp.minimum(z[n], 0.0) - jnp.log1p(e[n]) for n in ns]
    lf = [jnp.maximum(loglb, x) + jnp.log1p(jnp.exp(-jnp.abs(loglb - x))) for x in cc]
    k = [onemlb * jnp.where(z[n] >= 0, e[n], 1.0) / (1.0 + e[n]) for n in ns]
    hi = [x.astype(bf16) for x in lf]
    lo = [(lf[n] - hi[n].astype(f32)).astype(bf16) for n in ns]
    d = [jnp.dot(ast, hi[n], preferred_element_type=f32) + jnp.dot(ast, lo[n], preferred_element_type=f32)
         for n in ns]
    cum = [x[0:c] for x in d]
    tot = [x[c:c + HG_TOT_ROWS] for x in d]
    ref = [cum[n] - dirf * lf[n] for n in ns]
    attn = [jnp.zeros((c, c), f32) for _ in ns]
    for li, m in enumerate(HG_LEVELS):
        nb = c // (2 * m)
        split = [jnp.broadcast_to(x.reshape(nb, 2 * m, hd)[:, m - 1:m, :], (nb, 2 * m, hd)).reshape(c, hd)
                 for x in ref]
        x = [jnp.exp(-jnp.abs(cum[n] - split[n])) for n in ns]
        s = [lax.dot_general((q[n] * x[n]).astype(bf16), (k[n] * x[n]).astype(bf16), contract_last,
                             preferred_element_type=f32) for n in ns]
        attn = [attn[n] + mask_ref[li] * s[n] for n in ns]
    vb = [x.astype(bf16) for x in v]
    intra = [jnp.dot(attn[n].astype(bf16), vb[n], preferred_element_type=f32) for n in ns]
    upd = [lax.dot_general(vb[n], (k[n] * jnp.exp(tot[n][0:1] - cum[n])).astype(bf16), contract_first,
                           preferred_element_type=f32) for n in ns]
    for n in ns:
        o_ref[rows[n], :] = intra[n] + jnp.sum(q[n] * k[n], axis=-1, keepdims=True) * v[n]
        qd_sc[n] = (q[n] * jnp.exp(cum[n])).astype(bf16)
        dec_sc[n] = jnp.exp(tot[n])
        upd_sc[n] = upd[n]

    def body(ci, st):
        ce = ci + direction * (nc - 1 - 2 * ci)
        sin_sc[ce] = st.astype(bf16)
        return st * dec_sc[ce][0:1] + upd_sc[ce]

    st_sc[...] = lax.fori_loop(0, nc, body, st_sc[...])

    for n in range(nc):
        rows = slice(n * c, (n + 1) * c)
        o_ref[rows, :] += lax.dot_general(qd_sc[n], sin_sc[n], contract_last, preferred_element_type=f32)


def hgrn2_scan(proj3, lb):
    bsz, l, _ = proj3.shape
    hd = A_HEAD_DIM
    tt = min(HG_TT, l)
    nt = l // tt
    assert l % tt == 0 and tt % CHUNK == 0
    ast, masks = hgrn2_constants()
    lb = lb.astype(jnp.float32)
    vecs = [jnp.log(lb).reshape(2, 1, A_WIDTH), jnp.log1p(-lb).reshape(2, 1, A_WIDTH), (1.0 - lb).reshape(2, 1, A_WIDTH)]
    tidx = lambda d, i: i + d * (nt - 1 - 2 * i)
    vec = pl.BlockSpec((None, 1, hd), lambda b, h, d, i: (d, 0, h))
    return pl.pallas_call(
        functools.partial(_hgrn2_kernel, nc=tt // CHUNK),
        grid=(bsz, A_HEADS, 2, nt),
        in_specs=[pl.BlockSpec((None, tt, hd), lambda b, h, d, i: (b, tidx(d, i), h)),
                  pl.BlockSpec((None, tt, hd), lambda b, h, d, i: (b, tidx(d, i), (1 + d) * A_HEADS + h)),
                  pl.BlockSpec((None, tt, hd), lambda b, h, d, i: (b, tidx(d, i), 3 * A_HEADS + h)),
                  vec, vec, vec,
                  pl.BlockSpec((None,) + ast.shape[1:], lambda b, h, d, i: (d, 0, 0)),
                  pl.BlockSpec((None,) + masks.shape[1:], lambda b, h, d, i: (d, 0, 0, 0))],
        out_specs=pl.BlockSpec((None, None, tt, hd), lambda b, h, d, i: (d, b, tidx(d, i), h)),
        out_shape=jax.ShapeDtypeStruct((2, bsz, l, A_WIDTH), jnp.float32),
        scratch_shapes=[pltpu.VMEM((hd, hd), jnp.float32),
                        pltpu.VMEM((tt // CHUNK, CHUNK, hd), jnp.bfloat16),
                        pltpu.VMEM((tt // CHUNK, HG_TOT_ROWS, hd), jnp.float32),
                        pltpu.VMEM((tt // CHUNK, hd, hd), jnp.float32),
                        pltpu.VMEM((tt // CHUNK, hd, hd), jnp.bfloat16)],
        compiler_params=pltpu.CompilerParams(dimension_semantics=("parallel", "parallel", "parallel", "arbitrary"),
                                             vmem_limit_bytes=VMEM_LIMIT_BYTES),
        name="hgrn2_scan",
    )(proj3, proj3, proj3, *vecs, ast, masks)


S5_NS = S5_GROUPS * S5_STATE
S5_TT = 64
SUBLANES = 8


def _s5_scan_kernel(u_ref, win_ref, ar_ref, ai_ref, wout_ref, y_ref, bu_sc, xs_sc, st_sc, *, bsz, tt, reverse):
    @pl.when(pl.program_id(0) == 0)
    def _():
        st_sc[...] = jnp.zeros_like(st_sc)

    ub = u_ref[...].astype(jnp.bfloat16)
    halves = 2
    uw = B_WIDTH // halves
    sw = S5_NS // halves
    for hf in range(halves):
        for part in range(2):
            sc = slice(part * S5_NS + hf * sw, part * S5_NS + (hf + 1) * sw)
            bu_sc[:, sc] = jnp.dot(ub[:, hf * uw:(hf + 1) * uw], win_ref[hf * uw:(hf + 1) * uw, sc],
                                   preferred_element_type=jnp.float32)
    ar = jnp.broadcast_to(ar_ref[...], (bsz, S5_NS))
    ai = jnp.broadcast_to(ai_ref[...], (bsz, S5_NS))
    per = SUBLANES // bsz
    ngroups = tt // per

    def body(s, carry):
        xr, xi = carry
        p = (ngroups - 1 - s) if reverse else s
        base = pl.multiple_of(p * SUBLANES, SUBLANES)
        blk = bu_sc[pl.ds(base, SUBLANES), :]
        outs_r = [None] * per
        outs_i = [None] * per
        for ph in (range(per - 1, -1, -1) if reverse else range(per)):
            br = blk[ph * bsz:(ph + 1) * bsz, :S5_NS]
            bi = blk[ph * bsz:(ph + 1) * bsz, S5_NS:]
            xr, xi = ar * xr - ai * xi + br, ar * xi + ai * xr + bi
            outs_r[ph] = xr
            outs_i[ph] = xi
        xs_sc[pl.ds(base, SUBLANES), :S5_NS] = jnp.concatenate(outs_r, axis=0)
        xs_sc[pl.ds(base, SUBLANES), S5_NS:] = jnp.concatenate(outs_i, axis=0)
        return xr, xi

    xr, xi = lax.fori_loop(0, ngroups, body, (st_sc[0], st_sc[1]))
    st_sc[0] = xr
    st_sc[1] = xi
    for hf in range(halves):
        yc = slice(hf * uw, (hf + 1) * uw)
        acc = None
        for part in range(2):
            sc = slice(part * S5_NS + hf * sw, part * S5_NS + (hf + 1) * sw)
            term = jnp.dot(xs_sc[:, sc].astype(jnp.bfloat16), wout_ref[sc, yc], preferred_element_type=jnp.float32)
            acc = term if acc is None else acc + term
        y_ref[:, yc] = acc


def s5_scan(u_tb, win, ar, ai, wout, *, bsz, reverse):
    n = u_tb.shape[0]
    rows = S5_TT * bsz
    nt = n // rows
    assert n % rows == 0 and SUBLANES % bsz == 0
    idx = (lambda i: (nt - 1 - i, 0)) if reverse else (lambda i: (i, 0))
    const = lambda i: (0, 0)
    return pl.pallas_call(
        functools.partial(_s5_scan_kernel, bsz=bsz, tt=S5_TT, reverse=reverse),
        grid=(nt,),
        in_specs=[pl.BlockSpec((rows, B_WIDTH), idx),
                  pl.BlockSpec((B_WIDTH, 2 * S5_NS), const),
                  pl.BlockSpec((1, S5_NS), const),
                  pl.BlockSpec((1, S5_NS), const),
                  pl.BlockSpec((2 * S5_NS, B_WIDTH), const)],
        out_specs=pl.BlockSpec((rows, B_WIDTH), idx),
        out_shape=jax.ShapeDtypeStruct((n, B_WIDTH), jnp.float32),
        scratch_shapes=[pltpu.VMEM((rows, 2 * S5_NS), jnp.float32),
                        pltpu.VMEM((rows, 2 * S5_NS), jnp.float32),
                        pltpu.VMEM((2, bsz, S5_NS), jnp.float32)],
        compiler_params=pltpu.CompilerParams(dimension_semantics=("arbitrary",),
                                             vmem_limit_bytes=VMEM_LIMIT_BYTES),
        name="s5_scan_bwd" if reverse else "s5_scan_fwd",
    )(u_tb, win, ar, ai, wout)


def _s5_final_kernel(u_ref, yf_ref, yb_ref, d_ref, w_ref, b_ref, o_ref):
    y = d_ref[...] * u_ref[...] + yf_ref[...] + yb_ref[...]
    y = jax.nn.gelu(y)
    z = jnp.dot(y.astype(jnp.bfloat16), w_ref[...], preferred_element_type=jnp.float32) + b_ref[...]
    o_ref[...] = y * jax.nn.sigmoid(z)


def s5_finalize(u, yf, yb, d_skip, glu_w, glu_b, *, tm=512):
    n, w = u.shape
    row = pl.BlockSpec((tm, w), lambda i: (i, 0))
    vec = pl.BlockSpec((1, w), lambda i: (0, 0))
    return pl.pallas_call(
        _s5_final_kernel,
        grid=(n // tm,),
        in_specs=[row, row, row, vec, pl.BlockSpec((w, w), lambda i: (0, 0)), vec],
        out_specs=row,
        out_shape=jax.ShapeDtypeStruct((n, w), jnp.float32),
        compiler_params=pltpu.CompilerParams(dimension_semantics=("parallel",),
                                             vmem_limit_bytes=VMEM_LIMIT_BYTES),
        name="s5_finalize",
    )(u, yf, yb, d_skip.reshape(1, w).astype(jnp.float32), glu_w.astype(jnp.bfloat16),
      glu_b.reshape(1, w).astype(jnp.float32))


def s5_direction_params(lam_re, lam_im, log_step, b_re, b_im, c_re, c_im):
    step = jnp.exp(log_step)[:, None]
    mag = jnp.exp(lam_re * step)
    abar_re = mag * jnp.cos(lam_im * step)
    abar_im = mag * jnp.sin(lam_im * step)
    den = lam_re * lam_re + lam_im * lam_im
    fr = ((abar_re - 1.0) * lam_re + abar_im * lam_im) / den
    fi = (abar_im * lam_re - (abar_re - 1.0) * lam_im) / den
    bb_re = fr[..., None] * b_re - fi[..., None] * b_im
    bb_im = fr[..., None] * b_im + fi[..., None] * b_re
    eye = jnp.eye(S5_GROUPS, dtype=jnp.float32)
    win = jnp.concatenate([jnp.einsum('gnp,gh->gphn', bb, eye).reshape(B_WIDTH, S5_NS) for bb in (bb_re, bb_im)],
                          axis=1)
    wout = jnp.concatenate([jnp.einsum('gpn,gh->hngp', c, eye).reshape(S5_NS, B_WIDTH) for c in (c_re, -c_im)],
                           axis=0)
    return (win.astype(jnp.bfloat16), abar_re.reshape(1, S5_NS), abar_im.reshape(1, S5_NS),
            wout.astype(jnp.bfloat16))


def s5_mixer_tb(u_tb, bsz, lam_re, lam_im, log_step, b_re, b_im, c_re, c_im, d_skip, glu_w, glu_b):
    f32 = jnp.float32
    ys = []
    for direction in range(2):
        prm = s5_direction_params(lam_re[direction].astype(f32), lam_im[direction].astype(f32),
                                  log_step[direction].astype(f32), b_re[direction].astype(f32),
                                  b_im[direction].astype(f32), c_re[direction].astype(f32),
                                  c_im[direction].astype(f32))
        ys.append(s5_scan(u_tb, *prm, bsz=bsz, reverse=(direction == 1)))
    return s5_finalize(u_tb, ys[0], ys[1], d_skip, glu_w, glu_b)


def t5_bucket(rel):
    half = REL_BUCKETS // 2
    max_exact = half // 2
    base = jnp.where(rel > 0, half, 0)
    n = jnp.abs(rel)
    nf = jnp.maximum(n, 1).astype(jnp.float32)
    large = max_exact + (jnp.log(nf / max_exact) / math.log(REL_MAX_DIST / max_exact)
                         * (half - max_exact)).astype(jnp.int32)
    large = jnp.minimum(large, half - 1)
    return base + jnp.where(n < max_exact, n, large)


ATT_T = 512
LOG2E = math.log2(math.e)


def rel_bias_tiles(rel_bias, t):
    assert t >= REL_MAX_DIST
    table = rel_bias.astype(jnp.float32) * LOG2E
    tiles = []
    for d in (-1, 0, 1):
        c = table[t5_bucket(d * t + jnp.arange(-(t - 1), t))]
        w = jnp.concatenate([c, c[:1]], axis=0)
        m = jnp.tile(w, (t, 1))[:t * (2 * t - 1)].reshape(t, 2 * t - 1, -1)
        tiles.append(m[:, t - 1:2 * t - 1])
    far_neg = jnp.broadcast_to(table[t5_bucket(jnp.array(-2 * t))], tiles[0].shape)
    far_pos = jnp.broadcast_to(table[t5_bucket(jnp.array(2 * t))], tiles[0].shape)
    out = jnp.stack([far_neg] + tiles + [far_pos], axis=0)
    return jnp.transpose(out, (3, 0, 1, 2))


def _attn_operands(q, k, v, qg, kg):
    f32 = jnp.float32
    bf16 = jnp.bfloat16
    lane = lax.broadcasted_iota(jnp.int32, q.shape, 1)
    lo = lane < C_HEAD_DIM

    def half_sums(sq):
        return (jnp.sum(jnp.where(lo, sq, 0.0), axis=-1, keepdims=True),
                jnp.sum(jnp.where(lo, 0.0, sq), axis=-1, keepdims=True))

    def halfnorm(x, g):
        s_lo, s_hi = half_sums(x * x)
        return x * lax.rsqrt(jnp.where(lo, s_lo, s_hi) * (1.0 / C_HEAD_DIM) + EPS) * g

    def max_sq_norms(xb):
        n_lo, n_hi = half_sums(xb.astype(f32) * xb.astype(f32))
        return jnp.max(n_lo, axis=0, keepdims=True), jnp.max(n_hi, axis=0, keepdims=True)

    qn = halfnorm(q, qg) * (C_HEAD_DIM ** -0.5 * LOG2E)
    kn = halfnorm(k, kg)
    qb = qn.astype(bf16)
    kb = kn.astype(bf16)
    q2 = (jnp.where(lo, qb, 0.0).astype(bf16), jnp.where(lo, 0.0, qb).astype(bf16))
    q_lo, q_hi = max_sq_norms(qb)
    k_lo, k_hi = max_sq_norms(kb)
    sub = lax.broadcasted_iota(jnp.int32, (SUBLANES, q.shape[1]), 0)
    stats = jnp.where(sub == 0, q_lo, jnp.where(sub == 1, q_hi, jnp.where(sub == 2, k_lo,
                      jnp.where(sub == 3, k_hi, 0.0))))
    return q2, kn.T.astype(bf16), v.astype(bf16), stats


def _odd_proj_kernel(x_ref, g_ref, w_ref, qg_ref, kg_ref, o_ref, q2_ref, kt_ref, vb_ref, st_ref):
    x = x_ref[...]
    y = (x * lax.rsqrt(jnp.mean(x * x, axis=-1, keepdims=True) + EPS) * g_ref[...]).astype(jnp.bfloat16)
    hw = 2 * C_HEAD_DIM
    q, k, v = (jnp.dot(y, w_ref[:, p * C_WIDTH:(p + 1) * C_WIDTH], preferred_element_type=jnp.float32)
               for p in range(3))
    for h in range(C_HEADS):
        cols = slice(h * hw, (h + 1) * hw)
        q2, kt, vb, stats = _attn_operands(q[:, cols], k[:, cols], v[:, cols], qg_ref[...], kg_ref[...])
        q2_ref[h, 0] = q2[0]
        q2_ref[h, 1] = q2[1]
        kt_ref[h] = kt
        vb_ref[h] = vb
        st_ref[h] = stats
    att = 3 * C_WIDTH
    for c0 in range(att, w_ref.shape[1], PROJ_COLS):
        c1 = min(c0 + PROJ_COLS, w_ref.shape[1])
        o_ref[:, c0 - att:c1 - att] = jnp.dot(y, w_ref[:, c0:c1], preferred_element_type=jnp.float32)


def odd_in_proj(x, gain, w, q_gain, k_gain, *, tm=PROJ_TM):
    bsz, l, kdim = x.shape
    m = w.shape[1]
    rest = m - 3 * C_WIDTH
    tm = min(tm, l)
    hw = 2 * C_HEAD_DIM
    gq = jnp.tile(q_gain.astype(jnp.float32), 2).reshape(1, hw)
    gk = jnp.tile(k_gain.astype(jnp.float32), 2).reshape(1, hw)
    vec = pl.BlockSpec((1, hw), lambda b, i: (0, 0))
    return pl.pallas_call(
        _odd_proj_kernel,
        grid=(bsz, l // tm),
        in_specs=[pl.BlockSpec((None, tm, kdim), lambda b, i: (b, i, 0)),
                  pl.BlockSpec((1, kdim), lambda b, i: (0, 0)),
                  pl.BlockSpec((kdim, m), lambda b, i: (0, 0), pipeline_mode=pl.Buffered(1)),
                  vec, vec],
        out_specs=[pl.BlockSpec((None, tm, rest), lambda b, i: (b, i, 0)),
                   pl.BlockSpec((None, C_HEADS, 2, tm, hw), lambda b, i: (b, 0, 0, i, 0)),
                   pl.BlockSpec((None, C_HEADS, hw, tm), lambda b, i: (b, 0, 0, i)),
                   pl.BlockSpec((None, C_HEADS, tm, hw), lambda b, i: (b, 0, i, 0)),
                   pl.BlockSpec((None, C_HEADS, None, SUBLANES, hw), lambda b, i: (b, 0, i, 0, 0))],
        out_shape=[jax.ShapeDtypeStruct((bsz, l, rest), jnp.float32),
                   jax.ShapeDtypeStruct((bsz, C_HEADS, 2, l, hw), jnp.bfloat16),
                   jax.ShapeDtypeStruct((bsz, C_HEADS, hw, l), jnp.bfloat16),
                   jax.ShapeDtypeStruct((bsz, C_HEADS, l, hw), jnp.bfloat16),
                   jax.ShapeDtypeStruct((bsz, C_HEADS, l // tm, SUBLANES, hw), jnp.float32)],
        compiler_params=pltpu.CompilerParams(dimension_semantics=("parallel", "parallel"),
                                             vmem_limit_bytes=VMEM_LIMIT_BYTES),
        name="odd_in_proj",
    )(x, gain.reshape(1, kdim).astype(jnp.float32), w.astype(jnp.bfloat16), gq, gk)


ATT_ROWS = 64
ATT_SAFE_GAP = 100.0


def _attn_kernel(lam_ref, kmax_ref, bmax_ref, q2_ref, kt_ref, v_ref, bias_ref, g_ref, o_ref,
                 m_sc, l_sc, acc_sc, s_sc, p_sc, a_sc, *, t, nk, out_scale, bounded):
    f32 = jnp.float32
    b = pl.program_id(0)
    h = pl.program_id(1)
    qi = pl.program_id(2)
    q2 = q2_ref[...].reshape(2 * t, 2 * C_HEAD_DIM)
    r = ATT_ROWS
    hw = 2 * C_HEAD_DIM
    if bounded:
        q2f = q2.astype(f32)
        nq = jnp.sqrt(jnp.sum(q2f * q2f, axis=-1, keepdims=True))
        row = lax.broadcasted_iota(jnp.int32, nq.shape, 0)
        kc = jnp.where(row < t, kmax_ref[(b * C_HEADS + h) * 2], kmax_ref[(b * C_HEADS + h) * 2 + 1])
        m_sc[...] = jnp.broadcast_to(nq * kc + bmax_ref[h], m_sc.shape)
    else:
        m_sc[...] = jnp.full(m_sc.shape, -jnp.inf, f32)
    l_sc[...] = jnp.zeros_like(l_sc)
    acc_sc[...] = jnp.zeros_like(acc_sc)

    def body(ki, carry):
        off = pl.multiple_of(ki * t, t)
        bidx = jnp.clip(ki - qi, -2, 2) + 2
        s_sc[...] = jnp.dot(q2, kt_ref[:, pl.ds(off, t)], preferred_element_type=f32)
        for g in range(2 * t // r):
            rows = slice(g * r, (g + 1) * r)
            brow = (g * r) % t
            s = s_sc[rows, :] + bias_ref[bidx, brow:brow + r, :]
            m = m_sc[rows, :]
            if not bounded:
                m_prev = m
                m = jnp.maximum(m_prev, jnp.max(s, axis=-1, keepdims=True))
                alpha = jnp.exp2(m_prev - m)
                m_sc[rows, :] = m
                a_sc[rows, :] = alpha
            ps = [jnp.exp2(s[:, j * hw:(j + 1) * hw] - m) for j in range(t // hw)]
            psum = jnp.sum(sum(ps), axis=-1, keepdims=True)
            l_sc[rows, :] = (l_sc[rows, :] if bounded else alpha * l_sc[rows, :]) + psum
            for j in range(t // hw):
                p_sc[rows, j * hw:(j + 1) * hw] = ps[j].astype(jnp.bfloat16)
        pv = jnp.dot(p_sc[...], v_ref[pl.ds(off, t), :], preferred_element_type=f32)
        acc_sc[...] = (acc_sc[...] if bounded else a_sc[...] * acc_sc[...]) + pv
        return carry

    lax.fori_loop(0, nk, body, 0)
    a = acc_sc[...] / l_sc[...]
    o = a[:t] - lam_ref[0] * a[t:]
    y = o * lax.rsqrt(jnp.mean(o * o, axis=-1, keepdims=True) + EPS)
    o_ref[...] = y * g_ref[...] * out_scale


def diff_attention(q2, kt, vb, stats, lam, out_gain, bias5, layer_idx):
    f32 = jnp.float32
    bsz, _, l, _ = vb.shape
    t = ATT_T
    hw = 2 * C_HEAD_DIM
    lam_init = 0.8 - 0.6 * math.exp(-0.3 * layer_idx)
    lam_f = lam.astype(f32)
    lam_full = jnp.exp(jnp.sum(lam_f[0] * lam_f[1])) - jnp.exp(jnp.sum(lam_f[2] * lam_f[3])) + lam_init
    norms = jnp.sqrt(jnp.max(stats[..., 0:4, 0], axis=2)) * (1.0 + 1e-3)
    qmax, kmax = norms[..., 0:2], norms[..., 2:4]
    bmax = jnp.max(bias5, axis=(1, 2, 3))
    bmin = jnp.min(bias5, axis=(1, 2, 3))
    gap = 2.0 * qmax * kmax + (bmax - bmin)[None, :, None]
    smem = pl.BlockSpec(memory_space=pltpu.SMEM)

    def run(bounded):
        return pl.pallas_call(
            functools.partial(_attn_kernel, t=t, nk=l // t, out_scale=1.0 - lam_init, bounded=bounded),
            grid=(bsz, C_HEADS, l // t),
            in_specs=[smem, smem, smem,
                      pl.BlockSpec((None, None, 2, t, hw), lambda b, h, i: (b, h, 0, i, 0)),
                      pl.BlockSpec((None, None, hw, l), lambda b, h, i: (b, h, 0, 0)),
                      pl.BlockSpec((None, None, l, hw), lambda b, h, i: (b, h, 0, 0)),
                      pl.BlockSpec((None, 5, t, t), lambda b, h, i: (h, 0, 0, 0)),
                      pl.BlockSpec((1, hw), lambda b, h, i: (0, 0))],
            out_specs=pl.BlockSpec((None, t, hw), lambda b, h, i: (b, i, h)),
            out_shape=jax.ShapeDtypeStruct((bsz, l, C_WIDTH), f32),
            scratch_shapes=[pltpu.VMEM((2 * t, hw), f32), pltpu.VMEM((2 * t, hw), f32), pltpu.VMEM((2 * t, hw), f32),
                            pltpu.VMEM((2 * t, t), f32), pltpu.VMEM((2 * t, t), jnp.bfloat16),
                            pltpu.VMEM((2 * t, hw), f32)],
            compiler_params=pltpu.CompilerParams(dimension_semantics=("parallel", "parallel", "arbitrary"),
                                                 vmem_limit_bytes=VMEM_LIMIT_BYTES),
            name="diff_attention_bounded" if bounded else "diff_attention_online",
        )(lam_full.reshape(1), kmax.reshape(-1), bmax, q2, kt, vb, bias5, out_gain.reshape(1, hw).astype(f32))

    return lax.cond(jnp.all(gap < ATT_SAFE_GAP), lambda: run(True), lambda: run(False))


GDN_TT = 512
GDN_HEADS_PER_STEP = 4
LANES = 128
OD_QKV_BLOCK = 0
OD_GATE_BLOCK = OD_QKV_BLOCK + 3 * D_WIDTH // LANES
OD_AB_BLOCK = OD_GATE_BLOCK + D_WIDTH // LANES
OD_COLS = 2304


def _gdn_prep_kernel(prev_ref, cur_ref, next_ref, w_ref, o_ref, *, tl, nl):
    i = pl.program_id(1)
    part = pl.program_id(2)
    prev = jnp.where(i > 0, prev_ref[...], 0.0)
    nxt = jnp.where(i < nl - 1, next_ref[...], 0.0)
    ext = jnp.concatenate([prev, cur_ref[...], nxt], axis=0)
    halo = prev.shape[0]
    acc = None
    for j in range(CONV_WIDTH):
        start = halo - CONV_WIDTH // 2 + j
        term = w_ref[j:j + 1, :] * ext[start:start + tl, :]
        acc = term if acc is None else acc + term
    y = acc * jax.nn.sigmoid(acc)
    scale = jnp.where(part == 0, D_HEAD_DIM ** -0.5, 1.0)
    heads = []
    for h in range(D_HEADS):
        yh = y[:, h * LANES:(h + 1) * LANES]
        heads.append(yh * (lax.rsqrt(jnp.sum(yh * yh, axis=-1, keepdims=True) + EPS) * scale))
    o_ref[...] = jnp.where(part < 2, jnp.concatenate(heads, axis=1), y)


def gdn_prep(proj3, conv_w, *, tl=512):
    bsz, l, _ = proj3.shape
    halo = SUBLANES
    nl = l // tl
    blk0 = OD_QKV_BLOCK * LANES // D_WIDTH
    return pl.pallas_call(
        functools.partial(_gdn_prep_kernel, tl=tl, nl=nl),
        grid=(bsz, nl, 3),
        in_specs=[pl.BlockSpec((None, halo, D_WIDTH), lambda b, i, p: (b, jnp.maximum(i * (tl // halo) - 1, 0), blk0 + p)),
                  pl.BlockSpec((None, tl, D_WIDTH), lambda b, i, p: (b, i, blk0 + p)),
                  pl.BlockSpec((None, halo, D_WIDTH),
                               lambda b, i, p: (b, jnp.minimum((i + 1) * (tl // halo), l // halo - 1), blk0 + p)),
                  pl.BlockSpec((CONV_WIDTH, D_WIDTH), lambda b, i, p: (0, p))],
        out_specs=pl.BlockSpec((None, None, tl, D_WIDTH), lambda b, i, p: (p, b, i, 0)),
        out_shape=jax.ShapeDtypeStruct((3, bsz, l, D_WIDTH), jnp.float32),
        compiler_params=pltpu.CompilerParams(dimension_semantics=("parallel", "parallel", "parallel"),
                                             vmem_limit_bytes=VMEM_LIMIT_BYTES),
        name="gdn_prep",
    )(proj3, proj3, proj3, conv_w.astype(jnp.float32))


def _gdn_gates_kernel(x_ref, nega_ref, dtb_ref, o_ref):
    x = x_ref[...]
    z = x + dtb_ref[...]
    g = nega_ref[...] * (jnp.maximum(z, 0.0) + jnp.log1p(jnp.exp(-jnp.abs(z))))
    lane = lax.broadcasted_iota(jnp.int32, x.shape, 1)
    y = jnp.where(lane < 2 * D_HEADS, g, jax.nn.sigmoid(x))
    o_ref[...] = y.T[0:4 * D_HEADS, :]


def gdn_gates(proj3, a_log, dt_bias, *, tl=512):
    bsz, l, _ = proj3.shape
    pad = LANES - 2 * D_HEADS
    nega = jnp.pad(-jnp.exp(a_log.astype(jnp.float32)).reshape(1, -1), ((0, 0), (0, pad)))
    dtb = jnp.pad(dt_bias.astype(jnp.float32).reshape(1, -1), ((0, 0), (0, pad)))
    vec = pl.BlockSpec((1, LANES), lambda b, i: (0, 0))
    return pl.pallas_call(
        _gdn_gates_kernel,
        grid=(bsz, l // tl),
        in_specs=[pl.BlockSpec((None, tl, LANES), lambda b, i: (b, i, OD_AB_BLOCK)), vec, vec],
        out_specs=pl.BlockSpec((None, 4 * D_HEADS, tl), lambda b, i: (b, 0, i)),
        out_shape=jax.ShapeDtypeStruct((bsz, 4 * D_HEADS, l), jnp.float32),
        compiler_params=pltpu.CompilerParams(dimension_semantics=("parallel", "parallel"),
                                             vmem_limit_bytes=VMEM_LIMIT_BYTES),
        name="gdn_gates",
    )(proj3, nega, dtb)


def gdn_constants():
    import numpy as np
    c = CHUNK
    r = np.arange(c)[:, None]
    u = np.arange(c)[None, :]
    cum, incl, strict = [], [], []
    for direction in range(2):
        fwd = direction == 0
        cum.append(np.concatenate([(r <= u) if fwd else (r >= u), np.ones((c, c), bool)], axis=1))
        incl.append((u <= r) if fwd else (u >= r))
        strict.append((u < r) if fwd else (u > r))
    same = lambda b: (r // b) == (u // b)
    merges = [same(2 * b) & ~same(b) for b in (8, 16, 32)]
    f32 = jnp.float32
    return (jnp.asarray(np.stack(cum), jnp.bfloat16), jnp.asarray(np.stack(incl), f32),
            jnp.asarray(np.stack(strict), f32), jnp.asarray(same(8), f32), jnp.asarray(np.stack(merges), f32))


def _gdn_kernel(q_ref, k_ref, v_ref, g_ref, b_ref, cum_ref, incl_ref, strict_ref, d8_ref, mrg_ref, o_ref,
                s_sc, qd_sc, dec_sc, w_sc, u_sc, sin_sc, *, nc):
    direction = pl.program_id(2)

    @pl.when(pl.program_id(3) == 0)
    def _():
        s_sc[...] = jnp.zeros_like(s_sc)

    bf16 = jnp.bfloat16
    f32 = jnp.float32
    c = CHUNK
    hd = D_HEAD_DIM
    contract_last = (((1,), (1,)), ((), ()))
    contract_first = (((0,), (0,)), ((), ()))
    cumm = cum_ref[...]
    incl = incl_ref[...]
    strict = strict_ref[...]
    d8 = d8_ref[...]
    eye = (lax.broadcasted_iota(jnp.int32, (c, c), 0) == lax.broadcasted_iota(jnp.int32, (c, c), 1)).astype(f32)

    def mm(a, b):
        return jnp.dot(a.astype(bf16), b.astype(bf16), preferred_element_type=f32)

    def rep(x):
        return jnp.concatenate([x] * (hd // c), axis=1)

    nh = g_ref.shape[0]
    ns = range(nh * nc)
    head = [m // nc for m in ns]
    rows = [slice((m % nc) * c, (m % nc + 1) * c) for m in ns]
    cols = [slice(h * hd, (h + 1) * hd) for h in head]
    q = [q_ref[rows[m], cols[m]] for m in ns]
    k = [k_ref[rows[m], cols[m]] for m in ns]
    v = [v_ref[rows[m], cols[m]] for m in ns]
    kb = [x.astype(bf16) for x in k]
    kk = [lax.dot_general(x, x, contract_last, preferred_element_type=f32) for x in kb]
    qk = [lax.dot_general(q[n].astype(bf16), kb[n], contract_last, preferred_element_type=f32) for n in ns]
    grow = [jnp.broadcast_to(g_ref[head[m], :, rows[m]], (c, c)) for m in ns]
    ghi = [x.astype(bf16) for x in grow]
    glo = [(grow[n] - ghi[n].astype(f32)).astype(bf16) for n in ns]
    gm = [jnp.dot(ghi[n], cumm, preferred_element_type=f32) + jnp.dot(glo[n], cumm, preferred_element_type=f32)
          for n in ns]
    gam_row = [x[:, :c] for x in gm]
    tot = [x[:, c:] for x in gm]
    gam_col = [x.T for x in gam_row]
    beta_col = [jnp.broadcast_to(b_ref[head[m], :, rows[m]], (c, c)).T for m in ns]
    decay = [incl * jnp.exp(jnp.minimum(gam_col[n] - gam_row[n], 0.0)) for n in ns]
    a = [strict * beta_col[n] * kk[n] * decay[n] for n in ns]
    a0 = [x * d8 for x in a]
    n2 = [mm(x, x) for x in a0]
    n4 = [mm(x, x) for x in n2]
    t = [mm(eye - a0[n], eye + n2[n]) for n in ns]
    t = [mm(t[n], eye + n4[n]) for n in ns]
    for j in range(mrg_ref.shape[0]):
        p = [mm(a[n] * mrg_ref[j], t[n]) for n in ns]
        t = [t[n] - mm(t[n], p[n]) for n in ns]
    beta128 = [rep(x) for x in beta_col]
    egam128 = [rep(jnp.exp(x)) for x in gam_col]
    solb = [mm(t[n], jnp.concatenate([k[n] * beta128[n] * egam128[n], v[n] * beta128[n]], axis=1)).astype(bf16)
            for n in ns]
    av = [jnp.dot((qk[n] * decay[n]).astype(bf16), solb[n], preferred_element_type=f32) for n in ns]
    k_dec = [(k[n] * rep(jnp.exp(tot[n] - gam_col[n]))).astype(bf16) for n in ns]
    wu = [lax.dot_general(k_dec[n], solb[n], contract_first, preferred_element_type=f32) for n in ns]
    for n in ns:
        qd_sc[n] = (q[n] * egam128[n] - av[n][:, :hd]).astype(bf16)
        o_ref[rows[n], cols[n]] = av[n][:, hd:]
        w_sc[n] = wu[n][:, :hd].astype(bf16)
        u_sc[n] = wu[n][:, hd:]
        dec_sc[n] = rep(jnp.exp(tot[n][0:SUBLANES, :]))

    def body(ci, states):
        ce = ci + direction * (nc - 1 - 2 * ci)
        new = []
        for h in range(nh):
            m = h * nc + ce
            sb = states[h].astype(bf16)
            sin_sc[m] = sb
            new.append(states[h] * dec_sc[m][0:1] - jnp.dot(w_sc[m], sb, preferred_element_type=f32) + u_sc[m])
        return tuple(new)

    states = lax.fori_loop(0, nc, body, tuple(s_sc[h] for h in range(nh)))
    for h in range(nh):
        s_sc[h] = states[h]

    for m in ns:
        o_ref[rows[m], cols[m]] += jnp.dot(qd_sc[m], sin_sc[m], preferred_element_type=f32)


def gdn_scan(qkv, gb):
    _, bsz, l, _ = qkv.shape
    hd = D_HEAD_DIM
    tt = min(GDN_TT, l)
    nt = l // tt
    nc = tt // CHUNK
    assert l % tt == 0 and tt % CHUNK == 0
    consts = gdn_constants()
    gb4 = gb.reshape(bsz, 4 * D_HEADS, 1, l)
    nh = GDN_HEADS_PER_STEP
    assert D_HEADS % nh == 0
    tidx = lambda d, i: i + d * (nt - 1 - 2 * i)
    qkv_spec = lambda p: pl.BlockSpec((None, None, tt, nh * hd), lambda b, h, d, i: (p, b, tidx(d, i), h))
    row_spec = lambda off: pl.BlockSpec((None, nh, 1, tt),
                                        lambda b, h, d, i: (b, (off + d * D_HEADS) // nh + h, 0, tidx(d, i)))
    per_dir = lambda a: pl.BlockSpec((None,) + a.shape[1:], lambda b, h, d, i: (d,) + (0,) * (a.ndim - 1))
    whole = lambda a: pl.BlockSpec(a.shape, lambda b, h, d, i: (0,) * a.ndim)
    return pl.pallas_call(
        functools.partial(_gdn_kernel, nc=nc),
        grid=(bsz, D_HEADS // nh, 2, nt),
        in_specs=[qkv_spec(0), qkv_spec(1), qkv_spec(2), row_spec(0), row_spec(2 * D_HEADS),
                  per_dir(consts[0]), per_dir(consts[1]), per_dir(consts[2]), whole(consts[3]), whole(consts[4])],
        out_specs=pl.BlockSpec((None, None, tt, nh * hd), lambda b, h, d, i: (d, b, tidx(d, i), h)),
        out_shape=jax.ShapeDtypeStruct((2, bsz, l, D_WIDTH), jnp.float32),
        scratch_shapes=[pltpu.VMEM((nh, hd, hd), jnp.float32),
                        pltpu.VMEM((nh * nc, CHUNK, hd), jnp.bfloat16),
                        pltpu.VMEM((nh * nc, SUBLANES, hd), jnp.float32),
                        pltpu.VMEM((nh * nc, hd, hd), jnp.bfloat16),
                        pltpu.VMEM((nh * nc, hd, hd), jnp.float32),
                        pltpu.VMEM((nh * nc, hd, hd), jnp.bfloat16)],
        compiler_params=pltpu.CompilerParams(dimension_semantics=("parallel", "parallel", "parallel", "arbitrary"),
                                             vmem_limit_bytes=VMEM_LIMIT_BYTES),
        name="gdn_scan",
    )(qkv, qkv, qkv, gb4, gb4, *consts)


def gated_deltanet(proj3, conv_w, a_log, dt_bias):
    return gdn_scan(gdn_prep(proj3, conv_w), gdn_gates(proj3, a_log, dt_bias))


MOE_TT = 512
MOE_SUB = 128
MOE_ALIGN = 64
MOE_ROWS = 256
MOE_SLAB = 256
MOE_VMEM_LIMIT_BYTES = 56 * 1024 * 1024


def _router_kernel(x_ref, g_ref, wr_ref, h_ref, aff_ref):
    x = x_ref[...]
    h = (x * lax.rsqrt(jnp.mean(x * x, axis=-1, keepdims=True) + EPS) * g_ref[...]).astype(jnp.bfloat16)
    h_ref[...] = h
    logits = jnp.dot(h, wr_ref[...], preferred_element_type=jnp.float32)
    lane = lax.broadcasted_iota(jnp.int32, logits.shape, 1)
    logits = jnp.where(lane < N_EXPERTS, logits, -jnp.inf)
    p = jnp.exp(logits - jnp.max(logits, axis=-1, keepdims=True))
    aff = p / jnp.sum(p, axis=-1, keepdims=True)
    aff_ref[...] = aff.T[0:N_EXPERTS, :]


def moe_route(x, gain, w_router, *, tm=MOE_TT):
    bsz, l, d = x.shape
    wr = jnp.pad(w_router.astype(jnp.bfloat16), ((0, 0), (0, LANES - N_EXPERTS)))
    return pl.pallas_call(
        _router_kernel,
        grid=(bsz, l // tm),
        in_specs=[pl.BlockSpec((None, tm, d), lambda b, i: (b, i, 0)),
                  pl.BlockSpec((1, d), lambda b, i: (0, 0)),
                  pl.BlockSpec((d, LANES), lambda b, i: (0, 0))],
        out_specs=[pl.BlockSpec((None, tm, d), lambda b, i: (b, i, 0)),
                   pl.BlockSpec((None, N_EXPERTS, tm), lambda b, i: (b, 0, i))],
        out_shape=[jax.ShapeDtypeStruct((bsz, l, d), jnp.bfloat16),
                   jax.ShapeDtypeStruct((bsz, N_EXPERTS, l), jnp.float32)],
        compiler_params=pltpu.CompilerParams(dimension_semantics=("parallel", "parallel"),
                                             vmem_limit_bytes=VMEM_LIMIT_BYTES),
        name="moe_router",
    )(x, gain.reshape(1, d).astype(jnp.float32), wr)


def _select_kernel(aff_ref, pre_ref, smap_ref, gate_ref, cnt_ref, *, cap, tt):
    f32 = jnp.float32
    bf16 = jnp.bfloat16
    aff = aff_ref[...]
    e, l = aff.shape
    nl = l // LANES
    tiles = [slice(j * LANES, (j + 1) * LANES) for j in range(nl)]
    bits = pltpu.bitcast(aff, jnp.int32)
    bt = [bits[:, s] for s in tiles]

    def lane_total(x):
        return jnp.broadcast_to(jnp.sum(x, axis=-1, keepdims=True), (e, LANES))

    def search(i, thr):
        cand = thr | jnp.left_shift(jnp.int32(1), 30 - i)
        acc = jnp.zeros((e, LANES), jnp.int32)
        for x in bt:
            acc = acc + (x >= cand).astype(jnp.int32)
        return jnp.where(lane_total(acc) >= cap, cand, thr)

    thr = lax.fori_loop(0, 31, search, jnp.zeros((e, LANES), jnp.int32))
    gt = [x > thr for x in bt]
    eq = [x == thr for x in bt]
    acc = jnp.zeros((e, LANES), jnp.int32)
    for x in gt:
        acc = acc + x.astype(jnp.int32)
    need = (cap - lane_total(acc)).astype(f32)

    pre = pre_ref[...]

    def prefix(flags):
        outs = [jnp.dot(jnp.where(x, 1.0, 0.0).astype(bf16), pre, preferred_element_type=f32) for x in flags]
        carry = jnp.zeros((e, LANES), f32)
        res = []
        for o in outs:
            res.append(o[:, :LANES] + carry)
            carry = carry + o[:, LANES:]
        return res, [o[:, LANES:] for o in outs]

    rank_eq, _ = prefix(eq)
    sel = [jnp.logical_or(gt[j], jnp.logical_and(eq[j], rank_eq[j] < need)) for j in range(nl)]
    pos, totals = prefix(sel)
    lane = lax.broadcasted_iota(jnp.int32, (e, LANES), 1)
    cnt = jnp.zeros((e, LANES), f32)
    per = tt // LANES
    for j in range(nl):
        smap_ref[:, tiles[j]] = jnp.where(sel[j], pos[j], -1.0)
        gate_ref[:, tiles[j]] = jnp.where(sel[j], aff[:, tiles[j]], 0.0)
        cnt = cnt + jnp.where(lane == j // per, totals[j], 0.0)
    cnt_ref[...] = cnt


def moe_select(aff, cap, *, tt=MOE_TT):
    import numpy as np
    bsz, e, l = aff.shape
    assert l // tt <= LANES
    i = np.arange(LANES)
    pre = np.concatenate([i[:, None] < i[None, :], np.ones((LANES, LANES), bool)], axis=1)
    row = pl.BlockSpec((None, e, l), lambda b: (b, 0, 0))
    return pl.pallas_call(
        functools.partial(_select_kernel, cap=cap, tt=tt),
        grid=(bsz,),
        in_specs=[row, pl.BlockSpec((LANES, 2 * LANES), lambda b: (0, 0))],
        out_specs=[row, row, pl.BlockSpec((None, e, LANES), lambda b: (b, 0, 0))],
        out_shape=[jax.ShapeDtypeStruct((bsz, e, l), jnp.float32), jax.ShapeDtypeStruct((bsz, e, l), jnp.float32),
                   jax.ShapeDtypeStruct((bsz, e, LANES), jnp.float32)],
        compiler_params=pltpu.CompilerParams(dimension_semantics=("parallel",),
                                             vmem_limit_bytes=VMEM_LIMIT_BYTES),
        name="moe_select",
    )(aff, jnp.asarray(pre, jnp.bfloat16))


def _slot_one_hot(pos, base, rows, n):
    slot = (base + lax.broadcasted_iota(jnp.int32, (rows, n), 0)).astype(jnp.float32)
    return jnp.where(pos == slot, 1.0, 0.0).astype(jnp.bfloat16)


def _expert_kernel(cs_ref, h_ref, smap_ref, gate_ref, wg32_ref, wu32_ref, wd32_ref, o_ref,
                   xs_sc, gs_sc, wg_ref, wu_ref, wd_ref, *, nj, cap):
    e = pl.program_id(0)
    b = pl.program_id(1)
    j = pl.program_id(2)
    f32 = jnp.float32
    bf16 = jnp.bfloat16

    @pl.when(jnp.logical_and(b == 0, j == 0))
    def _():
        wg_ref[...] = wg32_ref[...].astype(bf16)
        wu_ref[...] = wu32_ref[...].astype(bf16)
        wd_ref[...] = wd32_ref[...].astype(bf16)

    @pl.when(j == 0)
    def _():
        xs_sc[...] = jnp.zeros_like(xs_sc)
        gs_sc[...] = jnp.zeros_like(gs_sc)

    base = (b * N_EXPERTS + e) * (nj + 1) + j
    c0 = cs_ref[base]
    c1 = cs_ref[base + 1]
    pos = smap_ref[...]
    gate = gate_ref[...]
    hb = h_ref[...]
    tt = hb.shape[0]

    def gather(r0, rows):
        oh = _slot_one_hot(pos, r0, rows, tt)
        xs_sc[pl.ds(r0, rows), :] += jnp.dot(oh, hb, preferred_element_type=f32)
        g = jnp.sum(oh.astype(f32) * gate, axis=-1, keepdims=True)
        gs_sc[pl.ds(r0, rows), :] += jnp.broadcast_to(g, (rows, LANES))

    win = min(MOE_SUB, cap)
    w0 = pl.multiple_of(jnp.minimum(c0 // MOE_ALIGN * MOE_ALIGN, cap - win), MOE_ALIGN)
    gather(w0, win)

    def rest(st, carry):
        gather(pl.multiple_of(st * MOE_ALIGN, MOE_ALIGN), MOE_ALIGN)
        return carry

    lax.fori_loop((w0 + win) // MOE_ALIGN, (c1 + MOE_ALIGN - 1) // MOE_ALIGN, rest, 0)

    @pl.when(j == nj - 1)
    def _():
        rows_per = min(MOE_ROWS, cap)
        for r in range(cap // rows_per):
            rows = slice(r * rows_per, (r + 1) * rows_per)
            xb = xs_sc[rows, :].astype(bf16)
            g = jnp.dot(xb, wg_ref[...], preferred_element_type=f32)
            u = jnp.dot(xb, wu_ref[...], preferred_element_type=f32)
            hid = (g * jax.nn.sigmoid(g) * u).astype(bf16)
            out = jnp.dot(hid, wd_ref[...], preferred_element_type=f32)
            scale = jnp.concatenate([gs_sc[rows, :]] * (out.shape[1] // LANES), axis=1)
            o_ref[rows, :] = (out * scale).astype(bf16)


def moe_experts(hb, smap, gate, cs, w_gate, w_up, w_down, layer, cap, *, tt=MOE_TT):
    bsz, l, d = hb.shape
    _, e, _, ff = w_gate.shape
    nj = l // tt
    smap4 = smap.reshape(bsz, e, 1, l)
    gate4 = gate.reshape(bsz, e, 1, l)
    tok = pl.BlockSpec((None, None, 1, tt), lambda ei, b, j, cs_ref: (b, ei, 0, j))
    once = pl.Buffered(1)
    grid_spec = pltpu.PrefetchScalarGridSpec(
        num_scalar_prefetch=1,
        grid=(e, bsz, nj),
        in_specs=[pl.BlockSpec((None, tt, d), lambda ei, b, j, cs_ref: (b, j, 0)), tok, tok,
                  pl.BlockSpec((None, None, d, ff), lambda ei, b, j, cs_ref: (layer, ei, 0, 0), pipeline_mode=once),
                  pl.BlockSpec((None, None, d, ff), lambda ei, b, j, cs_ref: (layer, ei, 0, 0), pipeline_mode=once),
                  pl.BlockSpec((None, None, ff, d), lambda ei, b, j, cs_ref: (layer, ei, 0, 0), pipeline_mode=once)],
        out_specs=pl.BlockSpec((None, None, cap, d), lambda ei, b, j, cs_ref: (b, ei, 0, 0)),
        scratch_shapes=[pltpu.VMEM((cap, d), jnp.float32), pltpu.VMEM((cap, LANES), jnp.float32),
                        pltpu.VMEM((d, ff), jnp.bfloat16), pltpu.VMEM((d, ff), jnp.bfloat16),
                        pltpu.VMEM((ff, d), jnp.bfloat16)])
    return pl.pallas_call(
        functools.partial(_expert_kernel, nj=nj, cap=cap),
        grid_spec=grid_spec,
        out_shape=jax.ShapeDtypeStruct((bsz, e, cap, d), jnp.bfloat16),
        compiler_params=pltpu.CompilerParams(dimension_semantics=("parallel", "arbitrary", "arbitrary"),
                                             vmem_limit_bytes=MOE_VMEM_LIMIT_BYTES),
        name="moe_experts",
    )(cs, hb, smap4, gate4, w_gate, w_up, w_down)


def _combine_kernel(cs_ref, x_ref, smap_ref, ow_ref, y_ref, *, nj, tt):
    b = pl.program_id(0)
    e = pl.program_id(2)

    @pl.when(e == 0)
    def _():
        y_ref[...] = x_ref[...]

    contract_first = (((0,), (0,)), ((), ()))
    cap = ow_ref.shape[0]
    win = min(2 * MOE_SUB, cap)
    base = (b * N_EXPERTS + e) * (nj + 1)
    cols = [slice(j * tt, (j + 1) * tt) for j in range(nj)]
    pos = [smap_ref[:, c] for c in cols]
    r0 = [pl.multiple_of(jnp.minimum(cs_ref[base + j] // MOE_SUB * MOE_SUB, cap - win), MOE_SUB) for j in range(nj)]
    oh = [_slot_one_hot(pos[j], r0[j], win, tt) for j in range(nj)]
    add = [lax.dot_general(oh[j], ow_ref[pl.ds(r0[j], win), :], contract_first, preferred_element_type=jnp.float32)
           for j in range(nj)]
    for j in range(nj):
        y_ref[cols[j], :] += add[j]

    for j in range(nj):
        def scatter(st, carry):
            s0 = pl.multiple_of(st * MOE_SUB, MOE_SUB)
            y_ref[cols[j], :] += lax.dot_general(_slot_one_hot(pos[j], s0, MOE_SUB, tt),
                                                 ow_ref[pl.ds(s0, MOE_SUB), :], contract_first,
                                                 preferred_element_type=jnp.float32)
            return carry

        lax.fori_loop((r0[j] + win) // MOE_SUB, (cs_ref[base + j + 1] + MOE_SUB - 1) // MOE_SUB, scatter, 0)


def moe_combine(x, smap, outw, cs, *, tt=MOE_TT):
    bsz, l, d = x.shape
    e, cap = outw.shape[1:3]
    nj = l // tt
    smap4 = smap.reshape(bsz, e, 1, l)
    grid_spec = pltpu.PrefetchScalarGridSpec(
        num_scalar_prefetch=1,
        grid=(bsz, d // MOE_SLAB, e),
        in_specs=[pl.BlockSpec((None, l, MOE_SLAB), lambda b, s, ei, cs_ref: (b, 0, s)),
                  pl.BlockSpec((None, None, 1, l), lambda b, s, ei, cs_ref: (b, ei, 0, 0)),
                  pl.BlockSpec((None, None, cap, MOE_SLAB), lambda b, s, ei, cs_ref: (b, ei, 0, s))],
        out_specs=pl.BlockSpec((None, l, MOE_SLAB), lambda b, s, ei, cs_ref: (b, 0, s)))
    return pl.pallas_call(
        functools.partial(_combine_kernel, nj=nj, tt=tt),
        grid_spec=grid_spec,
        out_shape=jax.ShapeDtypeStruct((bsz, l, d), jnp.float32),
        compiler_params=pltpu.CompilerParams(dimension_semantics=("parallel", "parallel", "arbitrary"),
                                             vmem_limit_bytes=MOE_VMEM_LIMIT_BYTES),
        name="moe_combine",
    )(cs, x, smap4, outw)


def ec_moe_layer(x, gain, w_router, w_gate, w_up, w_down, layer):
    bsz, l, d = x.shape
    cap = EC_CAPACITY_FACTOR * l // N_EXPERTS
    tt = min(MOE_TT, l)
    nj = l // tt
    hb, aff = moe_route(x, gain, w_router, tm=tt)
    smap, gate, cnt = moe_select(aff, cap, tt=tt)
    cs = jnp.concatenate([jnp.zeros((bsz, N_EXPERTS, 1), jnp.float32), jnp.cumsum(cnt[..., :nj], axis=-1)], axis=-1)
    cs = cs.astype(jnp.int32).reshape(-1)
    outw = moe_experts(hb, smap, gate, cs, w_gate, w_up, w_down, layer, cap, tt=tt)
    return moe_combine(x, smap, outw, cs, tt=tt)


def kernel(x, mix_norm, ffn_norm, ev_w_in, ev_w_out, a_lb_logits, a_out_norm, s5_lambda_re, s5_lambda_im, s5_log_step, s5_b_re, s5_b_im, s5_c_re, s5_c_im, s5_d, s5_glu_w, s5_glu_b, od_w_in, od_w_out, c_q_norm, c_k_norm, c_lambda, c_out_norm, rel_bias, d_conv_w, d_a_log, d_dt_bias, d_out_norm, moe_router, moe_w_gate, moe_w_up, moe_w_down):
    bsz, l, d = x.shape
    p = jax.nn.softmax(a_lb_logits.astype(jnp.float32), axis=0)
    cum = jnp.cumsum(p, axis=0)
    lower_bounds = cum - cum[0:1]
    bias5 = rel_bias_tiles(rel_bias, ATT_T)
    for layer in range(DEPTH):
        j = layer // 2
        if layer % 2 == 0:
            proj, u_tb = norm_matmul(x, mix_norm[layer], ev_w_in[j], tail=B_WIDTH)
            o_a2 = hgrn2_scan(proj, lower_bounds[j])
            o_b = s5_mixer_tb(u_tb.reshape(l * bsz, B_WIDTH), bsz, s5_lambda_re[j], s5_lambda_im[j], s5_log_step[j],
                              s5_b_re[j], s5_b_im[j], s5_c_re[j], s5_c_im[j], s5_d[j], s5_glu_w[j], s5_glu_b[j])
            x = mixer_out_proj(o_a2, proj, 4 * A_HEADS, a_out_norm[j], o_b.reshape(l, bsz * B_WIDTH), ev_w_out[j], x,
                               bidir_first=True, other_time_major=True)
        else:
            o2 = 3 * C_WIDTH + 3 * D_WIDTH
            o4 = o2 + 4 * D_HEADS
            w = od_w_in[j]
            w_in = jnp.concatenate([w[:, :o2], w[:, o4:], w[:, o2:o4],
                                    jnp.zeros((d, 3 * C_WIDTH + OD_COLS - w.shape[1]), w.dtype)], axis=1)
            proj, q2, kt, vb, stats = odd_in_proj(x, mix_norm[layer], w_in, c_q_norm[j], c_k_norm[j])
            o_c = diff_attention(q2, kt, vb, stats, c_lambda[j], c_out_norm[j], bias5, layer)
            o_d2 = gated_deltanet(proj, d_conv_w[j], d_a_log[j], d_dt_bias[j])
            x = mixer_out_proj(o_d2, proj, OD_GATE_BLOCK, d_out_norm[j], o_c, od_w_out[j], x, bidir_first=False)
        x = ec_moe_layer(x, ffn_norm[layer], moe_router[layer], moe_w_gate, moe_w_up, moe_w_down, layer)
    return x
```

```python
import functools
import math

import jax
import jax.numpy as jnp
from jax import lax
from jax.experimental import pallas as pl
from jax.experimental.pallas import tpu as pltpu

D_MODEL = 1024
DEPTH = 4
MIX_WIDTH = D_MODEL
A_WIDTH = MIX_WIDTH // 2
A_HEAD_DIM = 128
A_HEADS = A_WIDTH // A_HEAD_DIM
B_WIDTH = MIX_WIDTH - A_WIDTH
S5_GROUP = 16
S5_GROUPS = B_WIDTH // S5_GROUP
S5_STATE = 64
C_WIDTH = MIX_WIDTH // 2
C_HEAD_DIM = 64
C_HEADS = C_WIDTH // (2 * C_HEAD_DIM)
C_V_DIM = 2 * C_HEAD_DIM
D_WIDTH = MIX_WIDTH - C_WIDTH
D_HEAD_DIM = 128
D_HEADS = D_WIDTH // D_HEAD_DIM
CONV_WIDTH = 5
N_EXPERTS = 16
EXPERT_FF = 2 * D_MODEL
EC_CAPACITY_FACTOR = 2
REL_BUCKETS = 32
REL_MAX_DIST = 128
CHUNK = 64
Q_BLOCK = 128
EPS = 1e-6

VMEM_LIMIT_BYTES = 48 * 1024 * 1024


PROJ_TM = 512
PROJ_COLS = 512


def _norm_matmul_kernel(x_ref, g_ref, w_ref, o_ref, *tail_ref, main):
    x = x_ref[...]
    y = (x * lax.rsqrt(jnp.mean(x * x, axis=-1, keepdims=True) + EPS) * g_ref[...]).astype(jnp.bfloat16)
    for c0 in range(0, main, PROJ_COLS):
        c1 = min(c0 + PROJ_COLS, main)
        o_ref[:, c0:c1] = jnp.dot(y, w_ref[:, c0:c1], preferred_element_type=jnp.float32)
    if tail_ref:
        tail_ref[0][...] = jnp.dot(y, w_ref[:, main:], preferred_element_type=jnp.float32)


def norm_matmul(x, gain, w, *, tail=0, tm=PROJ_TM):
    bsz, l, k = x.shape
    m = w.shape[1]
    main = m - tail
    tm = min(tm, l)
    out_shape = [jax.ShapeDtypeStruct((bsz, l, main), jnp.float32)]
    out_specs = [pl.BlockSpec((None, tm, main), lambda b, i: (b, i, 0))]
    if tail:
        out_shape.append(jax.ShapeDtypeStruct((l, bsz * tail), jnp.float32))
        out_specs.append(pl.BlockSpec((tm, tail), lambda b, i: (i, b)))
    outs = pl.pallas_call(
        functools.partial(_norm_matmul_kernel, main=main),
        grid=(bsz, l // tm),
        in_specs=[pl.BlockSpec((None, tm, k), lambda b, i: (b, i, 0)),
                  pl.BlockSpec((1, k), lambda b, i: (0, 0)),
                  pl.BlockSpec((k, m), lambda b, i: (0, 0), pipeline_mode=pl.Buffered(1))],
        out_specs=out_specs,
        out_shape=out_shape,
        compiler_params=pltpu.CompilerParams(dimension_semantics=("parallel", "parallel"),
                                             vmem_limit_bytes=VMEM_LIMIT_BYTES),
        name="norm_matmul",
    )(x, gain.reshape(1, k).astype(jnp.float32), w.astype(jnp.bfloat16))
    return outs if tail else outs[0]


def _mixer_out_kernel(of_ref, ob_ref, g_ref, gain_ref, other_ref, wb_ref, wo_ref, r_ref, o_ref):
    bf16 = jnp.bfloat16
    o = of_ref[...] + ob_ref[...]
    g = g_ref[...]
    gate = g * jax.nn.sigmoid(g)
    hd = gain_ref.shape[1]
    heads = []
    for h in range(o.shape[1] // hd):
        oh = o[:, h * hd:(h + 1) * hd]
        heads.append(oh * lax.rsqrt(jnp.mean(oh * oh, axis=-1, keepdims=True) + EPS) * gain_ref[...])
    y = (jnp.concatenate(heads, axis=1) * gate).astype(bf16)
    o_ref[...] = (r_ref[...] + jnp.dot(y, wb_ref[...], preferred_element_type=jnp.float32)
                  + jnp.dot(other_ref[...].astype(bf16), wo_ref[...], preferred_element_type=jnp.float32))


def mixer_out_proj(o2, proj3, gate_block, out_gain, other, w, res, *, bidir_first, other_time_major=False,
                   tm=PROJ_TM):
    _, bsz, l, k = o2.shape
    m = w.shape[1]
    tm = min(tm, l)
    wb = w.astype(jnp.bfloat16)
    w_bidir, w_other = (wb[:k], wb[k:]) if bidir_first else (wb[k:], wb[:k])
    other_spec = (pl.BlockSpec((tm, k), lambda b, i: (i, b)) if other_time_major
                  else pl.BlockSpec((None, tm, k), lambda b, i: (b, i, 0)))
    gb = gate_block * LANES // k
    row = pl.BlockSpec((None, tm, m), lambda b, i: (b, i, 0))
    wspec = pl.BlockSpec((k, m), lambda b, i: (0, 0))
    return pl.pallas_call(
        _mixer_out_kernel,
        grid=(bsz, l // tm),
        in_specs=[pl.BlockSpec((None, None, tm, k), lambda b, i: (0, b, i, 0)),
                  pl.BlockSpec((None, None, tm, k), lambda b, i: (1, b, i, 0)),
                  pl.BlockSpec((None, tm, k), lambda b, i: (b, i, gb)),
                  pl.BlockSpec((1, LANES), lambda b, i: (0, 0)),
                  other_spec, wspec, wspec, row],
        out_specs=row,
        out_shape=jax.ShapeDtypeStruct((bsz, l, m), jnp.float32),
        compiler_params=pltpu.CompilerParams(dimension_semantics=("parallel", "parallel"),
                                             vmem_limit_bytes=VMEM_LIMIT_BYTES),
        name="mixer_out_proj",
    )(o2, o2, proj3, out_gain.reshape(1, LANES).astype(jnp.float32), other, w_bidir, w_other, res)


HG_LEVELS = tuple(CHUNK >> (i + 1) for i in range(CHUNK.bit_length() - 1))
HG_TOT_ROWS = 8
HG_TT = 512


def hgrn2_constants():
    import numpy as np
    c = CHUNK
    r = np.arange(c)[:, None]
    u = np.arange(c)[None, :]
    stacks, masks = [], []
    for direction in range(2):
        fwd = direction == 0
        lvl_masks = []
        for m in HG_LEVELS:
            blk = r // (2 * m)
            later = (r % (2 * m)) >= m
            lvl_masks.append((blk == blk.T) & (later & ~later.T if fwd else ~later & later.T))
        stacks.append(np.concatenate([(u <= r) if fwd else (u >= r), np.ones((HG_TOT_ROWS, c), bool)], axis=0))
        masks.append(np.stack(lvl_masks))
    return (jnp.asarray(np.stack(stacks), jnp.bfloat16), jnp.asarray(np.stack(masks), jnp.float32))


def _hgrn2_kernel(q_ref, f_ref, v_ref, loglb_ref, log1mlb_ref, onemlb_ref, ast_ref, mask_ref, o_ref,
                  st_sc, qd_sc, dec_sc, upd_sc, sin_sc, *, nc):
    direction = pl.program_id(2)

    @pl.when(pl.program_id(3) == 0)
    def _():
        st_sc[...] = jnp.zeros_like(st_sc)

    bf16 = jnp.bfloat16
    f32 = jnp.float32
    c = CHUNK
    hd = A_HEAD_DIM
    dirf = direction.astype(f32)
    loglb = loglb_ref[...]
    log1mlb = log1mlb_ref[...]
    onemlb = onemlb_ref[...]
    ast = ast_ref[...]
    contract_last = (((1,), (1,)), ((), ()))
    contract_first = (((0,), (0,)), ((), ()))

    ns = range(nc)
    rows = [slice(n * c, (n + 1) * c) for n in ns]
    z = [f_ref[r, :] for r in rows]
    v = [v_ref[r, :] for r in rows]
    qr = [q_ref[r, :] for r in rows]
    q = [x * jax.nn.sigmoid(x) for x in qr]
    e = [jnp.exp(-jnp.abs(x)) for x in z]
    cc = [log1mlb + jnp.minimum(z[n], 0.0) - jnp.log1p(e[n]) for n in ns]
    lf = [jnp.maximum(loglb, x) + jnp.log1p(jnp.exp(-jnp.abs(loglb - x))) for x in cc]
    k = [onemlb * jnp.where(z[n] >= 0, e[n], 1.0) / (1.0 + e[n]) for n in ns]
    hi = [x.astype(bf16) for x in lf]
    lo = [(lf[n] - hi[n].astype(f32)).astype(bf16) for n in ns]
    d = [jnp.dot(ast, hi[n], preferred_element_type=f32) + jnp.dot(ast, lo[n], preferred_element_type=f32)
         for n in ns]
    cum = [x[0:c] for x in d]
    tot = [x[c:c + HG_TOT_ROWS] for x in d]
    ref = [cum[n] - dirf * lf[n] for n in ns]
    attn = [jnp.zeros((c, c), f32) for _ in ns]
    for li, m in enumerate(HG_LEVELS):
        nb = c // (2 * m)
        split = [jnp.broadcast_to(x.reshape(nb, 2 * m, hd)[:, m - 1:m, :], (nb, 2 * m, hd)).reshape(c, hd)
                 for x in ref]
        x = [jnp.exp(-jnp.abs(cum[n] - split[n])) for n in ns]
        s = [lax.dot_general((q[n] * x[n]).astype(bf16), (k[n] * x[n]).astype(bf16), contract_last,
                             preferred_element_type=f32) for n in ns]
        attn = [attn[n] + mask_ref[li] * s[n] for n in ns]
    vb = [x.astype(bf16) for x in v]
    intra = [jnp.dot(attn[n].astype(bf16), vb[n], preferred_element_type=f32) for n in ns]
    upd = [lax.dot_general(vb[n], (k[n] * jnp.exp(tot[n][0:1] - cum[n])).astype(bf16), contract_first,
                           preferred_element_type=f32) for n in ns]
    for n in ns:
        o_ref[rows[n], :] = intra[n] + jnp.sum(q[n] * k[n], axis=-1, keepdims=True) * v[n]
        qd_sc[n] = (q[n] * jnp.exp(cum[n])).astype(bf16)
        dec_sc[n] = jnp.exp(tot[n])
        upd_sc[n] = upd[n]

    def body(ci, st):
        ce = ci + direction * (nc - 1 - 2 * ci)
        sin_sc[ce] = st.astype(bf16)
        return st * dec_sc[ce][0:1] + upd_sc[ce]

    st_sc[...] = lax.fori_loop(0, nc, body, st_sc[...])

    for n in range(nc):
        rows = slice(n * c, (n + 1) * c)
        o_ref[rows, :] += lax.dot_general(qd_sc[n], sin_sc[n], contract_last, preferred_element_type=f32)


def hgrn2_scan(proj3, lb):
    bsz, l, _ = proj3.shape
    hd = A_HEAD_DIM
    tt = min(HG_TT, l)
    nt = l // tt
    assert l % tt == 0 and tt % CHUNK == 0
    ast, masks = hgrn2_constants()
    lb = lb.astype(jnp.float32)
    vecs = [jnp.log(lb).reshape(2, 1, A_WIDTH), jnp.log1p(-lb).reshape(2, 1, A_WIDTH), (1.0 - lb).reshape(2, 1, A_WIDTH)]
    tidx = lambda d, i: i + d * (nt - 1 - 2 * i)
    vec = pl.BlockSpec((None, 1, hd), lambda b, h, d, i: (d, 0, h))
    return pl.pallas_call(
        functools.partial(_hgrn2_kernel, nc=tt // CHUNK),
        grid=(bsz, A_HEADS, 2, nt),
        in_specs=[pl.BlockSpec((None, tt, hd), lambda b, h, d, i: (b, tidx(d, i), h)),
                  pl.BlockSpec((None, tt, hd), lambda b, h, d, i: (b, tidx(d, i), (1 + d) * A_HEADS + h)),
                  pl.BlockSpec((None, tt, hd), lambda b, h, d, i: (b, tidx(d, i), 3 * A_HEADS + h)),
                  vec, vec, vec,
                  pl.BlockSpec((None,) + ast.shape[1:], lambda b, h, d, i: (d, 0, 0)),
                  pl.BlockSpec((None,) + masks.shape[1:], lambda b, h, d, i: (d, 0, 0, 0))],
        out_specs=pl.BlockSpec((None, None, tt, hd), lambda b, h, d, i: (d, b, tidx(d, i), h)),
        out_shape=jax.ShapeDtypeStruct((2, bsz, l, A_WIDTH), jnp.float32),
        scratch_shapes=[pltpu.VMEM((hd, hd), jnp.float32),
                        pltpu.VMEM((tt // CHUNK, CHUNK, hd), jnp.bfloat16),
                        pltpu.VMEM((tt // CHUNK, HG_TOT_ROWS, hd), jnp.float32),
                        pltpu.VMEM((tt // CHUNK, hd, hd), jnp.float32),
                        pltpu.VMEM((tt // CHUNK, hd, hd), jnp.bfloat16)],
        compiler_params=pltpu.CompilerParams(dimension_semantics=("parallel", "parallel", "parallel", "arbitrary"),
                                             vmem_limit_bytes=VMEM_LIMIT_BYTES),
        name="hgrn2_scan",
    )(proj3, proj3, proj3, *vecs, ast, masks)


S5_NS = S5_GROUPS * S5_STATE
S5_TT = 64
SUBLANES = 8


def _s5_scan_kernel(u_ref, win_ref, ar_ref, ai_ref, wout_ref, y_ref, bu_sc, xs_sc, st_sc, *, bsz, tt, reverse):
    @pl.when(pl.program_id(0) == 0)
    def _():
        st_sc[...] = jnp.zeros_like(st_sc)

    ub = u_ref[...].astype(jnp.bfloat16)
    halves = 2
    uw = B_WIDTH // halves
    sw = S5_NS // halves
    for hf in range(halves):
        for part in range(2):
            sc = slice(part * S5_NS + hf * sw, part * S5_NS + (hf + 1) * sw)
            bu_sc[:, sc] = jnp.dot(ub[:, hf * uw:(hf + 1) * uw], win_ref[hf * uw:(hf + 1) * uw, sc],
                                   preferred_element_type=jnp.float32)
    ar = jnp.broadcast_to(ar_ref[...], (bsz, S5_NS))
    ai = jnp.broadcast_to(ai_ref[...], (bsz, S5_NS))
    per = SUBLANES // bsz
    ngroups = tt // per

    def body(s, carry):
        xr, xi = carry
        p = (ngroups - 1 - s) if reverse else s
        base = pl.multiple_of(p * SUBLANES, SUBLANES)
        blk = bu_sc[pl.ds(base, SUBLANES), :]
        outs_r = [None] * per
        outs_i = [None] * per
        for ph in (range(per - 1, -1, -1) if reverse else range(per)):
            br = blk[ph * bsz:(ph + 1) * bsz, :S5_NS]
            bi = blk[ph * bsz:(ph + 1) * bsz, S5_NS:]
            xr, xi = ar * xr - ai * xi + br, ar * xi + ai * xr + bi
            outs_r[ph] = xr
            outs_i[ph] = xi
        xs_sc[pl.ds(base, SUBLANES), :S5_NS] = jnp.concatenate(outs_r, axis=0)
        xs_sc[pl.ds(base, SUBLANES), S5_NS:] = jnp.concatenate(outs_i, axis=0)
        return xr, xi

    xr, xi = lax.fori_loop(0, ngroups, body, (st_sc[0], st_sc[1]))
    st_sc[0] = xr
    st_sc[1] = xi
    for hf in range(halves):
        yc = slice(hf * uw, (hf + 1) * uw)
        acc = None
        for part in range(2):
            sc = slice(part * S5_NS + hf * sw, part * S5_NS + (hf + 1) * sw)
            term = jnp.dot(xs_sc[:, sc].astype(jnp.bfloat16), wout_ref[sc, yc], preferred_element_type=jnp.float32)
            acc = term if acc is None else acc + term
        y_ref[:, yc] = acc


def s5_scan(u_tb, win, ar, ai, wout, *, bsz, reverse):
    n = u_tb.shape[0]
    rows = S5_TT * bsz
    nt = n // rows
    assert n % rows == 0 and SUBLANES % bsz == 0
    idx = (lambda i: (nt - 1 - i, 0)) if reverse else (lambda i: (i, 0))
    const = lambda i: (0, 0)
    return pl.pallas_call(
        functools.partial(_s5_scan_kernel, bsz=bsz, tt=S5_TT, reverse=reverse),
        grid=(nt,),
        in_specs=[pl.BlockSpec((rows, B_WIDTH), idx),
                  pl.BlockSpec((B_WIDTH, 2 * S5_NS), const),
                  pl.BlockSpec((1, S5_NS), const),
                  pl.BlockSpec((1, S5_NS), const),
                  pl.BlockSpec((2 * S5_NS, B_WIDTH), const)],
        out_specs=pl.BlockSpec((rows, B_WIDTH), idx),
        out_shape=jax.ShapeDtypeStruct((n, B_WIDTH), jnp.float32),
        scratch_shapes=[pltpu.VMEM((rows, 2 * S5_NS), jnp.float32),
                        pltpu.VMEM((rows, 2 * S5_NS), jnp.float32),
                        pltpu.VMEM((2, bsz, S5_NS), jnp.float32)],
        compiler_params=pltpu.CompilerParams(dimension_semantics=("arbitrary",),
                                             vmem_limit_bytes=VMEM_LIMIT_BYTES),
        name="s5_scan_bwd" if reverse else "s5_scan_fwd",
    )(u_tb, win, ar, ai, wout)


def _s5_final_kernel(u_ref, yf_ref, yb_ref, d_ref, w_ref, b_ref, o_ref):
    y = d_ref[...] * u_ref[...] + yf_ref[...] + yb_ref[...]
    y = jax.nn.gelu(y)
    z = jnp.dot(y.astype(jnp.bfloat16), w_ref[...], preferred_element_type=jnp.float32) + b_ref[...]
    o_ref[...] = y * jax.nn.sigmoid(z)


def s5_finalize(u, yf, yb, d_skip, glu_w, glu_b, *, tm=512):
    n, w = u.shape
    row = pl.BlockSpec((tm, w), lambda i: (i, 0))
    vec = pl.BlockSpec((1, w), lambda i: (0, 0))
    return pl.pallas_call(
        _s5_final_kernel,
        grid=(n // tm,),
        in_specs=[row, row, row, vec, pl.BlockSpec((w, w), lambda i: (0, 0)), vec],
        out_specs=row,
        out_shape=jax.ShapeDtypeStruct((n, w), jnp.float32),
        compiler_params=pltpu.CompilerParams(dimension_semantics=("parallel",),
                                             vmem_limit_bytes=VMEM_LIMIT_BYTES),
        name="s5_finalize",
    )(u, yf, yb, d_skip.reshape(1, w).astype(jnp.float32), glu_w.astype(jnp.bfloat16),
      glu_b.reshape(1, w).astype(jnp.float32))


def s5_direction_params(lam_re, lam_im, log_step, b_re, b_im, c_re, c_im):
    step = jnp.exp(log_step)[:, None]
    mag = jnp.exp(lam_re * step)
    abar_re = mag * jnp.cos(lam_im * step)
    abar_im = mag * jnp.sin(lam_im * step)
    den = lam_re * lam_re + lam_im * lam_im
    fr = ((abar_re - 1.0) * lam_re + abar_im * lam_im) / den
    fi = (abar_im * lam_re - (abar_re - 1.0) * lam_im) / den
    bb_re = fr[..., None] * b_re - fi[..., None] * b_im
    bb_im = fr[..., None] * b_im + fi[..., None] * b_re
    eye = jnp.eye(S5_GROUPS, dtype=jnp.float32)
    win = jnp.concatenate([jnp.einsum('gnp,gh->gphn', bb, eye).reshape(B_WIDTH, S5_NS) for bb in (bb_re, bb_im)],
                          axis=1)
    wout = jnp.concatenate([jnp.einsum('gpn,gh->hngp', c, eye).reshape(S5_NS, B_WIDTH) for c in (c_re, -c_im)],
                           axis=0)
    return (win.astype(jnp.bfloat16), abar_re.reshape(1, S5_NS), abar_im.reshape(1, S5_NS),
            wout.astype(jnp.bfloat16))


def s5_mixer_tb(u_tb, bsz, lam_re, lam_im, log_step, b_re, b_im, c_re, c_im, d_skip, glu_w, glu_b):
    f32 = jnp.float32
    ys = []
    for direction in range(2):
        prm = s5_direction_params(lam_re[direction].astype(f32), lam_im[direction].astype(f32),
                                  log_step[direction].astype(f32), b_re[direction].astype(f32),
                                  b_im[direction].astype(f32), c_re[direction].astype(f32),
                                  c_im[direction].astype(f32))
        ys.append(s5_scan(u_tb, *prm, bsz=bsz, reverse=(direction == 1)))
    return s5_finalize(u_tb, ys[0], ys[1], d_skip, glu_w, glu_b)


def t5_bucket(rel):
    half = REL_BUCKETS // 2
    max_exact = half // 2
    base = jnp.where(rel > 0, half, 0)
    n = jnp.abs(rel)
    nf = jnp.maximum(n, 1).astype(jnp.float32)
    large = max_exact + (jnp.log(nf / max_exact) / math.log(REL_MAX_DIST / max_exact)
                         * (half - max_exact)).astype(jnp.int32)
    large = jnp.minimum(large, half - 1)
    return base + jnp.where(n < max_exact, n, large)


ATT_T = 512
LOG2E = math.log2(math.e)


def rel_bias_tiles(rel_bias, t):
    assert t >= REL_MAX_DIST
    table = rel_bias.astype(jnp.float32) * LOG2E
    tiles = []
    for d in (-1, 0, 1):
        c = table[t5_bucket(d * t + jnp.arange(-(t - 1), t))]
        w = jnp.concatenate([c, c[:1]], axis=0)
        m = jnp.tile(w, (t, 1))[:t * (2 * t - 1)].reshape(t, 2 * t - 1, -1)
        tiles.append(m[:, t - 1:2 * t - 1])
    far_neg = jnp.broadcast_to(table[t5_bucket(jnp.array(-2 * t))], tiles[0].shape)
    far_pos = jnp.broadcast_to(table[t5_bucket(jnp.array(2 * t))], tiles[0].shape)
    out = jnp.stack([far_neg] + tiles + [far_pos], axis=0)
    return jnp.transpose(out, (3, 0, 1, 2))


def _attn_operands(q, k, v, qg, kg):
    f32 = jnp.float32
    bf16 = jnp.bfloat16
    lane = lax.broadcasted_iota(jnp.int32, q.shape, 1)
    lo = lane < C_HEAD_DIM

    def half_sums(sq):
        return (jnp.sum(jnp.where(lo, sq, 0.0), axis=-1, keepdims=True),
                jnp.sum(jnp.where(lo, 0.0, sq), axis=-1, keepdims=True))

    def halfnorm(x, g):
        s_lo, s_hi = half_sums(x * x)
        return x * lax.rsqrt(jnp.where(lo, s_lo, s_hi) * (1.0 / C_HEAD_DIM) + EPS) * g

    def max_sq_norms(xb):
        n_lo, n_hi = half_sums(xb.astype(f32) * xb.astype(f32))
        return jnp.max(n_lo, axis=0, keepdims=True), jnp.max(n_hi, axis=0, keepdims=True)

    qn = halfnorm(q, qg) * (C_HEAD_DIM ** -0.5 * LOG2E)
    kn = halfnorm(k, kg)
    qb = qn.astype(bf16)
    kb = kn.astype(bf16)
    q2 = (jnp.where(lo, qb, 0.0).astype(bf16), jnp.where(lo, 0.0, qb).astype(bf16))
    q_lo, q_hi = max_sq_norms(qb)
    k_lo, k_hi = max_sq_norms(kb)
    sub = lax.broadcasted_iota(jnp.int32, (SUBLANES, q.shape[1]), 0)
    stats = jnp.where(sub == 0, q_lo, jnp.where(sub == 1, q_hi, jnp.where(sub == 2, k_lo,
                      jnp.where(sub == 3, k_hi, 0.0))))
    return q2, kn.T.astype(bf16), v.astype(bf16), stats


def _odd_proj_kernel(x_ref, g_ref, w_ref, qg_ref, kg_ref, o_ref, q2_ref, kt_ref, vb_ref, st_ref):
    x = x_ref[...]
    y = (x * lax.rsqrt(jnp.mean(x * x, axis=-1, keepdims=True) + EPS) * g_ref[...]).astype(jnp.bfloat16)
    hw = 2 * C_HEAD_DIM
    q, k, v = (jnp.dot(y, w_ref[:, p * C_WIDTH:(p + 1) * C_WIDTH], preferred_element_type=jnp.float32)
               for p in range(3))
    for h in range(C_HEADS):
        cols = slice(h * hw, (h + 1) * hw)
        q2, kt, vb, stats = _attn_operands(q[:, cols], k[:, cols], v[:, cols], qg_ref[...], kg_ref[...])
        q2_ref[h, 0] = q2[0]
        q2_ref[h, 1] = q2[1]
        kt_ref[h] = kt
        vb_ref[h] = vb
        st_ref[h] = stats
    att = 3 * C_WIDTH
    for c0 in range(att, w_ref.shape[1], PROJ_COLS):
        c1 = min(c0 + PROJ_COLS, w_ref.shape[1])
        o_ref[:, c0 - att:c1 - att] = jnp.dot(y, w_ref[:, c0:c1], preferred_element_type=jnp.float32)


def odd_in_proj(x, gain, w, q_gain, k_gain, *, tm=PROJ_TM):
    bsz, l, kdim = x.shape
    m = w.shape[1]
    rest = m - 3 * C_WIDTH
    tm = min(tm, l)
    hw = 2 * C_HEAD_DIM
    gq = jnp.tile(q_gain.astype(jnp.float32), 2).reshape(1, hw)
    gk = jnp.tile(k_gain.astype(jnp.float32), 2).reshape(1, hw)
    vec = pl.BlockSpec((1, hw), lambda b, i: (0, 0))
    return pl.pallas_call(
        _odd_proj_kernel,
        grid=(bsz, l // tm),
        in_specs=[pl.BlockSpec((None, tm, kdim), lambda b, i: (b, i, 0)),
                  pl.BlockSpec((1, kdim), lambda b, i: (0, 0)),
                  pl.BlockSpec((kdim, m), lambda b, i: (0, 0), pipeline_mode=pl.Buffered(1)),
                  vec, vec],
        out_specs=[pl.BlockSpec((None, tm, rest), lambda b, i: (b, i, 0)),
                   pl.BlockSpec((None, C_HEADS, 2, tm, hw), lambda b, i: (b, 0, 0, i, 0)),
                   pl.BlockSpec((None, C_HEADS, hw, tm), lambda b, i: (b, 0, 0, i)),
                   pl.BlockSpec((None, C_HEADS, tm, hw), lambda b, i: (b, 0, i, 0)),
                   pl.BlockSpec((None, C_HEADS, None, SUBLANES, hw), lambda b, i: (b, 0, i, 0, 0))],
        out_shape=[jax.ShapeDtypeStruct((bsz, l, rest), jnp.float32),
                   jax.ShapeDtypeStruct((bsz, C_HEADS, 2, l, hw), jnp.bfloat16),
                   jax.ShapeDtypeStruct((bsz, C_HEADS, hw, l), jnp.bfloat16),
                   jax.ShapeDtypeStruct((bsz, C_HEADS, l, hw), jnp.bfloat16),
                   jax.ShapeDtypeStruct((bsz, C_HEADS, l // tm, SUBLANES, hw), jnp.float32)],
        compiler_params=pltpu.CompilerParams(dimension_semantics=("parallel", "parallel"),
                                             vmem_limit_bytes=VMEM_LIMIT_BYTES),
        name="odd_in_proj",
    )(x, gain.reshape(1, kdim).astype(jnp.float32), w.astype(jnp.bfloat16), gq, gk)


ATT_ROWS = 64
ATT_SAFE_GAP = 100.0


def _attn_kernel(lam_ref, kmax_ref, bmax_ref, q2_ref, kt_ref, v_ref, bias_ref, g_ref, o_ref,
                 m_sc, l_sc, acc_sc, s_sc, p_sc, a_sc, *, t, nk, out_scale, bounded):
    f32 = jnp.float32
    b = pl.program_id(0)
    h = pl.program_id(1)
    qi = pl.program_id(2)
    q2 = q2_ref[...].reshape(2 * t, 2 * C_HEAD_DIM)
    r = ATT_ROWS
    hw = 2 * C_HEAD_DIM
    if bounded:
        q2f = q2.astype(f32)
        nq = jnp.sqrt(jnp.sum(q2f * q2f, axis=-1, keepdims=True))
        row = lax.broadcasted_iota(jnp.int32, nq.shape, 0)
        kc = jnp.where(row < t, kmax_ref[(b * C_HEADS + h) * 2], kmax_ref[(b * C_HEADS + h) * 2 + 1])
        m_sc[...] = jnp.broadcast_to(nq * kc + bmax_ref[h], m_sc.shape)
    else:
        m_sc[...] = jnp.full(m_sc.shape, -jnp.inf, f32)
    l_sc[...] = jnp.zeros_like(l_sc)
    acc_sc[...] = jnp.zeros_like(acc_sc)

    def body(ki, carry):
        off = pl.multiple_of(ki * t, t)
        bidx = jnp.clip(ki - qi, -2, 2) + 2
        s_sc[...] = jnp.dot(q2, kt_ref[:, pl.ds(off, t)], preferred_element_type=f32)
        for g in range(2 * t // r):
            rows = slice(g * r, (g + 1) * r)
            brow = (g * r) % t
            s = s_sc[rows, :] + bias_ref[bidx, brow:brow + r, :]
            m = m_sc[rows, :]
            if not bounded:
                m_prev = m
                m = jnp.maximum(m_prev, jnp.max(s, axis=-1, keepdims=True))
                alpha = jnp.exp2(m_prev - m)
                m_sc[rows, :] = m
                a_sc[rows, :] = alpha
            ps = [jnp.exp2(s[:, j * hw:(j + 1) * hw] - m) for j in range(t // hw)]
            psum = jnp.sum(sum(ps), axis=-1, keepdims=True)
            l_sc[rows, :] = (l_sc[rows, :] if bounded else alpha * l_sc[rows, :]) + psum
            for j in range(t // hw):
                p_sc[rows, j * hw:(j + 1) * hw] = ps[j].astype(jnp.bfloat16)
        pv = jnp.dot(p_sc[...], v_ref[pl.ds(off, t), :], preferred_element_type=f32)
        acc_sc[...] = (acc_sc[...] if bounded else a_sc[...] * acc_sc[...]) + pv
        return carry

    lax.fori_loop(0, nk, body, 0)
    a = acc_sc[...] / l_sc[...]
    o = a[:t] - lam_ref[0] * a[t:]
    y = o * lax.rsqrt(jnp.mean(o * o, axis=-1, keepdims=True) + EPS)
    o_ref[...] = y * g_ref[...] * out_scale


def diff_attention(q2, kt, vb, stats, lam, out_gain, bias5, layer_idx):
    f32 = jnp.float32
    bsz, _, l, _ = vb.shape
    t = ATT_T
    hw = 2 * C_HEAD_DIM
    lam_init = 0.8 - 0.6 * math.exp(-0.3 * layer_idx)
    lam_f = lam.astype(f32)
    lam_full = jnp.exp(jnp.sum(lam_f[0] * lam_f[1])) - jnp.exp(jnp.sum(lam_f[2] * lam_f[3])) + lam_init
    norms = jnp.sqrt(jnp.max(stats[..., 0:4, 0], axis=2)) * (1.0 + 1e-3)
    qmax, kmax = norms[..., 0:2], norms[..., 2:4]
    bmax = jnp.max(bias5, axis=(1, 2, 3))
    bmin = jnp.min(bias5, axis=(1, 2, 3))
    gap = 2.0 * qmax * kmax + (bmax - bmin)[None, :, None]
    smem = pl.BlockSpec(memory_space=pltpu.SMEM)

    def run(bounded):
        return pl.pallas_call(
            functools.partial(_attn_kernel, t=t, nk=l // t, out_scale=1.0 - lam_init, bounded=bounded),
            grid=(bsz, C_HEADS, l // t),
            in_specs=[smem, smem, smem,
                      pl.BlockSpec((None, None, 2, t, hw), lambda b, h, i: (b, h, 0, i, 0)),
                      pl.BlockSpec((None, None, hw, l), lambda b, h, i: (b, h, 0, 0)),
                      pl.BlockSpec((None, None, l, hw), lambda b, h, i: (b, h, 0, 0)),
                      pl.BlockSpec((None, 5, t, t), lambda b, h, i: (h, 0, 0, 0)),
                      pl.BlockSpec((1, hw), lambda b, h, i: (0, 0))],
            out_specs=pl.BlockSpec((None, t, hw), lambda b, h, i: (b, i, h)),
            out_shape=jax.ShapeDtypeStruct((bsz, l, C_WIDTH), f32),
            scratch_shapes=[pltpu.VMEM((2 * t, hw), f32), pltpu.VMEM((2 * t, hw), f32), pltpu.VMEM((2 * t, hw), f32),
                            pltpu.VMEM((2 * t, t), f32), pltpu.VMEM((2 * t, t), jnp.bfloat16),
                            pltpu.VMEM((2 * t, hw), f32)],
            compiler_params=pltpu.CompilerParams(dimension_semantics=("parallel", "parallel", "arbitrary"),
                                                 vmem_limit_bytes=VMEM_LIMIT_BYTES),
            name="diff_attention_bounded" if bounded else "diff_attention_online",
        )(lam_full.reshape(1), kmax.reshape(-1), bmax, q2, kt, vb, bias5, out_gain.reshape(1, hw).astype(f32))

    return lax.cond(jnp.all(gap < ATT_SAFE_GAP), lambda: run(True), lambda: run(False))


GDN_TT = 512
GDN_HEADS_PER_STEP = 4
LANES = 128
OD_QKV_BLOCK = 0
OD_GATE_BLOCK = OD_QKV_BLOCK + 3 * D_WIDTH // LANES
OD_AB_BLOCK = OD_GATE_BLOCK + D_WIDTH // LANES
OD_COLS = 2304


def _gdn_prep_kernel(prev_ref, cur_ref, next_ref, w_ref, o_ref, *, tl, nl):
    i = pl.program_id(1)
    part = pl.program_id(2)
    prev = jnp.where(i > 0, prev_ref[...], 0.0)
    nxt = jnp.where(i < nl - 1, next_ref[...], 0.0)
    ext = jnp.concatenate([prev, cur_ref[...], nxt], axis=0)
    halo = prev.shape[0]
    acc = None
    for j in range(CONV_WIDTH):
        start = halo - CONV_WIDTH // 2 + j
        term = w_ref[j:j + 1, :] * ext[start:start + tl, :]
        acc = term if acc is None else acc + term
    y = acc * jax.nn.sigmoid(acc)
    scale = jnp.where(part == 0, D_HEAD_DIM ** -0.5, 1.0)
    heads = []
    for h in range(D_HEADS):
        yh = y[:, h * LANES:(h + 1) * LANES]
        heads.append(yh * (lax.rsqrt(jnp.sum(yh * yh, axis=-1, keepdims=True) + EPS) * scale))
    o_ref[...] = jnp.where(part < 2, jnp.concatenate(heads, axis=1), y)


def gdn_prep(proj3, conv_w, *, tl=512):
    bsz, l, _ = proj3.shape
    halo = SUBLANES
    nl = l // tl
    blk0 = OD_QKV_BLOCK * LANES // D_WIDTH
    return pl.pallas_call(
        functools.partial(_gdn_prep_kernel, tl=tl, nl=nl),
        grid=(bsz, nl, 3),
        in_specs=[pl.BlockSpec((None, halo, D_WIDTH), lambda b, i, p: (b, jnp.maximum(i * (tl // halo) - 1, 0), blk0 + p)),
                  pl.BlockSpec((None, tl, D_WIDTH), lambda b, i, p: (b, i, blk0 + p)),
                  pl.BlockSpec((None, halo, D_WIDTH),
                               lambda b, i, p: (b, jnp.minimum((i + 1) * (tl // halo), l // halo - 1), blk0 + p)),
                  pl.BlockSpec((CONV_WIDTH, D_WIDTH), lambda b, i, p: (0, p))],
        out_specs=pl.BlockSpec((None, None, tl, D_WIDTH), lambda b, i, p: (p, b, i, 0)),
        out_shape=jax.ShapeDtypeStruct((3, bsz, l, D_WIDTH), jnp.float32),
        compiler_params=pltpu.CompilerParams(dimension_semantics=("parallel", "parallel", "parallel"),
                                             vmem_limit_bytes=VMEM_LIMIT_BYTES),
        name="gdn_prep",
    )(proj3, proj3, proj3, conv_w.astype(jnp.float32))


def _gdn_gates_kernel(x_ref, nega_ref, dtb_ref, o_ref):
    x = x_ref[...]
    z = x + dtb_ref[...]
    g = nega_ref[...] * (jnp.maximum(z, 0.0) + jnp.log1p(jnp.exp(-jnp.abs(z))))
    lane = lax.broadcasted_iota(jnp.int32, x.shape, 1)
    y = jnp.where(lane < 2 * D_HEADS, g, jax.nn.sigmoid(x))
    o_ref[...] = y.T[0:4 * D_HEADS, :]


def gdn_gates(proj3, a_log, dt_bias, *, tl=512):
    bsz, l, _ = proj3.shape
    pad = LANES - 2 * D_HEADS
    nega = jnp.pad(-jnp.exp(a_log.astype(jnp.float32)).reshape(1, -1), ((0, 0), (0, pad)))
    dtb = jnp.pad(dt_bias.astype(jnp.float32).reshape(1, -1), ((0, 0), (0, pad)))
    vec = pl.BlockSpec((1, LANES), lambda b, i: (0, 0))
    return pl.pallas_call(
        _gdn_gates_kernel,
        grid=(bsz, l // tl),
        in_specs=[pl.BlockSpec((None, tl, LANES), lambda b, i: (b, i, OD_AB_BLOCK)), vec, vec],
        out_specs=pl.BlockSpec((None, 4 * D_HEADS, tl), lambda b, i: (b, 0, i)),
        out_shape=jax.ShapeDtypeStruct((bsz, 4 * D_HEADS, l), jnp.float32),
        compiler_params=pltpu.CompilerParams(dimension_semantics=("parallel", "parallel"),
                                             vmem_limit_bytes=VMEM_LIMIT_BYTES),
        name="gdn_gates",
    )(proj3, nega, dtb)


def gdn_constants():
    import numpy as np
    c = CHUNK
    r = np.arange(c)[:, None]
    u = np.arange(c)[None, :]
    cum, incl, strict = [], [], []
    for direction in range(2):
        fwd = direction == 0
        cum.append(np.concatenate([(r <= u) if fwd else (r >= u), np.ones((c, c), bool)], axis=1))
        incl.append((u <= r) if fwd else (u >= r))
        strict.append((u < r) if fwd else (u > r))
    same = lambda b: (r // b) == (u // b)
    merges = [same(2 * b) & ~same(b) for b in (8, 16, 32)]
    f32 = jnp.float32
    return (jnp.asarray(np.stack(cum), jnp.bfloat16), jnp.asarray(np.stack(incl), f32),
            jnp.asarray(np.stack(strict), f32), jnp.asarray(same(8), f32), jnp.asarray(np.stack(merges), f32))


def _gdn_kernel(q_ref, k_ref, v_ref, g_ref, b_ref, cum_ref, incl_ref, strict_ref, d8_ref, mrg_ref, o_ref,
                s_sc, qd_sc, dec_sc, w_sc, u_sc, sin_sc, *, nc):
    direction = pl.program_id(2)

    @pl.when(pl.program_id(3) == 0)
    def _():
        s_sc[...] = jnp.zeros_like(s_sc)

    bf16 = jnp.bfloat16
    f32 = jnp.float32
    c = CHUNK
    hd = D_HEAD_DIM
    contract_last = (((1,), (1,)), ((), ()))
    contract_first = (((0,), (0,)), ((), ()))
    cumm = cum_ref[...]
    incl = incl_ref[...]
    strict = strict_ref[...]
    d8 = d8_ref[...]
    eye = (lax.broadcasted_iota(jnp.int32, (c, c), 0) == lax.broadcasted_iota(jnp.int32, (c, c), 1)).astype(f32)

    def mm(a, b):
        return jnp.dot(a.astype(bf16), b.astype(bf16), preferred_element_type=f32)

    def rep(x):
        return jnp.concatenate([x] * (hd // c), axis=1)

    nh = g_ref.shape[0]
    ns = range(nh * nc)
    head = [m // nc for m in ns]
    rows = [slice((m % nc) * c, (m % nc + 1) * c) for m in ns]
    cols = [slice(h * hd, (h + 1) * hd) for h in head]
    q = [q_ref[rows[m], cols[m]] for m in ns]
    k = [k_ref[rows[m], cols[m]] for m in ns]
    v = [v_ref[rows[m], cols[m]] for m in ns]
    kb = [x.astype(bf16) for x in k]
    kk = [lax.dot_general(x, x, contract_last, preferred_element_type=f32) for x in kb]
    qk = [lax.dot_general(q[n].astype(bf16), kb[n], contract_last, preferred_element_type=f32) for n in ns]
    grow = [jnp.broadcast_to(g_ref[head[m], :, rows[m]], (c, c)) for m in ns]
    ghi = [x.astype(bf16) for x in grow]
    glo = [(grow[n] - ghi[n].astype(f32)).astype(bf16) for n in ns]
    gm = [jnp.dot(ghi[n], cumm, preferred_element_type=f32) + jnp.dot(glo[n], cumm, preferred_element_type=f32)
          for n in ns]
    gam_row = [x[:, :c] for x in gm]
    tot = [x[:, c:] for x in gm]
    gam_col = [x.T for x in gam_row]
    beta_col = [jnp.broadcast_to(b_ref[head[m], :, rows[m]], (c, c)).T for m in ns]
    decay = [incl * jnp.exp(jnp.minimum(gam_col[n] - gam_row[n], 0.0)) for n in ns]
    a = [strict * beta_col[n] * kk[n] * decay[n] for n in ns]
    a0 = [x * d8 for x in a]
    n2 = [mm(x, x) for x in a0]
    n4 = [mm(x, x) for x in n2]
    t = [mm(eye - a0[n], eye + n2[n]) for n in ns]
    t = [mm(t[n], eye + n4[n]) for n in ns]
    for j in range(mrg_ref.shape[0]):
        p = [mm(a[n] * mrg_ref[j], t[n]) for n in ns]
        t = [t[n] - mm(t[n], p[n]) for n in ns]
    beta128 = [rep(x) for x in beta_col]
    egam128 = [rep(jnp.exp(x)) for x in gam_col]
    solb = [mm(t[n], jnp.concatenate([k[n] * beta128[n] * egam128[n], v[n] * beta128[n]], axis=1)).astype(bf16)
            for n in ns]
    av = [jnp.dot((qk[n] * decay[n]).astype(bf16), solb[n], preferred_element_type=f32) for n in ns]
    k_dec = [(k[n] * rep(jnp.exp(tot[n] - gam_col[n]))).astype(bf16) for n in ns]
    wu = [lax.dot_general(k_dec[n], solb[n], contract_first, preferred_element_type=f32) for n in ns]
    for n in ns:
        qd_sc[n] = (q[n] * egam128[n] - av[n][:, :hd]).astype(bf16)
        o_ref[rows[n], cols[n]] = av[n][:, hd:]
        w_sc[n] = wu[n][:, :hd].astype(bf16)
        u_sc[n] = wu[n][:, hd:]
        dec_sc[n] = rep(jnp.exp(tot[n][0:SUBLANES, :]))

    def body(ci, states):
        ce = ci + direction * (nc - 1 - 2 * ci)
        new = []
        for h in range(nh):
            m = h * nc + ce
            sb = states[h].astype(bf16)
            sin_sc[m] = sb
            new.append(states[h] * dec_sc[m][0:1] - jnp.dot(w_sc[m], sb, preferred_element_type=f32) + u_sc[m])
        return tuple(new)

    states = lax.fori_loop(0, nc, body, tuple(s_sc[h] for h in range(nh)))
    for h in range(nh):
        s_sc[h] = states[h]

    for m in ns:
        o_ref[rows[m], cols[m]] += jnp.dot(qd_sc[m], sin_sc[m], preferred_element_type=f32)


def gdn_scan(qkv, gb):
    _, bsz, l, _ = qkv.shape
    hd = D_HEAD_DIM
    tt = min(GDN_TT, l)
    nt = l // tt
    nc = tt // CHUNK
    assert l % tt == 0 and tt % CHUNK == 0
    consts = gdn_constants()
    gb4 = gb.reshape(bsz, 4 * D_HEADS, 1, l)
    nh = GDN_HEADS_PER_STEP
    assert D_HEADS % nh == 0
    tidx = lambda d, i: i + d * (nt - 1 - 2 * i)
    qkv_spec = lambda p: pl.BlockSpec((None, None, tt, nh * hd), lambda b, h, d, i: (p, b, tidx(d, i), h))
    row_spec = lambda off: pl.BlockSpec((None, nh, 1, tt),
                                        lambda b, h, d, i: (b, (off + d * D_HEADS) // nh + h, 0, tidx(d, i)))
    per_dir = lambda a: pl.BlockSpec((None,) + a.shape[1:], lambda b, h, d, i: (d,) + (0,) * (a.ndim - 1))
    whole = lambda a: pl.BlockSpec(a.shape, lambda b, h, d, i: (0,) * a.ndim)
    return pl.pallas_call(
        functools.partial(_gdn_kernel, nc=nc),
        grid=(bsz, D_HEADS // nh, 2, nt),
        in_specs=[qkv_spec(0), qkv_spec(1), qkv_spec(2), row_spec(0), row_spec(2 * D_HEADS),
                  per_dir(consts[0]), per_dir(consts[1]), per_dir(consts[2]), whole(consts[3]), whole(consts[4])],
        out_specs=pl.BlockSpec((None, None, tt, nh * hd), lambda b, h, d, i: (d, b, tidx(d, i), h)),
        out_shape=jax.ShapeDtypeStruct((2, bsz, l, D_WIDTH), jnp.float32),
        scratch_shapes=[pltpu.VMEM((nh, hd, hd), jnp.float32),
                        pltpu.VMEM((nh * nc, CHUNK, hd), jnp.bfloat16),
                        pltpu.VMEM((nh * nc, SUBLANES, hd), jnp.float32),
                        pltpu.VMEM((nh * nc, hd, hd), jnp.bfloat16),
                        pltpu.VMEM((nh * nc, hd, hd), jnp.float32),
                        pltpu.VMEM((nh * nc, hd, hd), jnp.bfloat16)],
        compiler_params=pltpu.CompilerParams(dimension_semantics=("parallel", "parallel", "parallel", "arbitrary"),
                                             vmem_limit_bytes=VMEM_LIMIT_BYTES),
        name="gdn_scan",
    )(qkv, qkv, qkv, gb4, gb4, *consts)


def gated_deltanet(proj3, conv_w, a_log, dt_bias):
    return gdn_scan(gdn_prep(proj3, conv_w), gdn_gates(proj3, a_log, dt_bias))


MOE_TT = 512
MOE_SUB = 128
MOE_ALIGN = 64
MOE_TILES_PER_STEP = 2
MOE_ROWS = 256
MOE_SLAB = 256
MOE_VMEM_LIMIT_BYTES = 56 * 1024 * 1024


def _router_kernel(x_ref, g_ref, wr_ref, h_ref, aff_ref):
    x = x_ref[...]
    h = (x * lax.rsqrt(jnp.mean(x * x, axis=-1, keepdims=True) + EPS) * g_ref[...]).astype(jnp.bfloat16)
    h_ref[...] = h
    logits = jnp.dot(h, wr_ref[...], preferred_element_type=jnp.float32)
    lane = lax.broadcasted_iota(jnp.int32, logits.shape, 1)
    logits = jnp.where(lane < N_EXPERTS, logits, -jnp.inf)
    p = jnp.exp(logits - jnp.max(logits, axis=-1, keepdims=True))
    aff = p / jnp.sum(p, axis=-1, keepdims=True)
    aff_ref[...] = aff.T[0:N_EXPERTS, :]


def moe_route(x, gain, w_router, *, tm=MOE_TT):
    bsz, l, d = x.shape
    wr = jnp.pad(w_router.astype(jnp.bfloat16), ((0, 0), (0, LANES - N_EXPERTS)))
    return pl.pallas_call(
        _router_kernel,
        grid=(bsz, l // tm),
        in_specs=[pl.BlockSpec((None, tm, d), lambda b, i: (b, i, 0)),
                  pl.BlockSpec((1, d), lambda b, i: (0, 0)),
                  pl.BlockSpec((d, LANES), lambda b, i: (0, 0))],
        out_specs=[pl.BlockSpec((None, tm, d), lambda b, i: (b, i, 0)),
                   pl.BlockSpec((None, N_EXPERTS, tm), lambda b, i: (b, 0, i))],
        out_shape=[jax.ShapeDtypeStruct((bsz, l, d), jnp.bfloat16),
                   jax.ShapeDtypeStruct((bsz, N_EXPERTS, l), jnp.float32)],
        compiler_params=pltpu.CompilerParams(dimension_semantics=("parallel", "parallel"),
                                             vmem_limit_bytes=VMEM_LIMIT_BYTES),
        name="moe_router",
    )(x, gain.reshape(1, d).astype(jnp.float32), wr)


def _select_kernel(aff_ref, pre_ref, smap_ref, gate_ref, cnt_ref, *, cap, tt):
    f32 = jnp.float32
    bf16 = jnp.bfloat16
    aff = aff_ref[...]
    e, l = aff.shape
    nl = l // LANES
    tiles = [slice(j * LANES, (j + 1) * LANES) for j in range(nl)]
    bits = pltpu.bitcast(aff, jnp.int32)
    bt = [bits[:, s] for s in tiles]

    def lane_total(x):
        return jnp.broadcast_to(jnp.sum(x, axis=-1, keepdims=True), (e, LANES))

    def search(i, thr):
        cand = thr | jnp.left_shift(jnp.int32(1), 30 - i)
        acc = jnp.zeros((e, LANES), jnp.int32)
        for x in bt:
            acc = acc + (x >= cand).astype(jnp.int32)
        return jnp.where(lane_total(acc) >= cap, cand, thr)

    thr = lax.fori_loop(0, 31, search, jnp.zeros((e, LANES), jnp.int32))
    gt = [x > thr for x in bt]
    eq = [x == thr for x in bt]
    acc = jnp.zeros((e, LANES), jnp.int32)
    for x in gt:
        acc = acc + x.astype(jnp.int32)
    need = (cap - lane_total(acc)).astype(f32)

    pre = pre_ref[...]

    def prefix(flags):
        outs = [jnp.dot(jnp.where(x, 1.0, 0.0).astype(bf16), pre, preferred_element_type=f32) for x in flags]
        carry = jnp.zeros((e, LANES), f32)
        res = []
        for o in outs:
            res.append(o[:, :LANES] + carry)
            carry = carry + o[:, LANES:]
        return res, [o[:, LANES:] for o in outs]

    rank_eq, _ = prefix(eq)
    sel = [jnp.logical_or(gt[j], jnp.logical_and(eq[j], rank_eq[j] < need)) for j in range(nl)]
    pos, totals = prefix(sel)
    lane = lax.broadcasted_iota(jnp.int32, (e, LANES), 1)
    cnt = jnp.zeros((e, LANES), f32)
    per = tt // LANES
    for j in range(nl):
        smap_ref[:, tiles[j]] = jnp.where(sel[j], pos[j], -1.0)
        gate_ref[:, tiles[j]] = jnp.where(sel[j], aff[:, tiles[j]], 0.0)
        cnt = cnt + jnp.where(lane == j // per, totals[j], 0.0)
    cnt_ref[...] = cnt


def moe_select(aff, cap, *, tt=MOE_TT):
    import numpy as np
    bsz, e, l = aff.shape
    assert l // tt <= LANES
    i = np.arange(LANES)
    pre = np.concatenate([i[:, None] < i[None, :], np.ones((LANES, LANES), bool)], axis=1)
    row = pl.BlockSpec((None, e, l), lambda b: (b, 0, 0))
    return pl.pallas_call(
        functools.partial(_select_kernel, cap=cap, tt=tt),
        grid=(bsz,),
        in_specs=[row, pl.BlockSpec((LANES, 2 * LANES), lambda b: (0, 0))],
        out_specs=[row, row, pl.BlockSpec((None, e, LANES), lambda b: (b, 0, 0))],
        out_shape=[jax.ShapeDtypeStruct((bsz, e, l), jnp.float32), jax.ShapeDtypeStruct((bsz, e, l), jnp.float32),
                   jax.ShapeDtypeStruct((bsz, e, LANES), jnp.float32)],
        compiler_params=pltpu.CompilerParams(dimension_semantics=("parallel",),
                                             vmem_limit_bytes=VMEM_LIMIT_BYTES),
        name="moe_select",
    )(aff, jnp.asarray(pre, jnp.bfloat16))


def _slot_one_hot(pos, base, rows, n):
    slot = (base + lax.broadcasted_iota(jnp.int32, (rows, n), 0)).astype(jnp.float32)
    return jnp.where(pos == slot, 1.0, 0.0).astype(jnp.bfloat16)


def _expert_kernel(cs_ref, h_ref, smap_ref, gate_ref, wg32_ref, wu32_ref, wd32_ref, o_ref,
                   xs_sc, gs_sc, wg_ref, wu_ref, wd_ref, *, nj, per, tt, cap):
    e = pl.program_id(0)
    b = pl.program_id(1)
    j = pl.program_id(2)
    f32 = jnp.float32
    bf16 = jnp.bfloat16

    @pl.when(jnp.logical_and(b == 0, j == 0))
    def _():
        wg_ref[...] = wg32_ref[...].astype(bf16)
        wu_ref[...] = wu32_ref[...].astype(bf16)
        wd_ref[...] = wd32_ref[...].astype(bf16)

    @pl.when(j == 0)
    def _():
        xs_sc[...] = jnp.zeros_like(xs_sc)
        gs_sc[...] = jnp.zeros_like(gs_sc)

    win = min(MOE_SUB, cap)
    for s in range(per):
        base = (b * N_EXPERTS + e) * (nj * per + 1) + j * per + s
        c0 = cs_ref[base]
        c1 = cs_ref[base + 1]
        cols = slice(s * tt, (s + 1) * tt)
        pos = smap_ref[:, cols]
        gate = gate_ref[:, cols]
        hb = h_ref[cols, :]

        def gather(r0, rows, pos=pos, gate=gate, hb=hb):
            oh = _slot_one_hot(pos, r0, rows, tt)
            xs_sc[pl.ds(r0, rows), :] += jnp.dot(oh, hb, preferred_element_type=f32)
            g = jnp.sum(oh.astype(f32) * gate, axis=-1, keepdims=True)
            gs_sc[pl.ds(r0, rows), :] += jnp.broadcast_to(g, (rows, LANES))

        w0 = pl.multiple_of(jnp.minimum(c0 // MOE_ALIGN * MOE_ALIGN, cap - win), MOE_ALIGN)
        gather(w0, win)

        def rest(st, carry, gather=gather):
            gather(pl.multiple_of(st * MOE_ALIGN, MOE_ALIGN), MOE_ALIGN)
            return carry

        lax.fori_loop((w0 + win) // MOE_ALIGN, (c1 + MOE_ALIGN - 1) // MOE_ALIGN, rest, 0)

    @pl.when(j == nj - 1)
    def _():
        rows_per = min(MOE_ROWS, cap)
        for r in range(cap // rows_per):
            rows = slice(r * rows_per, (r + 1) * rows_per)
            xb = xs_sc[rows, :].astype(bf16)
            g = jnp.dot(xb, wg_ref[...], preferred_element_type=f32)
            u = jnp.dot(xb, wu_ref[...], preferred_element_type=f32)
            hid = (g * jax.nn.sigmoid(g) * u).astype(bf16)
            out = jnp.dot(hid, wd_ref[...], preferred_element_type=f32)
            scale = jnp.concatenate([gs_sc[rows, :]] * (out.shape[1] // LANES), axis=1)
            o_ref[rows, :] = (out * scale).astype(bf16)


def moe_experts(hb, smap, gate, cs, w_gate, w_up, w_down, layer, cap, *, tt=MOE_TT):
    bsz, l, d = hb.shape
    _, e, _, ff = w_gate.shape
    per = MOE_TILES_PER_STEP if (l // tt) % MOE_TILES_PER_STEP == 0 else 1
    nj = l // (tt * per)
    smap4 = smap.reshape(bsz, e, 1, l)
    gate4 = gate.reshape(bsz, e, 1, l)
    tok = pl.BlockSpec((None, None, 1, per * tt), lambda ei, b, j, cs_ref: (b, ei, 0, j))
    once = pl.Buffered(1)
    grid_spec = pltpu.PrefetchScalarGridSpec(
        num_scalar_prefetch=1,
        grid=(e, bsz, nj),
        in_specs=[pl.BlockSpec((None, per * tt, d), lambda ei, b, j, cs_ref: (b, j, 0)), tok, tok,
                  pl.BlockSpec((None, None, d, ff), lambda ei, b, j, cs_ref: (layer, ei, 0, 0), pipeline_mode=once),
                  pl.BlockSpec((None, None, d, ff), lambda ei, b, j, cs_ref: (layer, ei, 0, 0), pipeline_mode=once),
                  pl.BlockSpec((None, None, ff, d), lambda ei, b, j, cs_ref: (layer, ei, 0, 0), pipeline_mode=once)],
        out_specs=pl.BlockSpec((None, None, cap, d), lambda ei, b, j, cs_ref: (b, ei, 0, 0)),
        scratch_shapes=[pltpu.VMEM((cap, d), jnp.float32), pltpu.VMEM((cap, LANES), jnp.float32),
                        pltpu.VMEM((d, ff), jnp.bfloat16), pltpu.VMEM((d, ff), jnp.bfloat16),
                        pltpu.VMEM((ff, d), jnp.bfloat16)])
    return pl.pallas_call(
        functools.partial(_expert_kernel, nj=nj, per=per, tt=tt, cap=cap),
        grid_spec=grid_spec,
        out_shape=jax.ShapeDtypeStruct((bsz, e, cap, d), jnp.bfloat16),
        compiler_params=pltpu.CompilerParams(dimension_semantics=("parallel", "arbitrary", "arbitrary"),
                                             vmem_limit_bytes=MOE_VMEM_LIMIT_BYTES),
        name="moe_experts",
    )(cs, hb, smap4, gate4, w_gate, w_up, w_down)


def _combine_kernel(cs_ref, x_ref, smap_ref, ow_ref, y_ref, *, nj, tt):
    b = pl.program_id(0)
    e = pl.program_id(2)

    @pl.when(e == 0)
    def _():
        y_ref[...] = x_ref[...]

    contract_first = (((0,), (0,)), ((), ()))
    cap = ow_ref.shape[0]
    win = min(2 * MOE_SUB, cap)
    base = (b * N_EXPERTS + e) * (nj + 1)
    cols = [slice(j * tt, (j + 1) * tt) for j in range(nj)]
    pos = [smap_ref[:, c] for c in cols]
    r0 = [pl.multiple_of(jnp.minimum(cs_ref[base + j] // MOE_SUB * MOE_SUB, cap - win), MOE_SUB) for j in range(nj)]
    oh = [_slot_one_hot(pos[j], r0[j], win, tt) for j in range(nj)]
    add = [lax.dot_general(oh[j], ow_ref[pl.ds(r0[j], win), :], contract_first, preferred_element_type=jnp.float32)
           for j in range(nj)]
    for j in range(nj):
        y_ref[cols[j], :] += add[j]

    for j in range(nj):
        def scatter(st, carry):
            s0 = pl.multiple_of(st * MOE_SUB, MOE_SUB)
            y_ref[cols[j], :] += lax.dot_general(_slot_one_hot(pos[j], s0, MOE_SUB, tt),
                                                 ow_ref[pl.ds(s0, MOE_SUB), :], contract_first,
                                                 preferred_element_type=jnp.float32)
            return carry

        lax.fori_loop((r0[j] + win) // MOE_SUB, (cs_ref[base + j + 1] + MOE_SUB - 1) // MOE_SUB, scatter, 0)


def moe_combine(x, smap, outw, cs, *, tt=MOE_TT):
    bsz, l, d = x.shape
    e, cap = outw.shape[1:3]
    nj = l // tt
    smap4 = smap.reshape(bsz, e, 1, l)
    grid_spec = pltpu.PrefetchScalarGridSpec(
        num_scalar_prefetch=1,
        grid=(bsz, d // MOE_SLAB, e),
        in_specs=[pl.BlockSpec((None, l, MOE_SLAB), lambda b, s, ei, cs_ref: (b, 0, s)),
                  pl.BlockSpec((None, None, 1, l), lambda b, s, ei, cs_ref: (b, ei, 0, 0)),
                  pl.BlockSpec((None, None, cap, MOE_SLAB), lambda b, s, ei, cs_ref: (b, ei, 0, s))],
        out_specs=pl.BlockSpec((None, l, MOE_SLAB), lambda b, s, ei, cs_ref: (b, 0, s)))
    return pl.pallas_call(
        functools.partial(_combine_kernel, nj=nj, tt=tt),
        grid_spec=grid_spec,
        out_shape=jax.ShapeDtypeStruct((bsz, l, d), jnp.float32),
        compiler_params=pltpu.CompilerParams(dimension_semantics=("parallel", "parallel", "arbitrary"),
                                             vmem_limit_bytes=MOE_VMEM_LIMIT_BYTES),
        name="moe_combine",
    )(cs, x, smap4, outw)


def ec_moe_layer(x, gain, w_router, w_gate, w_up, w_down, layer):
    bsz, l, d = x.shape
    cap = EC_CAPACITY_FACTOR * l // N_EXPERTS
    tt = min(MOE_TT, l)
    nj = l // tt
    hb, aff = moe_route(x, gain, w_router, tm=tt)
    smap, gate, cnt = moe_select(aff, cap, tt=tt)
    cs = jnp.concatenate([jnp.zeros((bsz, N_EXPERTS, 1), jnp.float32), jnp.cumsum(cnt[..., :nj], axis=-1)], axis=-1)
    cs = cs.astype(jnp.int32).reshape(-1)
    outw = moe_experts(hb, smap, gate, cs, w_gate, w_up, w_down, layer, cap, tt=tt)
    return moe_combine(x, smap, outw, cs, tt=tt)


def kernel(x, mix_norm, ffn_norm, ev_w_in, ev_w_out, a_lb_logits, a_out_norm, s5_lambda_re, s5_lambda_im, s5_log_step, s5_b_re, s5_b_im, s5_c_re, s5_c_im, s5_d, s5_glu_w, s5_glu_b, od_w_in, od_w_out, c_q_norm, c_k_norm, c_lambda, c_out_norm, rel_bias, d_conv_w, d_a_log, d_dt_bias, d_out_norm, moe_router, moe_w_gate, moe_w_up, moe_w_down):
    bsz, l, d = x.shape
    p = jax.nn.softmax(a_lb_logits.astype(jnp.float32), axis=0)
    cum = jnp.cumsum(p, axis=0)
    lower_bounds = cum - cum[0:1]
    bias5 = rel_bias_tiles(rel_bias, ATT_T)
    for layer in range(DEPTH):
        j = layer // 2
        if layer % 2 == 0:
            proj, u_tb = norm_matmul(x, mix_norm[layer], ev_w_in[j], tail=B_WIDTH)
            o_a2 = hgrn2_scan(proj, lower_bounds[j])
            o_b = s5_mixer_tb(u_tb.reshape(l * bsz, B_WIDTH), bsz, s5_lambda_re[j], s5_lambda_im[j], s5_log_step[j],
                              s5_b_re[j], s5_b_im[j], s5_c_re[j], s5_c_im[j], s5_d[j], s5_glu_w[j], s5_glu_b[j])
            x = mixer_out_proj(o_a2, proj, 4 * A_HEADS, a_out_norm[j], o_b.reshape(l, bsz * B_WIDTH), ev_w_out[j], x,
                               bidir_first=True, other_time_major=True)
        else:
            o2 = 3 * C_WIDTH + 3 * D_WIDTH
            o4 = o2 + 4 * D_HEADS
            w = od_w_in[j]
            w_in = jnp.concatenate([w[:, :o2], w[:, o4:], w[:, o2:o4],
                                    jnp.zeros((d, 3 * C_WIDTH + OD_COLS - w.shape[1]), w.dtype)], axis=1)
            proj, q2, kt, vb, stats = odd_in_proj(x, mix_norm[layer], w_in, c_q_norm[j], c_k_norm[j])
            o_c = diff_attention(q2, kt, vb, stats, c_lambda[j], c_out_norm[j], bias5, layer)
            o_d2 = gated_deltanet(proj, d_conv_w[j], d_a_log[j], d_dt_bias[j])
            x = mixer_out_proj(o_d2, proj, OD_GATE_BLOCK, d_out_norm[j], o_c, od_w_out[j], x, bidir_first=False)
        x = ec_moe_layer(x, ffn_norm[layer], moe_router[layer], moe_w_gate, moe_w_up, moe_w_down, layer)
    return x
```

```python
import functools
import math

import jax
import jax.numpy as jnp
from jax import lax
from jax.experimental import pallas as pl
from jax.experimental.pallas import tpu as pltpu

D_MODEL = 1024
DEPTH = 4
MIX_WIDTH = D_MODEL
A_WIDTH = MIX_WIDTH // 2
A_HEAD_DIM = 128
A_HEADS = A_WIDTH // A_HEAD_DIM
B_WIDTH = MIX_WIDTH - A_WIDTH
S5_GROUP = 16
S5_GROUPS = B_WIDTH // S5_GROUP
S5_STATE = 64
C_WIDTH = MIX_WIDTH // 2
C_HEAD_DIM = 64
C_HEADS = C_WIDTH // (2 * C_HEAD_DIM)
C_V_DIM = 2 * C_HEAD_DIM
D_WIDTH = MIX_WIDTH - C_WIDTH
D_HEAD_DIM = 128
D_HEADS = D_WIDTH // D_HEAD_DIM
CONV_WIDTH = 5
N_EXPERTS = 16
EXPERT_FF = 2 * D_MODEL
EC_CAPACITY_FACTOR = 2
REL_BUCKETS = 32
REL_MAX_DIST = 128
CHUNK = 64
Q_BLOCK = 128
EPS = 1e-6

VMEM_LIMIT_BYTES = 48 * 1024 * 1024


PROJ_TM = 512
PROJ_COLS = 512


def _norm_matmul_kernel(x_ref, g_ref, w_ref, o_ref, *tail_ref, main):
    x = x_ref[...]
    y = (x * lax.rsqrt(jnp.mean(x * x, axis=-1, keepdims=True) + EPS) * g_ref[...]).astype(jnp.bfloat16)
    for c0 in range(0, main, PROJ_COLS):
        c1 = min(c0 + PROJ_COLS, main)
        o_ref[:, c0:c1] = jnp.dot(y, w_ref[:, c0:c1], preferred_element_type=jnp.float32)
    if tail_ref:
        tail_ref[0][...] = jnp.dot(y, w_ref[:, main:], preferred_element_type=jnp.float32)


def norm_matmul(x, gain, w, *, tail=0, tm=PROJ_TM):
    bsz, l, k = x.shape
    m = w.shape[1]
    main = m - tail
    tm = min(tm, l)
    out_shape = [jax.ShapeDtypeStruct((bsz, l, main), jnp.float32)]
    out_specs = [pl.BlockSpec((None, tm, main), lambda b, i: (b, i, 0))]
    if tail:
        out_shape.append(jax.ShapeDtypeStruct((l, bsz * tail), jnp.float32))
        out_specs.append(pl.BlockSpec((tm, tail), lambda b, i: (i, b)))
    outs = pl.pallas_call(
        functools.partial(_norm_matmul_kernel, main=main),
        grid=(bsz, l // tm),
        in_specs=[pl.BlockSpec((None, tm, k), lambda b, i: (b, i, 0)),
                  pl.BlockSpec((1, k), lambda b, i: (0, 0)),
                  pl.BlockSpec((k, m), lambda b, i: (0, 0), pipeline_mode=pl.Buffered(1))],
        out_specs=out_specs,
        out_shape=out_shape,
        compiler_params=pltpu.CompilerParams(dimension_semantics=("parallel", "parallel"),
                                             vmem_limit_bytes=VMEM_LIMIT_BYTES),
        name="norm_matmul",
    )(x, gain.reshape(1, k).astype(jnp.float32), w.astype(jnp.bfloat16))
    return outs if tail else outs[0]


def _mixer_out_kernel(of_ref, ob_ref, g_ref, gain_ref, other_ref, wb_ref, wo_ref, r_ref, o_ref):
    bf16 = jnp.bfloat16
    o = of_ref[...] + ob_ref[...]
    g = g_ref[...]
    gate = g * jax.nn.sigmoid(g)
    hd = gain_ref.shape[1]
    heads = []
    for h in range(o.shape[1] // hd):
        oh = o[:, h * hd:(h + 1) * hd]
        heads.append(oh * lax.rsqrt(jnp.mean(oh * oh, axis=-1, keepdims=True) + EPS) * gain_ref[...])
    y = (jnp.concatenate(heads, axis=1) * gate).astype(bf16)
    o_ref[...] = (r_ref[...] + jnp.dot(y, wb_ref[...], preferred_element_type=jnp.float32)
                  + jnp.dot(other_ref[...].astype(bf16), wo_ref[...], preferred_element_type=jnp.float32))


def mixer_out_proj(o2, proj3, gate_block, out_gain, other, w, res, *, bidir_first, other_time_major=False,
                   tm=PROJ_TM):
    _, bsz, l, k = o2.shape
    m = w.shape[1]
    tm = min(tm, l)
    wb = w.astype(jnp.bfloat16)
    w_bidir, w_other = (wb[:k], wb[k:]) if bidir_first else (wb[k:], wb[:k])
    other_spec = (pl.BlockSpec((tm, k), lambda b, i: (i, b)) if other_time_major
                  else pl.BlockSpec((None, tm, k), lambda b, i: (b, i, 0)))
    gb = gate_block * LANES // k
    row = pl.BlockSpec((None, tm, m), lambda b, i: (b, i, 0))
    wspec = pl.BlockSpec((k, m), lambda b, i: (0, 0))
    return pl.pallas_call(
        _mixer_out_kernel,
        grid=(bsz, l // tm),
        in_specs=[pl.BlockSpec((None, None, tm, k), lambda b, i: (0, b, i, 0)),
                  pl.BlockSpec((None, None, tm, k), lambda b, i: (1, b, i, 0)),
                  pl.BlockSpec((None, tm, k), lambda b, i: (b, i, gb)),
                  pl.BlockSpec((1, LANES), lambda b, i: (0, 0)),
                  other_spec, wspec, wspec, row],
        out_specs=row,
        out_shape=jax.ShapeDtypeStruct((bsz, l, m), jnp.float32),
        compiler_params=pltpu.CompilerParams(dimension_semantics=("parallel", "parallel"),
                                             vmem_limit_bytes=VMEM_LIMIT_BYTES),
        name="mixer_out_proj",
    )(o2, o2, proj3, out_gain.reshape(1, LANES).astype(jnp.float32), other, w_bidir, w_other, res)


HG_LEVELS = tuple(CHUNK >> (i + 1) for i in range(CHUNK.bit_length() - 1))
HG_TOT_ROWS = 8
HG_TT = 1024


def hgrn2_constants():
    import numpy as np
    c = CHUNK
    r = np.arange(c)[:, None]
    u = np.arange(c)[None, :]
    stacks, masks = [], []
    for direction in range(2):
        fwd = direction == 0
        lvl_masks = []
        for m in HG_LEVELS:
            blk = r // (2 * m)
            later = (r % (2 * m)) >= m
            lvl_masks.append((blk == blk.T) & (later & ~later.T if fwd else ~later & later.T))
        stacks.append(np.concatenate([(u <= r) if fwd else (u >= r), np.ones((HG_TOT_ROWS, c), bool)], axis=0))
        masks.append(np.stack(lvl_masks))
    return (jnp.asarray(np.stack(stacks), jnp.bfloat16), jnp.asarray(np.stack(masks), jnp.float32))


def _hgrn2_kernel(q_ref, f_ref, v_ref, loglb_ref, log1mlb_ref, onemlb_ref, ast_ref, mask_ref, o_ref,
                  st_sc, qd_sc, dec_sc, upd_sc, sin_sc, *, nc):
    direction = pl.program_id(2)

    @pl.when(pl.program_id(3) == 0)
    def _():
        st_sc[...] = jnp.zeros_like(st_sc)

    bf16 = jnp.bfloat16
    f32 = jnp.float32
    c = CHUNK
    hd = A_HEAD_DIM
    dirf = direction.astype(f32)
    loglb = loglb_ref[...]
    log1mlb = log1mlb_ref[...]
    onemlb = onemlb_ref[...]
    ast = ast_ref[...]
    contract_last = (((1,), (1,)), ((), ()))
    contract_first = (((0,), (0,)), ((), ()))

    ns = range(nc)
    rows = [slice(n * c, (n + 1) * c) for n in ns]
    z = [f_ref[r, :] for r in rows]
    v = [v_ref[r, :] for r in rows]
    qr = [q_ref[r, :] for r in rows]
    q = [x * jax.nn.sigmoid(x) for x in qr]
    e = [jnp.exp(-jnp.abs(x)) for x in z]
    cc = [log1mlb + jnp.minimum(z[n], 0.0) - jnp.log1p(e[n]) for n in ns]
    lf = [jnp.maximum(loglb, x) + jnp.log1p(jnp.exp(-jnp.abs(loglb - x))) for x in cc]
    k = [onemlb * jnp.where(z[n] >= 0, e[n], 1.0) / (1.0 + e[n]) for n in ns]
    hi = [x.astype(bf16) for x in lf]
    lo = [(lf[n] - hi[n].astype(f32)).astype(bf16) for n in ns]
    d = [jnp.dot(ast, hi[n], preferred_element_type=f32) + jnp.dot(ast, lo[n], preferred_element_type=f32)
         for n in ns]
    cum = [x[0:c] for x in d]
    tot = [x[c:c + HG_TOT_ROWS] for x in d]
    ref = [cum[n] - dirf * lf[n] for n in ns]
    attn = [jnp.zeros((c, c), f32) for _ in ns]
    for li, m in enumerate(HG_LEVELS):
        nb = c // (2 * m)
        split = [jnp.broadcast_to(x.reshape(nb, 2 * m, hd)[:, m - 1:m, :], (nb, 2 * m, hd)).reshape(c, hd)
                 for x in ref]
        x = [jnp.exp(-jnp.abs(cum[n] - split[n])) for n in ns]
        s = [lax.dot_general((q[n] * x[n]).astype(bf16), (k[n] * x[n]).astype(bf16), contract_last,
                             preferred_element_type=f32) for n in ns]
        attn = [attn[n] + mask_ref[li] * s[n] for n in ns]
    vb = [x.astype(bf16) for x in v]
    intra = [jnp.dot(attn[n].astype(bf16), vb[n], preferred_element_type=f32) for n in ns]
    upd = [lax.dot_general(vb[n], (k[n] * jnp.exp(tot[n][0:1] - cum[n])).astype(bf16), contract_first,
                           preferred_element_type=f32) for n in ns]
    for n in ns:
        o_ref[rows[n], :] = intra[n] + jnp.sum(q[n] * k[n], axis=-1, keepdims=True) * v[n]
        qd_sc[n] = (q[n] * jnp.exp(cum[n])).astype(bf16)
        dec_sc[n] = jnp.exp(tot[n])
        upd_sc[n] = upd[n]

    def body(ci, st):
        ce = ci + direction * (nc - 1 - 2 * ci)
        sin_sc[ce] = st.astype(bf16)
        return st * dec_sc[ce][0:1] + upd_sc[ce]

    st_sc[...] = lax.fori_loop(0, nc, body, st_sc[...])

    for n in range(nc):
        rows = slice(n * c, (n + 1) * c)
        o_ref[rows, :] += lax.dot_general(qd_sc[n], sin_sc[n], contract_last, preferred_element_type=f32)


def hgrn2_scan(proj3, lb):
    bsz, l, _ = proj3.shape
    hd = A_HEAD_DIM
    tt = min(HG_TT, l)
    nt = l // tt
    assert l % tt == 0 and tt % CHUNK == 0
    ast, masks = hgrn2_constants()
    lb = lb.astype(jnp.float32)
    vecs = [jnp.log(lb).reshape(2, 1, A_WIDTH), jnp.log1p(-lb).reshape(2, 1, A_WIDTH), (1.0 - lb).reshape(2, 1, A_WIDTH)]
    tidx = lambda d, i: i + d * (nt - 1 - 2 * i)
    vec = pl.BlockSpec((None, 1, hd), lambda b, h, d, i: (d, 0, h))
    return pl.pallas_call(
        functools.partial(_hgrn2_kernel, nc=tt // CHUNK),
        grid=(bsz, A_HEADS, 2, nt),
        in_specs=[pl.BlockSpec((None, tt, hd), lambda b, h, d, i: (b, tidx(d, i), h)),
                  pl.BlockSpec((None, tt, hd), lambda b, h, d, i: (b, tidx(d, i), (1 + d) * A_HEADS + h)),
                  pl.BlockSpec((None, tt, hd), lambda b, h, d, i: (b, tidx(d, i), 3 * A_HEADS + h)),
                  vec, vec, vec,
                  pl.BlockSpec((None,) + ast.shape[1:], lambda b, h, d, i: (d, 0, 0)),
                  pl.BlockSpec((None,) + masks.shape[1:], lambda b, h, d, i: (d, 0, 0, 0))],
        out_specs=pl.BlockSpec((None, None, tt, hd), lambda b, h, d, i: (d, b, tidx(d, i), h)),
        out_shape=jax.ShapeDtypeStruct((2, bsz, l, A_WIDTH), jnp.float32),
        scratch_shapes=[pltpu.VMEM((hd, hd), jnp.float32),
                        pltpu.VMEM((tt // CHUNK, CHUNK, hd), jnp.bfloat16),
                        pltpu.VMEM((tt // CHUNK, HG_TOT_ROWS, hd), jnp.float32),
                        pltpu.VMEM((tt // CHUNK, hd, hd), jnp.float32),
                        pltpu.VMEM((tt // CHUNK, hd, hd), jnp.bfloat16)],
        compiler_params=pltpu.CompilerParams(dimension_semantics=("parallel", "parallel", "parallel", "arbitrary"),
                                             vmem_limit_bytes=VMEM_LIMIT_BYTES),
        name="hgrn2_scan",
    )(proj3, proj3, proj3, *vecs, ast, masks)


S5_NS = S5_GROUPS * S5_STATE
S5_TT = 128
SUBLANES = 8


def _s5_scan_kernel(u_ref, win_ref, ar_ref, ai_ref, wout_ref, y_ref, bu_sc, xs_sc, st_sc, *, bsz, tt, reverse):
    @pl.when(pl.program_id(0) == 0)
    def _():
        st_sc[...] = jnp.zeros_like(st_sc)

    ub = u_ref[...].astype(jnp.bfloat16)
    halves = 2
    uw = B_WIDTH // halves
    sw = S5_NS // halves
    for hf in range(halves):
        for part in range(2):
            sc = slice(part * S5_NS + hf * sw, part * S5_NS + (hf + 1) * sw)
            bu_sc[:, sc] = jnp.dot(ub[:, hf * uw:(hf + 1) * uw], win_ref[hf * uw:(hf + 1) * uw, sc],
                                   preferred_element_type=jnp.float32)
    ar = jnp.broadcast_to(ar_ref[...], (bsz, S5_NS))
    ai = jnp.broadcast_to(ai_ref[...], (bsz, S5_NS))
    per = SUBLANES // bsz
    ngroups = tt // per

    def body(s, carry):
        xr, xi = carry
        p = (ngroups - 1 - s) if reverse else s
        base = pl.multiple_of(p * SUBLANES, SUBLANES)
        blk = bu_sc[pl.ds(base, SUBLANES), :]
        outs_r = [None] * per
        outs_i = [None] * per
        for ph in (range(per - 1, -1, -1) if reverse else range(per)):
            br = blk[ph * bsz:(ph + 1) * bsz, :S5_NS]
            bi = blk[ph * bsz:(ph + 1) * bsz, S5_NS:]
            xr, xi = ar * xr - ai * xi + br, ar * xi + ai * xr + bi
            outs_r[ph] = xr
            outs_i[ph] = xi
        xs_sc[pl.ds(base, SUBLANES), :S5_NS] = jnp.concatenate(outs_r, axis=0)
        xs_sc[pl.ds(base, SUBLANES), S5_NS:] = jnp.concatenate(outs_i, axis=0)
        return xr, xi

    xr, xi = lax.fori_loop(0, ngroups, body, (st_sc[0], st_sc[1]))
    st_sc[0] = xr
    st_sc[1] = xi
    for hf in range(halves):
        yc = slice(hf * uw, (hf + 1) * uw)
        acc = None
        for part in range(2):
            sc = slice(part * S5_NS + hf * sw, part * S5_NS + (hf + 1) * sw)
            term = jnp.dot(xs_sc[:, sc].astype(jnp.bfloat16), wout_ref[sc, yc], preferred_element_type=jnp.float32)
            acc = term if acc is None else acc + term
        y_ref[:, yc] = acc


def s5_scan(u_tb, win, ar, ai, wout, *, bsz, reverse):
    n = u_tb.shape[0]
    rows = S5_TT * bsz
    nt = n // rows
    assert n % rows == 0 and SUBLANES % bsz == 0
    idx = (lambda i: (nt - 1 - i, 0)) if reverse else (lambda i: (i, 0))
    const = lambda i: (0, 0)
    return pl.pallas_call(
        functools.partial(_s5_scan_kernel, bsz=bsz, tt=S5_TT, reverse=reverse),
        grid=(nt,),
        in_specs=[pl.BlockSpec((rows, B_WIDTH), idx),
                  pl.BlockSpec((B_WIDTH, 2 * S5_NS), const),
                  pl.BlockSpec((1, S5_NS), const),
                  pl.BlockSpec((1, S5_NS), const),
                  pl.BlockSpec((2 * S5_NS, B_WIDTH), const)],
        out_specs=pl.BlockSpec((rows, B_WIDTH), idx),
        out_shape=jax.ShapeDtypeStruct((n, B_WIDTH), jnp.float32),
        scratch_shapes=[pltpu.VMEM((rows, 2 * S5_NS), jnp.float32),
                        pltpu.VMEM((rows, 2 * S5_NS), jnp.float32),
                        pltpu.VMEM((2, bsz, S5_NS), jnp.float32)],
        compiler_params=pltpu.CompilerParams(dimension_semantics=("arbitrary",),
                                             vmem_limit_bytes=VMEM_LIMIT_BYTES),
        name="s5_scan_bwd" if reverse else "s5_scan_fwd",
    )(u_tb, win, ar, ai, wout)


def _s5_final_kernel(u_ref, yf_ref, yb_ref, d_ref, w_ref, b_ref, o_ref):
    y = d_ref[...] * u_ref[...] + yf_ref[...] + yb_ref[...]
    y = jax.nn.gelu(y)
    z = jnp.dot(y.astype(jnp.bfloat16), w_ref[...], preferred_element_type=jnp.float32) + b_ref[...]
    o_ref[...] = y * jax.nn.sigmoid(z)


def s5_finalize(u, yf, yb, d_skip, glu_w, glu_b, *, tm=512):
    n, w = u.shape
    row = pl.BlockSpec((tm, w), lambda i: (i, 0))
    vec = pl.BlockSpec((1, w), lambda i: (0, 0))
    return pl.pallas_call(
        _s5_final_kernel,
        grid=(n // tm,),
        in_specs=[row, row, row, vec, pl.BlockSpec((w, w), lambda i: (0, 0)), vec],
        out_specs=row,
        out_shape=jax.ShapeDtypeStruct((n, w), jnp.float32),
        compiler_params=pltpu.CompilerParams(dimension_semantics=("parallel",),
                                             vmem_limit_bytes=VMEM_LIMIT_BYTES),
        name="s5_finalize",
    )(u, yf, yb, d_skip.reshape(1, w).astype(jnp.float32), glu_w.astype(jnp.bfloat16),
      glu_b.reshape(1, w).astype(jnp.float32))


def s5_direction_params(lam_re, lam_im, log_step, b_re, b_im, c_re, c_im):
    step = jnp.exp(log_step)[:, None]
    mag = jnp.exp(lam_re * step)
    abar_re = mag * jnp.cos(lam_im * step)
    abar_im = mag * jnp.sin(lam_im * step)
    den = lam_re * lam_re + lam_im * lam_im
    fr = ((abar_re - 1.0) * lam_re + abar_im * lam_im) / den
    fi = (abar_im * lam_re - (abar_re - 1.0) * lam_im) / den
    bb_re = fr[..., None] * b_re - fi[..., None] * b_im
    bb_im = fr[..., None] * b_im + fi[..., None] * b_re
    eye = jnp.eye(S5_GROUPS, dtype=jnp.float32)
    win = jnp.concatenate([jnp.einsum('gnp,gh->gphn', bb, eye).reshape(B_WIDTH, S5_NS) for bb in (bb_re, bb_im)],
                          axis=1)
    wout = jnp.concatenate([jnp.einsum('gpn,gh->hngp', c, eye).reshape(S5_NS, B_WIDTH) for c in (c_re, -c_im)],
                           axis=0)
    return (win.astype(jnp.bfloat16), abar_re.reshape(1, S5_NS), abar_im.reshape(1, S5_NS),
            wout.astype(jnp.bfloat16))


def s5_mixer_tb(u_tb, bsz, lam_re, lam_im, log_step, b_re, b_im, c_re, c_im, d_skip, glu_w, glu_b):
    f32 = jnp.float32
    ys = []
    for direction in range(2):
        prm = s5_direction_params(lam_re[direction].astype(f32), lam_im[direction].astype(f32),
                                  log_step[direction].astype(f32), b_re[direction].astype(f32),
                                  b_im[direction].astype(f32), c_re[direction].astype(f32),
                                  c_im[direction].astype(f32))
        ys.append(s5_scan(u_tb, *prm, bsz=bsz, reverse=(direction == 1)))
    return s5_finalize(u_tb, ys[0], ys[1], d_skip, glu_w, glu_b)


def t5_bucket(rel):
    half = REL_BUCKETS // 2
    max_exact = half // 2
    base = jnp.where(rel > 0, half, 0)
    n = jnp.abs(rel)
    nf = jnp.maximum(n, 1).astype(jnp.float32)
    large = max_exact + (jnp.log(nf / max_exact) / math.log(REL_MAX_DIST / max_exact)
                         * (half - max_exact)).astype(jnp.int32)
    large = jnp.minimum(large, half - 1)
    return base + jnp.where(n < max_exact, n, large)


ATT_T = 512
LOG2E = math.log2(math.e)


def rel_bias_tiles(rel_bias, t):
    assert t >= REL_MAX_DIST
    table = rel_bias.astype(jnp.float32) * LOG2E
    tiles = []
    for d in (-1, 0, 1):
        c = table[t5_bucket(d * t + jnp.arange(-(t - 1), t))]
        w = jnp.concatenate([c, c[:1]], axis=0)
        m = jnp.tile(w, (t, 1))[:t * (2 * t - 1)].reshape(t, 2 * t - 1, -1)
        tiles.append(m[:, t - 1:2 * t - 1])
    far_neg = jnp.broadcast_to(table[t5_bucket(jnp.array(-2 * t))], tiles[0].shape)
    far_pos = jnp.broadcast_to(table[t5_bucket(jnp.array(2 * t))], tiles[0].shape)
    out = jnp.stack([far_neg] + tiles + [far_pos], axis=0)
    return jnp.transpose(out, (3, 0, 1, 2))


def _attn_operands(q, k, v, qg, kg):
    f32 = jnp.float32
    bf16 = jnp.bfloat16
    lane = lax.broadcasted_iota(jnp.int32, q.shape, 1)
    lo = lane < C_HEAD_DIM

    def half_sums(sq):
        return (jnp.sum(jnp.where(lo, sq, 0.0), axis=-1, keepdims=True),
                jnp.sum(jnp.where(lo, 0.0, sq), axis=-1, keepdims=True))

    def halfnorm(x, g):
        s_lo, s_hi = half_sums(x * x)
        return x * lax.rsqrt(jnp.where(lo, s_lo, s_hi) * (1.0 / C_HEAD_DIM) + EPS) * g

    def max_sq_norms(xb):
        n_lo, n_hi = half_sums(xb.astype(f32) * xb.astype(f32))
        return jnp.max(n_lo, axis=0, keepdims=True), jnp.max(n_hi, axis=0, keepdims=True)

    qn = halfnorm(q, qg) * (C_HEAD_DIM ** -0.5 * LOG2E)
    kn = halfnorm(k, kg)
    qb = qn.astype(bf16)
    kb = kn.astype(bf16)
    q2 = (jnp.where(lo, qb, 0.0).astype(bf16), jnp.where(lo, 0.0, qb).astype(bf16))
    q_lo, q_hi = max_sq_norms(qb)
    k_lo, k_hi = max_sq_norms(kb)
    sub = lax.broadcasted_iota(jnp.int32, (SUBLANES, q.shape[1]), 0)
    stats = jnp.where(sub == 0, q_lo, jnp.where(sub == 1, q_hi, jnp.where(sub == 2, k_lo,
                      jnp.where(sub == 3, k_hi, 0.0))))
    return q2, kn.T.astype(bf16), v.astype(bf16), stats


def _odd_proj_kernel(x_ref, g_ref, w_ref, qg_ref, kg_ref, o_ref, q2_ref, kt_ref, vb_ref, st_ref):
    x = x_ref[...]
    y = (x * lax.rsqrt(jnp.mean(x * x, axis=-1, keepdims=True) + EPS) * g_ref[...]).astype(jnp.bfloat16)
    hw = 2 * C_HEAD_DIM
    q, k, v = (jnp.dot(y, w_ref[:, p * C_WIDTH:(p + 1) * C_WIDTH], preferred_element_type=jnp.float32)
               for p in range(3))
    for h in range(C_HEADS):
        cols = slice(h * hw, (h + 1) * hw)
        q2, kt, vb, stats = _attn_operands(q[:, cols], k[:, cols], v[:, cols], qg_ref[...], kg_ref[...])
        q2_ref[h, 0] = q2[0]
        q2_ref[h, 1] = q2[1]
        kt_ref[h] = kt
        vb_ref[h] = vb
        st_ref[h] = stats
    att = 3 * C_WIDTH
    for c0 in range(att, w_ref.shape[1], PROJ_COLS):
        c1 = min(c0 + PROJ_COLS, w_ref.shape[1])
        o_ref[:, c0 - att:c1 - att] = jnp.dot(y, w_ref[:, c0:c1], preferred_element_type=jnp.float32)


def odd_in_proj(x, gain, w, q_gain, k_gain, *, tm=PROJ_TM):
    bsz, l, kdim = x.shape
    m = w.shape[1]
    rest = m - 3 * C_WIDTH
    tm = min(tm, l)
    hw = 2 * C_HEAD_DIM
    gq = jnp.tile(q_gain.astype(jnp.float32), 2).reshape(1, hw)
    gk = jnp.tile(k_gain.astype(jnp.float32), 2).reshape(1, hw)
    vec = pl.BlockSpec((1, hw), lambda b, i: (0, 0))
    return pl.pallas_call(
        _odd_proj_kernel,
        grid=(bsz, l // tm),
        in_specs=[pl.BlockSpec((None, tm, kdim), lambda b, i: (b, i, 0)),
                  pl.BlockSpec((1, kdim), lambda b, i: (0, 0)),
                  pl.BlockSpec((kdim, m), lambda b, i: (0, 0), pipeline_mode=pl.Buffered(1)),
                  vec, vec],
        out_specs=[pl.BlockSpec((None, tm, rest), lambda b, i: (b, i, 0)),
                   pl.BlockSpec((None, C_HEADS, 2, tm, hw), lambda b, i: (b, 0, 0, i, 0)),
                   pl.BlockSpec((None, C_HEADS, hw, tm), lambda b, i: (b, 0, 0, i)),
                   pl.BlockSpec((None, C_HEADS, tm, hw), lambda b, i: (b, 0, i, 0)),
                   pl.BlockSpec((None, C_HEADS, None, SUBLANES, hw), lambda b, i: (b, 0, i, 0, 0))],
        out_shape=[jax.ShapeDtypeStruct((bsz, l, rest), jnp.float32),
                   jax.ShapeDtypeStruct((bsz, C_HEADS, 2, l, hw), jnp.bfloat16),
                   jax.ShapeDtypeStruct((bsz, C_HEADS, hw, l), jnp.bfloat16),
                   jax.ShapeDtypeStruct((bsz, C_HEADS, l, hw), jnp.bfloat16),
                   jax.ShapeDtypeStruct((bsz, C_HEADS, l // tm, SUBLANES, hw), jnp.float32)],
        compiler_params=pltpu.CompilerParams(dimension_semantics=("parallel", "parallel"),
                                             vmem_limit_bytes=VMEM_LIMIT_BYTES),
        name="odd_in_proj",
    )(x, gain.reshape(1, kdim).astype(jnp.float32), w.astype(jnp.bfloat16), gq, gk)


ATT_ROWS = 64
ATT_SAFE_GAP = 100.0


def _attn_kernel(lam_ref, kmax_ref, bmax_ref, q2_ref, kt_ref, v_ref, bias_ref, g_ref, o_ref,
                 m_sc, l_sc, acc_sc, s_sc, p_sc, a_sc, *, t, nk, out_scale, bounded):
    f32 = jnp.float32
    b = pl.program_id(0)
    h = pl.program_id(1)
    qi = pl.program_id(2)
    q2 = q2_ref[...].reshape(2 * t, 2 * C_HEAD_DIM)
    r = ATT_ROWS
    hw = 2 * C_HEAD_DIM
    if bounded:
        q2f = q2.astype(f32)
        nq = jnp.sqrt(jnp.sum(q2f * q2f, axis=-1, keepdims=True))
        row = lax.broadcasted_iota(jnp.int32, nq.shape, 0)
        kc = jnp.where(row < t, kmax_ref[(b * C_HEADS + h) * 2], kmax_ref[(b * C_HEADS + h) * 2 + 1])
        m_sc[...] = jnp.broadcast_to(nq * kc + bmax_ref[h], m_sc.shape)
    else:
        m_sc[...] = jnp.full(m_sc.shape, -jnp.inf, f32)
    l_sc[...] = jnp.zeros_like(l_sc)
    acc_sc[...] = jnp.zeros_like(acc_sc)

    def body(ki, carry):
        off = pl.multiple_of(ki * t, t)
        bidx = jnp.clip(ki - qi, -2, 2) + 2
        s_sc[...] = jnp.dot(q2, kt_ref[:, pl.ds(off, t)], preferred_element_type=f32)
        for g in range(2 * t // r):
            rows = slice(g * r, (g + 1) * r)
            brow = (g * r) % t
            s = s_sc[rows, :] + bias_ref[bidx, brow:brow + r, :]
            m = m_sc[rows, :]
            if not bounded:
                m_prev = m
                m = jnp.maximum(m_prev, jnp.max(s, axis=-1, keepdims=True))
                alpha = jnp.exp2(m_prev - m)
                m_sc[rows, :] = m
                a_sc[rows, :] = alpha
            ps = [jnp.exp2(s[:, j * hw:(j + 1) * hw] - m) for j in range(t // hw)]
            psum = jnp.sum(sum(ps), axis=-1, keepdims=True)
            l_sc[rows, :] = (l_sc[rows, :] if bounded else alpha * l_sc[rows, :]) + psum
            for j in range(t // hw):
                p_sc[rows, j * hw:(j + 1) * hw] = ps[j].astype(jnp.bfloat16)
        pv = jnp.dot(p_sc[...], v_ref[pl.ds(off, t), :], preferred_element_type=f32)
        acc_sc[...] = (acc_sc[...] if bounded else a_sc[...] * acc_sc[...]) + pv
        return carry

    lax.fori_loop(0, nk, body, 0)
    a = acc_sc[...] / l_sc[...]
    o = a[:t] - lam_ref[0] * a[t:]
    y = o * lax.rsqrt(jnp.mean(o * o, axis=-1, keepdims=True) + EPS)
    o_ref[...] = y * g_ref[...] * out_scale


def diff_attention(q2, kt, vb, stats, lam, out_gain, bias5, layer_idx):
    f32 = jnp.float32
    bsz, _, l, _ = vb.shape
    t = ATT_T
    hw = 2 * C_HEAD_DIM
    lam_init = 0.8 - 0.6 * math.exp(-0.3 * layer_idx)
    lam_f = lam.astype(f32)
    lam_full = jnp.exp(jnp.sum(lam_f[0] * lam_f[1])) - jnp.exp(jnp.sum(lam_f[2] * lam_f[3])) + lam_init
    norms = jnp.sqrt(jnp.max(stats[..., 0:4, 0], axis=2)) * (1.0 + 1e-3)
    qmax, kmax = norms[..., 0:2], norms[..., 2:4]
    bmax = jnp.max(bias5, axis=(1, 2, 3))
    bmin = jnp.min(bias5, axis=(1, 2, 3))
    gap = 2.0 * qmax * kmax + (bmax - bmin)[None, :, None]
    smem = pl.BlockSpec(memory_space=pltpu.SMEM)

    def run(bounded):
        return pl.pallas_call(
            functools.partial(_attn_kernel, t=t, nk=l // t, out_scale=1.0 - lam_init, bounded=bounded),
            grid=(bsz, C_HEADS, l // t),
            in_specs=[smem, smem, smem,
                      pl.BlockSpec((None, None, 2, t, hw), lambda b, h, i: (b, h, 0, i, 0)),
                      pl.BlockSpec((None, None, hw, l), lambda b, h, i: (b, h, 0, 0)),
                      pl.BlockSpec((None, None, l, hw), lambda b, h, i: (b, h, 0, 0)),
                      pl.BlockSpec((None, 5, t, t), lambda b, h, i: (h, 0, 0, 0)),
                      pl.BlockSpec((1, hw), lambda b, h, i: (0, 0))],
            out_specs=pl.BlockSpec((None, t, hw), lambda b, h, i: (b, i, h)),
            out_shape=jax.ShapeDtypeStruct((bsz, l, C_WIDTH), f32),
            scratch_shapes=[pltpu.VMEM((2 * t, hw), f32), pltpu.VMEM((2 * t, hw), f32), pltpu.VMEM((2 * t, hw), f32),
                            pltpu.VMEM((2 * t, t), f32), pltpu.VMEM((2 * t, t), jnp.bfloat16),
                            pltpu.VMEM((2 * t, hw), f32)],
            compiler_params=pltpu.CompilerParams(dimension_semantics=("parallel", "parallel", "arbitrary"),
                                                 vmem_limit_bytes=VMEM_LIMIT_BYTES),
            name="diff_attention_bounded" if bounded else "diff_attention_online",
        )(lam_full.reshape(1), kmax.reshape(-1), bmax, q2, kt, vb, bias5, out_gain.reshape(1, hw).astype(f32))

    return lax.cond(jnp.all(gap < ATT_SAFE_GAP), lambda: run(True), lambda: run(False))


GDN_TT = 512
GDN_HEADS_PER_STEP = 4
LANES = 128
OD_QKV_BLOCK = 0
OD_GATE_BLOCK = OD_QKV_BLOCK + 3 * D_WIDTH // LANES
OD_AB_BLOCK = OD_GATE_BLOCK + D_WIDTH // LANES
OD_COLS = 2304


def _gdn_prep_kernel(prev_ref, cur_ref, next_ref, w_ref, o_ref, *, tl, nl):
    i = pl.program_id(1)
    part = pl.program_id(2)
    prev = jnp.where(i > 0, prev_ref[...], 0.0)
    nxt = jnp.where(i < nl - 1, next_ref[...], 0.0)
    ext = jnp.concatenate([prev, cur_ref[...], nxt], axis=0)
    halo = prev.shape[0]
    acc = None
    for j in range(CONV_WIDTH):
        start = halo - CONV_WIDTH // 2 + j
        term = w_ref[j:j + 1, :] * ext[start:start + tl, :]
        acc = term if acc is None else acc + term
    y = acc * jax.nn.sigmoid(acc)
    scale = jnp.where(part == 0, D_HEAD_DIM ** -0.5, 1.0)
    heads = []
    for h in range(D_HEADS):
        yh = y[:, h * LANES:(h + 1) * LANES]
        heads.append(yh * (lax.rsqrt(jnp.sum(yh * yh, axis=-1, keepdims=True) + EPS) * scale))
    o_ref[...] = jnp.where(part < 2, jnp.concatenate(heads, axis=1), y)


def gdn_prep(proj3, conv_w, *, tl=1024):
    bsz, l, _ = proj3.shape
    halo = SUBLANES
    nl = l // tl
    blk0 = OD_QKV_BLOCK * LANES // D_WIDTH
    return pl.pallas_call(
        functools.partial(_gdn_prep_kernel, tl=tl, nl=nl),
        grid=(bsz, nl, 3),
        in_specs=[pl.BlockSpec((None, halo, D_WIDTH), lambda b, i, p: (b, jnp.maximum(i * (tl // halo) - 1, 0), blk0 + p)),
                  pl.BlockSpec((None, tl, D_WIDTH), lambda b, i, p: (b, i, blk0 + p)),
                  pl.BlockSpec((None, halo, D_WIDTH),
                               lambda b, i, p: (b, jnp.minimum((i + 1) * (tl // halo), l // halo - 1), blk0 + p)),
                  pl.BlockSpec((CONV_WIDTH, D_WIDTH), lambda b, i, p: (0, p))],
        out_specs=pl.BlockSpec((None, None, tl, D_WIDTH), lambda b, i, p: (p, b, i, 0)),
        out_shape=jax.ShapeDtypeStruct((3, bsz, l, D_WIDTH), jnp.float32),
        compiler_params=pltpu.CompilerParams(dimension_semantics=("parallel", "parallel", "parallel"),
                                             vmem_limit_bytes=VMEM_LIMIT_BYTES),
        name="gdn_prep",
    )(proj3, proj3, proj3, conv_w.astype(jnp.float32))


def _gdn_gates_kernel(x_ref, nega_ref, dtb_ref, o_ref):
    x = x_ref[...]
    z = x + dtb_ref[...]
    g = nega_ref[...] * (jnp.maximum(z, 0.0) + jnp.log1p(jnp.exp(-jnp.abs(z))))
    lane = lax.broadcasted_iota(jnp.int32, x.shape, 1)
    y = jnp.where(lane < 2 * D_HEADS, g, jax.nn.sigmoid(x))
    o_ref[...] = y.T[0:4 * D_HEADS, :]


def gdn_gates(proj3, a_log, dt_bias, *, tl=512):
    bsz, l, _ = proj3.shape
    pad = LANES - 2 * D_HEADS
    nega = jnp.pad(-jnp.exp(a_log.astype(jnp.float32)).reshape(1, -1), ((0, 0), (0, pad)))
    dtb = jnp.pad(dt_bias.astype(jnp.float32).reshape(1, -1), ((0, 0), (0, pad)))
    vec = pl.BlockSpec((1, LANES), lambda b, i: (0, 0))
    return pl.pallas_call(
        _gdn_gates_kernel,
        grid=(bsz, l // tl),
        in_specs=[pl.BlockSpec((None, tl, LANES), lambda b, i: (b, i, OD_AB_BLOCK)), vec, vec],
        out_specs=pl.BlockSpec((None, 4 * D_HEADS, tl), lambda b, i: (b, 0, i)),
        out_shape=jax.ShapeDtypeStruct((bsz, 4 * D_HEADS, l), jnp.float32),
        compiler_params=pltpu.CompilerParams(dimension_semantics=("parallel", "parallel"),
                                             vmem_limit_bytes=VMEM_LIMIT_BYTES),
        name="gdn_gates",
    )(proj3, nega, dtb)


def gdn_constants():
    import numpy as np
    c = CHUNK
    r = np.arange(c)[:, None]
    u = np.arange(c)[None, :]
    cum, incl, strict = [], [], []
    for direction in range(2):
        fwd = direction == 0
        cum.append(np.concatenate([(r <= u) if fwd else (r >= u), np.ones((c, c), bool)], axis=1))
        incl.append((u <= r) if fwd else (u >= r))
        strict.append((u < r) if fwd else (u > r))
    same = lambda b: (r // b) == (u // b)
    merges = [same(2 * b) & ~same(b) for b in (8, 16, 32)]
    f32 = jnp.float32
    return (jnp.asarray(np.stack(cum), jnp.bfloat16), jnp.asarray(np.stack(incl), f32),
            jnp.asarray(np.stack(strict), f32), jnp.asarray(same(8), f32), jnp.asarray(np.stack(merges), f32))


def _gdn_kernel(q_ref, k_ref, v_ref, g_ref, b_ref, cum_ref, incl_ref, strict_ref, d8_ref, mrg_ref, o_ref,
                s_sc, qd_sc, dec_sc, w_sc, u_sc, sin_sc, *, nc):
    direction = pl.program_id(2)

    @pl.when(pl.program_id(3) == 0)
    def _():
        s_sc[...] = jnp.zeros_like(s_sc)

    bf16 = jnp.bfloat16
    f32 = jnp.float32
    c = CHUNK
    hd = D_HEAD_DIM
    contract_last = (((1,), (1,)), ((), ()))
    contract_first = (((0,), (0,)), ((), ()))
    cumm = cum_ref[...]
    incl = incl_ref[...]
    strict = strict_ref[...]
    d8 = d8_ref[...]
    eye = (lax.broadcasted_iota(jnp.int32, (c, c), 0) == lax.broadcasted_iota(jnp.int32, (c, c), 1)).astype(f32)

    def mm(a, b):
        return jnp.dot(a.astype(bf16), b.astype(bf16), preferred_element_type=f32)

    def rep(x):
        return jnp.concatenate([x] * (hd // c), axis=1)

    nh = g_ref.shape[0]
    ns = range(nh * nc)
    head = [m // nc for m in ns]
    rows = [slice((m % nc) * c, (m % nc + 1) * c) for m in ns]
    cols = [slice(h * hd, (h + 1) * hd) for h in head]
    q = [q_ref[rows[m], cols[m]] for m in ns]
    k = [k_ref[rows[m], cols[m]] for m in ns]
    v = [v_ref[rows[m], cols[m]] for m in ns]
    kb = [x.astype(bf16) for x in k]
    kk = [lax.dot_general(x, x, contract_last, preferred_element_type=f32) for x in kb]
    qk = [lax.dot_general(q[n].astype(bf16), kb[n], contract_last, preferred_element_type=f32) for n in ns]
    grow = [jnp.broadcast_to(g_ref[head[m], :, rows[m]], (c, c)) for m in ns]
    ghi = [x.astype(bf16) for x in grow]
    glo = [(grow[n] - ghi[n].astype(f32)).astype(bf16) for n in ns]
    gm = [jnp.dot(ghi[n], cumm, preferred_element_type=f32) + jnp.dot(glo[n], cumm, preferred_element_type=f32)
          for n in ns]
    gam_row = [x[:, :c] for x in gm]
    tot = [x[:, c:] for x in gm]
    gam_col = [x.T for x in gam_row]
    beta_col = [jnp.broadcast_to(b_ref[head[m], :, rows[m]], (c, c)).T for m in ns]
    decay = [incl * jnp.exp(jnp.minimum(gam_col[n] - gam_row[n], 0.0)) for n in ns]
    a = [strict * beta_col[n] * kk[n] * decay[n] for n in ns]
    a0 = [x * d8 for x in a]
    n2 = [mm(x, x) for x in a0]
    n4 = [mm(x, x) for x in n2]
    t = [mm(eye - a0[n], eye + n2[n]) for n in ns]
    t = [mm(t[n], eye + n4[n]) for n in ns]
    for j in range(mrg_ref.shape[0]):
        p = [mm(a[n] * mrg_ref[j], t[n]) for n in ns]
        t = [t[n] - mm(t[n], p[n]) for n in ns]
    beta128 = [rep(x) for x in beta_col]
    egam128 = [rep(jnp.exp(x)) for x in gam_col]
    solb = [mm(t[n], jnp.concatenate([k[n] * beta128[n] * egam128[n], v[n] * beta128[n]], axis=1)).astype(bf16)
            for n in ns]
    av = [jnp.dot((qk[n] * decay[n]).astype(bf16), solb[n], preferred_element_type=f32) for n in ns]
    k_dec = [(k[n] * rep(jnp.exp(tot[n] - gam_col[n]))).astype(bf16) for n in ns]
    wu = [lax.dot_general(k_dec[n], solb[n], contract_first, preferred_element_type=f32) for n in ns]
    for n in ns:
        qd_sc[n] = (q[n] * egam128[n] - av[n][:, :hd]).astype(bf16)
        o_ref[rows[n], cols[n]] = av[n][:, hd:]
        w_sc[n] = wu[n][:, :hd].astype(bf16)
        u_sc[n] = wu[n][:, hd:]
        dec_sc[n] = rep(jnp.exp(tot[n][0:SUBLANES, :]))

    def body(ci, states):
        ce = ci + direction * (nc - 1 - 2 * ci)
        new = []
        for h in range(nh):
            m = h * nc + ce
            sb = states[h].astype(bf16)
            sin_sc[m] = sb
            new.append(states[h] * dec_sc[m][0:1] - jnp.dot(w_sc[m], sb, preferred_element_type=f32) + u_sc[m])
        return tuple(new)

    states = lax.fori_loop(0, nc, body, tuple(s_sc[h] for h in range(nh)))
    for h in range(nh):
        s_sc[h] = states[h]

    for m in ns:
        o_ref[rows[m], cols[m]] += jnp.dot(qd_sc[m], sin_sc[m], preferred_element_type=f32)


def gdn_scan(qkv, gb):
    _, bsz, l, _ = qkv.shape
    hd = D_HEAD_DIM
    tt = min(GDN_TT, l)
    nt = l // tt
    nc = tt // CHUNK
    assert l % tt == 0 and tt % CHUNK == 0
    consts = gdn_constants()
    gb4 = gb.reshape(bsz, 4 * D_HEADS, 1, l)
    nh = GDN_HEADS_PER_STEP
    assert D_HEADS % nh == 0
    tidx = lambda d, i: i + d * (nt - 1 - 2 * i)
    qkv_spec = lambda p: pl.BlockSpec((None, None, tt, nh * hd), lambda b, h, d, i: (p, b, tidx(d, i), h))
    row_spec = lambda off: pl.BlockSpec((None, nh, 1, tt),
                                        lambda b, h, d, i: (b, (off + d * D_HEADS) // nh + h, 0, tidx(d, i)))
    per_dir = lambda a: pl.BlockSpec((None,) + a.shape[1:], lambda b, h, d, i: (d,) + (0,) * (a.ndim - 1))
    whole = lambda a: pl.BlockSpec(a.shape, lambda b, h, d, i: (0,) * a.ndim)
    return pl.pallas_call(
        functools.partial(_gdn_kernel, nc=nc),
        grid=(bsz, D_HEADS // nh, 2, nt),
        in_specs=[qkv_spec(0), qkv_spec(1), qkv_spec(2), row_spec(0), row_spec(2 * D_HEADS),
                  per_dir(consts[0]), per_dir(consts[1]), per_dir(consts[2]), whole(consts[3]), whole(consts[4])],
        out_specs=pl.BlockSpec((None, None, tt, nh * hd), lambda b, h, d, i: (d, b, tidx(d, i), h)),
        out_shape=jax.ShapeDtypeStruct((2, bsz, l, D_WIDTH), jnp.float32),
        scratch_shapes=[pltpu.VMEM((nh, hd, hd), jnp.float32),
                        pltpu.VMEM((nh * nc, CHUNK, hd), jnp.bfloat16),
                        pltpu.VMEM((nh * nc, SUBLANES, hd), jnp.float32),
                        pltpu.VMEM((nh * nc, hd, hd), jnp.bfloat16),
                        pltpu.VMEM((nh * nc, hd, hd), jnp.float32),
                        pltpu.VMEM((nh * nc, hd, hd), jnp.bfloat16)],
        compiler_params=pltpu.CompilerParams(dimension_semantics=("parallel", "parallel", "parallel", "arbitrary"),
                                             vmem_limit_bytes=VMEM_LIMIT_BYTES),
        name="gdn_scan",
    )(qkv, qkv, qkv, gb4, gb4, *consts)


def gated_deltanet(proj3, conv_w, a_log, dt_bias):
    return gdn_scan(gdn_prep(proj3, conv_w), gdn_gates(proj3, a_log, dt_bias))


MOE_TT = 512
MOE_SUB = 128
MOE_ALIGN = 64
MOE_TILES_PER_STEP = 4
MOE_ROWS = 256
MOE_SLAB = 256
MOE_VMEM_LIMIT_BYTES = 56 * 1024 * 1024
MOE_EXPERT_VMEM_LIMIT_BYTES = 60 * 1024 * 1024


def _router_kernel(x_ref, g_ref, wr_ref, h_ref, aff_ref):
    x = x_ref[...]
    h = (x * lax.rsqrt(jnp.mean(x * x, axis=-1, keepdims=True) + EPS) * g_ref[...]).astype(jnp.bfloat16)
    h_ref[...] = h
    logits = jnp.dot(h, wr_ref[...], preferred_element_type=jnp.float32)
    lane = lax.broadcasted_iota(jnp.int32, logits.shape, 1)
    logits = jnp.where(lane < N_EXPERTS, logits, -jnp.inf)
    p = jnp.exp(logits - jnp.max(logits, axis=-1, keepdims=True))
    aff = p / jnp.sum(p, axis=-1, keepdims=True)
    aff_ref[...] = aff.T[0:N_EXPERTS, :]


def moe_route(x, gain, w_router, *, tm=MOE_TT):
    bsz, l, d = x.shape
    wr = jnp.pad(w_router.astype(jnp.bfloat16), ((0, 0), (0, LANES - N_EXPERTS)))
    return pl.pallas_call(
        _router_kernel,
        grid=(bsz, l // tm),
        in_specs=[pl.BlockSpec((None, tm, d), lambda b, i: (b, i, 0)),
                  pl.BlockSpec((1, d), lambda b, i: (0, 0)),
                  pl.BlockSpec((d, LANES), lambda b, i: (0, 0))],
        out_specs=[pl.BlockSpec((None, tm, d), lambda b, i: (b, i, 0)),
                   pl.BlockSpec((None, N_EXPERTS, tm), lambda b, i: (b, 0, i))],
        out_shape=[jax.ShapeDtypeStruct((bsz, l, d), jnp.bfloat16),
                   jax.ShapeDtypeStruct((bsz, N_EXPERTS, l), jnp.float32)],
        compiler_params=pltpu.CompilerParams(dimension_semantics=("parallel", "parallel"),
                                             vmem_limit_bytes=VMEM_LIMIT_BYTES),
        name="moe_router",
    )(x, gain.reshape(1, d).astype(jnp.float32), wr)


def _select_kernel(aff_ref, pre_ref, smap_ref, gate_ref, cnt_ref, *, cap, tt):
    f32 = jnp.float32
    bf16 = jnp.bfloat16
    aff = aff_ref[...]
    e, l = aff.shape
    nl = l // LANES
    tiles = [slice(j * LANES, (j + 1) * LANES) for j in range(nl)]
    bits = pltpu.bitcast(aff, jnp.int32)
    bt = [bits[:, s] for s in tiles]

    def lane_total(x):
        return jnp.broadcast_to(jnp.sum(x, axis=-1, keepdims=True), (e, LANES))

    def search(i, thr):
        cand = thr | jnp.left_shift(jnp.int32(1), 30 - i)
        acc = jnp.zeros((e, LANES), jnp.int32)
        for x in bt:
            acc = acc + (x >= cand).astype(jnp.int32)
        return jnp.where(lane_total(acc) >= cap, cand, thr)

    thr = lax.fori_loop(0, 31, search, jnp.zeros((e, LANES), jnp.int32))
    gt = [x > thr for x in bt]
    eq = [x == thr for x in bt]
    acc = jnp.zeros((e, LANES), jnp.int32)
    for x in gt:
        acc = acc + x.astype(jnp.int32)
    need = (cap - lane_total(acc)).astype(f32)

    pre = pre_ref[...]

    def prefix(flags):
        outs = [jnp.dot(jnp.where(x, 1.0, 0.0).astype(bf16), pre, preferred_element_type=f32) for x in flags]
        carry = jnp.zeros((e, LANES), f32)
        res = []
        for o in outs:
            res.append(o[:, :LANES] + carry)
            carry = carry + o[:, LANES:]
        return res, [o[:, LANES:] for o in outs]

    rank_eq, _ = prefix(eq)
    sel = [jnp.logical_or(gt[j], jnp.logical_and(eq[j], rank_eq[j] < need)) for j in range(nl)]
    pos, totals = prefix(sel)
    lane = lax.broadcasted_iota(jnp.int32, (e, LANES), 1)
    cnt = jnp.zeros((e, LANES), f32)
    per = tt // LANES
    for j in range(nl):
        smap_ref[:, tiles[j]] = jnp.where(sel[j], pos[j], -1.0)
        gate_ref[:, tiles[j]] = jnp.where(sel[j], aff[:, tiles[j]], 0.0)
        cnt = cnt + jnp.where(lane == j // per, totals[j], 0.0)
    cnt_ref[...] = cnt


def moe_select(aff, cap, *, tt=MOE_TT):
    import numpy as np
    bsz, e, l = aff.shape
    assert l // tt <= LANES
    i = np.arange(LANES)
    pre = np.concatenate([i[:, None] < i[None, :], np.ones((LANES, LANES), bool)], axis=1)
    row = pl.BlockSpec((None, e, l), lambda b: (b, 0, 0))
    return pl.pallas_call(
        functools.partial(_select_kernel, cap=cap, tt=tt),
        grid=(bsz,),
        in_specs=[row, pl.BlockSpec((LANES, 2 * LANES), lambda b: (0, 0))],
        out_specs=[row, row, pl.BlockSpec((None, e, LANES), lambda b: (b, 0, 0))],
        out_shape=[jax.ShapeDtypeStruct((bsz, e, l), jnp.float32), jax.ShapeDtypeStruct((bsz, e, l), jnp.float32),
                   jax.ShapeDtypeStruct((bsz, e, LANES), jnp.float32)],
        compiler_params=pltpu.CompilerParams(dimension_semantics=("parallel",),
                                             vmem_limit_bytes=VMEM_LIMIT_BYTES),
        name="moe_select",
    )(aff, jnp.asarray(pre, jnp.bfloat16))


def _slot_one_hot(pos, base, rows, n):
    slot = (base + lax.broadcasted_iota(jnp.int32, (rows, n), 0)).astype(jnp.float32)
    return jnp.where(pos == slot, 1.0, 0.0).astype(jnp.bfloat16)


def _expert_kernel(cs_ref, h_ref, smap_ref, gate_ref, wg32_ref, wu32_ref, wd32_ref, o_ref,
                   xs_sc, gs_sc, wg_ref, wu_ref, wd_ref, *, nj, per, tt, cap):
    e = pl.program_id(0)
    b = pl.program_id(1)
    j = pl.program_id(2)
    f32 = jnp.float32
    bf16 = jnp.bfloat16

    @pl.when(jnp.logical_and(b == 0, j == 0))
    def _():
        wg_ref[...] = wg32_ref[...].astype(bf16)
        wu_ref[...] = wu32_ref[...].astype(bf16)
        wd_ref[...] = wd32_ref[...].astype(bf16)

    @pl.when(j == 0)
    def _():
        xs_sc[...] = jnp.zeros_like(xs_sc)
        gs_sc[...] = jnp.zeros_like(gs_sc)

    win = min(MOE_SUB, cap)
    for s in range(per):
        base = (b * N_EXPERTS + e) * (nj * per + 1) + j * per + s
        c0 = cs_ref[base]
        c1 = cs_ref[base + 1]
        cols = slice(s * tt, (s + 1) * tt)
        pos = smap_ref[:, cols]
        gate = gate_ref[:, cols]
        hb = h_ref[cols, :]

        def gather(r0, rows, pos=pos, gate=gate, hb=hb):
            oh = _slot_one_hot(pos, r0, rows, tt)
            xs_sc[pl.ds(r0, rows), :] += jnp.dot(oh, hb, preferred_element_type=f32)
            g = jnp.sum(oh.astype(f32) * gate, axis=-1, keepdims=True)
            gs_sc[pl.ds(r0, rows), :] += jnp.broadcast_to(g, (rows, LANES))

        w0 = pl.multiple_of(jnp.minimum(c0 // MOE_ALIGN * MOE_ALIGN, cap - win), MOE_ALIGN)
        gather(w0, win)

        def rest(st, carry, gather=gather):
            gather(pl.multiple_of(st * MOE_ALIGN, MOE_ALIGN), MOE_ALIGN)
            return carry

        lax.fori_loop((w0 + win) // MOE_ALIGN, (c1 + MOE_ALIGN - 1) // MOE_ALIGN, rest, 0)

    @pl.when(j == nj - 1)
    def _():
        rows_per = min(MOE_ROWS, cap)
        for r in range(cap // rows_per):
            rows = slice(r * rows_per, (r + 1) * rows_per)
            xb = xs_sc[rows, :].astype(bf16)
            g = jnp.dot(xb, wg_ref[...], preferred_element_type=f32)
            u = jnp.dot(xb, wu_ref[...], preferred_element_type=f32)
            hid = (g * jax.nn.sigmoid(g) * u).astype(bf16)
            out = jnp.dot(hid, wd_ref[...], preferred_element_type=f32)
            scale = jnp.concatenate([gs_sc[rows, :]] * (out.shape[1] // LANES), axis=1)
            o_ref[rows, :] = (out * scale).astype(bf16)


def moe_experts(hb, smap, gate, cs, w_gate, w_up, w_down, layer, cap, *, tt=MOE_TT):
    bsz, l, d = hb.shape
    _, e, _, ff = w_gate.shape
    per = MOE_TILES_PER_STEP if (l // tt) % MOE_TILES_PER_STEP == 0 else 1
    nj = l // (tt * per)
    smap4 = smap.reshape(bsz, e, 1, l)
    gate4 = gate.reshape(bsz, e, 1, l)
    tok = pl.BlockSpec((None, None, 1, per * tt), lambda ei, b, j, cs_ref: (b, ei, 0, j))
    once = pl.Buffered(1)
    grid_spec = pltpu.PrefetchScalarGridSpec(
        num_scalar_prefetch=1,
        grid=(e, bsz, nj),
        in_specs=[pl.BlockSpec((None, per * tt, d), lambda ei, b, j, cs_ref: (b, j, 0)), tok, tok,
                  pl.BlockSpec((None, None, d, ff), lambda ei, b, j, cs_ref: (layer, ei, 0, 0), pipeline_mode=once),
                  pl.BlockSpec((None, None, d, ff), lambda ei, b, j, cs_ref: (layer, ei, 0, 0), pipeline_mode=once),
                  pl.BlockSpec((None, None, ff, d), lambda ei, b, j, cs_ref: (layer, ei, 0, 0), pipeline_mode=once)],
        out_specs=pl.BlockSpec((None, None, cap, d), lambda ei, b, j, cs_ref: (b, ei, 0, 0)),
        scratch_shapes=[pltpu.VMEM((cap, d), jnp.float32), pltpu.VMEM((cap, LANES), jnp.float32),
                        pltpu.VMEM((d, ff), jnp.bfloat16), pltpu.VMEM((d, ff), jnp.bfloat16),
                        pltpu.VMEM((ff, d), jnp.bfloat16)])
    return pl.pallas_call(
        functools.partial(_expert_kernel, nj=nj, per=per, tt=tt, cap=cap),
        grid_spec=grid_spec,
        out_shape=jax.ShapeDtypeStruct((bsz, e, cap, d), jnp.bfloat16),
        compiler_params=pltpu.CompilerParams(dimension_semantics=("parallel", "arbitrary", "arbitrary"),
                                             vmem_limit_bytes=MOE_EXPERT_VMEM_LIMIT_BYTES),
        name="moe_experts",
    )(cs, hb, smap4, gate4, w_gate, w_up, w_down)


def _combine_kernel(cs_ref, x_ref, smap_ref, ow_ref, y_ref, *, nj, tt):
    b = pl.program_id(0)
    e = pl.program_id(2)

    @pl.when(e == 0)
    def _():
        y_ref[...] = x_ref[...]

    contract_first = (((0,), (0,)), ((), ()))
    cap = ow_ref.shape[0]
    win = min(2 * MOE_SUB, cap)
    base = (b * N_EXPERTS + e) * (nj + 1)
    cols = [slice(j * tt, (j + 1) * tt) for j in range(nj)]
    pos = [smap_ref[:, c] for c in cols]
    r0 = [pl.multiple_of(jnp.minimum(cs_ref[base + j] // MOE_SUB * MOE_SUB, cap - win), MOE_SUB) for j in range(nj)]
    oh = [_slot_one_hot(pos[j], r0[j], win, tt) for j in range(nj)]
    add = [lax.dot_general(oh[j], ow_ref[pl.ds(r0[j], win), :], contract_first, preferred_element_type=jnp.float32)
           for j in range(nj)]
    for j in range(nj):
        y_ref[cols[j], :] += add[j]

    for j in range(nj):
        def scatter(st, carry):
            s0 = pl.multiple_of(st * MOE_SUB, MOE_SUB)
            y_ref[cols[j], :] += lax.dot_general(_slot_one_hot(pos[j], s0, MOE_SUB, tt),
                                                 ow_ref[pl.ds(s0, MOE_SUB), :], contract_first,
                                                 preferred_element_type=jnp.float32)
            return carry

        lax.fori_loop((r0[j] + win) // MOE_SUB, (cs_ref[base + j + 1] + MOE_SUB - 1) // MOE_SUB, scatter, 0)


def moe_combine(x, smap, outw, cs, *, tt=MOE_TT):
    bsz, l, d = x.shape
    e, cap = outw.shape[1:3]
    nj = l // tt
    smap4 = smap.reshape(bsz, e, 1, l)
    grid_spec = pltpu.PrefetchScalarGridSpec(
        num_scalar_prefetch=1,
        grid=(bsz, d // MOE_SLAB, e),
        in_specs=[pl.BlockSpec((None, l, MOE_SLAB), lambda b, s, ei, cs_ref: (b, 0, s)),
                  pl.BlockSpec((None, None, 1, l), lambda b, s, ei, cs_ref: (b, ei, 0, 0)),
                  pl.BlockSpec((None, None, cap, MOE_SLAB), lambda b, s, ei, cs_ref: (b, ei, 0, s))],
        out_specs=pl.BlockSpec((None, l, MOE_SLAB), lambda b, s, ei, cs_ref: (b, 0, s)))
    return pl.pallas_call(
        functools.partial(_combine_kernel, nj=nj, tt=tt),
        grid_spec=grid_spec,
        out_shape=jax.ShapeDtypeStruct((bsz, l, d), jnp.float32),
        compiler_params=pltpu.CompilerParams(dimension_semantics=("parallel", "parallel", "arbitrary"),
                                             vmem_limit_bytes=MOE_VMEM_LIMIT_BYTES),
        name="moe_combine",
    )(cs, x, smap4, outw)


def ec_moe_layer(x, gain, w_router, w_gate, w_up, w_down, layer):
    bsz, l, d = x.shape
    cap = EC_CAPACITY_FACTOR * l // N_EXPERTS
    tt = min(MOE_TT, l)
    nj = l // tt
    hb, aff = moe_route(x, gain, w_router, tm=tt)
    smap, gate, cnt = moe_select(aff, cap, tt=tt)
    cs = jnp.concatenate([jnp.zeros((bsz, N_EXPERTS, 1), jnp.float32), jnp.cumsum(cnt[..., :nj], axis=-1)], axis=-1)
    cs = cs.astype(jnp.int32).reshape(-1)
    outw = moe_experts(hb, smap, gate, cs, w_gate, w_up, w_down, layer, cap, tt=tt)
    return moe_combine(x, smap, outw, cs, tt=tt)


def kernel(x, mix_norm, ffn_norm, ev_w_in, ev_w_out, a_lb_logits, a_out_norm, s5_lambda_re, s5_lambda_im, s5_log_step, s5_b_re, s5_b_im, s5_c_re, s5_c_im, s5_d, s5_glu_w, s5_glu_b, od_w_in, od_w_out, c_q_norm, c_k_norm, c_lambda, c_out_norm, rel_bias, d_conv_w, d_a_log, d_dt_bias, d_out_norm, moe_router, moe_w_gate, moe_w_up, moe_w_down):
    bsz, l, d = x.shape
    p = jax.nn.softmax(a_lb_logits.astype(jnp.float32), axis=0)
    cum = jnp.cumsum(p, axis=0)
    lower_bounds = cum - cum[0:1]
    bias5 = rel_bias_tiles(rel_bias, ATT_T)
    for layer in range(DEPTH):
        j = layer // 2
        if layer % 2 == 0:
            proj, u_tb = norm_matmul(x, mix_norm[layer], ev_w_in[j], tail=B_WIDTH)
            o_a2 = hgrn2_scan(proj, lower_bounds[j])
            o_b = s5_mixer_tb(u_tb.reshape(l * bsz, B_WIDTH), bsz, s5_lambda_re[j], s5_lambda_im[j], s5_log_step[j],
                              s5_b_re[j], s5_b_im[j], s5_c_re[j], s5_c_im[j], s5_d[j], s5_glu_w[j], s5_glu_b[j])
            x = mixer_out_proj(o_a2, proj, 4 * A_HEADS, a_out_norm[j], o_b.reshape(l, bsz * B_WIDTH), ev_w_out[j], x,
                               bidir_first=True, other_time_major=True)
        else:
            o2 = 3 * C_WIDTH + 3 * D_WIDTH
            o4 = o2 + 4 * D_HEADS
            w = od_w_in[j]
            w_in = jnp.concatenate([w[:, :o2], w[:, o4:], w[:, o2:o4],
                                    jnp.zeros((d, 3 * C_WIDTH + OD_COLS - w.shape[1]), w.dtype)], axis=1)
            proj, q2, kt, vb, stats = odd_in_proj(x, mix_norm[layer], w_in, c_q_norm[j], c_k_norm[j])
            o_c = diff_attention(q2, kt, vb, stats, c_lambda[j], c_out_norm[j], bias5, layer)
            o_d2 = gated_deltanet(proj, d_conv_w[j], d_a_log[j], d_dt_bias[j])
            x = mixer_out_proj(o_d2, proj, OD_GATE_BLOCK, d_out_norm[j], o_c, od_w_out[j], x, bidir_first=False)
        x = ec_moe_layer(x, ffn_norm[layer], moe_router[layer], moe_w_gate, moe_w_up, moe_w_down, layer)
    return x
```

```python
import functools
import math

import jax
import jax.numpy as jnp
from jax import lax
from jax.experimental import pallas as pl
from jax.experimental.pallas import tpu as pltpu

D_MODEL = 1024
DEPTH = 4
MIX_WIDTH = D_MODEL
A_WIDTH = MIX_WIDTH // 2
A_HEAD_DIM = 128
A_HEADS = A_WIDTH // A_HEAD_DIM
B_WIDTH = MIX_WIDTH - A_WIDTH
S5_GROUP = 16
S5_GROUPS = B_WIDTH // S5_GROUP
S5_STATE = 64
C_WIDTH = MIX_WIDTH // 2
C_HEAD_DIM = 64
C_HEADS = C_WIDTH // (2 * C_HEAD_DIM)
C_V_DIM = 2 * C_HEAD_DIM
D_WIDTH = MIX_WIDTH - C_WIDTH
D_HEAD_DIM = 128
D_HEADS = D_WIDTH // D_HEAD_DIM
CONV_WIDTH = 5
N_EXPERTS = 16
EXPERT_FF = 2 * D_MODEL
EC_CAPACITY_FACTOR = 2
REL_BUCKETS = 32
REL_MAX_DIST = 128
CHUNK = 64
Q_BLOCK = 128
EPS = 1e-6

VMEM_LIMIT_BYTES = 48 * 1024 * 1024


PROJ_TM = 512
PROJ_COLS = 512


def _norm_matmul_kernel(x_ref, g_ref, w_ref, o_ref, *tail_ref, main):
    x = x_ref[...]
    y = (x * lax.rsqrt(jnp.mean(x * x, axis=-1, keepdims=True) + EPS) * g_ref[...]).astype(jnp.bfloat16)
    for c0 in range(0, main, PROJ_COLS):
        c1 = min(c0 + PROJ_COLS, main)
        o_ref[:, c0:c1] = jnp.dot(y, w_ref[:, c0:c1], preferred_element_type=jnp.float32)
    if tail_ref:
        tail_ref[0][...] = jnp.dot(y, w_ref[:, main:], preferred_element_type=jnp.float32)


def norm_matmul(x, gain, w, *, tail=0, tm=PROJ_TM):
    bsz, l, k = x.shape
    m = w.shape[1]
    main = m - tail
    tm = min(tm, l)
    out_shape = [jax.ShapeDtypeStruct((bsz, l, main), jnp.float32)]
    out_specs = [pl.BlockSpec((None, tm, main), lambda b, i: (b, i, 0))]
    if tail:
        out_shape.append(jax.ShapeDtypeStruct((l, bsz * tail), jnp.float32))
        out_specs.append(pl.BlockSpec((tm, tail), lambda b, i: (i, b)))
    outs = pl.pallas_call(
        functools.partial(_norm_matmul_kernel, main=main),
        grid=(bsz, l // tm),
        in_specs=[pl.BlockSpec((None, tm, k), lambda b, i: (b, i, 0)),
                  pl.BlockSpec((1, k), lambda b, i: (0, 0)),
                  pl.BlockSpec((k, m), lambda b, i: (0, 0), pipeline_mode=pl.Buffered(1))],
        out_specs=out_specs,
        out_shape=out_shape,
        compiler_params=pltpu.CompilerParams(dimension_semantics=("parallel", "parallel"),
                                             vmem_limit_bytes=VMEM_LIMIT_BYTES),
        name="norm_matmul",
    )(x, gain.reshape(1, k).astype(jnp.float32), w.astype(jnp.bfloat16))
    return outs if tail else outs[0]


def _mixer_out_kernel(of_ref, ob_ref, g_ref, gain_ref, other_ref, wb_ref, wo_ref, r_ref, o_ref):
    bf16 = jnp.bfloat16
    o = of_ref[...] + ob_ref[...]
    g = g_ref[...]
    gate = g * jax.nn.sigmoid(g)
    hd = gain_ref.shape[1]
    heads = []
    for h in range(o.shape[1] // hd):
        oh = o[:, h * hd:(h + 1) * hd]
        heads.append(oh * lax.rsqrt(jnp.mean(oh * oh, axis=-1, keepdims=True) + EPS) * gain_ref[...])
    y = (jnp.concatenate(heads, axis=1) * gate).astype(bf16)
    o_ref[...] = (r_ref[...] + jnp.dot(y, wb_ref[...], preferred_element_type=jnp.float32)
                  + jnp.dot(other_ref[...].astype(bf16), wo_ref[...], preferred_element_type=jnp.float32))


def mixer_out_proj(o2, proj3, gate_block, out_gain, other, w, res, *, bidir_first, other_time_major=False,
                   tm=PROJ_TM):
    _, bsz, l, k = o2.shape
    m = w.shape[1]
    tm = min(tm, l)
    wb = w.astype(jnp.bfloat16)
    w_bidir, w_other = (wb[:k], wb[k:]) if bidir_first else (wb[k:], wb[:k])
    other_spec = (pl.BlockSpec((tm, k), lambda b, i: (i, b)) if other_time_major
                  else pl.BlockSpec((None, tm, k), lambda b, i: (b, i, 0)))
    gb = gate_block * LANES // k
    row = pl.BlockSpec((None, tm, m), lambda b, i: (b, i, 0))
    wspec = pl.BlockSpec((k, m), lambda b, i: (0, 0))
    return pl.pallas_call(
        _mixer_out_kernel,
        grid=(bsz, l // tm),
        in_specs=[pl.BlockSpec((None, None, tm, k), lambda b, i: (0, b, i, 0)),
                  pl.BlockSpec((None, None, tm, k), lambda b, i: (1, b, i, 0)),
                  pl.BlockSpec((None, tm, k), lambda b, i: (b, i, gb)),
                  pl.BlockSpec((1, LANES), lambda b, i: (0, 0)),
                  other_spec, wspec, wspec, row],
        out_specs=row,
        out_shape=jax.ShapeDtypeStruct((bsz, l, m), jnp.float32),
        compiler_params=pltpu.CompilerParams(dimension_semantics=("parallel", "parallel"),
                                             vmem_limit_bytes=VMEM_LIMIT_BYTES),
        name="mixer_out_proj",
    )(o2, o2, proj3, out_gain.reshape(1, LANES).astype(jnp.float32), other, w_bidir, w_other, res)


HG_LEVELS = tuple(CHUNK >> (i + 1) for i in range(CHUNK.bit_length() - 1))
HG_TOT_ROWS = 8
HG_TT = 1024


def hgrn2_constants():
    import numpy as np
    c = CHUNK
    r = np.arange(c)[:, None]
    u = np.arange(c)[None, :]
    stacks, masks = [], []
    for direction in range(2):
        fwd = direction == 0
        lvl_masks = []
        for m in HG_LEVELS:
            blk = r // (2 * m)
            later = (r % (2 * m)) >= m
            lvl_masks.append((blk == blk.T) & (later & ~later.T if fwd else ~later & later.T))
        stacks.append(np.concatenate([(u <= r) if fwd else (u >= r), np.ones((HG_TOT_ROWS, c), bool)], axis=0))
        masks.append(np.stack(lvl_masks))
    return (jnp.asarray(np.stack(stacks), jnp.bfloat16), jnp.asarray(np.stack(masks), jnp.float32))


def _hgrn2_kernel(q_ref, f_ref, v_ref, loglb_ref, log1mlb_ref, onemlb_ref, ast_ref, mask_ref, o_ref,
                  st_sc, qd_sc, dec_sc, upd_sc, sin_sc, *, nc):
    direction = pl.program_id(2)

    @pl.when(pl.program_id(3) == 0)
    def _():
        st_sc[...] = jnp.zeros_like(st_sc)

    bf16 = jnp.bfloat16
    f32 = jnp.float32
    c = CHUNK
    hd = A_HEAD_DIM
    dirf = direction.astype(f32)
    loglb = loglb_ref[...]
    log1mlb = log1mlb_ref[...]
    onemlb = onemlb_ref[...]
    ast = ast_ref[...]
    contract_last = (((1,), (1,)), ((), ()))
    contract_first = (((0,), (0,)), ((), ()))

    ns = range(nc)
    rows = [slice(n * c, (n + 1) * c) for n in ns]
    z = [f_ref[r, :] for r in rows]
    v = [v_ref[r, :] for r in rows]
    qr = [q_ref[r, :] for r in rows]
    q = [x * jax.nn.sigmoid(x) for x in qr]
    e = [jnp.exp(-jnp.abs(x)) for x in z]
    cc = [log1mlb + jnp.minimum(z[n], 0.0) - jnp.log1p(e[n]) for n in ns]
    lf = [jnp.maximum(loglb, x) + jnp.log1p(jnp.exp(-jnp.abs(loglb - x))) for x in cc]
    k = [onemlb * jnp.where(z[n] >= 0, e[n], 1.0) / (1.0 + e[n]) for n in ns]
    hi = [x.astype(bf16) for x in lf]
    lo = [(lf[n] - hi[n].astype(f32)).astype(bf16) for n in ns]
    d = [jnp.dot(ast, hi[n], preferred_element_type=f32) + jnp.dot(ast, lo[n], preferred_element_type=f32)
         for n in ns]
    cum = [x[0:c] for x in d]
    tot = [x[c:c + HG_TOT_ROWS] for x in d]
    ref = [cum[n] - dirf * lf[n] for n in ns]
    attn = [jnp.zeros((c, c), f32) for _ in ns]
    for li, m in enumerate(HG_LEVELS):
        nb = c // (2 * m)
        split = [jnp.broadcast_to(x.reshape(nb, 2 * m, hd)[:, m - 1:m, :], (nb, 2 * m, hd)).reshape(c, hd)
                 for x in ref]
        x = [jnp.exp(-jnp.abs(cum[n] - split[n])) for n in ns]
        s = [lax.dot_general((q[n] * x[n]).astype(bf16), (k[n] * x[n]).astype(bf16), contract_last,
                             preferred_element_type=f32) for n in ns]
        attn = [attn[n] + mask_ref[li] * s[n] for n in ns]
    vb = [x.astype(bf16) for x in v]
    intra = [jnp.dot(attn[n].astype(bf16), vb[n], preferred_element_type=f32) for n in ns]
    upd = [lax.dot_general(vb[n], (k[n] * jnp.exp(tot[n][0:1] - cum[n])).astype(bf16), contract_first,
                           preferred_element_type=f32) for n in ns]
    for n in ns:
        o_ref[rows[n], :] = intra[n] + jnp.sum(q[n] * k[n], axis=-1, keepdims=True) * v[n]
        qd_sc[n] = (q[n] * jnp.exp(cum[n])).astype(bf16)
        dec_sc[n] = jnp.exp(tot[n])
        upd_sc[n] = upd[n]

    def body(ci, st):
        ce = ci + direction * (nc - 1 - 2 * ci)
        sin_sc[ce] = st.astype(bf16)
        return st * dec_sc[ce][0:1] + upd_sc[ce]

    st_sc[...] = lax.fori_loop(0, nc, body, st_sc[...])

    for n in range(nc):
        rows = slice(n * c, (n + 1) * c)
        o_ref[rows, :] += lax.dot_general(qd_sc[n], sin_sc[n], contract_last, preferred_element_type=f32)


def hgrn2_scan(proj3, lb):
    bsz, l, _ = proj3.shape
    hd = A_HEAD_DIM
    tt = min(HG_TT, l)
    nt = l // tt
    assert l % tt == 0 and tt % CHUNK == 0
    ast, masks = hgrn2_constants()
    lb = lb.astype(jnp.float32)
    vecs = [jnp.log(lb).reshape(2, 1, A_WIDTH), jnp.log1p(-lb).reshape(2, 1, A_WIDTH), (1.0 - lb).reshape(2, 1, A_WIDTH)]
    tidx = lambda d, i: i + d * (nt - 1 - 2 * i)
    vec = pl.BlockSpec((None, 1, hd), lambda b, h, d, i: (d, 0, h))
    return pl.pallas_call(
        functools.partial(_hgrn2_kernel, nc=tt // CHUNK),
        grid=(bsz, A_HEADS, 2, nt),
        in_specs=[pl.BlockSpec((None, tt, hd), lambda b, h, d, i: (b, tidx(d, i), h)),
                  pl.BlockSpec((None, tt, hd), lambda b, h, d, i: (b, tidx(d, i), (1 + d) * A_HEADS + h)),
                  pl.BlockSpec((None, tt, hd), lambda b, h, d, i: (b, tidx(d, i), 3 * A_HEADS + h)),
                  vec, vec, vec,
                  pl.BlockSpec((None,) + ast.shape[1:], lambda b, h, d, i: (d, 0, 0)),
                  pl.BlockSpec((None,) + masks.shape[1:], lambda b, h, d, i: (d, 0, 0, 0))],
        out_specs=pl.BlockSpec((None, None, tt, hd), lambda b, h, d, i: (d, b, tidx(d, i), h)),
        out_shape=jax.ShapeDtypeStruct((2, bsz, l, A_WIDTH), jnp.float32),
        scratch_shapes=[pltpu.VMEM((hd, hd), jnp.float32),
                        pltpu.VMEM((tt // CHUNK, CHUNK, hd), jnp.bfloat16),
                        pltpu.VMEM((tt // CHUNK, HG_TOT_ROWS, hd), jnp.float32),
                        pltpu.VMEM((tt // CHUNK, hd, hd), jnp.float32),
                        pltpu.VMEM((tt // CHUNK, hd, hd), jnp.bfloat16)],
        compiler_params=pltpu.CompilerParams(dimension_semantics=("parallel", "parallel", "parallel", "arbitrary"),
                                             vmem_limit_bytes=VMEM_LIMIT_BYTES),
        name="hgrn2_scan",
    )(proj3, proj3, proj3, *vecs, ast, masks)


S5_NS = S5_GROUPS * S5_STATE
S5_TT = 128
SUBLANES = 8


def _s5_scan_kernel(u_ref, win_ref, ar_ref, ai_ref, wout_ref, y_ref, bu_sc, xs_sc, st_sc, *, bsz, tt, reverse):
    @pl.when(pl.program_id(0) == 0)
    def _():
        st_sc[...] = jnp.zeros_like(st_sc)

    ub = u_ref[...].astype(jnp.bfloat16)
    halves = 2
    uw = B_WIDTH // halves
    sw = S5_NS // halves
    for hf in range(halves):
        for part in range(2):
            sc = slice(part * S5_NS + hf * sw, part * S5_NS + (hf + 1) * sw)
            bu_sc[:, sc] = jnp.dot(ub[:, hf * uw:(hf + 1) * uw], win_ref[hf * uw:(hf + 1) * uw, sc],
                                   preferred_element_type=jnp.float32)
    ar = jnp.broadcast_to(ar_ref[...], (bsz, S5_NS))
    ai = jnp.broadcast_to(ai_ref[...], (bsz, S5_NS))
    per = SUBLANES // bsz
    ngroups = tt // per

    def body(s, carry):
        xr, xi = carry
        p = (ngroups - 1 - s) if reverse else s
        base = pl.multiple_of(p * SUBLANES, SUBLANES)
        blk = bu_sc[pl.ds(base, SUBLANES), :]
        outs_r = [None] * per
        outs_i = [None] * per
        for ph in (range(per - 1, -1, -1) if reverse else range(per)):
            br = blk[ph * bsz:(ph + 1) * bsz, :S5_NS]
            bi = blk[ph * bsz:(ph + 1) * bsz, S5_NS:]
            xr, xi = ar * xr - ai * xi + br, ar * xi + ai * xr + bi
            outs_r[ph] = xr
            outs_i[ph] = xi
        xs_sc[pl.ds(base, SUBLANES), :S5_NS] = jnp.concatenate(outs_r, axis=0)
        xs_sc[pl.ds(base, SUBLANES), S5_NS:] = jnp.concatenate(outs_i, axis=0)
        return xr, xi

    xr, xi = lax.fori_loop(0, ngroups, body, (st_sc[0], st_sc[1]))
    st_sc[0] = xr
    st_sc[1] = xi
    for hf in range(halves):
        yc = slice(hf * uw, (hf + 1) * uw)
        acc = None
        for part in range(2):
            sc = slice(part * S5_NS + hf * sw, part * S5_NS + (hf + 1) * sw)
            term = jnp.dot(xs_sc[:, sc].astype(jnp.bfloat16), wout_ref[sc, yc], preferred_element_type=jnp.float32)
            acc = term if acc is None else acc + term
        y_ref[:, yc] = acc


def s5_scan(u_tb, win, ar, ai, wout, *, bsz, reverse):
    n = u_tb.shape[0]
    rows = S5_TT * bsz
    nt = n // rows
    assert n % rows == 0 and SUBLANES % bsz == 0
    idx = (lambda i: (nt - 1 - i, 0)) if reverse else (lambda i: (i, 0))
    const = lambda i: (0, 0)
    return pl.pallas_call(
        functools.partial(_s5_scan_kernel, bsz=bsz, tt=S5_TT, reverse=reverse),
        grid=(nt,),
        in_specs=[pl.BlockSpec((rows, B_WIDTH), idx),
                  pl.BlockSpec((B_WIDTH, 2 * S5_NS), const),
                  pl.BlockSpec((1, S5_NS), const),
                  pl.BlockSpec((1, S5_NS), const),
                  pl.BlockSpec((2 * S5_NS, B_WIDTH), const)],
        out_specs=pl.BlockSpec((rows, B_WIDTH), idx),
        out_shape=jax.ShapeDtypeStruct((n, B_WIDTH), jnp.float32),
        scratch_shapes=[pltpu.VMEM((rows, 2 * S5_NS), jnp.float32),
                        pltpu.VMEM((rows, 2 * S5_NS), jnp.float32),
                        pltpu.VMEM((2, bsz, S5_NS), jnp.float32)],
        compiler_params=pltpu.CompilerParams(dimension_semantics=("arbitrary",),
                                             vmem_limit_bytes=VMEM_LIMIT_BYTES),
        name="s5_scan_bwd" if reverse else "s5_scan_fwd",
    )(u_tb, win, ar, ai, wout)


def _s5_final_kernel(u_ref, yf_ref, yb_ref, d_ref, w_ref, b_ref, o_ref):
    y = d_ref[...] * u_ref[...] + yf_ref[...] + yb_ref[...]
    y = jax.nn.gelu(y)
    z = jnp.dot(y.astype(jnp.bfloat16), w_ref[...], preferred_element_type=jnp.float32) + b_ref[...]
    o_ref[...] = y * jax.nn.sigmoid(z)


def s5_finalize(u, yf, yb, d_skip, glu_w, glu_b, *, tm=512):
    n, w = u.shape
    row = pl.BlockSpec((tm, w), lambda i: (i, 0))
    vec = pl.BlockSpec((1, w), lambda i: (0, 0))
    return pl.pallas_call(
        _s5_final_kernel,
        grid=(n // tm,),
        in_specs=[row, row, row, vec, pl.BlockSpec((w, w), lambda i: (0, 0)), vec],
        out_specs=row,
        out_shape=jax.ShapeDtypeStruct((n, w), jnp.float32),
        compiler_params=pltpu.CompilerParams(dimension_semantics=("parallel",),
                                             vmem_limit_bytes=VMEM_LIMIT_BYTES),
        name="s5_finalize",
    )(u, yf, yb, d_skip.reshape(1, w).astype(jnp.float32), glu_w.astype(jnp.bfloat16),
      glu_b.reshape(1, w).astype(jnp.float32))


def s5_direction_params(lam_re, lam_im, log_step, b_re, b_im, c_re, c_im):
    step = jnp.exp(log_step)[:, None]
    mag = jnp.exp(lam_re * step)
    abar_re = mag * jnp.cos(lam_im * step)
    abar_im = mag * jnp.sin(lam_im * step)
    den = lam_re * lam_re + lam_im * lam_im
    fr = ((abar_re - 1.0) * lam_re + abar_im * lam_im) / den
    fi = (abar_im * lam_re - (abar_re - 1.0) * lam_im) / den
    bb_re = fr[..., None] * b_re - fi[..., None] * b_im
    bb_im = fr[..., None] * b_im + fi[..., None] * b_re
    eye = jnp.eye(S5_GROUPS, dtype=jnp.float32)
    win = jnp.concatenate([jnp.einsum('gnp,gh->gphn', bb, eye).reshape(B_WIDTH, S5_NS) for bb in (bb_re, bb_im)],
                          axis=1)
    wout = jnp.concatenate([jnp.einsum('gpn,gh->hngp', c, eye).reshape(S5_NS, B_WIDTH) for c in (c_re, -c_im)],
                           axis=0)
    return (win.astype(jnp.bfloat16), abar_re.reshape(1, S5_NS), abar_im.reshape(1, S5_NS),
            wout.astype(jnp.bfloat16))


def s5_mixer_tb(u_tb, bsz, lam_re, lam_im, log_step, b_re, b_im, c_re, c_im, d_skip, glu_w, glu_b):
    f32 = jnp.float32
    ys = []
    for direction in range(2):
        prm = s5_direction_params(lam_re[direction].astype(f32), lam_im[direction].astype(f32),
                                  log_step[direction].astype(f32), b_re[direction].astype(f32),
                                  b_im[direction].astype(f32), c_re[direction].astype(f32),
                                  c_im[direction].astype(f32))
        ys.append(s5_scan(u_tb, *prm, bsz=bsz, reverse=(direction == 1)))
    return s5_finalize(u_tb, ys[0], ys[1], d_skip, glu_w, glu_b)


def t5_bucket(rel):
    half = REL_BUCKETS // 2
    max_exact = half // 2
    base = jnp.where(rel > 0, half, 0)
    n = jnp.abs(rel)
    nf = jnp.maximum(n, 1).astype(jnp.float32)
    large = max_exact + (jnp.log(nf / max_exact) / math.log(REL_MAX_DIST / max_exact)
                         * (half - max_exact)).astype(jnp.int32)
    large = jnp.minimum(large, half - 1)
    return base + jnp.where(n < max_exact, n, large)


ATT_T = 512
LOG2E = math.log2(math.e)


def rel_bias_tiles(rel_bias, t):
    assert t >= REL_MAX_DIST
    table = rel_bias.astype(jnp.float32) * LOG2E
    tiles = []
    for d in (-1, 0, 1):
        c = table[t5_bucket(d * t + jnp.arange(-(t - 1), t))]
        w = jnp.concatenate([c, c[:1]], axis=0)
        m = jnp.tile(w, (t, 1))[:t * (2 * t - 1)].reshape(t, 2 * t - 1, -1)
        tiles.append(m[:, t - 1:2 * t - 1])
    far_neg = jnp.broadcast_to(table[t5_bucket(jnp.array(-2 * t))], tiles[0].shape)
    far_pos = jnp.broadcast_to(table[t5_bucket(jnp.array(2 * t))], tiles[0].shape)
    out = jnp.stack([far_neg] + tiles + [far_pos], axis=0)
    return jnp.transpose(out, (3, 0, 1, 2))


def _attn_operands(q, k, v, qg, kg):
    f32 = jnp.float32
    bf16 = jnp.bfloat16
    lane = lax.broadcasted_iota(jnp.int32, q.shape, 1)
    lo = lane < C_HEAD_DIM

    def half_sums(sq):
        return (jnp.sum(jnp.where(lo, sq, 0.0), axis=-1, keepdims=True),
                jnp.sum(jnp.where(lo, 0.0, sq), axis=-1, keepdims=True))

    def halfnorm(x, g):
        s_lo, s_hi = half_sums(x * x)
        return x * lax.rsqrt(jnp.where(lo, s_lo, s_hi) * (1.0 / C_HEAD_DIM) + EPS) * g

    def max_sq_norms(xb):
        n_lo, n_hi = half_sums(xb.astype(f32) * xb.astype(f32))
        return jnp.max(n_lo, axis=0, keepdims=True), jnp.max(n_hi, axis=0, keepdims=True)

    qn = halfnorm(q, qg) * (C_HEAD_DIM ** -0.5 * LOG2E)
    kn = halfnorm(k, kg)
    qb = qn.astype(bf16)
    kb = kn.astype(bf16)
    q2 = (jnp.where(lo, qb, 0.0).astype(bf16), jnp.where(lo, 0.0, qb).astype(bf16))
    q_lo, q_hi = max_sq_norms(qb)
    k_lo, k_hi = max_sq_norms(kb)
    sub = lax.broadcasted_iota(jnp.int32, (SUBLANES, q.shape[1]), 0)
    stats = jnp.where(sub == 0, q_lo, jnp.where(sub == 1, q_hi, jnp.where(sub == 2, k_lo,
                      jnp.where(sub == 3, k_hi, 0.0))))
    return q2, kn.T.astype(bf16), v.astype(bf16), stats


def _odd_proj_kernel(x_ref, g_ref, w_ref, qg_ref, kg_ref, o_ref, q2_ref, kt_ref, vb_ref, st_ref):
    x = x_ref[...]
    y = (x * lax.rsqrt(jnp.mean(x * x, axis=-1, keepdims=True) + EPS) * g_ref[...]).astype(jnp.bfloat16)
    hw = 2 * C_HEAD_DIM
    q, k, v = (jnp.dot(y, w_ref[:, p * C_WIDTH:(p + 1) * C_WIDTH], preferred_element_type=jnp.float32)
               for p in range(3))
    for h in range(C_HEADS):
        cols = slice(h * hw, (h + 1) * hw)
        q2, kt, vb, stats = _attn_operands(q[:, cols], k[:, cols], v[:, cols], qg_ref[...], kg_ref[...])
        q2_ref[h, 0] = q2[0]
        q2_ref[h, 1] = q2[1]
        kt_ref[h] = kt
        vb_ref[h] = vb
        st_ref[h] = stats
    att = 3 * C_WIDTH
    for c0 in range(att, w_ref.shape[1], PROJ_COLS):
        c1 = min(c0 + PROJ_COLS, w_ref.shape[1])
        o_ref[:, c0 - att:c1 - att] = jnp.dot(y, w_ref[:, c0:c1], preferred_element_type=jnp.float32)


def odd_in_proj(x, gain, w, q_gain, k_gain, *, tm=PROJ_TM):
    bsz, l, kdim = x.shape
    m = w.shape[1]
    rest = m - 3 * C_WIDTH
    tm = min(tm, l)
    hw = 2 * C_HEAD_DIM
    gq = jnp.tile(q_gain.astype(jnp.float32), 2).reshape(1, hw)
    gk = jnp.tile(k_gain.astype(jnp.float32), 2).reshape(1, hw)
    vec = pl.BlockSpec((1, hw), lambda b, i: (0, 0))
    return pl.pallas_call(
        _odd_proj_kernel,
        grid=(bsz, l // tm),
        in_specs=[pl.BlockSpec((None, tm, kdim), lambda b, i: (b, i, 0)),
                  pl.BlockSpec((1, kdim), lambda b, i: (0, 0)),
                  pl.BlockSpec((kdim, m), lambda b, i: (0, 0), pipeline_mode=pl.Buffered(1)),
                  vec, vec],
        out_specs=[pl.BlockSpec((None, tm, rest), lambda b, i: (b, i, 0)),
                   pl.BlockSpec((None, C_HEADS, 2, tm, hw), lambda b, i: (b, 0, 0, i, 0)),
                   pl.BlockSpec((None, C_HEADS, hw, tm), lambda b, i: (b, 0, 0, i)),
                   pl.BlockSpec((None, C_HEADS, tm, hw), lambda b, i: (b, 0, i, 0)),
                   pl.BlockSpec((None, C_HEADS, None, SUBLANES, hw), lambda b, i: (b, 0, i, 0, 0))],
        out_shape=[jax.ShapeDtypeStruct((bsz, l, rest), jnp.float32),
                   jax.ShapeDtypeStruct((bsz, C_HEADS, 2, l, hw), jnp.bfloat16),
                   jax.ShapeDtypeStruct((bsz, C_HEADS, hw, l), jnp.bfloat16),
                   jax.ShapeDtypeStruct((bsz, C_HEADS, l, hw), jnp.bfloat16),
                   jax.ShapeDtypeStruct((bsz, C_HEADS, l // tm, SUBLANES, hw), jnp.float32)],
        compiler_params=pltpu.CompilerParams(dimension_semantics=("parallel", "parallel"),
                                             vmem_limit_bytes=VMEM_LIMIT_BYTES),
        name="odd_in_proj",
    )(x, gain.reshape(1, kdim).astype(jnp.float32), w.astype(jnp.bfloat16), gq, gk)


ATT_ROWS = 64
ATT_SAFE_GAP = 100.0


def _attn_kernel(lam_ref, kmax_ref, bmax_ref, q2_ref, kt_ref, v_ref, bias_ref, g_ref, o_ref,
                 m_sc, l_sc, acc_sc, s_sc, p_sc, a_sc, *, t, nk, out_scale, bounded):
    f32 = jnp.float32
    b = pl.program_id(0)
    h = pl.program_id(1)
    qi = pl.program_id(2)
    q2 = q2_ref[...].reshape(2 * t, 2 * C_HEAD_DIM)
    r = ATT_ROWS
    hw = 2 * C_HEAD_DIM
    if bounded:
        q2f = q2.astype(f32)
        nq = jnp.sqrt(jnp.sum(q2f * q2f, axis=-1, keepdims=True))
        row = lax.broadcasted_iota(jnp.int32, nq.shape, 0)
        kc = jnp.where(row < t, kmax_ref[(b * C_HEADS + h) * 2], kmax_ref[(b * C_HEADS + h) * 2 + 1])
        m_sc[...] = jnp.broadcast_to(nq * kc + bmax_ref[h], m_sc.shape)
    else:
        m_sc[...] = jnp.full(m_sc.shape, -jnp.inf, f32)
    l_sc[...] = jnp.zeros_like(l_sc)
    acc_sc[...] = jnp.zeros_like(acc_sc)

    def body(ki, carry):
        off = pl.multiple_of(ki * t, t)
        bidx = jnp.clip(ki - qi, -2, 2) + 2
        s_sc[...] = jnp.dot(q2, kt_ref[:, pl.ds(off, t)], preferred_element_type=f32)
        for g in range(2 * t // r):
            rows = slice(g * r, (g + 1) * r)
            brow = (g * r) % t
            s = s_sc[rows, :] + bias_ref[bidx, brow:brow + r, :]
            m = m_sc[rows, :]
            if not bounded:
                m_prev = m
                m = jnp.maximum(m_prev, jnp.max(s, axis=-1, keepdims=True))
                alpha = jnp.exp2(m_prev - m)
                m_sc[rows, :] = m
                a_sc[rows, :] = alpha
            ps = [jnp.exp2(s[:, j * hw:(j + 1) * hw] - m) for j in range(t // hw)]
            psum = jnp.sum(sum(ps), axis=-1, keepdims=True)
            l_sc[rows, :] = (l_sc[rows, :] if bounded else alpha * l_sc[rows, :]) + psum
            for j in range(t // hw):
                p_sc[rows, j * hw:(j + 1) * hw] = ps[j].astype(jnp.bfloat16)
        pv = jnp.dot(p_sc[...], v_ref[pl.ds(off, t), :], preferred_element_type=f32)
        acc_sc[...] = (acc_sc[...] if bounded else a_sc[...] * acc_sc[...]) + pv
        return carry

    lax.fori_loop(0, nk, body, 0)
    a = acc_sc[...] / l_sc[...]
    o = a[:t] - lam_ref[0] * a[t:]
    y = o * lax.rsqrt(jnp.mean(o * o, axis=-1, keepdims=True) + EPS)
    o_ref[...] = y * g_ref[...] * out_scale


def diff_attention(q2, kt, vb, stats, lam, out_gain, bias5, layer_idx):
    f32 = jnp.float32
    bsz, _, l, _ = vb.shape
    t = ATT_T
    hw = 2 * C_HEAD_DIM
    lam_init = 0.8 - 0.6 * math.exp(-0.3 * layer_idx)
    lam_f = lam.astype(f32)
    lam_full = jnp.exp(jnp.sum(lam_f[0] * lam_f[1])) - jnp.exp(jnp.sum(lam_f[2] * lam_f[3])) + lam_init
    norms = jnp.sqrt(jnp.max(stats[..., 0:4, 0], axis=2)) * (1.0 + 1e-3)
    qmax, kmax = norms[..., 0:2], norms[..., 2:4]
    bmax = jnp.max(bias5, axis=(1, 2, 3))
    bmin = jnp.min(bias5, axis=(1, 2, 3))
    gap = 2.0 * qmax * kmax + (bmax - bmin)[None, :, None]
    smem = pl.BlockSpec(memory_space=pltpu.SMEM)

    def run(bounded):
        return pl.pallas_call(
            functools.partial(_attn_kernel, t=t, nk=l // t, out_scale=1.0 - lam_init, bounded=bounded),
            grid=(bsz, C_HEADS, l // t),
            in_specs=[smem, smem, smem,
                      pl.BlockSpec((None, None, 2, t, hw), lambda b, h, i: (b, h, 0, i, 0)),
                      pl.BlockSpec((None, None, hw, l), lambda b, h, i: (b, h, 0, 0)),
                      pl.BlockSpec((None, None, l, hw), lambda b, h, i: (b, h, 0, 0)),
                      pl.BlockSpec((None, 5, t, t), lambda b, h, i: (h, 0, 0, 0)),
                      pl.BlockSpec((1, hw), lambda b, h, i: (0, 0))],
            out_specs=pl.BlockSpec((None, t, hw), lambda b, h, i: (b, i, h)),
            out_shape=jax.ShapeDtypeStruct((bsz, l, C_WIDTH), f32),
            scratch_shapes=[pltpu.VMEM((2 * t, hw), f32), pltpu.VMEM((2 * t, hw), f32), pltpu.VMEM((2 * t, hw), f32),
                            pltpu.VMEM((2 * t, t), f32), pltpu.VMEM((2 * t, t), jnp.bfloat16),
                            pltpu.VMEM((2 * t, hw), f32)],
            compiler_params=pltpu.CompilerParams(dimension_semantics=("parallel", "parallel", "arbitrary"),
                                                 vmem_limit_bytes=VMEM_LIMIT_BYTES),
            name="diff_attention_bounded" if bounded else "diff_attention_online",
        )(lam_full.reshape(1), kmax.reshape(-1), bmax, q2, kt, vb, bias5, out_gain.reshape(1, hw).astype(f32))

    return lax.cond(jnp.all(gap < ATT_SAFE_GAP), lambda: run(True), lambda: run(False))


GDN_TT = 512
GDN_HEADS_PER_STEP = 4
LANES = 128
OD_QKV_BLOCK = 0
OD_GATE_BLOCK = OD_QKV_BLOCK + 3 * D_WIDTH // LANES
OD_AB_BLOCK = OD_GATE_BLOCK + D_WIDTH // LANES
OD_COLS = 2304


def _gdn_prep_kernel(prev_ref, cur_ref, next_ref, w_ref, o_ref, *, tl, nl):
    i = pl.program_id(1)
    part = pl.program_id(2)
    prev = jnp.where(i > 0, prev_ref[...], 0.0)
    nxt = jnp.where(i < nl - 1, next_ref[...], 0.0)
    ext = jnp.concatenate([prev, cur_ref[...], nxt], axis=0)
    halo = prev.shape[0]
    acc = None
    for j in range(CONV_WIDTH):
        start = halo - CONV_WIDTH // 2 + j
        term = w_ref[j:j + 1, :] * ext[start:start + tl, :]
        acc = term if acc is None else acc + term
    y = acc * jax.nn.sigmoid(acc)
    scale = jnp.where(part == 0, D_HEAD_DIM ** -0.5, 1.0)
    heads = []
    for h in range(D_HEADS):
        yh = y[:, h * LANES:(h + 1) * LANES]
        heads.append(yh * (lax.rsqrt(jnp.sum(yh * yh, axis=-1, keepdims=True) + EPS) * scale))
    o_ref[...] = jnp.where(part < 2, jnp.concatenate(heads, axis=1), y)


def gdn_prep(proj3, conv_w, *, tl=1024):
    bsz, l, _ = proj3.shape
    halo = SUBLANES
    nl = l // tl
    blk0 = OD_QKV_BLOCK * LANES // D_WIDTH
    return pl.pallas_call(
        functools.partial(_gdn_prep_kernel, tl=tl, nl=nl),
        grid=(bsz, nl, 3),
        in_specs=[pl.BlockSpec((None, halo, D_WIDTH), lambda b, i, p: (b, jnp.maximum(i * (tl // halo) - 1, 0), blk0 + p)),
                  pl.BlockSpec((None, tl, D_WIDTH), lambda b, i, p: (b, i, blk0 + p)),
                  pl.BlockSpec((None, halo, D_WIDTH),
                               lambda b, i, p: (b, jnp.minimum((i + 1) * (tl // halo), l // halo - 1), blk0 + p)),
                  pl.BlockSpec((CONV_WIDTH, D_WIDTH), lambda b, i, p: (0, p))],
        out_specs=pl.BlockSpec((None, None, tl, D_WIDTH), lambda b, i, p: (p, b, i, 0)),
        out_shape=jax.ShapeDtypeStruct((3, bsz, l, D_WIDTH), jnp.float32),
        compiler_params=pltpu.CompilerParams(dimension_semantics=("parallel", "parallel", "parallel"),
                                             vmem_limit_bytes=VMEM_LIMIT_BYTES),
        name="gdn_prep",
    )(proj3, proj3, proj3, conv_w.astype(jnp.float32))


def _gdn_gates_kernel(x_ref, nega_ref, dtb_ref, o_ref):
    x = x_ref[...]
    z = x + dtb_ref[...]
    g = nega_ref[...] * (jnp.maximum(z, 0.0) + jnp.log1p(jnp.exp(-jnp.abs(z))))
    lane = lax.broadcasted_iota(jnp.int32, x.shape, 1)
    y = jnp.where(lane < 2 * D_HEADS, g, jax.nn.sigmoid(x))
    o_ref[...] = y.T[0:4 * D_HEADS, :]


def gdn_gates(proj3, a_log, dt_bias, *, tl=512):
    bsz, l, _ = proj3.shape
    pad = LANES - 2 * D_HEADS
    nega = jnp.pad(-jnp.exp(a_log.astype(jnp.float32)).reshape(1, -1), ((0, 0), (0, pad)))
    dtb = jnp.pad(dt_bias.astype(jnp.float32).reshape(1, -1), ((0, 0), (0, pad)))
    vec = pl.BlockSpec((1, LANES), lambda b, i: (0, 0))
    return pl.pallas_call(
        _gdn_gates_kernel,
        grid=(bsz, l // tl),
        in_specs=[pl.BlockSpec((None, tl, LANES), lambda b, i: (b, i, OD_AB_BLOCK)), vec, vec],
        out_specs=pl.BlockSpec((None, 4 * D_HEADS, tl), lambda b, i: (b, 0, i)),
        out_shape=jax.ShapeDtypeStruct((bsz, 4 * D_HEADS, l), jnp.float32),
        compiler_params=pltpu.CompilerParams(dimension_semantics=("parallel", "parallel"),
                                             vmem_limit_bytes=VMEM_LIMIT_BYTES),
        name="gdn_gates",
    )(proj3, nega, dtb)


def gdn_constants():
    import numpy as np
    c = CHUNK
    r = np.arange(c)[:, None]
    u = np.arange(c)[None, :]
    cum, incl, strict = [], [], []
    for direction in range(2):
        fwd = direction == 0
        cum.append(np.concatenate([(r <= u) if fwd else (r >= u), np.ones((c, c), bool)], axis=1))
        incl.append((u <= r) if fwd else (u >= r))
        strict.append((u < r) if fwd else (u > r))
    same = lambda b: (r // b) == (u // b)
    merges = [same(2 * b) & ~same(b) for b in (8, 16, 32)]
    f32 = jnp.float32
    return (jnp.asarray(np.stack(cum), jnp.bfloat16), jnp.asarray(np.stack(incl), f32),
            jnp.asarray(np.stack(strict), f32), jnp.asarray(same(8), f32), jnp.asarray(np.stack(merges), f32))


def _gdn_kernel(q_ref, k_ref, v_ref, g_ref, b_ref, cum_ref, incl_ref, strict_ref, d8_ref, mrg_ref, o_ref,
                s_sc, qd_sc, dec_sc, w_sc, u_sc, sin_sc, *, nc):
    direction = pl.program_id(2)

    @pl.when(pl.program_id(3) == 0)
    def _():
        s_sc[...] = jnp.zeros_like(s_sc)

    bf16 = jnp.bfloat16
    f32 = jnp.float32
    c = CHUNK
    hd = D_HEAD_DIM
    contract_last = (((1,), (1,)), ((), ()))
    contract_first = (((0,), (0,)), ((), ()))
    cumm = cum_ref[...]
    incl = incl_ref[...]
    strict = strict_ref[...]
    d8 = d8_ref[...]
    eye = (lax.broadcasted_iota(jnp.int32, (c, c), 0) == lax.broadcasted_iota(jnp.int32, (c, c), 1)).astype(f32)

    def mm(a, b):
        return jnp.dot(a.astype(bf16), b.astype(bf16), preferred_element_type=f32)

    def rep(x):
        return jnp.concatenate([x] * (hd // c), axis=1)

    nh = g_ref.shape[0]
    ns = range(nh * nc)
    head = [m // nc for m in ns]
    rows = [slice((m % nc) * c, (m % nc + 1) * c) for m in ns]
    cols = [slice(h * hd, (h + 1) * hd) for h in head]
    q = [q_ref[rows[m], cols[m]] for m in ns]
    k = [k_ref[rows[m], cols[m]] for m in ns]
    v = [v_ref[rows[m], cols[m]] for m in ns]
    kb = [x.astype(bf16) for x in k]
    kk = [lax.dot_general(x, x, contract_last, preferred_element_type=f32) for x in kb]
    qk = [lax.dot_general(q[n].astype(bf16), kb[n], contract_last, preferred_element_type=f32) for n in ns]
    grow = [jnp.broadcast_to(g_ref[head[m], :, rows[m]], (c, c)) for m in ns]
    ghi = [x.astype(bf16) for x in grow]
    glo = [(grow[n] - ghi[n].astype(f32)).astype(bf16) for n in ns]
    gm = [jnp.dot(ghi[n], cumm, preferred_element_type=f32) + jnp.dot(glo[n], cumm, preferred_element_type=f32)
          for n in ns]
    gam_row = [x[:, :c] for x in gm]
    tot = [x[:, c:] for x in gm]
    gam_col = [x.T for x in gam_row]
    beta_col = [jnp.broadcast_to(b_ref[head[m], :, rows[m]], (c, c)).T for m in ns]
    decay = [incl * jnp.exp(jnp.minimum(gam_col[n] - gam_row[n], 0.0)) for n in ns]
    a = [strict * beta_col[n] * kk[n] * decay[n] for n in ns]
    a0 = [x * d8 for x in a]
    n2 = [mm(x, x) for x in a0]
    n4 = [mm(x, x) for x in n2]
    t = [mm(eye - a0[n], eye + n2[n]) for n in ns]
    t = [mm(t[n], eye + n4[n]) for n in ns]
    for j in range(mrg_ref.shape[0]):
        p = [mm(a[n] * mrg_ref[j], t[n]) for n in ns]
        t = [t[n] - mm(t[n], p[n]) for n in ns]
    beta128 = [rep(x) for x in beta_col]
    egam128 = [rep(jnp.exp(x)) for x in gam_col]
    solb = [mm(t[n], jnp.concatenate([k[n] * beta128[n] * egam128[n], v[n] * beta128[n]], axis=1)).astype(bf16)
            for n in ns]
    av = [jnp.dot((qk[n] * decay[n]).astype(bf16), solb[n], preferred_element_type=f32) for n in ns]
    k_dec = [(k[n] * rep(jnp.exp(tot[n] - gam_col[n]))).astype(bf16) for n in ns]
    wu = [lax.dot_general(k_dec[n], solb[n], contract_first, preferred_element_type=f32) for n in ns]
    for n in ns:
        qd_sc[n] = (q[n] * egam128[n] - av[n][:, :hd]).astype(bf16)
        o_ref[rows[n], cols[n]] = av[n][:, hd:]
        w_sc[n] = wu[n][:, :hd].astype(bf16)
        u_sc[n] = wu[n][:, hd:]
        dec_sc[n] = rep(jnp.exp(tot[n][0:SUBLANES, :]))

    def body(ci, states):
        ce = ci + direction * (nc - 1 - 2 * ci)
        new = []
        for h in range(nh):
            m = h * nc + ce
            sb = states[h].astype(bf16)
            sin_sc[m] = sb
            new.append(states[h] * dec_sc[m][0:1] - jnp.dot(w_sc[m], sb, preferred_element_type=f32) + u_sc[m])
        return tuple(new)

    states = lax.fori_loop(0, nc, body, tuple(s_sc[h] for h in range(nh)))
    for h in range(nh):
        s_sc[h] = states[h]

    for m in ns:
        o_ref[rows[m], cols[m]] += jnp.dot(qd_sc[m], sin_sc[m], preferred_element_type=f32)


def gdn_scan(qkv, gb):
    _, bsz, l, _ = qkv.shape
    hd = D_HEAD_DIM
    tt = min(GDN_TT, l)
    nt = l // tt
    nc = tt // CHUNK
    assert l % tt == 0 and tt % CHUNK == 0
    consts = gdn_constants()
    gb4 = gb.reshape(bsz, 4 * D_HEADS, 1, l)
    nh = GDN_HEADS_PER_STEP
    assert D_HEADS % nh == 0
    tidx = lambda d, i: i + d * (nt - 1 - 2 * i)
    qkv_spec = lambda p: pl.BlockSpec((None, None, tt, nh * hd), lambda b, h, d, i: (p, b, tidx(d, i), h))
    row_spec = lambda off: pl.BlockSpec((None, nh, 1, tt),
                                        lambda b, h, d, i: (b, (off + d * D_HEADS) // nh + h, 0, tidx(d, i)))
    per_dir = lambda a: pl.BlockSpec((None,) + a.shape[1:], lambda b, h, d, i: (d,) + (0,) * (a.ndim - 1))
    whole = lambda a: pl.BlockSpec(a.shape, lambda b, h, d, i: (0,) * a.ndim)
    return pl.pallas_call(
        functools.partial(_gdn_kernel, nc=nc),
        grid=(bsz, D_HEADS // nh, 2, nt),
        in_specs=[qkv_spec(0), qkv_spec(1), qkv_spec(2), row_spec(0), row_spec(2 * D_HEADS),
                  per_dir(consts[0]), per_dir(consts[1]), per_dir(consts[2]), whole(consts[3]), whole(consts[4])],
        out_specs=pl.BlockSpec((None, None, tt, nh * hd), lambda b, h, d, i: (d, b, tidx(d, i), h)),
        out_shape=jax.ShapeDtypeStruct((2, bsz, l, D_WIDTH), jnp.float32),
        scratch_shapes=[pltpu.VMEM((nh, hd, hd), jnp.float32),
                        pltpu.VMEM((nh * nc, CHUNK, hd), jnp.bfloat16),
                        pltpu.VMEM((nh * nc, SUBLANES, hd), jnp.float32),
                        pltpu.VMEM((nh * nc, hd, hd), jnp.bfloat16),
                        pltpu.VMEM((nh * nc, hd, hd), jnp.float32),
                        pltpu.VMEM((nh * nc, hd, hd), jnp.bfloat16)],
        compiler_params=pltpu.CompilerParams(dimension_semantics=("parallel", "parallel", "parallel", "arbitrary"),
                                             vmem_limit_bytes=VMEM_LIMIT_BYTES),
        name="gdn_scan",
    )(qkv, qkv, qkv, gb4, gb4, *consts)


def gated_deltanet(proj3, conv_w, a_log, dt_bias):
    return gdn_scan(gdn_prep(proj3, conv_w), gdn_gates(proj3, a_log, dt_bias))


MOE_TT = 512
MOE_SUB = 128
MOE_ALIGN = 64
MOE_TILES_PER_STEP = 4
MOE_ROWS = 256
MOE_SLAB = 256
MOE_VMEM_LIMIT_BYTES = 56 * 1024 * 1024
MOE_EXPERT_VMEM_LIMIT_BYTES = 60 * 1024 * 1024


def _router_kernel(x_ref, g_ref, wr_ref, h_ref, aff_ref):
    x = x_ref[...]
    h = (x * lax.rsqrt(jnp.mean(x * x, axis=-1, keepdims=True) + EPS) * g_ref[...]).astype(jnp.bfloat16)
    h_ref[...] = h
    logits = jnp.dot(h, wr_ref[...], preferred_element_type=jnp.float32)
    lane = lax.broadcasted_iota(jnp.int32, logits.shape, 1)
    logits = jnp.where(lane < N_EXPERTS, logits, -jnp.inf)
    p = jnp.exp(logits - jnp.max(logits, axis=-1, keepdims=True))
    aff = p / jnp.sum(p, axis=-1, keepdims=True)
    aff_ref[...] = aff.T[0:N_EXPERTS, :]


def moe_route(x, gain, w_router, *, tm=MOE_TT):
    bsz, l, d = x.shape
    wr = jnp.pad(w_router.astype(jnp.bfloat16), ((0, 0), (0, LANES - N_EXPERTS)))
    return pl.pallas_call(
        _router_kernel,
        grid=(bsz, l // tm),
        in_specs=[pl.BlockSpec((None, tm, d), lambda b, i: (b, i, 0)),
                  pl.BlockSpec((1, d), lambda b, i: (0, 0)),
                  pl.BlockSpec((d, LANES), lambda b, i: (0, 0))],
        out_specs=[pl.BlockSpec((None, tm, d), lambda b, i: (b, i, 0)),
                   pl.BlockSpec((None, N_EXPERTS, tm), lambda b, i: (b, 0, i))],
        out_shape=[jax.ShapeDtypeStruct((bsz, l, d), jnp.bfloat16),
                   jax.ShapeDtypeStruct((bsz, N_EXPERTS, l), jnp.float32)],
        compiler_params=pltpu.CompilerParams(dimension_semantics=("parallel", "parallel"),
                                             vmem_limit_bytes=VMEM_LIMIT_BYTES),
        name="moe_router",
    )(x, gain.reshape(1, d).astype(jnp.float32), wr)


def _select_kernel(aff_ref, pre_ref, smap_ref, gate_ref, cnt_ref, *, cap, tt):
    f32 = jnp.float32
    bf16 = jnp.bfloat16
    aff = aff_ref[...]
    e, l = aff.shape
    nl = l // LANES
    tiles = [slice(j * LANES, (j + 1) * LANES) for j in range(nl)]
    bits = pltpu.bitcast(aff, jnp.int32)
    bt = [bits[:, s] for s in tiles]

    def lane_total(x):
        return jnp.broadcast_to(jnp.sum(x, axis=-1, keepdims=True), (e, LANES))

    def search(i, thr):
        cand = thr | jnp.left_shift(jnp.int32(1), 30 - i)
        acc = jnp.zeros((e, LANES), jnp.int32)
        for x in bt:
            acc = acc + (x >= cand).astype(jnp.int32)
        return jnp.where(lane_total(acc) >= cap, cand, thr)

    thr = lax.fori_loop(0, 31, search, jnp.zeros((e, LANES), jnp.int32))
    gt = [x > thr for x in bt]
    eq = [x == thr for x in bt]
    acc = jnp.zeros((e, LANES), jnp.int32)
    for x in gt:
        acc = acc + x.astype(jnp.int32)
    need = (cap - lane_total(acc)).astype(f32)

    pre = pre_ref[...]

    def prefix(flags):
        outs = [jnp.dot(jnp.where(x, 1.0, 0.0).astype(bf16), pre, preferred_element_type=f32) for x in flags]
        carry = jnp.zeros((e, LANES), f32)
        res = []
        for o in outs:
            res.append(o[:, :LANES] + carry)
            carry = carry + o[:, LANES:]
        return res, [o[:, LANES:] for o in outs]

    rank_eq, _ = prefix(eq)
    sel = [jnp.logical_or(gt[j], jnp.logical_and(eq[j], rank_eq[j] < need)) for j in range(nl)]
    pos, totals = prefix(sel)
    lane = lax.broadcasted_iota(jnp.int32, (e, LANES), 1)
    cnt = jnp.zeros((e, LANES), f32)
    per = tt // LANES
    for j in range(nl):
        smap_ref[:, tiles[j]] = jnp.where(sel[j], pos[j], -1.0)
        gate_ref[:, tiles[j]] = jnp.where(sel[j], aff[:, tiles[j]], 0.0)
        cnt = cnt + jnp.where(lane == j // per, totals[j], 0.0)
    cnt_ref[...] = cnt


def moe_select(aff, cap, *, tt=MOE_TT):
    import numpy as np
    bsz, e, l = aff.shape
    assert l // tt <= LANES
    i = np.arange(LANES)
    pre = np.concatenate([i[:, None] < i[None, :], np.ones((LANES, LANES), bool)], axis=1)
    row = pl.BlockSpec((None, e, l), lambda b: (b, 0, 0))
    return pl.pallas_call(
        functools.partial(_select_kernel, cap=cap, tt=tt),
        grid=(bsz,),
        in_specs=[row, pl.BlockSpec((LANES, 2 * LANES), lambda b: (0, 0))],
        out_specs=[row, row, pl.BlockSpec((None, e, LANES), lambda b: (b, 0, 0))],
        out_shape=[jax.ShapeDtypeStruct((bsz, e, l), jnp.float32), jax.ShapeDtypeStruct((bsz, e, l), jnp.float32),
                   jax.ShapeDtypeStruct((bsz, e, LANES), jnp.float32)],
        compiler_params=pltpu.CompilerParams(dimension_semantics=("parallel",),
                                             vmem_limit_bytes=VMEM_LIMIT_BYTES),
        name="moe_select",
    )(aff, jnp.asarray(pre, jnp.bfloat16))


def _slot_one_hot(pos, base, rows, n):
    slot = (base + lax.broadcasted_iota(jnp.int32, (rows, n), 0)).astype(jnp.float32)
    return jnp.where(pos == slot, 1.0, 0.0).astype(jnp.bfloat16)


def _expert_kernel(cs_ref, h_ref, smap_ref, gate_ref, wg32_ref, wu32_ref, wd32_ref, o_ref,
                   xs_sc, gs_sc, wg_ref, wu_ref, wd_ref, *, nj, per, tt, cap):
    e = pl.program_id(0)
    b = pl.program_id(1)
    j = pl.program_id(2)
    f32 = jnp.float32
    bf16 = jnp.bfloat16

    @pl.when(jnp.logical_and(b == 0, j == 0))
    def _():
        wg_ref[...] = wg32_ref[...].astype(bf16)
        wu_ref[...] = wu32_ref[...].astype(bf16)
        wd_ref[...] = wd32_ref[...].astype(bf16)

    @pl.when(j == 0)
    def _():
        xs_sc[...] = jnp.zeros_like(xs_sc)
        gs_sc[...] = jnp.zeros_like(gs_sc)

    win = min(MOE_SUB, cap)
    for s in range(per):
        base = (b * N_EXPERTS + e) * (nj * per + 1) + j * per + s
        c0 = cs_ref[base]
        c1 = cs_ref[base + 1]
        cols = slice(s * tt, (s + 1) * tt)
        pos = smap_ref[:, cols]
        gate = gate_ref[:, cols]
        hb = h_ref[cols, :]

        def gather(r0, rows, pos=pos, gate=gate, hb=hb):
            oh = _slot_one_hot(pos, r0, rows, tt)
            xs_sc[pl.ds(r0, rows), :] += jnp.dot(oh, hb, preferred_element_type=f32)
            g = jnp.sum(oh.astype(f32) * gate, axis=-1, keepdims=True)
            gs_sc[pl.ds(r0, rows), :] += jnp.broadcast_to(g, (rows, LANES))

        w0 = pl.multiple_of(jnp.minimum(c0 // MOE_ALIGN * MOE_ALIGN, cap - win), MOE_ALIGN)
        gather(w0, win)

        def rest(st, carry, gather=gather):
            gather(pl.multiple_of(st * MOE_ALIGN, MOE_ALIGN), MOE_ALIGN)
            return carry

        lax.fori_loop((w0 + win) // MOE_ALIGN, (c1 + MOE_ALIGN - 1) // MOE_ALIGN, rest, 0)

    @pl.when(j == nj - 1)
    def _():
        rows_per = min(MOE_ROWS, cap)
        for r in range(cap // rows_per):
            rows = slice(r * rows_per, (r + 1) * rows_per)
            xb = xs_sc[rows, :].astype(bf16)
            g = jnp.dot(xb, wg_ref[...], preferred_element_type=f32)
            u = jnp.dot(xb, wu_ref[...], preferred_element_type=f32)
            hid = (g * jax.nn.sigmoid(g) * u).astype(bf16)
            out = jnp.dot(hid, wd_ref[...], preferred_element_type=f32)
            scale = jnp.concatenate([gs_sc[rows, :]] * (out.shape[1] // LANES), axis=1)
            o_ref[rows, :] = (out * scale).astype(bf16)


def moe_experts(hb, smap, gate, cs, w_gate, w_up, w_down, layer, cap, *, tt=MOE_TT):
    bsz, l, d = hb.shape
    _, e, _, ff = w_gate.shape
    per = MOE_TILES_PER_STEP if (l // tt) % MOE_TILES_PER_STEP == 0 else 1
    nj = l // (tt * per)
    smap4 = smap.reshape(bsz, e, 1, l)
    gate4 = gate.reshape(bsz, e, 1, l)
    tok = pl.BlockSpec((None, None, 1, per * tt), lambda ei, b, j, cs_ref: (b, ei, 0, j))
    once = pl.Buffered(1)
    grid_spec = pltpu.PrefetchScalarGridSpec(
        num_scalar_prefetch=1,
        grid=(e, bsz, nj),
        in_specs=[pl.BlockSpec((None, per * tt, d), lambda ei, b, j, cs_ref: (b, j, 0)), tok, tok,
                  pl.BlockSpec((None, None, d, ff), lambda ei, b, j, cs_ref: (layer, ei, 0, 0), pipeline_mode=once),
                  pl.BlockSpec((None, None, d, ff), lambda ei, b, j, cs_ref: (layer, ei, 0, 0), pipeline_mode=once),
                  pl.BlockSpec((None, None, ff, d), lambda ei, b, j, cs_ref: (layer, ei, 0, 0), pipeline_mode=once)],
        out_specs=pl.BlockSpec((None, None, cap, d), lambda ei, b, j, cs_ref: (b, ei, 0, 0)),
        scratch_shapes=[pltpu.VMEM((cap, d), jnp.float32), pltpu.VMEM((cap, LANES), jnp.float32),
                        pltpu.VMEM((d, ff), jnp.bfloat16), pltpu.VMEM((d, ff), jnp.bfloat16),
                        pltpu.VMEM((ff, d), jnp.bfloat16)])
    return pl.pallas_call(
        functools.partial(_expert_kernel, nj=nj, per=per, tt=tt, cap=cap),
        grid_spec=grid_spec,
        out_shape=jax.ShapeDtypeStruct((bsz, e, cap, d), jnp.bfloat16),
        compiler_params=pltpu.CompilerParams(dimension_semantics=("parallel", "arbitrary", "arbitrary"),
                                             vmem_limit_bytes=MOE_EXPERT_VMEM_LIMIT_BYTES),
        name="moe_experts",
    )(cs, hb, smap4, gate4, w_gate, w_up, w_down)


def _combine_kernel(cs_ref, x_ref, smap_ref, ow_ref, y_ref, *, nj, tt):
    b = pl.program_id(0)
    y_ref[...] = x_ref[...]
    contract_first = (((0,), (0,)), ((), ()))
    ne, cap = ow_ref.shape[0:2]
    win = min(2 * MOE_SUB, cap)
    cols = [slice(j * tt, (j + 1) * tt) for j in range(nj)]

    def expert(e, carry):
        base = (b * ne + e) * (nj + 1)
        pos = [smap_ref[e, :, c] for c in cols]
        r0 = [pl.multiple_of(jnp.minimum(cs_ref[base + j] // MOE_SUB * MOE_SUB, cap - win), MOE_SUB)
              for j in range(nj)]
        oh = [_slot_one_hot(pos[j], r0[j], win, tt) for j in range(nj)]
        add = [lax.dot_general(oh[j], ow_ref[e, pl.ds(r0[j], win), :], contract_first,
                               preferred_element_type=jnp.float32) for j in range(nj)]
        for j in range(nj):
            y_ref[cols[j], :] += add[j]

        for j in range(nj):
            def scatter(st, c, j=j):
                s0 = pl.multiple_of(st * MOE_SUB, MOE_SUB)
                y_ref[cols[j], :] += lax.dot_general(_slot_one_hot(pos[j], s0, MOE_SUB, tt),
                                                     ow_ref[e, pl.ds(s0, MOE_SUB), :], contract_first,
                                                     preferred_element_type=jnp.float32)
                return c

            lax.fori_loop((r0[j] + win) // MOE_SUB, (cs_ref[base + j + 1] + MOE_SUB - 1) // MOE_SUB, scatter, 0)
        return carry

    lax.fori_loop(0, ne, expert, 0)


def moe_combine(x, smap, outw, cs, *, tt=MOE_TT):
    bsz, l, d = x.shape
    e, cap = outw.shape[1:3]
    nj = l // tt
    smap4 = smap.reshape(bsz, e, 1, l)
    grid_spec = pltpu.PrefetchScalarGridSpec(
        num_scalar_prefetch=1,
        grid=(bsz, d // MOE_SLAB),
        in_specs=[pl.BlockSpec((None, l, MOE_SLAB), lambda b, s, cs_ref: (b, 0, s), pipeline_mode=pl.Buffered(1)),
                  pl.BlockSpec((None, e, 1, l), lambda b, s, cs_ref: (b, 0, 0, 0)),
                  pl.BlockSpec((None, e, cap, MOE_SLAB), lambda b, s, cs_ref: (b, 0, 0, s))],
        out_specs=pl.BlockSpec((None, l, MOE_SLAB), lambda b, s, cs_ref: (b, 0, s)))
    return pl.pallas_call(
        functools.partial(_combine_kernel, nj=nj, tt=tt),
        grid_spec=grid_spec,
        out_shape=jax.ShapeDtypeStruct((bsz, l, d), jnp.float32),
        compiler_params=pltpu.CompilerParams(dimension_semantics=("parallel", "parallel"),
                                             vmem_limit_bytes=MOE_VMEM_LIMIT_BYTES),
        name="moe_combine",
    )(cs, x, smap4, outw)


def ec_moe_layer(x, gain, w_router, w_gate, w_up, w_down, layer):
    bsz, l, d = x.shape
    cap = EC_CAPACITY_FACTOR * l // N_EXPERTS
    tt = min(MOE_TT, l)
    nj = l // tt
    hb, aff = moe_route(x, gain, w_router, tm=tt)
    smap, gate, cnt = moe_select(aff, cap, tt=tt)
    cs = jnp.concatenate([jnp.zeros((bsz, N_EXPERTS, 1), jnp.float32), jnp.cumsum(cnt[..., :nj], axis=-1)], axis=-1)
    cs = cs.astype(jnp.int32).reshape(-1)
    outw = moe_experts(hb, smap, gate, cs, w_gate, w_up, w_down, layer, cap, tt=tt)
    return moe_combine(x, smap, outw, cs, tt=tt)


def kernel(x, mix_norm, ffn_norm, ev_w_in, ev_w_out, a_lb_logits, a_out_norm, s5_lambda_re, s5_lambda_im, s5_log_step, s5_b_re, s5_b_im, s5_c_re, s5_c_im, s5_d, s5_glu_w, s5_glu_b, od_w_in, od_w_out, c_q_norm, c_k_norm, c_lambda, c_out_norm, rel_bias, d_conv_w, d_a_log, d_dt_bias, d_out_norm, moe_router, moe_w_gate, moe_w_up, moe_w_down):
    bsz, l, d = x.shape
    p = jax.nn.softmax(a_lb_logits.astype(jnp.float32), axis=0)
    cum = jnp.cumsum(p, axis=0)
    lower_bounds = cum - cum[0:1]
    bias5 = rel_bias_tiles(rel_bias, ATT_T)
    for layer in range(DEPTH):
        j = layer // 2
        if layer % 2 == 0:
            proj, u_tb = norm_matmul(x, mix_norm[layer], ev_w_in[j], tail=B_WIDTH)
            o_a2 = hgrn2_scan(proj, lower_bounds[j])
            o_b = s5_mixer_tb(u_tb.reshape(l * bsz, B_WIDTH), bsz, s5_lambda_re[j], s5_lambda_im[j], s5_log_step[j],
                              s5_b_re[j], s5_b_im[j], s5_c_re[j], s5_c_im[j], s5_d[j], s5_glu_w[j], s5_glu_b[j])
            x = mixer_out_proj(o_a2, proj, 4 * A_HEADS, a_out_norm[j], o_b.reshape(l, bsz * B_WIDTH), ev_w_out[j], x,
                               bidir_first=True, other_time_major=True)
        else:
            o2 = 3 * C_WIDTH + 3 * D_WIDTH
            o4 = o2 + 4 * D_HEADS
            w = od_w_in[j]
            w_in = jnp.concatenate([w[:, :o2], w[:, o4:], w[:, o2:o4],
                                    jnp.zeros((d, 3 * C_WIDTH + OD_COLS - w.shape[1]), w.dtype)], axis=1)
            proj, q2, kt, vb, stats = odd_in_proj(x, mix_norm[layer], w_in, c_q_norm[j], c_k_norm[j])
            o_c = diff_attention(q2, kt, vb, stats, c_lambda[j], c_out_norm[j], bias5, layer)
            o_d2 = gated_deltanet(proj, d_conv_w[j], d_a_log[j], d_dt_bias[j])
            x = mixer_out_proj(o_d2, proj, OD_GATE_BLOCK, d_out_norm[j], o_c, od_w_out[j], x, bidir_first=False)
        x = ec_moe_layer(x, ffn_norm[layer], moe_router[layer], moe_w_gate, moe_w_up, moe_w_down, layer)
    return x
```

```python
import functools
import math

import jax
import jax.numpy as jnp
from jax import lax
from jax.experimental import pallas as pl
from jax.experimental.pallas import tpu as pltpu

D_MODEL = 1024
DEPTH = 4
MIX_WIDTH = D_MODEL
A_WIDTH = MIX_WIDTH // 2
A_HEAD_DIM = 128
A_HEADS = A_WIDTH // A_HEAD_DIM
B_WIDTH = MIX_WIDTH - A_WIDTH
S5_GROUP = 16
S5_GROUPS = B_WIDTH // S5_GROUP
S5_STATE = 64
C_WIDTH = MIX_WIDTH // 2
C_HEAD_DIM = 64
C_HEADS = C_WIDTH // (2 * C_HEAD_DIM)
C_V_DIM = 2 * C_HEAD_DIM
D_WIDTH = MIX_WIDTH - C_WIDTH
D_HEAD_DIM = 128
D_HEADS = D_WIDTH // D_HEAD_DIM
CONV_WIDTH = 5
N_EXPERTS = 16
EXPERT_FF = 2 * D_MODEL
EC_CAPACITY_FACTOR = 2
REL_BUCKETS = 32
REL_MAX_DIST = 128
CHUNK = 64
Q_BLOCK = 128
EPS = 1e-6

VMEM_LIMIT_BYTES = 48 * 1024 * 1024


PROJ_TM = 512
PROJ_COLS = 512


def _norm_matmul_kernel(x_ref, g_ref, w_ref, o_ref, *tail_ref, main):
    x = x_ref[...]
    y = (x * lax.rsqrt(jnp.mean(x * x, axis=-1, keepdims=True) + EPS) * g_ref[...]).astype(jnp.bfloat16)
    for c0 in range(0, main, PROJ_COLS):
        c1 = min(c0 + PROJ_COLS, main)
        o_ref[:, c0:c1] = jnp.dot(y, w_ref[:, c0:c1], preferred_element_type=jnp.float32)
    if tail_ref:
        tail_ref[0][...] = jnp.dot(y, w_ref[:, main:], preferred_element_type=jnp.float32)


def norm_matmul(x, gain, w, *, tail=0, tm=PROJ_TM):
    bsz, l, k = x.shape
    m = w.shape[1]
    main = m - tail
    tm = min(tm, l)
    out_shape = [jax.ShapeDtypeStruct((bsz, l, main), jnp.float32)]
    out_specs = [pl.BlockSpec((None, tm, main), lambda b, i: (b, i, 0))]
    if tail:
        out_shape.append(jax.ShapeDtypeStruct((l, bsz * tail), jnp.float32))
        out_specs.append(pl.BlockSpec((tm, tail), lambda b, i: (i, b)))
    outs = pl.pallas_call(
        functools.partial(_norm_matmul_kernel, main=main),
        grid=(bsz, l // tm),
        in_specs=[pl.BlockSpec((None, tm, k), lambda b, i: (b, i, 0)),
                  pl.BlockSpec((1, k), lambda b, i: (0, 0)),
                  pl.BlockSpec((k, m), lambda b, i: (0, 0), pipeline_mode=pl.Buffered(1))],
        out_specs=out_specs,
        out_shape=out_shape,
        compiler_params=pltpu.CompilerParams(dimension_semantics=("parallel", "parallel"),
                                             vmem_limit_bytes=VMEM_LIMIT_BYTES),
        name="norm_matmul",
    )(x, gain.reshape(1, k).astype(jnp.float32), w.astype(jnp.bfloat16))
    return outs if tail else outs[0]


def _mixer_out_kernel(of_ref, ob_ref, g_ref, gain_ref, other_ref, wb_ref, wo_ref, r_ref, o_ref):
    bf16 = jnp.bfloat16
    o = of_ref[...] + ob_ref[...]
    g = g_ref[...]
    gate = g * jax.nn.sigmoid(g)
    hd = gain_ref.shape[1]
    heads = []
    for h in range(o.shape[1] // hd):
        oh = o[:, h * hd:(h + 1) * hd]
        heads.append(oh * lax.rsqrt(jnp.mean(oh * oh, axis=-1, keepdims=True) + EPS) * gain_ref[...])
    y = (jnp.concatenate(heads, axis=1) * gate).astype(bf16)
    o_ref[...] = (r_ref[...] + jnp.dot(y, wb_ref[...], preferred_element_type=jnp.float32)
                  + jnp.dot(other_ref[...].astype(bf16), wo_ref[...], preferred_element_type=jnp.float32))


def mixer_out_proj(o2, proj3, gate_block, out_gain, other, w, res, *, bidir_first, other_time_major=False,
                   tm=PROJ_TM):
    _, bsz, l, k = o2.shape
    m = w.shape[1]
    tm = min(tm, l)
    wb = w.astype(jnp.bfloat16)
    w_bidir, w_other = (wb[:k], wb[k:]) if bidir_first else (wb[k:], wb[:k])
    other_spec = (pl.BlockSpec((tm, k), lambda b, i: (i, b)) if other_time_major
                  else pl.BlockSpec((None, tm, k), lambda b, i: (b, i, 0)))
    gb = gate_block * LANES // k
    row = pl.BlockSpec((None, tm, m), lambda b, i: (b, i, 0))
    wspec = pl.BlockSpec((k, m), lambda b, i: (0, 0))
    return pl.pallas_call(
        _mixer_out_kernel,
        grid=(bsz, l // tm),
        in_specs=[pl.BlockSpec((None, None, tm, k), lambda b, i: (0, b, i, 0)),
                  pl.BlockSpec((None, None, tm, k), lambda b, i: (1, b, i, 0)),
                  pl.BlockSpec((None, tm, k), lambda b, i: (b, i, gb)),
                  pl.BlockSpec((1, LANES), lambda b, i: (0, 0)),
                  other_spec, wspec, wspec, row],
        out_specs=row,
        out_shape=jax.ShapeDtypeStruct((bsz, l, m), jnp.float32),
        compiler_params=pltpu.CompilerParams(dimension_semantics=("parallel", "parallel"),
                                             vmem_limit_bytes=VMEM_LIMIT_BYTES),
        name="mixer_out_proj",
    )(o2, o2, proj3, out_gain.reshape(1, LANES).astype(jnp.float32), other, w_bidir, w_other, res)


HG_LEVELS = tuple(CHUNK >> (i + 1) for i in range(CHUNK.bit_length() - 1))
HG_TOT_ROWS = 8
HG_TT = 1024


def hgrn2_constants():
    import numpy as np
    c = CHUNK
    r = np.arange(c)[:, None]
    u = np.arange(c)[None, :]
    stacks, masks = [], []
    for direction in range(2):
        fwd = direction == 0
        lvl_masks = []
        for m in HG_LEVELS:
            blk = r // (2 * m)
            later = (r % (2 * m)) >= m
            lvl_masks.append((blk == blk.T) & (later & ~later.T if fwd else ~later & later.T))
        stacks.append(np.concatenate([(u <= r) if fwd else (u >= r), np.ones((HG_TOT_ROWS, c), bool)], axis=0))
        masks.append(np.stack(lvl_masks))
    return (jnp.asarray(np.stack(stacks), jnp.bfloat16), jnp.asarray(np.stack(masks), jnp.float32))


def _hgrn2_kernel(q_ref, f_ref, v_ref, loglb_ref, log1mlb_ref, onemlb_ref, ast_ref, mask_ref, o_ref,
                  st_sc, qd_sc, dec_sc, upd_sc, sin_sc, *, nc):
    direction = pl.program_id(2)

    @pl.when(pl.program_id(3) == 0)
    def _():
        st_sc[...] = jnp.zeros_like(st_sc)

    bf16 = jnp.bfloat16
    f32 = jnp.float32
    c = CHUNK
    hd = A_HEAD_DIM
    dirf = direction.astype(f32)
    loglb = loglb_ref[...]
    log1mlb = log1mlb_ref[...]
    onemlb = onemlb_ref[...]
    ast = ast_ref[...]
    contract_last = (((1,), (1,)), ((), ()))
    contract_first = (((0,), (0,)), ((), ()))

    ns = range(nc)
    rows = [slice(n * c, (n + 1) * c) for n in ns]
    z = [f_ref[r, :] for r in rows]
    v = [v_ref[r, :] for r in rows]
    qr = [q_ref[r, :] for r in rows]
    q = [x * jax.nn.sigmoid(x) for x in qr]
    e = [jnp.exp(-jnp.abs(x)) for x in z]
    cc = [log1mlb + jnp.minimum(z[n], 0.0) - jnp.log1p(e[n]) for n in ns]
    lf = [jnp.maximum(loglb, x) + jnp.log1p(jnp.exp(-jnp.abs(loglb - x))) for x in cc]
    k = [onemlb * jnp.where(z[n] >= 0, e[n], 1.0) / (1.0 + e[n]) for n in ns]
    hi = [x.astype(bf16) for x in lf]
    lo = [(lf[n] - hi[n].astype(f32)).astype(bf16) for n in ns]
    d = [jnp.dot(ast, hi[n], preferred_element_type=f32) + jnp.dot(ast, lo[n], preferred_element_type=f32)
         for n in ns]
    cum = [x[0:c] for x in d]
    tot = [x[c:c + HG_TOT_ROWS] for x in d]
    ref = [cum[n] - dirf * lf[n] for n in ns]
    attn = [jnp.zeros((c, c), f32) for _ in ns]
    for li, m in enumerate(HG_LEVELS):
        nb = c // (2 * m)
        split = [jnp.broadcast_to(x.reshape(nb, 2 * m, hd)[:, m - 1:m, :], (nb, 2 * m, hd)).reshape(c, hd)
                 for x in ref]
        x = [jnp.exp(-jnp.abs(cum[n] - split[n])) for n in ns]
        s = [lax.dot_general((q[n] * x[n]).astype(bf16), (k[n] * x[n]).astype(bf16), contract_last,
                             preferred_element_type=f32) for n in ns]
        attn = [attn[n] + mask_ref[li] * s[n] for n in ns]
    vb = [x.astype(bf16) for x in v]
    intra = [jnp.dot(attn[n].astype(bf16), vb[n], preferred_element_type=f32) for n in ns]
    upd = [lax.dot_general(vb[n], (k[n] * jnp.exp(tot[n][0:1] - cum[n])).astype(bf16), contract_first,
                           preferred_element_type=f32) for n in ns]
    for n in ns:
        o_ref[rows[n], :] = intra[n] + jnp.sum(q[n] * k[n], axis=-1, keepdims=True) * v[n]
        qd_sc[n] = (q[n] * jnp.exp(cum[n])).astype(bf16)
        dec_sc[n] = jnp.exp(tot[n])
        upd_sc[n] = upd[n]

    def body(ci, st):
        ce = ci + direction * (nc - 1 - 2 * ci)
        sin_sc[ce] = st.astype(bf16)
        return st * dec_sc[ce][0:1] + upd_sc[ce]

    st_sc[...] = lax.fori_loop(0, nc, body, st_sc[...])

    for n in range(nc):
        rows = slice(n * c, (n + 1) * c)
        o_ref[rows, :] += lax.dot_general(qd_sc[n], sin_sc[n], contract_last, preferred_element_type=f32)


def hgrn2_scan(proj3, lb):
    bsz, l, _ = proj3.shape
    hd = A_HEAD_DIM
    tt = min(HG_TT, l)
    nt = l // tt
    assert l % tt == 0 and tt % CHUNK == 0
    ast, masks = hgrn2_constants()
    lb = lb.astype(jnp.float32)
    vecs = [jnp.log(lb).reshape(2, 1, A_WIDTH), jnp.log1p(-lb).reshape(2, 1, A_WIDTH), (1.0 - lb).reshape(2, 1, A_WIDTH)]
    tidx = lambda d, i: i + d * (nt - 1 - 2 * i)
    vec = pl.BlockSpec((None, 1, hd), lambda b, h, d, i: (d, 0, h))
    return pl.pallas_call(
        functools.partial(_hgrn2_kernel, nc=tt // CHUNK),
        grid=(bsz, A_HEADS, 2, nt),
        in_specs=[pl.BlockSpec((None, tt, hd), lambda b, h, d, i: (b, tidx(d, i), h)),
                  pl.BlockSpec((None, tt, hd), lambda b, h, d, i: (b, tidx(d, i), (1 + d) * A_HEADS + h)),
                  pl.BlockSpec((None, tt, hd), lambda b, h, d, i: (b, tidx(d, i), 3 * A_HEADS + h)),
                  vec, vec, vec,
                  pl.BlockSpec((None,) + ast.shape[1:], lambda b, h, d, i: (d, 0, 0)),
                  pl.BlockSpec((None,) + masks.shape[1:], lambda b, h, d, i: (d, 0, 0, 0))],
        out_specs=pl.BlockSpec((None, None, tt, hd), lambda b, h, d, i: (d, b, tidx(d, i), h)),
        out_shape=jax.ShapeDtypeStruct((2, bsz, l, A_WIDTH), jnp.float32),
        scratch_shapes=[pltpu.VMEM((hd, hd), jnp.float32),
                        pltpu.VMEM((tt // CHUNK, CHUNK, hd), jnp.bfloat16),
                        pltpu.VMEM((tt // CHUNK, HG_TOT_ROWS, hd), jnp.float32),
                        pltpu.VMEM((tt // CHUNK, hd, hd), jnp.float32),
                        pltpu.VMEM((tt // CHUNK, hd, hd), jnp.bfloat16)],
        compiler_params=pltpu.CompilerParams(dimension_semantics=("parallel", "parallel", "parallel", "arbitrary"),
                                             vmem_limit_bytes=VMEM_LIMIT_BYTES),
        name="hgrn2_scan",
    )(proj3, proj3, proj3, *vecs, ast, masks)


S5_NS = S5_GROUPS * S5_STATE
S5_TT = 128
SUBLANES = 8


def _s5_scan_kernel(u_ref, win_ref, ar_ref, ai_ref, wout_ref, y_ref, bu_sc, xs_sc, st_sc, *, bsz, tt, reverse):
    @pl.when(pl.program_id(0) == 0)
    def _():
        st_sc[...] = jnp.zeros_like(st_sc)

    ub = u_ref[...].astype(jnp.bfloat16)
    halves = 2
    uw = B_WIDTH // halves
    sw = S5_NS // halves
    for hf in range(halves):
        for part in range(2):
            sc = slice(part * S5_NS + hf * sw, part * S5_NS + (hf + 1) * sw)
            bu_sc[:, sc] = jnp.dot(ub[:, hf * uw:(hf + 1) * uw], win_ref[hf * uw:(hf + 1) * uw, sc],
                                   preferred_element_type=jnp.float32)
    ar = jnp.broadcast_to(ar_ref[...], (bsz, S5_NS))
    ai = jnp.broadcast_to(ai_ref[...], (bsz, S5_NS))
    per = SUBLANES // bsz
    ngroups = tt // per

    def body(s, carry):
        xr, xi = carry
        p = (ngroups - 1 - s) if reverse else s
        base = pl.multiple_of(p * SUBLANES, SUBLANES)
        blk = bu_sc[pl.ds(base, SUBLANES), :]
        outs_r = [None] * per
        outs_i = [None] * per
        for ph in (range(per - 1, -1, -1) if reverse else range(per)):
            br = blk[ph * bsz:(ph + 1) * bsz, :S5_NS]
            bi = blk[ph * bsz:(ph + 1) * bsz, S5_NS:]
            xr, xi = ar * xr - ai * xi + br, ar * xi + ai * xr + bi
            outs_r[ph] = xr
            outs_i[ph] = xi
        xs_sc[pl.ds(base, SUBLANES), :S5_NS] = jnp.concatenate(outs_r, axis=0)
        xs_sc[pl.ds(base, SUBLANES), S5_NS:] = jnp.concatenate(outs_i, axis=0)
        return xr, xi

    xr, xi = lax.fori_loop(0, ngroups, body, (st_sc[0], st_sc[1]))
    st_sc[0] = xr
    st_sc[1] = xi
    for hf in range(halves):
        yc = slice(hf * uw, (hf + 1) * uw)
        acc = None
        for part in range(2):
            sc = slice(part * S5_NS + hf * sw, part * S5_NS + (hf + 1) * sw)
            term = jnp.dot(xs_sc[:, sc].astype(jnp.bfloat16), wout_ref[sc, yc], preferred_element_type=jnp.float32)
            acc = term if acc is None else acc + term
        y_ref[:, yc] = acc


def s5_scan(u_tb, win, ar, ai, wout, *, bsz, reverse):
    n = u_tb.shape[0]
    rows = S5_TT * bsz
    nt = n // rows
    assert n % rows == 0 and SUBLANES % bsz == 0
    idx = (lambda i: (nt - 1 - i, 0)) if reverse else (lambda i: (i, 0))
    const = lambda i: (0, 0)
    return pl.pallas_call(
        functools.partial(_s5_scan_kernel, bsz=bsz, tt=S5_TT, reverse=reverse),
        grid=(nt,),
        in_specs=[pl.BlockSpec((rows, B_WIDTH), idx),
                  pl.BlockSpec((B_WIDTH, 2 * S5_NS), const),
                  pl.BlockSpec((1, S5_NS), const),
                  pl.BlockSpec((1, S5_NS), const),
                  pl.BlockSpec((2 * S5_NS, B_WIDTH), const)],
        out_specs=pl.BlockSpec((rows, B_WIDTH), idx),
        out_shape=jax.ShapeDtypeStruct((n, B_WIDTH), jnp.float32),
        scratch_shapes=[pltpu.VMEM((rows, 2 * S5_NS), jnp.float32),
                        pltpu.VMEM((rows, 2 * S5_NS), jnp.float32),
                        pltpu.VMEM((2, bsz, S5_NS), jnp.float32)],
        compiler_params=pltpu.CompilerParams(dimension_semantics=("arbitrary",),
                                             vmem_limit_bytes=VMEM_LIMIT_BYTES),
        name="s5_scan_bwd" if reverse else "s5_scan_fwd",
    )(u_tb, win, ar, ai, wout)


def _s5_final_kernel(u_ref, yf_ref, yb_ref, d_ref, w_ref, b_ref, o_ref):
    y = d_ref[...] * u_ref[...] + yf_ref[...] + yb_ref[...]
    y = jax.nn.gelu(y)
    z = jnp.dot(y.astype(jnp.bfloat16), w_ref[...], preferred_element_type=jnp.float32) + b_ref[...]
    o_ref[...] = y * jax.nn.sigmoid(z)


def s5_finalize(u, yf, yb, d_skip, glu_w, glu_b, *, tm=512):
    n, w = u.shape
    row = pl.BlockSpec((tm, w), lambda i: (i, 0))
    vec = pl.BlockSpec((1, w), lambda i: (0, 0))
    return pl.pallas_call(
        _s5_final_kernel,
        grid=(n // tm,),
        in_specs=[row, row, row, vec, pl.BlockSpec((w, w), lambda i: (0, 0)), vec],
        out_specs=row,
        out_shape=jax.ShapeDtypeStruct((n, w), jnp.float32),
        compiler_params=pltpu.CompilerParams(dimension_semantics=("parallel",),
                                             vmem_limit_bytes=VMEM_LIMIT_BYTES),
        name="s5_finalize",
    )(u, yf, yb, d_skip.reshape(1, w).astype(jnp.float32), glu_w.astype(jnp.bfloat16),
      glu_b.reshape(1, w).astype(jnp.float32))


def s5_direction_params(lam_re, lam_im, log_step, b_re, b_im, c_re, c_im):
    step = jnp.exp(log_step)[:, None]
    mag = jnp.exp(lam_re * step)
    abar_re = mag * jnp.cos(lam_im * step)
    abar_im = mag * jnp.sin(lam_im * step)
    den = lam_re * lam_re + lam_im * lam_im
    fr = ((abar_re - 1.0) * lam_re + abar_im * lam_im) / den
    fi = (abar_im * lam_re - (abar_re - 1.0) * lam_im) / den
    bb_re = fr[..., None] * b_re - fi[..., None] * b_im
    bb_im = fr[..., None] * b_im + fi[..., None] * b_re
    eye = jnp.eye(S5_GROUPS, dtype=jnp.float32)
    win = jnp.concatenate([jnp.einsum('gnp,gh->gphn', bb, eye).reshape(B_WIDTH, S5_NS) for bb in (bb_re, bb_im)],
                          axis=1)
    wout = jnp.concatenate([jnp.einsum('gpn,gh->hngp', c, eye).reshape(S5_NS, B_WIDTH) for c in (c_re, -c_im)],
                           axis=0)
    return (win.astype(jnp.bfloat16), abar_re.reshape(1, S5_NS), abar_im.reshape(1, S5_NS),
            wout.astype(jnp.bfloat16))


def s5_mixer_tb(u_tb, bsz, lam_re, lam_im, log_step, b_re, b_im, c_re, c_im, d_skip, glu_w, glu_b):
    f32 = jnp.float32
    ys = []
    for direction in range(2):
        prm = s5_direction_params(lam_re[direction].astype(f32), lam_im[direction].astype(f32),
                                  log_step[direction].astype(f32), b_re[direction].astype(f32),
                                  b_im[direction].astype(f32), c_re[direction].astype(f32),
                                  c_im[direction].astype(f32))
        ys.append(s5_scan(u_tb, *prm, bsz=bsz, reverse=(direction == 1)))
    return s5_finalize(u_tb, ys[0], ys[1], d_skip, glu_w, glu_b)


def t5_bucket(rel):
    half = REL_BUCKETS // 2
    max_exact = half // 2
    base = jnp.where(rel > 0, half, 0)
    n = jnp.abs(rel)
    nf = jnp.maximum(n, 1).astype(jnp.float32)
    large = max_exact + (jnp.log(nf / max_exact) / math.log(REL_MAX_DIST / max_exact)
                         * (half - max_exact)).astype(jnp.int32)
    large = jnp.minimum(large, half - 1)
    return base + jnp.where(n < max_exact, n, large)


ATT_T = 512
LOG2E = math.log2(math.e)


def rel_bias_tiles(rel_bias, t):
    assert t >= REL_MAX_DIST
    table = rel_bias.astype(jnp.float32) * LOG2E
    tiles = []
    for d in (-1, 0, 1):
        c = table[t5_bucket(d * t + jnp.arange(-(t - 1), t))]
        w = jnp.concatenate([c, c[:1]], axis=0)
        m = jnp.tile(w, (t, 1))[:t * (2 * t - 1)].reshape(t, 2 * t - 1, -1)
        tiles.append(m[:, t - 1:2 * t - 1])
    far_neg = jnp.broadcast_to(table[t5_bucket(jnp.array(-2 * t))], tiles[0].shape)
    far_pos = jnp.broadcast_to(table[t5_bucket(jnp.array(2 * t))], tiles[0].shape)
    out = jnp.stack([far_neg] + tiles + [far_pos], axis=0)
    return jnp.transpose(out, (3, 0, 1, 2))


def _attn_operands(q, k, v, qg, kg):
    f32 = jnp.float32
    bf16 = jnp.bfloat16
    lane = lax.broadcasted_iota(jnp.int32, q.shape, 1)
    lo = lane < C_HEAD_DIM

    def half_sums(sq):
        return (jnp.sum(jnp.where(lo, sq, 0.0), axis=-1, keepdims=True),
                jnp.sum(jnp.where(lo, 0.0, sq), axis=-1, keepdims=True))

    def halfnorm(x, g):
        s_lo, s_hi = half_sums(x * x)
        return x * lax.rsqrt(jnp.where(lo, s_lo, s_hi) * (1.0 / C_HEAD_DIM) + EPS) * g

    def max_sq_norms(xb):
        n_lo, n_hi = half_sums(xb.astype(f32) * xb.astype(f32))
        return jnp.max(n_lo, axis=0, keepdims=True), jnp.max(n_hi, axis=0, keepdims=True)

    qn = halfnorm(q, qg) * (C_HEAD_DIM ** -0.5 * LOG2E)
    kn = halfnorm(k, kg)
    qb = qn.astype(bf16)
    kb = kn.astype(bf16)
    q2 = (jnp.where(lo, qb, 0.0).astype(bf16), jnp.where(lo, 0.0, qb).astype(bf16))
    q_lo, q_hi = max_sq_norms(qb)
    k_lo, k_hi = max_sq_norms(kb)
    sub = lax.broadcasted_iota(jnp.int32, (SUBLANES, q.shape[1]), 0)
    stats = jnp.where(sub == 0, q_lo, jnp.where(sub == 1, q_hi, jnp.where(sub == 2, k_lo,
                      jnp.where(sub == 3, k_hi, 0.0))))
    return q2, kn.T.astype(bf16), v.astype(bf16), stats


def _odd_proj_kernel(x_ref, g_ref, w_ref, qg_ref, kg_ref, o_ref, q2_ref, kt_ref, vb_ref, st_ref):
    x = x_ref[...]
    y = (x * lax.rsqrt(jnp.mean(x * x, axis=-1, keepdims=True) + EPS) * g_ref[...]).astype(jnp.bfloat16)
    hw = 2 * C_HEAD_DIM
    q, k, v = (jnp.dot(y, w_ref[:, p * C_WIDTH:(p + 1) * C_WIDTH], preferred_element_type=jnp.float32)
               for p in range(3))
    for h in range(C_HEADS):
        cols = slice(h * hw, (h + 1) * hw)
        q2, kt, vb, stats = _attn_operands(q[:, cols], k[:, cols], v[:, cols], qg_ref[...], kg_ref[...])
        q2_ref[h, 0] = q2[0]
        q2_ref[h, 1] = q2[1]
        kt_ref[h] = kt
        vb_ref[h] = vb
        st_ref[h] = stats
    att = 3 * C_WIDTH
    for c0 in range(att, w_ref.shape[1], PROJ_COLS):
        c1 = min(c0 + PROJ_COLS, w_ref.shape[1])
        o_ref[:, c0 - att:c1 - att] = jnp.dot(y, w_ref[:, c0:c1], preferred_element_type=jnp.float32)


def odd_in_proj(x, gain, w, q_gain, k_gain, *, tm=PROJ_TM):
    bsz, l, kdim = x.shape
    m = w.shape[1]
    rest = m - 3 * C_WIDTH
    tm = min(tm, l)
    hw = 2 * C_HEAD_DIM
    gq = jnp.tile(q_gain.astype(jnp.float32), 2).reshape(1, hw)
    gk = jnp.tile(k_gain.astype(jnp.float32), 2).reshape(1, hw)
    vec = pl.BlockSpec((1, hw), lambda b, i: (0, 0))
    return pl.pallas_call(
        _odd_proj_kernel,
        grid=(bsz, l // tm),
        in_specs=[pl.BlockSpec((None, tm, kdim), lambda b, i: (b, i, 0)),
                  pl.BlockSpec((1, kdim), lambda b, i: (0, 0)),
                  pl.BlockSpec((kdim, m), lambda b, i: (0, 0), pipeline_mode=pl.Buffered(1)),
                  vec, vec],
        out_specs=[pl.BlockSpec((None, tm, rest), lambda b, i: (b, i, 0)),
                   pl.BlockSpec((None, C_HEADS, 2, tm, hw), lambda b, i: (b, 0, 0, i, 0)),
                   pl.BlockSpec((None, C_HEADS, hw, tm), lambda b, i: (b, 0, 0, i)),
                   pl.BlockSpec((None, C_HEADS, tm, hw), lambda b, i: (b, 0, i, 0)),
                   pl.BlockSpec((None, C_HEADS, None, SUBLANES, hw), lambda b, i: (b, 0, i, 0, 0))],
        out_shape=[jax.ShapeDtypeStruct((bsz, l, rest), jnp.float32),
                   jax.ShapeDtypeStruct((bsz, C_HEADS, 2, l, hw), jnp.bfloat16),
                   jax.ShapeDtypeStruct((bsz, C_HEADS, hw, l), jnp.bfloat16),
                   jax.ShapeDtypeStruct((bsz, C_HEADS, l, hw), jnp.bfloat16),
                   jax.ShapeDtypeStruct((bsz, C_HEADS, l // tm, SUBLANES, hw), jnp.float32)],
        compiler_params=pltpu.CompilerParams(dimension_semantics=("parallel", "parallel"),
                                             vmem_limit_bytes=VMEM_LIMIT_BYTES),
        name="odd_in_proj",
    )(x, gain.reshape(1, kdim).astype(jnp.float32), w.astype(jnp.bfloat16), gq, gk)


ATT_ROWS = 64
ATT_KEY_TILES = 4
ATT_SAFE_GAP = 100.0


def _attn_kernel(lam_ref, kmax_ref, bmax_ref, q2_ref, kt_ref, v_ref, bias_ref, g_ref, o_ref,
                 m_sc, l_sc, acc_sc, s_sc, p_sc, a_sc, *, t, nk, ktiles, out_scale, bounded):
    f32 = jnp.float32
    b = pl.program_id(0)
    h = pl.program_id(1)
    qi = pl.program_id(2)
    q2 = q2_ref[...].reshape(2 * t, 2 * C_HEAD_DIM)
    r = ATT_ROWS
    hw = 2 * C_HEAD_DIM
    if bounded:
        q2f = q2.astype(f32)
        nq = jnp.sqrt(jnp.sum(q2f * q2f, axis=-1, keepdims=True))
        row = lax.broadcasted_iota(jnp.int32, nq.shape, 0)
        kc = jnp.where(row < t, kmax_ref[(b * C_HEADS + h) * 2], kmax_ref[(b * C_HEADS + h) * 2 + 1])
        m_sc[...] = jnp.broadcast_to(nq * kc + bmax_ref[h], m_sc.shape)
    else:
        m_sc[...] = jnp.full(m_sc.shape, -jnp.inf, f32)
    l_sc[...] = jnp.zeros_like(l_sc)
    acc_sc[...] = jnp.zeros_like(acc_sc)

    kw = ktiles * t

    def body(ki, carry):
        off = pl.multiple_of(ki * kw, kw)
        bidx = [jnp.clip(ki * ktiles + c - qi, -2, 2) + 2 for c in range(ktiles)]
        s_sc[...] = jnp.dot(q2, kt_ref[:, pl.ds(off, kw)], preferred_element_type=f32)
        for g in range(2 * t // r):
            rows = slice(g * r, (g + 1) * r)
            brow = (g * r) % t
            m = m_sc[rows, :]
            s = [s_sc[rows, c * t:(c + 1) * t] + bias_ref[bidx[c], brow:brow + r, :] for c in range(ktiles)]
            if not bounded:
                m_prev = m
                smax = functools.reduce(jnp.maximum, [jnp.max(x, axis=-1, keepdims=True) for x in s])
                m = jnp.maximum(m_prev, smax)
                alpha = jnp.exp2(m_prev - m)
                m_sc[rows, :] = m
                a_sc[rows, :] = alpha
            ps = [jnp.exp2(x[:, j * hw:(j + 1) * hw] - m) for x in s for j in range(t // hw)]
            psum = jnp.sum(sum(ps), axis=-1, keepdims=True)
            l_sc[rows, :] = (l_sc[rows, :] if bounded else alpha * l_sc[rows, :]) + psum
            for j in range(kw // hw):
                p_sc[rows, j * hw:(j + 1) * hw] = ps[j].astype(jnp.bfloat16)
        pv = jnp.dot(p_sc[...], v_ref[pl.ds(off, kw), :], preferred_element_type=f32)
        acc_sc[...] = (acc_sc[...] if bounded else a_sc[...] * acc_sc[...]) + pv
        return carry

    lax.fori_loop(0, nk // ktiles, body, 0)
    a = acc_sc[...] / l_sc[...]
    o = a[:t] - lam_ref[0] * a[t:]
    y = o * lax.rsqrt(jnp.mean(o * o, axis=-1, keepdims=True) + EPS)
    o_ref[...] = y * g_ref[...] * out_scale


def diff_attention(q2, kt, vb, stats, lam, out_gain, bias5, layer_idx):
    f32 = jnp.float32
    bsz, _, l, _ = vb.shape
    t = ATT_T
    hw = 2 * C_HEAD_DIM
    lam_init = 0.8 - 0.6 * math.exp(-0.3 * layer_idx)
    lam_f = lam.astype(f32)
    lam_full = jnp.exp(jnp.sum(lam_f[0] * lam_f[1])) - jnp.exp(jnp.sum(lam_f[2] * lam_f[3])) + lam_init
    norms = jnp.sqrt(jnp.max(stats[..., 0:4, 0], axis=2)) * (1.0 + 1e-3)
    qmax, kmax = norms[..., 0:2], norms[..., 2:4]
    bmax = jnp.max(bias5, axis=(1, 2, 3))
    bmin = jnp.min(bias5, axis=(1, 2, 3))
    gap = 2.0 * qmax * kmax + (bmax - bmin)[None, :, None]
    smem = pl.BlockSpec(memory_space=pltpu.SMEM)
    ktiles = math.gcd(ATT_KEY_TILES, l // t)

    def run(bounded):
        return pl.pallas_call(
            functools.partial(_attn_kernel, t=t, nk=l // t, ktiles=ktiles, out_scale=1.0 - lam_init,
                              bounded=bounded),
            grid=(bsz, C_HEADS, l // t),
            in_specs=[smem, smem, smem,
                      pl.BlockSpec((None, None, 2, t, hw), lambda b, h, i: (b, h, 0, i, 0)),
                      pl.BlockSpec((None, None, hw, l), lambda b, h, i: (b, h, 0, 0)),
                      pl.BlockSpec((None, None, l, hw), lambda b, h, i: (b, h, 0, 0)),
                      pl.BlockSpec((None, 5, t, t), lambda b, h, i: (h, 0, 0, 0)),
                      pl.BlockSpec((1, hw), lambda b, h, i: (0, 0))],
            out_specs=pl.BlockSpec((None, t, hw), lambda b, h, i: (b, i, h)),
            out_shape=jax.ShapeDtypeStruct((bsz, l, C_WIDTH), f32),
            scratch_shapes=[pltpu.VMEM((2 * t, hw), f32), pltpu.VMEM((2 * t, hw), f32), pltpu.VMEM((2 * t, hw), f32),
                            pltpu.VMEM((2 * t, ktiles * t), f32), pltpu.VMEM((2 * t, ktiles * t), jnp.bfloat16),
                            pltpu.VMEM((2 * t, hw), f32)],
            compiler_params=pltpu.CompilerParams(dimension_semantics=("parallel", "parallel", "arbitrary"),
                                                 vmem_limit_bytes=VMEM_LIMIT_BYTES),
            name="diff_attention_bounded" if bounded else "diff_attention_online",
        )(lam_full.reshape(1), kmax.reshape(-1), bmax, q2, kt, vb, bias5, out_gain.reshape(1, hw).astype(f32))

    return lax.cond(jnp.all(gap < ATT_SAFE_GAP), lambda: run(True), lambda: run(False))


GDN_TT = 512
GDN_HEADS_PER_STEP = 4
LANES = 128
OD_QKV_BLOCK = 0
OD_GATE_BLOCK = OD_QKV_BLOCK + 3 * D_WIDTH // LANES
OD_AB_BLOCK = OD_GATE_BLOCK + D_WIDTH // LANES
OD_COLS = 2304


def _gdn_prep_kernel(prev_ref, cur_ref, next_ref, w_ref, o_ref, *, tl, nl):
    i = pl.program_id(1)
    part = pl.program_id(2)
    prev = jnp.where(i > 0, prev_ref[...], 0.0)
    nxt = jnp.where(i < nl - 1, next_ref[...], 0.0)
    ext = jnp.concatenate([prev, cur_ref[...], nxt], axis=0)
    halo = prev.shape[0]
    acc = None
    for j in range(CONV_WIDTH):
        start = halo - CONV_WIDTH // 2 + j
        term = w_ref[j:j + 1, :] * ext[start:start + tl, :]
        acc = term if acc is None else acc + term
    y = acc * jax.nn.sigmoid(acc)
    scale = jnp.where(part == 0, D_HEAD_DIM ** -0.5, 1.0)
    heads = []
    for h in range(D_HEADS):
        yh = y[:, h * LANES:(h + 1) * LANES]
        heads.append(yh * (lax.rsqrt(jnp.sum(yh * yh, axis=-1, keepdims=True) + EPS) * scale))
    o_ref[...] = jnp.where(part < 2, jnp.concatenate(heads, axis=1), y)


def gdn_prep(proj3, conv_w, *, tl=1024):
    bsz, l, _ = proj3.shape
    halo = SUBLANES
    nl = l // tl
    blk0 = OD_QKV_BLOCK * LANES // D_WIDTH
    return pl.pallas_call(
        functools.partial(_gdn_prep_kernel, tl=tl, nl=nl),
        grid=(bsz, nl, 3),
        in_specs=[pl.BlockSpec((None, halo, D_WIDTH), lambda b, i, p: (b, jnp.maximum(i * (tl // halo) - 1, 0), blk0 + p)),
                  pl.BlockSpec((None, tl, D_WIDTH), lambda b, i, p: (b, i, blk0 + p)),
                  pl.BlockSpec((None, halo, D_WIDTH),
                               lambda b, i, p: (b, jnp.minimum((i + 1) * (tl // halo), l // halo - 1), blk0 + p)),
                  pl.BlockSpec((CONV_WIDTH, D_WIDTH), lambda b, i, p: (0, p))],
        out_specs=pl.BlockSpec((None, None, tl, D_WIDTH), lambda b, i, p: (p, b, i, 0)),
        out_shape=jax.ShapeDtypeStruct((3, bsz, l, D_WIDTH), jnp.float32),
        compiler_params=pltpu.CompilerParams(dimension_semantics=("parallel", "parallel", "parallel"),
                                             vmem_limit_bytes=VMEM_LIMIT_BYTES),
        name="gdn_prep",
    )(proj3, proj3, proj3, conv_w.astype(jnp.float32))


def _gdn_gates_kernel(x_ref, nega_ref, dtb_ref, o_ref):
    x = x_ref[...]
    z = x + dtb_ref[...]
    g = nega_ref[...] * (jnp.maximum(z, 0.0) + jnp.log1p(jnp.exp(-jnp.abs(z))))
    lane = lax.broadcasted_iota(jnp.int32, x.shape, 1)
    y = jnp.where(lane < 2 * D_HEADS, g, jax.nn.sigmoid(x))
    o_ref[...] = y.T[0:4 * D_HEADS, :]


def gdn_gates(proj3, a_log, dt_bias, *, tl=512):
    bsz, l, _ = proj3.shape
    pad = LANES - 2 * D_HEADS
    nega = jnp.pad(-jnp.exp(a_log.astype(jnp.float32)).reshape(1, -1), ((0, 0), (0, pad)))
    dtb = jnp.pad(dt_bias.astype(jnp.float32).reshape(1, -1), ((0, 0), (0, pad)))
    vec = pl.BlockSpec((1, LANES), lambda b, i: (0, 0))
    return pl.pallas_call(
        _gdn_gates_kernel,
        grid=(bsz, l // tl),
        in_specs=[pl.BlockSpec((None, tl, LANES), lambda b, i: (b, i, OD_AB_BLOCK)), vec, vec],
        out_specs=pl.BlockSpec((None, 4 * D_HEADS, tl), lambda b, i: (b, 0, i)),
        out_shape=jax.ShapeDtypeStruct((bsz, 4 * D_HEADS, l), jnp.float32),
        compiler_params=pltpu.CompilerParams(dimension_semantics=("parallel", "parallel"),
                                             vmem_limit_bytes=VMEM_LIMIT_BYTES),
        name="gdn_gates",
    )(proj3, nega, dtb)


def gdn_constants():
    import numpy as np
    c = CHUNK
    r = np.arange(c)[:, None]
    u = np.arange(c)[None, :]
    cum, incl, strict = [], [], []
    for direction in range(2):
        fwd = direction == 0
        cum.append(np.concatenate([(r <= u) if fwd else (r >= u), np.ones((c, c), bool)], axis=1))
        incl.append((u <= r) if fwd else (u >= r))
        strict.append((u < r) if fwd else (u > r))
    same = lambda b: (r // b) == (u // b)
    merges = [same(2 * b) & ~same(b) for b in (8, 16, 32)]
    f32 = jnp.float32
    return (jnp.asarray(np.stack(cum), jnp.bfloat16), jnp.asarray(np.stack(incl), f32),
            jnp.asarray(np.stack(strict), f32), jnp.asarray(same(8), f32), jnp.asarray(np.stack(merges), f32))


def _gdn_kernel(q_ref, k_ref, v_ref, g_ref, b_ref, cum_ref, incl_ref, strict_ref, d8_ref, mrg_ref, o_ref,
                s_sc, qd_sc, dec_sc, w_sc, u_sc, sin_sc, *, nc):
    direction = pl.program_id(2)

    @pl.when(pl.program_id(3) == 0)
    def _():
        s_sc[...] = jnp.zeros_like(s_sc)

    bf16 = jnp.bfloat16
    f32 = jnp.float32
    c = CHUNK
    hd = D_HEAD_DIM
    contract_last = (((1,), (1,)), ((), ()))
    contract_first = (((0,), (0,)), ((), ()))
    cumm = cum_ref[...]
    incl = incl_ref[...]
    strict = strict_ref[...]
    d8 = d8_ref[...]
    eye = (lax.broadcasted_iota(jnp.int32, (c, c), 0) == lax.broadcasted_iota(jnp.int32, (c, c), 1)).astype(f32)

    def mm(a, b):
        return jnp.dot(a.astype(bf16), b.astype(bf16), preferred_element_type=f32)

    def rep(x):
        return jnp.concatenate([x] * (hd // c), axis=1)

    nh = g_ref.shape[0]
    ns = range(nh * nc)
    head = [m // nc for m in ns]
    rows = [slice((m % nc) * c, (m % nc + 1) * c) for m in ns]
    cols = [slice(h * hd, (h + 1) * hd) for h in head]
    q = [q_ref[rows[m], cols[m]] for m in ns]
    k = [k_ref[rows[m], cols[m]] for m in ns]
    v = [v_ref[rows[m], cols[m]] for m in ns]
    kb = [x.astype(bf16) for x in k]
    kk = [lax.dot_general(x, x, contract_last, preferred_element_type=f32) for x in kb]
    qk = [lax.dot_general(q[n].astype(bf16), kb[n], contract_last, preferred_element_type=f32) for n in ns]
    grow = [jnp.broadcast_to(g_ref[head[m], :, rows[m]], (c, c)) for m in ns]
    ghi = [x.astype(bf16) for x in grow]
    glo = [(grow[n] - ghi[n].astype(f32)).astype(bf16) for n in ns]
    gm = [jnp.dot(ghi[n], cumm, preferred_element_type=f32) + jnp.dot(glo[n], cumm, preferred_element_type=f32)
          for n in ns]
    gam_row = [x[:, :c] for x in gm]
    tot = [x[:, c:] for x in gm]
    gam_col = [x.T for x in gam_row]
    beta_col = [jnp.broadcast_to(b_ref[head[m], :, rows[m]], (c, c)).T for m in ns]
    decay = [incl * jnp.exp(jnp.minimum(gam_col[n] - gam_row[n], 0.0)) for n in ns]
    a = [strict * beta_col[n] * kk[n] * decay[n] for n in ns]
    a0 = [x * d8 for x in a]
    n2 = [mm(x, x) for x in a0]
    n4 = [mm(x, x) for x in n2]
    t = [mm(eye - a0[n], eye + n2[n]) for n in ns]
    t = [mm(t[n], eye + n4[n]) for n in ns]
    for j in range(mrg_ref.shape[0]):
        p = [mm(a[n] * mrg_ref[j], t[n]) for n in ns]
        t = [t[n] - mm(t[n], p[n]) for n in ns]
    beta128 = [rep(x) for x in beta_col]
    egam128 = [rep(jnp.exp(x)) for x in gam_col]
    solb = [mm(t[n], jnp.concatenate([k[n] * beta128[n] * egam128[n], v[n] * beta128[n]], axis=1)).astype(bf16)
            for n in ns]
    av = [jnp.dot((qk[n] * decay[n]).astype(bf16), solb[n], preferred_element_type=f32) for n in ns]
    k_dec = [(k[n] * rep(jnp.exp(tot[n] - gam_col[n]))).astype(bf16) for n in ns]
    wu = [lax.dot_general(k_dec[n], solb[n], contract_first, preferred_element_type=f32) for n in ns]
    for n in ns:
        qd_sc[n] = (q[n] * egam128[n] - av[n][:, :hd]).astype(bf16)
        o_ref[rows[n], cols[n]] = av[n][:, hd:]
        w_sc[n] = wu[n][:, :hd].astype(bf16)
        u_sc[n] = wu[n][:, hd:]
        dec_sc[n] = rep(jnp.exp(tot[n][0:SUBLANES, :]))

    def body(ci, states):
        ce = ci + direction * (nc - 1 - 2 * ci)
        new = []
        for h in range(nh):
            m = h * nc + ce
            sb = states[h].astype(bf16)
            sin_sc[m] = sb
            new.append(states[h] * dec_sc[m][0:1] - jnp.dot(w_sc[m], sb, preferred_element_type=f32) + u_sc[m])
        return tuple(new)

    states = lax.fori_loop(0, nc, body, tuple(s_sc[h] for h in range(nh)))
    for h in range(nh):
        s_sc[h] = states[h]

    for m in ns:
        o_ref[rows[m], cols[m]] += jnp.dot(qd_sc[m], sin_sc[m], preferred_element_type=f32)


def gdn_scan(qkv, gb):
    _, bsz, l, _ = qkv.shape
    hd = D_HEAD_DIM
    tt = min(GDN_TT, l)
    nt = l // tt
    nc = tt // CHUNK
    assert l % tt == 0 and tt % CHUNK == 0
    consts = gdn_constants()
    gb4 = gb.reshape(bsz, 4 * D_HEADS, 1, l)
    nh = GDN_HEADS_PER_STEP
    assert D_HEADS % nh == 0
    tidx = lambda d, i: i + d * (nt - 1 - 2 * i)
    qkv_spec = lambda p: pl.BlockSpec((None, None, tt, nh * hd), lambda b, h, d, i: (p, b, tidx(d, i), h))
    row_spec = lambda off: pl.BlockSpec((None, nh, 1, tt),
                                        lambda b, h, d, i: (b, (off + d * D_HEADS) // nh + h, 0, tidx(d, i)))
    per_dir = lambda a: pl.BlockSpec((None,) + a.shape[1:], lambda b, h, d, i: (d,) + (0,) * (a.ndim - 1))
    whole = lambda a: pl.BlockSpec(a.shape, lambda b, h, d, i: (0,) * a.ndim)
    return pl.pallas_call(
        functools.partial(_gdn_kernel, nc=nc),
        grid=(bsz, D_HEADS // nh, 2, nt),
        in_specs=[qkv_spec(0), qkv_spec(1), qkv_spec(2), row_spec(0), row_spec(2 * D_HEADS),
                  per_dir(consts[0]), per_dir(consts[1]), per_dir(consts[2]), whole(consts[3]), whole(consts[4])],
        out_specs=pl.BlockSpec((None, None, tt, nh * hd), lambda b, h, d, i: (d, b, tidx(d, i), h)),
        out_shape=jax.ShapeDtypeStruct((2, bsz, l, D_WIDTH), jnp.float32),
        scratch_shapes=[pltpu.VMEM((nh, hd, hd), jnp.float32),
                        pltpu.VMEM((nh * nc, CHUNK, hd), jnp.bfloat16),
                        pltpu.VMEM((nh * nc, SUBLANES, hd), jnp.float32),
                        pltpu.VMEM((nh * nc, hd, hd), jnp.bfloat16),
                        pltpu.VMEM((nh * nc, hd, hd), jnp.float32),
                        pltpu.VMEM((nh * nc, hd, hd), jnp.bfloat16)],
        compiler_params=pltpu.CompilerParams(dimension_semantics=("parallel", "parallel", "parallel", "arbitrary"),
                                             vmem_limit_bytes=VMEM_LIMIT_BYTES),
        name="gdn_scan",
    )(qkv, qkv, qkv, gb4, gb4, *consts)


def gated_deltanet(proj3, conv_w, a_log, dt_bias):
    return gdn_scan(gdn_prep(proj3, conv_w), gdn_gates(proj3, a_log, dt_bias))


MOE_TT = 512
MOE_SUB = 128
MOE_ALIGN = 64
MOE_TILES_PER_STEP = 4
MOE_ROWS = 256
MOE_SLAB = 256
MOE_VMEM_LIMIT_BYTES = 56 * 1024 * 1024
MOE_EXPERT_VMEM_LIMIT_BYTES = 60 * 1024 * 1024


def _router_kernel(x_ref, g_ref, wr_ref, h_ref, aff_ref):
    x = x_ref[...]
    h = (x * lax.rsqrt(jnp.mean(x * x, axis=-1, keepdims=True) + EPS) * g_ref[...]).astype(jnp.bfloat16)
    h_ref[...] = h
    logits = jnp.dot(h, wr_ref[...], preferred_element_type=jnp.float32)
    lane = lax.broadcasted_iota(jnp.int32, logits.shape, 1)
    logits = jnp.where(lane < N_EXPERTS, logits, -jnp.inf)
    p = jnp.exp(logits - jnp.max(logits, axis=-1, keepdims=True))
    aff = p / jnp.sum(p, axis=-1, keepdims=True)
    aff_ref[...] = aff.T[0:N_EXPERTS, :]


def moe_route(x, gain, w_router, *, tm=MOE_TT):
    bsz, l, d = x.shape
    wr = jnp.pad(w_router.astype(jnp.bfloat16), ((0, 0), (0, LANES - N_EXPERTS)))
    return pl.pallas_call(
        _router_kernel,
        grid=(bsz, l // tm),
        in_specs=[pl.BlockSpec((None, tm, d), lambda b, i: (b, i, 0)),
                  pl.BlockSpec((1, d), lambda b, i: (0, 0)),
                  pl.BlockSpec((d, LANES), lambda b, i: (0, 0))],
        out_specs=[pl.BlockSpec((None, tm, d), lambda b, i: (b, i, 0)),
                   pl.BlockSpec((None, N_EXPERTS, tm), lambda b, i: (b, 0, i))],
        out_shape=[jax.ShapeDtypeStruct((bsz, l, d), jnp.bfloat16),
                   jax.ShapeDtypeStruct((bsz, N_EXPERTS, l), jnp.float32)],
        compiler_params=pltpu.CompilerParams(dimension_semantics=("parallel", "parallel"),
                                             vmem_limit_bytes=VMEM_LIMIT_BYTES),
        name="moe_router",
    )(x, gain.reshape(1, d).astype(jnp.float32), wr)


def _select_kernel(aff_ref, pre_ref, smap_ref, gate_ref, cnt_ref, *, cap, tt):
    f32 = jnp.float32
    bf16 = jnp.bfloat16
    aff = aff_ref[...]
    e, l = aff.shape
    nl = l // LANES
    tiles = [slice(j * LANES, (j + 1) * LANES) for j in range(nl)]
    bits = pltpu.bitcast(aff, jnp.int32)
    bt = [bits[:, s] for s in tiles]

    def lane_total(x):
        return jnp.broadcast_to(jnp.sum(x, axis=-1, keepdims=True), (e, LANES))

    def search(i, thr):
        cand = thr | jnp.left_shift(jnp.int32(1), 30 - i)
        acc = jnp.zeros((e, LANES), jnp.int32)
        for x in bt:
            acc = acc + (x >= cand).astype(jnp.int32)
        return jnp.where(lane_total(acc) >= cap, cand, thr)

    thr = lax.fori_loop(0, 31, search, jnp.zeros((e, LANES), jnp.int32))
    gt = [x > thr for x in bt]
    eq = [x == thr for x in bt]
    acc = jnp.zeros((e, LANES), jnp.int32)
    for x in gt:
        acc = acc + x.astype(jnp.int32)
    need = (cap - lane_total(acc)).astype(f32)

    pre = pre_ref[...]

    def prefix(flags):
        outs = [jnp.dot(jnp.where(x, 1.0, 0.0).astype(bf16), pre, preferred_element_type=f32) for x in flags]
        carry = jnp.zeros((e, LANES), f32)
        res = []
        for o in outs:
            res.append(o[:, :LANES] + carry)
            carry = carry + o[:, LANES:]
        return res, [o[:, LANES:] for o in outs]

    rank_eq, _ = prefix(eq)
    sel = [jnp.logical_or(gt[j], jnp.logical_and(eq[j], rank_eq[j] < need)) for j in range(nl)]
    pos, totals = prefix(sel)
    lane = lax.broadcasted_iota(jnp.int32, (e, LANES), 1)
    cnt = jnp.zeros((e, LANES), f32)
    per = tt // LANES
    for j in range(nl):
        smap_ref[:, tiles[j]] = jnp.where(sel[j], pos[j], -1.0)
        gate_ref[:, tiles[j]] = jnp.where(sel[j], aff[:, tiles[j]], 0.0)
        cnt = cnt + jnp.where(lane == j // per, totals[j], 0.0)
    cnt_ref[...] = cnt


def moe_select(aff, cap, *, tt=MOE_TT):
    import numpy as np
    bsz, e, l = aff.shape
    assert l // tt <= LANES
    i = np.arange(LANES)
    pre = np.concatenate([i[:, None] < i[None, :], np.ones((LANES, LANES), bool)], axis=1)
    row = pl.BlockSpec((None, e, l), lambda b: (b, 0, 0))
    return pl.pallas_call(
        functools.partial(_select_kernel, cap=cap, tt=tt),
        grid=(bsz,),
        in_specs=[row, pl.BlockSpec((LANES, 2 * LANES), lambda b: (0, 0))],
        out_specs=[row, row, pl.BlockSpec((None, e, LANES), lambda b: (b, 0, 0))],
        out_shape=[jax.ShapeDtypeStruct((bsz, e, l), jnp.float32), jax.ShapeDtypeStruct((bsz, e, l), jnp.float32),
                   jax.ShapeDtypeStruct((bsz, e, LANES), jnp.float32)],
        compiler_params=pltpu.CompilerParams(dimension_semantics=("parallel",),
                                             vmem_limit_bytes=VMEM_LIMIT_BYTES),
        name="moe_select",
    )(aff, jnp.asarray(pre, jnp.bfloat16))


def _slot_one_hot(pos, base, rows, n):
    slot = (base + lax.broadcasted_iota(jnp.int32, (rows, n), 0)).astype(jnp.float32)
    return jnp.where(pos == slot, 1.0, 0.0).astype(jnp.bfloat16)


def _expert_kernel(cs_ref, h_ref, smap_ref, gate_ref, wg32_ref, wu32_ref, wd32_ref, o_ref,
                   xs_sc, gs_sc, wg_ref, wu_ref, wd_ref, *, nj, per, tt, cap):
    e = pl.program_id(0)
    b = pl.program_id(1)
    j = pl.program_id(2)
    f32 = jnp.float32
    bf16 = jnp.bfloat16

    @pl.when(jnp.logical_and(b == 0, j == 0))
    def _():
        wg_ref[...] = wg32_ref[...].astype(bf16)
        wu_ref[...] = wu32_ref[...].astype(bf16)
        wd_ref[...] = wd32_ref[...].astype(bf16)

    @pl.when(j == 0)
    def _():
        xs_sc[...] = jnp.zeros_like(xs_sc)
        gs_sc[...] = jnp.zeros_like(gs_sc)

    win = min(MOE_SUB, cap)
    for s in range(per):
        base = (b * N_EXPERTS + e) * (nj * per + 1) + j * per + s
        c0 = cs_ref[base]
        c1 = cs_ref[base + 1]
        cols = slice(s * tt, (s + 1) * tt)
        pos = smap_ref[:, cols]
        gate = gate_ref[:, cols]
        hb = h_ref[cols, :]

        def gather(r0, rows, pos=pos, gate=gate, hb=hb):
            oh = _slot_one_hot(pos, r0, rows, tt)
            xs_sc[pl.ds(r0, rows), :] += jnp.dot(oh, hb, preferred_element_type=f32)
            g = jnp.sum(oh.astype(f32) * gate, axis=-1, keepdims=True)
            gs_sc[pl.ds(r0, rows), :] += jnp.broadcast_to(g, (rows, LANES))

        w0 = pl.multiple_of(jnp.minimum(c0 // MOE_ALIGN * MOE_ALIGN, cap - win), MOE_ALIGN)
        gather(w0, win)

        def rest(st, carry, gather=gather):
            gather(pl.multiple_of(st * MOE_ALIGN, MOE_ALIGN), MOE_ALIGN)
            return carry

        lax.fori_loop((w0 + win) // MOE_ALIGN, (c1 + MOE_ALIGN - 1) // MOE_ALIGN, rest, 0)

    @pl.when(j == nj - 1)
    def _():
        rows_per = min(MOE_ROWS, cap)
        for r in range(cap // rows_per):
            rows = slice(r * rows_per, (r + 1) * rows_per)
            xb = xs_sc[rows, :].astype(bf16)
            g = jnp.dot(xb, wg_ref[...], preferred_element_type=f32)
            u = jnp.dot(xb, wu_ref[...], preferred_element_type=f32)
            hid = (g * jax.nn.sigmoid(g) * u).astype(bf16)
            out = jnp.dot(hid, wd_ref[...], preferred_element_type=f32)
            scale = jnp.concatenate([gs_sc[rows, :]] * (out.shape[1] // LANES), axis=1)
            o_ref[rows, :] = (out * scale).astype(bf16)


def moe_experts(hb, smap, gate, cs, w_gate, w_up, w_down, layer, cap, *, tt=MOE_TT):
    bsz, l, d = hb.shape
    _, e, _, ff = w_gate.shape
    per = MOE_TILES_PER_STEP if (l // tt) % MOE_TILES_PER_STEP == 0 else 1
    nj = l // (tt * per)
    smap4 = smap.reshape(bsz, e, 1, l)
    gate4 = gate.reshape(bsz, e, 1, l)
    tok = pl.BlockSpec((None, None, 1, per * tt), lambda ei, b, j, cs_ref: (b, ei, 0, j))
    once = pl.Buffered(1)
    grid_spec = pltpu.PrefetchScalarGridSpec(
        num_scalar_prefetch=1,
        grid=(e, bsz, nj),
        in_specs=[pl.BlockSpec((None, per * tt, d), lambda ei, b, j, cs_ref: (b, j, 0)), tok, tok,
                  pl.BlockSpec((None, None, d, ff), lambda ei, b, j, cs_ref: (layer, ei, 0, 0), pipeline_mode=once),
                  pl.BlockSpec((None, None, d, ff), lambda ei, b, j, cs_ref: (layer, ei, 0, 0), pipeline_mode=once),
                  pl.BlockSpec((None, None, ff, d), lambda ei, b, j, cs_ref: (layer, ei, 0, 0), pipeline_mode=once)],
        out_specs=pl.BlockSpec((None, None, cap, d), lambda ei, b, j, cs_ref: (b, ei, 0, 0)),
        scratch_shapes=[pltpu.VMEM((cap, d), jnp.float32), pltpu.VMEM((cap, LANES), jnp.float32),
                        pltpu.VMEM((d, ff), jnp.bfloat16), pltpu.VMEM((d, ff), jnp.bfloat16),
                        pltpu.VMEM((ff, d), jnp.bfloat16)])
    return pl.pallas_call(
        functools.partial(_expert_kernel, nj=nj, per=per, tt=tt, cap=cap),
        grid_spec=grid_spec,
        out_shape=jax.ShapeDtypeStruct((bsz, e, cap, d), jnp.bfloat16),
        compiler_params=pltpu.CompilerParams(dimension_semantics=("parallel", "arbitrary", "arbitrary"),
                                             vmem_limit_bytes=MOE_EXPERT_VMEM_LIMIT_BYTES),
        name="moe_experts",
    )(cs, hb, smap4, gate4, w_gate, w_up, w_down)


def _combine_kernel(cs_ref, x_ref, smap_ref, ow_ref, y_ref, *, nj, tt):
    b = pl.program_id(0)
    e = pl.program_id(2)

    @pl.when(e == 0)
    def _():
        y_ref[...] = x_ref[...]

    contract_first = (((0,), (0,)), ((), ()))
    cap = ow_ref.shape[0]
    win = min(2 * MOE_SUB, cap)
    base = (b * N_EXPERTS + e) * (nj + 1)
    cols = [slice(j * tt, (j + 1) * tt) for j in range(nj)]
    pos = [smap_ref[:, c] for c in cols]
    r0 = [pl.multiple_of(jnp.minimum(cs_ref[base + j] // MOE_SUB * MOE_SUB, cap - win), MOE_SUB) for j in range(nj)]
    oh = [_slot_one_hot(pos[j], r0[j], win, tt) for j in range(nj)]
    add = [lax.dot_general(oh[j], ow_ref[pl.ds(r0[j], win), :], contract_first, preferred_element_type=jnp.float32)
           for j in range(nj)]
    for j in range(nj):
        y_ref[cols[j], :] += add[j]

    for j in range(nj):
        def scatter(st, carry):
            s0 = pl.multiple_of(st * MOE_SUB, MOE_SUB)
            y_ref[cols[j], :] += lax.dot_general(_slot_one_hot(pos[j], s0, MOE_SUB, tt),
                                                 ow_ref[pl.ds(s0, MOE_SUB), :], contract_first,
                                                 preferred_element_type=jnp.float32)
            return carry

        lax.fori_loop((r0[j] + win) // MOE_SUB, (cs_ref[base + j + 1] + MOE_SUB - 1) // MOE_SUB, scatter, 0)


def moe_combine(x, smap, outw, cs, *, tt=MOE_TT):
    bsz, l, d = x.shape
    e, cap = outw.shape[1:3]
    nj = l // tt
    smap4 = smap.reshape(bsz, e, 1, l)
    grid_spec = pltpu.PrefetchScalarGridSpec(
        num_scalar_prefetch=1,
        grid=(bsz, d // MOE_SLAB, e),
        in_specs=[pl.BlockSpec((None, l, MOE_SLAB), lambda b, s, ei, cs_ref: (b, 0, s)),
                  pl.BlockSpec((None, None, 1, l), lambda b, s, ei, cs_ref: (b, ei, 0, 0)),
                  pl.BlockSpec((None, None, cap, MOE_SLAB), lambda b, s, ei, cs_ref: (b, ei, 0, s))],
        out_specs=pl.BlockSpec((None, l, MOE_SLAB), lambda b, s, ei, cs_ref: (b, 0, s)))
    return pl.pallas_call(
        functools.partial(_combine_kernel, nj=nj, tt=tt),
        grid_spec=grid_spec,
        out_shape=jax.ShapeDtypeStruct((bsz, l, d), jnp.float32),
        compiler_params=pltpu.CompilerParams(dimension_semantics=("parallel", "parallel", "arbitrary"),
                                             vmem_limit_bytes=MOE_VMEM_LIMIT_BYTES),
        name="moe_combine",
    )(cs, x, smap4, outw)


def ec_moe_layer(x, gain, w_router, w_gate, w_up, w_down, layer):
    bsz, l, d = x.shape
    cap = EC_CAPACITY_FACTOR * l // N_EXPERTS
    tt = min(MOE_TT, l)
    nj = l // tt
    hb, aff = moe_route(x, gain, w_router, tm=tt)
    smap, gate, cnt = moe_select(aff, cap, tt=tt)
    cs = jnp.concatenate([jnp.zeros((bsz, N_EXPERTS, 1), jnp.float32), jnp.cumsum(cnt[..., :nj], axis=-1)], axis=-1)
    cs = cs.astype(jnp.int32).reshape(-1)
    outw = moe_experts(hb, smap, gate, cs, w_gate, w_up, w_down, layer, cap, tt=tt)
    return moe_combine(x, smap, outw, cs, tt=tt)


def kernel(x, mix_norm, ffn_norm, ev_w_in, ev_w_out, a_lb_logits, a_out_norm, s5_lambda_re, s5_lambda_im, s5_log_step, s5_b_re, s5_b_im, s5_c_re, s5_c_im, s5_d, s5_glu_w, s5_glu_b, od_w_in, od_w_out, c_q_norm, c_k_norm, c_lambda, c_out_norm, rel_bias, d_conv_w, d_a_log, d_dt_bias, d_out_norm, moe_router, moe_w_gate, moe_w_up, moe_w_down):
    bsz, l, d = x.shape
    p = jax.nn.softmax(a_lb_logits.astype(jnp.float32), axis=0)
    cum = jnp.cumsum(p, axis=0)
    lower_bounds = cum - cum[0:1]
    bias5 = rel_bias_tiles(rel_bias, ATT_T)
    for layer in range(DEPTH):
        j = layer // 2
        if layer % 2 == 0:
            proj, u_tb = norm_matmul(x, mix_norm[layer], ev_w_in[j], tail=B_WIDTH)
            o_a2 = hgrn2_scan(proj, lower_bounds[j])
            o_b = s5_mixer_tb(u_tb.reshape(l * bsz, B_WIDTH), bsz, s5_lambda_re[j], s5_lambda_im[j], s5_log_step[j],
                              s5_b_re[j], s5_b_im[j], s5_c_re[j], s5_c_im[j], s5_d[j], s5_glu_w[j], s5_glu_b[j])
            x = mixer_out_proj(o_a2, proj, 4 * A_HEADS, a_out_norm[j], o_b.reshape(l, bsz * B_WIDTH), ev_w_out[j], x,
                               bidir_first=True, other_time_major=True)
        else:
            o2 = 3 * C_WIDTH + 3 * D_WIDTH
            o4 = o2 + 4 * D_HEADS
            w = od_w_in[j]
            w_in = jnp.concatenate([w[:, :o2], w[:, o4:], w[:, o2:o4],
                                    jnp.zeros((d, 3 * C_WIDTH + OD_COLS - w.shape[1]), w.dtype)], axis=1)
            proj, q2, kt, vb, stats = odd_in_proj(x, mix_norm[layer], w_in, c_q_norm[j], c_k_norm[j])
            o_c = diff_attention(q2, kt, vb, stats, c_lambda[j], c_out_norm[j], bias5, layer)
            o_d2 = gated_deltanet(proj, d_conv_w[j], d_a_log[j], d_dt_bias[j])
            x = mixer_out_proj(o_d2, proj, OD_GATE_BLOCK, d_out_norm[j], o_c, od_w_out[j], x, bidir_first=False)
        x = ec_moe_layer(x, ffn_norm[layer], moe_router[layer], moe_w_gate, moe_w_up, moe_w_down, layer)
    return x
```

```python
import functools
import math

import jax
import jax.numpy as jnp
import numpy as np
from jax import lax
from jax.experimental import pallas as pl
from jax.experimental.pallas import tpu as pltpu

D_MODEL = 1024
DEPTH = 4
MIX_WIDTH = D_MODEL
A_WIDTH = MIX_WIDTH // 2
A_HEAD_DIM = 128
A_HEADS = A_WIDTH // A_HEAD_DIM
B_WIDTH = MIX_WIDTH - A_WIDTH
S5_GROUP = 16
S5_GROUPS = B_WIDTH // S5_GROUP
S5_STATE = 64
C_WIDTH = MIX_WIDTH // 2
C_HEAD_DIM = 64
C_HEADS = C_WIDTH // (2 * C_HEAD_DIM)
D_WIDTH = MIX_WIDTH - C_WIDTH
D_HEAD_DIM = 128
D_HEADS = D_WIDTH // D_HEAD_DIM
CONV_WIDTH = 5
N_EXPERTS = 16
EC_CAPACITY_FACTOR = 2
REL_BUCKETS = 32
REL_MAX_DIST = 128
CHUNK = 64
EPS = 1e-6

LANES = 128
SUBLANES = 8
VMEM_LIMIT_BYTES = 48 * 1024 * 1024


PROJ_TM = 512
PROJ_COLS = 512


def _norm_matmul_kernel(x_ref, g_ref, w_ref, o_ref, *tail_ref, main):
    x = x_ref[...]
    y = (x * lax.rsqrt(jnp.mean(x * x, axis=-1, keepdims=True) + EPS) * g_ref[...]).astype(jnp.bfloat16)
    for c0 in range(0, main, PROJ_COLS):
        c1 = min(c0 + PROJ_COLS, main)
        o_ref[:, c0:c1] = jnp.dot(y, w_ref[:, c0:c1], preferred_element_type=jnp.float32)
    if tail_ref:
        tail_ref[0][...] = jnp.dot(y, w_ref[:, main:], preferred_element_type=jnp.float32)


def norm_matmul(x, gain, w, *, tail=0, tm=PROJ_TM):
    bsz, l, k = x.shape
    m = w.shape[1]
    main = m - tail
    tm = min(tm, l)
    out_shape = [jax.ShapeDtypeStruct((bsz, l, main), jnp.float32)]
    out_specs = [pl.BlockSpec((None, tm, main), lambda b, i: (b, i, 0))]
    if tail:
        out_shape.append(jax.ShapeDtypeStruct((l, bsz * tail), jnp.float32))
        out_specs.append(pl.BlockSpec((tm, tail), lambda b, i: (i, b)))
    outs = pl.pallas_call(
        functools.partial(_norm_matmul_kernel, main=main),
        grid=(bsz, l // tm),
        in_specs=[pl.BlockSpec((None, tm, k), lambda b, i: (b, i, 0)),
                  pl.BlockSpec((1, k), lambda b, i: (0, 0)),
                  pl.BlockSpec((k, m), lambda b, i: (0, 0), pipeline_mode=pl.Buffered(1))],
        out_specs=out_specs,
        out_shape=out_shape,
        compiler_params=pltpu.CompilerParams(dimension_semantics=("parallel", "parallel"),
                                             vmem_limit_bytes=VMEM_LIMIT_BYTES),
        name="norm_matmul",
    )(x, gain.reshape(1, k).astype(jnp.float32), w.astype(jnp.bfloat16))
    return outs if tail else outs[0]


def _mixer_out_kernel(of_ref, ob_ref, g_ref, gain_ref, other_ref, wb_ref, wo_ref, r_ref, o_ref):
    bf16 = jnp.bfloat16
    o = of_ref[...] + ob_ref[...]
    g = g_ref[...]
    gate = g * jax.nn.sigmoid(g)
    hd = gain_ref.shape[1]
    heads = []
    for h in range(o.shape[1] // hd):
        oh = o[:, h * hd:(h + 1) * hd]
        heads.append(oh * lax.rsqrt(jnp.mean(oh * oh, axis=-1, keepdims=True) + EPS) * gain_ref[...])
    y = (jnp.concatenate(heads, axis=1) * gate).astype(bf16)
    o_ref[...] = (r_ref[...] + jnp.dot(y, wb_ref[...], preferred_element_type=jnp.float32)
                  + jnp.dot(other_ref[...].astype(bf16), wo_ref[...], preferred_element_type=jnp.float32))


def mixer_out_proj(o2, proj3, gate_block, out_gain, other, w, res, *, bidir_first, other_time_major=False,
                   tm=PROJ_TM):
    _, bsz, l, k = o2.shape
    m = w.shape[1]
    tm = min(tm, l)
    wb = w.astype(jnp.bfloat16)
    w_bidir, w_other = (wb[:k], wb[k:]) if bidir_first else (wb[k:], wb[:k])
    other_spec = (pl.BlockSpec((tm, k), lambda b, i: (i, b)) if other_time_major
                  else pl.BlockSpec((None, tm, k), lambda b, i: (b, i, 0)))
    gb = gate_block * LANES // k
    row = pl.BlockSpec((None, tm, m), lambda b, i: (b, i, 0))
    wspec = pl.BlockSpec((k, m), lambda b, i: (0, 0))
    return pl.pallas_call(
        _mixer_out_kernel,
        grid=(bsz, l // tm),
        in_specs=[pl.BlockSpec((None, None, tm, k), lambda b, i: (0, b, i, 0)),
                  pl.BlockSpec((None, None, tm, k), lambda b, i: (1, b, i, 0)),
                  pl.BlockSpec((None, tm, k), lambda b, i: (b, i, gb)),
                  pl.BlockSpec((1, LANES), lambda b, i: (0, 0)),
                  other_spec, wspec, wspec, row],
        out_specs=row,
        out_shape=jax.ShapeDtypeStruct((bsz, l, m), jnp.float32),
        compiler_params=pltpu.CompilerParams(dimension_semantics=("parallel", "parallel"),
                                             vmem_limit_bytes=VMEM_LIMIT_BYTES),
        name="mixer_out_proj",
    )(o2, o2, proj3, out_gain.reshape(1, LANES).astype(jnp.float32), other, w_bidir, w_other, res)


HG_LEVELS = tuple(CHUNK >> (i + 1) for i in range(CHUNK.bit_length() - 1))
HG_TOT_ROWS = 8
HG_TT = 1024


def hgrn2_constants():
    c = CHUNK
    r = np.arange(c)[:, None]
    u = np.arange(c)[None, :]
    stacks, masks = [], []
    for direction in range(2):
        fwd = direction == 0
        lvl_masks = []
        for m in HG_LEVELS:
            blk = r // (2 * m)
            later = (r % (2 * m)) >= m
            lvl_masks.append((blk == blk.T) & (later & ~later.T if fwd else ~later & later.T))
        stacks.append(np.concatenate([(u <= r) if fwd else (u >= r), np.ones((HG_TOT_ROWS, c), bool)], axis=0))
        masks.append(np.stack(lvl_masks))
    return (jnp.asarray(np.stack(stacks), jnp.bfloat16), jnp.asarray(np.stack(masks), jnp.float32))


def _hgrn2_kernel(q_ref, f_ref, v_ref, loglb_ref, log1mlb_ref, onemlb_ref, ast_ref, mask_ref, o_ref,
                  st_sc, qd_sc, dec_sc, upd_sc, sin_sc, *, nc):
    direction = pl.program_id(2)

    @pl.when(pl.program_id(3) == 0)
    def _():
        st_sc[...] = jnp.zeros_like(st_sc)

    bf16 = jnp.bfloat16
    f32 = jnp.float32
    c = CHUNK
    hd = A_HEAD_DIM
    dirf = direction.astype(f32)
    loglb = loglb_ref[...]
    log1mlb = log1mlb_ref[...]
    onemlb = onemlb_ref[...]
    ast = ast_ref[...]
    contract_last = (((1,), (1,)), ((), ()))
    contract_first = (((0,), (0,)), ((), ()))

    ns = range(nc)
    rows = [slice(n * c, (n + 1) * c) for n in ns]
    z = [f_ref[r, :] for r in rows]
    v = [v_ref[r, :] for r in rows]
    qr = [q_ref[r, :] for r in rows]
    q = [x * jax.nn.sigmoid(x) for x in qr]
    e = [jnp.exp(-jnp.abs(x)) for x in z]
    cc = [log1mlb + jnp.minimum(z[n], 0.0) - jnp.log1p(e[n]) for n in ns]
    lf = [jnp.maximum(loglb, x) + jnp.log1p(jnp.exp(-jnp.abs(loglb - x))) for x in cc]
    k = [onemlb * jnp.where(z[n] >= 0, e[n], 1.0) / (1.0 + e[n]) for n in ns]
    hi = [x.astype(bf16) for x in lf]
    lo = [(lf[n] - hi[n].astype(f32)).astype(bf16) for n in ns]
    d = [jnp.dot(ast, hi[n], preferred_element_type=f32) + jnp.dot(ast, lo[n], preferred_element_type=f32)
         for n in ns]
    cum = [x[0:c] for x in d]
    tot = [x[c:c + HG_TOT_ROWS] for x in d]
    ref = [cum[n] - dirf * lf[n] for n in ns]
    attn = [jnp.zeros((c, c), f32) for _ in ns]
    for li, m in enumerate(HG_LEVELS):
        nb = c // (2 * m)
        split = [jnp.broadcast_to(x.reshape(nb, 2 * m, hd)[:, m - 1:m, :], (nb, 2 * m, hd)).reshape(c, hd)
                 for x in ref]
        x = [jnp.exp(-jnp.abs(cum[n] - split[n])) for n in ns]
        s = [lax.dot_general((q[n] * x[n]).astype(bf16), (k[n] * x[n]).astype(bf16), contract_last,
                             preferred_element_type=f32) for n in ns]
        attn = [attn[n] + mask_ref[li] * s[n] for n in ns]
    vb = [x.astype(bf16) for x in v]
    intra = [jnp.dot(attn[n].astype(bf16), vb[n], preferred_element_type=f32) for n in ns]
    upd = [lax.dot_general(vb[n], (k[n] * jnp.exp(tot[n][0:1] - cum[n])).astype(bf16), contract_first,
                           preferred_element_type=f32) for n in ns]
    for n in ns:
        o_ref[rows[n], :] = intra[n] + jnp.sum(q[n] * k[n], axis=-1, keepdims=True) * v[n]
        qd_sc[n] = (q[n] * jnp.exp(cum[n])).astype(bf16)
        dec_sc[n] = jnp.exp(tot[n])
        upd_sc[n] = upd[n]

    def body(ci, st):
        ce = ci + direction * (nc - 1 - 2 * ci)
        sin_sc[ce] = st.astype(bf16)
        return st * dec_sc[ce][0:1] + upd_sc[ce]

    st_sc[...] = lax.fori_loop(0, nc, body, st_sc[...])

    for n in range(nc):
        rows = slice(n * c, (n + 1) * c)
        o_ref[rows, :] += lax.dot_general(qd_sc[n], sin_sc[n], contract_last, preferred_element_type=f32)


def hgrn2_scan(proj3, lb):
    bsz, l, _ = proj3.shape
    hd = A_HEAD_DIM
    tt = min(HG_TT, l)
    nt = l // tt
    assert l % tt == 0 and tt % CHUNK == 0
    ast, masks = hgrn2_constants()
    lb = lb.astype(jnp.float32)
    vecs = [jnp.log(lb).reshape(2, 1, A_WIDTH), jnp.log1p(-lb).reshape(2, 1, A_WIDTH), (1.0 - lb).reshape(2, 1, A_WIDTH)]
    tidx = lambda d, i: i + d * (nt - 1 - 2 * i)
    vec = pl.BlockSpec((None, 1, hd), lambda b, h, d, i: (d, 0, h))
    return pl.pallas_call(
        functools.partial(_hgrn2_kernel, nc=tt // CHUNK),
        grid=(bsz, A_HEADS, 2, nt),
        in_specs=[pl.BlockSpec((None, tt, hd), lambda b, h, d, i: (b, tidx(d, i), h)),
                  pl.BlockSpec((None, tt, hd), lambda b, h, d, i: (b, tidx(d, i), (1 + d) * A_HEADS + h)),
                  pl.BlockSpec((None, tt, hd), lambda b, h, d, i: (b, tidx(d, i), 3 * A_HEADS + h)),
                  vec, vec, vec,
                  pl.BlockSpec((None,) + ast.shape[1:], lambda b, h, d, i: (d, 0, 0)),
                  pl.BlockSpec((None,) + masks.shape[1:], lambda b, h, d, i: (d, 0, 0, 0))],
        out_specs=pl.BlockSpec((None, None, tt, hd), lambda b, h, d, i: (d, b, tidx(d, i), h)),
        out_shape=jax.ShapeDtypeStruct((2, bsz, l, A_WIDTH), jnp.float32),
        scratch_shapes=[pltpu.VMEM((hd, hd), jnp.float32),
                        pltpu.VMEM((tt // CHUNK, CHUNK, hd), jnp.bfloat16),
                        pltpu.VMEM((tt // CHUNK, HG_TOT_ROWS, hd), jnp.float32),
                        pltpu.VMEM((tt // CHUNK, hd, hd), jnp.float32),
                        pltpu.VMEM((tt // CHUNK, hd, hd), jnp.bfloat16)],
        compiler_params=pltpu.CompilerParams(dimension_semantics=("parallel", "parallel", "parallel", "arbitrary"),
                                             vmem_limit_bytes=VMEM_LIMIT_BYTES),
        name="hgrn2_scan",
    )(proj3, proj3, proj3, *vecs, ast, masks)


S5_NS = S5_GROUPS * S5_STATE
S5_TT = 128


def _s5_scan_kernel(u_ref, win_ref, ar_ref, ai_ref, wout_ref, y_ref, bu_sc, xs_sc, st_sc, *, bsz, tt, reverse):
    @pl.when(pl.program_id(0) == 0)
    def _():
        st_sc[...] = jnp.zeros_like(st_sc)

    ub = u_ref[...].astype(jnp.bfloat16)
    halves = 2
    uw = B_WIDTH // halves
    sw = S5_NS // halves
    for hf in range(halves):
        for part in range(2):
            sc = slice(part * S5_NS + hf * sw, part * S5_NS + (hf + 1) * sw)
            bu_sc[:, sc] = jnp.dot(ub[:, hf * uw:(hf + 1) * uw], win_ref[hf * uw:(hf + 1) * uw, sc],
                                   preferred_element_type=jnp.float32)
    ar = jnp.broadcast_to(ar_ref[...], (bsz, S5_NS))
    ai = jnp.broadcast_to(ai_ref[...], (bsz, S5_NS))
    per = SUBLANES // bsz
    ngroups = tt // per

    def body(s, carry):
        xr, xi = carry
        p = (ngroups - 1 - s) if reverse else s
        base = pl.multiple_of(p * SUBLANES, SUBLANES)
        blk = bu_sc[pl.ds(base, SUBLANES), :]
        outs_r = [None] * per
        outs_i = [None] * per
        for ph in (range(per - 1, -1, -1) if reverse else range(per)):
            br = blk[ph * bsz:(ph + 1) * bsz, :S5_NS]
            bi = blk[ph * bsz:(ph + 1) * bsz, S5_NS:]
            xr, xi = ar * xr - ai * xi + br, ar * xi + ai * xr + bi
            outs_r[ph] = xr
            outs_i[ph] = xi
        xs_sc[pl.ds(base, SUBLANES), :S5_NS] = jnp.concatenate(outs_r, axis=0)
        xs_sc[pl.ds(base, SUBLANES), S5_NS:] = jnp.concatenate(outs_i, axis=0)
        return xr, xi

    xr, xi = lax.fori_loop(0, ngroups, body, (st_sc[0], st_sc[1]))
    st_sc[0] = xr
    st_sc[1] = xi
    for hf in range(halves):
        yc = slice(hf * uw, (hf + 1) * uw)
        acc = None
        for part in range(2):
            sc = slice(part * S5_NS + hf * sw, part * S5_NS + (hf + 1) * sw)
            term = jnp.dot(xs_sc[:, sc].astype(jnp.bfloat16), wout_ref[sc, yc], preferred_element_type=jnp.float32)
            acc = term if acc is None else acc + term
        y_ref[:, yc] = acc


def s5_scan(u_tb, win, ar, ai, wout, *, bsz, reverse):
    n = u_tb.shape[0]
    rows = S5_TT * bsz
    nt = n // rows
    assert n % rows == 0 and SUBLANES % bsz == 0
    idx = (lambda i: (nt - 1 - i, 0)) if reverse else (lambda i: (i, 0))
    const = lambda i: (0, 0)
    return pl.pallas_call(
        functools.partial(_s5_scan_kernel, bsz=bsz, tt=S5_TT, reverse=reverse),
        grid=(nt,),
        in_specs=[pl.BlockSpec((rows, B_WIDTH), idx),
                  pl.BlockSpec((B_WIDTH, 2 * S5_NS), const),
                  pl.BlockSpec((1, S5_NS), const),
                  pl.BlockSpec((1, S5_NS), const),
                  pl.BlockSpec((2 * S5_NS, B_WIDTH), const)],
        out_specs=pl.BlockSpec((rows, B_WIDTH), idx),
        out_shape=jax.ShapeDtypeStruct((n, B_WIDTH), jnp.float32),
        scratch_shapes=[pltpu.VMEM((rows, 2 * S5_NS), jnp.float32),
                        pltpu.VMEM((rows, 2 * S5_NS), jnp.float32),
                        pltpu.VMEM((2, bsz, S5_NS), jnp.float32)],
        compiler_params=pltpu.CompilerParams(dimension_semantics=("arbitrary",),
                                             vmem_limit_bytes=VMEM_LIMIT_BYTES),
        name="s5_scan_bwd" if reverse else "s5_scan_fwd",
    )(u_tb, win, ar, ai, wout)


def _s5_final_kernel(u_ref, yf_ref, yb_ref, d_ref, w_ref, b_ref, o_ref):
    y = d_ref[...] * u_ref[...] + yf_ref[...] + yb_ref[...]
    y = jax.nn.gelu(y)
    z = jnp.dot(y.astype(jnp.bfloat16), w_ref[...], preferred_element_type=jnp.float32) + b_ref[...]
    o_ref[...] = y * jax.nn.sigmoid(z)


def s5_finalize(u, yf, yb, d_skip, glu_w, glu_b, *, tm=512):
    n, w = u.shape
    row = pl.BlockSpec((tm, w), lambda i: (i, 0))
    vec = pl.BlockSpec((1, w), lambda i: (0, 0))
    return pl.pallas_call(
        _s5_final_kernel,
        grid=(n // tm,),
        in_specs=[row, row, row, vec, pl.BlockSpec((w, w), lambda i: (0, 0)), vec],
        out_specs=row,
        out_shape=jax.ShapeDtypeStruct((n, w), jnp.float32),
        compiler_params=pltpu.CompilerParams(dimension_semantics=("parallel",),
                                             vmem_limit_bytes=VMEM_LIMIT_BYTES),
        name="s5_finalize",
    )(u, yf, yb, d_skip.reshape(1, w).astype(jnp.float32), glu_w.astype(jnp.bfloat16),
      glu_b.reshape(1, w).astype(jnp.float32))


def s5_direction_params(lam_re, lam_im, log_step, b_re, b_im, c_re, c_im):
    step = jnp.exp(log_step)[:, None]
    mag = jnp.exp(lam_re * step)
    abar_re = mag * jnp.cos(lam_im * step)
    abar_im = mag * jnp.sin(lam_im * step)
    den = lam_re * lam_re + lam_im * lam_im
    fr = ((abar_re - 1.0) * lam_re + abar_im * lam_im) / den
    fi = (abar_im * lam_re - (abar_re - 1.0) * lam_im) / den
    bb_re = fr[..., None] * b_re - fi[..., None] * b_im
    bb_im = fr[..., None] * b_im + fi[..., None] * b_re
    eye = jnp.eye(S5_GROUPS, dtype=jnp.float32)
    win = jnp.concatenate([jnp.einsum('gnp,gh->gphn', bb, eye).reshape(B_WIDTH, S5_NS) for bb in (bb_re, bb_im)],
                          axis=1)
    wout = jnp.concatenate([jnp.einsum('gpn,gh->hngp', c, eye).reshape(S5_NS, B_WIDTH) for c in (c_re, -c_im)],
                           axis=0)
    return (win.astype(jnp.bfloat16), abar_re.reshape(1, S5_NS), abar_im.reshape(1, S5_NS),
            wout.astype(jnp.bfloat16))


def s5_mixer_tb(u_tb, bsz, lam_re, lam_im, log_step, b_re, b_im, c_re, c_im, d_skip, glu_w, glu_b):
    f32 = jnp.float32
    ys = []
    for direction in range(2):
        prm = s5_direction_params(lam_re[direction].astype(f32), lam_im[direction].astype(f32),
                                  log_step[direction].astype(f32), b_re[direction].astype(f32),
                                  b_im[direction].astype(f32), c_re[direction].astype(f32),
                                  c_im[direction].astype(f32))
        ys.append(s5_scan(u_tb, *prm, bsz=bsz, reverse=(direction == 1)))
    return s5_finalize(u_tb, ys[0], ys[1], d_skip, glu_w, glu_b)


def t5_bucket(rel):
    half = REL_BUCKETS // 2
    max_exact = half // 2
    base = jnp.where(rel > 0, half, 0)
    n = jnp.abs(rel)
    nf = jnp.maximum(n, 1).astype(jnp.float32)
    large = max_exact + (jnp.log(nf / max_exact) / math.log(REL_MAX_DIST / max_exact)
                         * (half - max_exact)).astype(jnp.int32)
    large = jnp.minimum(large, half - 1)
    return base + jnp.where(n < max_exact, n, large)


ATT_T = 512
LOG2E = math.log2(math.e)


def rel_bias_tiles(rel_bias, t):
    assert t >= REL_MAX_DIST
    table = rel_bias.astype(jnp.float32) * LOG2E
    tiles = []
    for d in (-1, 0, 1):
        c = table[t5_bucket(d * t + jnp.arange(-(t - 1), t))]
        w = jnp.concatenate([c, c[:1]], axis=0)
        m = jnp.tile(w, (t, 1))[:t * (2 * t - 1)].reshape(t, 2 * t - 1, -1)
        tiles.append(m[:, t - 1:2 * t - 1])
    far_neg = jnp.broadcast_to(table[t5_bucket(jnp.array(-2 * t))], tiles[0].shape)
    far_pos = jnp.broadcast_to(table[t5_bucket(jnp.array(2 * t))], tiles[0].shape)
    out = jnp.stack([far_neg] + tiles + [far_pos], axis=0)
    return jnp.transpose(out, (3, 0, 1, 2))


def _attn_operands(q, k, v, qg, kg):
    f32 = jnp.float32
    bf16 = jnp.bfloat16
    lane = lax.broadcasted_iota(jnp.int32, q.shape, 1)
    lo = lane < C_HEAD_DIM

    def half_sums(sq):
        return (jnp.sum(jnp.where(lo, sq, 0.0), axis=-1, keepdims=True),
                jnp.sum(jnp.where(lo, 0.0, sq), axis=-1, keepdims=True))

    def halfnorm(x, g):
        s_lo, s_hi = half_sums(x * x)
        return x * lax.rsqrt(jnp.where(lo, s_lo, s_hi) * (1.0 / C_HEAD_DIM) + EPS) * g

    def max_sq_norms(xb):
        n_lo, n_hi = half_sums(xb.astype(f32) * xb.astype(f32))
        return jnp.max(n_lo, axis=0, keepdims=True), jnp.max(n_hi, axis=0, keepdims=True)

    qn = halfnorm(q, qg) * (C_HEAD_DIM ** -0.5 * LOG2E)
    kn = halfnorm(k, kg)
    qb = qn.astype(bf16)
    kb = kn.astype(bf16)
    q2 = (jnp.where(lo, qb, 0.0).astype(bf16), jnp.where(lo, 0.0, qb).astype(bf16))
    q_lo, q_hi = max_sq_norms(qb)
    k_lo, k_hi = max_sq_norms(kb)
    sub = lax.broadcasted_iota(jnp.int32, (SUBLANES, q.shape[1]), 0)
    stats = jnp.where(sub == 0, q_lo, jnp.where(sub == 1, q_hi, jnp.where(sub == 2, k_lo,
                      jnp.where(sub == 3, k_hi, 0.0))))
    return q2, kn.T.astype(bf16), v.astype(bf16), stats


def _odd_proj_kernel(x_ref, g_ref, w_ref, qg_ref, kg_ref, o_ref, q2_ref, kt_ref, vb_ref, st_ref):
    x = x_ref[...]
    y = (x * lax.rsqrt(jnp.mean(x * x, axis=-1, keepdims=True) + EPS) * g_ref[...]).astype(jnp.bfloat16)
    hw = 2 * C_HEAD_DIM
    q, k, v = (jnp.dot(y, w_ref[:, p * C_WIDTH:(p + 1) * C_WIDTH], preferred_element_type=jnp.float32)
               for p in range(3))
    for h in range(C_HEADS):
        cols = slice(h * hw, (h + 1) * hw)
        q2, kt, vb, stats = _attn_operands(q[:, cols], k[:, cols], v[:, cols], qg_ref[...], kg_ref[...])
        q2_ref[h, 0] = q2[0]
        q2_ref[h, 1] = q2[1]
        kt_ref[h] = kt
        vb_ref[h] = vb
        st_ref[h] = stats
    att = 3 * C_WIDTH
    for c0 in range(att, w_ref.shape[1], PROJ_COLS):
        c1 = min(c0 + PROJ_COLS, w_ref.shape[1])
        o_ref[:, c0 - att:c1 - att] = jnp.dot(y, w_ref[:, c0:c1], preferred_element_type=jnp.float32)


def odd_in_proj(x, gain, w, q_gain, k_gain, *, tm=PROJ_TM):
    bsz, l, kdim = x.shape
    m = w.shape[1]
    rest = m - 3 * C_WIDTH
    tm = min(tm, l)
    hw = 2 * C_HEAD_DIM
    gq = jnp.tile(q_gain.astype(jnp.float32), 2).reshape(1, hw)
    gk = jnp.tile(k_gain.astype(jnp.float32), 2).reshape(1, hw)
    vec = pl.BlockSpec((1, hw), lambda b, i: (0, 0))
    return pl.pallas_call(
        _odd_proj_kernel,
        grid=(bsz, l // tm),
        in_specs=[pl.BlockSpec((None, tm, kdim), lambda b, i: (b, i, 0)),
                  pl.BlockSpec((1, kdim), lambda b, i: (0, 0)),
                  pl.BlockSpec((kdim, m), lambda b, i: (0, 0), pipeline_mode=pl.Buffered(1)),
                  vec, vec],
        out_specs=[pl.BlockSpec((None, tm, rest), lambda b, i: (b, i, 0)),
                   pl.BlockSpec((None, C_HEADS, 2, tm, hw), lambda b, i: (b, 0, 0, i, 0)),
                   pl.BlockSpec((None, C_HEADS, hw, tm), lambda b, i: (b, 0, 0, i)),
                   pl.BlockSpec((None, C_HEADS, tm, hw), lambda b, i: (b, 0, i, 0)),
                   pl.BlockSpec((None, C_HEADS, None, SUBLANES, hw), lambda b, i: (b, 0, i, 0, 0))],
        out_shape=[jax.ShapeDtypeStruct((bsz, l, rest), jnp.float32),
                   jax.ShapeDtypeStruct((bsz, C_HEADS, 2, l, hw), jnp.bfloat16),
                   jax.ShapeDtypeStruct((bsz, C_HEADS, hw, l), jnp.bfloat16),
                   jax.ShapeDtypeStruct((bsz, C_HEADS, l, hw), jnp.bfloat16),
                   jax.ShapeDtypeStruct((bsz, C_HEADS, l // tm, SUBLANES, hw), jnp.float32)],
        compiler_params=pltpu.CompilerParams(dimension_semantics=("parallel", "parallel"),
                                             vmem_limit_bytes=VMEM_LIMIT_BYTES),
        name="odd_in_proj",
    )(x, gain.reshape(1, kdim).astype(jnp.float32), w.astype(jnp.bfloat16), gq, gk)


ATT_ROWS = 64
ATT_KEY_TILES = 4
ATT_SAFE_GAP = 100.0


def _attn_kernel(lam_ref, kmax_ref, bmax_ref, q2_ref, kt_ref, v_ref, bias_ref, g_ref, o_ref,
                 m_sc, l_sc, acc_sc, s_sc, p_sc, a_sc, *, t, nk, ktiles, out_scale, bounded):
    f32 = jnp.float32
    b = pl.program_id(0)
    h = pl.program_id(1)
    qi = pl.program_id(2)
    q2 = q2_ref[...].reshape(2 * t, 2 * C_HEAD_DIM)
    r = ATT_ROWS
    hw = 2 * C_HEAD_DIM
    if bounded:
        q2f = q2.astype(f32)
        nq = jnp.sqrt(jnp.sum(q2f * q2f, axis=-1, keepdims=True))
        row = lax.broadcasted_iota(jnp.int32, nq.shape, 0)
        kc = jnp.where(row < t, kmax_ref[(b * C_HEADS + h) * 2], kmax_ref[(b * C_HEADS + h) * 2 + 1])
        m_sc[...] = jnp.broadcast_to(nq * kc + bmax_ref[h], m_sc.shape)
    else:
        m_sc[...] = jnp.full(m_sc.shape, -jnp.inf, f32)
    l_sc[...] = jnp.zeros_like(l_sc)
    acc_sc[...] = jnp.zeros_like(acc_sc)

    kw = ktiles * t

    def body(ki, carry):
        off = pl.multiple_of(ki * kw, kw)
        bidx = [jnp.clip(ki * ktiles + c - qi, -2, 2) + 2 for c in range(ktiles)]
        s_sc[...] = jnp.dot(q2, kt_ref[:, pl.ds(off, kw)], preferred_element_type=f32)
        for g in range(2 * t // r):
            rows = slice(g * r, (g + 1) * r)
            brow = (g * r) % t
            m = m_sc[rows, :]
            s = [s_sc[rows, c * t:(c + 1) * t] + bias_ref[bidx[c], brow:brow + r, :] for c in range(ktiles)]
            if not bounded:
                m_prev = m
                smax = functools.reduce(jnp.maximum, [jnp.max(x, axis=-1, keepdims=True) for x in s])
                m = jnp.maximum(m_prev, smax)
                alpha = jnp.exp2(m_prev - m)
                m_sc[rows, :] = m
                a_sc[rows, :] = alpha
            ps = [jnp.exp2(x[:, j * hw:(j + 1) * hw] - m) for x in s for j in range(t // hw)]
            psum = jnp.sum(sum(ps), axis=-1, keepdims=True)
            l_sc[rows, :] = (l_sc[rows, :] if bounded else alpha * l_sc[rows, :]) + psum
            for j in range(kw // hw):
                p_sc[rows, j * hw:(j + 1) * hw] = ps[j].astype(jnp.bfloat16)
        pv = jnp.dot(p_sc[...], v_ref[pl.ds(off, kw), :], preferred_element_type=f32)
        acc_sc[...] = (acc_sc[...] if bounded else a_sc[...] * acc_sc[...]) + pv
        return carry

    lax.fori_loop(0, nk // ktiles, body, 0)
    a = acc_sc[...] / l_sc[...]
    o = a[:t] - lam_ref[0] * a[t:]
    y = o * lax.rsqrt(jnp.mean(o * o, axis=-1, keepdims=True) + EPS)
    o_ref[...] = y * g_ref[...] * out_scale


def diff_attention(q2, kt, vb, stats, lam, out_gain, bias5, layer_idx):
    f32 = jnp.float32
    bsz, _, l, _ = vb.shape
    t = ATT_T
    hw = 2 * C_HEAD_DIM
    lam_init = 0.8 - 0.6 * math.exp(-0.3 * layer_idx)
    lam_f = lam.astype(f32)
    lam_full = jnp.exp(jnp.sum(lam_f[0] * lam_f[1])) - jnp.exp(jnp.sum(lam_f[2] * lam_f[3])) + lam_init
    norms = jnp.sqrt(jnp.max(stats[..., 0:4, 0], axis=2)) * (1.0 + 1e-3)
    qmax, kmax = norms[..., 0:2], norms[..., 2:4]
    bmax = jnp.max(bias5, axis=(1, 2, 3))
    bmin = jnp.min(bias5, axis=(1, 2, 3))
    gap = 2.0 * qmax * kmax + (bmax - bmin)[None, :, None]
    smem = pl.BlockSpec(memory_space=pltpu.SMEM)
    ktiles = math.gcd(ATT_KEY_TILES, l // t)

    def run(bounded):
        return pl.pallas_call(
            functools.partial(_attn_kernel, t=t, nk=l // t, ktiles=ktiles, out_scale=1.0 - lam_init,
                              bounded=bounded),
            grid=(bsz, C_HEADS, l // t),
            in_specs=[smem, smem, smem,
                      pl.BlockSpec((None, None, 2, t, hw), lambda b, h, i: (b, h, 0, i, 0)),
                      pl.BlockSpec((None, None, hw, l), lambda b, h, i: (b, h, 0, 0)),
                      pl.BlockSpec((None, None, l, hw), lambda b, h, i: (b, h, 0, 0)),
                      pl.BlockSpec((None, 5, t, t), lambda b, h, i: (h, 0, 0, 0)),
                      pl.BlockSpec((1, hw), lambda b, h, i: (0, 0))],
            out_specs=pl.BlockSpec((None, t, hw), lambda b, h, i: (b, i, h)),
            out_shape=jax.ShapeDtypeStruct((bsz, l, C_WIDTH), f32),
            scratch_shapes=[pltpu.VMEM((2 * t, hw), f32), pltpu.VMEM((2 * t, hw), f32), pltpu.VMEM((2 * t, hw), f32),
                            pltpu.VMEM((2 * t, ktiles * t), f32), pltpu.VMEM((2 * t, ktiles * t), jnp.bfloat16),
                            pltpu.VMEM((2 * t, hw), f32)],
            compiler_params=pltpu.CompilerParams(dimension_semantics=("parallel", "parallel", "arbitrary"),
                                                 vmem_limit_bytes=VMEM_LIMIT_BYTES),
            name="diff_attention_bounded" if bounded else "diff_attention_online",
        )(lam_full.reshape(1), kmax.reshape(-1), bmax, q2, kt, vb, bias5, out_gain.reshape(1, hw).astype(f32))

    return lax.cond(jnp.all(gap < ATT_SAFE_GAP), lambda: run(True), lambda: run(False))


GDN_TT = 512
GDN_HEADS_PER_STEP = 4
OD_QKV_BLOCK = 0
OD_GATE_BLOCK = OD_QKV_BLOCK + 3 * D_WIDTH // LANES
OD_AB_BLOCK = OD_GATE_BLOCK + D_WIDTH // LANES
OD_COLS = 2304


def _gdn_prep_kernel(prev_ref, cur_ref, next_ref, w_ref, o_ref, *, tl, nl):
    i = pl.program_id(1)
    part = pl.program_id(2)
    prev = jnp.where(i > 0, prev_ref[...], 0.0)
    nxt = jnp.where(i < nl - 1, next_ref[...], 0.0)
    ext = jnp.concatenate([prev, cur_ref[...], nxt], axis=0)
    halo = prev.shape[0]
    acc = None
    for j in range(CONV_WIDTH):
        start = halo - CONV_WIDTH // 2 + j
        term = w_ref[j:j + 1, :] * ext[start:start + tl, :]
        acc = term if acc is None else acc + term
    y = acc * jax.nn.sigmoid(acc)
    scale = jnp.where(part == 0, D_HEAD_DIM ** -0.5, 1.0)
    heads = []
    for h in range(D_HEADS):
        yh = y[:, h * LANES:(h + 1) * LANES]
        heads.append(yh * (lax.rsqrt(jnp.sum(yh * yh, axis=-1, keepdims=True) + EPS) * scale))
    o_ref[...] = jnp.where(part < 2, jnp.concatenate(heads, axis=1), y)


def gdn_prep(proj3, conv_w, *, tl=1024):
    bsz, l, _ = proj3.shape
    halo = SUBLANES
    nl = l // tl
    blk0 = OD_QKV_BLOCK * LANES // D_WIDTH
    return pl.pallas_call(
        functools.partial(_gdn_prep_kernel, tl=tl, nl=nl),
        grid=(bsz, nl, 3),
        in_specs=[pl.BlockSpec((None, halo, D_WIDTH), lambda b, i, p: (b, jnp.maximum(i * (tl // halo) - 1, 0), blk0 + p)),
                  pl.BlockSpec((None, tl, D_WIDTH), lambda b, i, p: (b, i, blk0 + p)),
                  pl.BlockSpec((None, halo, D_WIDTH),
                               lambda b, i, p: (b, jnp.minimum((i + 1) * (tl // halo), l // halo - 1), blk0 + p)),
                  pl.BlockSpec((CONV_WIDTH, D_WIDTH), lambda b, i, p: (0, p))],
        out_specs=pl.BlockSpec((None, None, tl, D_WIDTH), lambda b, i, p: (p, b, i, 0)),
        out_shape=jax.ShapeDtypeStruct((3, bsz, l, D_WIDTH), jnp.float32),
        compiler_params=pltpu.CompilerParams(dimension_semantics=("parallel", "parallel", "parallel"),
                                             vmem_limit_bytes=VMEM_LIMIT_BYTES),
        name="gdn_prep",
    )(proj3, proj3, proj3, conv_w.astype(jnp.float32))


def _gdn_gates_kernel(x_ref, nega_ref, dtb_ref, o_ref):
    x = x_ref[...]
    z = x + dtb_ref[...]
    g = nega_ref[...] * (jnp.maximum(z, 0.0) + jnp.log1p(jnp.exp(-jnp.abs(z))))
    lane = lax.broadcasted_iota(jnp.int32, x.shape, 1)
    y = jnp.where(lane < 2 * D_HEADS, g, jax.nn.sigmoid(x))
    o_ref[...] = y.T[0:4 * D_HEADS, :]


def gdn_gates(proj3, a_log, dt_bias, *, tl=512):
    bsz, l, _ = proj3.shape
    pad = LANES - 2 * D_HEADS
    nega = jnp.pad(-jnp.exp(a_log.astype(jnp.float32)).reshape(1, -1), ((0, 0), (0, pad)))
    dtb = jnp.pad(dt_bias.astype(jnp.float32).reshape(1, -1), ((0, 0), (0, pad)))
    vec = pl.BlockSpec((1, LANES), lambda b, i: (0, 0))
    return pl.pallas_call(
        _gdn_gates_kernel,
        grid=(bsz, l // tl),
        in_specs=[pl.BlockSpec((None, tl, LANES), lambda b, i: (b, i, OD_AB_BLOCK)), vec, vec],
        out_specs=pl.BlockSpec((None, 4 * D_HEADS, tl), lambda b, i: (b, 0, i)),
        out_shape=jax.ShapeDtypeStruct((bsz, 4 * D_HEADS, l), jnp.float32),
        compiler_params=pltpu.CompilerParams(dimension_semantics=("parallel", "parallel"),
                                             vmem_limit_bytes=VMEM_LIMIT_BYTES),
        name="gdn_gates",
    )(proj3, nega, dtb)


def gdn_constants():
    c = CHUNK
    r = np.arange(c)[:, None]
    u = np.arange(c)[None, :]
    cum, incl, strict = [], [], []
    for direction in range(2):
        fwd = direction == 0
        cum.append(np.concatenate([(r <= u) if fwd else (r >= u), np.ones((c, c), bool)], axis=1))
        incl.append((u <= r) if fwd else (u >= r))
        strict.append((u < r) if fwd else (u > r))
    same = lambda b: (r // b) == (u // b)
    merges = [same(2 * b) & ~same(b) for b in (8, 16, 32)]
    f32 = jnp.float32
    return (jnp.asarray(np.stack(cum), jnp.bfloat16), jnp.asarray(np.stack(incl), f32),
            jnp.asarray(np.stack(strict), f32), jnp.asarray(same(8), f32), jnp.asarray(np.stack(merges), f32))


def _gdn_kernel(q_ref, k_ref, v_ref, g_ref, b_ref, cum_ref, incl_ref, strict_ref, d8_ref, mrg_ref, o_ref,
                s_sc, qd_sc, dec_sc, w_sc, u_sc, sin_sc, *, nc):
    direction = pl.program_id(2)

    @pl.when(pl.program_id(3) == 0)
    def _():
        s_sc[...] = jnp.zeros_like(s_sc)

    bf16 = jnp.bfloat16
    f32 = jnp.float32
    c = CHUNK
    hd = D_HEAD_DIM
    contract_last = (((1,), (1,)), ((), ()))
    contract_first = (((0,), (0,)), ((), ()))
    cumm = cum_ref[...]
    incl = incl_ref[...]
    strict = strict_ref[...]
    d8 = d8_ref[...]
    eye = (lax.broadcasted_iota(jnp.int32, (c, c), 0) == lax.broadcasted_iota(jnp.int32, (c, c), 1)).astype(f32)

    def mm(a, b):
        return jnp.dot(a.astype(bf16), b.astype(bf16), preferred_element_type=f32)

    def rep(x):
        return jnp.concatenate([x] * (hd // c), axis=1)

    nh = g_ref.shape[0]
    ns = range(nh * nc)
    head = [m // nc for m in ns]
    rows = [slice((m % nc) * c, (m % nc + 1) * c) for m in ns]
    cols = [slice(h * hd, (h + 1) * hd) for h in head]
    q = [q_ref[rows[m], cols[m]] for m in ns]
    k = [k_ref[rows[m], cols[m]] for m in ns]
    v = [v_ref[rows[m], cols[m]] for m in ns]
    kb = [x.astype(bf16) for x in k]
    kk = [lax.dot_general(x, x, contract_last, preferred_element_type=f32) for x in kb]
    qk = [lax.dot_general(q[n].astype(bf16), kb[n], contract_last, preferred_element_type=f32) for n in ns]
    grow = [jnp.broadcast_to(g_ref[head[m], :, rows[m]], (c, c)) for m in ns]
    ghi = [x.astype(bf16) for x in grow]
    glo = [(grow[n] - ghi[n].astype(f32)).astype(bf16) for n in ns]
    gm = [jnp.dot(ghi[n], cumm, preferred_element_type=f32) + jnp.dot(glo[n], cumm, preferred_element_type=f32)
          for n in ns]
    gam_row = [x[:, :c] for x in gm]
    tot = [x[:, c:] for x in gm]
    gam_col = [x.T for x in gam_row]
    beta_col = [jnp.broadcast_to(b_ref[head[m], :, rows[m]], (c, c)).T for m in ns]
    decay = [incl * jnp.exp(jnp.minimum(gam_col[n] - gam_row[n], 0.0)) for n in ns]
    a = [strict * beta_col[n] * kk[n] * decay[n] for n in ns]
    a0 = [x * d8 for x in a]
    n2 = [mm(x, x) for x in a0]
    n4 = [mm(x, x) for x in n2]
    t = [mm(eye - a0[n], eye + n2[n]) for n in ns]
    t = [mm(t[n], eye + n4[n]) for n in ns]
    for j in range(mrg_ref.shape[0]):
        p = [mm(a[n] * mrg_ref[j], t[n]) for n in ns]
        t = [t[n] - mm(t[n], p[n]) for n in ns]
    beta128 = [rep(x) for x in beta_col]
    egam128 = [rep(jnp.exp(x)) for x in gam_col]
    solb = [mm(t[n], jnp.concatenate([k[n] * beta128[n] * egam128[n], v[n] * beta128[n]], axis=1)).astype(bf16)
            for n in ns]
    av = [jnp.dot((qk[n] * decay[n]).astype(bf16), solb[n], preferred_element_type=f32) for n in ns]
    k_dec = [(k[n] * rep(jnp.exp(tot[n] - gam_col[n]))).astype(bf16) for n in ns]
    wu = [lax.dot_general(k_dec[n], solb[n], contract_first, preferred_element_type=f32) for n in ns]
    for n in ns:
        qd_sc[n] = (q[n] * egam128[n] - av[n][:, :hd]).astype(bf16)
        o_ref[rows[n], cols[n]] = av[n][:, hd:]
        w_sc[n] = wu[n][:, :hd].astype(bf16)
        u_sc[n] = wu[n][:, hd:]
        dec_sc[n] = rep(jnp.exp(tot[n][0:SUBLANES, :]))

    def body(ci, states):
        ce = ci + direction * (nc - 1 - 2 * ci)
        new = []
        for h in range(nh):
            m = h * nc + ce
            sb = states[h].astype(bf16)
            sin_sc[m] = sb
            new.append(states[h] * dec_sc[m][0:1] - jnp.dot(w_sc[m], sb, preferred_element_type=f32) + u_sc[m])
        return tuple(new)

    states = lax.fori_loop(0, nc, body, tuple(s_sc[h] for h in range(nh)))
    for h in range(nh):
        s_sc[h] = states[h]

    for m in ns:
        o_ref[rows[m], cols[m]] += jnp.dot(qd_sc[m], sin_sc[m], preferred_element_type=f32)


def gdn_scan(qkv, gb):
    _, bsz, l, _ = qkv.shape
    hd = D_HEAD_DIM
    tt = min(GDN_TT, l)
    nt = l // tt
    nc = tt // CHUNK
    assert l % tt == 0 and tt % CHUNK == 0
    consts = gdn_constants()
    gb4 = gb.reshape(bsz, 4 * D_HEADS, 1, l)
    nh = GDN_HEADS_PER_STEP
    assert D_HEADS % nh == 0
    tidx = lambda d, i: i + d * (nt - 1 - 2 * i)
    qkv_spec = lambda p: pl.BlockSpec((None, None, tt, nh * hd), lambda b, h, d, i: (p, b, tidx(d, i), h))
    row_spec = lambda off: pl.BlockSpec((None, nh, 1, tt),
                                        lambda b, h, d, i: (b, (off + d * D_HEADS) // nh + h, 0, tidx(d, i)))
    per_dir = lambda a: pl.BlockSpec((None,) + a.shape[1:], lambda b, h, d, i: (d,) + (0,) * (a.ndim - 1))
    whole = lambda a: pl.BlockSpec(a.shape, lambda b, h, d, i: (0,) * a.ndim)
    return pl.pallas_call(
        functools.partial(_gdn_kernel, nc=nc),
        grid=(bsz, D_HEADS // nh, 2, nt),
        in_specs=[qkv_spec(0), qkv_spec(1), qkv_spec(2), row_spec(0), row_spec(2 * D_HEADS),
                  per_dir(consts[0]), per_dir(consts[1]), per_dir(consts[2]), whole(consts[3]), whole(consts[4])],
        out_specs=pl.BlockSpec((None, None, tt, nh * hd), lambda b, h, d, i: (d, b, tidx(d, i), h)),
        out_shape=jax.ShapeDtypeStruct((2, bsz, l, D_WIDTH), jnp.float32),
        scratch_shapes=[pltpu.VMEM((nh, hd, hd), jnp.float32),
                        pltpu.VMEM((nh * nc, CHUNK, hd), jnp.bfloat16),
                        pltpu.VMEM((nh * nc, SUBLANES, hd), jnp.float32),
                        pltpu.VMEM((nh * nc, hd, hd), jnp.bfloat16),
                        pltpu.VMEM((nh * nc, hd, hd), jnp.float32),
                        pltpu.VMEM((nh * nc, hd, hd), jnp.bfloat16)],
        compiler_params=pltpu.CompilerParams(dimension_semantics=("parallel", "parallel", "parallel", "arbitrary"),
                                             vmem_limit_bytes=VMEM_LIMIT_BYTES),
        name="gdn_scan",
    )(qkv, qkv, qkv, gb4, gb4, *consts)


def gated_deltanet(proj3, conv_w, a_log, dt_bias):
    return gdn_scan(gdn_prep(proj3, conv_w), gdn_gates(proj3, a_log, dt_bias))


MOE_TT = 512
MOE_SUB = 128
MOE_ALIGN = 64
MOE_TILES_PER_STEP = 4
MOE_ROWS = 256
MOE_SLAB = 256
MOE_VMEM_LIMIT_BYTES = 56 * 1024 * 1024
MOE_EXPERT_VMEM_LIMIT_BYTES = 60 * 1024 * 1024


def _router_kernel(x_ref, g_ref, wr_ref, h_ref, aff_ref):
    x = x_ref[...]
    h = (x * lax.rsqrt(jnp.mean(x * x, axis=-1, keepdims=True) + EPS) * g_ref[...]).astype(jnp.bfloat16)
    h_ref[...] = h
    logits = jnp.dot(h, wr_ref[...], preferred_element_type=jnp.float32)
    lane = lax.broadcasted_iota(jnp.int32, logits.shape, 1)
    logits = jnp.where(lane < N_EXPERTS, logits, -jnp.inf)
    p = jnp.exp(logits - jnp.max(logits, axis=-1, keepdims=True))
    aff = p / jnp.sum(p, axis=-1, keepdims=True)
    aff_ref[...] = aff.T[0:N_EXPERTS, :]


def moe_route(x, gain, w_router, *, tm=MOE_TT):
    bsz, l, d = x.shape
    wr = jnp.pad(w_router.astype(jnp.bfloat16), ((0, 0), (0, LANES - N_EXPERTS)))
    return pl.pallas_call(
        _router_kernel,
        grid=(bsz, l // tm),
        in_specs=[pl.BlockSpec((None, tm, d), lambda b, i: (b, i, 0)),
                  pl.BlockSpec((1, d), lambda b, i: (0, 0)),
                  pl.BlockSpec((d, LANES), lambda b, i: (0, 0))],
        out_specs=[pl.BlockSpec((None, tm, d), lambda b, i: (b, i, 0)),
                   pl.BlockSpec((None, N_EXPERTS, tm), lambda b, i: (b, 0, i))],
        out_shape=[jax.ShapeDtypeStruct((bsz, l, d), jnp.bfloat16),
                   jax.ShapeDtypeStruct((bsz, N_EXPERTS, l), jnp.float32)],
        compiler_params=pltpu.CompilerParams(dimension_semantics=("parallel", "parallel"),
                                             vmem_limit_bytes=VMEM_LIMIT_BYTES),
        name="moe_router",
    )(x, gain.reshape(1, d).astype(jnp.float32), wr)


def _select_kernel(aff_ref, pre_ref, smap_ref, gate_ref, cnt_ref, *, cap, tt):
    f32 = jnp.float32
    bf16 = jnp.bfloat16
    aff = aff_ref[...]
    e, l = aff.shape
    nl = l // LANES
    tiles = [slice(j * LANES, (j + 1) * LANES) for j in range(nl)]
    bits = pltpu.bitcast(aff, jnp.int32)
    bt = [bits[:, s] for s in tiles]

    def lane_total(x):
        return jnp.broadcast_to(jnp.sum(x, axis=-1, keepdims=True), (e, LANES))

    def search(i, thr):
        cand = thr | jnp.left_shift(jnp.int32(1), 30 - i)
        acc = jnp.zeros((e, LANES), jnp.int32)
        for x in bt:
            acc = acc + (x >= cand).astype(jnp.int32)
        return jnp.where(lane_total(acc) >= cap, cand, thr)

    thr = lax.fori_loop(0, 31, search, jnp.zeros((e, LANES), jnp.int32))
    gt = [x > thr for x in bt]
    eq = [x == thr for x in bt]
    acc = jnp.zeros((e, LANES), jnp.int32)
    for x in gt:
        acc = acc + x.astype(jnp.int32)
    need = (cap - lane_total(acc)).astype(f32)

    pre = pre_ref[...]

    def prefix(flags):
        outs = [jnp.dot(jnp.where(x, 1.0, 0.0).astype(bf16), pre, preferred_element_type=f32) for x in flags]
        carry = jnp.zeros((e, LANES), f32)
        res = []
        for o in outs:
            res.append(o[:, :LANES] + carry)
            carry = carry + o[:, LANES:]
        return res, [o[:, LANES:] for o in outs]

    rank_eq, _ = prefix(eq)
    sel = [jnp.logical_or(gt[j], jnp.logical_and(eq[j], rank_eq[j] < need)) for j in range(nl)]
    pos, totals = prefix(sel)
    lane = lax.broadcasted_iota(jnp.int32, (e, LANES), 1)
    cnt = jnp.zeros((e, LANES), f32)
    per = tt // LANES
    for j in range(nl):
        smap_ref[:, tiles[j]] = jnp.where(sel[j], pos[j], -1.0)
        gate_ref[:, tiles[j]] = jnp.where(sel[j], aff[:, tiles[j]], 0.0)
        cnt = cnt + jnp.where(lane == j // per, totals[j], 0.0)
    cnt_ref[...] = cnt


def moe_select(aff, cap, *, tt=MOE_TT):
    bsz, e, l = aff.shape
    assert l // tt <= LANES
    i = np.arange(LANES)
    pre = np.concatenate([i[:, None] < i[None, :], np.ones((LANES, LANES), bool)], axis=1)
    row = pl.BlockSpec((None, e, l), lambda b: (b, 0, 0))
    return pl.pallas_call(
        functools.partial(_select_kernel, cap=cap, tt=tt),
        grid=(bsz,),
        in_specs=[row, pl.BlockSpec((LANES, 2 * LANES), lambda b: (0, 0))],
        out_specs=[row, row, pl.BlockSpec((None, e, LANES), lambda b: (b, 0, 0))],
        out_shape=[jax.ShapeDtypeStruct((bsz, e, l), jnp.float32), jax.ShapeDtypeStruct((bsz, e, l), jnp.float32),
                   jax.ShapeDtypeStruct((bsz, e, LANES), jnp.float32)],
        compiler_params=pltpu.CompilerParams(dimension_semantics=("parallel",),
                                             vmem_limit_bytes=VMEM_LIMIT_BYTES),
        name="moe_select",
    )(aff, jnp.asarray(pre, jnp.bfloat16))


def _slot_one_hot(pos, base, rows, n):
    slot = (base + lax.broadcasted_iota(jnp.int32, (rows, n), 0)).astype(jnp.float32)
    return jnp.where(pos == slot, 1.0, 0.0).astype(jnp.bfloat16)


def _expert_kernel(cs_ref, h_ref, smap_ref, gate_ref, wg32_ref, wu32_ref, wd32_ref, o_ref,
                   xs_sc, gs_sc, wg_ref, wu_ref, wd_ref, *, nj, per, tt, cap):
    e = pl.program_id(0)
    b = pl.program_id(1)
    j = pl.program_id(2)
    f32 = jnp.float32
    bf16 = jnp.bfloat16

    @pl.when(jnp.logical_and(b == 0, j == 0))
    def _():
        wg_ref[...] = wg32_ref[...].astype(bf16)
        wu_ref[...] = wu32_ref[...].astype(bf16)
        wd_ref[...] = wd32_ref[...].astype(bf16)

    @pl.when(j == 0)
    def _():
        xs_sc[...] = jnp.zeros_like(xs_sc)
        gs_sc[...] = jnp.zeros_like(gs_sc)

    win = min(MOE_SUB, cap)
    for s in range(per):
        base = (b * N_EXPERTS + e) * (nj * per + 1) + j * per + s
        c0 = cs_ref[base]
        c1 = cs_ref[base + 1]
        cols = slice(s * tt, (s + 1) * tt)
        pos = smap_ref[:, cols]
        gate = gate_ref[:, cols]
        hb = h_ref[cols, :]

        def gather(r0, rows, pos=pos, gate=gate, hb=hb):
            oh = _slot_one_hot(pos, r0, rows, tt)
            xs_sc[pl.ds(r0, rows), :] += jnp.dot(oh, hb, preferred_element_type=f32)
            g = jnp.sum(oh.astype(f32) * gate, axis=-1, keepdims=True)
            gs_sc[pl.ds(r0, rows), :] += jnp.broadcast_to(g, (rows, LANES))

        w0 = pl.multiple_of(jnp.minimum(c0 // MOE_ALIGN * MOE_ALIGN, cap - win), MOE_ALIGN)
        gather(w0, win)

        def rest(st, carry, gather=gather):
            gather(pl.multiple_of(st * MOE_ALIGN, MOE_ALIGN), MOE_ALIGN)
            return carry

        lax.fori_loop((w0 + win) // MOE_ALIGN, (c1 + MOE_ALIGN - 1) // MOE_ALIGN, rest, 0)

    @pl.when(j == nj - 1)
    def _():
        rows_per = min(MOE_ROWS, cap)
        for r in range(cap // rows_per):
            rows = slice(r * rows_per, (r + 1) * rows_per)
            xb = xs_sc[rows, :].astype(bf16)
            g = jnp.dot(xb, wg_ref[...], preferred_element_type=f32)
            u = jnp.dot(xb, wu_ref[...], preferred_element_type=f32)
            hid = (g * jax.nn.sigmoid(g) * u).astype(bf16)
            out = jnp.dot(hid, wd_ref[...], preferred_element_type=f32)
            scale = jnp.concatenate([gs_sc[rows, :]] * (out.shape[1] // LANES), axis=1)
            o_ref[rows, :] = (out * scale).astype(bf16)


def moe_experts(hb, smap, gate, cs, w_gate, w_up, w_down, layer, cap, *, tt=MOE_TT):
    bsz, l, d = hb.shape
    _, e, _, ff = w_gate.shape
    per = MOE_TILES_PER_STEP if (l // tt) % MOE_TILES_PER_STEP == 0 else 1
    nj = l // (tt * per)
    smap4 = smap.reshape(bsz, e, 1, l)
    gate4 = gate.reshape(bsz, e, 1, l)
    tok = pl.BlockSpec((None, None, 1, per * tt), lambda ei, b, j, cs_ref: (b, ei, 0, j))
    once = pl.Buffered(1)
    grid_spec = pltpu.PrefetchScalarGridSpec(
        num_scalar_prefetch=1,
        grid=(e, bsz, nj),
        in_specs=[pl.BlockSpec((None, per * tt, d), lambda ei, b, j, cs_ref: (b, j, 0)), tok, tok,
                  pl.BlockSpec((None, None, d, ff), lambda ei, b, j, cs_ref: (layer, ei, 0, 0), pipeline_mode=once),
                  pl.BlockSpec((None, None, d, ff), lambda ei, b, j, cs_ref: (layer, ei, 0, 0), pipeline_mode=once),
                  pl.BlockSpec((None, None, ff, d), lambda ei, b, j, cs_ref: (layer, ei, 0, 0), pipeline_mode=once)],
        out_specs=pl.BlockSpec((None, None, cap, d), lambda ei, b, j, cs_ref: (b, ei, 0, 0)),
        scratch_shapes=[pltpu.VMEM((cap, d), jnp.float32), pltpu.VMEM((cap, LANES), jnp.float32),
                        pltpu.VMEM((d, ff), jnp.bfloat16), pltpu.VMEM((d, ff), jnp.bfloat16),
                        pltpu.VMEM((ff, d), jnp.bfloat16)])
    return pl.pallas_call(
        functools.partial(_expert_kernel, nj=nj, per=per, tt=tt, cap=cap),
        grid_spec=grid_spec,
        out_shape=jax.ShapeDtypeStruct((bsz, e, cap, d), jnp.bfloat16),
        compiler_params=pltpu.CompilerParams(dimension_semantics=("parallel", "arbitrary", "arbitrary"),
                                             vmem_limit_bytes=MOE_EXPERT_VMEM_LIMIT_BYTES),
        name="moe_experts",
    )(cs, hb, smap4, gate4, w_gate, w_up, w_down)


def _combine_kernel(cs_ref, x_ref, smap_ref, ow_ref, y_ref, *, nj, tt):
    b = pl.program_id(0)
    e = pl.program_id(2)

    @pl.when(e == 0)
    def _():
        y_ref[...] = x_ref[...]

    contract_first = (((0,), (0,)), ((), ()))
    cap = ow_ref.shape[0]
    win = min(2 * MOE_SUB, cap)
    base = (b * N_EXPERTS + e) * (nj + 1)
    cols = [slice(j * tt, (j + 1) * tt) for j in range(nj)]
    pos = [smap_ref[:, c] for c in cols]
    r0 = [pl.multiple_of(jnp.minimum(cs_ref[base + j] // MOE_SUB * MOE_SUB, cap - win), MOE_SUB) for j in range(nj)]
    oh = [_slot_one_hot(pos[j], r0[j], win, tt) for j in range(nj)]
    add = [lax.dot_general(oh[j], ow_ref[pl.ds(r0[j], win), :], contract_first, preferred_element_type=jnp.float32)
           for j in range(nj)]
    for j in range(nj):
        y_ref[cols[j], :] += add[j]

    for j in range(nj):
        def scatter(st, carry):
            s0 = pl.multiple_of(st * MOE_SUB, MOE_SUB)
            y_ref[cols[j], :] += lax.dot_general(_slot_one_hot(pos[j], s0, MOE_SUB, tt),
                                                 ow_ref[pl.ds(s0, MOE_SUB), :], contract_first,
                                                 preferred_element_type=jnp.float32)
            return carry

        lax.fori_loop((r0[j] + win) // MOE_SUB, (cs_ref[base + j + 1] + MOE_SUB - 1) // MOE_SUB, scatter, 0)


def moe_combine(x, smap, outw, cs, *, tt=MOE_TT):
    bsz, l, d = x.shape
    e, cap = outw.shape[1:3]
    nj = l // tt
    smap4 = smap.reshape(bsz, e, 1, l)
    grid_spec = pltpu.PrefetchScalarGridSpec(
        num_scalar_prefetch=1,
        grid=(bsz, d // MOE_SLAB, e),
        in_specs=[pl.BlockSpec((None, l, MOE_SLAB), lambda b, s, ei, cs_ref: (b, 0, s)),
                  pl.BlockSpec((None, None, 1, l), lambda b, s, ei, cs_ref: (b, ei, 0, 0)),
                  pl.BlockSpec((None, None, cap, MOE_SLAB), lambda b, s, ei, cs_ref: (b, ei, 0, s))],
        out_specs=pl.BlockSpec((None, l, MOE_SLAB), lambda b, s, ei, cs_ref: (b, 0, s)))
    return pl.pallas_call(
        functools.partial(_combine_kernel, nj=nj, tt=tt),
        grid_spec=grid_spec,
        out_shape=jax.ShapeDtypeStruct((bsz, l, d), jnp.float32),
        compiler_params=pltpu.CompilerParams(dimension_semantics=("parallel", "parallel", "arbitrary"),
                                             vmem_limit_bytes=MOE_VMEM_LIMIT_BYTES),
        name="moe_combine",
    )(cs, x, smap4, outw)


def ec_moe_layer(x, gain, w_router, w_gate, w_up, w_down, layer):
    bsz, l, d = x.shape
    cap = EC_CAPACITY_FACTOR * l // N_EXPERTS
    tt = min(MOE_TT, l)
    nj = l // tt
    hb, aff = moe_route(x, gain, w_router, tm=tt)
    smap, gate, cnt = moe_select(aff, cap, tt=tt)
    cs = jnp.concatenate([jnp.zeros((bsz, N_EXPERTS, 1), jnp.float32), jnp.cumsum(cnt[..., :nj], axis=-1)], axis=-1)
    cs = cs.astype(jnp.int32).reshape(-1)
    outw = moe_experts(hb, smap, gate, cs, w_gate, w_up, w_down, layer, cap, tt=tt)
    return moe_combine(x, smap, outw, cs, tt=tt)


def kernel(x, mix_norm, ffn_norm, ev_w_in, ev_w_out, a_lb_logits, a_out_norm, s5_lambda_re, s5_lambda_im, s5_log_step, s5_b_re, s5_b_im, s5_c_re, s5_c_im, s5_d, s5_glu_w, s5_glu_b, od_w_in, od_w_out, c_q_norm, c_k_norm, c_lambda, c_out_norm, rel_bias, d_conv_w, d_a_log, d_dt_bias, d_out_norm, moe_router, moe_w_gate, moe_w_up, moe_w_down):
    bsz, l, d = x.shape
    p = jax.nn.softmax(a_lb_logits.astype(jnp.float32), axis=0)
    cum = jnp.cumsum(p, axis=0)
    lower_bounds = cum - cum[0:1]
    bias5 = rel_bias_tiles(rel_bias, ATT_T)
    for layer in range(DEPTH):
        j = layer // 2
        if layer % 2 == 0:
            proj, u_tb = norm_matmul(x, mix_norm[layer], ev_w_in[j], tail=B_WIDTH)
            o_a2 = hgrn2_scan(proj, lower_bounds[j])
            o_b = s5_mixer_tb(u_tb.reshape(l * bsz, B_WIDTH), bsz, s5_lambda_re[j], s5_lambda_im[j], s5_log_step[j],
                              s5_b_re[j], s5_b_im[j], s5_c_re[j], s5_c_im[j], s5_d[j], s5_glu_w[j], s5_glu_b[j])
            x = mixer_out_proj(o_a2, proj, 4 * A_HEADS, a_out_norm[j], o_b.reshape(l, bsz * B_WIDTH), ev_w_out[j], x,
                               bidir_first=True, other_time_major=True)
        else:
            o2 = 3 * C_WIDTH + 3 * D_WIDTH
            o4 = o2 + 4 * D_HEADS
            w = od_w_in[j]
            w_in = jnp.concatenate([w[:, :o2], w[:, o4:], w[:, o2:o4],
                                    jnp.zeros((d, 3 * C_WIDTH + OD_COLS - w.shape[1]), w.dtype)], axis=1)
            proj, q2, kt, vb, stats = odd_in_proj(x, mix_norm[layer], w_in, c_q_norm[j], c_k_norm[j])
            o_c = diff_attention(q2, kt, vb, stats, c_lambda[j], c_out_norm[j], bias5, layer)
            o_d2 = gated_deltanet(proj, d_conv_w[j], d_a_log[j], d_dt_bias[j])
            x = mixer_out_proj(o_d2, proj, OD_GATE_BLOCK, d_out_norm[j], o_c, od_w_out[j], x, bidir_first=False)
        x = ec_moe_layer(x, ffn_norm[layer], moe_router[layer], moe_w_gate, moe_w_up, moe_w_down, layer)
    return x
```

```python
import functools
import math

import jax
import jax.numpy as jnp
import numpy as np
from jax import lax
from jax.experimental import pallas as pl
from jax.experimental.pallas import tpu as pltpu

D_MODEL = 1024
DEPTH = 4
MIX_WIDTH = D_MODEL
A_WIDTH = MIX_WIDTH // 2
A_HEAD_DIM = 128
A_HEADS = A_WIDTH // A_HEAD_DIM
B_WIDTH = MIX_WIDTH - A_WIDTH
S5_GROUP = 16
S5_GROUPS = B_WIDTH // S5_GROUP
S5_STATE = 64
C_WIDTH = MIX_WIDTH // 2
C_HEAD_DIM = 64
C_HEADS = C_WIDTH // (2 * C_HEAD_DIM)
D_WIDTH = MIX_WIDTH - C_WIDTH
D_HEAD_DIM = 128
D_HEADS = D_WIDTH // D_HEAD_DIM
CONV_WIDTH = 5
N_EXPERTS = 16
EC_CAPACITY_FACTOR = 2
REL_BUCKETS = 32
REL_MAX_DIST = 128
CHUNK = 64
EPS = 1e-6

LANES = 128
SUBLANES = 8
VMEM_LIMIT_BYTES = 48 * 1024 * 1024


PROJ_TM = 512
PROJ_COLS = 512


def _norm_matmul_kernel(x_ref, g_ref, w_ref, o_ref, *tail_ref, main):
    x = x_ref[...]
    y = (x * lax.rsqrt(jnp.mean(x * x, axis=-1, keepdims=True) + EPS) * g_ref[...]).astype(jnp.bfloat16)
    for c0 in range(0, main, PROJ_COLS):
        c1 = min(c0 + PROJ_COLS, main)
        o_ref[:, c0:c1] = jnp.dot(y, w_ref[:, c0:c1], preferred_element_type=jnp.float32)
    if tail_ref:
        tail_ref[0][...] = jnp.dot(y, w_ref[:, main:], preferred_element_type=jnp.float32)


def norm_matmul(x, gain, w, *, tail=0, tm=PROJ_TM):
    bsz, l, k = x.shape
    m = w.shape[1]
    main = m - tail
    tm = min(tm, l)
    out_shape = [jax.ShapeDtypeStruct((bsz, l, main), jnp.float32)]
    out_specs = [pl.BlockSpec((None, tm, main), lambda b, i: (b, i, 0))]
    if tail:
        out_shape.append(jax.ShapeDtypeStruct((l, bsz * tail), jnp.float32))
        out_specs.append(pl.BlockSpec((tm, tail), lambda b, i: (i, b)))
    outs = pl.pallas_call(
        functools.partial(_norm_matmul_kernel, main=main),
        grid=(bsz, l // tm),
        in_specs=[pl.BlockSpec((None, tm, k), lambda b, i: (b, i, 0)),
                  pl.BlockSpec((1, k), lambda b, i: (0, 0)),
                  pl.BlockSpec((k, m), lambda b, i: (0, 0), pipeline_mode=pl.Buffered(1))],
        out_specs=out_specs,
        out_shape=out_shape,
        compiler_params=pltpu.CompilerParams(dimension_semantics=("parallel", "parallel"),
                                             vmem_limit_bytes=VMEM_LIMIT_BYTES),
        name="norm_matmul",
    )(x, gain.reshape(1, k).astype(jnp.float32), w.astype(jnp.bfloat16))
    return outs if tail else outs[0]


def _mixer_out_kernel(of_ref, ob_ref, g_ref, gain_ref, other_ref, wb_ref, wo_ref, r_ref, o_ref):
    bf16 = jnp.bfloat16
    o = of_ref[...] + ob_ref[...]
    g = g_ref[...]
    gate = g * jax.nn.sigmoid(g)
    hd = gain_ref.shape[1]
    heads = []
    for h in range(o.shape[1] // hd):
        oh = o[:, h * hd:(h + 1) * hd]
        heads.append(oh * lax.rsqrt(jnp.mean(oh * oh, axis=-1, keepdims=True) + EPS) * gain_ref[...])
    y = (jnp.concatenate(heads, axis=1) * gate).astype(bf16)
    o_ref[...] = (r_ref[...] + jnp.dot(y, wb_ref[...], preferred_element_type=jnp.float32)
                  + jnp.dot(other_ref[...].astype(bf16), wo_ref[...], preferred_element_type=jnp.float32))


def mixer_out_proj(o2, proj3, gate_block, out_gain, other, w, res, *, bidir_first, other_time_major=False,
                   tm=PROJ_TM):
    _, bsz, l, k = o2.shape
    m = w.shape[1]
    tm = min(tm, l)
    wb = w.astype(jnp.bfloat16)
    w_bidir, w_other = (wb[:k], wb[k:]) if bidir_first else (wb[k:], wb[:k])
    other_spec = (pl.BlockSpec((tm, k), lambda b, i: (i, b)) if other_time_major
                  else pl.BlockSpec((None, tm, k), lambda b, i: (b, i, 0)))
    gb = gate_block * LANES // k
    row = pl.BlockSpec((None, tm, m), lambda b, i: (b, i, 0))
    wspec = pl.BlockSpec((k, m), lambda b, i: (0, 0))
    return pl.pallas_call(
        _mixer_out_kernel,
        grid=(bsz, l // tm),
        in_specs=[pl.BlockSpec((None, None, tm, k), lambda b, i: (0, b, i, 0)),
                  pl.BlockSpec((None, None, tm, k), lambda b, i: (1, b, i, 0)),
                  pl.BlockSpec((None, tm, k), lambda b, i: (b, i, gb)),
                  pl.BlockSpec((1, LANES), lambda b, i: (0, 0)),
                  other_spec, wspec, wspec, row],
        out_specs=row,
        out_shape=jax.ShapeDtypeStruct((bsz, l, m), jnp.float32),
        compiler_params=pltpu.CompilerParams(dimension_semantics=("parallel", "parallel"),
                                             vmem_limit_bytes=VMEM_LIMIT_BYTES),
        name="mixer_out_proj",
    )(o2, o2, proj3, out_gain.reshape(1, LANES).astype(jnp.float32), other, w_bidir, w_other, res)


HG_LEVELS = tuple(CHUNK >> (i + 1) for i in range(CHUNK.bit_length() - 1))
HG_TOT_ROWS = 8
HG_TT = 1024


def hgrn2_constants():
    c = CHUNK
    r = np.arange(c)[:, None]
    u = np.arange(c)[None, :]
    stacks, masks = [], []
    for direction in range(2):
        fwd = direction == 0
        lvl_masks = []
        for m in HG_LEVELS:
            blk = r // (2 * m)
            later = (r % (2 * m)) >= m
            lvl_masks.append((blk == blk.T) & (later & ~later.T if fwd else ~later & later.T))
        stacks.append(np.concatenate([(u <= r) if fwd else (u >= r), np.ones((HG_TOT_ROWS, c), bool)], axis=0))
        masks.append(np.stack(lvl_masks))
    return (jnp.asarray(np.stack(stacks), jnp.bfloat16), jnp.asarray(np.stack(masks), jnp.float32))


def _hgrn2_kernel(q_ref, f_ref, v_ref, loglb_ref, log1mlb_ref, onemlb_ref, ast_ref, mask_ref, o_ref,
                  st_sc, qd_sc, dec_sc, upd_sc, sin_sc, *, nc):
    direction = pl.program_id(2)

    @pl.when(pl.program_id(3) == 0)
    def _():
        st_sc[...] = jnp.zeros_like(st_sc)

    bf16 = jnp.bfloat16
    f32 = jnp.float32
    c = CHUNK
    hd = A_HEAD_DIM
    dirf = direction.astype(f32)
    loglb = loglb_ref[...]
    log1mlb = log1mlb_ref[...]
    onemlb = onemlb_ref[...]
    ast = ast_ref[...]
    contract_last = (((1,), (1,)), ((), ()))
    contract_first = (((0,), (0,)), ((), ()))

    ns = range(nc)
    rows = [slice(n * c, (n + 1) * c) for n in ns]
    z = [f_ref[r, :] for r in rows]
    v = [v_ref[r, :] for r in rows]
    qr = [q_ref[r, :] for r in rows]
    q = [x * jax.nn.sigmoid(x) for x in qr]
    e = [jnp.exp(-jnp.abs(x)) for x in z]
    cc = [log1mlb + jnp.minimum(z[n], 0.0) - jnp.log1p(e[n]) for n in ns]
    lf = [jnp.maximum(loglb, x) + jnp.log1p(jnp.exp(-jnp.abs(loglb - x))) for x in cc]
    k = [onemlb * jnp.where(z[n] >= 0, e[n], 1.0) / (1.0 + e[n]) for n in ns]
    hi = [x.astype(bf16) for x in lf]
    lo = [(lf[n] - hi[n].astype(f32)).astype(bf16) for n in ns]
    d = [jnp.dot(ast, hi[n], preferred_element_type=f32) + jnp.dot(ast, lo[n], preferred_element_type=f32)
         for n in ns]
    cum = [x[0:c] for x in d]
    tot = [x[c:c + HG_TOT_ROWS] for x in d]
    ref = [cum[n] - dirf * lf[n] for n in ns]
    attn = [jnp.zeros((c, c), f32) for _ in ns]
    for li, m in enumerate(HG_LEVELS):
        nb = c // (2 * m)
        split = [jnp.broadcast_to(x.reshape(nb, 2 * m, hd)[:, m - 1:m, :], (nb, 2 * m, hd)).reshape(c, hd)
                 for x in ref]
        x = [jnp.exp(-jnp.abs(cum[n] - split[n])) for n in ns]
        s = [lax.dot_general((q[n] * x[n]).astype(bf16), (k[n] * x[n]).astype(bf16), contract_last,
                             preferred_element_type=f32) for n in ns]
        attn = [attn[n] + mask_ref[li] * s[n] for n in ns]
    vb = [x.astype(bf16) for x in v]
    intra = [jnp.dot(attn[n].astype(bf16), vb[n], preferred_element_type=f32) for n in ns]
    upd = [lax.dot_general(vb[n], (k[n] * jnp.exp(tot[n][0:1] - cum[n])).astype(bf16), contract_first,
                           preferred_element_type=f32) for n in ns]
    for n in ns:
        o_ref[rows[n], :] = intra[n] + jnp.sum(q[n] * k[n], axis=-1, keepdims=True) * v[n]
        qd_sc[n] = (q[n] * jnp.exp(cum[n])).astype(bf16)
        dec_sc[n] = jnp.exp(tot[n])
        upd_sc[n] = upd[n]

    def body(ci, st):
        ce = ci + direction * (nc - 1 - 2 * ci)
        sin_sc[ce] = st.astype(bf16)
        return st * dec_sc[ce][0:1] + upd_sc[ce]

    st_sc[...] = lax.fori_loop(0, nc, body, st_sc[...])

    for n in range(nc):
        rows = slice(n * c, (n + 1) * c)
        o_ref[rows, :] += lax.dot_general(qd_sc[n], sin_sc[n], contract_last, preferred_element_type=f32)


def hgrn2_scan(proj3, lb):
    bsz, l, _ = proj3.shape
    hd = A_HEAD_DIM
    tt = min(HG_TT, l)
    nt = l // tt
    assert l % tt == 0 and tt % CHUNK == 0
    ast, masks = hgrn2_constants()
    lb = lb.astype(jnp.float32)
    vecs = [jnp.log(lb).reshape(2, 1, A_WIDTH), jnp.log1p(-lb).reshape(2, 1, A_WIDTH), (1.0 - lb).reshape(2, 1, A_WIDTH)]
    tidx = lambda d, i: i + d * (nt - 1 - 2 * i)
    vec = pl.BlockSpec((None, 1, hd), lambda b, h, d, i: (d, 0, h))
    return pl.pallas_call(
        functools.partial(_hgrn2_kernel, nc=tt // CHUNK),
        grid=(bsz, A_HEADS, 2, nt),
        in_specs=[pl.BlockSpec((None, tt, hd), lambda b, h, d, i: (b, tidx(d, i), h)),
                  pl.BlockSpec((None, tt, hd), lambda b, h, d, i: (b, tidx(d, i), (1 + d) * A_HEADS + h)),
                  pl.BlockSpec((None, tt, hd), lambda b, h, d, i: (b, tidx(d, i), 3 * A_HEADS + h)),
                  vec, vec, vec,
                  pl.BlockSpec((None,) + ast.shape[1:], lambda b, h, d, i: (d, 0, 0)),
                  pl.BlockSpec((None,) + masks.shape[1:], lambda b, h, d, i: (d, 0, 0, 0))],
        out_specs=pl.BlockSpec((None, None, tt, hd), lambda b, h, d, i: (d, b, tidx(d, i), h)),
        out_shape=jax.ShapeDtypeStruct((2, bsz, l, A_WIDTH), jnp.float32),
        scratch_shapes=[pltpu.VMEM((hd, hd), jnp.float32),
                        pltpu.VMEM((tt // CHUNK, CHUNK, hd), jnp.bfloat16),
                        pltpu.VMEM((tt // CHUNK, HG_TOT_ROWS, hd), jnp.float32),
                        pltpu.VMEM((tt // CHUNK, hd, hd), jnp.float32),
                        pltpu.VMEM((tt // CHUNK, hd, hd), jnp.bfloat16)],
        compiler_params=pltpu.CompilerParams(dimension_semantics=("parallel", "parallel", "parallel", "arbitrary"),
                                             vmem_limit_bytes=VMEM_LIMIT_BYTES),
        name="hgrn2_scan",
    )(proj3, proj3, proj3, *vecs, ast, masks)


S5_NS = S5_GROUPS * S5_STATE
S5_TT = 128


def _s5_scan_kernel(u_ref, win_ref, ar_ref, ai_ref, wout_ref, y_ref, bu_sc, xs_sc, st_sc, *, bsz, tt, reverse):
    @pl.when(pl.program_id(0) == 0)
    def _():
        st_sc[...] = jnp.zeros_like(st_sc)

    ub = u_ref[...].astype(jnp.bfloat16)
    halves = 2
    uw = B_WIDTH // halves
    sw = S5_NS // halves
    for hf in range(halves):
        for part in range(2):
            sc = slice(part * S5_NS + hf * sw, part * S5_NS + (hf + 1) * sw)
            bu_sc[:, sc] = jnp.dot(ub[:, hf * uw:(hf + 1) * uw], win_ref[hf * uw:(hf + 1) * uw, sc],
                                   preferred_element_type=jnp.float32)
    ar = jnp.broadcast_to(ar_ref[...], (bsz, S5_NS))
    ai = jnp.broadcast_to(ai_ref[...], (bsz, S5_NS))
    per = SUBLANES // bsz
    ngroups = tt // per

    def body(s, carry):
        xr, xi = carry
        p = (ngroups - 1 - s) if reverse else s
        base = pl.multiple_of(p * SUBLANES, SUBLANES)
        blk = bu_sc[pl.ds(base, SUBLANES), :]
        outs_r = [None] * per
        outs_i = [None] * per
        for ph in (range(per - 1, -1, -1) if reverse else range(per)):
            br = blk[ph * bsz:(ph + 1) * bsz, :S5_NS]
            bi = blk[ph * bsz:(ph + 1) * bsz, S5_NS:]
            xr, xi = ar * xr - ai * xi + br, ar * xi + ai * xr + bi
            outs_r[ph] = xr
            outs_i[ph] = xi
        xs_sc[pl.ds(base, SUBLANES), :S5_NS] = jnp.concatenate(outs_r, axis=0)
        xs_sc[pl.ds(base, SUBLANES), S5_NS:] = jnp.concatenate(outs_i, axis=0)
        return xr, xi

    xr, xi = lax.fori_loop(0, ngroups, body, (st_sc[0], st_sc[1]))
    st_sc[0] = xr
    st_sc[1] = xi
    for hf in range(halves):
        yc = slice(hf * uw, (hf + 1) * uw)
        acc = None
        for part in range(2):
            sc = slice(part * S5_NS + hf * sw, part * S5_NS + (hf + 1) * sw)
            term = jnp.dot(xs_sc[:, sc].astype(jnp.bfloat16), wout_ref[sc, yc], preferred_element_type=jnp.float32)
            acc = term if acc is None else acc + term
        y_ref[:, yc] = acc


def s5_scan(u_tb, win, ar, ai, wout, *, bsz, reverse):
    n = u_tb.shape[0]
    rows = S5_TT * bsz
    nt = n // rows
    assert n % rows == 0 and SUBLANES % bsz == 0
    idx = (lambda i: (nt - 1 - i, 0)) if reverse else (lambda i: (i, 0))
    const = lambda i: (0, 0)
    return pl.pallas_call(
        functools.partial(_s5_scan_kernel, bsz=bsz, tt=S5_TT, reverse=reverse),
        grid=(nt,),
        in_specs=[pl.BlockSpec((rows, B_WIDTH), idx),
                  pl.BlockSpec((B_WIDTH, 2 * S5_NS), const),
                  pl.BlockSpec((1, S5_NS), const),
                  pl.BlockSpec((1, S5_NS), const),
                  pl.BlockSpec((2 * S5_NS, B_WIDTH), const)],
        out_specs=pl.BlockSpec((rows, B_WIDTH), idx),
        out_shape=jax.ShapeDtypeStruct((n, B_WIDTH), jnp.float32),
        scratch_shapes=[pltpu.VMEM((rows, 2 * S5_NS), jnp.float32),
                        pltpu.VMEM((rows, 2 * S5_NS), jnp.float32),
                        pltpu.VMEM((2, bsz, S5_NS), jnp.float32)],
        compiler_params=pltpu.CompilerParams(dimension_semantics=("arbitrary",),
                                             vmem_limit_bytes=VMEM_LIMIT_BYTES),
        name="s5_scan_bwd" if reverse else "s5_scan_fwd",
    )(u_tb, win, ar, ai, wout)


def _s5_final_kernel(u_ref, yf_ref, yb_ref, d_ref, w_ref, b_ref, o_ref):
    y = d_ref[...] * u_ref[...] + yf_ref[...] + yb_ref[...]
    y = jax.nn.gelu(y)
    z = jnp.dot(y.astype(jnp.bfloat16), w_ref[...], preferred_element_type=jnp.float32) + b_ref[...]
    o_ref[...] = y * jax.nn.sigmoid(z)


def s5_finalize(u, yf, yb, d_skip, glu_w, glu_b, *, tm=512):
    n, w = u.shape
    row = pl.BlockSpec((tm, w), lambda i: (i, 0))
    vec = pl.BlockSpec((1, w), lambda i: (0, 0))
    return pl.pallas_call(
        _s5_final_kernel,
        grid=(n // tm,),
        in_specs=[row, row, row, vec, pl.BlockSpec((w, w), lambda i: (0, 0)), vec],
        out_specs=row,
        out_shape=jax.ShapeDtypeStruct((n, w), jnp.float32),
        compiler_params=pltpu.CompilerParams(dimension_semantics=("parallel",),
                                             vmem_limit_bytes=VMEM_LIMIT_BYTES),
        name="s5_finalize",
    )(u, yf, yb, d_skip.reshape(1, w).astype(jnp.float32), glu_w.astype(jnp.bfloat16),
      glu_b.reshape(1, w).astype(jnp.float32))


def s5_direction_params(lam_re, lam_im, log_step, b_re, b_im, c_re, c_im):
    step = jnp.exp(log_step)[:, None]
    mag = jnp.exp(lam_re * step)
    abar_re = mag * jnp.cos(lam_im * step)
    abar_im = mag * jnp.sin(lam_im * step)
    den = lam_re * lam_re + lam_im * lam_im
    fr = ((abar_re - 1.0) * lam_re + abar_im * lam_im) / den
    fi = (abar_im * lam_re - (abar_re - 1.0) * lam_im) / den
    bb_re = fr[..., None] * b_re - fi[..., None] * b_im
    bb_im = fr[..., None] * b_im + fi[..., None] * b_re
    eye = jnp.eye(S5_GROUPS, dtype=jnp.float32)
    win = jnp.concatenate([jnp.einsum('gnp,gh->gphn', bb, eye).reshape(B_WIDTH, S5_NS) for bb in (bb_re, bb_im)],
                          axis=1)
    wout = jnp.concatenate([jnp.einsum('gpn,gh->hngp', c, eye).reshape(S5_NS, B_WIDTH) for c in (c_re, -c_im)],
                           axis=0)
    return (win.astype(jnp.bfloat16), abar_re.reshape(1, S5_NS), abar_im.reshape(1, S5_NS),
            wout.astype(jnp.bfloat16))


def s5_mixer_tb(u_tb, bsz, lam_re, lam_im, log_step, b_re, b_im, c_re, c_im, d_skip, glu_w, glu_b):
    f32 = jnp.float32
    ys = []
    for direction in range(2):
        prm = s5_direction_params(lam_re[direction].astype(f32), lam_im[direction].astype(f32),
                                  log_step[direction].astype(f32), b_re[direction].astype(f32),
                                  b_im[direction].astype(f32), c_re[direction].astype(f32),
                                  c_im[direction].astype(f32))
        ys.append(s5_scan(u_tb, *prm, bsz=bsz, reverse=(direction == 1)))
    return s5_finalize(u_tb, ys[0], ys[1], d_skip, glu_w, glu_b)


def t5_bucket(rel):
    half = REL_BUCKETS // 2
    max_exact = half // 2
    base = jnp.where(rel > 0, half, 0)
    n = jnp.abs(rel)
    nf = jnp.maximum(n, 1).astype(jnp.float32)
    large = max_exact + (jnp.log(nf / max_exact) / math.log(REL_MAX_DIST / max_exact)
                         * (half - max_exact)).astype(jnp.int32)
    large = jnp.minimum(large, half - 1)
    return base + jnp.where(n < max_exact, n, large)


ATT_T = 512
LOG2E = math.log2(math.e)


def rel_bias_tiles(rel_bias, t):
    assert t >= REL_MAX_DIST
    table = rel_bias.astype(jnp.float32) * LOG2E
    tiles = []
    for d in (-1, 0, 1):
        c = table[t5_bucket(d * t + jnp.arange(-(t - 1), t))]
        w = jnp.concatenate([c, c[:1]], axis=0)
        m = jnp.tile(w, (t, 1))[:t * (2 * t - 1)].reshape(t, 2 * t - 1, -1)
        tiles.append(m[:, t - 1:2 * t - 1])
    far_neg = jnp.broadcast_to(table[t5_bucket(jnp.array(-2 * t))], tiles[0].shape)
    far_pos = jnp.broadcast_to(table[t5_bucket(jnp.array(2 * t))], tiles[0].shape)
    out = jnp.stack([far_neg] + tiles + [far_pos], axis=0)
    return jnp.transpose(out, (3, 0, 1, 2))


def _attn_operands(q, k, v, qg, kg):
    f32 = jnp.float32
    bf16 = jnp.bfloat16
    lane = lax.broadcasted_iota(jnp.int32, q.shape, 1)
    lo = lane < C_HEAD_DIM

    def half_sums(sq):
        return (jnp.sum(jnp.where(lo, sq, 0.0), axis=-1, keepdims=True),
                jnp.sum(jnp.where(lo, 0.0, sq), axis=-1, keepdims=True))

    def halfnorm(x, g):
        s_lo, s_hi = half_sums(x * x)
        return x * lax.rsqrt(jnp.where(lo, s_lo, s_hi) * (1.0 / C_HEAD_DIM) + EPS) * g

    def max_sq_norms(xb):
        n_lo, n_hi = half_sums(xb.astype(f32) * xb.astype(f32))
        return jnp.max(n_lo, axis=0, keepdims=True), jnp.max(n_hi, axis=0, keepdims=True)

    qn = halfnorm(q, qg) * (C_HEAD_DIM ** -0.5 * LOG2E)
    kn = halfnorm(k, kg)
    qb = qn.astype(bf16)
    kb = kn.astype(bf16)
    q2 = (jnp.where(lo, qb, 0.0).astype(bf16), jnp.where(lo, 0.0, qb).astype(bf16))
    q_lo, q_hi = max_sq_norms(qb)
    k_lo, k_hi = max_sq_norms(kb)
    sub = lax.broadcasted_iota(jnp.int32, (SUBLANES, q.shape[1]), 0)
    stats = jnp.where(sub == 0, q_lo, jnp.where(sub == 1, q_hi, jnp.where(sub == 2, k_lo,
                      jnp.where(sub == 3, k_hi, 0.0))))
    return q2, kn.T.astype(bf16), v.astype(bf16), stats


def _odd_proj_kernel(x_ref, g_ref, w_ref, qg_ref, kg_ref, o_ref, q2_ref, kt_ref, vb_ref, st_ref):
    x = x_ref[...]
    y = (x * lax.rsqrt(jnp.mean(x * x, axis=-1, keepdims=True) + EPS) * g_ref[...]).astype(jnp.bfloat16)
    hw = 2 * C_HEAD_DIM
    q, k, v = (jnp.dot(y, w_ref[:, p * C_WIDTH:(p + 1) * C_WIDTH], preferred_element_type=jnp.float32)
               for p in range(3))
    for h in range(C_HEADS):
        cols = slice(h * hw, (h + 1) * hw)
        q2, kt, vb, stats = _attn_operands(q[:, cols], k[:, cols], v[:, cols], qg_ref[...], kg_ref[...])
        q2_ref[h, 0] = q2[0]
        q2_ref[h, 1] = q2[1]
        kt_ref[h] = kt
        vb_ref[h] = vb
        st_ref[h] = stats
    att = 3 * C_WIDTH
    for c0 in range(att, w_ref.shape[1], PROJ_COLS):
        c1 = min(c0 + PROJ_COLS, w_ref.shape[1])
        o_ref[:, c0 - att:c1 - att] = jnp.dot(y, w_ref[:, c0:c1], preferred_element_type=jnp.float32)


def odd_in_proj(x, gain, w, q_gain, k_gain, *, tm=PROJ_TM):
    bsz, l, kdim = x.shape
    m = w.shape[1]
    rest = m - 3 * C_WIDTH
    tm = min(tm, l)
    hw = 2 * C_HEAD_DIM
    gq = jnp.tile(q_gain.astype(jnp.float32), 2).reshape(1, hw)
    gk = jnp.tile(k_gain.astype(jnp.float32), 2).reshape(1, hw)
    vec = pl.BlockSpec((1, hw), lambda b, i: (0, 0))
    return pl.pallas_call(
        _odd_proj_kernel,
        grid=(bsz, l // tm),
        in_specs=[pl.BlockSpec((None, tm, kdim), lambda b, i: (b, i, 0)),
                  pl.BlockSpec((1, kdim), lambda b, i: (0, 0)),
                  pl.BlockSpec((kdim, m), lambda b, i: (0, 0), pipeline_mode=pl.Buffered(1)),
                  vec, vec],
        out_specs=[pl.BlockSpec((None, tm, rest), lambda b, i: (b, i, 0)),
                   pl.BlockSpec((None, C_HEADS, 2, tm, hw), lambda b, i: (b, 0, 0, i, 0)),
                   pl.BlockSpec((None, C_HEADS, hw, tm), lambda b, i: (b, 0, 0, i)),
                   pl.BlockSpec((None, C_HEADS, tm, hw), lambda b, i: (b, 0, i, 0)),
                   pl.BlockSpec((None, C_HEADS, None, SUBLANES, hw), lambda b, i: (b, 0, i, 0, 0))],
        out_shape=[jax.ShapeDtypeStruct((bsz, l, rest), jnp.float32),
                   jax.ShapeDtypeStruct((bsz, C_HEADS, 2, l, hw), jnp.bfloat16),
                   jax.ShapeDtypeStruct((bsz, C_HEADS, hw, l), jnp.bfloat16),
                   jax.ShapeDtypeStruct((bsz, C_HEADS, l, hw), jnp.bfloat16),
                   jax.ShapeDtypeStruct((bsz, C_HEADS, l // tm, SUBLANES, hw), jnp.float32)],
        compiler_params=pltpu.CompilerParams(dimension_semantics=("parallel", "parallel"),
                                             vmem_limit_bytes=VMEM_LIMIT_BYTES),
        name="odd_in_proj",
    )(x, gain.reshape(1, kdim).astype(jnp.float32), w.astype(jnp.bfloat16), gq, gk)


ATT_ROWS = 64
ATT_KEY_TILES = 4
ATT_SAFE_GAP = 100.0


def _attn_kernel(lam_ref, kmax_ref, bmax_ref, q2_ref, kt_ref, v_ref, bias_ref, g_ref, o_ref,
                 m_sc, l_sc, acc_sc, s_sc, p_sc, a_sc, *, t, nk, ktiles, out_scale, bounded):
    f32 = jnp.float32
    b = pl.program_id(0)
    h = pl.program_id(1)
    qi = pl.program_id(2)
    q2 = q2_ref[...].reshape(2 * t, 2 * C_HEAD_DIM)
    r = ATT_ROWS
    hw = 2 * C_HEAD_DIM
    if bounded:
        q2f = q2.astype(f32)
        nq = jnp.sqrt(jnp.sum(q2f * q2f, axis=-1, keepdims=True))
        row = lax.broadcasted_iota(jnp.int32, nq.shape, 0)
        kc = jnp.where(row < t, kmax_ref[(b * C_HEADS + h) * 2], kmax_ref[(b * C_HEADS + h) * 2 + 1])
        m_sc[...] = jnp.broadcast_to(nq * kc + bmax_ref[h], m_sc.shape)
    else:
        m_sc[...] = jnp.full(m_sc.shape, -jnp.inf, f32)
    l_sc[...] = jnp.zeros_like(l_sc)
    acc_sc[...] = jnp.zeros_like(acc_sc)

    kw = ktiles * t

    def body(ki, carry):
        off = pl.multiple_of(ki * kw, kw)
        bidx = [jnp.clip(ki * ktiles + c - qi, -2, 2) + 2 for c in range(ktiles)]
        s_sc[...] = jnp.dot(q2, kt_ref[:, pl.ds(off, kw)], preferred_element_type=f32)
        for g in range(2 * t // r):
            rows = slice(g * r, (g + 1) * r)
            brow = (g * r) % t
            m = m_sc[rows, :]
            s = [s_sc[rows, c * t:(c + 1) * t] + bias_ref[bidx[c], brow:brow + r, :] for c in range(ktiles)]
            if not bounded:
                m_prev = m
                smax = functools.reduce(jnp.maximum, [jnp.max(x, axis=-1, keepdims=True) for x in s])
                m = jnp.maximum(m_prev, smax)
                alpha = jnp.exp2(m_prev - m)
                m_sc[rows, :] = m
                a_sc[rows, :] = alpha
            ps = [jnp.exp2(x[:, j * hw:(j + 1) * hw] - m) for x in s for j in range(t // hw)]
            psum = jnp.sum(sum(ps), axis=-1, keepdims=True)
            l_sc[rows, :] = (l_sc[rows, :] if bounded else alpha * l_sc[rows, :]) + psum
            for j in range(kw // hw):
                p_sc[rows, j * hw:(j + 1) * hw] = ps[j].astype(jnp.bfloat16)
        pv = jnp.dot(p_sc[...], v_ref[pl.ds(off, kw), :], preferred_element_type=f32)
        acc_sc[...] = (acc_sc[...] if bounded else a_sc[...] * acc_sc[...]) + pv
        return carry

    lax.fori_loop(0, nk // ktiles, body, 0)
    a = acc_sc[...] / l_sc[...]
    o = a[:t] - lam_ref[0] * a[t:]
    y = o * lax.rsqrt(jnp.mean(o * o, axis=-1, keepdims=True) + EPS)
    o_ref[...] = y * g_ref[...] * out_scale


def diff_attention(q2, kt, vb, stats, lam, out_gain, bias5, layer_idx):
    f32 = jnp.float32
    bsz, _, l, _ = vb.shape
    t = ATT_T
    hw = 2 * C_HEAD_DIM
    lam_init = 0.8 - 0.6 * math.exp(-0.3 * layer_idx)
    lam_f = lam.astype(f32)
    lam_full = jnp.exp(jnp.sum(lam_f[0] * lam_f[1])) - jnp.exp(jnp.sum(lam_f[2] * lam_f[3])) + lam_init
    norms = jnp.sqrt(jnp.max(stats[..., 0:4, 0], axis=2)) * (1.0 + 1e-3)
    qmax, kmax = norms[..., 0:2], norms[..., 2:4]
    bmax = jnp.max(bias5, axis=(1, 2, 3))
    bmin = jnp.min(bias5, axis=(1, 2, 3))
    gap = 2.0 * qmax * kmax + (bmax - bmin)[None, :, None]
    smem = pl.BlockSpec(memory_space=pltpu.SMEM)
    ktiles = math.gcd(ATT_KEY_TILES, l // t)

    def run(bounded):
        return pl.pallas_call(
            functools.partial(_attn_kernel, t=t, nk=l // t, ktiles=ktiles, out_scale=1.0 - lam_init,
                              bounded=bounded),
            grid=(bsz, C_HEADS, l // t),
            in_specs=[smem, smem, smem,
                      pl.BlockSpec((None, None, 2, t, hw), lambda b, h, i: (b, h, 0, i, 0)),
                      pl.BlockSpec((None, None, hw, l), lambda b, h, i: (b, h, 0, 0)),
                      pl.BlockSpec((None, None, l, hw), lambda b, h, i: (b, h, 0, 0)),
                      pl.BlockSpec((None, 5, t, t), lambda b, h, i: (h, 0, 0, 0)),
                      pl.BlockSpec((1, hw), lambda b, h, i: (0, 0))],
            out_specs=pl.BlockSpec((None, t, hw), lambda b, h, i: (b, i, h)),
            out_shape=jax.ShapeDtypeStruct((bsz, l, C_WIDTH), f32),
            scratch_shapes=[pltpu.VMEM((2 * t, hw), f32), pltpu.VMEM((2 * t, hw), f32), pltpu.VMEM((2 * t, hw), f32),
                            pltpu.VMEM((2 * t, ktiles * t), f32), pltpu.VMEM((2 * t, ktiles * t), jnp.bfloat16),
                            pltpu.VMEM((2 * t, hw), f32)],
            compiler_params=pltpu.CompilerParams(dimension_semantics=("parallel", "parallel", "arbitrary"),
                                                 vmem_limit_bytes=VMEM_LIMIT_BYTES),
            name="diff_attention_bounded" if bounded else "diff_attention_online",
        )(lam_full.reshape(1), kmax.reshape(-1), bmax, q2, kt, vb, bias5, out_gain.reshape(1, hw).astype(f32))

    return lax.cond(jnp.all(gap < ATT_SAFE_GAP), lambda: run(True), lambda: run(False))


GDN_TT = 512
GDN_HEADS_PER_STEP = 4
OD_QKV_BLOCK = 0
OD_GATE_BLOCK = OD_QKV_BLOCK + 3 * D_WIDTH // LANES
OD_AB_BLOCK = OD_GATE_BLOCK + D_WIDTH // LANES
OD_COLS = 2304


def _gdn_prep_kernel(prev_ref, cur_ref, next_ref, w_ref, o_ref, *, tl, nl):
    i = pl.program_id(1)
    part = pl.program_id(2)
    prev = jnp.where(i > 0, prev_ref[...], 0.0)
    nxt = jnp.where(i < nl - 1, next_ref[...], 0.0)
    ext = jnp.concatenate([prev, cur_ref[...], nxt], axis=0)
    halo = prev.shape[0]
    acc = None
    for j in range(CONV_WIDTH):
        start = halo - CONV_WIDTH // 2 + j
        term = w_ref[j:j + 1, :] * ext[start:start + tl, :]
        acc = term if acc is None else acc + term
    y = acc * jax.nn.sigmoid(acc)
    scale = jnp.where(part == 0, D_HEAD_DIM ** -0.5, 1.0)
    heads = []
    for h in range(D_HEADS):
        yh = y[:, h * LANES:(h + 1) * LANES]
        heads.append(yh * (lax.rsqrt(jnp.sum(yh * yh, axis=-1, keepdims=True) + EPS) * scale))
    o_ref[...] = jnp.where(part < 2, jnp.concatenate(heads, axis=1), y)


def gdn_prep(proj3, conv_w, *, tl=1024):
    bsz, l, _ = proj3.shape
    halo = SUBLANES
    nl = l // tl
    blk0 = OD_QKV_BLOCK * LANES // D_WIDTH
    return pl.pallas_call(
        functools.partial(_gdn_prep_kernel, tl=tl, nl=nl),
        grid=(bsz, nl, 3),
        in_specs=[pl.BlockSpec((None, halo, D_WIDTH), lambda b, i, p: (b, jnp.maximum(i * (tl // halo) - 1, 0), blk0 + p)),
                  pl.BlockSpec((None, tl, D_WIDTH), lambda b, i, p: (b, i, blk0 + p)),
                  pl.BlockSpec((None, halo, D_WIDTH),
                               lambda b, i, p: (b, jnp.minimum((i + 1) * (tl // halo), l // halo - 1), blk0 + p)),
                  pl.BlockSpec((CONV_WIDTH, D_WIDTH), lambda b, i, p: (0, p))],
        out_specs=pl.BlockSpec((None, None, tl, D_WIDTH), lambda b, i, p: (p, b, i, 0)),
        out_shape=jax.ShapeDtypeStruct((3, bsz, l, D_WIDTH), jnp.float32),
        compiler_params=pltpu.CompilerParams(dimension_semantics=("parallel", "parallel", "parallel"),
                                             vmem_limit_bytes=VMEM_LIMIT_BYTES),
        name="gdn_prep",
    )(proj3, proj3, proj3, conv_w.astype(jnp.float32))


def _gdn_gates_kernel(x_ref, nega_ref, dtb_ref, o_ref):
    x = x_ref[...]
    z = x + dtb_ref[...]
    g = nega_ref[...] * (jnp.maximum(z, 0.0) + jnp.log1p(jnp.exp(-jnp.abs(z))))
    lane = lax.broadcasted_iota(jnp.int32, x.shape, 1)
    y = jnp.where(lane < 2 * D_HEADS, g, jax.nn.sigmoid(x))
    o_ref[...] = y.T[0:4 * D_HEADS, :]


def gdn_gates(proj3, a_log, dt_bias, *, tl=512):
    bsz, l, _ = proj3.shape
    pad = LANES - 2 * D_HEADS
    nega = jnp.pad(-jnp.exp(a_log.astype(jnp.float32)).reshape(1, -1), ((0, 0), (0, pad)))
    dtb = jnp.pad(dt_bias.astype(jnp.float32).reshape(1, -1), ((0, 0), (0, pad)))
    vec = pl.BlockSpec((1, LANES), lambda b, i: (0, 0))
    return pl.pallas_call(
        _gdn_gates_kernel,
        grid=(bsz, l // tl),
        in_specs=[pl.BlockSpec((None, tl, LANES), lambda b, i: (b, i, OD_AB_BLOCK)), vec, vec],
        out_specs=pl.BlockSpec((None, 4 * D_HEADS, tl), lambda b, i: (b, 0, i)),
        out_shape=jax.ShapeDtypeStruct((bsz, 4 * D_HEADS, l), jnp.float32),
        compiler_params=pltpu.CompilerParams(dimension_semantics=("parallel", "parallel"),
                                             vmem_limit_bytes=VMEM_LIMIT_BYTES),
        name="gdn_gates",
    )(proj3, nega, dtb)


def gdn_constants():
    c = CHUNK
    r = np.arange(c)[:, None]
    u = np.arange(c)[None, :]
    cum, incl, strict = [], [], []
    for direction in range(2):
        fwd = direction == 0
        cum.append(np.concatenate([(r <= u) if fwd else (r >= u), np.ones((c, c), bool)], axis=1))
        incl.append((u <= r) if fwd else (u >= r))
        strict.append((u < r) if fwd else (u > r))
    same = lambda b: (r // b) == (u // b)
    merges = [same(2 * b) & ~same(b) for b in (8, 16, 32)]
    f32 = jnp.float32
    return (jnp.asarray(np.stack(cum), jnp.bfloat16), jnp.asarray(np.stack(incl), f32),
            jnp.asarray(np.stack(strict), f32), jnp.asarray(same(8), f32), jnp.asarray(np.stack(merges), f32))


def _gdn_kernel(q_ref, k_ref, v_ref, g_ref, b_ref, cum_ref, incl_ref, strict_ref, d8_ref, mrg_ref, o_ref,
                s_sc, qd_sc, dec_sc, w_sc, u_sc, sin_sc, *, nc):
    direction = pl.program_id(2)

    @pl.when(pl.program_id(3) == 0)
    def _():
        s_sc[...] = jnp.zeros_like(s_sc)

    bf16 = jnp.bfloat16
    f32 = jnp.float32
    c = CHUNK
    hd = D_HEAD_DIM
    contract_last = (((1,), (1,)), ((), ()))
    contract_first = (((0,), (0,)), ((), ()))
    cumm = cum_ref[...]
    incl = incl_ref[...]
    strict = strict_ref[...]
    d8 = d8_ref[...]
    eye = (lax.broadcasted_iota(jnp.int32, (c, c), 0) == lax.broadcasted_iota(jnp.int32, (c, c), 1)).astype(f32)

    def mm(a, b):
        return jnp.dot(a.astype(bf16), b.astype(bf16), preferred_element_type=f32)

    def rep(x):
        return jnp.concatenate([x] * (hd // c), axis=1)

    nh = g_ref.shape[0]
    ns = range(nh * nc)
    head = [m // nc for m in ns]
    rows = [slice((m % nc) * c, (m % nc + 1) * c) for m in ns]
    cols = [slice(h * hd, (h + 1) * hd) for h in head]
    q = [q_ref[rows[m], cols[m]] for m in ns]
    k = [k_ref[rows[m], cols[m]] for m in ns]
    v = [v_ref[rows[m], cols[m]] for m in ns]
    kb = [x.astype(bf16) for x in k]
    kk = [lax.dot_general(x, x, contract_last, preferred_element_type=f32) for x in kb]
    qk = [lax.dot_general(q[n].astype(bf16), kb[n], contract_last, preferred_element_type=f32) for n in ns]
    grow = [jnp.broadcast_to(g_ref[head[m], :, rows[m]], (c, c)) for m in ns]
    ghi = [x.astype(bf16) for x in grow]
    glo = [(grow[n] - ghi[n].astype(f32)).astype(bf16) for n in ns]
    gm = [jnp.dot(ghi[n], cumm, preferred_element_type=f32) + jnp.dot(glo[n], cumm, preferred_element_type=f32)
          for n in ns]
    gam_row = [x[:, :c] for x in gm]
    tot = [x[:, c:] for x in gm]
    gam_col = [x.T for x in gam_row]
    beta_col = [jnp.broadcast_to(b_ref[head[m], :, rows[m]], (c, c)).T for m in ns]
    decay = [incl * jnp.exp(jnp.minimum(gam_col[n] - gam_row[n], 0.0)) for n in ns]
    a = [strict * beta_col[n] * kk[n] * decay[n] for n in ns]
    a0 = [x * d8 for x in a]
    n2 = [mm(x, x) for x in a0]
    n4 = [mm(x, x) for x in n2]
    t = [mm(eye - a0[n], eye + n2[n]) for n in ns]
    t = [mm(t[n], eye + n4[n]) for n in ns]
    for j in range(mrg_ref.shape[0]):
        p = [mm(a[n] * mrg_ref[j], t[n]) for n in ns]
        t = [t[n] - mm(t[n], p[n]) for n in ns]
    beta128 = [rep(x) for x in beta_col]
    egam128 = [rep(jnp.exp(x)) for x in gam_col]
    solb = [mm(t[n], jnp.concatenate([k[n] * beta128[n] * egam128[n], v[n] * beta128[n]], axis=1)).astype(bf16)
            for n in ns]
    av = [jnp.dot((qk[n] * decay[n]).astype(bf16), solb[n], preferred_element_type=f32) for n in ns]
    k_dec = [(k[n] * rep(jnp.exp(tot[n] - gam_col[n]))).astype(bf16) for n in ns]
    wu = [lax.dot_general(k_dec[n], solb[n], contract_first, preferred_element_type=f32) for n in ns]
    for n in ns:
        qd_sc[n] = (q[n] * egam128[n] - av[n][:, :hd]).astype(bf16)
        o_ref[rows[n], cols[n]] = av[n][:, hd:]
        w_sc[n] = wu[n][:, :hd].astype(bf16)
        u_sc[n] = wu[n][:, hd:]
        dec_sc[n] = rep(jnp.exp(tot[n][0:SUBLANES, :]))

    def body(ci, states):
        ce = ci + direction * (nc - 1 - 2 * ci)
        new = []
        for h in range(nh):
            m = h * nc + ce
            sb = states[h].astype(bf16)
            sin_sc[m] = sb
            new.append(states[h] * dec_sc[m][0:1] - jnp.dot(w_sc[m], sb, preferred_element_type=f32) + u_sc[m])
        return tuple(new)

    states = lax.fori_loop(0, nc, body, tuple(s_sc[h] for h in range(nh)))
    for h in range(nh):
        s_sc[h] = states[h]

    for m in ns:
        o_ref[rows[m], cols[m]] += jnp.dot(qd_sc[m], sin_sc[m], preferred_element_type=f32)


def gdn_scan(qkv, gb):
    _, bsz, l, _ = qkv.shape
    hd = D_HEAD_DIM
    tt = min(GDN_TT, l)
    nt = l // tt
    nc = tt // CHUNK
    assert l % tt == 0 and tt % CHUNK == 0
    consts = gdn_constants()
    gb4 = gb.reshape(bsz, 4 * D_HEADS, 1, l)
    nh = GDN_HEADS_PER_STEP
    assert D_HEADS % nh == 0
    tidx = lambda d, i: i + d * (nt - 1 - 2 * i)
    qkv_spec = lambda p: pl.BlockSpec((None, None, tt, nh * hd), lambda b, h, d, i: (p, b, tidx(d, i), h))
    row_spec = lambda off: pl.BlockSpec((None, nh, 1, tt),
                                        lambda b, h, d, i: (b, (off + d * D_HEADS) // nh + h, 0, tidx(d, i)))
    per_dir = lambda a: pl.BlockSpec((None,) + a.shape[1:], lambda b, h, d, i: (d,) + (0,) * (a.ndim - 1))
    whole = lambda a: pl.BlockSpec(a.shape, lambda b, h, d, i: (0,) * a.ndim)
    return pl.pallas_call(
        functools.partial(_gdn_kernel, nc=nc),
        grid=(bsz, D_HEADS // nh, 2, nt),
        in_specs=[qkv_spec(0), qkv_spec(1), qkv_spec(2), row_spec(0), row_spec(2 * D_HEADS),
                  per_dir(consts[0]), per_dir(consts[1]), per_dir(consts[2]), whole(consts[3]), whole(consts[4])],
        out_specs=pl.BlockSpec((None, None, tt, nh * hd), lambda b, h, d, i: (d, b, tidx(d, i), h)),
        out_shape=jax.ShapeDtypeStruct((2, bsz, l, D_WIDTH), jnp.float32),
        scratch_shapes=[pltpu.VMEM((nh, hd, hd), jnp.float32),
                        pltpu.VMEM((nh * nc, CHUNK, hd), jnp.bfloat16),
                        pltpu.VMEM((nh * nc, SUBLANES, hd), jnp.float32),
                        pltpu.VMEM((nh * nc, hd, hd), jnp.bfloat16),
                        pltpu.VMEM((nh * nc, hd, hd), jnp.float32),
                        pltpu.VMEM((nh * nc, hd, hd), jnp.bfloat16)],
        compiler_params=pltpu.CompilerParams(dimension_semantics=("parallel", "parallel", "parallel", "arbitrary"),
                                             vmem_limit_bytes=VMEM_LIMIT_BYTES),
        name="gdn_scan",
    )(qkv, qkv, qkv, gb4, gb4, *consts)


def gated_deltanet(proj3, conv_w, a_log, dt_bias):
    return gdn_scan(gdn_prep(proj3, conv_w), gdn_gates(proj3, a_log, dt_bias))


MOE_TT = 512
MOE_SUB = 128
MOE_ALIGN = 64
MOE_TILES_PER_STEP = 4
MOE_ROWS = 256
MOE_SLAB = 256
MOE_VMEM_LIMIT_BYTES = 56 * 1024 * 1024
MOE_EXPERT_VMEM_LIMIT_BYTES = 60 * 1024 * 1024


def _router_kernel(x_ref, g_ref, wr_ref, h_ref, aff_ref):
    x = x_ref[...]
    h = (x * lax.rsqrt(jnp.mean(x * x, axis=-1, keepdims=True) + EPS) * g_ref[...]).astype(jnp.bfloat16)
    h_ref[...] = h
    logits = jnp.dot(h, wr_ref[...], preferred_element_type=jnp.float32)
    lane = lax.broadcasted_iota(jnp.int32, logits.shape, 1)
    logits = jnp.where(lane < N_EXPERTS, logits, -jnp.inf)
    p = jnp.exp(logits - jnp.max(logits, axis=-1, keepdims=True))
    aff = p / jnp.sum(p, axis=-1, keepdims=True)
    aff_ref[...] = aff.T[0:N_EXPERTS, :]


def moe_route(x, gain, w_router, *, tm=MOE_TT):
    bsz, l, d = x.shape
    wr = jnp.pad(w_router.astype(jnp.bfloat16), ((0, 0), (0, LANES - N_EXPERTS)))
    return pl.pallas_call(
        _router_kernel,
        grid=(bsz, l // tm),
        in_specs=[pl.BlockSpec((None, tm, d), lambda b, i: (b, i, 0)),
                  pl.BlockSpec((1, d), lambda b, i: (0, 0)),
                  pl.BlockSpec((d, LANES), lambda b, i: (0, 0))],
        out_specs=[pl.BlockSpec((None, tm, d), lambda b, i: (b, i, 0)),
                   pl.BlockSpec((None, N_EXPERTS, tm), lambda b, i: (b, 0, i))],
        out_shape=[jax.ShapeDtypeStruct((bsz, l, d), jnp.bfloat16),
                   jax.ShapeDtypeStruct((bsz, N_EXPERTS, l), jnp.float32)],
        compiler_params=pltpu.CompilerParams(dimension_semantics=("parallel", "parallel"),
                                             vmem_limit_bytes=VMEM_LIMIT_BYTES),
        name="moe_router",
    )(x, gain.reshape(1, d).astype(jnp.float32), wr)


def _select_kernel(aff_ref, pre_ref, smap_ref, gate_ref, cnt_ref, *, cap, tt):
    f32 = jnp.float32
    bf16 = jnp.bfloat16
    aff = aff_ref[...]
    e, l = aff.shape
    nl = l // LANES
    tiles = [slice(j * LANES, (j + 1) * LANES) for j in range(nl)]
    bits = pltpu.bitcast(aff, jnp.int32)
    bt = [bits[:, s] for s in tiles]

    def lane_total(x):
        return jnp.broadcast_to(jnp.sum(x, axis=-1, keepdims=True), (e, LANES))

    def search(i, thr):
        cand = thr | jnp.left_shift(jnp.int32(1), 30 - i)
        acc = jnp.zeros((e, LANES), jnp.int32)
        for x in bt:
            acc = acc + (x >= cand).astype(jnp.int32)
        return jnp.where(lane_total(acc) >= cap, cand, thr)

    thr = lax.fori_loop(0, 31, search, jnp.zeros((e, LANES), jnp.int32))
    gt = [x > thr for x in bt]
    eq = [x == thr for x in bt]
    acc = jnp.zeros((e, LANES), jnp.int32)
    for x in gt:
        acc = acc + x.astype(jnp.int32)
    need = (cap - lane_total(acc)).astype(f32)

    pre = pre_ref[...]

    def prefix(flags):
        outs = [jnp.dot(jnp.where(x, 1.0, 0.0).astype(bf16), pre, preferred_element_type=f32) for x in flags]
        carry = jnp.zeros((e, LANES), f32)
        res = []
        for o in outs:
            res.append(o[:, :LANES] + carry)
            carry = carry + o[:, LANES:]
        return res, [o[:, LANES:] for o in outs]

    rank_eq, _ = prefix(eq)
    sel = [jnp.logical_or(gt[j], jnp.logical_and(eq[j], rank_eq[j] < need)) for j in range(nl)]
    pos, totals = prefix(sel)
    lane = lax.broadcasted_iota(jnp.int32, (e, LANES), 1)
    cnt = jnp.zeros((e, LANES), f32)
    per = tt // LANES
    for j in range(nl):
        smap_ref[:, tiles[j]] = jnp.where(sel[j], pos[j], -1.0)
        gate_ref[:, tiles[j]] = jnp.where(sel[j], aff[:, tiles[j]], 0.0)
        cnt = cnt + jnp.where(lane == j // per, totals[j], 0.0)
    cnt_ref[...] = cnt


def moe_select(aff, cap, *, tt=MOE_TT):
    bsz, e, l = aff.shape
    assert l // tt <= LANES
    i = np.arange(LANES)
    pre = np.concatenate([i[:, None] < i[None, :], np.ones((LANES, LANES), bool)], axis=1)
    row = pl.BlockSpec((None, e, l), lambda b: (b, 0, 0))
    return pl.pallas_call(
        functools.partial(_select_kernel, cap=cap, tt=tt),
        grid=(bsz,),
        in_specs=[row, pl.BlockSpec((LANES, 2 * LANES), lambda b: (0, 0))],
        out_specs=[row, row, pl.BlockSpec((None, e, LANES), lambda b: (b, 0, 0))],
        out_shape=[jax.ShapeDtypeStruct((bsz, e, l), jnp.float32), jax.ShapeDtypeStruct((bsz, e, l), jnp.float32),
                   jax.ShapeDtypeStruct((bsz, e, LANES), jnp.float32)],
        compiler_params=pltpu.CompilerParams(dimension_semantics=("parallel",),
                                             vmem_limit_bytes=VMEM_LIMIT_BYTES),
        name="moe_select",
    )(aff, jnp.asarray(pre, jnp.bfloat16))


def _slot_one_hot(pos, base, rows, n):
    slot = (base + lax.broadcasted_iota(jnp.int32, (rows, n), 0)).astype(jnp.float32)
    return jnp.where(pos == slot, 1.0, 0.0).astype(jnp.bfloat16)


def _expert_kernel(cs_ref, h_ref, smap_ref, gate_ref, wg32_ref, wu32_ref, wd32_ref, o_ref,
                   xs_sc, gs_sc, wg_ref, wu_ref, wd_ref, *, nj, per, tt, cap):
    e = pl.program_id(0)
    b = pl.program_id(1)
    j = pl.program_id(2)
    f32 = jnp.float32
    bf16 = jnp.bfloat16

    @pl.when(jnp.logical_and(b == 0, j == 0))
    def _():
        wg_ref[...] = wg32_ref[...].astype(bf16)
        wu_ref[...] = wu32_ref[...].astype(bf16)
        wd_ref[...] = wd32_ref[...].astype(bf16)

    @pl.when(j == 0)
    def _():
        xs_sc[...] = jnp.zeros_like(xs_sc)
        gs_sc[...] = jnp.zeros_like(gs_sc)

    win = min(MOE_SUB, cap)
    for s in range(per):
        base = (b * N_EXPERTS + e) * (nj * per + 1) + j * per + s
        c0 = cs_ref[base]
        c1 = cs_ref[base + 1]
        cols = slice(s * tt, (s + 1) * tt)
        pos = smap_ref[pl.ds(e % SUBLANES, 1), cols]
        gate = gate_ref[pl.ds(e % SUBLANES, 1), cols]
        hb = h_ref[cols, :]

        def gather(r0, rows, pos=pos, gate=gate, hb=hb):
            oh = _slot_one_hot(pos, r0, rows, tt)
            xs_sc[pl.ds(r0, rows), :] += jnp.dot(oh, hb, preferred_element_type=f32)
            g = jnp.sum(oh.astype(f32) * gate, axis=-1, keepdims=True)
            gs_sc[pl.ds(r0, rows), :] += jnp.broadcast_to(g, (rows, LANES))

        w0 = pl.multiple_of(jnp.minimum(c0 // MOE_ALIGN * MOE_ALIGN, cap - win), MOE_ALIGN)
        gather(w0, win)

        def rest(st, carry, gather=gather):
            gather(pl.multiple_of(st * MOE_ALIGN, MOE_ALIGN), MOE_ALIGN)
            return carry

        lax.fori_loop((w0 + win) // MOE_ALIGN, (c1 + MOE_ALIGN - 1) // MOE_ALIGN, rest, 0)

    @pl.when(j == nj - 1)
    def _():
        rows_per = min(MOE_ROWS, cap)
        for r in range(cap // rows_per):
            rows = slice(r * rows_per, (r + 1) * rows_per)
            xb = xs_sc[rows, :].astype(bf16)
            g = jnp.dot(xb, wg_ref[...], preferred_element_type=f32)
            u = jnp.dot(xb, wu_ref[...], preferred_element_type=f32)
            hid = (g * jax.nn.sigmoid(g) * u).astype(bf16)
            out = jnp.dot(hid, wd_ref[...], preferred_element_type=f32)
            scale = jnp.concatenate([gs_sc[rows, :]] * (out.shape[1] // LANES), axis=1)
            o_ref[rows, :] = (out * scale).astype(bf16)


def moe_experts(hb, smap, gate, cs, w_gate, w_up, w_down, layer, cap, *, tt=MOE_TT):
    bsz, l, d = hb.shape
    _, e, _, ff = w_gate.shape
    per = MOE_TILES_PER_STEP if (l // tt) % MOE_TILES_PER_STEP == 0 else 1
    nj = l // (tt * per)
    tok = pl.BlockSpec((None, SUBLANES, per * tt), lambda ei, b, j, cs_ref: (b, ei // SUBLANES, j))
    once = pl.Buffered(1)
    grid_spec = pltpu.PrefetchScalarGridSpec(
        num_scalar_prefetch=1,
        grid=(e, bsz, nj),
        in_specs=[pl.BlockSpec((None, per * tt, d), lambda ei, b, j, cs_ref: (b, j, 0)), tok, tok,
                  pl.BlockSpec((None, None, d, ff), lambda ei, b, j, cs_ref: (layer, ei, 0, 0), pipeline_mode=once),
                  pl.BlockSpec((None, None, d, ff), lambda ei, b, j, cs_ref: (layer, ei, 0, 0), pipeline_mode=once),
                  pl.BlockSpec((None, None, ff, d), lambda ei, b, j, cs_ref: (layer, ei, 0, 0), pipeline_mode=once)],
        out_specs=pl.BlockSpec((None, None, cap, d), lambda ei, b, j, cs_ref: (b, ei, 0, 0)),
        scratch_shapes=[pltpu.VMEM((cap, d), jnp.float32), pltpu.VMEM((cap, LANES), jnp.float32),
                        pltpu.VMEM((d, ff), jnp.bfloat16), pltpu.VMEM((d, ff), jnp.bfloat16),
                        pltpu.VMEM((ff, d), jnp.bfloat16)])
    return pl.pallas_call(
        functools.partial(_expert_kernel, nj=nj, per=per, tt=tt, cap=cap),
        grid_spec=grid_spec,
        out_shape=jax.ShapeDtypeStruct((bsz, e, cap, d), jnp.bfloat16),
        compiler_params=pltpu.CompilerParams(dimension_semantics=("parallel", "arbitrary", "arbitrary"),
                                             vmem_limit_bytes=MOE_EXPERT_VMEM_LIMIT_BYTES),
        name="moe_experts",
    )(cs, hb, smap, gate, w_gate, w_up, w_down)


def _combine_kernel(cs_ref, x_ref, smap_ref, ow_ref, y_ref, *, nj, tt):
    b = pl.program_id(0)
    e = pl.program_id(2)

    @pl.when(e == 0)
    def _():
        y_ref[...] = x_ref[...]

    contract_first = (((0,), (0,)), ((), ()))
    cap = ow_ref.shape[0]
    win = min(2 * MOE_SUB, cap)
    base = (b * N_EXPERTS + e) * (nj + 1)
    cols = [slice(j * tt, (j + 1) * tt) for j in range(nj)]
    pos = [smap_ref[pl.ds(e % SUBLANES, 1), c] for c in cols]
    r0 = [pl.multiple_of(jnp.minimum(cs_ref[base + j] // MOE_SUB * MOE_SUB, cap - win), MOE_SUB) for j in range(nj)]
    oh = [_slot_one_hot(pos[j], r0[j], win, tt) for j in range(nj)]
    add = [lax.dot_general(oh[j], ow_ref[pl.ds(r0[j], win), :], contract_first, preferred_element_type=jnp.float32)
           for j in range(nj)]
    for j in range(nj):
        y_ref[cols[j], :] += add[j]

    for j in range(nj):
        def scatter(st, carry):
            s0 = pl.multiple_of(st * MOE_SUB, MOE_SUB)
            y_ref[cols[j], :] += lax.dot_general(_slot_one_hot(pos[j], s0, MOE_SUB, tt),
                                                 ow_ref[pl.ds(s0, MOE_SUB), :], contract_first,
                                                 preferred_element_type=jnp.float32)
            return carry

        lax.fori_loop((r0[j] + win) // MOE_SUB, (cs_ref[base + j + 1] + MOE_SUB - 1) // MOE_SUB, scatter, 0)


def moe_combine(x, smap, outw, cs, *, tt=MOE_TT):
    bsz, l, d = x.shape
    e, cap = outw.shape[1:3]
    nj = l // tt
    grid_spec = pltpu.PrefetchScalarGridSpec(
        num_scalar_prefetch=1,
        grid=(bsz, d // MOE_SLAB, e),
        in_specs=[pl.BlockSpec((None, l, MOE_SLAB), lambda b, s, ei, cs_ref: (b, 0, s)),
                  pl.BlockSpec((None, SUBLANES, l), lambda b, s, ei, cs_ref: (b, ei // SUBLANES, 0)),
                  pl.BlockSpec((None, None, cap, MOE_SLAB), lambda b, s, ei, cs_ref: (b, ei, 0, s))],
        out_specs=pl.BlockSpec((None, l, MOE_SLAB), lambda b, s, ei, cs_ref: (b, 0, s)))
    return pl.pallas_call(
        functools.partial(_combine_kernel, nj=nj, tt=tt),
        grid_spec=grid_spec,
        out_shape=jax.ShapeDtypeStruct((bsz, l, d), jnp.float32),
        compiler_params=pltpu.CompilerParams(dimension_semantics=("parallel", "parallel", "arbitrary"),
                                             vmem_limit_bytes=MOE_VMEM_LIMIT_BYTES),
        name="moe_combine",
    )(cs, x, smap, outw)


def ec_moe_layer(x, gain, w_router, w_gate, w_up, w_down, layer):
    bsz, l, d = x.shape
    cap = EC_CAPACITY_FACTOR * l // N_EXPERTS
    tt = min(MOE_TT, l)
    nj = l // tt
    hb, aff = moe_route(x, gain, w_router, tm=tt)
    smap, gate, cnt = moe_select(aff, cap, tt=tt)
    cs = jnp.concatenate([jnp.zeros((bsz, N_EXPERTS, 1), jnp.float32), jnp.cumsum(cnt[..., :nj], axis=-1)], axis=-1)
    cs = cs.astype(jnp.int32).reshape(-1)
    outw = moe_experts(hb, smap, gate, cs, w_gate, w_up, w_down, layer, cap, tt=tt)
    return moe_combine(x, smap, outw, cs, tt=tt)


def kernel(x, mix_norm, ffn_norm, ev_w_in, ev_w_out, a_lb_logits, a_out_norm, s5_lambda_re, s5_lambda_im, s5_log_step, s5_b_re, s5_b_im, s5_c_re, s5_c_im, s5_d, s5_glu_w, s5_glu_b, od_w_in, od_w_out, c_q_norm, c_k_norm, c_lambda, c_out_norm, rel_bias, d_conv_w, d_a_log, d_dt_bias, d_out_norm, moe_router, moe_w_gate, moe_w_up, moe_w_down):
    bsz, l, d = x.shape
    p = jax.nn.softmax(a_lb_logits.astype(jnp.float32), axis=0)
    cum = jnp.cumsum(p, axis=0)
    lower_bounds = cum - cum[0:1]
    bias5 = rel_bias_tiles(rel_bias, ATT_T)
    for layer in range(DEPTH):
        j = layer // 2
        if layer % 2 == 0:
            proj, u_tb = norm_matmul(x, mix_norm[layer], ev_w_in[j], tail=B_WIDTH)
            o_a2 = hgrn2_scan(proj, lower_bounds[j])
            o_b = s5_mixer_tb(u_tb.reshape(l * bsz, B_WIDTH), bsz, s5_lambda_re[j], s5_lambda_im[j], s5_log_step[j],
                              s5_b_re[j], s5_b_im[j], s5_c_re[j], s5_c_im[j], s5_d[j], s5_glu_w[j], s5_glu_b[j])
            x = mixer_out_proj(o_a2, proj, 4 * A_HEADS, a_out_norm[j], o_b.reshape(l, bsz * B_WIDTH), ev_w_out[j], x,
                               bidir_first=True, other_time_major=True)
        else:
            o2 = 3 * C_WIDTH + 3 * D_WIDTH
            o4 = o2 + 4 * D_HEADS
            w = od_w_in[j]
            w_in = jnp.concatenate([w[:, :o2], w[:, o4:], w[:, o2:o4],
                                    jnp.zeros((d, 3 * C_WIDTH + OD_COLS - w.shape[1]), w.dtype)], axis=1)
            proj, q2, kt, vb, stats = odd_in_proj(x, mix_norm[layer], w_in, c_q_norm[j], c_k_norm[j])
            o_c = diff_attention(q2, kt, vb, stats, c_lambda[j], c_out_norm[j], bias5, layer)
            o_d2 = gated_deltanet(proj, d_conv_w[j], d_a_log[j], d_dt_bias[j])
            x = mixer_out_proj(o_d2, proj, OD_GATE_BLOCK, d_out_norm[j], o_c, od_w_out[j], x, bidir_first=False)
        x = ec_moe_layer(x, ffn_norm[layer], moe_router[layer], moe_w_gate, moe_w_up, moe_w_down, layer)
    return x
```

```python
import functools
import math

import jax
import jax.numpy as jnp
import numpy as np
from jax import lax
from jax.experimental import pallas as pl
from jax.experimental.pallas import tpu as pltpu

D_MODEL = 1024
DEPTH = 4
MIX_WIDTH = D_MODEL
A_WIDTH = MIX_WIDTH // 2
A_HEAD_DIM = 128
A_HEADS = A_WIDTH // A_HEAD_DIM
B_WIDTH = MIX_WIDTH - A_WIDTH
S5_GROUP = 16
S5_GROUPS = B_WIDTH // S5_GROUP
S5_STATE = 64
C_WIDTH = MIX_WIDTH // 2
C_HEAD_DIM = 64
C_HEADS = C_WIDTH // (2 * C_HEAD_DIM)
D_WIDTH = MIX_WIDTH - C_WIDTH
D_HEAD_DIM = 128
D_HEADS = D_WIDTH // D_HEAD_DIM
CONV_WIDTH = 5
N_EXPERTS = 16
EC_CAPACITY_FACTOR = 2
REL_BUCKETS = 32
REL_MAX_DIST = 128
CHUNK = 64
EPS = 1e-6

LANES = 128
SUBLANES = 8
VMEM_LIMIT_BYTES = 48 * 1024 * 1024


PROJ_TM = 512
PROJ_COLS = 512


def _norm_matmul_kernel(x_ref, g_ref, w_ref, o_ref, *tail_ref, main):
    x = x_ref[...]
    y = (x * lax.rsqrt(jnp.mean(x * x, axis=-1, keepdims=True) + EPS) * g_ref[...]).astype(jnp.bfloat16)
    for c0 in range(0, main, PROJ_COLS):
        c1 = min(c0 + PROJ_COLS, main)
        o_ref[:, c0:c1] = jnp.dot(y, w_ref[:, c0:c1], preferred_element_type=jnp.float32)
    if tail_ref:
        tail_ref[0][...] = jnp.dot(y, w_ref[:, main:], preferred_element_type=jnp.float32)


def norm_matmul(x, gain, w, *, tail=0, tm=PROJ_TM):
    bsz, l, k = x.shape
    m = w.shape[1]
    main = m - tail
    tm = min(tm, l)
    out_shape = [jax.ShapeDtypeStruct((bsz, l, main), jnp.float32)]
    out_specs = [pl.BlockSpec((None, tm, main), lambda b, i: (b, i, 0))]
    if tail:
        out_shape.append(jax.ShapeDtypeStruct((l, bsz * tail), jnp.float32))
        out_specs.append(pl.BlockSpec((tm, tail), lambda b, i: (i, b)))
    outs = pl.pallas_call(
        functools.partial(_norm_matmul_kernel, main=main),
        grid=(bsz, l // tm),
        in_specs=[pl.BlockSpec((None, tm, k), lambda b, i: (b, i, 0)),
                  pl.BlockSpec((1, k), lambda b, i: (0, 0)),
                  pl.BlockSpec((k, m), lambda b, i: (0, 0), pipeline_mode=pl.Buffered(1))],
        out_specs=out_specs,
        out_shape=out_shape,
        compiler_params=pltpu.CompilerParams(dimension_semantics=("parallel", "parallel"),
                                             vmem_limit_bytes=VMEM_LIMIT_BYTES),
        name="norm_matmul",
    )(x, gain.reshape(1, k).astype(jnp.float32), w.astype(jnp.bfloat16))
    return outs if tail else outs[0]


def _mixer_out_kernel(of_ref, ob_ref, g_ref, gain_ref, other_ref, wb_ref, wo_ref, r_ref, o_ref):
    bf16 = jnp.bfloat16
    o = of_ref[...] + ob_ref[...]
    g = g_ref[...]
    gate = g * jax.nn.sigmoid(g)
    hd = gain_ref.shape[1]
    heads = []
    for h in range(o.shape[1] // hd):
        oh = o[:, h * hd:(h + 1) * hd]
        heads.append(oh * lax.rsqrt(jnp.mean(oh * oh, axis=-1, keepdims=True) + EPS) * gain_ref[...])
    y = (jnp.concatenate(heads, axis=1) * gate).astype(bf16)
    o_ref[...] = (r_ref[...] + jnp.dot(y, wb_ref[...], preferred_element_type=jnp.float32)
                  + jnp.dot(other_ref[...].astype(bf16), wo_ref[...], preferred_element_type=jnp.float32))


def mixer_out_proj(o2, proj3, gate_block, out_gain, other, w, res, *, bidir_first, other_time_major=False,
                   tm=PROJ_TM):
    _, bsz, l, k = o2.shape
    m = w.shape[1]
    tm = min(tm, l)
    wb = w.astype(jnp.bfloat16)
    w_bidir, w_other = (wb[:k], wb[k:]) if bidir_first else (wb[k:], wb[:k])
    other_spec = (pl.BlockSpec((tm, k), lambda b, i: (i, b)) if other_time_major
                  else pl.BlockSpec((None, tm, k), lambda b, i: (b, i, 0)))
    gb = gate_block * LANES // k
    row = pl.BlockSpec((None, tm, m), lambda b, i: (b, i, 0))
    wspec = pl.BlockSpec((k, m), lambda b, i: (0, 0))
    return pl.pallas_call(
        _mixer_out_kernel,
        grid=(bsz, l // tm),
        in_specs=[pl.BlockSpec((None, None, tm, k), lambda b, i: (0, b, i, 0)),
                  pl.BlockSpec((None, None, tm, k), lambda b, i: (1, b, i, 0)),
                  pl.BlockSpec((None, tm, k), lambda b, i: (b, i, gb)),
                  pl.BlockSpec((1, LANES), lambda b, i: (0, 0)),
                  other_spec, wspec, wspec, row],
        out_specs=row,
        out_shape=jax.ShapeDtypeStruct((bsz, l, m), jnp.float32),
        compiler_params=pltpu.CompilerParams(dimension_semantics=("parallel", "parallel"),
                                             vmem_limit_bytes=VMEM_LIMIT_BYTES),
        name="mixer_out_proj",
    )(o2, o2, proj3, out_gain.reshape(1, LANES).astype(jnp.float32), other, w_bidir, w_other, res)


HG_LEVELS = tuple(CHUNK >> (i + 1) for i in range(CHUNK.bit_length() - 1))
HG_TOT_ROWS = 8
HG_TT = 1024


def hgrn2_constants():
    c = CHUNK
    r = np.arange(c)[:, None]
    u = np.arange(c)[None, :]
    stacks, masks = [], []
    for direction in range(2):
        fwd = direction == 0
        lvl_masks = []
        for m in HG_LEVELS:
            blk = r // (2 * m)
            later = (r % (2 * m)) >= m
            lvl_masks.append((blk == blk.T) & (later & ~later.T if fwd else ~later & later.T))
        stacks.append(np.concatenate([(u <= r) if fwd else (u >= r), np.ones((HG_TOT_ROWS, c), bool)], axis=0))
        masks.append(np.stack(lvl_masks))
    return (jnp.asarray(np.stack(stacks), jnp.bfloat16), jnp.asarray(np.stack(masks), jnp.float32))


def _hgrn2_kernel(q_ref, f_ref, v_ref, loglb_ref, log1mlb_ref, onemlb_ref, ast_ref, mask_ref, o_ref,
                  st_sc, qd_sc, dec_sc, upd_sc, sin_sc, *, nc):
    direction = pl.program_id(2)

    @pl.when(pl.program_id(3) == 0)
    def _():
        st_sc[...] = jnp.zeros_like(st_sc)

    bf16 = jnp.bfloat16
    f32 = jnp.float32
    c = CHUNK
    hd = A_HEAD_DIM
    dirf = direction.astype(f32)
    loglb = loglb_ref[...]
    log1mlb = log1mlb_ref[...]
    onemlb = onemlb_ref[...]
    ast = ast_ref[...]
    contract_last = (((1,), (1,)), ((), ()))
    contract_first = (((0,), (0,)), ((), ()))

    ns = range(nc)
    rows = [slice(n * c, (n + 1) * c) for n in ns]
    z = [f_ref[r, :] for r in rows]
    v = [v_ref[r, :] for r in rows]
    qr = [q_ref[r, :] for r in rows]
    q = [x * jax.nn.sigmoid(x) for x in qr]
    e = [jnp.exp(-jnp.abs(x)) for x in z]
    cc = [log1mlb + jnp.minimum(z[n], 0.0) - jnp.log1p(e[n]) for n in ns]
    lf = [jnp.maximum(loglb, x) + jnp.log1p(jnp.exp(-jnp.abs(loglb - x))) for x in cc]
    k = [onemlb * jnp.where(z[n] >= 0, e[n], 1.0) / (1.0 + e[n]) for n in ns]
    hi = [x.astype(bf16) for x in lf]
    lo = [(lf[n] - hi[n].astype(f32)).astype(bf16) for n in ns]
    d = [jnp.dot(ast, hi[n], preferred_element_type=f32) + jnp.dot(ast, lo[n], preferred_element_type=f32)
         for n in ns]
    cum = [x[0:c] for x in d]
    tot = [x[c:c + HG_TOT_ROWS] for x in d]
    ref = [cum[n] - dirf * lf[n] for n in ns]
    attn = [jnp.zeros((c, c), f32) for _ in ns]
    for li, m in enumerate(HG_LEVELS):
        nb = c // (2 * m)
        split = [jnp.broadcast_to(x.reshape(nb, 2 * m, hd)[:, m - 1:m, :], (nb, 2 * m, hd)).reshape(c, hd)
                 for x in ref]
        x = [jnp.exp(-jnp.abs(cum[n] - split[n])) for n in ns]
        s = [lax.dot_general((q[n] * x[n]).astype(bf16), (k[n] * x[n]).astype(bf16), contract_last,
                             preferred_element_type=f32) for n in ns]
        attn = [attn[n] + mask_ref[li] * s[n] for n in ns]
    vb = [x.astype(bf16) for x in v]
    intra = [jnp.dot(attn[n].astype(bf16), vb[n], preferred_element_type=f32) for n in ns]
    upd = [lax.dot_general(vb[n], (k[n] * jnp.exp(tot[n][0:1] - cum[n])).astype(bf16), contract_first,
                           preferred_element_type=f32) for n in ns]
    for n in ns:
        o_ref[rows[n], :] = intra[n] + jnp.sum(q[n] * k[n], axis=-1, keepdims=True) * v[n]
        qd_sc[n] = (q[n] * jnp.exp(cum[n])).astype(bf16)
        dec_sc[n] = jnp.exp(tot[n])
        upd_sc[n] = upd[n]

    def body(ci, st):
        ce = ci + direction * (nc - 1 - 2 * ci)
        sin_sc[ce] = st.astype(bf16)
        return st * dec_sc[ce][0:1] + upd_sc[ce]

    st_sc[...] = lax.fori_loop(0, nc, body, st_sc[...])

    for n in range(nc):
        rows = slice(n * c, (n + 1) * c)
        o_ref[rows, :] += lax.dot_general(qd_sc[n], sin_sc[n], contract_last, preferred_element_type=f32)


def hgrn2_scan(proj3, lb):
    bsz, l, _ = proj3.shape
    hd = A_HEAD_DIM
    tt = min(HG_TT, l)
    nt = l // tt
    assert l % tt == 0 and tt % CHUNK == 0
    ast, masks = hgrn2_constants()
    lb = lb.astype(jnp.float32)
    vecs = [jnp.log(lb).reshape(2, 1, A_WIDTH), jnp.log1p(-lb).reshape(2, 1, A_WIDTH), (1.0 - lb).reshape(2, 1, A_WIDTH)]
    tidx = lambda d, i: i + d * (nt - 1 - 2 * i)
    vec = pl.BlockSpec((None, 1, hd), lambda b, h, d, i: (d, 0, h))
    return pl.pallas_call(
        functools.partial(_hgrn2_kernel, nc=tt // CHUNK),
        grid=(bsz, A_HEADS, 2, nt),
        in_specs=[pl.BlockSpec((None, tt, hd), lambda b, h, d, i: (b, tidx(d, i), h)),
                  pl.BlockSpec((None, tt, hd), lambda b, h, d, i: (b, tidx(d, i), (1 + d) * A_HEADS + h)),
                  pl.BlockSpec((None, tt, hd), lambda b, h, d, i: (b, tidx(d, i), 3 * A_HEADS + h)),
                  vec, vec, vec,
                  pl.BlockSpec((None,) + ast.shape[1:], lambda b, h, d, i: (d, 0, 0)),
                  pl.BlockSpec((None,) + masks.shape[1:], lambda b, h, d, i: (d, 0, 0, 0))],
        out_specs=pl.BlockSpec((None, None, tt, hd), lambda b, h, d, i: (d, b, tidx(d, i), h)),
        out_shape=jax.ShapeDtypeStruct((2, bsz, l, A_WIDTH), jnp.float32),
        scratch_shapes=[pltpu.VMEM((hd, hd), jnp.float32),
                        pltpu.VMEM((tt // CHUNK, CHUNK, hd), jnp.bfloat16),
                        pltpu.VMEM((tt // CHUNK, HG_TOT_ROWS, hd), jnp.float32),
                        pltpu.VMEM((tt // CHUNK, hd, hd), jnp.float32),
                        pltpu.VMEM((tt // CHUNK, hd, hd), jnp.bfloat16)],
        compiler_params=pltpu.CompilerParams(dimension_semantics=("parallel", "parallel", "parallel", "arbitrary"),
                                             vmem_limit_bytes=VMEM_LIMIT_BYTES),
        name="hgrn2_scan",
    )(proj3, proj3, proj3, *vecs, ast, masks)


S5_NS = S5_GROUPS * S5_STATE
S5_TT = 128


def _s5_scan_kernel(u_ref, win_ref, ar_ref, ai_ref, wout_ref, y_ref, bu_sc, xs_sc, st_sc, *, bsz, tt, reverse):
    @pl.when(pl.program_id(0) == 0)
    def _():
        st_sc[...] = jnp.zeros_like(st_sc)

    ub = u_ref[...].astype(jnp.bfloat16)
    halves = 2
    uw = B_WIDTH // halves
    sw = S5_NS // halves
    for hf in range(halves):
        for part in range(2):
            sc = slice(part * S5_NS + hf * sw, part * S5_NS + (hf + 1) * sw)
            bu_sc[:, sc] = jnp.dot(ub[:, hf * uw:(hf + 1) * uw], win_ref[hf * uw:(hf + 1) * uw, sc],
                                   preferred_element_type=jnp.float32)
    ar = jnp.broadcast_to(ar_ref[...], (bsz, S5_NS))
    ai = jnp.broadcast_to(ai_ref[...], (bsz, S5_NS))
    per = SUBLANES // bsz
    ngroups = tt // per

    def body(s, carry):
        xr, xi = carry
        p = (ngroups - 1 - s) if reverse else s
        base = pl.multiple_of(p * SUBLANES, SUBLANES)
        blk = bu_sc[pl.ds(base, SUBLANES), :]
        outs_r = [None] * per
        outs_i = [None] * per
        for ph in (range(per - 1, -1, -1) if reverse else range(per)):
            br = blk[ph * bsz:(ph + 1) * bsz, :S5_NS]
            bi = blk[ph * bsz:(ph + 1) * bsz, S5_NS:]
            xr, xi = ar * xr - ai * xi + br, ar * xi + ai * xr + bi
            outs_r[ph] = xr
            outs_i[ph] = xi
        xs_sc[pl.ds(base, SUBLANES), :S5_NS] = jnp.concatenate(outs_r, axis=0)
        xs_sc[pl.ds(base, SUBLANES), S5_NS:] = jnp.concatenate(outs_i, axis=0)
        return xr, xi

    xr, xi = lax.fori_loop(0, ngroups, body, (st_sc[0], st_sc[1]))
    st_sc[0] = xr
    st_sc[1] = xi
    for hf in range(halves):
        yc = slice(hf * uw, (hf + 1) * uw)
        acc = None
        for part in range(2):
            sc = slice(part * S5_NS + hf * sw, part * S5_NS + (hf + 1) * sw)
            term = jnp.dot(xs_sc[:, sc].astype(jnp.bfloat16), wout_ref[sc, yc], preferred_element_type=jnp.float32)
            acc = term if acc is None else acc + term
        y_ref[:, yc] = acc


def s5_scan(u_tb, win, ar, ai, wout, *, bsz, reverse):
    n = u_tb.shape[0]
    rows = S5_TT * bsz
    nt = n // rows
    assert n % rows == 0 and SUBLANES % bsz == 0
    idx = (lambda i: (nt - 1 - i, 0)) if reverse else (lambda i: (i, 0))
    const = lambda i: (0, 0)
    return pl.pallas_call(
        functools.partial(_s5_scan_kernel, bsz=bsz, tt=S5_TT, reverse=reverse),
        grid=(nt,),
        in_specs=[pl.BlockSpec((rows, B_WIDTH), idx),
                  pl.BlockSpec((B_WIDTH, 2 * S5_NS), const),
                  pl.BlockSpec((1, S5_NS), const),
                  pl.BlockSpec((1, S5_NS), const),
                  pl.BlockSpec((2 * S5_NS, B_WIDTH), const)],
        out_specs=pl.BlockSpec((rows, B_WIDTH), idx),
        out_shape=jax.ShapeDtypeStruct((n, B_WIDTH), jnp.float32),
        scratch_shapes=[pltpu.VMEM((rows, 2 * S5_NS), jnp.float32),
                        pltpu.VMEM((rows, 2 * S5_NS), jnp.float32),
                        pltpu.VMEM((2, bsz, S5_NS), jnp.float32)],
        compiler_params=pltpu.CompilerParams(dimension_semantics=("arbitrary",),
                                             vmem_limit_bytes=VMEM_LIMIT_BYTES),
        name="s5_scan_bwd" if reverse else "s5_scan_fwd",
    )(u_tb, win, ar, ai, wout)


def _s5_final_kernel(u_ref, yf_ref, yb_ref, d_ref, w_ref, b_ref, o_ref):
    y = d_ref[...] * u_ref[...] + yf_ref[...] + yb_ref[...]
    y = jax.nn.gelu(y)
    z = jnp.dot(y.astype(jnp.bfloat16), w_ref[...], preferred_element_type=jnp.float32) + b_ref[...]
    o_ref[...] = y * jax.nn.sigmoid(z)


def s5_finalize(u, yf, yb, d_skip, glu_w, glu_b, *, tm=512):
    n, w = u.shape
    row = pl.BlockSpec((tm, w), lambda i: (i, 0))
    vec = pl.BlockSpec((1, w), lambda i: (0, 0))
    return pl.pallas_call(
        _s5_final_kernel,
        grid=(n // tm,),
        in_specs=[row, row, row, vec, pl.BlockSpec((w, w), lambda i: (0, 0)), vec],
        out_specs=row,
        out_shape=jax.ShapeDtypeStruct((n, w), jnp.float32),
        compiler_params=pltpu.CompilerParams(dimension_semantics=("parallel",),
                                             vmem_limit_bytes=VMEM_LIMIT_BYTES),
        name="s5_finalize",
    )(u, yf, yb, d_skip.reshape(1, w).astype(jnp.float32), glu_w.astype(jnp.bfloat16),
      glu_b.reshape(1, w).astype(jnp.float32))


def s5_direction_params(lam_re, lam_im, log_step, b_re, b_im, c_re, c_im):
    step = jnp.exp(log_step)[:, None]
    mag = jnp.exp(lam_re * step)
    abar_re = mag * jnp.cos(lam_im * step)
    abar_im = mag * jnp.sin(lam_im * step)
    den = lam_re * lam_re + lam_im * lam_im
    fr = ((abar_re - 1.0) * lam_re + abar_im * lam_im) / den
    fi = (abar_im * lam_re - (abar_re - 1.0) * lam_im) / den
    bb_re = fr[..., None] * b_re - fi[..., None] * b_im
    bb_im = fr[..., None] * b_im + fi[..., None] * b_re
    eye = jnp.eye(S5_GROUPS, dtype=jnp.float32)
    win = jnp.concatenate([jnp.einsum('gnp,gh->gphn', bb, eye).reshape(B_WIDTH, S5_NS) for bb in (bb_re, bb_im)],
                          axis=1)
    wout = jnp.concatenate([jnp.einsum('gpn,gh->hngp', c, eye).reshape(S5_NS, B_WIDTH) for c in (c_re, -c_im)],
                           axis=0)
    return (win.astype(jnp.bfloat16), abar_re.reshape(1, S5_NS), abar_im.reshape(1, S5_NS),
            wout.astype(jnp.bfloat16))


def s5_mixer_tb(u_tb, bsz, lam_re, lam_im, log_step, b_re, b_im, c_re, c_im, d_skip, glu_w, glu_b):
    f32 = jnp.float32
    ys = []
    for direction in range(2):
        prm = s5_direction_params(lam_re[direction].astype(f32), lam_im[direction].astype(f32),
                                  log_step[direction].astype(f32), b_re[direction].astype(f32),
                                  b_im[direction].astype(f32), c_re[direction].astype(f32),
                                  c_im[direction].astype(f32))
        ys.append(s5_scan(u_tb, *prm, bsz=bsz, reverse=(direction == 1)))
    return s5_finalize(u_tb, ys[0], ys[1], d_skip, glu_w, glu_b)


def t5_bucket(rel):
    half = REL_BUCKETS // 2
    max_exact = half // 2
    base = jnp.where(rel > 0, half, 0)
    n = jnp.abs(rel)
    nf = jnp.maximum(n, 1).astype(jnp.float32)
    large = max_exact + (jnp.log(nf / max_exact) / math.log(REL_MAX_DIST / max_exact)
                         * (half - max_exact)).astype(jnp.int32)
    large = jnp.minimum(large, half - 1)
    return base + jnp.where(n < max_exact, n, large)


ATT_T = 512
LOG2E = math.log2(math.e)


def rel_bias_tiles(rel_bias, t):
    assert t >= REL_MAX_DIST
    table = rel_bias.astype(jnp.float32) * LOG2E
    tiles = []
    for d in (-1, 0, 1):
        c = table[t5_bucket(d * t + jnp.arange(-(t - 1), t))]
        w = jnp.concatenate([c, c[:1]], axis=0)
        m = jnp.tile(w, (t, 1))[:t * (2 * t - 1)].reshape(t, 2 * t - 1, -1)
        tiles.append(m[:, t - 1:2 * t - 1])
    far_neg = jnp.broadcast_to(table[t5_bucket(jnp.array(-2 * t))], tiles[0].shape)
    far_pos = jnp.broadcast_to(table[t5_bucket(jnp.array(2 * t))], tiles[0].shape)
    out = jnp.stack([far_neg] + tiles + [far_pos], axis=0)
    return jnp.transpose(out, (3, 0, 1, 2))


def _attn_operands(q, k, v, qg, kg):
    f32 = jnp.float32
    bf16 = jnp.bfloat16
    lane = lax.broadcasted_iota(jnp.int32, q.shape, 1)
    lo = lane < C_HEAD_DIM

    def half_sums(sq):
        return (jnp.sum(jnp.where(lo, sq, 0.0), axis=-1, keepdims=True),
                jnp.sum(jnp.where(lo, 0.0, sq), axis=-1, keepdims=True))

    def halfnorm(x, g):
        s_lo, s_hi = half_sums(x * x)
        return x * lax.rsqrt(jnp.where(lo, s_lo, s_hi) * (1.0 / C_HEAD_DIM) + EPS) * g

    def max_sq_norms(xb):
        n_lo, n_hi = half_sums(xb.astype(f32) * xb.astype(f32))
        return jnp.max(n_lo, axis=0, keepdims=True), jnp.max(n_hi, axis=0, keepdims=True)

    qn = halfnorm(q, qg) * (C_HEAD_DIM ** -0.5 * LOG2E)
    kn = halfnorm(k, kg)
    qb = qn.astype(bf16)
    kb = kn.astype(bf16)
    q2 = (jnp.where(lo, qb, 0.0).astype(bf16), jnp.where(lo, 0.0, qb).astype(bf16))
    q_lo, q_hi = max_sq_norms(qb)
    k_lo, k_hi = max_sq_norms(kb)
    sub = lax.broadcasted_iota(jnp.int32, (SUBLANES, q.shape[1]), 0)
    stats = jnp.where(sub == 0, q_lo, jnp.where(sub == 1, q_hi, jnp.where(sub == 2, k_lo,
                      jnp.where(sub == 3, k_hi, 0.0))))
    return q2, kn.T.astype(bf16), v.astype(bf16), stats


def _odd_proj_kernel(x_ref, g_ref, w_ref, qg_ref, kg_ref, o_ref, q2_ref, kt_ref, vb_ref, st_ref):
    x = x_ref[...]
    y = (x * lax.rsqrt(jnp.mean(x * x, axis=-1, keepdims=True) + EPS) * g_ref[...]).astype(jnp.bfloat16)
    hw = 2 * C_HEAD_DIM
    q, k, v = (jnp.dot(y, w_ref[:, p * C_WIDTH:(p + 1) * C_WIDTH], preferred_element_type=jnp.float32)
               for p in range(3))
    for h in range(C_HEADS):
        cols = slice(h * hw, (h + 1) * hw)
        q2, kt, vb, stats = _attn_operands(q[:, cols], k[:, cols], v[:, cols], qg_ref[...], kg_ref[...])
        q2_ref[h, 0] = q2[0]
        q2_ref[h, 1] = q2[1]
        kt_ref[h] = kt
        vb_ref[h] = vb
        st_ref[h] = stats
    att = 3 * C_WIDTH
    for c0 in range(att, w_ref.shape[1], PROJ_COLS):
        c1 = min(c0 + PROJ_COLS, w_ref.shape[1])
        o_ref[:, c0 - att:c1 - att] = jnp.dot(y, w_ref[:, c0:c1], preferred_element_type=jnp.float32)


def odd_in_proj(x, gain, w, q_gain, k_gain, *, tm=PROJ_TM):
    bsz, l, kdim = x.shape
    m = w.shape[1]
    rest = m - 3 * C_WIDTH
    tm = min(tm, l)
    hw = 2 * C_HEAD_DIM
    gq = jnp.tile(q_gain.astype(jnp.float32), 2).reshape(1, hw)
    gk = jnp.tile(k_gain.astype(jnp.float32), 2).reshape(1, hw)
    vec = pl.BlockSpec((1, hw), lambda b, i: (0, 0))
    return pl.pallas_call(
        _odd_proj_kernel,
        grid=(bsz, l // tm),
        in_specs=[pl.BlockSpec((None, tm, kdim), lambda b, i: (b, i, 0)),
                  pl.BlockSpec((1, kdim), lambda b, i: (0, 0)),
                  pl.BlockSpec((kdim, m), lambda b, i: (0, 0), pipeline_mode=pl.Buffered(1)),
                  vec, vec],
        out_specs=[pl.BlockSpec((None, tm, rest), lambda b, i: (b, i, 0)),
                   pl.BlockSpec((None, C_HEADS, 2, tm, hw), lambda b, i: (b, 0, 0, i, 0)),
                   pl.BlockSpec((None, C_HEADS, hw, tm), lambda b, i: (b, 0, 0, i)),
                   pl.BlockSpec((None, C_HEADS, tm, hw), lambda b, i: (b, 0, i, 0)),
                   pl.BlockSpec((None, C_HEADS, None, SUBLANES, hw), lambda b, i: (b, 0, i, 0, 0))],
        out_shape=[jax.ShapeDtypeStruct((bsz, l, rest), jnp.float32),
                   jax.ShapeDtypeStruct((bsz, C_HEADS, 2, l, hw), jnp.bfloat16),
                   jax.ShapeDtypeStruct((bsz, C_HEADS, hw, l), jnp.bfloat16),
                   jax.ShapeDtypeStruct((bsz, C_HEADS, l, hw), jnp.bfloat16),
                   jax.ShapeDtypeStruct((bsz, C_HEADS, l // tm, SUBLANES, hw), jnp.float32)],
        compiler_params=pltpu.CompilerParams(dimension_semantics=("parallel", "parallel"),
                                             vmem_limit_bytes=VMEM_LIMIT_BYTES),
        name="odd_in_proj",
    )(x, gain.reshape(1, kdim).astype(jnp.float32), w.astype(jnp.bfloat16), gq, gk)


ATT_ROWS = 64
ATT_KEY_TILES = 4
ATT_SAFE_GAP = 100.0


def _attn_kernel(lam_ref, kmax_ref, bmax_ref, q2_ref, kt_ref, v_ref, bias_ref, g_ref, o_ref,
                 m_sc, l_sc, acc_sc, s_sc, p_sc, a_sc, *, t, nk, ktiles, out_scale, bounded):
    f32 = jnp.float32
    b = pl.program_id(0)
    h = pl.program_id(1)
    qi = pl.program_id(2)
    q2 = q2_ref[...].reshape(2 * t, 2 * C_HEAD_DIM)
    r = ATT_ROWS
    hw = 2 * C_HEAD_DIM
    if bounded:
        q2f = q2.astype(f32)
        nq = jnp.sqrt(jnp.sum(q2f * q2f, axis=-1, keepdims=True))
        row = lax.broadcasted_iota(jnp.int32, nq.shape, 0)
        kc = jnp.where(row < t, kmax_ref[(b * C_HEADS + h) * 2], kmax_ref[(b * C_HEADS + h) * 2 + 1])
        m_sc[...] = jnp.broadcast_to(nq * kc + bmax_ref[h], m_sc.shape)
    else:
        m_sc[...] = jnp.full(m_sc.shape, -jnp.inf, f32)
    l_sc[...] = jnp.zeros_like(l_sc)
    acc_sc[...] = jnp.zeros_like(acc_sc)

    kw = ktiles * t

    def body(ki, carry):
        off = pl.multiple_of(ki * kw, kw)
        bidx = [jnp.clip(ki * ktiles + c - qi, -2, 2) + 2 for c in range(ktiles)]
        s_sc[...] = jnp.dot(q2, kt_ref[:, pl.ds(off, kw)], preferred_element_type=f32)
        for g in range(2 * t // r):
            rows = slice(g * r, (g + 1) * r)
            brow = (g * r) % t
            m = m_sc[rows, :]
            s = [s_sc[rows, c * t:(c + 1) * t] + bias_ref[bidx[c], brow:brow + r, :] for c in range(ktiles)]
            if not bounded:
                m_prev = m
                smax = functools.reduce(jnp.maximum, [jnp.max(x, axis=-1, keepdims=True) for x in s])
                m = jnp.maximum(m_prev, smax)
                alpha = jnp.exp2(m_prev - m)
                m_sc[rows, :] = m
                a_sc[rows, :] = alpha
            ps = [jnp.exp2(x[:, j * hw:(j + 1) * hw] - m) for x in s for j in range(t // hw)]
            psum = jnp.sum(sum(ps), axis=-1, keepdims=True)
            l_sc[rows, :] = (l_sc[rows, :] if bounded else alpha * l_sc[rows, :]) + psum
            for j in range(kw // hw):
                p_sc[rows, j * hw:(j + 1) * hw] = ps[j].astype(jnp.bfloat16)
        pv = jnp.dot(p_sc[...], v_ref[pl.ds(off, kw), :], preferred_element_type=f32)
        acc_sc[...] = (acc_sc[...] if bounded else a_sc[...] * acc_sc[...]) + pv
        return carry

    lax.fori_loop(0, nk // ktiles, body, 0)
    a = acc_sc[...] / l_sc[...]
    o = a[:t] - lam_ref[0] * a[t:]
    y = o * lax.rsqrt(jnp.mean(o * o, axis=-1, keepdims=True) + EPS)
    o_ref[...] = y * g_ref[...] * out_scale


def diff_attention(q2, kt, vb, stats, lam, out_gain, bias5, layer_idx):
    f32 = jnp.float32
    bsz, _, l, _ = vb.shape
    t = ATT_T
    hw = 2 * C_HEAD_DIM
    lam_init = 0.8 - 0.6 * math.exp(-0.3 * layer_idx)
    lam_f = lam.astype(f32)
    lam_full = jnp.exp(jnp.sum(lam_f[0] * lam_f[1])) - jnp.exp(jnp.sum(lam_f[2] * lam_f[3])) + lam_init
    norms = jnp.sqrt(jnp.max(stats[..., 0:4, 0], axis=2)) * (1.0 + 1e-3)
    qmax, kmax = norms[..., 0:2], norms[..., 2:4]
    bmax = jnp.max(bias5, axis=(1, 2, 3))
    bmin = jnp.min(bias5, axis=(1, 2, 3))
    gap = 2.0 * qmax * kmax + (bmax - bmin)[None, :, None]
    smem = pl.BlockSpec(memory_space=pltpu.SMEM)
    ktiles = math.gcd(ATT_KEY_TILES, l // t)

    def run(bounded):
        return pl.pallas_call(
            functools.partial(_attn_kernel, t=t, nk=l // t, ktiles=ktiles, out_scale=1.0 - lam_init,
                              bounded=bounded),
            grid=(bsz, C_HEADS, l // t),
            in_specs=[smem, smem, smem,
                      pl.BlockSpec((None, None, 2, t, hw), lambda b, h, i: (b, h, 0, i, 0)),
                      pl.BlockSpec((None, None, hw, l), lambda b, h, i: (b, h, 0, 0)),
                      pl.BlockSpec((None, None, l, hw), lambda b, h, i: (b, h, 0, 0)),
                      pl.BlockSpec((None, 5, t, t), lambda b, h, i: (h, 0, 0, 0)),
                      pl.BlockSpec((1, hw), lambda b, h, i: (0, 0))],
            out_specs=pl.BlockSpec((None, t, hw), lambda b, h, i: (b, i, h)),
            out_shape=jax.ShapeDtypeStruct((bsz, l, C_WIDTH), f32),
            scratch_shapes=[pltpu.VMEM((2 * t, hw), f32), pltpu.VMEM((2 * t, hw), f32), pltpu.VMEM((2 * t, hw), f32),
                            pltpu.VMEM((2 * t, ktiles * t), f32), pltpu.VMEM((2 * t, ktiles * t), jnp.bfloat16),
                            pltpu.VMEM((2 * t, hw), f32)],
            compiler_params=pltpu.CompilerParams(dimension_semantics=("parallel", "parallel", "arbitrary"),
                                                 vmem_limit_bytes=VMEM_LIMIT_BYTES),
            name="diff_attention_bounded" if bounded else "diff_attention_online",
        )(lam_full.reshape(1), kmax.reshape(-1), bmax, q2, kt, vb, bias5, out_gain.reshape(1, hw).astype(f32))

    return lax.cond(jnp.all(gap < ATT_SAFE_GAP), lambda: run(True), lambda: run(False))


GDN_TT = 512
GDN_HEADS_PER_STEP = 4
OD_QKV_BLOCK = 0
OD_GATE_BLOCK = OD_QKV_BLOCK + 3 * D_WIDTH // LANES
OD_AB_BLOCK = OD_GATE_BLOCK + D_WIDTH // LANES
OD_COLS = 2304


def _gdn_prep_kernel(prev_ref, cur_ref, next_ref, w_ref, o_ref, *, tl, nl):
    i = pl.program_id(1)
    part = pl.program_id(2)
    prev = jnp.where(i > 0, prev_ref[...], 0.0)
    nxt = jnp.where(i < nl - 1, next_ref[...], 0.0)
    ext = jnp.concatenate([prev, cur_ref[...], nxt], axis=0)
    halo = prev.shape[0]
    acc = None
    for j in range(CONV_WIDTH):
        start = halo - CONV_WIDTH // 2 + j
        term = w_ref[j:j + 1, :] * ext[start:start + tl, :]
        acc = term if acc is None else acc + term
    y = acc * jax.nn.sigmoid(acc)
    scale = jnp.where(part == 0, D_HEAD_DIM ** -0.5, 1.0)
    heads = []
    for h in range(D_HEADS):
        yh = y[:, h * LANES:(h + 1) * LANES]
        heads.append(yh * (lax.rsqrt(jnp.sum(yh * yh, axis=-1, keepdims=True) + EPS) * scale))
    o_ref[...] = jnp.where(part < 2, jnp.concatenate(heads, axis=1), y)


def gdn_prep(proj3, conv_w, *, tl=1024):
    bsz, l, _ = proj3.shape
    halo = SUBLANES
    nl = l // tl
    blk0 = OD_QKV_BLOCK * LANES // D_WIDTH
    return pl.pallas_call(
        functools.partial(_gdn_prep_kernel, tl=tl, nl=nl),
        grid=(bsz, nl, 3),
        in_specs=[pl.BlockSpec((None, halo, D_WIDTH), lambda b, i, p: (b, jnp.maximum(i * (tl // halo) - 1, 0), blk0 + p)),
                  pl.BlockSpec((None, tl, D_WIDTH), lambda b, i, p: (b, i, blk0 + p)),
                  pl.BlockSpec((None, halo, D_WIDTH),
                               lambda b, i, p: (b, jnp.minimum((i + 1) * (tl // halo), l // halo - 1), blk0 + p)),
                  pl.BlockSpec((CONV_WIDTH, D_WIDTH), lambda b, i, p: (0, p))],
        out_specs=pl.BlockSpec((None, None, tl, D_WIDTH), lambda b, i, p: (p, b, i, 0)),
        out_shape=jax.ShapeDtypeStruct((3, bsz, l, D_WIDTH), jnp.float32),
        compiler_params=pltpu.CompilerParams(dimension_semantics=("parallel", "parallel", "parallel"),
                                             vmem_limit_bytes=VMEM_LIMIT_BYTES),
        name="gdn_prep",
    )(proj3, proj3, proj3, conv_w.astype(jnp.float32))


def _gdn_gates_kernel(x_ref, nega_ref, dtb_ref, o_ref):
    x = x_ref[...]
    z = x + dtb_ref[...]
    g = nega_ref[...] * (jnp.maximum(z, 0.0) + jnp.log1p(jnp.exp(-jnp.abs(z))))
    lane = lax.broadcasted_iota(jnp.int32, x.shape, 1)
    y = jnp.where(lane < 2 * D_HEADS, g, jax.nn.sigmoid(x))
    o_ref[...] = y.T[0:4 * D_HEADS, :]


def gdn_gates(proj3, a_log, dt_bias, *, tl=512):
    bsz, l, _ = proj3.shape
    pad = LANES - 2 * D_HEADS
    nega = jnp.pad(-jnp.exp(a_log.astype(jnp.float32)).reshape(1, -1), ((0, 0), (0, pad)))
    dtb = jnp.pad(dt_bias.astype(jnp.float32).reshape(1, -1), ((0, 0), (0, pad)))
    vec = pl.BlockSpec((1, LANES), lambda b, i: (0, 0))
    return pl.pallas_call(
        _gdn_gates_kernel,
        grid=(bsz, l // tl),
        in_specs=[pl.BlockSpec((None, tl, LANES), lambda b, i: (b, i, OD_AB_BLOCK)), vec, vec],
        out_specs=pl.BlockSpec((None, 4 * D_HEADS, tl), lambda b, i: (b, 0, i)),
        out_shape=jax.ShapeDtypeStruct((bsz, 4 * D_HEADS, l), jnp.float32),
        compiler_params=pltpu.CompilerParams(dimension_semantics=("parallel", "parallel"),
                                             vmem_limit_bytes=VMEM_LIMIT_BYTES),
        name="gdn_gates",
    )(proj3, nega, dtb)


def gdn_constants():
    c = CHUNK
    r = np.arange(c)[:, None]
    u = np.arange(c)[None, :]
    cum, incl, strict = [], [], []
    for direction in range(2):
        fwd = direction == 0
        cum.append(np.concatenate([(r <= u) if fwd else (r >= u), np.ones((c, c), bool)], axis=1))
        incl.append((u <= r) if fwd else (u >= r))
        strict.append((u < r) if fwd else (u > r))
    same = lambda b: (r // b) == (u // b)
    merges = [same(2 * b) & ~same(b) for b in (8, 16, 32)]
    f32 = jnp.float32
    return (jnp.asarray(np.stack(cum), jnp.bfloat16), jnp.asarray(np.stack(incl), f32),
            jnp.asarray(np.stack(strict), f32), jnp.asarray(same(8), f32), jnp.asarray(np.stack(merges), f32))


def _gdn_kernel(q_ref, k_ref, v_ref, g_ref, b_ref, cum_ref, incl_ref, strict_ref, d8_ref, mrg_ref, o_ref,
                s_sc, qd_sc, dec_sc, w_sc, u_sc, sin_sc, *, nc):
    direction = pl.program_id(2)

    @pl.when(pl.program_id(3) == 0)
    def _():
        s_sc[...] = jnp.zeros_like(s_sc)

    bf16 = jnp.bfloat16
    f32 = jnp.float32
    c = CHUNK
    hd = D_HEAD_DIM
    contract_last = (((1,), (1,)), ((), ()))
    contract_first = (((0,), (0,)), ((), ()))
    cumm = cum_ref[...]
    incl = incl_ref[...]
    strict = strict_ref[...]
    d8 = d8_ref[...]
    eye = (lax.broadcasted_iota(jnp.int32, (c, c), 0) == lax.broadcasted_iota(jnp.int32, (c, c), 1)).astype(f32)

    def mm(a, b):
        return jnp.dot(a.astype(bf16), b.astype(bf16), preferred_element_type=f32)

    def rep(x):
        return jnp.concatenate([x] * (hd // c), axis=1)

    nh = g_ref.shape[0]
    ns = range(nh * nc)
    head = [m // nc for m in ns]
    rows = [slice((m % nc) * c, (m % nc + 1) * c) for m in ns]
    cols = [slice(h * hd, (h + 1) * hd) for h in head]
    q = [q_ref[rows[m], cols[m]] for m in ns]
    k = [k_ref[rows[m], cols[m]] for m in ns]
    v = [v_ref[rows[m], cols[m]] for m in ns]
    kb = [x.astype(bf16) for x in k]
    kk = [lax.dot_general(x, x, contract_last, preferred_element_type=f32) for x in kb]
    qk = [lax.dot_general(q[n].astype(bf16), kb[n], contract_last, preferred_element_type=f32) for n in ns]
    grow = [jnp.broadcast_to(g_ref[head[m], :, rows[m]], (c, c)) for m in ns]
    ghi = [x.astype(bf16) for x in grow]
    glo = [(grow[n] - ghi[n].astype(f32)).astype(bf16) for n in ns]
    gm = [jnp.dot(ghi[n], cumm, preferred_element_type=f32) + jnp.dot(glo[n], cumm, preferred_element_type=f32)
          for n in ns]
    gam_row = [x[:, :c] for x in gm]
    tot = [x[:, c:] for x in gm]
    gam_col = [x.T for x in gam_row]
    beta_col = [jnp.broadcast_to(b_ref[head[m], :, rows[m]], (c, c)).T for m in ns]
    decay = [incl * jnp.exp(jnp.minimum(gam_col[n] - gam_row[n], 0.0)) for n in ns]
    a = [strict * beta_col[n] * kk[n] * decay[n] for n in ns]
    a0 = [x * d8 for x in a]
    n2 = [mm(x, x) for x in a0]
    n4 = [mm(x, x) for x in n2]
    t = [mm(eye - a0[n], eye + n2[n]) for n in ns]
    t = [mm(t[n], eye + n4[n]) for n in ns]
    for j in range(mrg_ref.shape[0]):
        p = [mm(a[n] * mrg_ref[j], t[n]) for n in ns]
        t = [t[n] - mm(t[n], p[n]) for n in ns]
    beta128 = [rep(x) for x in beta_col]
    egam128 = [rep(jnp.exp(x)) for x in gam_col]
    solb = [mm(t[n], jnp.concatenate([k[n] * beta128[n] * egam128[n], v[n] * beta128[n]], axis=1)).astype(bf16)
            for n in ns]
    av = [jnp.dot((qk[n] * decay[n]).astype(bf16), solb[n], preferred_element_type=f32) for n in ns]
    k_dec = [(k[n] * rep(jnp.exp(tot[n] - gam_col[n]))).astype(bf16) for n in ns]
    wu = [lax.dot_general(k_dec[n], solb[n], contract_first, preferred_element_type=f32) for n in ns]
    for n in ns:
        qd_sc[n] = (q[n] * egam128[n] - av[n][:, :hd]).astype(bf16)
        o_ref[rows[n], cols[n]] = av[n][:, hd:]
        w_sc[n] = wu[n][:, :hd].astype(bf16)
        u_sc[n] = wu[n][:, hd:]
        dec_sc[n] = rep(jnp.exp(tot[n][0:SUBLANES, :]))

    def body(ci, states):
        ce = ci + direction * (nc - 1 - 2 * ci)
        new = []
        for h in range(nh):
            m = h * nc + ce
            sb = states[h].astype(bf16)
            sin_sc[m] = sb
            new.append(states[h] * dec_sc[m][0:1] - jnp.dot(w_sc[m], sb, preferred_element_type=f32) + u_sc[m])
        return tuple(new)

    states = lax.fori_loop(0, nc, body, tuple(s_sc[h] for h in range(nh)))
    for h in range(nh):
        s_sc[h] = states[h]

    for m in ns:
        o_ref[rows[m], cols[m]] += jnp.dot(qd_sc[m], sin_sc[m], preferred_element_type=f32)


def gdn_scan(qkv, gb):
    _, bsz, l, _ = qkv.shape
    hd = D_HEAD_DIM
    tt = min(GDN_TT, l)
    nt = l // tt
    nc = tt // CHUNK
    assert l % tt == 0 and tt % CHUNK == 0
    consts = gdn_constants()
    gb4 = gb.reshape(bsz, 4 * D_HEADS, 1, l)
    nh = GDN_HEADS_PER_STEP
    assert D_HEADS % nh == 0
    tidx = lambda d, i: i + d * (nt - 1 - 2 * i)
    qkv_spec = lambda p: pl.BlockSpec((None, None, tt, nh * hd), lambda b, h, d, i: (p, b, tidx(d, i), h))
    row_spec = lambda off: pl.BlockSpec((None, nh, 1, tt),
                                        lambda b, h, d, i: (b, (off + d * D_HEADS) // nh + h, 0, tidx(d, i)))
    per_dir = lambda a: pl.BlockSpec((None,) + a.shape[1:], lambda b, h, d, i: (d,) + (0,) * (a.ndim - 1))
    whole = lambda a: pl.BlockSpec(a.shape, lambda b, h, d, i: (0,) * a.ndim)
    return pl.pallas_call(
        functools.partial(_gdn_kernel, nc=nc),
        grid=(bsz, D_HEADS // nh, 2, nt),
        in_specs=[qkv_spec(0), qkv_spec(1), qkv_spec(2), row_spec(0), row_spec(2 * D_HEADS),
                  per_dir(consts[0]), per_dir(consts[1]), per_dir(consts[2]), whole(consts[3]), whole(consts[4])],
        out_specs=pl.BlockSpec((None, None, tt, nh * hd), lambda b, h, d, i: (d, b, tidx(d, i), h)),
        out_shape=jax.ShapeDtypeStruct((2, bsz, l, D_WIDTH), jnp.float32),
        scratch_shapes=[pltpu.VMEM((nh, hd, hd), jnp.float32),
                        pltpu.VMEM((nh * nc, CHUNK, hd), jnp.bfloat16),
                        pltpu.VMEM((nh * nc, SUBLANES, hd), jnp.float32),
                        pltpu.VMEM((nh * nc, hd, hd), jnp.bfloat16),
                        pltpu.VMEM((nh * nc, hd, hd), jnp.float32),
                        pltpu.VMEM((nh * nc, hd, hd), jnp.bfloat16)],
        compiler_params=pltpu.CompilerParams(dimension_semantics=("parallel", "parallel", "parallel", "arbitrary"),
                                             vmem_limit_bytes=VMEM_LIMIT_BYTES),
        name="gdn_scan",
    )(qkv, qkv, qkv, gb4, gb4, *consts)


def gated_deltanet(proj3, conv_w, a_log, dt_bias):
    return gdn_scan(gdn_prep(proj3, conv_w), gdn_gates(proj3, a_log, dt_bias))


MOE_TT = 512
MOE_SUB = 128
MOE_ALIGN = 64
MOE_TILES_PER_STEP = 4
MOE_EXPERTS_PER_COMBINE_STEP = 4
MOE_ROWS = 256
MOE_SLAB = 256
MOE_VMEM_LIMIT_BYTES = 56 * 1024 * 1024
MOE_EXPERT_VMEM_LIMIT_BYTES = 60 * 1024 * 1024


def _router_kernel(x_ref, g_ref, wr_ref, h_ref, aff_ref):
    x = x_ref[...]
    h = (x * lax.rsqrt(jnp.mean(x * x, axis=-1, keepdims=True) + EPS) * g_ref[...]).astype(jnp.bfloat16)
    h_ref[...] = h
    logits = jnp.dot(h, wr_ref[...], preferred_element_type=jnp.float32)
    lane = lax.broadcasted_iota(jnp.int32, logits.shape, 1)
    logits = jnp.where(lane < N_EXPERTS, logits, -jnp.inf)
    p = jnp.exp(logits - jnp.max(logits, axis=-1, keepdims=True))
    aff = p / jnp.sum(p, axis=-1, keepdims=True)
    aff_ref[...] = aff.T[0:N_EXPERTS, :]


def moe_route(x, gain, w_router, *, tm=MOE_TT):
    bsz, l, d = x.shape
    wr = jnp.pad(w_router.astype(jnp.bfloat16), ((0, 0), (0, LANES - N_EXPERTS)))
    return pl.pallas_call(
        _router_kernel,
        grid=(bsz, l // tm),
        in_specs=[pl.BlockSpec((None, tm, d), lambda b, i: (b, i, 0)),
                  pl.BlockSpec((1, d), lambda b, i: (0, 0)),
                  pl.BlockSpec((d, LANES), lambda b, i: (0, 0))],
        out_specs=[pl.BlockSpec((None, tm, d), lambda b, i: (b, i, 0)),
                   pl.BlockSpec((None, N_EXPERTS, tm), lambda b, i: (b, 0, i))],
        out_shape=[jax.ShapeDtypeStruct((bsz, l, d), jnp.bfloat16),
                   jax.ShapeDtypeStruct((bsz, N_EXPERTS, l), jnp.float32)],
        compiler_params=pltpu.CompilerParams(dimension_semantics=("parallel", "parallel"),
                                             vmem_limit_bytes=VMEM_LIMIT_BYTES),
        name="moe_router",
    )(x, gain.reshape(1, d).astype(jnp.float32), wr)


def _select_kernel(aff_ref, pre_ref, smap_ref, gate_ref, cnt_ref, *, cap, tt):
    f32 = jnp.float32
    bf16 = jnp.bfloat16
    aff = aff_ref[...]
    e, l = aff.shape
    nl = l // LANES
    tiles = [slice(j * LANES, (j + 1) * LANES) for j in range(nl)]
    bits = pltpu.bitcast(aff, jnp.int32)
    bt = [bits[:, s] for s in tiles]

    def lane_total(x):
        return jnp.broadcast_to(jnp.sum(x, axis=-1, keepdims=True), (e, LANES))

    def search(i, thr):
        cand = thr | jnp.left_shift(jnp.int32(1), 30 - i)
        acc = jnp.zeros((e, LANES), jnp.int32)
        for x in bt:
            acc = acc + (x >= cand).astype(jnp.int32)
        return jnp.where(lane_total(acc) >= cap, cand, thr)

    thr = lax.fori_loop(0, 31, search, jnp.zeros((e, LANES), jnp.int32))
    gt = [x > thr for x in bt]
    eq = [x == thr for x in bt]
    acc = jnp.zeros((e, LANES), jnp.int32)
    for x in gt:
        acc = acc + x.astype(jnp.int32)
    need = (cap - lane_total(acc)).astype(f32)

    pre = pre_ref[...]

    def prefix(flags):
        outs = [jnp.dot(jnp.where(x, 1.0, 0.0).astype(bf16), pre, preferred_element_type=f32) for x in flags]
        carry = jnp.zeros((e, LANES), f32)
        res = []
        for o in outs:
            res.append(o[:, :LANES] + carry)
            carry = carry + o[:, LANES:]
        return res, [o[:, LANES:] for o in outs]

    rank_eq, _ = prefix(eq)
    sel = [jnp.logical_or(gt[j], jnp.logical_and(eq[j], rank_eq[j] < need)) for j in range(nl)]
    pos, totals = prefix(sel)
    lane = lax.broadcasted_iota(jnp.int32, (e, LANES), 1)
    cnt = jnp.zeros((e, LANES), f32)
    per = tt // LANES
    for j in range(nl):
        smap_ref[:, tiles[j]] = jnp.where(sel[j], pos[j], -1.0)
        gate_ref[:, tiles[j]] = jnp.where(sel[j], aff[:, tiles[j]], 0.0)
        cnt = cnt + jnp.where(lane == j // per, totals[j], 0.0)
    cnt_ref[...] = cnt


def moe_select(aff, cap, *, tt=MOE_TT):
    bsz, e, l = aff.shape
    assert l // tt <= LANES
    i = np.arange(LANES)
    pre = np.concatenate([i[:, None] < i[None, :], np.ones((LANES, LANES), bool)], axis=1)
    row = pl.BlockSpec((None, e, l), lambda b: (b, 0, 0))
    return pl.pallas_call(
        functools.partial(_select_kernel, cap=cap, tt=tt),
        grid=(bsz,),
        in_specs=[row, pl.BlockSpec((LANES, 2 * LANES), lambda b: (0, 0))],
        out_specs=[row, row, pl.BlockSpec((None, e, LANES), lambda b: (b, 0, 0))],
        out_shape=[jax.ShapeDtypeStruct((bsz, e, l), jnp.float32), jax.ShapeDtypeStruct((bsz, e, l), jnp.float32),
                   jax.ShapeDtypeStruct((bsz, e, LANES), jnp.float32)],
        compiler_params=pltpu.CompilerParams(dimension_semantics=("parallel",),
                                             vmem_limit_bytes=VMEM_LIMIT_BYTES),
        name="moe_select",
    )(aff, jnp.asarray(pre, jnp.bfloat16))


def _slot_one_hot(pos, base, rows, n):
    slot = (base + lax.broadcasted_iota(jnp.int32, (rows, n), 0)).astype(jnp.float32)
    return jnp.where(pos == slot, 1.0, 0.0).astype(jnp.bfloat16)


def _expert_kernel(cs_ref, h_ref, smap_ref, gate_ref, wg32_ref, wu32_ref, wd32_ref, o_ref,
                   xs_sc, gs_sc, wg_ref, wu_ref, wd_ref, *, nj, per, tt, cap):
    e = pl.program_id(0)
    b = pl.program_id(1)
    j = pl.program_id(2)
    f32 = jnp.float32
    bf16 = jnp.bfloat16

    @pl.when(jnp.logical_and(b == 0, j == 0))
    def _():
        wg_ref[...] = wg32_ref[...].astype(bf16)
        wu_ref[...] = wu32_ref[...].astype(bf16)
        wd_ref[...] = wd32_ref[...].astype(bf16)

    @pl.when(j == 0)
    def _():
        xs_sc[...] = jnp.zeros_like(xs_sc)
        gs_sc[...] = jnp.zeros_like(gs_sc)

    win = min(MOE_SUB, cap)
    for s in range(per):
        base = (b * N_EXPERTS + e) * (nj * per + 1) + j * per + s
        c0 = cs_ref[base]
        c1 = cs_ref[base + 1]
        cols = slice(s * tt, (s + 1) * tt)
        pos = smap_ref[pl.ds(e % SUBLANES, 1), cols]
        gate = gate_ref[pl.ds(e % SUBLANES, 1), cols]
        hb = h_ref[cols, :]

        def gather(r0, rows, pos=pos, gate=gate, hb=hb):
            oh = _slot_one_hot(pos, r0, rows, tt)
            xs_sc[pl.ds(r0, rows), :] += jnp.dot(oh, hb, preferred_element_type=f32)
            g = jnp.sum(oh.astype(f32) * gate, axis=-1, keepdims=True)
            gs_sc[pl.ds(r0, rows), :] += jnp.broadcast_to(g, (rows, LANES))

        w0 = pl.multiple_of(jnp.minimum(c0 // MOE_ALIGN * MOE_ALIGN, cap - win), MOE_ALIGN)
        gather(w0, win)

        def rest(st, carry, gather=gather):
            gather(pl.multiple_of(st * MOE_ALIGN, MOE_ALIGN), MOE_ALIGN)
            return carry

        lax.fori_loop((w0 + win) // MOE_ALIGN, (c1 + MOE_ALIGN - 1) // MOE_ALIGN, rest, 0)

    @pl.when(j == nj - 1)
    def _():
        rows_per = min(MOE_ROWS, cap)
        for r in range(cap // rows_per):
            rows = slice(r * rows_per, (r + 1) * rows_per)
            xb = xs_sc[rows, :].astype(bf16)
            g = jnp.dot(xb, wg_ref[...], preferred_element_type=f32)
            u = jnp.dot(xb, wu_ref[...], preferred_element_type=f32)
            hid = (g * jax.nn.sigmoid(g) * u).astype(bf16)
            out = jnp.dot(hid, wd_ref[...], preferred_element_type=f32)
            scale = jnp.concatenate([gs_sc[rows, :]] * (out.shape[1] // LANES), axis=1)
            o_ref[rows, :] = (out * scale).astype(bf16)


def moe_experts(hb, smap, gate, cs, w_gate, w_up, w_down, layer, cap, *, tt=MOE_TT):
    bsz, l, d = hb.shape
    _, e, _, ff = w_gate.shape
    per = MOE_TILES_PER_STEP if (l // tt) % MOE_TILES_PER_STEP == 0 else 1
    nj = l // (tt * per)
    tok = pl.BlockSpec((None, SUBLANES, per * tt), lambda ei, b, j, cs_ref: (b, ei // SUBLANES, j))
    once = pl.Buffered(1)
    grid_spec = pltpu.PrefetchScalarGridSpec(
        num_scalar_prefetch=1,
        grid=(e, bsz, nj),
        in_specs=[pl.BlockSpec((None, per * tt, d), lambda ei, b, j, cs_ref: (b, j, 0)), tok, tok,
                  pl.BlockSpec((None, None, d, ff), lambda ei, b, j, cs_ref: (layer, ei, 0, 0), pipeline_mode=once),
                  pl.BlockSpec((None, None, d, ff), lambda ei, b, j, cs_ref: (layer, ei, 0, 0), pipeline_mode=once),
                  pl.BlockSpec((None, None, ff, d), lambda ei, b, j, cs_ref: (layer, ei, 0, 0), pipeline_mode=once)],
        out_specs=pl.BlockSpec((None, None, cap, d), lambda ei, b, j, cs_ref: (b, ei, 0, 0)),
        scratch_shapes=[pltpu.VMEM((cap, d), jnp.float32), pltpu.VMEM((cap, LANES), jnp.float32),
                        pltpu.VMEM((d, ff), jnp.bfloat16), pltpu.VMEM((d, ff), jnp.bfloat16),
                        pltpu.VMEM((ff, d), jnp.bfloat16)])
    return pl.pallas_call(
        functools.partial(_expert_kernel, nj=nj, per=per, tt=tt, cap=cap),
        grid_spec=grid_spec,
        out_shape=jax.ShapeDtypeStruct((bsz, e, cap, d), jnp.bfloat16),
        compiler_params=pltpu.CompilerParams(dimension_semantics=("parallel", "arbitrary", "arbitrary"),
                                             vmem_limit_bytes=MOE_EXPERT_VMEM_LIMIT_BYTES),
        name="moe_experts",
    )(cs, hb, smap, gate, w_gate, w_up, w_down)


def _combine_kernel(cs_ref, x_ref, smap_ref, ow_ref, y_ref, *, nj, tt):
    b = pl.program_id(0)
    eg = pl.program_id(2)

    @pl.when(eg == 0)
    def _():
        y_ref[...] = x_ref[...]

    contract_first = (((0,), (0,)), ((), ()))
    per, cap = ow_ref.shape[0:2]
    win = min(2 * MOE_SUB, cap)
    cols = [slice(j * tt, (j + 1) * tt) for j in range(nj)]
    for k in range(per):
        e = eg * per + k
        base = (b * N_EXPERTS + e) * (nj + 1)
        pos = [smap_ref[pl.ds(e % SUBLANES, 1), c] for c in cols]
        r0 = [pl.multiple_of(jnp.minimum(cs_ref[base + j] // MOE_SUB * MOE_SUB, cap - win), MOE_SUB)
              for j in range(nj)]
        oh = [_slot_one_hot(pos[j], r0[j], win, tt) for j in range(nj)]
        add = [lax.dot_general(oh[j], ow_ref[k, pl.ds(r0[j], win), :], contract_first,
                               preferred_element_type=jnp.float32) for j in range(nj)]
        for j in range(nj):
            y_ref[cols[j], :] += add[j]

        for j in range(nj):
            def scatter(st, carry, j=j, k=k, pos=pos):
                s0 = pl.multiple_of(st * MOE_SUB, MOE_SUB)
                y_ref[cols[j], :] += lax.dot_general(_slot_one_hot(pos[j], s0, MOE_SUB, tt),
                                                     ow_ref[k, pl.ds(s0, MOE_SUB), :], contract_first,
                                                     preferred_element_type=jnp.float32)
                return carry

            lax.fori_loop((r0[j] + win) // MOE_SUB, (cs_ref[base + j + 1] + MOE_SUB - 1) // MOE_SUB, scatter, 0)


def moe_combine(x, smap, outw, cs, *, tt=MOE_TT):
    bsz, l, d = x.shape
    e, cap = outw.shape[1:3]
    nj = l // tt
    per = MOE_EXPERTS_PER_COMBINE_STEP
    assert e % per == 0 and SUBLANES % per == 0
    grid_spec = pltpu.PrefetchScalarGridSpec(
        num_scalar_prefetch=1,
        grid=(bsz, d // MOE_SLAB, e // per),
        in_specs=[pl.BlockSpec((None, l, MOE_SLAB), lambda b, s, eg, cs_ref: (b, 0, s)),
                  pl.BlockSpec((None, SUBLANES, l), lambda b, s, eg, cs_ref: (b, eg * per // SUBLANES, 0)),
                  pl.BlockSpec((None, per, cap, MOE_SLAB), lambda b, s, eg, cs_ref: (b, eg, 0, s))],
        out_specs=pl.BlockSpec((None, l, MOE_SLAB), lambda b, s, eg, cs_ref: (b, 0, s)))
    return pl.pallas_call(
        functools.partial(_combine_kernel, nj=nj, tt=tt),
        grid_spec=grid_spec,
        out_shape=jax.ShapeDtypeStruct((bsz, l, d), jnp.float32),
        compiler_params=pltpu.CompilerParams(dimension_semantics=("parallel", "parallel", "arbitrary"),
                                             vmem_limit_bytes=MOE_VMEM_LIMIT_BYTES),
        name="moe_combine",
    )(cs, x, smap, outw)


def ec_moe_layer(x, gain, w_router, w_gate, w_up, w_down, layer):
    bsz, l, d = x.shape
    cap = EC_CAPACITY_FACTOR * l // N_EXPERTS
    tt = min(MOE_TT, l)
    nj = l // tt
    hb, aff = moe_route(x, gain, w_router, tm=tt)
    smap, gate, cnt = moe_select(aff, cap, tt=tt)
    cs = jnp.concatenate([jnp.zeros((bsz, N_EXPERTS, 1), jnp.float32), jnp.cumsum(cnt[..., :nj], axis=-1)], axis=-1)
    cs = cs.astype(jnp.int32).reshape(-1)
    outw = moe_experts(hb, smap, gate, cs, w_gate, w_up, w_down, layer, cap, tt=tt)
    return moe_combine(x, smap, outw, cs, tt=tt)


def kernel(x, mix_norm, ffn_norm, ev_w_in, ev_w_out, a_lb_logits, a_out_norm, s5_lambda_re, s5_lambda_im, s5_log_step, s5_b_re, s5_b_im, s5_c_re, s5_c_im, s5_d, s5_glu_w, s5_glu_b, od_w_in, od_w_out, c_q_norm, c_k_norm, c_lambda, c_out_norm, rel_bias, d_conv_w, d_a_log, d_dt_bias, d_out_norm, moe_router, moe_w_gate, moe_w_up, moe_w_down):
    bsz, l, d = x.shape
    p = jax.nn.softmax(a_lb_logits.astype(jnp.float32), axis=0)
    cum = jnp.cumsum(p, axis=0)
    lower_bounds = cum - cum[0:1]
    bias5 = rel_bias_tiles(rel_bias, ATT_T)
    for layer in range(DEPTH):
        j = layer // 2
        if layer % 2 == 0:
            proj, u_tb = norm_matmul(x, mix_norm[layer], ev_w_in[j], tail=B_WIDTH)
            o_a2 = hgrn2_scan(proj, lower_bounds[j])
            o_b = s5_mixer_tb(u_tb.reshape(l * bsz, B_WIDTH), bsz, s5_lambda_re[j], s5_lambda_im[j], s5_log_step[j],
                              s5_b_re[j], s5_b_im[j], s5_c_re[j], s5_c_im[j], s5_d[j], s5_glu_w[j], s5_glu_b[j])
            x = mixer_out_proj(o_a2, proj, 4 * A_HEADS, a_out_norm[j], o_b.reshape(l, bsz * B_WIDTH), ev_w_out[j], x,
                               bidir_first=True, other_time_major=True)
        else:
            o2 = 3 * C_WIDTH + 3 * D_WIDTH
            o4 = o2 + 4 * D_HEADS
            w = od_w_in[j]
            w_in = jnp.concatenate([w[:, :o2], w[:, o4:], w[:, o2:o4],
                                    jnp.zeros((d, 3 * C_WIDTH + OD_COLS - w.shape[1]), w.dtype)], axis=1)
            proj, q2, kt, vb, stats = odd_in_proj(x, mix_norm[layer], w_in, c_q_norm[j], c_k_norm[j])
            o_c = diff_attention(q2, kt, vb, stats, c_lambda[j], c_out_norm[j], bias5, layer)
            o_d2 = gated_deltanet(proj, d_conv_w[j], d_a_log[j], d_dt_bias[j])
            x = mixer_out_proj(o_d2, proj, OD_GATE_BLOCK, d_out_norm[j], o_c, od_w_out[j], x, bidir_first=False)
        x = ec_moe_layer(x, ffn_norm[layer], moe_router[layer], moe_w_gate, moe_w_up, moe_w_down, layer)
    return x
```

```python
import functools
import math

import jax
import jax.numpy as jnp
import numpy as np
from jax import lax
from jax.experimental import pallas as pl
from jax.experimental.pallas import tpu as pltpu

D_MODEL = 1024
DEPTH = 4
MIX_WIDTH = D_MODEL
A_WIDTH = MIX_WIDTH // 2
A_HEAD_DIM = 128
A_HEADS = A_WIDTH // A_HEAD_DIM
B_WIDTH = MIX_WIDTH - A_WIDTH
S5_GROUP = 16
S5_GROUPS = B_WIDTH // S5_GROUP
S5_STATE = 64
C_WIDTH = MIX_WIDTH // 2
C_HEAD_DIM = 64
C_HEADS = C_WIDTH // (2 * C_HEAD_DIM)
D_WIDTH = MIX_WIDTH - C_WIDTH
D_HEAD_DIM = 128
D_HEADS = D_WIDTH // D_HEAD_DIM
CONV_WIDTH = 5
N_EXPERTS = 16
EC_CAPACITY_FACTOR = 2
REL_BUCKETS = 32
REL_MAX_DIST = 128
CHUNK = 64
EPS = 1e-6

LANES = 128
SUBLANES = 8
VMEM_LIMIT_BYTES = 48 * 1024 * 1024


PROJ_TM = 512
PROJ_COLS = 512


def _norm_matmul_kernel(x_ref, g_ref, w_ref, o_ref, *tail_ref, main):
    x = x_ref[...]
    y = (x * lax.rsqrt(jnp.mean(x * x, axis=-1, keepdims=True) + EPS) * g_ref[...]).astype(jnp.bfloat16)
    for c0 in range(0, main, PROJ_COLS):
        c1 = min(c0 + PROJ_COLS, main)
        o_ref[:, c0:c1] = jnp.dot(y, w_ref[:, c0:c1], preferred_element_type=jnp.float32)
    if tail_ref:
        tail_ref[0][...] = jnp.dot(y, w_ref[:, main:], preferred_element_type=jnp.float32)


def norm_matmul(x, gain, w, *, tail=0, tm=PROJ_TM):
    bsz, l, k = x.shape
    m = w.shape[1]
    main = m - tail
    tm = min(tm, l)
    out_shape = [jax.ShapeDtypeStruct((bsz, l, main), jnp.float32)]
    out_specs = [pl.BlockSpec((None, tm, main), lambda b, i: (b, i, 0))]
    if tail:
        out_shape.append(jax.ShapeDtypeStruct((l, bsz * tail), jnp.float32))
        out_specs.append(pl.BlockSpec((tm, tail), lambda b, i: (i, b)))
    outs = pl.pallas_call(
        functools.partial(_norm_matmul_kernel, main=main),
        grid=(bsz, l // tm),
        in_specs=[pl.BlockSpec((None, tm, k), lambda b, i: (b, i, 0)),
                  pl.BlockSpec((1, k), lambda b, i: (0, 0)),
                  pl.BlockSpec((k, m), lambda b, i: (0, 0), pipeline_mode=pl.Buffered(1))],
        out_specs=out_specs,
        out_shape=out_shape,
        compiler_params=pltpu.CompilerParams(dimension_semantics=("parallel", "parallel"),
                                             vmem_limit_bytes=VMEM_LIMIT_BYTES),
        name="norm_matmul",
    )(x, gain.reshape(1, k).astype(jnp.float32), w.astype(jnp.bfloat16))
    return outs if tail else outs[0]


def _mixer_out_kernel(of_ref, ob_ref, g_ref, gain_ref, other_ref, wb_ref, wo_ref, r_ref, o_ref):
    bf16 = jnp.bfloat16
    o = of_ref[...] + ob_ref[...]
    g = g_ref[...]
    gate = g * jax.nn.sigmoid(g)
    hd = gain_ref.shape[1]
    heads = []
    for h in range(o.shape[1] // hd):
        oh = o[:, h * hd:(h + 1) * hd]
        heads.append(oh * lax.rsqrt(jnp.mean(oh * oh, axis=-1, keepdims=True) + EPS) * gain_ref[...])
    y = (jnp.concatenate(heads, axis=1) * gate).astype(bf16)
    o_ref[...] = (r_ref[...] + jnp.dot(y, wb_ref[...], preferred_element_type=jnp.float32)
                  + jnp.dot(other_ref[...].astype(bf16), wo_ref[...], preferred_element_type=jnp.float32))


def mixer_out_proj(o2, proj3, gate_block, out_gain, other, w, res, *, bidir_first, other_time_major=False,
                   tm=PROJ_TM):
    _, bsz, l, k = o2.shape
    m = w.shape[1]
    tm = min(tm, l)
    wb = w.astype(jnp.bfloat16)
    w_bidir, w_other = (wb[:k], wb[k:]) if bidir_first else (wb[k:], wb[:k])
    other_spec = (pl.BlockSpec((tm, k), lambda b, i: (i, b)) if other_time_major
                  else pl.BlockSpec((None, tm, k), lambda b, i: (b, i, 0)))
    gb = gate_block * LANES // k
    row = pl.BlockSpec((None, tm, m), lambda b, i: (b, i, 0))
    wspec = pl.BlockSpec((k, m), lambda b, i: (0, 0))
    return pl.pallas_call(
        _mixer_out_kernel,
        grid=(bsz, l // tm),
        in_specs=[pl.BlockSpec((None, None, tm, k), lambda b, i: (0, b, i, 0)),
                  pl.BlockSpec((None, None, tm, k), lambda b, i: (1, b, i, 0)),
                  pl.BlockSpec((None, tm, k), lambda b, i: (b, i, gb)),
                  pl.BlockSpec((1, LANES), lambda b, i: (0, 0)),
                  other_spec, wspec, wspec, row],
        out_specs=row,
        out_shape=jax.ShapeDtypeStruct((bsz, l, m), jnp.float32),
        compiler_params=pltpu.CompilerParams(dimension_semantics=("parallel", "parallel"),
                                             vmem_limit_bytes=VMEM_LIMIT_BYTES),
        name="mixer_out_proj",
    )(o2, o2, proj3, out_gain.reshape(1, LANES).astype(jnp.float32), other, w_bidir, w_other, res)


HG_LEVELS = tuple(CHUNK >> (i + 1) for i in range(CHUNK.bit_length() - 1))
HG_TOT_ROWS = 8
HG_TT = 1024


def hgrn2_constants():
    c = CHUNK
    r = np.arange(c)[:, None]
    u = np.arange(c)[None, :]
    stacks, masks = [], []
    for direction in range(2):
        fwd = direction == 0
        lvl_masks = []
        for m in HG_LEVELS:
            blk = r // (2 * m)
            later = (r % (2 * m)) >= m
            lvl_masks.append((blk == blk.T) & (later & ~later.T if fwd else ~later & later.T))
        stacks.append(np.concatenate([(u <= r) if fwd else (u >= r), np.ones((HG_TOT_ROWS, c), bool)], axis=0))
        masks.append(np.stack(lvl_masks))
    return (jnp.asarray(np.stack(stacks), jnp.bfloat16), jnp.asarray(np.stack(masks), jnp.float32))


def _hgrn2_kernel(q_ref, f_ref, v_ref, loglb_ref, log1mlb_ref, onemlb_ref, ast_ref, mask_ref, o_ref,
                  st_sc, qd_sc, dec_sc, upd_sc, sin_sc, *, nc):
    direction = pl.program_id(2)

    @pl.when(pl.program_id(3) == 0)
    def _():
        st_sc[...] = jnp.zeros_like(st_sc)

    bf16 = jnp.bfloat16
    f32 = jnp.float32
    c = CHUNK
    hd = A_HEAD_DIM
    dirf = direction.astype(f32)
    loglb = loglb_ref[...]
    log1mlb = log1mlb_ref[...]
    onemlb = onemlb_ref[...]
    ast = ast_ref[...]
    contract_last = (((1,), (1,)), ((), ()))
    contract_first = (((0,), (0,)), ((), ()))

    ns = range(nc)
    rows = [slice(n * c, (n + 1) * c) for n in ns]
    z = [f_ref[r, :] for r in rows]
    v = [v_ref[r, :] for r in rows]
    qr = [q_ref[r, :] for r in rows]
    q = [x * jax.nn.sigmoid(x) for x in qr]
    e = [jnp.exp(-jnp.abs(x)) for x in z]
    cc = [log1mlb + jnp.minimum(z[n], 0.0) - jnp.log1p(e[n]) for n in ns]
    lf = [jnp.maximum(loglb, x) + jnp.log1p(jnp.exp(-jnp.abs(loglb - x))) for x in cc]
    k = [onemlb * jnp.where(z[n] >= 0, e[n], 1.0) / (1.0 + e[n]) for n in ns]
    hi = [x.astype(bf16) for x in lf]
    lo = [(lf[n] - hi[n].astype(f32)).astype(bf16) for n in ns]
    d = [jnp.dot(ast, hi[n], preferred_element_type=f32) + jnp.dot(ast, lo[n], preferred_element_type=f32)
         for n in ns]
    cum = [x[0:c] for x in d]
    tot = [x[c:c + HG_TOT_ROWS] for x in d]
    ref = [cum[n] - dirf * lf[n] for n in ns]
    attn = [jnp.zeros((c, c), f32) for _ in ns]
    for li, m in enumerate(HG_LEVELS):
        nb = c // (2 * m)
        split = [jnp.broadcast_to(x.reshape(nb, 2 * m, hd)[:, m - 1:m, :], (nb, 2 * m, hd)).reshape(c, hd)
                 for x in ref]
        x = [jnp.exp(-jnp.abs(cum[n] - split[n])) for n in ns]
        s = [lax.dot_general((q[n] * x[n]).astype(bf16), (k[n] * x[n]).astype(bf16), contract_last,
                             preferred_element_type=f32) for n in ns]
        attn = [attn[n] + mask_ref[li] * s[n] for n in ns]
    vb = [x.astype(bf16) for x in v]
    intra = [jnp.dot(attn[n].astype(bf16), vb[n], preferred_element_type=f32) for n in ns]
    upd = [lax.dot_general(vb[n], (k[n] * jnp.exp(tot[n][0:1] - cum[n])).astype(bf16), contract_first,
                           preferred_element_type=f32) for n in ns]
    for n in ns:
        o_ref[rows[n], :] = intra[n] + jnp.sum(q[n] * k[n], axis=-1, keepdims=True) * v[n]
        qd_sc[n] = (q[n] * jnp.exp(cum[n])).astype(bf16)
        dec_sc[n] = jnp.exp(tot[n])
        upd_sc[n] = upd[n]

    def body(ci, st):
        ce = ci + direction * (nc - 1 - 2 * ci)
        sin_sc[ce] = st.astype(bf16)
        return st * dec_sc[ce][0:1] + upd_sc[ce]

    st_sc[...] = lax.fori_loop(0, nc, body, st_sc[...])

    for n in range(nc):
        rows = slice(n * c, (n + 1) * c)
        o_ref[rows, :] += lax.dot_general(qd_sc[n], sin_sc[n], contract_last, preferred_element_type=f32)


def hgrn2_scan(proj3, lb):
    bsz, l, _ = proj3.shape
    hd = A_HEAD_DIM
    tt = min(HG_TT, l)
    nt = l // tt
    assert l % tt == 0 and tt % CHUNK == 0
    ast, masks = hgrn2_constants()
    lb = lb.astype(jnp.float32)
    vecs = [jnp.log(lb).reshape(2, 1, A_WIDTH), jnp.log1p(-lb).reshape(2, 1, A_WIDTH), (1.0 - lb).reshape(2, 1, A_WIDTH)]
    tidx = lambda d, i: i + d * (nt - 1 - 2 * i)
    vec = pl.BlockSpec((None, 1, hd), lambda b, h, d, i: (d, 0, h))
    return pl.pallas_call(
        functools.partial(_hgrn2_kernel, nc=tt // CHUNK),
        grid=(bsz, A_HEADS, 2, nt),
        in_specs=[pl.BlockSpec((None, tt, hd), lambda b, h, d, i: (b, tidx(d, i), h)),
                  pl.BlockSpec((None, tt, hd), lambda b, h, d, i: (b, tidx(d, i), (1 + d) * A_HEADS + h)),
                  pl.BlockSpec((None, tt, hd), lambda b, h, d, i: (b, tidx(d, i), 3 * A_HEADS + h)),
                  vec, vec, vec,
                  pl.BlockSpec((None,) + ast.shape[1:], lambda b, h, d, i: (d, 0, 0)),
                  pl.BlockSpec((None,) + masks.shape[1:], lambda b, h, d, i: (d, 0, 0, 0))],
        out_specs=pl.BlockSpec((None, None, tt, hd), lambda b, h, d, i: (d, b, tidx(d, i), h)),
        out_shape=jax.ShapeDtypeStruct((2, bsz, l, A_WIDTH), jnp.float32),
        scratch_shapes=[pltpu.VMEM((hd, hd), jnp.float32),
                        pltpu.VMEM((tt // CHUNK, CHUNK, hd), jnp.bfloat16),
                        pltpu.VMEM((tt // CHUNK, HG_TOT_ROWS, hd), jnp.float32),
                        pltpu.VMEM((tt // CHUNK, hd, hd), jnp.float32),
                        pltpu.VMEM((tt // CHUNK, hd, hd), jnp.bfloat16)],
        compiler_params=pltpu.CompilerParams(dimension_semantics=("parallel", "parallel", "parallel", "arbitrary"),
                                             vmem_limit_bytes=VMEM_LIMIT_BYTES),
        name="hgrn2_scan",
    )(proj3, proj3, proj3, *vecs, ast, masks)


S5_NS = S5_GROUPS * S5_STATE
S5_TT = 128


def _s5_scan_kernel(u_ref, win_ref, ar_ref, ai_ref, wout_ref, y_ref, bu_sc, xs_sc, st_sc, *, bsz, tt, reverse):
    @pl.when(pl.program_id(0) == 0)
    def _():
        st_sc[...] = jnp.zeros_like(st_sc)

    ub = u_ref[...].astype(jnp.bfloat16)
    halves = 2
    uw = B_WIDTH // halves
    sw = S5_NS // halves
    for hf in range(halves):
        for part in range(2):
            sc = slice(part * S5_NS + hf * sw, part * S5_NS + (hf + 1) * sw)
            bu_sc[:, sc] = jnp.dot(ub[:, hf * uw:(hf + 1) * uw], win_ref[hf * uw:(hf + 1) * uw, sc],
                                   preferred_element_type=jnp.float32)
    ar = jnp.broadcast_to(ar_ref[...], (bsz, S5_NS))
    ai = jnp.broadcast_to(ai_ref[...], (bsz, S5_NS))
    per = SUBLANES // bsz
    ngroups = tt // per

    def body(s, carry):
        xr, xi = carry
        p = (ngroups - 1 - s) if reverse else s
        base = pl.multiple_of(p * SUBLANES, SUBLANES)
        blk = bu_sc[pl.ds(base, SUBLANES), :]
        outs_r = [None] * per
        outs_i = [None] * per
        for ph in (range(per - 1, -1, -1) if reverse else range(per)):
            br = blk[ph * bsz:(ph + 1) * bsz, :S5_NS]
            bi = blk[ph * bsz:(ph + 1) * bsz, S5_NS:]
            xr, xi = ar * xr - ai * xi + br, ar * xi + ai * xr + bi
            outs_r[ph] = xr
            outs_i[ph] = xi
        xs_sc[pl.ds(base, SUBLANES), :S5_NS] = jnp.concatenate(outs_r, axis=0)
        xs_sc[pl.ds(base, SUBLANES), S5_NS:] = jnp.concatenate(outs_i, axis=0)
        return xr, xi

    xr, xi = lax.fori_loop(0, ngroups, body, (st_sc[0], st_sc[1]))
    st_sc[0] = xr
    st_sc[1] = xi
    for hf in range(halves):
        yc = slice(hf * uw, (hf + 1) * uw)
        acc = None
        for part in range(2):
            sc = slice(part * S5_NS + hf * sw, part * S5_NS + (hf + 1) * sw)
            term = jnp.dot(xs_sc[:, sc].astype(jnp.bfloat16), wout_ref[sc, yc], preferred_element_type=jnp.float32)
            acc = term if acc is None else acc + term
        y_ref[:, yc] = acc


def s5_scan(u_tb, win, ar, ai, wout, *, bsz, reverse):
    n = u_tb.shape[0]
    rows = S5_TT * bsz
    nt = n // rows
    assert n % rows == 0 and SUBLANES % bsz == 0
    idx = (lambda i: (nt - 1 - i, 0)) if reverse else (lambda i: (i, 0))
    const = lambda i: (0, 0)
    return pl.pallas_call(
        functools.partial(_s5_scan_kernel, bsz=bsz, tt=S5_TT, reverse=reverse),
        grid=(nt,),
        in_specs=[pl.BlockSpec((rows, B_WIDTH), idx),
                  pl.BlockSpec((B_WIDTH, 2 * S5_NS), const),
                  pl.BlockSpec((1, S5_NS), const),
                  pl.BlockSpec((1, S5_NS), const),
                  pl.BlockSpec((2 * S5_NS, B_WIDTH), const)],
        out_specs=pl.BlockSpec((rows, B_WIDTH), idx),
        out_shape=jax.ShapeDtypeStruct((n, B_WIDTH), jnp.float32),
        scratch_shapes=[pltpu.VMEM((rows, 2 * S5_NS), jnp.float32),
                        pltpu.VMEM((rows, 2 * S5_NS), jnp.float32),
                        pltpu.VMEM((2, bsz, S5_NS), jnp.float32)],
        compiler_params=pltpu.CompilerParams(dimension_semantics=("arbitrary",),
                                             vmem_limit_bytes=VMEM_LIMIT_BYTES),
        name="s5_scan_bwd" if reverse else "s5_scan_fwd",
    )(u_tb, win, ar, ai, wout)


def _s5_final_kernel(u_ref, yf_ref, yb_ref, d_ref, w_ref, b_ref, o_ref):
    y = d_ref[...] * u_ref[...] + yf_ref[...] + yb_ref[...]
    y = jax.nn.gelu(y)
    z = jnp.dot(y.astype(jnp.bfloat16), w_ref[...], preferred_element_type=jnp.float32) + b_ref[...]
    o_ref[...] = y * jax.nn.sigmoid(z)


def s5_finalize(u, yf, yb, d_skip, glu_w, glu_b, *, tm=2048):
    n, w = u.shape
    tm = min(tm, n)
    row = pl.BlockSpec((tm, w), lambda i: (i, 0))
    vec = pl.BlockSpec((1, w), lambda i: (0, 0))
    return pl.pallas_call(
        _s5_final_kernel,
        grid=(n // tm,),
        in_specs=[row, row, row, vec, pl.BlockSpec((w, w), lambda i: (0, 0)), vec],
        out_specs=row,
        out_shape=jax.ShapeDtypeStruct((n, w), jnp.float32),
        compiler_params=pltpu.CompilerParams(dimension_semantics=("parallel",),
                                             vmem_limit_bytes=VMEM_LIMIT_BYTES),
        name="s5_finalize",
    )(u, yf, yb, d_skip.reshape(1, w).astype(jnp.float32), glu_w.astype(jnp.bfloat16),
      glu_b.reshape(1, w).astype(jnp.float32))


def s5_direction_params(lam_re, lam_im, log_step, b_re, b_im, c_re, c_im):
    step = jnp.exp(log_step)[:, None]
    mag = jnp.exp(lam_re * step)
    abar_re = mag * jnp.cos(lam_im * step)
    abar_im = mag * jnp.sin(lam_im * step)
    den = lam_re * lam_re + lam_im * lam_im
    fr = ((abar_re - 1.0) * lam_re + abar_im * lam_im) / den
    fi = (abar_im * lam_re - (abar_re - 1.0) * lam_im) / den
    bb_re = fr[..., None] * b_re - fi[..., None] * b_im
    bb_im = fr[..., None] * b_im + fi[..., None] * b_re
    eye = jnp.eye(S5_GROUPS, dtype=jnp.float32)
    win = jnp.concatenate([jnp.einsum('gnp,gh->gphn', bb, eye).reshape(B_WIDTH, S5_NS) for bb in (bb_re, bb_im)],
                          axis=1)
    wout = jnp.concatenate([jnp.einsum('gpn,gh->hngp', c, eye).reshape(S5_NS, B_WIDTH) for c in (c_re, -c_im)],
                           axis=0)
    return (win.astype(jnp.bfloat16), abar_re.reshape(1, S5_NS), abar_im.reshape(1, S5_NS),
            wout.astype(jnp.bfloat16))


def s5_mixer_tb(u_tb, bsz, lam_re, lam_im, log_step, b_re, b_im, c_re, c_im, d_skip, glu_w, glu_b):
    f32 = jnp.float32
    ys = []
    for direction in range(2):
        prm = s5_direction_params(lam_re[direction].astype(f32), lam_im[direction].astype(f32),
                                  log_step[direction].astype(f32), b_re[direction].astype(f32),
                                  b_im[direction].astype(f32), c_re[direction].astype(f32),
                                  c_im[direction].astype(f32))
        ys.append(s5_scan(u_tb, *prm, bsz=bsz, reverse=(direction == 1)))
    return s5_finalize(u_tb, ys[0], ys[1], d_skip, glu_w, glu_b)


def t5_bucket(rel):
    half = REL_BUCKETS // 2
    max_exact = half // 2
    base = jnp.where(rel > 0, half, 0)
    n = jnp.abs(rel)
    nf = jnp.maximum(n, 1).astype(jnp.float32)
    large = max_exact + (jnp.log(nf / max_exact) / math.log(REL_MAX_DIST / max_exact)
                         * (half - max_exact)).astype(jnp.int32)
    large = jnp.minimum(large, half - 1)
    return base + jnp.where(n < max_exact, n, large)


ATT_T = 512
LOG2E = math.log2(math.e)


def rel_bias_tiles(rel_bias, t):
    assert t >= REL_MAX_DIST
    table = rel_bias.astype(jnp.float32) * LOG2E
    tiles = []
    for d in (-1, 0, 1):
        c = table[t5_bucket(d * t + jnp.arange(-(t - 1), t))]
        w = jnp.concatenate([c, c[:1]], axis=0)
        m = jnp.tile(w, (t, 1))[:t * (2 * t - 1)].reshape(t, 2 * t - 1, -1)
        tiles.append(m[:, t - 1:2 * t - 1])
    far_neg = jnp.broadcast_to(table[t5_bucket(jnp.array(-2 * t))], tiles[0].shape)
    far_pos = jnp.broadcast_to(table[t5_bucket(jnp.array(2 * t))], tiles[0].shape)
    out = jnp.stack([far_neg] + tiles + [far_pos], axis=0)
    return jnp.transpose(out, (3, 0, 1, 2))


def _attn_operands(q, k, v, qg, kg):
    f32 = jnp.float32
    bf16 = jnp.bfloat16
    lane = lax.broadcasted_iota(jnp.int32, q.shape, 1)
    lo = lane < C_HEAD_DIM

    def half_sums(sq):
        return (jnp.sum(jnp.where(lo, sq, 0.0), axis=-1, keepdims=True),
                jnp.sum(jnp.where(lo, 0.0, sq), axis=-1, keepdims=True))

    def halfnorm(x, g):
        s_lo, s_hi = half_sums(x * x)
        return x * lax.rsqrt(jnp.where(lo, s_lo, s_hi) * (1.0 / C_HEAD_DIM) + EPS) * g

    def max_sq_norms(xb):
        n_lo, n_hi = half_sums(xb.astype(f32) * xb.astype(f32))
        return jnp.max(n_lo, axis=0, keepdims=True), jnp.max(n_hi, axis=0, keepdims=True)

    qn = halfnorm(q, qg) * (C_HEAD_DIM ** -0.5 * LOG2E)
    kn = halfnorm(k, kg)
    qb = qn.astype(bf16)
    kb = kn.astype(bf16)
    q2 = (jnp.where(lo, qb, 0.0).astype(bf16), jnp.where(lo, 0.0, qb).astype(bf16))
    q_lo, q_hi = max_sq_norms(qb)
    k_lo, k_hi = max_sq_norms(kb)
    sub = lax.broadcasted_iota(jnp.int32, (SUBLANES, q.shape[1]), 0)
    stats = jnp.where(sub == 0, q_lo, jnp.where(sub == 1, q_hi, jnp.where(sub == 2, k_lo,
                      jnp.where(sub == 3, k_hi, 0.0))))
    return q2, kn.T.astype(bf16), v.astype(bf16), stats


def _odd_proj_kernel(x_ref, g_ref, w_ref, qg_ref, kg_ref, o_ref, q2_ref, kt_ref, vb_ref, st_ref):
    x = x_ref[...]
    y = (x * lax.rsqrt(jnp.mean(x * x, axis=-1, keepdims=True) + EPS) * g_ref[...]).astype(jnp.bfloat16)
    hw = 2 * C_HEAD_DIM
    q, k, v = (jnp.dot(y, w_ref[:, p * C_WIDTH:(p + 1) * C_WIDTH], preferred_element_type=jnp.float32)
               for p in range(3))
    for h in range(C_HEADS):
        cols = slice(h * hw, (h + 1) * hw)
        q2, kt, vb, stats = _attn_operands(q[:, cols], k[:, cols], v[:, cols], qg_ref[...], kg_ref[...])
        q2_ref[h, 0] = q2[0]
        q2_ref[h, 1] = q2[1]
        kt_ref[h] = kt
        vb_ref[h] = vb
        st_ref[h] = stats
    att = 3 * C_WIDTH
    for c0 in range(att, w_ref.shape[1], PROJ_COLS):
        c1 = min(c0 + PROJ_COLS, w_ref.shape[1])
        o_ref[:, c0 - att:c1 - att] = jnp.dot(y, w_ref[:, c0:c1], preferred_element_type=jnp.float32)


def odd_in_proj(x, gain, w, q_gain, k_gain, *, tm=PROJ_TM):
    bsz, l, kdim = x.shape
    m = w.shape[1]
    rest = m - 3 * C_WIDTH
    tm = min(tm, l)
    hw = 2 * C_HEAD_DIM
    gq = jnp.tile(q_gain.astype(jnp.float32), 2).reshape(1, hw)
    gk = jnp.tile(k_gain.astype(jnp.float32), 2).reshape(1, hw)
    vec = pl.BlockSpec((1, hw), lambda b, i: (0, 0))
    return pl.pallas_call(
        _odd_proj_kernel,
        grid=(bsz, l // tm),
        in_specs=[pl.BlockSpec((None, tm, kdim), lambda b, i: (b, i, 0)),
                  pl.BlockSpec((1, kdim), lambda b, i: (0, 0)),
                  pl.BlockSpec((kdim, m), lambda b, i: (0, 0), pipeline_mode=pl.Buffered(1)),
                  vec, vec],
        out_specs=[pl.BlockSpec((None, tm, rest), lambda b, i: (b, i, 0)),
                   pl.BlockSpec((None, C_HEADS, 2, tm, hw), lambda b, i: (b, 0, 0, i, 0)),
                   pl.BlockSpec((None, C_HEADS, hw, tm), lambda b, i: (b, 0, 0, i)),
                   pl.BlockSpec((None, C_HEADS, tm, hw), lambda b, i: (b, 0, i, 0)),
                   pl.BlockSpec((None, C_HEADS, None, SUBLANES, hw), lambda b, i: (b, 0, i, 0, 0))],
        out_shape=[jax.ShapeDtypeStruct((bsz, l, rest), jnp.float32),
                   jax.ShapeDtypeStruct((bsz, C_HEADS, 2, l, hw), jnp.bfloat16),
                   jax.ShapeDtypeStruct((bsz, C_HEADS, hw, l), jnp.bfloat16),
                   jax.ShapeDtypeStruct((bsz, C_HEADS, l, hw), jnp.bfloat16),
                   jax.ShapeDtypeStruct((bsz, C_HEADS, l // tm, SUBLANES, hw), jnp.float32)],
        compiler_params=pltpu.CompilerParams(dimension_semantics=("parallel", "parallel"),
                                             vmem_limit_bytes=VMEM_LIMIT_BYTES),
        name="odd_in_proj",
    )(x, gain.reshape(1, kdim).astype(jnp.float32), w.astype(jnp.bfloat16), gq, gk)


ATT_ROWS = 64
ATT_KEY_TILES = 4
ATT_SAFE_GAP = 100.0


def _attn_kernel(lam_ref, kmax_ref, bmax_ref, q2_ref, kt_ref, v_ref, bias_ref, g_ref, o_ref,
                 m_sc, l_sc, acc_sc, s_sc, p_sc, a_sc, *, t, nk, ktiles, out_scale, bounded):
    f32 = jnp.float32
    b = pl.program_id(0)
    h = pl.program_id(1)
    qi = pl.program_id(2)
    q2 = q2_ref[...].reshape(2 * t, 2 * C_HEAD_DIM)
    r = ATT_ROWS
    hw = 2 * C_HEAD_DIM
    if bounded:
        q2f = q2.astype(f32)
        nq = jnp.sqrt(jnp.sum(q2f * q2f, axis=-1, keepdims=True))
        row = lax.broadcasted_iota(jnp.int32, nq.shape, 0)
        kc = jnp.where(row < t, kmax_ref[(b * C_HEADS + h) * 2], kmax_ref[(b * C_HEADS + h) * 2 + 1])
        m_sc[...] = jnp.broadcast_to(nq * kc + bmax_ref[h], m_sc.shape)
    else:
        m_sc[...] = jnp.full(m_sc.shape, -jnp.inf, f32)
    l_sc[...] = jnp.zeros_like(l_sc)
    acc_sc[...] = jnp.zeros_like(acc_sc)

    kw = ktiles * t

    def body(ki, carry):
        off = pl.multiple_of(ki * kw, kw)
        bidx = [jnp.clip(ki * ktiles + c - qi, -2, 2) + 2 for c in range(ktiles)]
        s_sc[...] = jnp.dot(q2, kt_ref[:, pl.ds(off, kw)], preferred_element_type=f32)
        for g in range(2 * t // r):
            rows = slice(g * r, (g + 1) * r)
            brow = (g * r) % t
            m = m_sc[rows, :]
            s = [s_sc[rows, c * t:(c + 1) * t] + bias_ref[bidx[c], brow:brow + r, :] for c in range(ktiles)]
            if not bounded:
                m_prev = m
                smax = functools.reduce(jnp.maximum, [jnp.max(x, axis=-1, keepdims=True) for x in s])
                m = jnp.maximum(m_prev, smax)
                alpha = jnp.exp2(m_prev - m)
                m_sc[rows, :] = m
                a_sc[rows, :] = alpha
            ps = [jnp.exp2(x[:, j * hw:(j + 1) * hw] - m) for x in s for j in range(t // hw)]
            psum = jnp.sum(sum(ps), axis=-1, keepdims=True)
            l_sc[rows, :] = (l_sc[rows, :] if bounded else alpha * l_sc[rows, :]) + psum
            for j in range(kw // hw):
                p_sc[rows, j * hw:(j + 1) * hw] = ps[j].astype(jnp.bfloat16)
        pv = jnp.dot(p_sc[...], v_ref[pl.ds(off, kw), :], preferred_element_type=f32)
        acc_sc[...] = (acc_sc[...] if bounded else a_sc[...] * acc_sc[...]) + pv
        return carry

    lax.fori_loop(0, nk // ktiles, body, 0)
    a = acc_sc[...] / l_sc[...]
    o = a[:t] - lam_ref[0] * a[t:]
    y = o * lax.rsqrt(jnp.mean(o * o, axis=-1, keepdims=True) + EPS)
    o_ref[...] = y * g_ref[...] * out_scale


def diff_attention(q2, kt, vb, stats, lam, out_gain, bias5, layer_idx):
    f32 = jnp.float32
    bsz, _, l, _ = vb.shape
    t = ATT_T
    hw = 2 * C_HEAD_DIM
    lam_init = 0.8 - 0.6 * math.exp(-0.3 * layer_idx)
    lam_f = lam.astype(f32)
    lam_full = jnp.exp(jnp.sum(lam_f[0] * lam_f[1])) - jnp.exp(jnp.sum(lam_f[2] * lam_f[3])) + lam_init
    norms = jnp.sqrt(jnp.max(stats[..., 0:4, 0], axis=2)) * (1.0 + 1e-3)
    qmax, kmax = norms[..., 0:2], norms[..., 2:4]
    bmax = jnp.max(bias5, axis=(1, 2, 3))
    bmin = jnp.min(bias5, axis=(1, 2, 3))
    gap = 2.0 * qmax * kmax + (bmax - bmin)[None, :, None]
    smem = pl.BlockSpec(memory_space=pltpu.SMEM)
    ktiles = math.gcd(ATT_KEY_TILES, l // t)

    def run(bounded):
        return pl.pallas_call(
            functools.partial(_attn_kernel, t=t, nk=l // t, ktiles=ktiles, out_scale=1.0 - lam_init,
                              bounded=bounded),
            grid=(bsz, C_HEADS, l // t),
            in_specs=[smem, smem, smem,
                      pl.BlockSpec((None, None, 2, t, hw), lambda b, h, i: (b, h, 0, i, 0)),
                      pl.BlockSpec((None, None, hw, l), lambda b, h, i: (b, h, 0, 0)),
                      pl.BlockSpec((None, None, l, hw), lambda b, h, i: (b, h, 0, 0)),
                      pl.BlockSpec((None, 5, t, t), lambda b, h, i: (h, 0, 0, 0)),
                      pl.BlockSpec((1, hw), lambda b, h, i: (0, 0))],
            out_specs=pl.BlockSpec((None, t, hw), lambda b, h, i: (b, i, h)),
            out_shape=jax.ShapeDtypeStruct((bsz, l, C_WIDTH), f32),
            scratch_shapes=[pltpu.VMEM((2 * t, hw), f32), pltpu.VMEM((2 * t, hw), f32), pltpu.VMEM((2 * t, hw), f32),
                            pltpu.VMEM((2 * t, ktiles * t), f32), pltpu.VMEM((2 * t, ktiles * t), jnp.bfloat16),
                            pltpu.VMEM((2 * t, hw), f32)],
            compiler_params=pltpu.CompilerParams(dimension_semantics=("parallel", "parallel", "arbitrary"),
                                                 vmem_limit_bytes=VMEM_LIMIT_BYTES),
            name="diff_attention_bounded" if bounded else "diff_attention_online",
        )(lam_full.reshape(1), kmax.reshape(-1), bmax, q2, kt, vb, bias5, out_gain.reshape(1, hw).astype(f32))

    return lax.cond(jnp.all(gap < ATT_SAFE_GAP), lambda: run(True), lambda: run(False))


GDN_TT = 512
GDN_HEADS_PER_STEP = 4
OD_QKV_BLOCK = 0
OD_GATE_BLOCK = OD_QKV_BLOCK + 3 * D_WIDTH // LANES
OD_AB_BLOCK = OD_GATE_BLOCK + D_WIDTH // LANES
OD_COLS = 2304


def _gdn_prep_kernel(prev_ref, cur_ref, next_ref, w_ref, o_ref, *, tl, nl):
    i = pl.program_id(1)
    part = pl.program_id(2)
    prev = jnp.where(i > 0, prev_ref[...], 0.0)
    nxt = jnp.where(i < nl - 1, next_ref[...], 0.0)
    ext = jnp.concatenate([prev, cur_ref[...], nxt], axis=0)
    halo = prev.shape[0]
    acc = None
    for j in range(CONV_WIDTH):
        start = halo - CONV_WIDTH // 2 + j
        term = w_ref[j:j + 1, :] * ext[start:start + tl, :]
        acc = term if acc is None else acc + term
    y = acc * jax.nn.sigmoid(acc)
    scale = jnp.where(part == 0, D_HEAD_DIM ** -0.5, 1.0)
    heads = []
    for h in range(D_HEADS):
        yh = y[:, h * LANES:(h + 1) * LANES]
        heads.append(yh * (lax.rsqrt(jnp.sum(yh * yh, axis=-1, keepdims=True) + EPS) * scale))
    o_ref[...] = jnp.where(part < 2, jnp.concatenate(heads, axis=1), y)


def gdn_prep(proj3, conv_w, *, tl=1024):
    bsz, l, _ = proj3.shape
    halo = SUBLANES
    nl = l // tl
    blk0 = OD_QKV_BLOCK * LANES // D_WIDTH
    return pl.pallas_call(
        functools.partial(_gdn_prep_kernel, tl=tl, nl=nl),
        grid=(bsz, nl, 3),
        in_specs=[pl.BlockSpec((None, halo, D_WIDTH), lambda b, i, p: (b, jnp.maximum(i * (tl // halo) - 1, 0), blk0 + p)),
                  pl.BlockSpec((None, tl, D_WIDTH), lambda b, i, p: (b, i, blk0 + p)),
                  pl.BlockSpec((None, halo, D_WIDTH),
                               lambda b, i, p: (b, jnp.minimum((i + 1) * (tl // halo), l // halo - 1), blk0 + p)),
                  pl.BlockSpec((CONV_WIDTH, D_WIDTH), lambda b, i, p: (0, p))],
        out_specs=pl.BlockSpec((None, None, tl, D_WIDTH), lambda b, i, p: (p, b, i, 0)),
        out_shape=jax.ShapeDtypeStruct((3, bsz, l, D_WIDTH), jnp.float32),
        compiler_params=pltpu.CompilerParams(dimension_semantics=("parallel", "parallel", "parallel"),
                                             vmem_limit_bytes=VMEM_LIMIT_BYTES),
        name="gdn_prep",
    )(proj3, proj3, proj3, conv_w.astype(jnp.float32))


def _gdn_gates_kernel(x_ref, nega_ref, dtb_ref, o_ref):
    x = x_ref[...]
    z = x + dtb_ref[...]
    g = nega_ref[...] * (jnp.maximum(z, 0.0) + jnp.log1p(jnp.exp(-jnp.abs(z))))
    lane = lax.broadcasted_iota(jnp.int32, x.shape, 1)
    y = jnp.where(lane < 2 * D_HEADS, g, jax.nn.sigmoid(x))
    o_ref[...] = y.T[0:4 * D_HEADS, :]


def gdn_gates(proj3, a_log, dt_bias, *, tl=2048):
    bsz, l, _ = proj3.shape
    tl = min(tl, l)
    pad = LANES - 2 * D_HEADS
    nega = jnp.pad(-jnp.exp(a_log.astype(jnp.float32)).reshape(1, -1), ((0, 0), (0, pad)))
    dtb = jnp.pad(dt_bias.astype(jnp.float32).reshape(1, -1), ((0, 0), (0, pad)))
    vec = pl.BlockSpec((1, LANES), lambda b, i: (0, 0))
    return pl.pallas_call(
        _gdn_gates_kernel,
        grid=(bsz, l // tl),
        in_specs=[pl.BlockSpec((None, tl, LANES), lambda b, i: (b, i, OD_AB_BLOCK)), vec, vec],
        out_specs=pl.BlockSpec((None, 4 * D_HEADS, tl), lambda b, i: (b, 0, i)),
        out_shape=jax.ShapeDtypeStruct((bsz, 4 * D_HEADS, l), jnp.float32),
        compiler_params=pltpu.CompilerParams(dimension_semantics=("parallel", "parallel"),
                                             vmem_limit_bytes=VMEM_LIMIT_BYTES),
        name="gdn_gates",
    )(proj3, nega, dtb)


def gdn_constants():
    c = CHUNK
    r = np.arange(c)[:, None]
    u = np.arange(c)[None, :]
    cum, incl, strict = [], [], []
    for direction in range(2):
        fwd = direction == 0
        cum.append(np.concatenate([(r <= u) if fwd else (r >= u), np.ones((c, c), bool)], axis=1))
        incl.append((u <= r) if fwd else (u >= r))
        strict.append((u < r) if fwd else (u > r))
    same = lambda b: (r // b) == (u // b)
    merges = [same(2 * b) & ~same(b) for b in (8, 16, 32)]
    f32 = jnp.float32
    return (jnp.asarray(np.stack(cum), jnp.bfloat16), jnp.asarray(np.stack(incl), f32),
            jnp.asarray(np.stack(strict), f32), jnp.asarray(same(8), f32), jnp.asarray(np.stack(merges), f32))


def _gdn_kernel(q_ref, k_ref, v_ref, g_ref, b_ref, cum_ref, incl_ref, strict_ref, d8_ref, mrg_ref, o_ref,
                s_sc, qd_sc, dec_sc, w_sc, u_sc, sin_sc, *, nc):
    direction = pl.program_id(2)

    @pl.when(pl.program_id(3) == 0)
    def _():
        s_sc[...] = jnp.zeros_like(s_sc)

    bf16 = jnp.bfloat16
    f32 = jnp.float32
    c = CHUNK
    hd = D_HEAD_DIM
    contract_last = (((1,), (1,)), ((), ()))
    contract_first = (((0,), (0,)), ((), ()))
    cumm = cum_ref[...]
    incl = incl_ref[...]
    strict = strict_ref[...]
    d8 = d8_ref[...]
    eye = (lax.broadcasted_iota(jnp.int32, (c, c), 0) == lax.broadcasted_iota(jnp.int32, (c, c), 1)).astype(f32)

    def mm(a, b):
        return jnp.dot(a.astype(bf16), b.astype(bf16), preferred_element_type=f32)

    def rep(x):
        return jnp.concatenate([x] * (hd // c), axis=1)

    nh = g_ref.shape[0]
    ns = range(nh * nc)
    head = [m // nc for m in ns]
    rows = [slice((m % nc) * c, (m % nc + 1) * c) for m in ns]
    cols = [slice(h * hd, (h + 1) * hd) for h in head]
    q = [q_ref[rows[m], cols[m]] for m in ns]
    k = [k_ref[rows[m], cols[m]] for m in ns]
    v = [v_ref[rows[m], cols[m]] for m in ns]
    kb = [x.astype(bf16) for x in k]
    kk = [lax.dot_general(x, x, contract_last, preferred_element_type=f32) for x in kb]
    qk = [lax.dot_general(q[n].astype(bf16), kb[n], contract_last, preferred_element_type=f32) for n in ns]
    grow = [jnp.broadcast_to(g_ref[head[m], :, rows[m]], (c, c)) for m in ns]
    ghi = [x.astype(bf16) for x in grow]
    glo = [(grow[n] - ghi[n].astype(f32)).astype(bf16) for n in ns]
    gm = [jnp.dot(ghi[n], cumm, preferred_element_type=f32) + jnp.dot(glo[n], cumm, preferred_element_type=f32)
          for n in ns]
    gam_row = [x[:, :c] for x in gm]
    tot = [x[:, c:] for x in gm]
    gam_col = [x.T for x in gam_row]
    beta_col = [jnp.broadcast_to(b_ref[head[m], :, rows[m]], (c, c)).T for m in ns]
    decay = [incl * jnp.exp(jnp.minimum(gam_col[n] - gam_row[n], 0.0)) for n in ns]
    a = [strict * beta_col[n] * kk[n] * decay[n] for n in ns]
    a0 = [x * d8 for x in a]
    n2 = [mm(x, x) for x in a0]
    n4 = [mm(x, x) for x in n2]
    t = [mm(eye - a0[n], eye + n2[n]) for n in ns]
    t = [mm(t[n], eye + n4[n]) for n in ns]
    for j in range(mrg_ref.shape[0]):
        p = [mm(a[n] * mrg_ref[j], t[n]) for n in ns]
        t = [t[n] - mm(t[n], p[n]) for n in ns]
    beta128 = [rep(x) for x in beta_col]
    egam128 = [rep(jnp.exp(x)) for x in gam_col]
    solb = [mm(t[n], jnp.concatenate([k[n] * beta128[n] * egam128[n], v[n] * beta128[n]], axis=1)).astype(bf16)
            for n in ns]
    av = [jnp.dot((qk[n] * decay[n]).astype(bf16), solb[n], preferred_element_type=f32) for n in ns]
    k_dec = [(k[n] * rep(jnp.exp(tot[n] - gam_col[n]))).astype(bf16) for n in ns]
    wu = [lax.dot_general(k_dec[n], solb[n], contract_first, preferred_element_type=f32) for n in ns]
    for n in ns:
        qd_sc[n] = (q[n] * egam128[n] - av[n][:, :hd]).astype(bf16)
        o_ref[rows[n], cols[n]] = av[n][:, hd:]
        w_sc[n] = wu[n][:, :hd].astype(bf16)
        u_sc[n] = wu[n][:, hd:]
        dec_sc[n] = rep(jnp.exp(tot[n][0:SUBLANES, :]))

    def body(ci, states):
        ce = ci + direction * (nc - 1 - 2 * ci)
        new = []
        for h in range(nh):
            m = h * nc + ce
            sb = states[h].astype(bf16)
            sin_sc[m] = sb
            new.append(states[h] * dec_sc[m][0:1] - jnp.dot(w_sc[m], sb, preferred_element_type=f32) + u_sc[m])
        return tuple(new)

    states = lax.fori_loop(0, nc, body, tuple(s_sc[h] for h in range(nh)))
    for h in range(nh):
        s_sc[h] = states[h]

    for m in ns:
        o_ref[rows[m], cols[m]] += jnp.dot(qd_sc[m], sin_sc[m], preferred_element_type=f32)


def gdn_scan(qkv, gb):
    _, bsz, l, _ = qkv.shape
    hd = D_HEAD_DIM
    tt = min(GDN_TT, l)
    nt = l // tt
    nc = tt // CHUNK
    assert l % tt == 0 and tt % CHUNK == 0
    consts = gdn_constants()
    gb4 = gb.reshape(bsz, 4 * D_HEADS, 1, l)
    nh = GDN_HEADS_PER_STEP
    assert D_HEADS % nh == 0
    tidx = lambda d, i: i + d * (nt - 1 - 2 * i)
    qkv_spec = lambda p: pl.BlockSpec((None, None, tt, nh * hd), lambda b, h, d, i: (p, b, tidx(d, i), h))
    row_spec = lambda off: pl.BlockSpec((None, nh, 1, tt),
                                        lambda b, h, d, i: (b, (off + d * D_HEADS) // nh + h, 0, tidx(d, i)))
    per_dir = lambda a: pl.BlockSpec((None,) + a.shape[1:], lambda b, h, d, i: (d,) + (0,) * (a.ndim - 1))
    whole = lambda a: pl.BlockSpec(a.shape, lambda b, h, d, i: (0,) * a.ndim)
    return pl.pallas_call(
        functools.partial(_gdn_kernel, nc=nc),
        grid=(bsz, D_HEADS // nh, 2, nt),
        in_specs=[qkv_spec(0), qkv_spec(1), qkv_spec(2), row_spec(0), row_spec(2 * D_HEADS),
                  per_dir(consts[0]), per_dir(consts[1]), per_dir(consts[2]), whole(consts[3]), whole(consts[4])],
        out_specs=pl.BlockSpec((None, None, tt, nh * hd), lambda b, h, d, i: (d, b, tidx(d, i), h)),
        out_shape=jax.ShapeDtypeStruct((2, bsz, l, D_WIDTH), jnp.float32),
        scratch_shapes=[pltpu.VMEM((nh, hd, hd), jnp.float32),
                        pltpu.VMEM((nh * nc, CHUNK, hd), jnp.bfloat16),
                        pltpu.VMEM((nh * nc, SUBLANES, hd), jnp.float32),
                        pltpu.VMEM((nh * nc, hd, hd), jnp.bfloat16),
                        pltpu.VMEM((nh * nc, hd, hd), jnp.float32),
                        pltpu.VMEM((nh * nc, hd, hd), jnp.bfloat16)],
        compiler_params=pltpu.CompilerParams(dimension_semantics=("parallel", "parallel", "parallel", "arbitrary"),
                                             vmem_limit_bytes=VMEM_LIMIT_BYTES),
        name="gdn_scan",
    )(qkv, qkv, qkv, gb4, gb4, *consts)


def gated_deltanet(proj3, conv_w, a_log, dt_bias):
    return gdn_scan(gdn_prep(proj3, conv_w), gdn_gates(proj3, a_log, dt_bias))


MOE_TT = 512
MOE_SUB = 128
MOE_ALIGN = 64
MOE_TILES_PER_STEP = 4
MOE_EXPERTS_PER_COMBINE_STEP = 8
MOE_ROWS = 256
MOE_SLAB = 256
MOE_VMEM_LIMIT_BYTES = 56 * 1024 * 1024
MOE_EXPERT_VMEM_LIMIT_BYTES = 60 * 1024 * 1024


def _router_kernel(x_ref, g_ref, wr_ref, h_ref, aff_ref):
    x = x_ref[...]
    h = (x * lax.rsqrt(jnp.mean(x * x, axis=-1, keepdims=True) + EPS) * g_ref[...]).astype(jnp.bfloat16)
    h_ref[...] = h
    logits = jnp.dot(h, wr_ref[...], preferred_element_type=jnp.float32)
    lane = lax.broadcasted_iota(jnp.int32, logits.shape, 1)
    logits = jnp.where(lane < N_EXPERTS, logits, -jnp.inf)
    p = jnp.exp(logits - jnp.max(logits, axis=-1, keepdims=True))
    aff = p / jnp.sum(p, axis=-1, keepdims=True)
    aff_ref[...] = aff.T[0:N_EXPERTS, :]


def moe_route(x, gain, w_router, *, tm=MOE_TT):
    bsz, l, d = x.shape
    wr = jnp.pad(w_router.astype(jnp.bfloat16), ((0, 0), (0, LANES - N_EXPERTS)))
    return pl.pallas_call(
        _router_kernel,
        grid=(bsz, l // tm),
        in_specs=[pl.BlockSpec((None, tm, d), lambda b, i: (b, i, 0)),
                  pl.BlockSpec((1, d), lambda b, i: (0, 0)),
                  pl.BlockSpec((d, LANES), lambda b, i: (0, 0))],
        out_specs=[pl.BlockSpec((None, tm, d), lambda b, i: (b, i, 0)),
                   pl.BlockSpec((None, N_EXPERTS, tm), lambda b, i: (b, 0, i))],
        out_shape=[jax.ShapeDtypeStruct((bsz, l, d), jnp.bfloat16),
                   jax.ShapeDtypeStruct((bsz, N_EXPERTS, l), jnp.float32)],
        compiler_params=pltpu.CompilerParams(dimension_semantics=("parallel", "parallel"),
                                             vmem_limit_bytes=VMEM_LIMIT_BYTES),
        name="moe_router",
    )(x, gain.reshape(1, d).astype(jnp.float32), wr)


def _select_kernel(aff_ref, pre_ref, smap_ref, gate_ref, cnt_ref, *, cap, tt):
    f32 = jnp.float32
    bf16 = jnp.bfloat16
    aff = aff_ref[...]
    e, l = aff.shape
    nl = l // LANES
    tiles = [slice(j * LANES, (j + 1) * LANES) for j in range(nl)]
    bits = pltpu.bitcast(aff, jnp.int32)
    bt = [bits[:, s] for s in tiles]

    def lane_total(x):
        return jnp.broadcast_to(jnp.sum(x, axis=-1, keepdims=True), (e, LANES))

    def search(i, thr):
        cand = thr | jnp.left_shift(jnp.int32(1), 30 - i)
        acc = jnp.zeros((e, LANES), jnp.int32)
        for x in bt:
            acc = acc + (x >= cand).astype(jnp.int32)
        return jnp.where(lane_total(acc) >= cap, cand, thr)

    thr = lax.fori_loop(0, 31, search, jnp.zeros((e, LANES), jnp.int32))
    gt = [x > thr for x in bt]
    eq = [x == thr for x in bt]
    acc = jnp.zeros((e, LANES), jnp.int32)
    for x in gt:
        acc = acc + x.astype(jnp.int32)
    need = (cap - lane_total(acc)).astype(f32)

    pre = pre_ref[...]

    def prefix(flags):
        outs = [jnp.dot(jnp.where(x, 1.0, 0.0).astype(bf16), pre, preferred_element_type=f32) for x in flags]
        carry = jnp.zeros((e, LANES), f32)
        res = []
        for o in outs:
            res.append(o[:, :LANES] + carry)
            carry = carry + o[:, LANES:]
        return res, [o[:, LANES:] for o in outs]

    rank_eq, _ = prefix(eq)
    sel = [jnp.logical_or(gt[j], jnp.logical_and(eq[j], rank_eq[j] < need)) for j in range(nl)]
    pos, totals = prefix(sel)
    lane = lax.broadcasted_iota(jnp.int32, (e, LANES), 1)
    cnt = jnp.zeros((e, LANES), f32)
    per = tt // LANES
    for j in range(nl):
        smap_ref[:, tiles[j]] = jnp.where(sel[j], pos[j], -1.0)
        gate_ref[:, tiles[j]] = jnp.where(sel[j], aff[:, tiles[j]], 0.0)
        cnt = cnt + jnp.where(lane == j // per, totals[j], 0.0)
    cnt_ref[...] = cnt


def moe_select(aff, cap, *, tt=MOE_TT):
    bsz, e, l = aff.shape
    assert l // tt <= LANES
    i = np.arange(LANES)
    pre = np.concatenate([i[:, None] < i[None, :], np.ones((LANES, LANES), bool)], axis=1)
    row = pl.BlockSpec((None, e, l), lambda b: (b, 0, 0))
    return pl.pallas_call(
        functools.partial(_select_kernel, cap=cap, tt=tt),
        grid=(bsz,),
        in_specs=[row, pl.BlockSpec((LANES, 2 * LANES), lambda b: (0, 0))],
        out_specs=[row, row, pl.BlockSpec((None, e, LANES), lambda b: (b, 0, 0))],
        out_shape=[jax.ShapeDtypeStruct((bsz, e, l), jnp.float32), jax.ShapeDtypeStruct((bsz, e, l), jnp.float32),
                   jax.ShapeDtypeStruct((bsz, e, LANES), jnp.float32)],
        compiler_params=pltpu.CompilerParams(dimension_semantics=("parallel",),
                                             vmem_limit_bytes=VMEM_LIMIT_BYTES),
        name="moe_select",
    )(aff, jnp.asarray(pre, jnp.bfloat16))


def _slot_one_hot(pos, base, rows, n):
    slot = (base + lax.broadcasted_iota(jnp.int32, (rows, n), 0)).astype(jnp.float32)
    return jnp.where(pos == slot, 1.0, 0.0).astype(jnp.bfloat16)


def _expert_kernel(cs_ref, h_ref, smap_ref, gate_ref, wg32_ref, wu32_ref, wd32_ref, o_ref,
                   xs_sc, gs_sc, wg_ref, wu_ref, wd_ref, *, nj, per, tt, cap):
    e = pl.program_id(0)
    b = pl.program_id(1)
    j = pl.program_id(2)
    f32 = jnp.float32
    bf16 = jnp.bfloat16

    @pl.when(jnp.logical_and(b == 0, j == 0))
    def _():
        wg_ref[...] = wg32_ref[...].astype(bf16)
        wu_ref[...] = wu32_ref[...].astype(bf16)
        wd_ref[...] = wd32_ref[...].astype(bf16)

    @pl.when(j == 0)
    def _():
        xs_sc[...] = jnp.zeros_like(xs_sc)
        gs_sc[...] = jnp.zeros_like(gs_sc)

    win = min(MOE_SUB, cap)
    for s in range(per):
        base = (b * N_EXPERTS + e) * (nj * per + 1) + j * per + s
        c0 = cs_ref[base]
        c1 = cs_ref[base + 1]
        cols = slice(s * tt, (s + 1) * tt)
        pos = smap_ref[pl.ds(e % SUBLANES, 1), cols]
        gate = gate_ref[pl.ds(e % SUBLANES, 1), cols]
        hb = h_ref[cols, :]

        def gather(r0, rows, pos=pos, gate=gate, hb=hb):
            oh = _slot_one_hot(pos, r0, rows, tt)
            xs_sc[pl.ds(r0, rows), :] += jnp.dot(oh, hb, preferred_element_type=f32)
            g = jnp.sum(oh.astype(f32) * gate, axis=-1, keepdims=True)
            gs_sc[pl.ds(r0, rows), :] += jnp.broadcast_to(g, (rows, LANES))

        w0 = pl.multiple_of(jnp.minimum(c0 // MOE_ALIGN * MOE_ALIGN, cap - win), MOE_ALIGN)
        gather(w0, win)

        def rest(st, carry, gather=gather):
            gather(pl.multiple_of(st * MOE_ALIGN, MOE_ALIGN), MOE_ALIGN)
            return carry

        lax.fori_loop((w0 + win) // MOE_ALIGN, (c1 + MOE_ALIGN - 1) // MOE_ALIGN, rest, 0)

    @pl.when(j == nj - 1)
    def _():
        rows_per = min(MOE_ROWS, cap)
        for r in range(cap // rows_per):
            rows = slice(r * rows_per, (r + 1) * rows_per)
            xb = xs_sc[rows, :].astype(bf16)
            g = jnp.dot(xb, wg_ref[...], preferred_element_type=f32)
            u = jnp.dot(xb, wu_ref[...], preferred_element_type=f32)
            hid = (g * jax.nn.sigmoid(g) * u).astype(bf16)
            out = jnp.dot(hid, wd_ref[...], preferred_element_type=f32)
            scale = jnp.concatenate([gs_sc[rows, :]] * (out.shape[1] // LANES), axis=1)
            o_ref[rows, :] = (out * scale).astype(bf16)


def moe_experts(hb, smap, gate, cs, w_gate, w_up, w_down, layer, cap, *, tt=MOE_TT):
    bsz, l, d = hb.shape
    _, e, _, ff = w_gate.shape
    per = MOE_TILES_PER_STEP if (l // tt) % MOE_TILES_PER_STEP == 0 else 1
    nj = l // (tt * per)
    tok = pl.BlockSpec((None, SUBLANES, per * tt), lambda ei, b, j, cs_ref: (b, ei // SUBLANES, j))
    once = pl.Buffered(1)
    grid_spec = pltpu.PrefetchScalarGridSpec(
        num_scalar_prefetch=1,
        grid=(e, bsz, nj),
        in_specs=[pl.BlockSpec((None, per * tt, d), lambda ei, b, j, cs_ref: (b, j, 0)), tok, tok,
                  pl.BlockSpec((None, None, d, ff), lambda ei, b, j, cs_ref: (layer, ei, 0, 0), pipeline_mode=once),
                  pl.BlockSpec((None, None, d, ff), lambda ei, b, j, cs_ref: (layer, ei, 0, 0), pipeline_mode=once),
                  pl.BlockSpec((None, None, ff, d), lambda ei, b, j, cs_ref: (layer, ei, 0, 0), pipeline_mode=once)],
        out_specs=pl.BlockSpec((None, None, cap, d), lambda ei, b, j, cs_ref: (b, ei, 0, 0)),
        scratch_shapes=[pltpu.VMEM((cap, d), jnp.float32), pltpu.VMEM((cap, LANES), jnp.float32),
                        pltpu.VMEM((d, ff), jnp.bfloat16), pltpu.VMEM((d, ff), jnp.bfloat16),
                        pltpu.VMEM((ff, d), jnp.bfloat16)])
    return pl.pallas_call(
        functools.partial(_expert_kernel, nj=nj, per=per, tt=tt, cap=cap),
        grid_spec=grid_spec,
        out_shape=jax.ShapeDtypeStruct((bsz, e, cap, d), jnp.bfloat16),
        compiler_params=pltpu.CompilerParams(dimension_semantics=("parallel", "arbitrary", "arbitrary"),
                                             vmem_limit_bytes=MOE_EXPERT_VMEM_LIMIT_BYTES),
        name="moe_experts",
    )(cs, hb, smap, gate, w_gate, w_up, w_down)


def _combine_kernel(cs_ref, x_ref, smap_ref, ow_ref, y_ref, *, nj, tt):
    b = pl.program_id(0)
    eg = pl.program_id(2)

    @pl.when(eg == 0)
    def _():
        y_ref[...] = x_ref[...]

    contract_first = (((0,), (0,)), ((), ()))
    per, cap = ow_ref.shape[0:2]
    win = min(2 * MOE_SUB, cap)
    cols = [slice(j * tt, (j + 1) * tt) for j in range(nj)]
    for k in range(per):
        e = eg * per + k
        base = (b * N_EXPERTS + e) * (nj + 1)
        pos = [smap_ref[pl.ds(e % SUBLANES, 1), c] for c in cols]
        r0 = [pl.multiple_of(jnp.minimum(cs_ref[base + j] // MOE_SUB * MOE_SUB, cap - win), MOE_SUB)
              for j in range(nj)]
        oh = [_slot_one_hot(pos[j], r0[j], win, tt) for j in range(nj)]
        add = [lax.dot_general(oh[j], ow_ref[k, pl.ds(r0[j], win), :], contract_first,
                               preferred_element_type=jnp.float32) for j in range(nj)]
        for j in range(nj):
            y_ref[cols[j], :] += add[j]

        for j in range(nj):
            def scatter(st, carry, j=j, k=k, pos=pos):
                s0 = pl.multiple_of(st * MOE_SUB, MOE_SUB)
                y_ref[cols[j], :] += lax.dot_general(_slot_one_hot(pos[j], s0, MOE_SUB, tt),
                                                     ow_ref[k, pl.ds(s0, MOE_SUB), :], contract_first,
                                                     preferred_element_type=jnp.float32)
                return carry

            lax.fori_loop((r0[j] + win) // MOE_SUB, (cs_ref[base + j + 1] + MOE_SUB - 1) // MOE_SUB, scatter, 0)


def moe_combine(x, smap, outw, cs, *, tt=MOE_TT):
    bsz, l, d = x.shape
    e, cap = outw.shape[1:3]
    nj = l // tt
    per = MOE_EXPERTS_PER_COMBINE_STEP
    assert e % per == 0 and SUBLANES % per == 0
    grid_spec = pltpu.PrefetchScalarGridSpec(
        num_scalar_prefetch=1,
        grid=(bsz, d // MOE_SLAB, e // per),
        in_specs=[pl.BlockSpec((None, l, MOE_SLAB), lambda b, s, eg, cs_ref: (b, 0, s)),
                  pl.BlockSpec((None, SUBLANES, l), lambda b, s, eg, cs_ref: (b, eg * per // SUBLANES, 0)),
                  pl.BlockSpec((None, per, cap, MOE_SLAB), lambda b, s, eg, cs_ref: (b, eg, 0, s))],
        out_specs=pl.BlockSpec((None, l, MOE_SLAB), lambda b, s, eg, cs_ref: (b, 0, s)))
    return pl.pallas_call(
        functools.partial(_combine_kernel, nj=nj, tt=tt),
        grid_spec=grid_spec,
        out_shape=jax.ShapeDtypeStruct((bsz, l, d), jnp.float32),
        compiler_params=pltpu.CompilerParams(dimension_semantics=("parallel", "parallel", "arbitrary"),
                                             vmem_limit_bytes=MOE_VMEM_LIMIT_BYTES),
        name="moe_combine",
    )(cs, x, smap, outw)


def ec_moe_layer(x, gain, w_router, w_gate, w_up, w_down, layer):
    bsz, l, d = x.shape
    cap = EC_CAPACITY_FACTOR * l // N_EXPERTS
    tt = min(MOE_TT, l)
    nj = l // tt
    hb, aff = moe_route(x, gain, w_router, tm=min(2 * tt, l))
    smap, gate, cnt = moe_select(aff, cap, tt=tt)
    cs = jnp.concatenate([jnp.zeros((bsz, N_EXPERTS, 1), jnp.float32), jnp.cumsum(cnt[..., :nj], axis=-1)], axis=-1)
    cs = cs.astype(jnp.int32).reshape(-1)
    outw = moe_experts(hb, smap, gate, cs, w_gate, w_up, w_down, layer, cap, tt=tt)
    return moe_combine(x, smap, outw, cs, tt=tt)


def kernel(x, mix_norm, ffn_norm, ev_w_in, ev_w_out, a_lb_logits, a_out_norm, s5_lambda_re, s5_lambda_im, s5_log_step, s5_b_re, s5_b_im, s5_c_re, s5_c_im, s5_d, s5_glu_w, s5_glu_b, od_w_in, od_w_out, c_q_norm, c_k_norm, c_lambda, c_out_norm, rel_bias, d_conv_w, d_a_log, d_dt_bias, d_out_norm, moe_router, moe_w_gate, moe_w_up, moe_w_down):
    bsz, l, d = x.shape
    p = jax.nn.softmax(a_lb_logits.astype(jnp.float32), axis=0)
    cum = jnp.cumsum(p, axis=0)
    lower_bounds = cum - cum[0:1]
    bias5 = rel_bias_tiles(rel_bias, ATT_T)
    for layer in range(DEPTH):
        j = layer // 2
        if layer % 2 == 0:
            proj, u_tb = norm_matmul(x, mix_norm[layer], ev_w_in[j], tail=B_WIDTH)
            o_a2 = hgrn2_scan(proj, lower_bounds[j])
            o_b = s5_mixer_tb(u_tb.reshape(l * bsz, B_WIDTH), bsz, s5_lambda_re[j], s5_lambda_im[j], s5_log_step[j],
                              s5_b_re[j], s5_b_im[j], s5_c_re[j], s5_c_im[j], s5_d[j], s5_glu_w[j], s5_glu_b[j])
            x = mixer_out_proj(o_a2, proj, 4 * A_HEADS, a_out_norm[j], o_b.reshape(l, bsz * B_WIDTH), ev_w_out[j], x,
                               bidir_first=True, other_time_major=True)
        else:
            o2 = 3 * C_WIDTH + 3 * D_WIDTH
            o4 = o2 + 4 * D_HEADS
            w = od_w_in[j]
            w_in = jnp.concatenate([w[:, :o2], w[:, o4:], w[:, o2:o4],
                                    jnp.zeros((d, 3 * C_WIDTH + OD_COLS - w.shape[1]), w.dtype)], axis=1)
            proj, q2, kt, vb, stats = odd_in_proj(x, mix_norm[layer], w_in, c_q_norm[j], c_k_norm[j])
            o_c = diff_attention(q2, kt, vb, stats, c_lambda[j], c_out_norm[j], bias5, layer)
            o_d2 = gated_deltanet(proj, d_conv_w[j], d_a_log[j], d_dt_bias[j])
            x = mixer_out_proj(o_d2, proj, OD_GATE_BLOCK, d_out_norm[j], o_c, od_w_out[j], x, bidir_first=False)
        x = ec_moe_layer(x, ffn_norm[layer], moe_router[layer], moe_w_gate, moe_w_up, moe_w_down, layer)
    return x
```
